```python
import math
import jax, jax.numpy as jnp
from jax import lax
import numpy as np

D_MODEL = 1024
BATCH = 16
SEQ = 256
DEPTH = 1
DEC_BATCH = 2
DEC_SEQ = 2048
PAST_LEN = 256

GRID_W = 64
MIX_WIDTH = D_MODEL
ATT_WIDTH = MIX_WIDTH // 2
SSM_WIDTH = MIX_WIDTH - ATT_WIDTH
HEAD_DIM = 64
N_ATT_HEADS = ATT_WIDTH // (2 * HEAD_DIM)
SSM_GROUP = 16
N_SSM_GROUPS = SSM_WIDTH // SSM_GROUP
SSM_STATE = 64
IN_WIDTH = 3 * ATT_WIDTH + SSM_WIDTH
D_FF = -(-8 * D_MODEL // (3 * 256)) * 256
N_MOD = 6
Q_BLOCK = 128
ROPE_BASE = 10000.0
NORM_EPS = 1e-6

kernel_name = "hybrid_diffattn_s5_prefix_dit_step"


def rmsnorm(x, g):
    xf = x.astype(jnp.float32)
    y = xf * lax.rsqrt(jnp.mean(xf * xf, axis=-1, keepdims=True) + NORM_EPS)
    return (y * g.astype(jnp.float32)).astype(x.dtype)


def modulation(cond, w_mod, b_mod, dtype):
    m = jax.nn.silu(cond.astype(jnp.float32)) @ w_mod.astype(jnp.float32) + b_mod.astype(jnp.float32)
    return [t.astype(dtype) for t in jnp.split(m[:, None, :], N_MOD, axis=-1)]


def axial_rope_tables(n_rows):
    t = jnp.arange(n_rows * GRID_W)
    row = (t // GRID_W).astype(jnp.float32)
    col = (t % GRID_W).astype(jnp.float32)
    half = HEAD_DIM // 2
    inv_freq = ROPE_BASE ** (-jnp.arange(0, half, 2, dtype=jnp.float32) / half)
    ang_r = row[:, None] * inv_freq
    ang_c = col[:, None] * inv_freq
    ang = jnp.concatenate([ang_r, ang_r, ang_c, ang_c], axis=-1)
    return jnp.cos(ang), jnp.sin(ang)


def rotate_half_axial(x):
    x1, x2, x3, x4 = jnp.split(x, 4, axis=-1)
    return jnp.concatenate([-x2, x1, -x4, x3], axis=-1)


def apply_rope(x, cos, sin):
    c = cos[None, :, None, None, :]
    s = sin[None, :, None, None, :]
    xf = x.astype(jnp.float32)
    return (xf * c + rotate_half_axial(xf) * s).astype(x.dtype)


def diff_attention(q, k, v, lam, subln_g, lam_init):
    b, lq = q.shape[:2]
    nb = lq // Q_BLOCK
    scale = HEAD_DIM ** -0.5
    qb = q.reshape(b, nb, Q_BLOCK, N_ATT_HEADS, 2, HEAD_DIM).transpose(1, 0, 2, 3, 4, 5)
    vf = v.astype(jnp.float32)

    def one_block(q_blk):
        s = jnp.einsum('bqhmd,bkhmd->bhmqk', q_blk, k,
                       preferred_element_type=jnp.float32) * scale
        p = jax.nn.softmax(s, axis=-1)
        w = p[:, :, 0] - lam * p[:, :, 1]
        return jnp.einsum('bhqk,bkhe->bqhe', w, vf)

    o = lax.map(one_block, qb)
    o = o.transpose(1, 0, 2, 3, 4).reshape(b, lq, N_ATT_HEADS, 2 * HEAD_DIM)
    o = rmsnorm(o, subln_g) * (1.0 - lam_init)
    return o.reshape(b, lq, ATT_WIDTH)


def _complex_affine_combine(e1, e2):
    a1r, a1i, b1r, b1i = e1
    a2r, a2i, b2r, b2i = e2
    return (a2r * a1r - a2i * a1i,
            a2r * a1i + a2i * a1r,
            a2r * b1r - a2i * b1i + b2r,
            a2r * b1i + a2i * b1r + b2i)


def s5_bidirectional(u, lam_re, lam_im, log_step, b_re, b_im, c_re, c_im, d_skip, h0_re, h0_im):
    bsz, L = u.shape[:2]
    uf = u.astype(jnp.float32).reshape(bsz, L, N_SSM_GROUPS, SSM_GROUP)
    ys, hr, hi = [], [], []
    for dr in range(2):
        lr = jnp.minimum(lam_re[dr].astype(jnp.float32), -1e-4)
        li = lam_im[dr].astype(jnp.float32)
        step = jnp.exp(log_step[dr].astype(jnp.float32))[:, None]
        mag = jnp.exp(lr * step)
        ab_re = mag * jnp.cos(li * step)
        ab_im = mag * jnp.sin(li * step)
        den = lr * lr + li * li
        nr = ab_re - 1.0
        f_re = (nr * lr + ab_im * li) / den
        f_im = (ab_im * lr - nr * li) / den
        br = b_re[dr].astype(jnp.float32)
        bi = b_im[dr].astype(jnp.float32)
        bb_re = f_re[:, :, None] * br - f_im[:, :, None] * bi
        bb_im = f_re[:, :, None] * bi + f_im[:, :, None] * br
        bu_re = jnp.einsum('blgc,gpc->blgp', uf, bb_re)
        bu_im = jnp.einsum('blgc,gpc->blgp', uf, bb_im)
        reverse = dr == 1
        edge = -1 if reverse else 0
        h0r = h0_re[:, dr].astype(jnp.float32)
        h0i = h0_im[:, dr].astype(jnp.float32)
        bu_re = bu_re.at[:, edge].add(ab_re * h0r - ab_im * h0i)
        bu_im = bu_im.at[:, edge].add(ab_re * h0i + ab_im * h0r)
        a_re = jnp.broadcast_to(ab_re, bu_re.shape)
        a_im = jnp.broadcast_to(ab_im, bu_im.shape)
        _, _, h_re, h_im = lax.associative_scan(
            _complex_affine_combine, (a_re, a_im, bu_re, bu_im), axis=1, reverse=reverse)
        y = (jnp.einsum('blgp,gcp->blgc', h_re, c_re[dr].astype(jnp.float32))
             - jnp.einsum('blgp,gcp->blgc', h_im, c_im[dr].astype(jnp.float32)))
        ys.append(y.reshape(bsz, L, SSM_WIDTH)
                  + d_skip[dr].astype(jnp.float32) * uf.reshape(bsz, L, SSM_WIDTH))
        last = 0 if reverse else -1
        hr.append(h_re[:, last])
        hi.append(h_im[:, last])
    return ys[0] + ys[1], jnp.stack(hr, axis=1), jnp.stack(hi, axis=1)


def layer_forward(x, cond, lp, lam_init, rope, ctx_k, ctx_v, h0_re, h0_im):
    shift1, scale1, gate1, shift2, scale2, gate2 = modulation(cond, lp['w_mod'], lp['b_mod'], x.dtype)
    g = lp['norm_g']
    bsz, L = x.shape[:2]
    h = rmsnorm(x, g[0]) * (1 + scale1) + shift1
    proj = h @ lp['w_in']
    q = proj[..., :ATT_WIDTH].reshape(bsz, L, N_ATT_HEADS, 2, HEAD_DIM)
    k = proj[..., ATT_WIDTH:2 * ATT_WIDTH].reshape(bsz, L, N_ATT_HEADS, 2, HEAD_DIM)
    v = proj[..., 2 * ATT_WIDTH:3 * ATT_WIDTH].reshape(bsz, L, N_ATT_HEADS, 2 * HEAD_DIM)
    u = proj[..., 3 * ATT_WIDTH:]
    keys, vals, q_in = k, v, q
    if rope is not None:
        q_in = apply_rope(q, *rope)
        keys = apply_rope(k, *rope)
    if ctx_k is not None:
        keys = jnp.concatenate([ctx_k.astype(keys.dtype), keys], axis=1)
        vals = jnp.concatenate([ctx_v.astype(vals.dtype), vals], axis=1)
    lp_lam = lp['lam'].astype(jnp.float32)
    lam = (jnp.exp(jnp.sum(lp_lam[0] * lp_lam[1])) - jnp.exp(jnp.sum(lp_lam[2] * lp_lam[3]))
           + lam_init)
    attn = diff_attention(q_in, keys, vals, lam, lp['subln_g'], lam_init)
    y_ssm, hf_re, hf_im = s5_bidirectional(
        u, lp['lam_re'], lp['lam_im'], lp['log_step'], lp['b_re'], lp['b_im'],
        lp['c_re'], lp['c_im'], lp['d_skip'], h0_re, h0_im)
    z = jax.nn.gelu(y_ssm)
    z = z * jax.nn.sigmoid(z @ lp['w_glu'].astype(jnp.float32) + lp['b_glu'].astype(jnp.float32))
    mix = jnp.concatenate([attn, z], axis=-1).astype(x.dtype) @ lp['w_o']
    x = x + gate1 * rmsnorm(mix, g[1])
    h = rmsnorm(x, g[2]) * (1 + scale2) + shift2
    gt, up = jnp.split(h @ lp['w_ffn_in'], 2, axis=-1)
    f = (jax.nn.silu(gt) * up) @ lp['w_ffn_out']
    x = x + gate2 * rmsnorm(f, g[3])
    return x, k, v, hf_re, hf_im


def setup_inputs(seed: int = 0) -> dict:
    key = jax.random.key(seed)
    ks = jax.random.split(key, 32)
    f32 = jnp.float32

    def nrm(k, shape, scale):
        return scale * jax.random.normal(k, shape, f32)

    G, P = N_SSM_GROUPS, SSM_STATE
    n = jnp.arange(P, dtype=f32)
    return {
        'x_prompt': nrm(ks[0], (BATCH, SEQ, D_MODEL), 1.0),
        'x_sample': nrm(ks[1], (DEC_BATCH, DEC_SEQ, D_MODEL), 1.0),
        'cache_k': nrm(ks[2], (DEC_BATCH, DEPTH, PAST_LEN, 2 * N_ATT_HEADS, HEAD_DIM), 1.0),
        'cache_v': nrm(ks[3], (DEC_BATCH, DEPTH, PAST_LEN, N_ATT_HEADS, 2 * HEAD_DIM), 1.0),
        'state_ssm_re': nrm(ks[4], (DEC_BATCH, DEPTH, 2, G, P), 0.1),
        'state_ssm_im': nrm(ks[5], (DEC_BATCH, DEPTH, 2, G, P), 0.1),
        'c': nrm(ks[6], (DEC_BATCH, D_MODEL), 1.0),
        'c_ctx': nrm(ks[7], (D_MODEL,), 1.0),
        'w_mod': nrm(ks[8], (DEPTH, D_MODEL, N_MOD * D_MODEL), 0.5 * D_MODEL ** -0.5),
        'b_mod': nrm(ks[9], (DEPTH, N_MOD * D_MODEL), 0.01),
        'norm_g': 1.0 + nrm(ks[10], (DEPTH, 4, D_MODEL), 0.01),
        'w_in': nrm(ks[11], (DEPTH, D_MODEL, IN_WIDTH), D_MODEL ** -0.5),
        'lam_params': nrm(ks[12], (DEPTH, 4, HEAD_DIM), 0.1),
        'subln_g': 1.0 + nrm(ks[13], (DEPTH, 2 * HEAD_DIM), 0.01),
        'ssm_lambda_re': -0.5 + nrm(ks[14], (DEPTH, 2, G, P), 0.01),
        'ssm_lambda_im': math.pi * n + nrm(ks[15], (DEPTH, 2, G, P), 0.01),
        'ssm_log_step': jax.random.uniform(ks[16], (DEPTH, 2, G), f32,
                                           math.log(1e-3), math.log(1e-1)),
        'ssm_b_re': nrm(ks[17], (DEPTH, 2, G, P, SSM_GROUP), (2 * SSM_GROUP) ** -0.5),
        'ssm_b_im': nrm(ks[18], (DEPTH, 2, G, P, SSM_GROUP), (2 * SSM_GROUP) ** -0.5),
        'ssm_c_re': nrm(ks[19], (DEPTH, 2, G, SSM_GROUP, P), (2 * P) ** -0.5),
        'ssm_c_im': nrm(ks[20], (DEPTH, 2, G, SSM_GROUP, P), (2 * P) ** -0.5),
        'ssm_d': nrm(ks[21], (DEPTH, 2, SSM_WIDTH), 0.5),
        'w_glu': nrm(ks[22], (DEPTH, SSM_WIDTH, SSM_WIDTH), SSM_WIDTH ** -0.5),
        'b_glu': nrm(ks[23], (DEPTH, SSM_WIDTH), 0.01),
        'w_o': nrm(ks[24], (DEPTH, MIX_WIDTH, D_MODEL), MIX_WIDTH ** -0.5),
        'w_ffn_in': nrm(ks[25], (DEPTH, D_MODEL, 2 * D_FF), D_MODEL ** -0.5),
        'w_ffn_out': nrm(ks[26], (DEPTH, D_FF, D_MODEL), D_FF ** -0.5),
    }


def reference(x_prompt, x_sample, cache_k, cache_v, state_ssm_re, state_ssm_im, c, c_ctx,
              w_mod, b_mod, norm_g, w_in, lam_params, subln_g,
              ssm_lambda_re, ssm_lambda_im, ssm_log_step, ssm_b_re, ssm_b_im,
              ssm_c_re, ssm_c_im, ssm_d, w_glu, b_glu, w_o, w_ffn_in, w_ffn_out):
    n_rows = x_sample.shape[1] // GRID_W
    rope = axial_rope_tables(n_rows)
    bp, lp_len = x_prompt.shape[:2]
    bd, past = cache_k.shape[0], cache_k.shape[2]
    zeros_h = jnp.zeros((bp, 2, N_SSM_GROUPS, SSM_STATE), jnp.float32)
    xp, xs = x_prompt, x_sample
    ks_out, vs_out, hr_out, hi_out = [], [], [], []
    for l in range(DEPTH):
        lam_init = 0.8 - 0.6 * math.exp(-0.3 * l)
        lp = {
            'w_mod': w_mod[l], 'b_mod': b_mod[l], 'norm_g': norm_g[l], 'w_in': w_in[l],
            'lam': lam_params[l], 'subln_g': subln_g[l],
            'lam_re': ssm_lambda_re[l], 'lam_im': ssm_lambda_im[l], 'log_step': ssm_log_step[l],
            'b_re': ssm_b_re[l], 'b_im': ssm_b_im[l], 'c_re': ssm_c_re[l], 'c_im': ssm_c_im[l],
            'd_skip': ssm_d[l], 'w_glu': w_glu[l], 'b_glu': b_glu[l], 'w_o': w_o[l],
            'w_ffn_in': w_ffn_in[l], 'w_ffn_out': w_ffn_out[l],
        }
        xp, k_ctx, v_ctx, h_re, h_im = layer_forward(
            xp, c_ctx[None, :], lp, lam_init, None, None, None, zeros_h, zeros_h)
        ks_out.append(k_ctx.reshape(bp, lp_len, 2 * N_ATT_HEADS, HEAD_DIM))
        vs_out.append(v_ctx)
        hr_out.append(h_re)
        hi_out.append(h_im)
        ck = cache_k[:, l].reshape(bd, past, N_ATT_HEADS, 2, HEAD_DIM)
        cv = cache_v[:, l]
        xs, _, _, _, _ = layer_forward(
            xs, c, lp, lam_init, rope, ck, cv, state_ssm_re[:, l], state_ssm_im[:, l])
    new_cache_k = jnp.stack(ks_out, axis=1)
    new_cache_v = jnp.stack(vs_out, axis=1)
    new_state_ssm_re = jnp.stack(hr_out, axis=1)
    new_state_ssm_im = jnp.stack(hi_out, axis=1)
    return (xp, xs, new_cache_k, new_cache_v, new_state_ssm_re, new_state_ssm_im)
```

```python
import functools
import math

import jax
import jax.numpy as jnp
from jax import lax
from jax.experimental import pallas as pl
from jax.experimental.pallas import tpu as pltpu

F32 = jnp.float32
BF16 = jnp.bfloat16

D_MODEL = 1024
GRID_W = 64
ATT_WIDTH = 512
SSM_WIDTH = 512
HEAD_DIM = 64
N_HEADS = 4
HEAD_W = 2 * HEAD_DIM
SSM_GROUP = 16
N_GROUPS = 32
SSM_STATE = 64
D_FF = 2816
N_MOD = 6
ROPE_BASE = 10000.0
NORM_EPS = 1e-6

CHUNK = 16
CW = CHUNK * SSM_GROUP
LANES = 128
N_LEVELS = 8
GROUP_BLOCK = 4

VMEM_LIMIT = 56 * 1024 * 1024


def _cparams(n_axes):
    return pltpu.CompilerParams(
        dimension_semantics=("arbitrary",) * n_axes,
        vmem_limit_bytes=VMEM_LIMIT)


def _rms(x, g):
    ms = jnp.mean(x * x, axis=-1, keepdims=True)
    return x * lax.rsqrt(ms + NORM_EPS) * g


def _silu(x):
    return x * jax.nn.sigmoid(x)


def _mod_kernel(cond_ref, w_ref, b_ref, o_ref):
    s = _silu(cond_ref[...]).astype(BF16)
    o_ref[...] = jnp.dot(s, w_ref[...].astype(BF16),
                         preferred_element_type=F32) + b_ref[...]


def _modulation(cond, w_mod, b_mod):
    n = w_mod.shape[1]
    tn = 1536
    return pl.pallas_call(
        _mod_kernel,
        grid=(n // tn,),
        in_specs=[pl.BlockSpec((8, D_MODEL), lambda i: (0, 0)),
                  pl.BlockSpec((D_MODEL, tn), lambda i: (0, i)),
                  pl.BlockSpec((1, tn), lambda i: (0, i))],
        out_specs=pl.BlockSpec((8, tn), lambda i: (0, i)),
        out_shape=jax.ShapeDtypeStruct((8, n), F32),
        compiler_params=_cparams(1),
        name="modulation",
    )(cond, w_mod, b_mod.reshape(1, n))


def _rope(x, cos, sa, sb):
    return (x * cos + pltpu.roll(x, HEAD_W - 16, axis=1) * sa
            + pltpu.roll(x, 16, axis=1) * sb)


def _qkv_kernel(*refs, rope):
    if rope:
        x_ref, mod_ref, g_ref, w_ref, cos_ref, sa_ref, sb_ref, q_ref, k_ref, v_ref = refs
    else:
        x_ref, mod_ref, g_ref, w_ref, q_ref, k_ref, v_ref = refs
    shift = mod_ref[:, 0:D_MODEL]
    scale = mod_ref[:, D_MODEL:2 * D_MODEL]
    h = (_rms(x_ref[...], g_ref[...]) * (1.0 + scale) + shift).astype(BF16)
    proj = jnp.dot(h, w_ref[...], preferred_element_type=F32)
    q = proj[:, 0:ATT_WIDTH]
    k = proj[:, ATT_WIDTH:2 * ATT_WIDTH]
    v = proj[:, 2 * ATT_WIDTH:3 * ATT_WIDTH]
    qscale = HEAD_DIM ** -0.5
    if rope:
        cos, sa, sb = cos_ref[...], sa_ref[...], sb_ref[...]
        for hd in range(N_HEADS):
            sl = slice(hd * HEAD_W, (hd + 1) * HEAD_W)
            q_ref[:, sl] = (_rope(q[:, sl], cos, sa, sb) * qscale).astype(q_ref.dtype)
            k_ref[:, sl] = _rope(k[:, sl], cos, sa, sb).astype(k_ref.dtype)
    else:
        q_ref[...] = (q * qscale).astype(q_ref.dtype)
        k_ref[...] = k.astype(k_ref.dtype)
    v_ref[...] = v.astype(v_ref.dtype)


def _qkv_proj(x2d, mod3, g0, w_qkv, rope_tabs, seq_len, kv_dtype):
    n_tok = x2d.shape[0]
    tm = 512
    n_tiles = n_tok // tm
    tiles_per_mod = n_tiles // mod3.shape[0]
    tiles_per_seq = seq_len // tm if seq_len >= tm else 1
    rope = rope_tabs is not None
    in_specs = [pl.BlockSpec((tm, D_MODEL), lambda i: (i, 0)),
                pl.BlockSpec((None, 1, 2 * D_MODEL), lambda i: (i // tiles_per_mod, 0, 0)),
                pl.BlockSpec((1, D_MODEL), lambda i: (0, 0)),
                pl.BlockSpec((D_MODEL, 3 * ATT_WIDTH), lambda i: (0, 0))]
    args = [x2d, mod3, g0, w_qkv]
    if rope:
        for tab in rope_tabs:
            in_specs.append(pl.BlockSpec((tm, HEAD_W), lambda i: (i % tiles_per_seq, 0)))
            args.append(tab)
    out_spec = pl.BlockSpec((tm, ATT_WIDTH), lambda i: (i, 0))
    return pl.pallas_call(
        functools.partial(_qkv_kernel, rope=rope),
        grid=(n_tiles,),
        in_specs=in_specs,
        out_specs=[out_spec, out_spec, out_spec],
        out_shape=[jax.ShapeDtypeStruct((n_tok, ATT_WIDTH), BF16),
                   jax.ShapeDtypeStruct((n_tok, ATT_WIDTH), kv_dtype),
                   jax.ShapeDtypeStruct((n_tok, ATT_WIDTH), kv_dtype)],
        compiler_params=_cparams(1),
        name="qkv_proj",
    )(*args)


def _ut_kernel(x_ref, mod_ref, g_ref, w_ref, o_ref):
    shift = mod_ref[:, 0:D_MODEL]
    scale = mod_ref[:, D_MODEL:2 * D_MODEL]
    h = (_rms(x_ref[...], g_ref[...]) * (1.0 + scale) + shift).astype(BF16)
    ut = lax.dot_general(w_ref[...], h, (((1,), (1,)), ((), ())),
                         preferred_element_type=F32)
    o_ref[...] = ut.reshape(N_GROUPS, SSM_GROUP, ut.shape[1]).astype(o_ref.dtype)


def _ut_proj(x2d, mod3, g0, w_ut):
    n_tok = x2d.shape[0]
    n_rows = n_tok // CHUNK
    rb = min(256, n_rows // mod3.shape[0])
    x_rows = x2d.reshape(n_rows, CHUNK * D_MODEL)
    n_rb = n_rows // rb
    blocks_per_mod = (n_rows // mod3.shape[0]) // rb
    return pl.pallas_call(
        _ut_kernel,
        grid=(n_rb, CHUNK),
        in_specs=[pl.BlockSpec((rb, D_MODEL), lambda i, t: (i, t)),
                  pl.BlockSpec((None, 1, 2 * D_MODEL), lambda i, t: (i // blocks_per_mod, 0, 0)),
                  pl.BlockSpec((1, D_MODEL), lambda i, t: (0, 0)),
                  pl.BlockSpec((SSM_WIDTH, D_MODEL), lambda i, t: (0, 0))],
        out_specs=pl.BlockSpec((N_GROUPS, None, SSM_GROUP, rb), lambda i, t: (0, t, 0, i)),
        out_shape=jax.ShapeDtypeStruct((N_GROUPS, CHUNK, SSM_GROUP, n_rows), BF16),
        compiler_params=_cparams(2),
        name="ssm_in_proj",
    )(x_rows, mod3, g0, w_ut)


def _attn_kernel(*refs, has_ctx, lam_init):
    if has_ctx:
        lam_ref, sg_ref, q_ref, ck_ref, cv_ref, k_ref, v_ref, o_ref = refs
    else:
        lam_ref, sg_ref, q_ref, k_ref, v_ref, o_ref = refs
    lp = lam_ref[...]
    lam = (jnp.exp(jnp.sum(lp[0:1] * lp[1:2], axis=-1, keepdims=True))
           - jnp.exp(jnp.sum(lp[2:3] * lp[3:4], axis=-1, keepdims=True)) + lam_init)
    tq = q_ref.shape[0]
    first_map = lax.broadcasted_iota(jnp.int32, (1, HEAD_W), 1) < HEAD_DIM
    nt = (((1,), (1,)), ((), ()))
    for hd in range(N_HEADS):
        sl = slice(hd * HEAD_W, (hd + 1) * HEAD_W)
        qh = q_ref[:, sl]
        zero = jnp.zeros_like(qh)
        qs = jnp.concatenate([jnp.where(first_map, qh, zero),
                              jnp.where(first_map, zero, qh)], axis=0)
        parts = [(k_ref[:, sl].astype(BF16), v_ref[:, sl].astype(BF16))]
        if has_ctx:
            parts.insert(0, (ck_ref[:, sl].astype(BF16), cv_ref[:, sl].astype(BF16)))
        scores = [lax.dot_general(qs, kk, nt, preferred_element_type=F32) for kk, _ in parts]
        mx = scores[0].max(axis=-1, keepdims=True)
        for s in scores[1:]:
            mx = jnp.maximum(mx, s.max(axis=-1, keepdims=True))
        es = [jnp.exp(s - mx) for s in scores]
        den = es[0].sum(axis=-1, keepdims=True)
        for e in es[1:]:
            den = den + e.sum(axis=-1, keepdims=True)
        inv = 1.0 / den
        inv0 = inv[0:tq]
        inv1 = inv[tq:2 * tq] * lam
        o = None
        for e, (_, vv) in zip(es, parts):
            w = (e[0:tq] * inv0 - e[tq:2 * tq] * inv1).astype(BF16)
            pv = jnp.dot(w, vv, preferred_element_type=F32)
            o = pv if o is None else o + pv
        o = _rms(o, sg_ref[...]) * (1.0 - lam_init)
        o_ref[:, sl] = o.astype(o_ref.dtype)


def _attention(q, k, v, ctx_k, ctx_v, lam_params, subln_g, n_batch, seq_len, lam_init):
    has_ctx = ctx_k is not None
    tq = 256
    n_q = seq_len // tq
    in_specs = [pl.BlockSpec((4, HEAD_DIM), lambda b, i: (0, 0)),
                pl.BlockSpec((1, HEAD_W), lambda b, i: (0, 0)),
                pl.BlockSpec((tq, ATT_WIDTH), lambda b, i: (b * n_q + i, 0))]
    args = [lam_params, subln_g.reshape(1, HEAD_W), q]
    if has_ctx:
        past = ctx_k.shape[1]
        ctx_spec = pl.BlockSpec((None, past, ATT_WIDTH), lambda b, i: (b, 0, 0))
        in_specs += [ctx_spec, ctx_spec]
        args += [ctx_k, ctx_v]
    kv_spec = pl.BlockSpec((seq_len, ATT_WIDTH), lambda b, i: (b, 0))
    in_specs += [kv_spec, kv_spec]
    args += [k, v]
    return pl.pallas_call(
        functools.partial(_attn_kernel, has_ctx=has_ctx, lam_init=lam_init),
        grid=(n_batch, n_q),
        in_specs=in_specs,
        out_specs=pl.BlockSpec((tq, ATT_WIDTH), lambda b, i: (b * n_q + i, 0)),
        out_shape=jax.ShapeDtypeStruct((n_batch * seq_len, ATT_WIDTH), BF16),
        compiler_params=_cparams(2),
        name="diff_attention",
    )(*args)


def _cmul(ar, ai, br, bi):
    return ar * br - ai * bi, ar * bi + ai * br


def _ssm_prep_kernel(lre_ref, lim_ref, ls_ref, bre_ref, bim_ref, cre_ref, cim_ref,
                     mt_ref, gt_ref, wo_ref, at_ref):
    lane = lax.broadcasted_iota(jnp.int32, (SSM_GROUP, CW), 1)
    nt = (((1,), (1,)), ((), ()))
    for gi in range(GROUP_BLOCK):
        gt_cols, wo_cols, at_cols, toeplitz = [], [], [], []
        for dr in range(2):
            lr = jnp.minimum(lre_ref[dr, gi], -1e-4)
            li = lim_ref[dr, gi]
            step = jnp.exp(ls_ref[dr, gi])
            mag = jnp.exp(lr * step)
            a_re = mag * jnp.cos(li * step)
            a_im = mag * jnp.sin(li * step)
            den = lr * lr + li * li
            nr = a_re - 1.0
            f_re = (nr * lr + a_im * li) / den
            f_im = (a_im * lr - nr * li) / den
            bt_re, bt_im = bre_ref[dr, gi], bim_ref[dr, gi]
            bb_re, bb_im = _cmul(f_re, f_im, bt_re, bt_im)
            c_re, c_im = cre_ref[dr, gi], cim_ref[dr, gi]
            pw = [(jnp.ones_like(a_re), jnp.zeros_like(a_im))]
            for _ in range(CHUNK):
                pw.append(_cmul(pw[-1][0], pw[-1][1], a_re, a_im))
            g_re, g_im, e_re, e_im = [], [], [], []
            for t in range(CHUNK):
                pr, pi = pw[CHUNK - 1 - t] if dr == 0 else pw[t]
                r, i = _cmul(bb_re, bb_im, pr, pi)
                g_re.append(r)
                g_im.append(i)
                pr, pi = pw[t + 1] if dr == 0 else pw[CHUNK - t]
                r, i = _cmul(c_re, c_im, pr, pi)
                e_re.append(r)
                e_im.append(-i)
            g_cat = jnp.concatenate([jnp.concatenate(g_re, axis=0),
                                     jnp.concatenate(g_im, axis=0)], axis=1)
            gt_cols.append(g_cat)
            wo_cols.append(jnp.concatenate([jnp.concatenate(e_re, axis=0),
                                            jnp.concatenate(e_im, axis=0)], axis=1))
            c_cat = jnp.concatenate([c_re, -c_im], axis=1)
            toeplitz.append(lax.dot_general(c_cat, g_cat, nt, precision=lax.Precision.HIGHEST,
                                            preferred_element_type=F32))
            pr, pi = pw[CHUNK]
            rows_p, rows_q = [], []
            for _ in range(N_LEVELS - 1):
                rows_p.append(jnp.concatenate([pr, pr], axis=1))
                rows_q.append(jnp.concatenate([-pi, pi], axis=1))
                pr, pi = _cmul(pr, pi, pr, pi)
            rows_p.append(jnp.zeros_like(rows_p[0]))
            rows_q.append(jnp.zeros_like(rows_q[0]))
            at_cols += [jnp.concatenate(rows_p, axis=0), jnp.concatenate(rows_q, axis=0)]
        kf_rev, kb = toeplitz
        blocks = []
        for t in range(CHUNK):
            fwd = pltpu.roll(kf_rev, (CW - (CHUNK - 1 - t) * SSM_GROUP) % CW, axis=1)
            bwd = pltpu.roll(kb, t * SSM_GROUP, axis=1)
            blocks.append(jnp.where(lane < (t + 1) * SSM_GROUP, fwd, 0.0)
                          + jnp.where(lane >= t * SSM_GROUP, bwd, 0.0))
        mt_ref[gi] = jnp.concatenate(blocks, axis=0).astype(mt_ref.dtype)
        gt_ref[gi] = jnp.concatenate(gt_cols, axis=1).astype(gt_ref.dtype)
        wo_ref[gi] = jnp.concatenate(wo_cols, axis=1).astype(wo_ref.dtype)
        at_ref[gi] = jnp.concatenate(at_cols, axis=1)


def _ssm_prep(lam_re, lam_im, log_step, b_re, b_im, c_re, c_im):
    row = lambda a: a.reshape(2, N_GROUPS, 1, SSM_STATE)
    bt = lambda a: jnp.swapaxes(a, 2, 3)
    gb = GROUP_BLOCK
    vec_spec = pl.BlockSpec((2, gb, 1, SSM_STATE), lambda i: (0, i, 0, 0))
    mat_spec = pl.BlockSpec((2, gb, SSM_GROUP, SSM_STATE), lambda i: (0, i, 0, 0))
    w_spec = pl.BlockSpec((gb, CW, CW), lambda i: (i, 0, 0))
    w_shape = jax.ShapeDtypeStruct((N_GROUPS, CW, CW), BF16)
    return pl.pallas_call(
        _ssm_prep_kernel,
        grid=(N_GROUPS // gb,),
        in_specs=[vec_spec, vec_spec,
                  pl.BlockSpec((2, gb, 1, 1), lambda i: (0, i, 0, 0)),
                  mat_spec, mat_spec, mat_spec, mat_spec],
        out_specs=[w_spec, w_spec, w_spec,
                   pl.BlockSpec((gb, N_LEVELS, 4 * 2 * SSM_STATE), lambda i: (i, 0, 0))],
        out_shape=[w_shape, w_shape, w_shape,
                   jax.ShapeDtypeStruct((N_GROUPS, N_LEVELS, 4 * 2 * SSM_STATE), F32)],
        compiler_params=_cparams(1),
        name="ssm_prep",
    )(row(lam_re), row(lam_im), log_step.reshape(2, N_GROUPS, 1, 1),
      bt(b_re), bt(b_im), c_re, c_im)


def _shift_rows(x, m, down):
    n = x.shape[0]
    return pltpu.roll(x, m if down else n - m, axis=0)


def _ssm_kernel(*refs, n_seq, has_h0):
    if has_h0:
        (xt_ref, mt_ref, gt_ref, wo_ref, at_ref, d_ref, wg_ref, bg_ref, h0_ref,
         z_ref, st_ref, zs_ref) = refs
    else:
        (xt_ref, mt_ref, gt_ref, wo_ref, at_ref, d_ref, wg_ref, bg_ref,
         z_ref, st_ref, zs_ref) = refs
    step = pl.program_id(0)
    n_rows = xt_ref.shape[-1]
    seg = n_rows // n_seq
    cw2 = 2 * SSM_STATE
    pos = lax.broadcasted_iota(jnp.int32, (n_rows, cw2), 0) % seg
    tn = (((0,), (0,)), ((), ()))
    nt = (((1,), (1,)), ((), ()))
    for gi in range(GROUP_BLOCK):
        xt = xt_ref[gi]
        at = at_ref[gi]
        s_all = lax.dot_general(xt, gt_ref[gi], tn, preferred_element_type=F32)
        h_parts, fin_parts = [], []
        for dr in range(2):
            s = s_all[:, dr * cw2:(dr + 1) * cw2]
            p_tab = at[:, (2 * dr) * cw2:(2 * dr + 1) * cw2]
            q_tab = at[:, (2 * dr + 1) * cw2:(2 * dr + 2) * cw2]
            down = dr == 0
            if has_h0:
                h0 = jnp.zeros((n_rows, cw2), F32)
                riota = lax.broadcasted_iota(jnp.int32, (n_rows, cw2), 0)
                for b in range(n_seq):
                    edge = b * seg if down else b * seg + seg - 1
                    h0 = jnp.where(riota == edge, h0_ref[gi, b:b + 1, dr * cw2:(dr + 1) * cw2], h0)
                s = s + p_tab[0:1] * h0 + q_tab[0:1] * pltpu.roll(h0, SSM_STATE, axis=1)
            m, lvl = 1, 0
            while m < seg:
                sh = _shift_rows(s, m, down)
                valid = (pos >= m) if down else (pos < seg - m)
                sh = jnp.where(valid, sh, 0.0)
                s = s + p_tab[lvl:lvl + 1] * sh + q_tab[lvl:lvl + 1] * pltpu.roll(sh, SSM_STATE, axis=1)
                m, lvl = 2 * m, lvl + 1
            fin_parts.append(s)
            ent = _shift_rows(s, 1, down)
            ent = jnp.where((pos >= 1) if down else (pos < seg - 1), ent, 0.0)
            if has_h0:
                ent = ent + h0
            h_parts.append(ent)
        h_all = jnp.concatenate(h_parts, axis=1).astype(BF16)
        fin = jnp.concatenate(fin_parts, axis=1)
        st_ref[gi] = fin
        yt = (jnp.dot(mt_ref[gi], xt, preferred_element_type=F32)
              + lax.dot_general(wo_ref[gi], h_all, nt, preferred_element_type=F32)
              + d_ref[gi] * xt.astype(F32))
        z = jax.nn.gelu(yt, approximate=True)
        grp = step * GROUP_BLOCK + gi
        for t in range(CHUNK):
            zs_ref[t, pl.ds(pl.multiple_of(grp * SSM_GROUP, SSM_GROUP), SSM_GROUP), :] = (
                z[t * SSM_GROUP:(t + 1) * SSM_GROUP, :])

    @pl.when(step == pl.num_programs(0) - 1)
    def _():
        for t in range(CHUNK):
            zt = zs_ref[t]
            gate = lax.dot_general(wg_ref[...], zt.astype(BF16), tn,
                                   preferred_element_type=F32) + bg_ref[...]
            z_ref[t] = (zt * jax.nn.sigmoid(gate)).astype(z_ref.dtype)


def _ssm(xt, mt, gt, wo, at, d_col, w_glu, b_glu_col, h0, n_seq):
    n_rows = xt.shape[-1]
    has_h0 = h0 is not None
    gb = GROUP_BLOCK
    w_spec = pl.BlockSpec((gb, CW, CW), lambda i: (i, 0, 0))
    in_specs = [pl.BlockSpec((gb, CW, n_rows), lambda i: (i, 0, 0)),
                w_spec, w_spec, w_spec,
                pl.BlockSpec((gb, N_LEVELS, 4 * 2 * SSM_STATE), lambda i: (i, 0, 0)),
                pl.BlockSpec((gb, CW, 1), lambda i: (i, 0, 0)),
                pl.BlockSpec((SSM_WIDTH, SSM_WIDTH), lambda i: (0, 0)),
                pl.BlockSpec((SSM_WIDTH, 1), lambda i: (0, 0))]
    args = [xt, mt, gt, wo, at, d_col, w_glu, b_glu_col]
    if has_h0:
        in_specs.append(pl.BlockSpec((gb, n_seq, CW), lambda i: (i, 0, 0)))
        args.append(h0)
    return pl.pallas_call(
        functools.partial(_ssm_kernel, n_seq=n_seq, has_h0=has_h0),
        grid=(N_GROUPS // gb,),
        in_specs=in_specs,
        out_specs=[pl.BlockSpec((CHUNK, SSM_WIDTH, n_rows), lambda i: (0, 0, 0)),
                   pl.BlockSpec((gb, n_rows, CW), lambda i: (i, 0, 0))],
        out_shape=[jax.ShapeDtypeStruct((CHUNK, SSM_WIDTH, n_rows), BF16),
                   jax.ShapeDtypeStruct((N_GROUPS, n_rows, CW), F32)],
        scratch_shapes=[pltpu.VMEM((CHUNK, SSM_WIDTH, n_rows), F32)],
        compiler_params=_cparams(1),
        name="ssm_scan_glu",
    )(*args)


FF_CHUNK = 256
ROWS_PER_TILE = 128
TOK_PER_TILE = ROWS_PER_TILE * CHUNK
SUB_TOK = 512


def _out_ffn_kernel(x_ref, attn_ref, zt_ref, mod_ref, g_ref, wo_ref, wfi_ref, wfo_ref,
                    o_ref, mixz_ref):
    j = pl.program_id(1)
    tn = (((0,), (0,)), ((), ()))

    @pl.when(j == 0)
    def _():
        for t in range(CHUNK):
            part = lax.dot_general(zt_ref[t], wo_ref[ATT_WIDTH:, :], tn,
                                   preferred_element_type=F32)
            for cb in range(D_MODEL // LANES):
                mixz_ref[cb, pl.ds(t, ROWS_PER_TILE, stride=CHUNK), :] = (
                    part[:, cb * LANES:(cb + 1) * LANES])

    gate1 = mod_ref[:, 2 * D_MODEL:3 * D_MODEL]
    shift2 = mod_ref[:, 3 * D_MODEL:4 * D_MODEL]
    scale2 = mod_ref[:, 4 * D_MODEL:5 * D_MODEL]
    gate2 = mod_ref[:, 5 * D_MODEL:6 * D_MODEL]
    rows = pl.ds(pl.multiple_of(j * SUB_TOK, SUB_TOK), SUB_TOK)
    mix = (jnp.dot(attn_ref[...], wo_ref[0:ATT_WIDTH, :], preferred_element_type=F32)
           + jnp.concatenate([mixz_ref[cb, rows, :] for cb in range(D_MODEL // LANES)], axis=1))
    x1 = x_ref[...] + gate1 * _rms(mix, g_ref[1:2, :])
    h = (_rms(x1, g_ref[2:3, :]) * (1.0 + scale2) + shift2).astype(BF16)
    acc = jnp.zeros((SUB_TOK, D_MODEL), F32)
    for c in range(D_FF // FF_CHUNK):
        lo = c * FF_CHUNK
        gt = jnp.dot(h, wfi_ref[:, lo:lo + FF_CHUNK], preferred_element_type=F32)
        up = jnp.dot(h, wfi_ref[:, D_FF + lo:D_FF + lo + FF_CHUNK], preferred_element_type=F32)
        a = (_silu(gt) * up).astype(BF16)
        acc = acc + jnp.dot(a, wfo_ref[lo:lo + FF_CHUNK, :], preferred_element_type=F32)
    o_ref[...] = x1 + gate2 * _rms(acc, g_ref[3:4, :])


def _out_ffn(x2d, attn, zt, mod3, norm_g, w_o, w_ffn_in, w_ffn_out):
    n_tok = x2d.shape[0]
    n_tiles = n_tok // TOK_PER_TILE
    n_sub = TOK_PER_TILE // SUB_TOK
    tiles_per_mod = n_tiles // mod3.shape[0]
    const = lambda i, j: (0, 0)
    row_spec = lambda w: pl.BlockSpec((SUB_TOK, w), lambda i, j: (i * n_sub + j, 0))
    return pl.pallas_call(
        _out_ffn_kernel,
        grid=(n_tiles, n_sub),
        in_specs=[row_spec(D_MODEL), row_spec(ATT_WIDTH),
                  pl.BlockSpec((CHUNK, SSM_WIDTH, ROWS_PER_TILE), lambda i, j: (0, 0, i)),
                  pl.BlockSpec((None, 1, N_MOD * D_MODEL), lambda i, j: (i // tiles_per_mod, 0, 0)),
                  pl.BlockSpec((4, D_MODEL), const),
                  pl.BlockSpec((2 * ATT_WIDTH, D_MODEL), const, pipeline_mode=pl.Buffered(1)),
                  pl.BlockSpec((D_MODEL, 2 * D_FF), const, pipeline_mode=pl.Buffered(1)),
                  pl.BlockSpec((D_FF, D_MODEL), const, pipeline_mode=pl.Buffered(1))],
        out_specs=row_spec(D_MODEL),
        out_shape=jax.ShapeDtypeStruct((n_tok, D_MODEL), F32),
        scratch_shapes=[pltpu.VMEM((D_MODEL // LANES, TOK_PER_TILE, LANES), F32)],
        compiler_params=_cparams(2),
        name="out_proj_ffn",
    )(x2d, attn, zt, mod3, norm_g, w_o, w_ffn_in, w_ffn_out)


def _rope_tables(seq_len):
    t = jnp.arange(seq_len)
    row = (t // GRID_W).astype(F32)
    col = (t % GRID_W).astype(F32)
    half = HEAD_DIM // 2
    inv_freq = ROPE_BASE ** (-jnp.arange(0, half, 2, dtype=F32) / half)
    ang_r = row[:, None] * inv_freq
    ang_c = col[:, None] * inv_freq
    ang = jnp.concatenate([ang_r, ang_r, ang_c, ang_c], axis=-1)
    cos, sin = jnp.cos(ang), jnp.sin(ang)
    upper = (jnp.arange(HEAD_DIM) % 32) < 16
    sa = jnp.where(upper, -sin, 0.0)
    sb = jnp.where(upper, 0.0, sin)
    two = lambda a: jnp.concatenate([a, a], axis=-1)
    return two(cos), two(sa), two(sb)


def _layer(x, mod_rows, lam_init, rope_tabs, ctx_k, ctx_v, h0, weights, prep, kv_dtype):
    n_batch, seq_len = x.shape[:2]
    x2d = x.reshape(n_batch * seq_len, D_MODEL)
    mod3 = mod_rows[:, None, :]
    mod_in = mod3[:, :, 0:2 * D_MODEL]
    g = weights['norm_g']
    q, k, v = _qkv_proj(x2d, mod_in, g[0:1], weights['w_qkv'], rope_tabs, seq_len, kv_dtype)
    ut = _ut_proj(x2d, mod_in, g[0:1], weights['w_ut'])
    attn = _attention(q, k, v, ctx_k, ctx_v, weights['lam'], weights['subln_g'],
                      n_batch, seq_len, lam_init)
    n_rows = n_batch * seq_len // CHUNK
    zt, st = _ssm(ut.reshape(N_GROUPS, CW, n_rows), *prep, weights['d_col'],
                  weights['w_glu'], weights['b_glu_col'], h0, n_batch)
    y = _out_ffn(x2d, attn, zt, mod3, g, weights['w_o'], weights['w_ffn_in'], weights['w_ffn_out'])
    return y.reshape(x.shape), k, v, st


def kernel(x_prompt, x_sample, cache_k, cache_v, state_ssm_re, state_ssm_im, c, c_ctx, w_mod, b_mod, norm_g, w_in, lam_params, subln_g, ssm_lambda_re, ssm_lambda_im, ssm_log_step, ssm_b_re, ssm_b_im, ssm_c_re, ssm_c_im, ssm_d, w_glu, b_glu, w_o, w_ffn_in, w_ffn_out):
    depth = w_mod.shape[0]
    assert depth == 1
    bp, lp_len = x_prompt.shape[:2]
    bd, ld_len = x_sample.shape[:2]
    past = cache_k.shape[2]
    xp, xs = x_prompt, x_sample
    cond = jnp.zeros((8, D_MODEL), F32).at[0].set(c_ctx).at[1:1 + bd].set(c)
    rope_tabs = _rope_tables(ld_len)
    ks_out, vs_out, hr_out, hi_out = [], [], [], []
    for l in range(depth):
        lam_init = 0.8 - 0.6 * math.exp(-0.3 * l)
        mods = _modulation(cond, w_mod[l], b_mod[l])
        d_sum = (ssm_d[l, 0] + ssm_d[l, 1]).reshape(N_GROUPS, 1, SSM_GROUP)
        weights = {
            'norm_g': norm_g[l],
            'w_qkv': w_in[l][:, 0:3 * ATT_WIDTH].astype(BF16),
            'w_ut': w_in[l][:, 3 * ATT_WIDTH:].T.astype(BF16),
            'lam': lam_params[l], 'subln_g': subln_g[l],
            'd_col': jnp.broadcast_to(d_sum, (N_GROUPS, CHUNK, SSM_GROUP)).reshape(N_GROUPS, CW, 1),
            'w_glu': w_glu[l].astype(BF16), 'b_glu_col': b_glu[l].reshape(SSM_WIDTH, 1),
            'w_o': w_o[l].astype(BF16),
            'w_ffn_in': w_ffn_in[l].astype(BF16), 'w_ffn_out': w_ffn_out[l].astype(BF16),
        }
        prep = _ssm_prep(ssm_lambda_re[l], ssm_lambda_im[l], ssm_log_step[l],
                         ssm_b_re[l], ssm_b_im[l], ssm_c_re[l], ssm_c_im[l])
        xp, k_ctx, v_ctx, st = _layer(xp, mods[0:1], lam_init, None, None, None, None,
                                      weights, prep, F32)
        ks_out.append(k_ctx.reshape(bp, lp_len, 2 * N_HEADS, HEAD_DIM))
        vs_out.append(v_ctx.reshape(bp, lp_len, N_HEADS, 2 * HEAD_DIM))
        n_chunks = lp_len // CHUNK
        st = st.reshape(N_GROUPS, bp, n_chunks, 2, 2, SSM_STATE)
        fin = jnp.stack([st[:, :, n_chunks - 1, 0], st[:, :, 0, 1]], axis=0)
        fin = fin.transpose(2, 0, 3, 1, 4)
        hr_out.append(fin[:, :, 0])
        hi_out.append(fin[:, :, 1])
        ck = cache_k[:, l].reshape(bd, past, ATT_WIDTH)
        cv = cache_v[:, l].reshape(bd, past, ATT_WIDTH)
        h0 = jnp.stack([state_ssm_re[:, l], state_ssm_im[:, l]], axis=2)
        h0 = h0.transpose(3, 0, 1, 2, 4).reshape(N_GROUPS, bd, CW)
        xs, _, _, _ = _layer(xs, mods[1:1 + bd], lam_init, rope_tabs, ck, cv, h0,
                             weights, prep, BF16)
    return (xp, xs, jnp.stack(ks_out, axis=1), jnp.stack(vs_out, axis=1),
            jnp.stack(hr_out, axis=1), jnp.stack(hi_out, axis=1))
```

```python
import functools
import math

import jax
import jax.numpy as jnp
from jax import lax
from jax.experimental import pallas as pl
from jax.experimental.pallas import tpu as pltpu

F32 = jnp.float32
BF16 = jnp.bfloat16

D_MODEL = 1024
GRID_W = 64
ATT_WIDTH = 512
SSM_WIDTH = 512
HEAD_DIM = 64
N_HEADS = 4
HEAD_W = 2 * HEAD_DIM
SSM_GROUP = 16
N_GROUPS = 32
SSM_STATE = 64
D_FF = 2816
N_MOD = 6
ROPE_BASE = 10000.0
NORM_EPS = 1e-6

LANES = 128
CHUNK = 16
CW = CHUNK * SSM_GROUP
N_LEVELS = 8
GROUP_BLOCK = 4

ROWS_PER_TILE = 128
TOK_PER_TILE = ROWS_PER_TILE * CHUNK
SUB_TOK = 512
N_SUB = TOK_PER_TILE // SUB_TOK

VMEM_LIMIT = 56 * 1024 * 1024

NT_DIMS = (((1,), (1,)), ((), ()))
TN_DIMS = (((0,), (0,)), ((), ()))


def _cparams(n_axes):
    return pltpu.CompilerParams(
        dimension_semantics=("arbitrary",) * n_axes,
        vmem_limit_bytes=VMEM_LIMIT)


def _rms(x, g):
    ms = jnp.mean(x * x, axis=-1, keepdims=True)
    return x * lax.rsqrt(ms + NORM_EPS) * g


def _silu(x):
    return x * jax.nn.sigmoid(x)


def _mod_kernel(cond_ref, w_ref, b_ref, o_ref):
    s = _silu(cond_ref[...]).astype(BF16)
    o_ref[...] = jnp.dot(s, w_ref[...].astype(BF16),
                         preferred_element_type=F32) + b_ref[...]


def _modulation(cond, w_mod, b_mod):
    n = w_mod.shape[1]
    tn = 1536
    return pl.pallas_call(
        _mod_kernel,
        grid=(n // tn,),
        in_specs=[pl.BlockSpec((8, D_MODEL), lambda i: (0, 0)),
                  pl.BlockSpec((D_MODEL, tn), lambda i: (0, i)),
                  pl.BlockSpec((1, tn), lambda i: (0, i))],
        out_specs=pl.BlockSpec((8, tn), lambda i: (0, i)),
        out_shape=jax.ShapeDtypeStruct((8, n), F32),
        compiler_params=_cparams(1),
        name="modulation",
    )(cond, w_mod, b_mod.reshape(1, n))


def _rope(x, cos, sa, sb):
    return (x * cos + pltpu.roll(x, HEAD_W - 16, axis=1) * sa
            + pltpu.roll(x, 16, axis=1) * sb)


def _in_proj_kernel(*refs, rope, seqs_per_sub):
    x_ref, mod_ref, g_ref, w_ref = refs[:4]
    refs = refs[4:]
    if rope:
        cos_ref, sa_ref, sb_ref = refs[:3]
        refs = refs[3:]
        q_ref, k_ref, v_ref, ut_ref, wut_ref = refs
    else:
        q_ref, k_ref, v_ref, kc_ref, vc_ref, ut_ref, wut_ref = refs
    j = pl.program_id(1)
    shift = mod_ref[:, 0:D_MODEL]
    scale = mod_ref[:, D_MODEL:2 * D_MODEL]
    g = g_ref[...]

    def norm_mod(xv):
        return (_rms(xv, g) * (1.0 + scale) + shift).astype(BF16)

    @pl.when((pl.program_id(0) == 0) & (j == 0))
    def _():
        wut_ref[...] = w_ref[:, 3 * ATT_WIDTH:].T

    rows = SUB_TOK // CHUNK
    xs = x_ref[pl.ds(pl.multiple_of(j * rows, rows), rows)].reshape(SUB_TOK, D_MODEL)
    proj = jnp.dot(norm_mod(xs), w_ref[:, 0:3 * ATT_WIDTH], preferred_element_type=F32)
    q = proj[:, 0:ATT_WIDTH]
    k = proj[:, ATT_WIDTH:2 * ATT_WIDTH]
    v = proj[:, 2 * ATT_WIDTH:3 * ATT_WIDTH]
    qscale = HEAD_DIM ** -0.5
    if rope:
        cos, sa, sb = cos_ref[...], sa_ref[...], sb_ref[...]
        for hd in range(N_HEADS):
            sl = slice(hd * HEAD_W, (hd + 1) * HEAD_W)
            q_ref[:, sl] = (_rope(q[:, sl], cos, sa, sb) * qscale).astype(q_ref.dtype)
            k_ref[:, sl] = _rope(k[:, sl], cos, sa, sb).astype(k_ref.dtype)
    else:
        q_ref[...] = (q * qscale).astype(q_ref.dtype)
        k_ref[...] = k.astype(k_ref.dtype)
        seq = SUB_TOK // seqs_per_sub
        for b in range(seqs_per_sub):
            for m in range(2 * N_HEADS):
                kc_ref[b, :, m, :] = k[b * seq:(b + 1) * seq, m * HEAD_DIM:(m + 1) * HEAD_DIM]
            for hd in range(N_HEADS):
                vc_ref[b, :, hd, :] = v[b * seq:(b + 1) * seq, hd * HEAD_W:(hd + 1) * HEAD_W]
    v_ref[...] = v.astype(v_ref.dtype)

    t_per_sub = CHUNK // N_SUB
    for jj in range(N_SUB):
        @pl.when(j == jj)
        def _():
            for t0 in range(jj * t_per_sub, (jj + 1) * t_per_sub, 2):
                xt = jnp.concatenate([x_ref[:, t0, :], x_ref[:, t0 + 1, :]], axis=0)
                ut = lax.dot_general(wut_ref[...], norm_mod(xt), NT_DIMS,
                                     preferred_element_type=F32)
                for d in range(2):
                    blk = ut[:, d * ROWS_PER_TILE:(d + 1) * ROWS_PER_TILE]
                    ut_ref[:, (t0 + d) * SSM_GROUP:(t0 + d + 1) * SSM_GROUP, :] = (
                        blk.reshape(N_GROUPS, SSM_GROUP, ROWS_PER_TILE).astype(ut_ref.dtype))


def _in_proj(x, mod3, g0, w_in, rope_tabs):
    n_batch, seq_len = x.shape[:2]
    n_tok = n_batch * seq_len
    n_rows = n_tok // CHUNK
    n_tiles = n_tok // TOK_PER_TILE
    tiles_per_mod = n_tiles // mod3.shape[0]
    x3 = x.reshape(n_rows, CHUNK, D_MODEL)
    rope = rope_tabs is not None
    seqs_per_sub = max(1, SUB_TOK // seq_len)
    in_specs = [pl.BlockSpec((ROWS_PER_TILE, CHUNK, D_MODEL), lambda i, j: (i, 0, 0)),
                pl.BlockSpec((None, 1, 2 * D_MODEL), lambda i, j: (i // tiles_per_mod, 0, 0)),
                pl.BlockSpec((1, D_MODEL), lambda i, j: (0, 0)),
                pl.BlockSpec((D_MODEL, 4 * ATT_WIDTH), lambda i, j: (0, 0))]
    args = [x3, mod3, g0, w_in]
    row_spec = pl.BlockSpec((SUB_TOK, ATT_WIDTH), lambda i, j: (i * N_SUB + j, 0))
    row_shape = jax.ShapeDtypeStruct((n_tok, ATT_WIDTH), BF16)
    out_specs = [row_spec, row_spec, row_spec]
    out_shape = [row_shape, row_shape, row_shape]
    if rope:
        assert seq_len == TOK_PER_TILE
        for tab in rope_tabs:
            in_specs.append(pl.BlockSpec((SUB_TOK, HEAD_W), lambda i, j: (j, 0)))
            args.append(tab)
    else:
        out_specs += [pl.BlockSpec((seqs_per_sub, seq_len, 2 * N_HEADS, HEAD_DIM),
                                   lambda i, j: (i * N_SUB + j, 0, 0, 0)),
                      pl.BlockSpec((seqs_per_sub, seq_len, N_HEADS, HEAD_W),
                                   lambda i, j: (i * N_SUB + j, 0, 0, 0))]
        out_shape += [jax.ShapeDtypeStruct((n_batch, seq_len, 2 * N_HEADS, HEAD_DIM), F32),
                      jax.ShapeDtypeStruct((n_batch, seq_len, N_HEADS, HEAD_W), F32)]
    out_specs.append(pl.BlockSpec((N_GROUPS, CW, ROWS_PER_TILE), lambda i, j: (0, 0, i)))
    out_shape.append(jax.ShapeDtypeStruct((N_GROUPS, CW, n_rows), BF16))
    return pl.pallas_call(
        functools.partial(_in_proj_kernel, rope=rope, seqs_per_sub=seqs_per_sub),
        grid=(n_tiles, N_SUB),
        in_specs=in_specs,
        out_specs=out_specs,
        out_shape=out_shape,
        scratch_shapes=[pltpu.VMEM((SSM_WIDTH, D_MODEL), BF16)],
        compiler_params=_cparams(2),
        name="in_proj",
    )(*args)


def _attn_kernel(*refs, has_ctx, lam_init, n_seq, seq_len, tq):
    if has_ctx:
        lam_ref, sg_ref, q_ref, ck_ref, cv_ref, k_ref, v_ref, o_ref = refs
    else:
        lam_ref, sg_ref, q_ref, k_ref, v_ref, o_ref = refs
    lp = lam_ref[...]
    lam = (jnp.exp(jnp.sum(lp[0:1] * lp[1:2], axis=-1, keepdims=True))
           - jnp.exp(jnp.sum(lp[2:3] * lp[3:4], axis=-1, keepdims=True)) + lam_init)
    first_map = lax.broadcasted_iota(jnp.int32, (1, HEAD_W), 1) < HEAD_DIM
    for b in range(n_seq):
        for hd in range(N_HEADS):
            sl = slice(hd * HEAD_W, (hd + 1) * HEAD_W)
            qh = q_ref[b * tq:(b + 1) * tq, sl]
            zero = jnp.zeros_like(qh)
            qs = jnp.concatenate([jnp.where(first_map, qh, zero),
                                  jnp.where(first_map, zero, qh)], axis=0)
            kv_rows = slice(b * seq_len, (b + 1) * seq_len)
            parts = [(k_ref[kv_rows, sl], v_ref[kv_rows, sl])]
            if has_ctx:
                parts.insert(0, (ck_ref[:, sl].astype(BF16), cv_ref[:, sl].astype(BF16)))
            scores = [lax.dot_general(qs, kk, NT_DIMS, preferred_element_type=F32)
                      for kk, _ in parts]
            mx = scores[0].max(axis=-1, keepdims=True)
            for s in scores[1:]:
                mx = jnp.maximum(mx, s.max(axis=-1, keepdims=True))
            es = [jnp.exp(s - mx) for s in scores]
            den = es[0].sum(axis=-1, keepdims=True)
            for e in es[1:]:
                den = den + e.sum(axis=-1, keepdims=True)
            inv = 1.0 / den
            inv0 = inv[0:tq]
            inv1 = inv[tq:2 * tq] * lam
            o = None
            for e, (_, vv) in zip(es, parts):
                w = (e[0:tq] * inv0 - e[tq:2 * tq] * inv1).astype(BF16)
                pv = jnp.dot(w, vv, preferred_element_type=F32)
                o = pv if o is None else o + pv
            o = _rms(o, sg_ref[...]) * (1.0 - lam_init)
            o_ref[b * tq:(b + 1) * tq, sl] = o.astype(o_ref.dtype)


def _attention(q, k, v, ctx_k, ctx_v, lam_params, subln_g, n_batch, seq_len, lam_init):
    has_ctx = ctx_k is not None
    tq = 256
    n_q = seq_len // tq
    n_seq = 1 if n_q > 1 else min(4, n_batch)
    in_specs = [pl.BlockSpec((4, HEAD_DIM), lambda b, i: (0, 0)),
                pl.BlockSpec((1, HEAD_W), lambda b, i: (0, 0)),
                pl.BlockSpec((n_seq * tq, ATT_WIDTH), lambda b, i: (b * n_q + i, 0))]
    args = [lam_params, subln_g.reshape(1, HEAD_W), q]
    if has_ctx:
        past = ctx_k.shape[1]
        ctx_spec = pl.BlockSpec((None, past, ATT_WIDTH), lambda b, i: (b, 0, 0))
        in_specs += [ctx_spec, ctx_spec]
        args += [ctx_k, ctx_v]
    kv_spec = pl.BlockSpec((n_seq * seq_len, ATT_WIDTH), lambda b, i: (b, 0))
    in_specs += [kv_spec, kv_spec]
    args += [k, v]
    return pl.pallas_call(
        functools.partial(_attn_kernel, has_ctx=has_ctx, lam_init=lam_init,
                          n_seq=n_seq, seq_len=seq_len, tq=tq),
        grid=(n_batch // n_seq, n_q),
        in_specs=in_specs,
        out_specs=pl.BlockSpec((n_seq * tq, ATT_WIDTH), lambda b, i: (b * n_q + i, 0)),
        out_shape=jax.ShapeDtypeStruct((n_batch * seq_len, ATT_WIDTH), BF16),
        compiler_params=_cparams(2),
        name="diff_attention",
    )(*args)


def _cmul(ar, ai, br, bi):
    return ar * br - ai * bi, ar * bi + ai * br


def _ssm_prep_kernel(lre_ref, lim_ref, ls_ref, bre_ref, bim_ref, cre_ref, cim_ref, d_ref,
                     mt_ref, gt_ref, wo_ref, at_ref):
    lane = lax.broadcasted_iota(jnp.int32, (SSM_GROUP, CW), 1)
    chan = lax.broadcasted_iota(jnp.int32, (SSM_GROUP, CW), 0)
    for gi in range(GROUP_BLOCK):
        gt_cols, wo_cols, at_cols, toeplitz = [], [], [], []
        for dr in range(2):
            lr = jnp.minimum(lre_ref[dr, gi], -1e-4)
            li = lim_ref[dr, gi]
            step = jnp.exp(ls_ref[dr, gi])
            mag = jnp.exp(lr * step)
            a_re = mag * jnp.cos(li * step)
            a_im = mag * jnp.sin(li * step)
            den = lr * lr + li * li
            nr = a_re - 1.0
            f_re = (nr * lr + a_im * li) / den
            f_im = (a_im * lr - nr * li) / den
            bt_re, bt_im = bre_ref[dr, gi], bim_ref[dr, gi]
            bb_re, bb_im = _cmul(f_re, f_im, bt_re, bt_im)
            c_re, c_im = cre_ref[dr, gi], cim_ref[dr, gi]
            pw = [(jnp.ones_like(a_re), jnp.zeros_like(a_im))]
            for _ in range(CHUNK):
                pw.append(_cmul(pw[-1][0], pw[-1][1], a_re, a_im))
            g_re, g_im, e_re, e_im = [], [], [], []
            for t in range(CHUNK):
                pr, pi = pw[CHUNK - 1 - t] if dr == 0 else pw[t]
                r, i = _cmul(bb_re, bb_im, pr, pi)
                g_re.append(r)
                g_im.append(i)
                pr, pi = pw[t + 1] if dr == 0 else pw[CHUNK - t]
                r, i = _cmul(c_re, c_im, pr, pi)
                e_re.append(r)
                e_im.append(-i)
            g_cat = jnp.concatenate([jnp.concatenate(g_re, axis=0),
                                     jnp.concatenate(g_im, axis=0)], axis=1)
            gt_cols.append(g_cat)
            wo_cols.append(jnp.concatenate([jnp.concatenate(e_re, axis=0),
                                            jnp.concatenate(e_im, axis=0)], axis=1))
            c_cat = jnp.concatenate([c_re, -c_im], axis=1)
            toeplitz.append(lax.dot_general(c_cat, g_cat, NT_DIMS,
                                            precision=lax.Precision.HIGHEST,
                                            preferred_element_type=F32))
            pr, pi = pw[CHUNK]
            rows_p, rows_q = [], []
            for _ in range(N_LEVELS - 1):
                rows_p.append(jnp.concatenate([pr, pr], axis=1))
                rows_q.append(jnp.concatenate([-pi, pi], axis=1))
                pr, pi = _cmul(pr, pi, pr, pi)
            rows_p.append(jnp.zeros_like(rows_p[0]))
            rows_q.append(jnp.zeros_like(rows_q[0]))
            at_cols += [jnp.concatenate(rows_p, axis=0), jnp.concatenate(rows_q, axis=0)]
        kf_rev, kb = toeplitz
        d_skip = d_ref[gi]
        blocks = []
        for t in range(CHUNK):
            fwd = pltpu.roll(kf_rev, (CW - (CHUNK - 1 - t) * SSM_GROUP) % CW, axis=1)
            bwd = pltpu.roll(kb, t * SSM_GROUP, axis=1)
            blocks.append(jnp.where(lane < (t + 1) * SSM_GROUP, fwd, 0.0)
                          + jnp.where(lane >= t * SSM_GROUP, bwd, 0.0)
                          + jnp.where(lane == chan + t * SSM_GROUP, d_skip, 0.0))
        mt_ref[gi] = jnp.concatenate(blocks, axis=0).astype(mt_ref.dtype)
        gt_ref[gi] = jnp.concatenate(gt_cols, axis=1).astype(gt_ref.dtype)
        wo_ref[gi] = jnp.concatenate(wo_cols, axis=1).astype(wo_ref.dtype)
        at_ref[gi] = jnp.concatenate(at_cols, axis=1)


def _ssm_prep(lam_re, lam_im, log_step, b_re, b_im, c_re, c_im, d_skip):
    row = lambda a: a.reshape(2, N_GROUPS, 1, SSM_STATE)
    bt = lambda a: jnp.swapaxes(a, 2, 3)
    d_col = (d_skip[0] + d_skip[1]).reshape(N_GROUPS, SSM_GROUP, 1)
    gb = GROUP_BLOCK
    vec_spec = pl.BlockSpec((2, gb, 1, SSM_STATE), lambda i: (0, i, 0, 0))
    mat_spec = pl.BlockSpec((2, gb, SSM_GROUP, SSM_STATE), lambda i: (0, i, 0, 0))
    w_spec = pl.BlockSpec((gb, CW, CW), lambda i: (i, 0, 0))
    w_shape = jax.ShapeDtypeStruct((N_GROUPS, CW, CW), BF16)
    return pl.pallas_call(
        _ssm_prep_kernel,
        grid=(N_GROUPS // gb,),
        in_specs=[vec_spec, vec_spec,
                  pl.BlockSpec((2, gb, 1, 1), lambda i: (0, i, 0, 0)),
                  mat_spec, mat_spec, mat_spec, mat_spec,
                  pl.BlockSpec((gb, SSM_GROUP, 1), lambda i: (i, 0, 0))],
        out_specs=[w_spec, w_spec, w_spec,
                   pl.BlockSpec((gb, N_LEVELS, 4 * 2 * SSM_STATE), lambda i: (i, 0, 0))],
        out_shape=[w_shape, w_shape, w_shape,
                   jax.ShapeDtypeStruct((N_GROUPS, N_LEVELS, 4 * 2 * SSM_STATE), F32)],
        compiler_params=_cparams(1),
        name="ssm_prep",
    )(row(lam_re), row(lam_im), log_step.reshape(2, N_GROUPS, 1, 1),
      bt(b_re), bt(b_im), c_re, c_im, d_col)


def _shift_rows(x, m, down):
    n = x.shape[0]
    return pltpu.roll(x, m if down else n - m, axis=0)


def _ssm_kernel(*refs, n_seq, has_h0):
    if has_h0:
        (xt_ref, mt_ref, gt_ref, wo_ref, at_ref, wg_ref, bg_ref, h0_ref,
         z_ref, zs_ref) = refs
    else:
        (xt_ref, mt_ref, gt_ref, wo_ref, at_ref, wg_ref, bg_ref,
         z_ref, st_ref, zs_ref, fin_ref) = refs
    step = pl.program_id(0)
    n_rows = xt_ref.shape[-1]
    seg = n_rows // n_seq
    cw2 = 2 * SSM_STATE
    pos = lax.broadcasted_iota(jnp.int32, (n_rows, cw2), 0) % seg
    for gi in range(GROUP_BLOCK):
        xt = xt_ref[gi]
        at = at_ref[gi]
        s_all = lax.dot_general(xt, gt_ref[gi], TN_DIMS, preferred_element_type=F32)
        h_parts = []
        for dr in range(2):
            s = s_all[:, dr * cw2:(dr + 1) * cw2]
            p_tab = at[:, (2 * dr) * cw2:(2 * dr + 1) * cw2]
            q_tab = at[:, (2 * dr + 1) * cw2:(2 * dr + 2) * cw2]
            down = dr == 0
            if has_h0:
                h0 = jnp.zeros((n_rows, cw2), F32)
                riota = lax.broadcasted_iota(jnp.int32, (n_rows, cw2), 0)
                for b in range(n_seq):
                    edge = b * seg if down else b * seg + seg - 1
                    h0 = jnp.where(riota == edge, h0_ref[gi, b:b + 1, dr * cw2:(dr + 1) * cw2], h0)
                s = s + p_tab[0:1] * h0 + q_tab[0:1] * pltpu.roll(h0, SSM_STATE, axis=1)
            m, lvl = 1, 0
            while m < seg:
                sh = _shift_rows(s, m, down)
                valid = (pos >= m) if down else (pos < seg - m)
                sh = jnp.where(valid, sh, 0.0)
                s = s + p_tab[lvl:lvl + 1] * sh + q_tab[lvl:lvl + 1] * pltpu.roll(sh, SSM_STATE, axis=1)
                m, lvl = 2 * m, lvl + 1
            if not has_h0:
                fin_ref[dr] = s
                st_ref[gi, :, dr * cw2:(dr + 1) * cw2] = (
                    fin_ref[dr, pl.ds(seg - 1 if down else 0, n_seq, stride=seg), :])
            ent = _shift_rows(s, 1, down)
            ent = jnp.where((pos >= 1) if down else (pos < seg - 1), ent, 0.0)
            if has_h0:
                ent = ent + h0
            h_parts.append(ent)
        h_all = jnp.concatenate(h_parts, axis=1).astype(BF16)
        yt = (jnp.dot(mt_ref[gi], xt, preferred_element_type=F32)
              + lax.dot_general(wo_ref[gi], h_all, NT_DIMS, preferred_element_type=F32))
        z = jax.nn.gelu(yt, approximate=True)
        grp = step * GROUP_BLOCK + gi
        for t in range(CHUNK):
            zs_ref[t, pl.ds(pl.multiple_of(grp * SSM_GROUP, SSM_GROUP), SSM_GROUP), :] = (
                z[t * SSM_GROUP:(t + 1) * SSM_GROUP, :])

    @pl.when(step == pl.num_programs(0) - 1)
    def _():
        for t in range(CHUNK):
            zt = zs_ref[t]
            gate = lax.dot_general(wg_ref[...], zt.astype(BF16), TN_DIMS,
                                   preferred_element_type=F32) + bg_ref[...]
            z_ref[t] = (zt * jax.nn.sigmoid(gate)).astype(z_ref.dtype)


def _ssm(xt, mt, gt, wo, at, w_glu, b_glu_col, h0, n_seq):
    n_rows = xt.shape[-1]
    has_h0 = h0 is not None
    gb = GROUP_BLOCK
    w_spec = pl.BlockSpec((gb, CW, CW), lambda i: (i, 0, 0))
    in_specs = [pl.BlockSpec((gb, CW, n_rows), lambda i: (i, 0, 0)),
                w_spec, w_spec, w_spec,
                pl.BlockSpec((gb, N_LEVELS, 4 * 2 * SSM_STATE), lambda i: (i, 0, 0)),
                pl.BlockSpec((SSM_WIDTH, SSM_WIDTH), lambda i: (0, 0)),
                pl.BlockSpec((SSM_WIDTH, 1), lambda i: (0, 0))]
    args = [xt, mt, gt, wo, at, w_glu, b_glu_col]
    out_specs = [pl.BlockSpec((CHUNK, SSM_WIDTH, n_rows), lambda i: (0, 0, 0))]
    out_shape = [jax.ShapeDtypeStruct((CHUNK, SSM_WIDTH, n_rows), BF16)]
    scratch = [pltpu.VMEM((CHUNK, SSM_WIDTH, n_rows), F32)]
    if has_h0:
        in_specs.append(pl.BlockSpec((gb, n_seq, CW), lambda i: (i, 0, 0)))
        args.append(h0)
    else:
        out_specs.append(pl.BlockSpec((gb, n_seq, CW), lambda i: (i, 0, 0)))
        out_shape.append(jax.ShapeDtypeStruct((N_GROUPS, n_seq, CW), F32))
        scratch.append(pltpu.VMEM((2, n_rows, 2 * SSM_STATE), F32))
    return pl.pallas_call(
        functools.partial(_ssm_kernel, n_seq=n_seq, has_h0=has_h0),
        grid=(N_GROUPS // gb,),
        in_specs=in_specs,
        out_specs=out_specs,
        out_shape=out_shape,
        scratch_shapes=scratch,
        compiler_params=_cparams(1),
        name="ssm_scan_glu",
    )(*args)


FF_CHUNK = 256


def _out_ffn_kernel(x_ref, attn_ref, zt_ref, mod_ref, g_ref, wo_ref, wfi_ref, wfo_ref,
                    o_ref, mixz_ref):
    j = pl.program_id(1)

    @pl.when(j == 0)
    def _():
        for t in range(CHUNK):
            part = lax.dot_general(zt_ref[t], wo_ref[ATT_WIDTH:, :], TN_DIMS,
                                   preferred_element_type=F32)
            for cb in range(D_MODEL // LANES):
                mixz_ref[cb, pl.ds(t, ROWS_PER_TILE, stride=CHUNK), :] = (
                    part[:, cb * LANES:(cb + 1) * LANES])

    gate1 = mod_ref[:, 2 * D_MODEL:3 * D_MODEL]
    shift2 = mod_ref[:, 3 * D_MODEL:4 * D_MODEL]
    scale2 = mod_ref[:, 4 * D_MODEL:5 * D_MODEL]
    gate2 = mod_ref[:, 5 * D_MODEL:6 * D_MODEL]
    rows = pl.ds(pl.multiple_of(j * SUB_TOK, SUB_TOK), SUB_TOK)
    mix = (jnp.dot(attn_ref[...], wo_ref[0:ATT_WIDTH, :], preferred_element_type=F32)
           + jnp.concatenate([mixz_ref[cb, rows, :] for cb in range(D_MODEL // LANES)], axis=1))
    x1 = x_ref[...] + gate1 * _rms(mix, g_ref[1:2, :])
    h = (_rms(x1, g_ref[2:3, :]) * (1.0 + scale2) + shift2).astype(BF16)
    acc = jnp.zeros((SUB_TOK, D_MODEL), F32)
    for c in range(D_FF // FF_CHUNK):
        lo = c * FF_CHUNK
        gt = jnp.dot(h, wfi_ref[:, lo:lo + FF_CHUNK], preferred_element_type=F32)
        up = jnp.dot(h, wfi_ref[:, D_FF + lo:D_FF + lo + FF_CHUNK], preferred_element_type=F32)
        a = (_silu(gt) * up).astype(BF16)
        acc = acc + jnp.dot(a, wfo_ref[lo:lo + FF_CHUNK, :], preferred_element_type=F32)
    o_ref[...] = x1 + gate2 * _rms(acc, g_ref[3:4, :])


def _out_ffn(x2d, attn, zt, mod3, norm_g, w_o, w_ffn_in, w_ffn_out):
    n_tok = x2d.shape[0]
    n_tiles = n_tok // TOK_PER_TILE
    tiles_per_mod = n_tiles // mod3.shape[0]
    const = lambda i, j: (0, 0)
    row_spec = lambda w: pl.BlockSpec((SUB_TOK, w), lambda i, j: (i * N_SUB + j, 0))
    return pl.pallas_call(
        _out_ffn_kernel,
        grid=(n_tiles, N_SUB),
        in_specs=[row_spec(D_MODEL), row_spec(ATT_WIDTH),
                  pl.BlockSpec((CHUNK, SSM_WIDTH, ROWS_PER_TILE), lambda i, j: (0, 0, i)),
                  pl.BlockSpec((None, 1, N_MOD * D_MODEL), lambda i, j: (i // tiles_per_mod, 0, 0)),
                  pl.BlockSpec((4, D_MODEL), const),
                  pl.BlockSpec((2 * ATT_WIDTH, D_MODEL), const, pipeline_mode=pl.Buffered(1)),
                  pl.BlockSpec((D_MODEL, 2 * D_FF), const, pipeline_mode=pl.Buffered(1)),
                  pl.BlockSpec((D_FF, D_MODEL), const, pipeline_mode=pl.Buffered(1))],
        out_specs=row_spec(D_MODEL),
        out_shape=jax.ShapeDtypeStruct((n_tok, D_MODEL), F32),
        scratch_shapes=[pltpu.VMEM((D_MODEL // LANES, TOK_PER_TILE, LANES), F32)],
        compiler_params=_cparams(2),
        name="out_proj_ffn",
    )(x2d, attn, zt, mod3, norm_g, w_o, w_ffn_in, w_ffn_out)


def _rope_tables(seq_len):
    t = jnp.arange(seq_len)
    row = (t // GRID_W).astype(F32)
    col = (t % GRID_W).astype(F32)
    half = HEAD_DIM // 2
    inv_freq = ROPE_BASE ** (-jnp.arange(0, half, 2, dtype=F32) / half)
    ang_r = row[:, None] * inv_freq
    ang_c = col[:, None] * inv_freq
    ang = jnp.concatenate([ang_r, ang_r, ang_c, ang_c], axis=-1)
    cos, sin = jnp.cos(ang), jnp.sin(ang)
    upper = (jnp.arange(HEAD_DIM) % 32) < 16
    sa = jnp.where(upper, -sin, 0.0)
    sb = jnp.where(upper, 0.0, sin)
    two = lambda a: jnp.concatenate([a, a], axis=-1)
    return two(cos), two(sa), two(sb)


def _layer(x, mod_rows, lam_init, rope_tabs, ctx_k, ctx_v, h0, weights, prep):
    n_batch, seq_len = x.shape[:2]
    mod3 = mod_rows[:, None, :]
    g = weights['norm_g']
    outs = _in_proj(x, mod3[:, :, 0:2 * D_MODEL], g[0:1], weights['w_in'], rope_tabs)
    q, k, v = outs[:3]
    attn = _attention(q, k, v, ctx_k, ctx_v, weights['lam'], weights['subln_g'],
                      n_batch, seq_len, lam_init)
    ssm_out = _ssm(outs[-1], *prep, weights['w_glu'], weights['b_glu_col'], h0, n_batch)
    y = _out_ffn(x.reshape(n_batch * seq_len, D_MODEL), attn, ssm_out[0], mod3, g,
                 weights['w_o'], weights['w_ffn_in'], weights['w_ffn_out'])
    return y.reshape(x.shape), outs[3:-1], ssm_out[1:]


def kernel(x_prompt, x_sample, cache_k, cache_v, state_ssm_re, state_ssm_im, c, c_ctx, w_mod, b_mod, norm_g, w_in, lam_params, subln_g, ssm_lambda_re, ssm_lambda_im, ssm_log_step, ssm_b_re, ssm_b_im, ssm_c_re, ssm_c_im, ssm_d, w_glu, b_glu, w_o, w_ffn_in, w_ffn_out):
    depth = w_mod.shape[0]
    assert depth == 1
    bp = x_prompt.shape[0]
    bd, ld_len = x_sample.shape[:2]
    past = cache_k.shape[2]
    xp, xs = x_prompt, x_sample
    cond = jnp.zeros((8, D_MODEL), F32).at[0].set(c_ctx).at[1:1 + bd].set(c)
    rope_tabs = _rope_tables(ld_len)
    ks_out, vs_out, hr_out, hi_out = [], [], [], []
    for l in range(depth):
        lam_init = 0.8 - 0.6 * math.exp(-0.3 * l)
        mods = _modulation(cond, w_mod[l], b_mod[l])
        weights = {
            'norm_g': norm_g[l],
            'w_in': w_in[l].astype(BF16),
            'lam': lam_params[l], 'subln_g': subln_g[l],
            'w_glu': w_glu[l].astype(BF16), 'b_glu_col': b_glu[l].reshape(SSM_WIDTH, 1),
            'w_o': w_o[l].astype(BF16),
            'w_ffn_in': w_ffn_in[l].astype(BF16), 'w_ffn_out': w_ffn_out[l].astype(BF16),
        }
        prep = _ssm_prep(ssm_lambda_re[l], ssm_lambda_im[l], ssm_log_step[l],
                         ssm_b_re[l], ssm_b_im[l], ssm_c_re[l], ssm_c_im[l], ssm_d[l])
        xp, (k_ctx, v_ctx), (st,) = _layer(xp, mods[0:1], lam_init, None, None, None, None,
                                           weights, prep)
        ks_out.append(k_ctx)
        vs_out.append(v_ctx)
        fin = st.reshape(N_GROUPS, bp, 2, 2, SSM_STATE).transpose(1, 2, 3, 0, 4)
        hr_out.append(fin[:, :, 0])
        hi_out.append(fin[:, :, 1])
        ck = cache_k[:, l].reshape(bd, past, ATT_WIDTH)
        cv = cache_v[:, l].reshape(bd, past, ATT_WIDTH)
        h0 = jnp.stack([state_ssm_re[:, l], state_ssm_im[:, l]], axis=2)
        h0 = h0.transpose(3, 0, 1, 2, 4).reshape(N_GROUPS, bd, CW)
        xs, _, _ = _layer(xs, mods[1:1 + bd], lam_init, rope_tabs, ck, cv, h0, weights, prep)
    return (xp, xs, jnp.stack(ks_out, axis=1), jnp.stack(vs_out, axis=1),
            jnp.stack(hr_out, axis=1), jnp.stack(hi_out, axis=1))
```

```python
import functools
import math

import jax
import jax.numpy as jnp
from jax import lax
from jax.experimental import pallas as pl
from jax.experimental.pallas import tpu as pltpu

F32 = jnp.float32
BF16 = jnp.bfloat16

D_MODEL = 1024
GRID_W = 64
ATT_WIDTH = 512
SSM_WIDTH = 512
HEAD_DIM = 64
N_HEADS = 4
HEAD_W = 2 * HEAD_DIM
SSM_GROUP = 16
N_GROUPS = 32
SSM_STATE = 64
D_FF = 2816
N_MOD = 6
ROPE_BASE = 10000.0
NORM_EPS = 1e-6

LANES = 128
N_COL_BLOCKS = D_MODEL // LANES
CHUNK = 16
CW = CHUNK * SSM_GROUP
N_LEVELS = 8
GROUP_BLOCK = 4

ROWS_PER_TILE = 128
TOK_PER_TILE = ROWS_PER_TILE * CHUNK
SUB_TOK = 512
N_SUB = TOK_PER_TILE // SUB_TOK

VMEM_LIMIT = 56 * 1024 * 1024

NT_DIMS = (((1,), (1,)), ((), ()))
TN_DIMS = (((0,), (0,)), ((), ()))


def _cparams(n_axes):
    return pltpu.CompilerParams(
        dimension_semantics=("arbitrary",) * n_axes,
        vmem_limit_bytes=VMEM_LIMIT)


def _rms(x, g):
    ms = jnp.mean(x * x, axis=-1, keepdims=True)
    return x * lax.rsqrt(ms + NORM_EPS) * g


def _silu(x):
    return x * jax.nn.sigmoid(x)


def _mod_kernel(cond_ref, w_ref, b_ref, o_ref):
    s = _silu(cond_ref[...]).astype(BF16)
    o_ref[...] = jnp.dot(s, w_ref[...].astype(BF16),
                         preferred_element_type=F32) + b_ref[...]


def _modulation(cond, w_mod, b_mod):
    n = w_mod.shape[1]
    tn = 1536
    return pl.pallas_call(
        _mod_kernel,
        grid=(n // tn,),
        in_specs=[pl.BlockSpec((8, D_MODEL), lambda i: (0, 0)),
                  pl.BlockSpec((D_MODEL, tn), lambda i: (0, i)),
                  pl.BlockSpec((1, tn), lambda i: (0, i))],
        out_specs=pl.BlockSpec((8, tn), lambda i: (0, i)),
        out_shape=jax.ShapeDtypeStruct((8, n), F32),
        compiler_params=_cparams(1),
        name="modulation",
    )(cond, w_mod, b_mod.reshape(1, n))


def _rope(x, cos, sa, sb):
    return (x * cos + pltpu.roll(x, HEAD_W - 16, axis=1) * sa
            + pltpu.roll(x, 16, axis=1) * sb)


def _in_proj_kernel(*refs, rope, seqs_per_sub):
    x_ref, x3_hbm, mod_ref, g_ref, w_ref = refs[:5]
    refs = refs[5:]
    if rope:
        cos_ref, sa_ref, sb_ref = refs[:3]
        refs = refs[3:]
        q_ref, k_ref, v_ref, ut_ref, wut_ref, xt_ref, xt_sem = refs
    else:
        q_ref, k_ref, v_ref, kc_ref, vc_ref, ut_ref, wut_ref, xt_ref, xt_sem = refs
    tile = pl.program_id(0)
    j = pl.program_id(1)

    def gather(t):
        src = x3_hbm.at[pl.ds(tile * ROWS_PER_TILE, ROWS_PER_TILE), t, :]
        return pltpu.make_async_copy(src, xt_ref.at[t], xt_sem.at[t])

    @pl.when(j == 0)
    def _():
        for t in range(CHUNK):
            gather(t).start()

    shift = mod_ref[:, 0:D_MODEL]
    scale = mod_ref[:, D_MODEL:2 * D_MODEL]
    g = g_ref[...]

    def norm_mod(xv):
        return (_rms(xv, g) * (1.0 + scale) + shift).astype(BF16)

    @pl.when((pl.program_id(0) == 0) & (j == 0))
    def _():
        wut_ref[...] = w_ref[:, 3 * ATT_WIDTH:].T

    tok = pl.ds(pl.multiple_of(j * SUB_TOK, SUB_TOK), SUB_TOK)
    xs = x_ref[tok, :]
    proj = jnp.dot(norm_mod(xs), w_ref[:, 0:3 * ATT_WIDTH], preferred_element_type=F32)
    q = proj[:, 0:ATT_WIDTH]
    k = proj[:, ATT_WIDTH:2 * ATT_WIDTH]
    v = proj[:, 2 * ATT_WIDTH:3 * ATT_WIDTH]
    qscale = HEAD_DIM ** -0.5 * math.log2(math.e)
    if rope:
        cos, sa, sb = cos_ref[...], sa_ref[...], sb_ref[...]
        for hd in range(N_HEADS):
            sl = slice(hd * HEAD_W, (hd + 1) * HEAD_W)
            q_ref[:, sl] = (_rope(q[:, sl], cos, sa, sb) * qscale).astype(q_ref.dtype)
            k_ref[:, sl] = _rope(k[:, sl], cos, sa, sb).astype(k_ref.dtype)
    else:
        q_ref[...] = (q * qscale).astype(q_ref.dtype)
        k_ref[...] = k.astype(k_ref.dtype)
        seq = SUB_TOK // seqs_per_sub
        for b in range(seqs_per_sub):
            for m in range(2 * N_HEADS):
                kc_ref[b, :, m, :] = k[b * seq:(b + 1) * seq, m * HEAD_DIM:(m + 1) * HEAD_DIM]
            for hd in range(N_HEADS):
                vc_ref[b, :, hd, :] = v[b * seq:(b + 1) * seq, hd * HEAD_W:(hd + 1) * HEAD_W]
    v_ref[...] = v.astype(v_ref.dtype)

    t_per_sub = CHUNK // N_SUB
    for jj in range(N_SUB):
        @pl.when(j == jj)
        def _():
            for t0 in range(jj * t_per_sub, (jj + 1) * t_per_sub, 2):
                gather(t0).wait()
                gather(t0 + 1).wait()
                xt = jnp.concatenate([xt_ref[t0], xt_ref[t0 + 1]], axis=0)
                ut = lax.dot_general(wut_ref[...], norm_mod(xt), NT_DIMS,
                                     preferred_element_type=F32)
                for d in range(2):
                    blk = ut[:, d * ROWS_PER_TILE:(d + 1) * ROWS_PER_TILE]
                    ut_ref[:, (t0 + d) * SSM_GROUP:(t0 + d + 1) * SSM_GROUP, :] = (
                        blk.reshape(N_GROUPS, SSM_GROUP, ROWS_PER_TILE).astype(ut_ref.dtype))


def _in_proj(x, mod3, g0, w_in, rope_tabs):
    n_batch, seq_len = x.shape[:2]
    n_tok = n_batch * seq_len
    n_rows = n_tok // CHUNK
    n_tiles = n_tok // TOK_PER_TILE
    tiles_per_mod = n_tiles // mod3.shape[0]
    x2d = x.reshape(n_tok, D_MODEL)
    rope = rope_tabs is not None
    seqs_per_sub = max(1, SUB_TOK // seq_len)
    in_specs = [pl.BlockSpec((TOK_PER_TILE, D_MODEL), lambda i, j: (i, 0)),
                pl.BlockSpec(memory_space=pl.ANY),
                pl.BlockSpec((None, 1, 2 * D_MODEL), lambda i, j: (i // tiles_per_mod, 0, 0)),
                pl.BlockSpec((1, D_MODEL), lambda i, j: (0, 0)),
                pl.BlockSpec((D_MODEL, 4 * ATT_WIDTH), lambda i, j: (0, 0))]
    args = [x2d, x.reshape(n_rows, CHUNK, D_MODEL), mod3, g0, w_in]
    row_spec = pl.BlockSpec((SUB_TOK, ATT_WIDTH), lambda i, j: (i * N_SUB + j, 0))
    row_shape = jax.ShapeDtypeStruct((n_tok, ATT_WIDTH), BF16)
    out_specs = [row_spec, row_spec, row_spec]
    out_shape = [row_shape, row_shape, row_shape]
    if rope:
        assert seq_len == TOK_PER_TILE
        for tab in rope_tabs:
            in_specs.append(pl.BlockSpec((SUB_TOK, HEAD_W), lambda i, j: (j, 0)))
            args.append(tab)
    else:
        out_specs += [pl.BlockSpec((seqs_per_sub, seq_len, 2 * N_HEADS, HEAD_DIM),
                                   lambda i, j: (i * N_SUB + j, 0, 0, 0)),
                      pl.BlockSpec((seqs_per_sub, seq_len, N_HEADS, HEAD_W),
                                   lambda i, j: (i * N_SUB + j, 0, 0, 0))]
        out_shape += [jax.ShapeDtypeStruct((n_batch, seq_len, 2 * N_HEADS, HEAD_DIM), F32),
                      jax.ShapeDtypeStruct((n_batch, seq_len, N_HEADS, HEAD_W), F32)]
    out_specs.append(pl.BlockSpec((N_GROUPS, CW, ROWS_PER_TILE), lambda i, j: (0, 0, i)))
    out_shape.append(jax.ShapeDtypeStruct((N_GROUPS, CW, n_rows), BF16))
    return pl.pallas_call(
        functools.partial(_in_proj_kernel, rope=rope, seqs_per_sub=seqs_per_sub),
        grid=(n_tiles, N_SUB),
        in_specs=in_specs,
        out_specs=out_specs,
        out_shape=out_shape,
        scratch_shapes=[pltpu.VMEM((SSM_WIDTH, D_MODEL), BF16),
                        pltpu.VMEM((CHUNK, ROWS_PER_TILE, D_MODEL), F32),
                        pltpu.SemaphoreType.DMA((CHUNK,))],
        compiler_params=_cparams(2),
        name="in_proj",
    )(*args)


def _attn_kernel(*refs, has_ctx, lam_init, n_seq, seq_len, tq):
    if has_ctx:
        lam_ref, sg_ref, q_ref, ck_ref, cv_ref, k_ref, v_ref, o_ref = refs
    else:
        lam_ref, sg_ref, q_ref, k_ref, v_ref, o_ref = refs
    lp = lam_ref[...]
    lam = (jnp.exp(jnp.sum(lp[0:1] * lp[1:2], axis=-1, keepdims=True))
           - jnp.exp(jnp.sum(lp[2:3] * lp[3:4], axis=-1, keepdims=True)) + lam_init)
    first_map = lax.broadcasted_iota(jnp.int32, (1, HEAD_W), 1) < HEAD_DIM
    for b in range(n_seq):
        for hd in range(N_HEADS):
            sl = slice(hd * HEAD_W, (hd + 1) * HEAD_W)
            qh = q_ref[b * tq:(b + 1) * tq, sl]
            zero = jnp.zeros_like(qh)
            qs = jnp.concatenate([jnp.where(first_map, qh, zero),
                                  jnp.where(first_map, zero, qh)], axis=0)
            kv_rows = slice(b * seq_len, (b + 1) * seq_len)
            parts = [(k_ref[kv_rows, sl], v_ref[kv_rows, sl])]
            if has_ctx:
                parts.insert(0, (ck_ref[:, sl].astype(BF16), cv_ref[:, sl].astype(BF16)))
            scores = [lax.dot_general(qs, kk, NT_DIMS, preferred_element_type=F32)
                      for kk, _ in parts]
            mx = scores[0].max(axis=-1, keepdims=True)
            for s in scores[1:]:
                mx = jnp.maximum(mx, s.max(axis=-1, keepdims=True))
            acc = None
            for s, (_, vv) in zip(scores, parts):
                e = jnp.exp2(s - mx).astype(BF16)
                v_one = jnp.concatenate([vv, jnp.ones_like(vv)], axis=1)
                pv = jnp.dot(e, v_one, preferred_element_type=F32)
                acc = pv if acc is None else acc + pv
            num = acc[:, 0:HEAD_W] / acc[:, HEAD_W:2 * HEAD_W]
            o = num[0:tq] - lam * num[tq:2 * tq]
            o = _rms(o, sg_ref[...]) * (1.0 - lam_init)
            o_ref[b * tq:(b + 1) * tq, sl] = o.astype(o_ref.dtype)


def _attention(q, k, v, ctx_k, ctx_v, lam_params, subln_g, n_batch, seq_len, lam_init):
    has_ctx = ctx_k is not None
    tq = 256
    n_q = seq_len // tq
    n_seq = 1 if n_q > 1 else min(4, n_batch)
    in_specs = [pl.BlockSpec((4, HEAD_DIM), lambda b, i: (0, 0)),
                pl.BlockSpec((1, HEAD_W), lambda b, i: (0, 0)),
                pl.BlockSpec((n_seq * tq, ATT_WIDTH), lambda b, i: (b * n_q + i, 0))]
    args = [lam_params, subln_g.reshape(1, HEAD_W), q]
    if has_ctx:
        past = ctx_k.shape[1]
        ctx_spec = pl.BlockSpec((None, past, ATT_WIDTH), lambda b, i: (b, 0, 0))
        in_specs += [ctx_spec, ctx_spec]
        args += [ctx_k, ctx_v]
    kv_spec = pl.BlockSpec((n_seq * seq_len, ATT_WIDTH), lambda b, i: (b, 0))
    in_specs += [kv_spec, kv_spec]
    args += [k, v]
    return pl.pallas_call(
        functools.partial(_attn_kernel, has_ctx=has_ctx, lam_init=lam_init,
                          n_seq=n_seq, seq_len=seq_len, tq=tq),
        grid=(n_batch // n_seq, n_q),
        in_specs=in_specs,
        out_specs=pl.BlockSpec((n_seq * tq, ATT_WIDTH), lambda b, i: (b * n_q + i, 0)),
        out_shape=jax.ShapeDtypeStruct((n_batch * seq_len, ATT_WIDTH), BF16),
        compiler_params=_cparams(2),
        name="diff_attention",
    )(*args)


def _cmul(ar, ai, br, bi):
    return ar * br - ai * bi, ar * bi + ai * br


def _ssm_prep_kernel(lre_ref, lim_ref, ls_ref, bre_ref, bim_ref, cre_ref, cim_ref, d_ref,
                     mt_ref, gt_ref, wo_ref, at_ref):
    lane = lax.broadcasted_iota(jnp.int32, (SSM_GROUP, CW), 1)
    chan = lax.broadcasted_iota(jnp.int32, (SSM_GROUP, CW), 0)
    for gi in range(GROUP_BLOCK):
        gt_cols, wo_cols, at_cols, toeplitz = [], [], [], []
        for dr in range(2):
            lr = jnp.minimum(lre_ref[dr, gi], -1e-4)
            li = lim_ref[dr, gi]
            step = jnp.exp(ls_ref[dr, gi])
            mag = jnp.exp(lr * step)
            a_re = mag * jnp.cos(li * step)
            a_im = mag * jnp.sin(li * step)
            den = lr * lr + li * li
            nr = a_re - 1.0
            f_re = (nr * lr + a_im * li) / den
            f_im = (a_im * lr - nr * li) / den
            bt_re, bt_im = bre_ref[dr, gi], bim_ref[dr, gi]
            bb_re, bb_im = _cmul(f_re, f_im, bt_re, bt_im)
            c_re, c_im = cre_ref[dr, gi], cim_ref[dr, gi]
            pw = [(jnp.ones_like(a_re), jnp.zeros_like(a_im))]
            for _ in range(CHUNK):
                pw.append(_cmul(pw[-1][0], pw[-1][1], a_re, a_im))
            g_re, g_im, e_re, e_im = [], [], [], []
            for t in range(CHUNK):
                pr, pi = pw[CHUNK - 1 - t] if dr == 0 else pw[t]
                r, i = _cmul(bb_re, bb_im, pr, pi)
                g_re.append(r)
                g_im.append(i)
                pr, pi = pw[t + 1] if dr == 0 else pw[CHUNK - t]
                r, i = _cmul(c_re, c_im, pr, pi)
                e_re.append(r)
                e_im.append(-i)
            g_cat = jnp.concatenate([jnp.concatenate(g_re, axis=0),
                                     jnp.concatenate(g_im, axis=0)], axis=1)
            gt_cols.append(g_cat)
            wo_cols.append(jnp.concatenate([jnp.concatenate(e_re, axis=0),
                                            jnp.concatenate(e_im, axis=0)], axis=1))
            c_cat = jnp.concatenate([c_re, -c_im], axis=1)
            toeplitz.append(lax.dot_general(c_cat, g_cat, NT_DIMS,
                                            precision=lax.Precision.HIGHEST,
                                            preferred_element_type=F32))
            pr, pi = pw[CHUNK]
            rows_p, rows_q = [], []
            for _ in range(N_LEVELS - 1):
                rows_p.append(jnp.concatenate([pr, pr], axis=1))
                rows_q.append(jnp.concatenate([-pi, pi], axis=1))
                pr, pi = _cmul(pr, pi, pr, pi)
            rows_p.append(jnp.zeros_like(rows_p[0]))
            rows_q.append(jnp.zeros_like(rows_q[0]))
            at_cols += [jnp.concatenate(rows_p, axis=0), jnp.concatenate(rows_q, axis=0)]
        kf_rev, kb = toeplitz
        d_skip = d_ref[gi]
        blocks = []
        for t in range(CHUNK):
            fwd = pltpu.roll(kf_rev, (CW - (CHUNK - 1 - t) * SSM_GROUP) % CW, axis=1)
            bwd = pltpu.roll(kb, t * SSM_GROUP, axis=1)
            blocks.append(jnp.where(lane < (t + 1) * SSM_GROUP, fwd, 0.0)
                          + jnp.where(lane >= t * SSM_GROUP, bwd, 0.0)
                          + jnp.where(lane == chan + t * SSM_GROUP, d_skip, 0.0))
        mt_ref[gi] = jnp.concatenate(blocks, axis=0).astype(mt_ref.dtype)
        gt_ref[gi] = jnp.concatenate(gt_cols, axis=1).astype(gt_ref.dtype)
        wo_ref[gi] = jnp.concatenate(wo_cols, axis=1).astype(wo_ref.dtype)
        at_ref[gi] = jnp.concatenate(at_cols, axis=1)


def _ssm_prep(lam_re, lam_im, log_step, b_re, b_im, c_re, c_im, d_skip):
    row = lambda a: a.reshape(2, N_GROUPS, 1, SSM_STATE)
    bt = lambda a: jnp.swapaxes(a, 2, 3)
    d_col = (d_skip[0] + d_skip[1]).reshape(N_GROUPS, SSM_GROUP, 1)
    gb = GROUP_BLOCK
    vec_spec = pl.BlockSpec((2, gb, 1, SSM_STATE), lambda i: (0, i, 0, 0))
    mat_spec = pl.BlockSpec((2, gb, SSM_GROUP, SSM_STATE), lambda i: (0, i, 0, 0))
    w_spec = pl.BlockSpec((gb, CW, CW), lambda i: (i, 0, 0))
    w_shape = jax.ShapeDtypeStruct((N_GROUPS, CW, CW), BF16)
    return pl.pallas_call(
        _ssm_prep_kernel,
        grid=(N_GROUPS // gb,),
        in_specs=[vec_spec, vec_spec,
                  pl.BlockSpec((2, gb, 1, 1), lambda i: (0, i, 0, 0)),
                  mat_spec, mat_spec, mat_spec, mat_spec,
                  pl.BlockSpec((gb, SSM_GROUP, 1), lambda i: (i, 0, 0))],
        out_specs=[w_spec, w_spec, w_spec,
                   pl.BlockSpec((gb, N_LEVELS, 4 * 2 * SSM_STATE), lambda i: (i, 0, 0))],
        out_shape=[w_shape, w_shape, w_shape,
                   jax.ShapeDtypeStruct((N_GROUPS, N_LEVELS, 4 * 2 * SSM_STATE), F32)],
        compiler_params=_cparams(1),
        name="ssm_prep",
    )(row(lam_re), row(lam_im), log_step.reshape(2, N_GROUPS, 1, 1),
      bt(b_re), bt(b_im), c_re, c_im, d_col)


def _shift_rows(x, m, down):
    n = x.shape[0]
    return pltpu.roll(x, m if down else n - m, axis=0)


def _ssm_kernel(*refs, n_seq, has_h0):
    if has_h0:
        (xt_ref, mt_ref, gt_ref, wo_ref, at_ref, wg_ref, bg_ref, h0_ref,
         z_ref, zs_ref) = refs
    else:
        (xt_ref, mt_ref, gt_ref, wo_ref, at_ref, wg_ref, bg_ref,
         z_ref, st_ref, zs_ref, fin_ref) = refs
    step = pl.program_id(0)
    n_rows = xt_ref.shape[-1]
    seg = n_rows // n_seq
    cw2 = 2 * SSM_STATE
    pos = lax.broadcasted_iota(jnp.int32, (n_rows, cw2), 0) % seg
    for gi in range(GROUP_BLOCK):
        xt = xt_ref[gi]
        at = at_ref[gi]
        s_all = lax.dot_general(xt, gt_ref[gi], TN_DIMS, preferred_element_type=F32)
        h_parts = []
        for dr in range(2):
            s = s_all[:, dr * cw2:(dr + 1) * cw2]
            p_tab = at[:, (2 * dr) * cw2:(2 * dr + 1) * cw2]
            q_tab = at[:, (2 * dr + 1) * cw2:(2 * dr + 2) * cw2]
            down = dr == 0
            if has_h0:
                h0 = jnp.zeros((n_rows, cw2), F32)
                riota = lax.broadcasted_iota(jnp.int32, (n_rows, cw2), 0)
                for b in range(n_seq):
                    edge = b * seg if down else b * seg + seg - 1
                    h0 = jnp.where(riota == edge, h0_ref[gi, b:b + 1, dr * cw2:(dr + 1) * cw2], h0)
                s = s + p_tab[0:1] * h0 + q_tab[0:1] * pltpu.roll(h0, SSM_STATE, axis=1)
            m, lvl = 1, 0
            while m < seg:
                sh = _shift_rows(s, m, down)
                valid = (pos >= m) if down else (pos < seg - m)
                sh = jnp.where(valid, sh, 0.0)
                s = s + p_tab[lvl:lvl + 1] * sh + q_tab[lvl:lvl + 1] * pltpu.roll(sh, SSM_STATE, axis=1)
                m, lvl = 2 * m, lvl + 1
            if not has_h0:
                fin_ref[dr] = s
                st_ref[gi, :, dr * cw2:(dr + 1) * cw2] = (
                    fin_ref[dr, pl.ds(seg - 1 if down else 0, n_seq, stride=seg), :])
            ent = _shift_rows(s, 1, down)
            ent = jnp.where((pos >= 1) if down else (pos < seg - 1), ent, 0.0)
            if has_h0:
                ent = ent + h0
            h_parts.append(ent)
        h_all = jnp.concatenate(h_parts, axis=1).astype(BF16)
        yt = (jnp.dot(mt_ref[gi], xt, preferred_element_type=F32)
              + lax.dot_general(wo_ref[gi], h_all, NT_DIMS, preferred_element_type=F32))
        z = jax.nn.gelu(yt, approximate=True)
        grp = step * GROUP_BLOCK + gi
        for t in range(CHUNK):
            zs_ref[t, pl.ds(pl.multiple_of(grp * SSM_GROUP, SSM_GROUP), SSM_GROUP), :] = (
                z[t * SSM_GROUP:(t + 1) * SSM_GROUP, :])

    @pl.when(step == pl.num_programs(0) - 1)
    def _():
        for t in range(CHUNK):
            zt = zs_ref[t]
            gate = lax.dot_general(wg_ref[...], zt.astype(BF16), TN_DIMS,
                                   preferred_element_type=F32) + bg_ref[...]
            z_ref[t] = (zt * jax.nn.sigmoid(gate)).astype(z_ref.dtype)


def _ssm(xt, mt, gt, wo, at, w_glu, b_glu_col, h0, n_seq):
    n_rows = xt.shape[-1]
    has_h0 = h0 is not None
    gb = GROUP_BLOCK
    w_spec = pl.BlockSpec((gb, CW, CW), lambda i: (i, 0, 0))
    in_specs = [pl.BlockSpec((gb, CW, n_rows), lambda i: (i, 0, 0)),
                w_spec, w_spec, w_spec,
                pl.BlockSpec((gb, N_LEVELS, 4 * 2 * SSM_STATE), lambda i: (i, 0, 0)),
                pl.BlockSpec((SSM_WIDTH, SSM_WIDTH), lambda i: (0, 0)),
                pl.BlockSpec((SSM_WIDTH, 1), lambda i: (0, 0))]
    args = [xt, mt, gt, wo, at, w_glu, b_glu_col]
    out_specs = [pl.BlockSpec((CHUNK, SSM_WIDTH, n_rows), lambda i: (0, 0, 0))]
    out_shape = [jax.ShapeDtypeStruct((CHUNK, SSM_WIDTH, n_rows), BF16)]
    scratch = [pltpu.VMEM((CHUNK, SSM_WIDTH, n_rows), F32)]
    if has_h0:
        in_specs.append(pl.BlockSpec((gb, n_seq, CW), lambda i: (i, 0, 0)))
        args.append(h0)
    else:
        out_specs.append(pl.BlockSpec((gb, n_seq, CW), lambda i: (i, 0, 0)))
        out_shape.append(jax.ShapeDtypeStruct((N_GROUPS, n_seq, CW), F32))
        scratch.append(pltpu.VMEM((2, n_rows, 2 * SSM_STATE), F32))
    return pl.pallas_call(
        functools.partial(_ssm_kernel, n_seq=n_seq, has_h0=has_h0),
        grid=(N_GROUPS // gb,),
        in_specs=in_specs,
        out_specs=out_specs,
        out_shape=out_shape,
        scratch_shapes=scratch,
        compiler_params=_cparams(1),
        name="ssm_scan_glu",
    )(*args)


FF_CHUNK = 256


def _out_ffn_kernel(x_ref, attn_ref, zt_ref, mod_ref, g_ref, wo_ref, wfi_ref, wfo_ref,
                    o_ref, mixz_ref, act_ref):
    j = pl.program_id(1)

    @pl.when(j == 0)
    def _():
        for t in range(CHUNK):
            part = lax.dot_general(zt_ref[t], wo_ref[ATT_WIDTH:, :], TN_DIMS,
                                   preferred_element_type=F32)
            for cb in range(D_MODEL // LANES):
                mixz_ref[cb, pl.ds(t, ROWS_PER_TILE, stride=CHUNK), :] = (
                    part[:, cb * LANES:(cb + 1) * LANES])

    gate1 = mod_ref[:, 2 * D_MODEL:3 * D_MODEL]
    shift2 = mod_ref[:, 3 * D_MODEL:4 * D_MODEL]
    scale2 = mod_ref[:, 4 * D_MODEL:5 * D_MODEL]
    gate2 = mod_ref[:, 5 * D_MODEL:6 * D_MODEL]
    rows = pl.ds(pl.multiple_of(j * SUB_TOK, SUB_TOK), SUB_TOK)
    mix = (jnp.dot(attn_ref[...], wo_ref[0:ATT_WIDTH, :], preferred_element_type=F32)
           + jnp.concatenate([mixz_ref[cb, rows, :] for cb in range(D_MODEL // LANES)], axis=1))
    x1 = x_ref[...] + gate1 * _rms(mix, g_ref[1:2, :])
    h = (_rms(x1, g_ref[2:3, :]) * (1.0 + scale2) + shift2).astype(BF16)
    for c in range(D_FF // FF_CHUNK):
        lo = c * FF_CHUNK
        gt = jnp.dot(h, wfi_ref[:, lo:lo + FF_CHUNK], preferred_element_type=F32)
        up = jnp.dot(h, wfi_ref[:, D_FF + lo:D_FF + lo + FF_CHUNK], preferred_element_type=F32)
        act_ref[:, lo:lo + FF_CHUNK] = (_silu(gt) * up).astype(BF16)
    f = jnp.dot(act_ref[...], wfo_ref[...], preferred_element_type=F32)
    o_ref[...] = x1 + gate2 * _rms(f, g_ref[3:4, :])


def _out_ffn(x2d, attn, zt, mod3, norm_g, w_o, w_ffn_in, w_ffn_out):
    n_tok = x2d.shape[0]
    n_tiles = n_tok // TOK_PER_TILE
    tiles_per_mod = n_tiles // mod3.shape[0]
    const = lambda i, j: (0, 0)
    row_spec = lambda w: pl.BlockSpec((SUB_TOK, w), lambda i, j: (i * N_SUB + j, 0))
    return pl.pallas_call(
        _out_ffn_kernel,
        grid=(n_tiles, N_SUB),
        in_specs=[row_spec(D_MODEL), row_spec(ATT_WIDTH),
                  pl.BlockSpec((CHUNK, SSM_WIDTH, ROWS_PER_TILE), lambda i, j: (0, 0, i)),
                  pl.BlockSpec((None, 1, N_MOD * D_MODEL), lambda i, j: (i // tiles_per_mod, 0, 0)),
                  pl.BlockSpec((4, D_MODEL), const),
                  pl.BlockSpec((2 * ATT_WIDTH, D_MODEL), const, pipeline_mode=pl.Buffered(1)),
                  pl.BlockSpec((D_MODEL, 2 * D_FF), const, pipeline_mode=pl.Buffered(1)),
                  pl.BlockSpec((D_FF, D_MODEL), const, pipeline_mode=pl.Buffered(1))],
        out_specs=row_spec(D_MODEL),
        out_shape=jax.ShapeDtypeStruct((n_tok, D_MODEL), F32),
        scratch_shapes=[pltpu.VMEM((D_MODEL // LANES, TOK_PER_TILE, LANES), F32),
                        pltpu.VMEM((SUB_TOK, D_FF), BF16)],
        compiler_params=_cparams(2),
        name="out_proj_ffn",
    )(x2d, attn, zt, mod3, norm_g, w_o, w_ffn_in, w_ffn_out)


def _rope_tables(seq_len):
    t = jnp.arange(seq_len)
    row = (t // GRID_W).astype(F32)
    col = (t % GRID_W).astype(F32)
    half = HEAD_DIM // 2
    inv_freq = ROPE_BASE ** (-jnp.arange(0, half, 2, dtype=F32) / half)
    ang_r = row[:, None] * inv_freq
    ang_c = col[:, None] * inv_freq
    ang = jnp.concatenate([ang_r, ang_r, ang_c, ang_c], axis=-1)
    cos, sin = jnp.cos(ang), jnp.sin(ang)
    upper = (jnp.arange(HEAD_DIM) % 32) < 16
    sa = jnp.where(upper, -sin, 0.0)
    sb = jnp.where(upper, 0.0, sin)
    two = lambda a: jnp.concatenate([a, a], axis=-1)
    return two(cos), two(sa), two(sb)


def _layer(x, mod_rows, lam_init, rope_tabs, ctx_k, ctx_v, h0, weights, prep):
    n_batch, seq_len = x.shape[:2]
    mod3 = mod_rows[:, None, :]
    g = weights['norm_g']
    outs = _in_proj(x, mod3[:, :, 0:2 * D_MODEL], g[0:1], weights['w_in'], rope_tabs)
    q, k, v = outs[:3]
    attn = _attention(q, k, v, ctx_k, ctx_v, weights['lam'], weights['subln_g'],
                      n_batch, seq_len, lam_init)
    ssm_out = _ssm(outs[-1], *prep, weights['w_glu'], weights['b_glu_col'], h0, n_batch)
    y = _out_ffn(x.reshape(n_batch * seq_len, D_MODEL), attn, ssm_out[0], mod3, g,
                 weights['w_o'], weights['w_ffn_in'], weights['w_ffn_out'])
    return y.reshape(x.shape), outs[3:-1], ssm_out[1:]


def kernel(x_prompt, x_sample, cache_k, cache_v, state_ssm_re, state_ssm_im, c, c_ctx, w_mod, b_mod, norm_g, w_in, lam_params, subln_g, ssm_lambda_re, ssm_lambda_im, ssm_log_step, ssm_b_re, ssm_b_im, ssm_c_re, ssm_c_im, ssm_d, w_glu, b_glu, w_o, w_ffn_in, w_ffn_out):
    depth = w_mod.shape[0]
    assert depth == 1
    bp = x_prompt.shape[0]
    bd, ld_len = x_sample.shape[:2]
    past = cache_k.shape[2]
    xp, xs = x_prompt, x_sample
    cond = jnp.zeros((8, D_MODEL), F32).at[0].set(c_ctx).at[1:1 + bd].set(c)
    rope_tabs = _rope_tables(ld_len)
    ks_out, vs_out, hr_out, hi_out = [], [], [], []
    for l in range(depth):
        lam_init = 0.8 - 0.6 * math.exp(-0.3 * l)
        mods = _modulation(cond, w_mod[l], b_mod[l])
        weights = {
            'norm_g': norm_g[l],
            'w_in': w_in[l].astype(BF16),
            'lam': lam_params[l], 'subln_g': subln_g[l],
            'w_glu': w_glu[l].astype(BF16), 'b_glu_col': b_glu[l].reshape(SSM_WIDTH, 1),
            'w_o': w_o[l].astype(BF16),
            'w_ffn_in': w_ffn_in[l].astype(BF16), 'w_ffn_out': w_ffn_out[l].astype(BF16),
        }
        prep = _ssm_prep(ssm_lambda_re[l], ssm_lambda_im[l], ssm_log_step[l],
                         ssm_b_re[l], ssm_b_im[l], ssm_c_re[l], ssm_c_im[l], ssm_d[l])
        xp, (k_ctx, v_ctx), (st,) = _layer(xp, mods[0:1], lam_init, None, None, None, None,
                                           weights, prep)
        ks_out.append(k_ctx)
        vs_out.append(v_ctx)
        fin = st.reshape(N_GROUPS, bp, 2, 2, SSM_STATE).transpose(1, 2, 3, 0, 4)
        hr_out.append(fin[:, :, 0])
        hi_out.append(fin[:, :, 1])
        ck = cache_k[:, l].reshape(bd, past, ATT_WIDTH)
        cv = cache_v[:, l].reshape(bd, past, ATT_WIDTH)
        h0 = jnp.stack([state_ssm_re[:, l], state_ssm_im[:, l]], axis=2)
        h0 = h0.transpose(3, 0, 1, 2, 4).reshape(N_GROUPS, bd, CW)
        xs, _, _ = _layer(xs, mods[1:1 + bd], lam_init, rope_tabs, ck, cv, h0, weights, prep)
    return (xp, xs, jnp.stack(ks_out, axis=1), jnp.stack(vs_out, axis=1),
            jnp.stack(hr_out, axis=1), jnp.stack(hi_out, axis=1))
```

```python
import functools
import math

import jax
import jax.numpy as jnp
import numpy as np
from jax import lax
from jax.experimental import pallas as pl
from jax.experimental.pallas import tpu as pltpu

F32 = jnp.float32
BF16 = jnp.bfloat16

D_MODEL = 1024
GRID_W = 64
ATT_WIDTH = 512
SSM_WIDTH = 512
HEAD_DIM = 64
N_HEADS = 4
HEAD_W = 2 * HEAD_DIM
SSM_GROUP = 16
N_GROUPS = 32
SSM_STATE = 64
D_FF = 2816
N_MOD = 6
ROPE_BASE = 10000.0
NORM_EPS = 1e-6

LANES = 128
N_COL_BLOCKS = D_MODEL // LANES
CHUNK = 16
CW = CHUNK * SSM_GROUP
N_LEVELS = 8
GROUP_BLOCK = 4

ROWS_PER_TILE = 128
TOK_PER_TILE = ROWS_PER_TILE * CHUNK
SUB_TOK = 512
N_SUB = TOK_PER_TILE // SUB_TOK

VMEM_LIMIT = 56 * 1024 * 1024

NT_DIMS = (((1,), (1,)), ((), ()))
TN_DIMS = (((0,), (0,)), ((), ()))


def _cparams(n_axes):
    return pltpu.CompilerParams(
        dimension_semantics=("arbitrary",) * n_axes,
        vmem_limit_bytes=VMEM_LIMIT)


def _rms(x, g):
    ms = jnp.mean(x * x, axis=-1, keepdims=True)
    return x * lax.rsqrt(ms + NORM_EPS) * g


def _silu(x):
    return x * jax.nn.sigmoid(x)


def _mod_kernel(cond_ref, w_ref, b_ref, o_ref):
    s = _silu(cond_ref[...]).astype(BF16)
    o_ref[...] = jnp.dot(s, w_ref[...].astype(BF16),
                         preferred_element_type=F32) + b_ref[...]


def _modulation(cond, w_mod, b_mod):
    n = w_mod.shape[1]
    tn = 1536
    return pl.pallas_call(
        _mod_kernel,
        grid=(n // tn,),
        in_specs=[pl.BlockSpec((8, D_MODEL), lambda i: (0, 0)),
                  pl.BlockSpec((D_MODEL, tn), lambda i: (0, i)),
                  pl.BlockSpec((1, tn), lambda i: (0, i))],
        out_specs=pl.BlockSpec((8, tn), lambda i: (0, i)),
        out_shape=jax.ShapeDtypeStruct((8, n), F32),
        compiler_params=_cparams(1),
        name="modulation",
    )(cond, w_mod, b_mod.reshape(1, n))


def _rope(x, cos, sa, sb):
    return (x * cos + pltpu.roll(x, HEAD_W - 16, axis=1) * sa
            + pltpu.roll(x, 16, axis=1) * sb)


def _in_proj_kernel(*refs, rope, seqs_per_sub):
    x_ref, x3_hbm, mod_ref, g_ref, w_ref = refs[:5]
    refs = refs[5:]
    if rope:
        cos_ref, sa_ref, sb_ref = refs[:3]
        refs = refs[3:]
        q_ref, k_ref, v_ref, ut_ref, wut_ref, xt_ref, xt_sem = refs
    else:
        q_ref, k_ref, v_ref, kc_ref, vc_ref, ut_ref, wut_ref, xt_ref, xt_sem = refs
    tile = pl.program_id(0)
    j = pl.program_id(1)
    t_per_sub = CHUNK // N_SUB
    t_early = CHUNK - t_per_sub

    def gather(tile_idx, t):
        src = x3_hbm.at[pl.ds(tile_idx * ROWS_PER_TILE, ROWS_PER_TILE), t, :]
        return pltpu.make_async_copy(src, xt_ref.at[t], xt_sem.at[t])

    @pl.when(j == 0)
    def _():
        @pl.when(tile == 0)
        def _():
            for t in range(t_early):
                gather(0, t).start()
        for t in range(t_early, CHUNK):
            gather(tile, t).start()

    t_base = j * t_per_sub
    for d in range(t_per_sub):
        gather(tile, t_base + d).wait()

    @pl.when((j == N_SUB - 1) & (tile + 1 < pl.num_programs(0)))
    def _():
        for t in range(t_early):
            gather(tile + 1, t).start()

    @pl.when((tile == 0) & (j == 0))
    def _():
        wut_ref[...] = w_ref[:, 3 * ATT_WIDTH:].T

    shift = mod_ref[:, 0:D_MODEL]
    gain = g_ref[...] * (1.0 + mod_ref[:, D_MODEL:2 * D_MODEL])

    def norm_mod(xv):
        ms = jnp.mean(xv * xv, axis=-1, keepdims=True)
        return (xv * lax.rsqrt(ms + NORM_EPS) * gain + shift).astype(BF16)

    tok = pl.ds(pl.multiple_of(j * SUB_TOK, SUB_TOK), SUB_TOK)
    proj = jnp.dot(norm_mod(x_ref[tok, :]), w_ref[:, 0:3 * ATT_WIDTH],
                   preferred_element_type=F32)
    q = proj[:, 0:ATT_WIDTH]
    k = proj[:, ATT_WIDTH:2 * ATT_WIDTH]
    v = proj[:, 2 * ATT_WIDTH:3 * ATT_WIDTH]
    qscale = HEAD_DIM ** -0.5 * math.log2(math.e)
    if rope:
        cos, sa, sb = cos_ref[...], sa_ref[...], sb_ref[...]
        for hd in range(N_HEADS):
            sl = slice(hd * HEAD_W, (hd + 1) * HEAD_W)
            q_ref[:, sl] = (_rope(q[:, sl], cos, sa, sb) * qscale).astype(q_ref.dtype)
            k_ref[:, sl] = _rope(k[:, sl], cos, sa, sb).astype(k_ref.dtype)
    else:
        q_ref[...] = (q * qscale).astype(q_ref.dtype)
        k_ref[...] = k.astype(k_ref.dtype)
        seq = SUB_TOK // seqs_per_sub
        for b in range(seqs_per_sub):
            for m in range(2 * N_HEADS):
                kc_ref[b, :, m, :] = k[b * seq:(b + 1) * seq, m * HEAD_DIM:(m + 1) * HEAD_DIM]
            for hd in range(N_HEADS):
                vc_ref[b, :, hd, :] = v[b * seq:(b + 1) * seq, hd * HEAD_W:(hd + 1) * HEAD_W]
    v_ref[...] = v.astype(v_ref.dtype)

    for d0 in range(0, t_per_sub, 2):
        xt = jnp.concatenate([xt_ref[t_base + d0], xt_ref[t_base + d0 + 1]], axis=0)
        ut = lax.dot_general(wut_ref[...], norm_mod(xt), NT_DIMS,
                             preferred_element_type=F32)
        for d in range(2):
            blk = ut[:, d * ROWS_PER_TILE:(d + 1) * ROWS_PER_TILE]
            row0 = pl.multiple_of((t_base + d0 + d) * SSM_GROUP, SSM_GROUP)
            ut_ref[:, pl.ds(row0, SSM_GROUP), :] = (
                blk.reshape(N_GROUPS, SSM_GROUP, ROWS_PER_TILE).astype(ut_ref.dtype))


def _in_proj(x, mod3, g0, w_in, rope_tabs):
    n_batch, seq_len = x.shape[:2]
    n_tok = n_batch * seq_len
    n_rows = n_tok // CHUNK
    n_tiles = n_tok // TOK_PER_TILE
    tiles_per_mod = n_tiles // mod3.shape[0]
    rope = rope_tabs is not None
    seqs_per_sub = max(1, SUB_TOK // seq_len)
    in_specs = [pl.BlockSpec((TOK_PER_TILE, D_MODEL), lambda i, j: (i, 0)),
                pl.BlockSpec(memory_space=pl.ANY),
                pl.BlockSpec((None, 1, 2 * D_MODEL), lambda i, j: (i // tiles_per_mod, 0, 0)),
                pl.BlockSpec((1, D_MODEL), lambda i, j: (0, 0)),
                pl.BlockSpec((D_MODEL, 4 * ATT_WIDTH), lambda i, j: (0, 0))]
    args = [x.reshape(n_tok, D_MODEL), x.reshape(n_rows, CHUNK, D_MODEL), mod3, g0, w_in]
    row_spec = pl.BlockSpec((SUB_TOK, ATT_WIDTH), lambda i, j: (i * N_SUB + j, 0))
    row_shape = jax.ShapeDtypeStruct((n_tok, ATT_WIDTH), BF16)
    out_specs = [row_spec, row_spec, row_spec]
    out_shape = [row_shape, row_shape, row_shape]
    if rope:
        assert seq_len == TOK_PER_TILE
        for tab in rope_tabs:
            in_specs.append(pl.BlockSpec((SUB_TOK, HEAD_W), lambda i, j: (j, 0)))
            args.append(tab)
    else:
        out_specs += [pl.BlockSpec((seqs_per_sub, seq_len, 2 * N_HEADS, HEAD_DIM),
                                   lambda i, j: (i * N_SUB + j, 0, 0, 0)),
                      pl.BlockSpec((seqs_per_sub, seq_len, N_HEADS, HEAD_W),
                                   lambda i, j: (i * N_SUB + j, 0, 0, 0))]
        out_shape += [jax.ShapeDtypeStruct((n_batch, seq_len, 2 * N_HEADS, HEAD_DIM), F32),
                      jax.ShapeDtypeStruct((n_batch, seq_len, N_HEADS, HEAD_W), F32)]
    out_specs.append(pl.BlockSpec((N_GROUPS, CW, ROWS_PER_TILE), lambda i, j: (0, 0, i)))
    out_shape.append(jax.ShapeDtypeStruct((N_GROUPS, CW, n_rows), BF16))
    return pl.pallas_call(
        functools.partial(_in_proj_kernel, rope=rope, seqs_per_sub=seqs_per_sub),
        grid=(n_tiles, N_SUB),
        in_specs=in_specs,
        out_specs=out_specs,
        out_shape=out_shape,
        scratch_shapes=[pltpu.VMEM((SSM_WIDTH, D_MODEL), BF16),
                        pltpu.VMEM((CHUNK, ROWS_PER_TILE, D_MODEL), F32),
                        pltpu.SemaphoreType.DMA((CHUNK,))],
        compiler_params=_cparams(2),
        name="in_proj",
    )(*args)


def _attn_kernel(*refs, has_ctx, lam_init, n_seq, seq_len, tq):
    if has_ctx:
        lam_ref, sg_ref, q_ref, ck_ref, cv_ref, k_ref, v_ref, o_ref = refs
    else:
        lam_ref, sg_ref, q_ref, k_ref, v_ref, o_ref = refs
    lp = lam_ref[...]
    lam = (jnp.exp(jnp.sum(lp[0:1] * lp[1:2], axis=-1, keepdims=True))
           - jnp.exp(jnp.sum(lp[2:3] * lp[3:4], axis=-1, keepdims=True)) + lam_init)
    first_map = lax.broadcasted_iota(jnp.int32, (1, HEAD_W), 1) < HEAD_DIM
    for b in range(n_seq):
        for hd in range(N_HEADS):
            sl = slice(hd * HEAD_W, (hd + 1) * HEAD_W)
            qh = q_ref[b * tq:(b + 1) * tq, sl]
            zero = jnp.zeros_like(qh)
            qs = jnp.concatenate([jnp.where(first_map, qh, zero),
                                  jnp.where(first_map, zero, qh)], axis=0)
            kv_rows = slice(b * seq_len, (b + 1) * seq_len)
            parts = [(k_ref[kv_rows, sl], v_ref[kv_rows, sl])]
            if has_ctx:
                parts.insert(0, (ck_ref[:, sl].astype(BF16), cv_ref[:, sl].astype(BF16)))
            scores = [lax.dot_general(qs, kk, NT_DIMS, preferred_element_type=F32)
                      for kk, _ in parts]
            mx = scores[0].max(axis=-1, keepdims=True)
            for s in scores[1:]:
                mx = jnp.maximum(mx, s.max(axis=-1, keepdims=True))
            acc = None
            for s, (_, vv) in zip(scores, parts):
                e = jnp.exp2(s - mx).astype(BF16)
                v_one = jnp.concatenate([vv, jnp.ones_like(vv)], axis=1)
                pv = jnp.dot(e, v_one, preferred_element_type=F32)
                acc = pv if acc is None else acc + pv
            num = acc[:, 0:HEAD_W] / acc[:, HEAD_W:2 * HEAD_W]
            o = num[0:tq] - lam * num[tq:2 * tq]
            o = _rms(o, sg_ref[...]) * (1.0 - lam_init)
            o_ref[b * tq:(b + 1) * tq, sl] = o.astype(o_ref.dtype)


def _attention(q, k, v, ctx_k, ctx_v, lam_params, subln_g, n_batch, seq_len, lam_init):
    has_ctx = ctx_k is not None
    tq = 256
    n_q = seq_len // tq
    n_seq = 1 if n_q > 1 else min(4, n_batch)
    in_specs = [pl.BlockSpec((4, HEAD_DIM), lambda b, i: (0, 0)),
                pl.BlockSpec((1, HEAD_W), lambda b, i: (0, 0)),
                pl.BlockSpec((n_seq * tq, ATT_WIDTH), lambda b, i: (b * n_q + i, 0))]
    args = [lam_params, subln_g.reshape(1, HEAD_W), q]
    if has_ctx:
        past = ctx_k.shape[1]
        ctx_spec = pl.BlockSpec((None, past, ATT_WIDTH), lambda b, i: (b, 0, 0))
        in_specs += [ctx_spec, ctx_spec]
        args += [ctx_k, ctx_v]
    kv_spec = pl.BlockSpec((n_seq * seq_len, ATT_WIDTH), lambda b, i: (b, 0))
    in_specs += [kv_spec, kv_spec]
    args += [k, v]
    return pl.pallas_call(
        functools.partial(_attn_kernel, has_ctx=has_ctx, lam_init=lam_init,
                          n_seq=n_seq, seq_len=seq_len, tq=tq),
        grid=(n_batch // n_seq, n_q),
        in_specs=in_specs,
        out_specs=pl.BlockSpec((n_seq * tq, ATT_WIDTH), lambda b, i: (b * n_q + i, 0)),
        out_shape=jax.ShapeDtypeStruct((n_batch * seq_len, ATT_WIDTH), BF16),
        compiler_params=_cparams(2),
        name="diff_attention",
    )(*args)


def _cmul(ar, ai, br, bi):
    return ar * br - ai * bi, ar * bi + ai * br


def _ssm_prep_kernel(lre_ref, lim_ref, ls_ref, bre_ref, bim_ref, cre_ref, cim_ref, d_ref,
                     mt_ref, gt_ref, wo_ref, at_ref):
    lane = lax.broadcasted_iota(jnp.int32, (SSM_GROUP, CW), 1)
    chan = lax.broadcasted_iota(jnp.int32, (SSM_GROUP, CW), 0)
    for gi in range(GROUP_BLOCK):
        gt_cols, wo_cols, at_cols, toeplitz = [], [], [], []
        for dr in range(2):
            lr = jnp.minimum(lre_ref[dr, gi], -1e-4)
            li = lim_ref[dr, gi]
            step = jnp.exp(ls_ref[dr, gi])
            mag = jnp.exp(lr * step)
            a_re = mag * jnp.cos(li * step)
            a_im = mag * jnp.sin(li * step)
            den = lr * lr + li * li
            nr = a_re - 1.0
            f_re = (nr * lr + a_im * li) / den
            f_im = (a_im * lr - nr * li) / den
            bt_re, bt_im = bre_ref[dr, gi], bim_ref[dr, gi]
            bb_re, bb_im = _cmul(f_re, f_im, bt_re, bt_im)
            c_re, c_im = cre_ref[dr, gi], cim_ref[dr, gi]
            pw = [(jnp.ones_like(a_re), jnp.zeros_like(a_im))]
            for _ in range(CHUNK):
                pw.append(_cmul(pw[-1][0], pw[-1][1], a_re, a_im))
            g_re, g_im, e_re, e_im = [], [], [], []
            for t in range(CHUNK):
                pr, pi = pw[CHUNK - 1 - t] if dr == 0 else pw[t]
                r, i = _cmul(bb_re, bb_im, pr, pi)
                g_re.append(r)
                g_im.append(i)
                pr, pi = pw[t + 1] if dr == 0 else pw[CHUNK - t]
                r, i = _cmul(c_re, c_im, pr, pi)
                e_re.append(r)
                e_im.append(-i)
            g_cat = jnp.concatenate([jnp.concatenate(g_re, axis=0),
                                     jnp.concatenate(g_im, axis=0)], axis=1)
            gt_cols.append(g_cat)
            wo_cols.append(jnp.concatenate([jnp.concatenate(e_re, axis=0),
                                            jnp.concatenate(e_im, axis=0)], axis=1))
            c_cat = jnp.concatenate([c_re, -c_im], axis=1)
            toeplitz.append(lax.dot_general(c_cat, g_cat, NT_DIMS,
                                            precision=lax.Precision.HIGHEST,
                                            preferred_element_type=F32))
            pr, pi = pw[CHUNK]
            rows_p, rows_q = [], []
            for _ in range(N_LEVELS - 1):
                rows_p.append(jnp.concatenate([pr, pr], axis=1))
                rows_q.append(jnp.concatenate([-pi, pi], axis=1))
                pr, pi = _cmul(pr, pi, pr, pi)
            rows_p.append(jnp.zeros_like(rows_p[0]))
            rows_q.append(jnp.zeros_like(rows_q[0]))
            at_cols += [jnp.concatenate(rows_p, axis=0), jnp.concatenate(rows_q, axis=0)]
        kf_rev, kb = toeplitz
        d_skip = d_ref[gi]
        blocks = []
        for t in range(CHUNK):
            fwd = pltpu.roll(kf_rev, (CW - (CHUNK - 1 - t) * SSM_GROUP) % CW, axis=1)
            bwd = pltpu.roll(kb, t * SSM_GROUP, axis=1)
            blocks.append(jnp.where(lane < (t + 1) * SSM_GROUP, fwd, 0.0)
                          + jnp.where(lane >= t * SSM_GROUP, bwd, 0.0)
                          + jnp.where(lane == chan + t * SSM_GROUP, d_skip, 0.0))
        mt_ref[gi] = jnp.concatenate(blocks, axis=0).astype(mt_ref.dtype)
        gt_ref[gi] = jnp.concatenate(gt_cols, axis=1).astype(gt_ref.dtype)
        wo_ref[gi] = jnp.concatenate(wo_cols, axis=1).astype(wo_ref.dtype)
        at_ref[gi] = jnp.concatenate(at_cols, axis=1)


def _ssm_prep(lam_re, lam_im, log_step, b_re, b_im, c_re, c_im, d_skip):
    row = lambda a: a.reshape(2, N_GROUPS, 1, SSM_STATE)
    bt = lambda a: jnp.swapaxes(a, 2, 3)
    d_col = (d_skip[0] + d_skip[1]).reshape(N_GROUPS, SSM_GROUP, 1)
    gb = GROUP_BLOCK
    vec_spec = pl.BlockSpec((2, gb, 1, SSM_STATE), lambda i: (0, i, 0, 0))
    mat_spec = pl.BlockSpec((2, gb, SSM_GROUP, SSM_STATE), lambda i: (0, i, 0, 0))
    w_spec = pl.BlockSpec((gb, CW, CW), lambda i: (i, 0, 0))
    w_shape = jax.ShapeDtypeStruct((N_GROUPS, CW, CW), BF16)
    return pl.pallas_call(
        _ssm_prep_kernel,
        grid=(N_GROUPS // gb,),
        in_specs=[vec_spec, vec_spec,
                  pl.BlockSpec((2, gb, 1, 1), lambda i: (0, i, 0, 0)),
                  mat_spec, mat_spec, mat_spec, mat_spec,
                  pl.BlockSpec((gb, SSM_GROUP, 1), lambda i: (i, 0, 0))],
        out_specs=[w_spec, w_spec, w_spec,
                   pl.BlockSpec((gb, N_LEVELS, 4 * 2 * SSM_STATE), lambda i: (i, 0, 0))],
        out_shape=[w_shape, w_shape, w_shape,
                   jax.ShapeDtypeStruct((N_GROUPS, N_LEVELS, 4 * 2 * SSM_STATE), F32)],
        compiler_params=_cparams(1),
        name="ssm_prep",
    )(row(lam_re), row(lam_im), log_step.reshape(2, N_GROUPS, 1, 1),
      bt(b_re), bt(b_im), c_re, c_im, d_col)


def _shift_rows(x, m, down):
    n = x.shape[0]
    return pltpu.roll(x, m if down else n - m, axis=0)


def _ssm_kernel(*refs, n_seq, has_h0):
    if has_h0:
        (xt_ref, mt_ref, gt_ref, wo_ref, at_ref, wg_ref, bg_ref, h0_ref,
         z_ref, zs_ref) = refs
    else:
        (xt_ref, mt_ref, gt_ref, wo_ref, at_ref, wg_ref, bg_ref,
         z_ref, st_ref, zs_ref, fin_ref) = refs
    step = pl.program_id(0)
    n_rows = xt_ref.shape[-1]
    seg = n_rows // n_seq
    cw2 = 2 * SSM_STATE
    pos = lax.broadcasted_iota(jnp.int32, (n_rows, cw2), 0) % seg
    for gi in range(GROUP_BLOCK):
        xt = xt_ref[gi]
        at = at_ref[gi]
        s_all = lax.dot_general(xt, gt_ref[gi], TN_DIMS, preferred_element_type=F32)
        h_parts = []
        for dr in range(2):
            s = s_all[:, dr * cw2:(dr + 1) * cw2]
            p_tab = at[:, (2 * dr) * cw2:(2 * dr + 1) * cw2]
            q_tab = at[:, (2 * dr + 1) * cw2:(2 * dr + 2) * cw2]
            down = dr == 0
            if has_h0:
                h0 = jnp.zeros((n_rows, cw2), F32)
                riota = lax.broadcasted_iota(jnp.int32, (n_rows, cw2), 0)
                for b in range(n_seq):
                    edge = b * seg if down else b * seg + seg - 1
                    h0 = jnp.where(riota == edge, h0_ref[gi, b:b + 1, dr * cw2:(dr + 1) * cw2], h0)
                s = s + p_tab[0:1] * h0 + q_tab[0:1] * pltpu.roll(h0, SSM_STATE, axis=1)
            m, lvl = 1, 0
            while m < seg:
                sh = _shift_rows(s, m, down)
                valid = (pos >= m) if down else (pos < seg - m)
                sh = jnp.where(valid, sh, 0.0)
                s = s + p_tab[lvl:lvl + 1] * sh + q_tab[lvl:lvl + 1] * pltpu.roll(sh, SSM_STATE, axis=1)
                m, lvl = 2 * m, lvl + 1
            if not has_h0:
                fin_ref[dr] = s
                st_ref[gi, :, dr * cw2:(dr + 1) * cw2] = (
                    fin_ref[dr, pl.ds(seg - 1 if down else 0, n_seq, stride=seg), :])
            ent = _shift_rows(s, 1, down)
            ent = jnp.where((pos >= 1) if down else (pos < seg - 1), ent, 0.0)
            if has_h0:
                ent = ent + h0
            h_parts.append(ent)
        h_all = jnp.concatenate(h_parts, axis=1).astype(BF16)
        yt = (jnp.dot(mt_ref[gi], xt, preferred_element_type=F32)
              + lax.dot_general(wo_ref[gi], h_all, NT_DIMS, preferred_element_type=F32))
        z = jax.nn.gelu(yt, approximate=True)
        grp = step * GROUP_BLOCK + gi
        for t in range(CHUNK):
            zs_ref[t, pl.ds(pl.multiple_of(grp * SSM_GROUP, SSM_GROUP), SSM_GROUP), :] = (
                z[t * SSM_GROUP:(t + 1) * SSM_GROUP, :])

    @pl.when(step == pl.num_programs(0) - 1)
    def _():
        for t in range(CHUNK):
            zt = zs_ref[t]
            gate = jnp.dot(wg_ref[...], zt.astype(BF16),
                           preferred_element_type=F32) + bg_ref[...]
            z_ref[t] = (zt * jax.nn.sigmoid(gate)).astype(z_ref.dtype)


def _ssm(xt, mt, gt, wo, at, w_glu_t, b_glu_col, h0, n_seq):
    n_rows = xt.shape[-1]
    has_h0 = h0 is not None
    gb = GROUP_BLOCK
    w_spec = pl.BlockSpec((gb, CW, CW), lambda i: (i, 0, 0))
    in_specs = [pl.BlockSpec((gb, CW, n_rows), lambda i: (i, 0, 0)),
                w_spec, w_spec, w_spec,
                pl.BlockSpec((gb, N_LEVELS, 4 * 2 * SSM_STATE), lambda i: (i, 0, 0)),
                pl.BlockSpec((SSM_WIDTH, SSM_WIDTH), lambda i: (0, 0)),
                pl.BlockSpec((SSM_WIDTH, 1), lambda i: (0, 0))]
    args = [xt, mt, gt, wo, at, w_glu_t, b_glu_col]
    out_specs = [pl.BlockSpec((CHUNK, SSM_WIDTH, n_rows), lambda i: (0, 0, 0))]
    out_shape = [jax.ShapeDtypeStruct((CHUNK, SSM_WIDTH, n_rows), BF16)]
    scratch = [pltpu.VMEM((CHUNK, SSM_WIDTH, n_rows), F32)]
    if has_h0:
        in_specs.append(pl.BlockSpec((gb, n_seq, CW), lambda i: (i, 0, 0)))
        args.append(h0)
    else:
        out_specs.append(pl.BlockSpec((gb, n_seq, CW), lambda i: (i, 0, 0)))
        out_shape.append(jax.ShapeDtypeStruct((N_GROUPS, n_seq, CW), F32))
        scratch.append(pltpu.VMEM((2, n_rows, 2 * SSM_STATE), F32))
    return pl.pallas_call(
        functools.partial(_ssm_kernel, n_seq=n_seq, has_h0=has_h0),
        grid=(N_GROUPS // gb,),
        in_specs=in_specs,
        out_specs=out_specs,
        out_shape=out_shape,
        scratch_shapes=scratch,
        compiler_params=_cparams(1),
        name="ssm_scan_glu",
    )(*args)


FF_CHUNK = 256


def _out_ffn_kernel(x_ref, attn_ref, zt_ref, mod_ref, g_ref, wo_ref, wfi_ref, wfo_ref,
                    o_ref, mixz_ref, act_ref):
    j = pl.program_id(1)

    @pl.when(j == 0)
    def _():
        for t in range(CHUNK):
            part = lax.dot_general(zt_ref[t], wo_ref[ATT_WIDTH:, :], TN_DIMS,
                                   preferred_element_type=F32)
            for cb in range(N_COL_BLOCKS):
                mixz_ref[cb, pl.ds(t, ROWS_PER_TILE, stride=CHUNK), :] = (
                    part[:, cb * LANES:(cb + 1) * LANES])

    gate1 = mod_ref[:, 2 * D_MODEL:3 * D_MODEL]
    shift2 = mod_ref[:, 3 * D_MODEL:4 * D_MODEL]
    scale2 = mod_ref[:, 4 * D_MODEL:5 * D_MODEL]
    gate2 = mod_ref[:, 5 * D_MODEL:6 * D_MODEL]
    rows = pl.ds(pl.multiple_of(j * SUB_TOK, SUB_TOK), SUB_TOK)
    mix = (jnp.dot(attn_ref[...], wo_ref[0:ATT_WIDTH, :], preferred_element_type=F32)
           + jnp.concatenate([mixz_ref[cb, rows, :] for cb in range(N_COL_BLOCKS)], axis=1))
    x1 = x_ref[...] + gate1 * _rms(mix, g_ref[1:2, :])
    h = (_rms(x1, g_ref[2:3, :]) * (1.0 + scale2) + shift2).astype(BF16)
    for c in range(D_FF // FF_CHUNK):
        lo = c * FF_CHUNK
        gt = jnp.dot(h, wfi_ref[:, lo:lo + FF_CHUNK], preferred_element_type=F32)
        up = jnp.dot(h, wfi_ref[:, D_FF + lo:D_FF + lo + FF_CHUNK], preferred_element_type=F32)
        act_ref[:, lo:lo + FF_CHUNK] = (_silu(gt) * up).astype(BF16)
    f = jnp.dot(act_ref[...], wfo_ref[...], preferred_element_type=F32)
    o_ref[...] = x1 + gate2 * _rms(f, g_ref[3:4, :])


def _out_ffn(x2d, attn, zt, mod3, norm_g, w_o, w_ffn_in, w_ffn_out):
    n_tok = x2d.shape[0]
    n_tiles = n_tok // TOK_PER_TILE
    tiles_per_mod = n_tiles // mod3.shape[0]
    const = lambda i, j: (0, 0)
    row_spec = lambda w: pl.BlockSpec((SUB_TOK, w), lambda i, j: (i * N_SUB + j, 0))
    return pl.pallas_call(
        _out_ffn_kernel,
        grid=(n_tiles, N_SUB),
        in_specs=[row_spec(D_MODEL), row_spec(ATT_WIDTH),
                  pl.BlockSpec((CHUNK, SSM_WIDTH, ROWS_PER_TILE), lambda i, j: (0, 0, i)),
                  pl.BlockSpec((None, 1, N_MOD * D_MODEL), lambda i, j: (i // tiles_per_mod, 0, 0)),
                  pl.BlockSpec((4, D_MODEL), const),
                  pl.BlockSpec((2 * ATT_WIDTH, D_MODEL), const, pipeline_mode=pl.Buffered(1)),
                  pl.BlockSpec((D_MODEL, 2 * D_FF), const, pipeline_mode=pl.Buffered(1)),
                  pl.BlockSpec((D_FF, D_MODEL), const, pipeline_mode=pl.Buffered(1))],
        out_specs=row_spec(D_MODEL),
        out_shape=jax.ShapeDtypeStruct((n_tok, D_MODEL), F32),
        scratch_shapes=[pltpu.VMEM((N_COL_BLOCKS, TOK_PER_TILE, LANES), F32),
                        pltpu.VMEM((SUB_TOK, D_FF), BF16)],
        compiler_params=_cparams(2),
        name="out_proj_ffn",
    )(x2d, attn, zt, mod3, norm_g, w_o, w_ffn_in, w_ffn_out)


def _rope_tables(seq_len):
    t = np.arange(seq_len)
    row = (t // GRID_W).astype(np.float32)
    col = (t % GRID_W).astype(np.float32)
    half = HEAD_DIM // 2
    inv_freq = (np.float32(ROPE_BASE)
                ** (-np.arange(0, half, 2, dtype=np.float32) / np.float32(half))).astype(np.float32)
    ang_r = row[:, None] * inv_freq
    ang_c = col[:, None] * inv_freq
    ang = np.concatenate([ang_r, ang_r, ang_c, ang_c], axis=-1)
    cos, sin = np.cos(ang), np.sin(ang)
    upper = (np.arange(HEAD_DIM) % 32) < 16
    sa = np.where(upper, -sin, 0.0)
    sb = np.where(upper, 0.0, sin)
    two = lambda a: jnp.asarray(np.concatenate([a, a], axis=-1), dtype=F32)
    return two(cos), two(sa), two(sb)


def _layer(x, mod_rows, lam_init, rope_tabs, ctx_k, ctx_v, h0, weights, prep):
    n_batch, seq_len = x.shape[:2]
    mod3 = mod_rows[:, None, :]
    g = weights['norm_g']
    outs = _in_proj(x, mod3[:, :, 0:2 * D_MODEL], g[0:1], weights['w_in'], rope_tabs)
    q, k, v = outs[:3]
    attn = _attention(q, k, v, ctx_k, ctx_v, weights['lam'], weights['subln_g'],
                      n_batch, seq_len, lam_init)
    ssm_out = _ssm(outs[-1], *prep, weights['w_glu_t'],weights['b_glu_col'], h0, n_batch)
    y = _out_ffn(x.reshape(n_batch * seq_len, D_MODEL), attn, ssm_out[0], mod3, g,
                 weights['w_o'], weights['w_ffn_in'], weights['w_ffn_out'])
    return y.reshape(x.shape), outs[3:-1], ssm_out[1:]


def kernel(x_prompt, x_sample, cache_k, cache_v, state_ssm_re, state_ssm_im, c, c_ctx, w_mod, b_mod, norm_g, w_in, lam_params, subln_g, ssm_lambda_re, ssm_lambda_im, ssm_log_step, ssm_b_re, ssm_b_im, ssm_c_re, ssm_c_im, ssm_d, w_glu, b_glu, w_o, w_ffn_in, w_ffn_out):
    depth = w_mod.shape[0]
    assert depth == 1
    bp = x_prompt.shape[0]
    bd, ld_len = x_sample.shape[:2]
    past = cache_k.shape[2]
    xp, xs = x_prompt, x_sample
    cond = jnp.zeros((8, D_MODEL), F32).at[0].set(c_ctx).at[1:1 + bd].set(c)
    rope_tabs = _rope_tables(ld_len)
    ks_out, vs_out, hr_out, hi_out = [], [], [], []
    for l in range(depth):
        lam_init = 0.8 - 0.6 * math.exp(-0.3 * l)
        mods = _modulation(cond, w_mod[l], b_mod[l])
        weights = {
            'norm_g': norm_g[l],
            'w_in': w_in[l].astype(BF16),
            'lam': lam_params[l], 'subln_g': subln_g[l],
            'w_glu_t': w_glu[l].T.astype(BF16), 'b_glu_col': b_glu[l].reshape(SSM_WIDTH, 1),
            'w_o': w_o[l].astype(BF16),
            'w_ffn_in': w_ffn_in[l].astype(BF16), 'w_ffn_out': w_ffn_out[l].astype(BF16),
        }
        prep = _ssm_prep(ssm_lambda_re[l], ssm_lambda_im[l], ssm_log_step[l],
                         ssm_b_re[l], ssm_b_im[l], ssm_c_re[l], ssm_c_im[l], ssm_d[l])
        xp, (k_ctx, v_ctx), (st,) = _layer(xp, mods[0:1], lam_init, None, None, None, None,
                                           weights, prep)
        ks_out.append(k_ctx)
        vs_out.append(v_ctx)
        fin = st.reshape(N_GROUPS, bp, 2, 2, SSM_STATE).transpose(1, 2, 3, 0, 4)
        hr_out.append(fin[:, :, 0])
        hi_out.append(fin[:, :, 1])
        ck = cache_k[:, l].reshape(bd, past, ATT_WIDTH)
        cv = cache_v[:, l].reshape(bd, past, ATT_WIDTH)
        h0 = jnp.stack([state_ssm_re[:, l], state_ssm_im[:, l]], axis=2)
        h0 = h0.transpose(3, 0, 1, 2, 4).reshape(N_GROUPS, bd, CW)
        xs, _, _ = _layer(xs, mods[1:1 + bd], lam_init, rope_tabs, ck, cv, h0, weights, prep)
    return (xp, xs, jnp.stack(ks_out, axis=1), jnp.stack(vs_out, axis=1),
            jnp.stack(hr_out, axis=1), jnp.stack(hi_out, axis=1))
```

```python
import functools
import math

import jax
import jax.numpy as jnp
import numpy as np
from jax import lax
from jax.experimental import pallas as pl
from jax.experimental.pallas import tpu as pltpu

F32 = jnp.float32
BF16 = jnp.bfloat16

D_MODEL = 1024
GRID_W = 64
ATT_WIDTH = 512
SSM_WIDTH = 512
HEAD_DIM = 64
N_HEADS = 4
HEAD_W = 2 * HEAD_DIM
SSM_GROUP = 16
N_GROUPS = 32
SSM_STATE = 64
D_FF = 2816
N_MOD = 6
ROPE_BASE = 10000.0
NORM_EPS = 1e-6

LANES = 128
N_COL_BLOCKS = D_MODEL // LANES
CHUNK = 16
CW = CHUNK * SSM_GROUP
N_LEVELS = 8
GROUP_BLOCK = 4

ROWS_PER_TILE = 128
TOK_PER_TILE = ROWS_PER_TILE * CHUNK
SUB_TOK = 512
N_SUB = TOK_PER_TILE // SUB_TOK

VMEM_LIMIT = 56 * 1024 * 1024

NT_DIMS = (((1,), (1,)), ((), ()))
TN_DIMS = (((0,), (0,)), ((), ()))


def _cparams(n_axes):
    return pltpu.CompilerParams(
        dimension_semantics=("arbitrary",) * n_axes,
        vmem_limit_bytes=VMEM_LIMIT)


def _rms(x, g):
    ms = jnp.mean(x * x, axis=-1, keepdims=True)
    return x * lax.rsqrt(ms + NORM_EPS) * g


def _silu(x):
    return x * jax.nn.sigmoid(x)


def _mod_kernel(cond_ref, w_ref, b_ref, o_ref):
    s = _silu(cond_ref[...]).astype(BF16)
    o_ref[...] = jnp.dot(s, w_ref[...].astype(BF16),
                         preferred_element_type=F32) + b_ref[...]


def _modulation(cond, w_mod, b_mod):
    n = w_mod.shape[1]
    tn = 1536
    return pl.pallas_call(
        _mod_kernel,
        grid=(n // tn,),
        in_specs=[pl.BlockSpec((8, D_MODEL), lambda i: (0, 0)),
                  pl.BlockSpec((D_MODEL, tn), lambda i: (0, i)),
                  pl.BlockSpec((1, tn), lambda i: (0, i))],
        out_specs=pl.BlockSpec((8, tn), lambda i: (0, i)),
        out_shape=jax.ShapeDtypeStruct((8, n), F32),
        compiler_params=_cparams(1),
        name="modulation",
    )(cond, w_mod, b_mod.reshape(1, n))


def _rope(x, cos, sa, sb):
    return (x * cos + pltpu.roll(x, HEAD_W - 16, axis=1) * sa
            + pltpu.roll(x, 16, axis=1) * sb)


def _in_proj_kernel(*refs, rope, seqs_per_sub):
    x_ref, x3_hbm, mod_ref, g_ref, w_ref = refs[:5]
    refs = refs[5:]
    if rope:
        cos_ref, sa_ref, sb_ref = refs[:3]
        refs = refs[3:]
        q_ref, k_ref, v_ref, ut_ref, wut_ref, xt_ref, xt_sem = refs
    else:
        q_ref, k_ref, v_ref, kc_ref, vc_ref, ut_ref, wut_ref, xt_ref, xt_sem = refs
    tile = pl.program_id(0)
    j = pl.program_id(1)
    t_per_sub = CHUNK // N_SUB
    t_early = CHUNK - t_per_sub

    def gather(tile_idx, t):
        src = x3_hbm.at[pl.ds(tile_idx * ROWS_PER_TILE, ROWS_PER_TILE), t, :]
        return pltpu.make_async_copy(src, xt_ref.at[t], xt_sem.at[t])

    @pl.when(j == 0)
    def _():
        @pl.when(tile == 0)
        def _():
            for t in range(t_early):
                gather(0, t).start()
        for t in range(t_early, CHUNK):
            gather(tile, t).start()

    t_base = j * t_per_sub
    for d in range(t_per_sub):
        gather(tile, t_base + d).wait()

    @pl.when((j == N_SUB - 1) & (tile + 1 < pl.num_programs(0)))
    def _():
        for t in range(t_early):
            gather(tile + 1, t).start()

    @pl.when((tile == 0) & (j == 0))
    def _():
        wut_ref[...] = w_ref[:, 3 * ATT_WIDTH:].T

    shift = mod_ref[:, 0:D_MODEL]
    gain = g_ref[...] * (1.0 + mod_ref[:, D_MODEL:2 * D_MODEL])

    def norm_mod(xv):
        ms = jnp.mean(xv * xv, axis=-1, keepdims=True)
        return (xv * lax.rsqrt(ms + NORM_EPS) * gain + shift).astype(BF16)

    tok = pl.ds(pl.multiple_of(j * SUB_TOK, SUB_TOK), SUB_TOK)
    proj = jnp.dot(norm_mod(x_ref[tok, :]), w_ref[:, 0:3 * ATT_WIDTH],
                   preferred_element_type=F32)
    q = proj[:, 0:ATT_WIDTH]
    k = proj[:, ATT_WIDTH:2 * ATT_WIDTH]
    v = proj[:, 2 * ATT_WIDTH:3 * ATT_WIDTH]
    qscale = HEAD_DIM ** -0.5 * math.log2(math.e)
    if rope:
        cos, sa, sb = cos_ref[...], sa_ref[...], sb_ref[...]
        for hd in range(N_HEADS):
            sl = slice(hd * HEAD_W, (hd + 1) * HEAD_W)
            q_ref[:, sl] = (_rope(q[:, sl], cos, sa, sb) * qscale).astype(q_ref.dtype)
            k_ref[:, sl] = _rope(k[:, sl], cos, sa, sb).astype(k_ref.dtype)
    else:
        q_ref[...] = (q * qscale).astype(q_ref.dtype)
        k_ref[...] = k.astype(k_ref.dtype)
        seq = SUB_TOK // seqs_per_sub
        for b in range(seqs_per_sub):
            for m in range(2 * N_HEADS):
                kc_ref[b, :, m, :] = k[b * seq:(b + 1) * seq, m * HEAD_DIM:(m + 1) * HEAD_DIM]
            for hd in range(N_HEADS):
                vc_ref[b, :, hd, :] = v[b * seq:(b + 1) * seq, hd * HEAD_W:(hd + 1) * HEAD_W]
    v_ref[...] = v.astype(v_ref.dtype)

    for d0 in range(0, t_per_sub, 2):
        xt = jnp.concatenate([xt_ref[t_base + d0], xt_ref[t_base + d0 + 1]], axis=0)
        ut = lax.dot_general(wut_ref[...], norm_mod(xt), NT_DIMS,
                             preferred_element_type=F32)
        for d in range(2):
            blk = ut[:, d * ROWS_PER_TILE:(d + 1) * ROWS_PER_TILE]
            row0 = pl.multiple_of((t_base + d0 + d) * SSM_GROUP, SSM_GROUP)
            ut_ref[:, pl.ds(row0, SSM_GROUP), :] = (
                blk.reshape(N_GROUPS, SSM_GROUP, ROWS_PER_TILE).astype(ut_ref.dtype))


def _in_proj(x, mod3, g0, w_in, rope_tabs):
    n_batch, seq_len = x.shape[:2]
    n_tok = n_batch * seq_len
    n_rows = n_tok // CHUNK
    n_tiles = n_tok // TOK_PER_TILE
    tiles_per_mod = n_tiles // mod3.shape[0]
    rope = rope_tabs is not None
    seqs_per_sub = max(1, SUB_TOK // seq_len)
    in_specs = [pl.BlockSpec((TOK_PER_TILE, D_MODEL), lambda i, j: (i, 0)),
                pl.BlockSpec(memory_space=pl.ANY),
                pl.BlockSpec((None, 1, 2 * D_MODEL), lambda i, j: (i // tiles_per_mod, 0, 0)),
                pl.BlockSpec((1, D_MODEL), lambda i, j: (0, 0)),
                pl.BlockSpec((D_MODEL, 4 * ATT_WIDTH), lambda i, j: (0, 0))]
    args = [x.reshape(n_tok, D_MODEL), x.reshape(n_rows, CHUNK, D_MODEL), mod3, g0, w_in]
    row_spec = pl.BlockSpec((SUB_TOK, ATT_WIDTH), lambda i, j: (i * N_SUB + j, 0))
    row_shape = jax.ShapeDtypeStruct((n_tok, ATT_WIDTH), BF16)
    out_specs = [row_spec, row_spec, row_spec]
    out_shape = [row_shape, row_shape, row_shape]
    if rope:
        assert seq_len == TOK_PER_TILE
        for tab in rope_tabs:
            in_specs.append(pl.BlockSpec((SUB_TOK, HEAD_W), lambda i, j: (j, 0)))
            args.append(tab)
    else:
        out_specs += [pl.BlockSpec((seqs_per_sub, seq_len, 2 * N_HEADS, HEAD_DIM),
                                   lambda i, j: (i * N_SUB + j, 0, 0, 0)),
                      pl.BlockSpec((seqs_per_sub, seq_len, N_HEADS, HEAD_W),
                                   lambda i, j: (i * N_SUB + j, 0, 0, 0))]
        out_shape += [jax.ShapeDtypeStruct((n_batch, seq_len, 2 * N_HEADS, HEAD_DIM), F32),
                      jax.ShapeDtypeStruct((n_batch, seq_len, N_HEADS, HEAD_W), F32)]
    out_specs.append(pl.BlockSpec((N_GROUPS, CW, ROWS_PER_TILE), lambda i, j: (0, 0, i)))
    out_shape.append(jax.ShapeDtypeStruct((N_GROUPS, CW, n_rows), BF16))
    return pl.pallas_call(
        functools.partial(_in_proj_kernel, rope=rope, seqs_per_sub=seqs_per_sub),
        grid=(n_tiles, N_SUB),
        in_specs=in_specs,
        out_specs=out_specs,
        out_shape=out_shape,
        scratch_shapes=[pltpu.VMEM((SSM_WIDTH, D_MODEL), BF16),
                        pltpu.VMEM((CHUNK, ROWS_PER_TILE, D_MODEL), F32),
                        pltpu.SemaphoreType.DMA((CHUNK,))],
        compiler_params=_cparams(2),
        name="in_proj",
    )(*args)


def _attn_kernel(*refs, has_ctx, lam_init, n_seq, seq_len, tq, n_cast):
    if n_cast:
        cast_in = refs[len(refs) - 2 * n_cast - 1:len(refs) - n_cast - 1]
        cast_out = refs[len(refs) - n_cast:]
        refs = refs[:len(refs) - 2 * n_cast - 1] + (refs[len(refs) - n_cast - 1],)
        for src, dst in zip(cast_in, cast_out):
            dst[...] = src[...].astype(dst.dtype)
    if has_ctx:
        lam_ref, sg_ref, q_ref, ck_ref, cv_ref, k_ref, v_ref, o_ref = refs
    else:
        lam_ref, sg_ref, q_ref, k_ref, v_ref, o_ref = refs
    lp = lam_ref[...]
    lam = (jnp.exp(jnp.sum(lp[0:1] * lp[1:2], axis=-1, keepdims=True))
           - jnp.exp(jnp.sum(lp[2:3] * lp[3:4], axis=-1, keepdims=True)) + lam_init)
    first_map = lax.broadcasted_iota(jnp.int32, (1, HEAD_W), 1) < HEAD_DIM
    for b in range(n_seq):
        for hd in range(N_HEADS):
            sl = slice(hd * HEAD_W, (hd + 1) * HEAD_W)
            qh = q_ref[b * tq:(b + 1) * tq, sl]
            zero = jnp.zeros_like(qh)
            qs = jnp.concatenate([jnp.where(first_map, qh, zero),
                                  jnp.where(first_map, zero, qh)], axis=0)
            kv_rows = slice(b * seq_len, (b + 1) * seq_len)
            parts = [(k_ref[kv_rows, sl], v_ref[kv_rows, sl])]
            if has_ctx:
                parts.insert(0, (ck_ref[:, sl].astype(BF16), cv_ref[:, sl].astype(BF16)))
            scores = [lax.dot_general(qs, kk, NT_DIMS, preferred_element_type=F32)
                      for kk, _ in parts]
            mx = scores[0].max(axis=-1, keepdims=True)
            for s in scores[1:]:
                mx = jnp.maximum(mx, s.max(axis=-1, keepdims=True))
            acc = None
            for s, (_, vv) in zip(scores, parts):
                e = jnp.exp2(s - mx).astype(BF16)
                v_one = jnp.concatenate([vv, jnp.ones_like(vv)], axis=1)
                pv = jnp.dot(e, v_one, preferred_element_type=F32)
                acc = pv if acc is None else acc + pv
            num = acc[:, 0:HEAD_W] / acc[:, HEAD_W:2 * HEAD_W]
            o = num[0:tq] - lam * num[tq:2 * tq]
            o = _rms(o, sg_ref[...]) * (1.0 - lam_init)
            o_ref[b * tq:(b + 1) * tq, sl] = o.astype(o_ref.dtype)


def _attention(q, k, v, ctx_k, ctx_v, lam_params, subln_g, n_batch, seq_len, lam_init,
               cast_weights=()):
    has_ctx = ctx_k is not None
    tq = min(512, seq_len)
    n_q = seq_len // tq
    n_seq = 1 if n_q > 1 else min(4, n_batch)
    in_specs = [pl.BlockSpec((4, HEAD_DIM), lambda b, i: (0, 0)),
                pl.BlockSpec((1, HEAD_W), lambda b, i: (0, 0)),
                pl.BlockSpec((n_seq * tq, ATT_WIDTH), lambda b, i: (b * n_q + i, 0))]
    args = [lam_params, subln_g.reshape(1, HEAD_W), q]
    if has_ctx:
        past = ctx_k.shape[1]
        ctx_spec = pl.BlockSpec((None, past, ATT_WIDTH), lambda b, i: (b, 0, 0))
        in_specs += [ctx_spec, ctx_spec]
        args += [ctx_k, ctx_v]
    kv_spec = pl.BlockSpec((n_seq * seq_len, ATT_WIDTH), lambda b, i: (b, 0))
    in_specs += [kv_spec, kv_spec]
    args += [k, v]
    out_specs = [pl.BlockSpec((n_seq * tq, ATT_WIDTH), lambda b, i: (b * n_q + i, 0))]
    out_shape = [jax.ShapeDtypeStruct((n_batch * seq_len, ATT_WIDTH), BF16)]
    n_steps = (n_batch // n_seq) * n_q
    for w in cast_weights:
        rows = w.shape[0] // n_steps
        spec = pl.BlockSpec((rows, w.shape[1]), lambda b, i: (b * n_q + i, 0))
        in_specs.append(spec)
        args.append(w)
        out_specs.append(spec)
        out_shape.append(jax.ShapeDtypeStruct(w.shape, BF16))
    outs = pl.pallas_call(
        functools.partial(_attn_kernel, has_ctx=has_ctx, lam_init=lam_init,
                          n_seq=n_seq, seq_len=seq_len, tq=tq, n_cast=len(cast_weights)),
        grid=(n_batch // n_seq, n_q),
        in_specs=in_specs,
        out_specs=out_specs,
        out_shape=out_shape,
        compiler_params=_cparams(2),
        name="diff_attention",
    )(*args)
    return outs[0], tuple(outs[1:])


def _cmul(ar, ai, br, bi):
    return ar * br - ai * bi, ar * bi + ai * br


def _ssm_prep_kernel(lre_ref, lim_ref, ls_ref, bre_ref, bim_ref, cre_ref, cim_ref, d_ref,
                     mt_ref, gt_ref, wo_ref, at_ref):
    lane = lax.broadcasted_iota(jnp.int32, (SSM_GROUP, CW), 1)
    chan = lax.broadcasted_iota(jnp.int32, (SSM_GROUP, CW), 0)
    for gi in range(GROUP_BLOCK):
        gt_cols, wo_cols, at_cols, toeplitz = [], [], [], []
        for dr in range(2):
            lr = jnp.minimum(lre_ref[dr, gi], -1e-4)
            li = lim_ref[dr, gi]
            step = jnp.exp(ls_ref[dr, gi])
            mag = jnp.exp(lr * step)
            a_re = mag * jnp.cos(li * step)
            a_im = mag * jnp.sin(li * step)
            den = lr * lr + li * li
            nr = a_re - 1.0
            f_re = (nr * lr + a_im * li) / den
            f_im = (a_im * lr - nr * li) / den
            bt_re, bt_im = bre_ref[dr, gi], bim_ref[dr, gi]
            bb_re, bb_im = _cmul(f_re, f_im, bt_re, bt_im)
            c_re, c_im = cre_ref[dr, gi], cim_ref[dr, gi]
            pw = [(jnp.ones_like(a_re), jnp.zeros_like(a_im))]
            for _ in range(CHUNK):
                pw.append(_cmul(pw[-1][0], pw[-1][1], a_re, a_im))
            g_re, g_im, e_re, e_im = [], [], [], []
            for t in range(CHUNK):
                pr, pi = pw[CHUNK - 1 - t] if dr == 0 else pw[t]
                r, i = _cmul(bb_re, bb_im, pr, pi)
                g_re.append(r)
                g_im.append(i)
                pr, pi = pw[t + 1] if dr == 0 else pw[CHUNK - t]
                r, i = _cmul(c_re, c_im, pr, pi)
                e_re.append(r)
                e_im.append(-i)
            g_cat = jnp.concatenate([jnp.concatenate(g_re, axis=0),
                                     jnp.concatenate(g_im, axis=0)], axis=1)
            gt_cols.append(g_cat)
            wo_cols.append(jnp.concatenate([jnp.concatenate(e_re, axis=0),
                                            jnp.concatenate(e_im, axis=0)], axis=1))
            c_cat = jnp.concatenate([c_re, -c_im], axis=1)
            toeplitz.append(lax.dot_general(c_cat, g_cat, NT_DIMS,
                                            precision=lax.Precision.HIGHEST,
                                            preferred_element_type=F32))
            pr, pi = pw[CHUNK]
            rows_p, rows_q = [], []
            for _ in range(N_LEVELS - 1):
                rows_p.append(jnp.concatenate([pr, pr], axis=1))
                rows_q.append(jnp.concatenate([-pi, pi], axis=1))
                pr, pi = _cmul(pr, pi, pr, pi)
            rows_p.append(jnp.zeros_like(rows_p[0]))
            rows_q.append(jnp.zeros_like(rows_q[0]))
            at_cols += [jnp.concatenate(rows_p, axis=0), jnp.concatenate(rows_q, axis=0)]
        kf_rev, kb = toeplitz
        d_skip = d_ref[gi]
        blocks = []
        for t in range(CHUNK):
            fwd = pltpu.roll(kf_rev, (CW - (CHUNK - 1 - t) * SSM_GROUP) % CW, axis=1)
            bwd = pltpu.roll(kb, t * SSM_GROUP, axis=1)
            blocks.append(jnp.where(lane < (t + 1) * SSM_GROUP, fwd, 0.0)
                          + jnp.where(lane >= t * SSM_GROUP, bwd, 0.0)
                          + jnp.where(lane == chan + t * SSM_GROUP, d_skip, 0.0))
        mt_ref[gi] = jnp.concatenate(blocks, axis=0).astype(mt_ref.dtype)
        gt_ref[gi] = jnp.concatenate(gt_cols, axis=1).astype(gt_ref.dtype)
        wo_ref[gi] = jnp.concatenate(wo_cols, axis=1).astype(wo_ref.dtype)
        at_ref[gi] = jnp.concatenate(at_cols, axis=1)


def _ssm_prep(lam_re, lam_im, log_step, b_re, b_im, c_re, c_im, d_skip):
    row = lambda a: a.reshape(2, N_GROUPS, 1, SSM_STATE)
    bt = lambda a: jnp.swapaxes(a, 2, 3)
    d_col = (d_skip[0] + d_skip[1]).reshape(N_GROUPS, SSM_GROUP, 1)
    gb = GROUP_BLOCK
    vec_spec = pl.BlockSpec((2, gb, 1, SSM_STATE), lambda i: (0, i, 0, 0))
    mat_spec = pl.BlockSpec((2, gb, SSM_GROUP, SSM_STATE), lambda i: (0, i, 0, 0))
    w_spec = pl.BlockSpec((gb, CW, CW), lambda i: (i, 0, 0))
    w_shape = jax.ShapeDtypeStruct((N_GROUPS, CW, CW), BF16)
    return pl.pallas_call(
        _ssm_prep_kernel,
        grid=(N_GROUPS // gb,),
        in_specs=[vec_spec, vec_spec,
                  pl.BlockSpec((2, gb, 1, 1), lambda i: (0, i, 0, 0)),
                  mat_spec, mat_spec, mat_spec, mat_spec,
                  pl.BlockSpec((gb, SSM_GROUP, 1), lambda i: (i, 0, 0))],
        out_specs=[w_spec, w_spec, w_spec,
                   pl.BlockSpec((gb, N_LEVELS, 4 * 2 * SSM_STATE), lambda i: (i, 0, 0))],
        out_shape=[w_shape, w_shape, w_shape,
                   jax.ShapeDtypeStruct((N_GROUPS, N_LEVELS, 4 * 2 * SSM_STATE), F32)],
        compiler_params=_cparams(1),
        name="ssm_prep",
    )(row(lam_re), row(lam_im), log_step.reshape(2, N_GROUPS, 1, 1),
      bt(b_re), bt(b_im), c_re, c_im, d_col)


def _shift_rows(x, m, down):
    n = x.shape[0]
    return pltpu.roll(x, m if down else n - m, axis=0)


def _ssm_kernel(*refs, n_seq, has_h0):
    if has_h0:
        (xt_ref, mt_ref, gt_ref, wo_ref, at_ref, wg_ref, bg_ref, h0_ref,
         z_ref, zs_ref) = refs
    else:
        (xt_ref, mt_ref, gt_ref, wo_ref, at_ref, wg_ref, bg_ref,
         z_ref, st_ref, zs_ref, fin_ref) = refs
    step = pl.program_id(0)
    n_rows = xt_ref.shape[-1]
    seg = n_rows // n_seq
    cw2 = 2 * SSM_STATE
    pos = lax.broadcasted_iota(jnp.int32, (n_rows, cw2), 0) % seg
    for gi in range(GROUP_BLOCK):
        xt = xt_ref[gi]
        at = at_ref[gi]
        s_all = lax.dot_general(xt, gt_ref[gi], TN_DIMS, preferred_element_type=F32)
        h_parts = []
        for dr in range(2):
            s = s_all[:, dr * cw2:(dr + 1) * cw2]
            p_tab = at[:, (2 * dr) * cw2:(2 * dr + 1) * cw2]
            q_tab = at[:, (2 * dr + 1) * cw2:(2 * dr + 2) * cw2]
            down = dr == 0
            if has_h0:
                h0 = jnp.zeros((n_rows, cw2), F32)
                riota = lax.broadcasted_iota(jnp.int32, (n_rows, cw2), 0)
                for b in range(n_seq):
                    edge = b * seg if down else b * seg + seg - 1
                    h0 = jnp.where(riota == edge, h0_ref[gi, b:b + 1, dr * cw2:(dr + 1) * cw2], h0)
                s = s + p_tab[0:1] * h0 + q_tab[0:1] * pltpu.roll(h0, SSM_STATE, axis=1)
            m, lvl = 1, 0
            while m < seg:
                sh = _shift_rows(s, m, down)
                valid = (pos >= m) if down else (pos < seg - m)
                sh = jnp.where(valid, sh, 0.0)
                s = s + p_tab[lvl:lvl + 1] * sh + q_tab[lvl:lvl + 1] * pltpu.roll(sh, SSM_STATE, axis=1)
                m, lvl = 2 * m, lvl + 1
            if not has_h0:
                fin_ref[dr] = s
                st_ref[gi, :, dr * cw2:(dr + 1) * cw2] = (
                    fin_ref[dr, pl.ds(seg - 1 if down else 0, n_seq, stride=seg), :])
            ent = _shift_rows(s, 1, down)
            ent = jnp.where((pos >= 1) if down else (pos < seg - 1), ent, 0.0)
            if has_h0:
                ent = ent + h0
            h_parts.append(ent)
        h_all = jnp.concatenate(h_parts, axis=1).astype(BF16)
        yt = (jnp.dot(mt_ref[gi], xt, preferred_element_type=F32)
              + lax.dot_general(wo_ref[gi], h_all, NT_DIMS, preferred_element_type=F32))
        z = jax.nn.gelu(yt, approximate=True)
        grp = step * GROUP_BLOCK + gi
        for t in range(CHUNK):
            zs_ref[t, pl.ds(pl.multiple_of(grp * SSM_GROUP, SSM_GROUP), SSM_GROUP), :] = (
                z[t * SSM_GROUP:(t + 1) * SSM_GROUP, :])

    @pl.when(step == pl.num_programs(0) - 1)
    def _():
        for t in range(CHUNK):
            zt = zs_ref[t]
            gate = jnp.dot(wg_ref[...], zt.astype(BF16),
                           preferred_element_type=F32) + bg_ref[...]
            z_ref[t] = (zt * jax.nn.sigmoid(gate)).astype(z_ref.dtype)


def _ssm(xt, mt, gt, wo, at, w_glu_t, b_glu_col, h0, n_seq):
    n_rows = xt.shape[-1]
    has_h0 = h0 is not None
    gb = GROUP_BLOCK
    w_spec = pl.BlockSpec((gb, CW, CW), lambda i: (i, 0, 0))
    in_specs = [pl.BlockSpec((gb, CW, n_rows), lambda i: (i, 0, 0)),
                w_spec, w_spec, w_spec,
                pl.BlockSpec((gb, N_LEVELS, 4 * 2 * SSM_STATE), lambda i: (i, 0, 0)),
                pl.BlockSpec((SSM_WIDTH, SSM_WIDTH), lambda i: (0, 0)),
                pl.BlockSpec((SSM_WIDTH, 1), lambda i: (0, 0))]
    args = [xt, mt, gt, wo, at, w_glu_t, b_glu_col]
    out_specs = [pl.BlockSpec((CHUNK, SSM_WIDTH, n_rows), lambda i: (0, 0, 0))]
    out_shape = [jax.ShapeDtypeStruct((CHUNK, SSM_WIDTH, n_rows), BF16)]
    scratch = [pltpu.VMEM((CHUNK, SSM_WIDTH, n_rows), F32)]
    if has_h0:
        in_specs.append(pl.BlockSpec((gb, n_seq, CW), lambda i: (i, 0, 0)))
        args.append(h0)
    else:
        out_specs.append(pl.BlockSpec((gb, n_seq, CW), lambda i: (i, 0, 0)))
        out_shape.append(jax.ShapeDtypeStruct((N_GROUPS, n_seq, CW), F32))
        scratch.append(pltpu.VMEM((2, n_rows, 2 * SSM_STATE), F32))
    return pl.pallas_call(
        functools.partial(_ssm_kernel, n_seq=n_seq, has_h0=has_h0),
        grid=(N_GROUPS // gb,),
        in_specs=in_specs,
        out_specs=out_specs,
        out_shape=out_shape,
        scratch_shapes=scratch,
        compiler_params=_cparams(1),
        name="ssm_scan_glu",
    )(*args)


FF_CHUNK = 256


def _out_ffn_kernel(x_ref, attn_ref, zt_ref, mod_ref, g_ref, wo_ref, wfi_ref, wfo_ref,
                    o_ref, mixz_ref, act_ref):
    j = pl.program_id(1)

    @pl.when(j == 0)
    def _():
        for t in range(CHUNK):
            part = lax.dot_general(zt_ref[t], wo_ref[ATT_WIDTH:, :], TN_DIMS,
                                   preferred_element_type=F32)
            for cb in range(N_COL_BLOCKS):
                mixz_ref[cb, pl.ds(t, ROWS_PER_TILE, stride=CHUNK), :] = (
                    part[:, cb * LANES:(cb + 1) * LANES])

    gate1 = mod_ref[:, 2 * D_MODEL:3 * D_MODEL]
    shift2 = mod_ref[:, 3 * D_MODEL:4 * D_MODEL]
    scale2 = mod_ref[:, 4 * D_MODEL:5 * D_MODEL]
    gate2 = mod_ref[:, 5 * D_MODEL:6 * D_MODEL]
    rows = pl.ds(pl.multiple_of(j * SUB_TOK, SUB_TOK), SUB_TOK)
    mix = (jnp.dot(attn_ref[...], wo_ref[0:ATT_WIDTH, :], preferred_element_type=F32)
           + jnp.concatenate([mixz_ref[cb, rows, :] for cb in range(N_COL_BLOCKS)], axis=1))
    x1 = x_ref[...] + gate1 * _rms(mix, g_ref[1:2, :])
    h = (_rms(x1, g_ref[2:3, :]) * (1.0 + scale2) + shift2).astype(BF16)
    for c in range(D_FF // FF_CHUNK):
        lo = c * FF_CHUNK
        gt = jnp.dot(h, wfi_ref[:, lo:lo + FF_CHUNK], preferred_element_type=F32)
        up = jnp.dot(h, wfi_ref[:, D_FF + lo:D_FF + lo + FF_CHUNK], preferred_element_type=F32)
        act_ref[:, lo:lo + FF_CHUNK] = (_silu(gt) * up).astype(BF16)
    f = jnp.dot(act_ref[...], wfo_ref[...], preferred_element_type=F32)
    o_ref[...] = x1 + gate2 * _rms(f, g_ref[3:4, :])


def _out_ffn(x2d, attn, zt, mod3, norm_g, w_o, w_ffn_in, w_ffn_out):
    n_tok = x2d.shape[0]
    n_tiles = n_tok // TOK_PER_TILE
    tiles_per_mod = n_tiles // mod3.shape[0]
    const = lambda i, j: (0, 0)
    row_spec = lambda w: pl.BlockSpec((SUB_TOK, w), lambda i, j: (i * N_SUB + j, 0))
    return pl.pallas_call(
        _out_ffn_kernel,
        grid=(n_tiles, N_SUB),
        in_specs=[row_spec(D_MODEL), row_spec(ATT_WIDTH),
                  pl.BlockSpec((CHUNK, SSM_WIDTH, ROWS_PER_TILE), lambda i, j: (0, 0, i)),
                  pl.BlockSpec((None, 1, N_MOD * D_MODEL), lambda i, j: (i // tiles_per_mod, 0, 0)),
                  pl.BlockSpec((4, D_MODEL), const),
                  pl.BlockSpec((2 * ATT_WIDTH, D_MODEL), const, pipeline_mode=pl.Buffered(1)),
                  pl.BlockSpec((D_MODEL, 2 * D_FF), const, pipeline_mode=pl.Buffered(1)),
                  pl.BlockSpec((D_FF, D_MODEL), const, pipeline_mode=pl.Buffered(1))],
        out_specs=row_spec(D_MODEL),
        out_shape=jax.ShapeDtypeStruct((n_tok, D_MODEL), F32),
        scratch_shapes=[pltpu.VMEM((N_COL_BLOCKS, TOK_PER_TILE, LANES), F32),
                        pltpu.VMEM((SUB_TOK, D_FF), BF16)],
        compiler_params=_cparams(2),
        name="out_proj_ffn",
    )(x2d, attn, zt, mod3, norm_g, w_o, w_ffn_in, w_ffn_out)


def _rope_tables(seq_len):
    t = np.arange(seq_len)
    row = (t // GRID_W).astype(np.float32)
    col = (t % GRID_W).astype(np.float32)
    half = HEAD_DIM // 2
    inv_freq = (np.float32(ROPE_BASE)
                ** (-np.arange(0, half, 2, dtype=np.float32) / np.float32(half))).astype(np.float32)
    ang_r = row[:, None] * inv_freq
    ang_c = col[:, None] * inv_freq
    ang = np.concatenate([ang_r, ang_r, ang_c, ang_c], axis=-1)
    cos, sin = np.cos(ang), np.sin(ang)
    upper = (np.arange(HEAD_DIM) % 32) < 16
    sa = np.where(upper, -sin, 0.0)
    sb = np.where(upper, 0.0, sin)
    two = lambda a: jnp.asarray(np.concatenate([a, a], axis=-1), dtype=F32)
    return two(cos), two(sa), two(sb)


def _layer(x, mod_rows, lam_init, rope_tabs, ctx_k, ctx_v, h0, weights, prep):
    n_batch, seq_len = x.shape[:2]
    mod3 = mod_rows[:, None, :]
    g = weights['norm_g']
    outs = _in_proj(x, mod3[:, :, 0:2 * D_MODEL], g[0:1], weights['w_in'], rope_tabs)
    q, k, v = outs[:3]
    pending = () if 'ffn_bf16' in weights else weights['ffn_f32']
    attn, cast = _attention(q, k, v, ctx_k, ctx_v, weights['lam'], weights['subln_g'],
                            n_batch, seq_len, lam_init, cast_weights=pending)
    if pending:
        weights['ffn_bf16'] = cast
    ssm_out = _ssm(outs[-1], *prep, weights['w_glu_t'], weights['b_glu_col'], h0, n_batch)
    y = _out_ffn(x.reshape(n_batch * seq_len, D_MODEL), attn, ssm_out[0], mod3, g,
                 weights['w_o'], *weights['ffn_bf16'])
    return y.reshape(x.shape), outs[3:-1], ssm_out[1:]


def kernel(x_prompt, x_sample, cache_k, cache_v, state_ssm_re, state_ssm_im, c, c_ctx, w_mod, b_mod, norm_g, w_in, lam_params, subln_g, ssm_lambda_re, ssm_lambda_im, ssm_log_step, ssm_b_re, ssm_b_im, ssm_c_re, ssm_c_im, ssm_d, w_glu, b_glu, w_o, w_ffn_in, w_ffn_out):
    depth = w_mod.shape[0]
    assert depth == 1
    bp = x_prompt.shape[0]
    bd, ld_len = x_sample.shape[:2]
    past = cache_k.shape[2]
    xp, xs = x_prompt, x_sample
    cond = jnp.zeros((8, D_MODEL), F32).at[0].set(c_ctx).at[1:1 + bd].set(c)
    rope_tabs = _rope_tables(ld_len)
    ks_out, vs_out, hr_out, hi_out = [], [], [], []
    for l in range(depth):
        lam_init = 0.8 - 0.6 * math.exp(-0.3 * l)
        mods = _modulation(cond, w_mod[l], b_mod[l])
        weights = {
            'norm_g': norm_g[l],
            'w_in': w_in[l].astype(BF16),
            'lam': lam_params[l], 'subln_g': subln_g[l],
            'w_glu_t': w_glu[l].T.astype(BF16), 'b_glu_col': b_glu[l].reshape(SSM_WIDTH, 1),
            'w_o': w_o[l].astype(BF16),
            'ffn_f32': (w_ffn_in[l], w_ffn_out[l]),
        }
        prep = _ssm_prep(ssm_lambda_re[l], ssm_lambda_im[l], ssm_log_step[l],
                         ssm_b_re[l], ssm_b_im[l], ssm_c_re[l], ssm_c_im[l], ssm_d[l])
        ck = cache_k[:, l].reshape(bd, past, ATT_WIDTH)
        cv = cache_v[:, l].reshape(bd, past, ATT_WIDTH)
        h0 = jnp.stack([state_ssm_re[:, l], state_ssm_im[:, l]], axis=2)
        h0 = h0.transpose(3, 0, 1, 2, 4).reshape(N_GROUPS, bd, CW)
        xs, _, _ = _layer(xs, mods[1:1 + bd], lam_init, rope_tabs, ck, cv, h0, weights, prep)
        xp, (k_ctx, v_ctx), (st,) = _layer(xp, mods[0:1], lam_init, None, None, None, None,
                                           weights, prep)
        ks_out.append(k_ctx)
        vs_out.append(v_ctx)
        fin = st.reshape(N_GROUPS, bp, 2, 2, SSM_STATE).transpose(1, 2, 3, 0, 4)
        hr_out.append(fin[:, :, 0])
        hi_out.append(fin[:, :, 1])
    return (xp, xs, jnp.stack(ks_out, axis=1), jnp.stack(vs_out, axis=1),
            jnp.stack(hr_out, axis=1), jnp.stack(hi_out, axis=1))
```

```python
import functools
import math

import jax
import jax.numpy as jnp
import numpy as np
from jax import lax
from jax.experimental import pallas as pl
from jax.experimental.pallas import tpu as pltpu

F32 = jnp.float32
BF16 = jnp.bfloat16

D_MODEL = 1024
GRID_W = 64
ATT_WIDTH = 512
SSM_WIDTH = 512
HEAD_DIM = 64
N_HEADS = 4
HEAD_W = 2 * HEAD_DIM
SSM_GROUP = 16
N_GROUPS = 32
SSM_STATE = 64
D_FF = 2816
N_MOD = 6
ROPE_BASE = 10000.0
NORM_EPS = 1e-6

LANES = 128
N_COL_BLOCKS = D_MODEL // LANES
CHUNK = 16
CW = CHUNK * SSM_GROUP
N_LEVELS = 8
GROUP_BLOCK = 4

ROWS_PER_TILE = 128
TOK_PER_TILE = ROWS_PER_TILE * CHUNK
SUB_TOK = 512
N_SUB = TOK_PER_TILE // SUB_TOK

VMEM_LIMIT = 56 * 1024 * 1024

NT_DIMS = (((1,), (1,)), ((), ()))
TN_DIMS = (((0,), (0,)), ((), ()))


def _cparams(n_axes):
    return pltpu.CompilerParams(
        dimension_semantics=("arbitrary",) * n_axes,
        vmem_limit_bytes=VMEM_LIMIT)


def _rms(x, g):
    ms = jnp.mean(x * x, axis=-1, keepdims=True)
    return x * lax.rsqrt(ms + NORM_EPS) * g


def _silu(x):
    return x * jax.nn.sigmoid(x)


MOD_ROWS = 8


def _mod_kernel(ctx_ref, c_ref, w_ref, b_ref, o_ref):
    n_lat = c_ref.shape[0]
    row = lax.broadcasted_iota(jnp.int32, (MOD_ROWS, D_MODEL), 0)
    cond = jnp.where(row == 0, ctx_ref[...], 0.0)
    for b in range(n_lat):
        cond = jnp.where(row == 1 + b, c_ref[b:b + 1, :], cond)
    m = jnp.dot(_silu(cond).astype(BF16), w_ref[...].astype(BF16),
                preferred_element_type=F32) + b_ref[...]
    o_ref[:, 0, :] = m


def _modulation(c_ctx, c, w_mod, b_mod):
    n = w_mod.shape[1]
    tn = 512
    assert 1 + c.shape[0] <= MOD_ROWS
    return pl.pallas_call(
        _mod_kernel,
        grid=(n // tn,),
        in_specs=[pl.BlockSpec((1, D_MODEL), lambda i: (0, 0)),
                  pl.BlockSpec(c.shape, lambda i: (0, 0)),
                  pl.BlockSpec((D_MODEL, tn), lambda i: (0, i)),
                  pl.BlockSpec((1, tn), lambda i: (0, i))],
        out_specs=pl.BlockSpec((MOD_ROWS, 1, tn), lambda i: (0, 0, i)),
        out_shape=jax.ShapeDtypeStruct((MOD_ROWS, 1, n), F32),
        compiler_params=_cparams(1),
        name="modulation",
    )(c_ctx.reshape(1, D_MODEL), c, w_mod, b_mod.reshape(1, n))


def _rope(x, cos, sa, sb):
    return (x * cos + pltpu.roll(x, HEAD_W - 16, axis=1) * sa
            + pltpu.roll(x, 16, axis=1) * sb)


def _in_proj_kernel(*refs, rope, seqs_per_sub):
    x_ref, x3_hbm, mod_ref, g_ref, w_ref = refs[:5]
    refs = refs[5:]
    if rope:
        cos_ref, sa_ref, sb_ref = refs[:3]
        refs = refs[3:]
        q_ref, k_ref, v_ref, ut_ref, wut_ref, xt_ref, xt_sem = refs
    else:
        q_ref, k_ref, v_ref, kc_ref, vc_ref, ut_ref, wut_ref, xt_ref, xt_sem = refs
    tile = pl.program_id(0)
    j = pl.program_id(1)
    t_per_sub = CHUNK // N_SUB
    t_early = CHUNK - t_per_sub

    def gather(tile_idx, t):
        src = x3_hbm.at[pl.ds(tile_idx * ROWS_PER_TILE, ROWS_PER_TILE), t, :]
        return pltpu.make_async_copy(src, xt_ref.at[t], xt_sem.at[t])

    @pl.when(j == 0)
    def _():
        @pl.when(tile == 0)
        def _():
            for t in range(t_early):
                gather(0, t).start()
        for t in range(t_early, CHUNK):
            gather(tile, t).start()

    t_base = j * t_per_sub
    for d in range(t_per_sub):
        gather(tile, t_base + d).wait()

    @pl.when((j == N_SUB - 1) & (tile + 1 < pl.num_programs(0)))
    def _():
        for t in range(t_early):
            gather(tile + 1, t).start()

    @pl.when((tile == 0) & (j == 0))
    def _():
        wut_ref[...] = w_ref[:, 3 * ATT_WIDTH:].T

    shift = mod_ref[:, 0:D_MODEL]
    gain = g_ref[...] * (1.0 + mod_ref[:, D_MODEL:2 * D_MODEL])

    def norm_mod(xv):
        ms = jnp.mean(xv * xv, axis=-1, keepdims=True)
        return (xv * lax.rsqrt(ms + NORM_EPS) * gain + shift).astype(BF16)

    tok = pl.ds(pl.multiple_of(j * SUB_TOK, SUB_TOK), SUB_TOK)
    proj = jnp.dot(norm_mod(x_ref[tok, :]), w_ref[:, 0:3 * ATT_WIDTH],
                   preferred_element_type=F32)
    q = proj[:, 0:ATT_WIDTH]
    k = proj[:, ATT_WIDTH:2 * ATT_WIDTH]
    v = proj[:, 2 * ATT_WIDTH:3 * ATT_WIDTH]
    qscale = HEAD_DIM ** -0.5 * math.log2(math.e)
    if rope:
        cos, sa, sb = cos_ref[...], sa_ref[...], sb_ref[...]
        for hd in range(N_HEADS):
            sl = slice(hd * HEAD_W, (hd + 1) * HEAD_W)
            q_ref[:, sl] = (_rope(q[:, sl], cos, sa, sb) * qscale).astype(q_ref.dtype)
            k_ref[:, sl] = _rope(k[:, sl], cos, sa, sb).astype(k_ref.dtype)
    else:
        q_ref[...] = (q * qscale).astype(q_ref.dtype)
        k_ref[...] = k.astype(k_ref.dtype)
        seq = SUB_TOK // seqs_per_sub
        for b in range(seqs_per_sub):
            for m in range(2 * N_HEADS):
                kc_ref[b, :, m, :] = k[b * seq:(b + 1) * seq, m * HEAD_DIM:(m + 1) * HEAD_DIM]
            for hd in range(N_HEADS):
                vc_ref[b, :, hd, :] = v[b * seq:(b + 1) * seq, hd * HEAD_W:(hd + 1) * HEAD_W]
    v_ref[...] = v.astype(v_ref.dtype)

    for d0 in range(0, t_per_sub, 2):
        xt = jnp.concatenate([xt_ref[t_base + d0], xt_ref[t_base + d0 + 1]], axis=0)
        ut = lax.dot_general(wut_ref[...], norm_mod(xt), NT_DIMS,
                             preferred_element_type=F32)
        for d in range(2):
            blk = ut[:, d * ROWS_PER_TILE:(d + 1) * ROWS_PER_TILE]
            row0 = pl.multiple_of((t_base + d0 + d) * SSM_GROUP, SSM_GROUP)
            ut_ref[:, pl.ds(row0, SSM_GROUP), :] = (
                blk.reshape(N_GROUPS, SSM_GROUP, ROWS_PER_TILE).astype(ut_ref.dtype))


def _in_proj(x, mods, mod_rows, g0, w_in, rope_tabs):
    n_batch, seq_len = x.shape[:2]
    n_tok = n_batch * seq_len
    n_rows = n_tok // CHUNK
    n_tiles = n_tok // TOK_PER_TILE
    mod_row0, n_mod = mod_rows
    tiles_per_mod = n_tiles // n_mod
    rope = rope_tabs is not None
    seqs_per_sub = max(1, SUB_TOK // seq_len)
    in_specs = [pl.BlockSpec((TOK_PER_TILE, D_MODEL), lambda i, j: (i, 0)),
                pl.BlockSpec(memory_space=pl.ANY),
                pl.BlockSpec((None, 1, 2 * D_MODEL),
                             lambda i, j: (mod_row0 + i // tiles_per_mod, 0, 0)),
                pl.BlockSpec((1, D_MODEL), lambda i, j: (0, 0)),
                pl.BlockSpec((D_MODEL, 4 * ATT_WIDTH), lambda i, j: (0, 0))]
    args = [x.reshape(n_tok, D_MODEL), x.reshape(n_rows, CHUNK, D_MODEL), mods, g0, w_in]
    row_spec = pl.BlockSpec((SUB_TOK, ATT_WIDTH), lambda i, j: (i * N_SUB + j, 0))
    row_shape = jax.ShapeDtypeStruct((n_tok, ATT_WIDTH), BF16)
    out_specs = [row_spec, row_spec, row_spec]
    out_shape = [row_shape, row_shape, row_shape]
    if rope:
        assert seq_len == TOK_PER_TILE
        for tab in rope_tabs:
            in_specs.append(pl.BlockSpec((SUB_TOK, HEAD_W), lambda i, j: (j, 0)))
            args.append(tab)
    else:
        out_specs += [pl.BlockSpec((seqs_per_sub, seq_len, 2 * N_HEADS, HEAD_DIM),
                                   lambda i, j: (i * N_SUB + j, 0, 0, 0)),
                      pl.BlockSpec((seqs_per_sub, seq_len, N_HEADS, HEAD_W),
                                   lambda i, j: (i * N_SUB + j, 0, 0, 0))]
        out_shape += [jax.ShapeDtypeStruct((n_batch, seq_len, 2 * N_HEADS, HEAD_DIM), F32),
                      jax.ShapeDtypeStruct((n_batch, seq_len, N_HEADS, HEAD_W), F32)]
    out_specs.append(pl.BlockSpec((N_GROUPS, CW, ROWS_PER_TILE), lambda i, j: (0, 0, i)))
    out_shape.append(jax.ShapeDtypeStruct((N_GROUPS, CW, n_rows), BF16))
    return pl.pallas_call(
        functools.partial(_in_proj_kernel, rope=rope, seqs_per_sub=seqs_per_sub),
        grid=(n_tiles, N_SUB),
        in_specs=in_specs,
        out_specs=out_specs,
        out_shape=out_shape,
        scratch_shapes=[pltpu.VMEM((SSM_WIDTH, D_MODEL), BF16),
                        pltpu.VMEM((CHUNK, ROWS_PER_TILE, D_MODEL), F32),
                        pltpu.SemaphoreType.DMA((CHUNK,))],
        compiler_params=_cparams(2),
        name="in_proj",
    )(*args)


def _attn_kernel(*refs, has_ctx, lam_init, n_seq, seq_len, tq, n_cast):
    if n_cast:
        cast_in = refs[len(refs) - 2 * n_cast - 1:len(refs) - n_cast - 1]
        cast_out = refs[len(refs) - n_cast:]
        refs = refs[:len(refs) - 2 * n_cast - 1] + (refs[len(refs) - n_cast - 1],)
        for src, dst in zip(cast_in, cast_out):
            dst[...] = src[...].astype(dst.dtype)
    if has_ctx:
        lam_ref, sg_ref, q_ref, ck_ref, cv_ref, k_ref, v_ref, o_ref = refs
    else:
        lam_ref, sg_ref, q_ref, k_ref, v_ref, o_ref = refs
    lp = lam_ref[...]
    lam = (jnp.exp(jnp.sum(lp[0:1] * lp[1:2], axis=-1, keepdims=True))
           - jnp.exp(jnp.sum(lp[2:3] * lp[3:4], axis=-1, keepdims=True)) + lam_init)
    first_map = lax.broadcasted_iota(jnp.int32, (1, HEAD_W), 1) < HEAD_DIM
    for b in range(n_seq):
        for hd in range(N_HEADS):
            sl = slice(hd * HEAD_W, (hd + 1) * HEAD_W)
            qh = q_ref[b * tq:(b + 1) * tq, sl]
            zero = jnp.zeros_like(qh)
            qs = jnp.concatenate([jnp.where(first_map, qh, zero),
                                  jnp.where(first_map, zero, qh)], axis=0)
            kv_rows = slice(b * seq_len, (b + 1) * seq_len)
            parts = [(k_ref[kv_rows, sl], v_ref[kv_rows, sl])]
            if has_ctx:
                parts.insert(0, (ck_ref[:, sl].astype(BF16), cv_ref[:, sl].astype(BF16)))
            scores = [lax.dot_general(qs, kk, NT_DIMS, preferred_element_type=F32)
                      for kk, _ in parts]
            mx = scores[0].max(axis=-1, keepdims=True)
            for s in scores[1:]:
                mx = jnp.maximum(mx, s.max(axis=-1, keepdims=True))
            acc = None
            for s, (_, vv) in zip(scores, parts):
                e = jnp.exp2(s - mx).astype(BF16)
                v_one = jnp.concatenate([vv, jnp.ones_like(vv)], axis=1)
                pv = jnp.dot(e, v_one, preferred_element_type=F32)
                acc = pv if acc is None else acc + pv
            num = acc[:, 0:HEAD_W] / acc[:, HEAD_W:2 * HEAD_W]
            o = num[0:tq] - lam * num[tq:2 * tq]
            o = _rms(o, sg_ref[...]) * (1.0 - lam_init)
            o_ref[b * tq:(b + 1) * tq, sl] = o.astype(o_ref.dtype)


def _attention(q, k, v, ctx_k, ctx_v, lam_params, subln_g, n_batch, seq_len, lam_init,
               cast_weights=()):
    has_ctx = ctx_k is not None
    tq = min(512, seq_len)
    n_q = seq_len // tq
    n_seq = 1 if n_q > 1 else min(4, n_batch)
    in_specs = [pl.BlockSpec((4, HEAD_DIM), lambda b, i: (0, 0)),
                pl.BlockSpec((1, HEAD_W), lambda b, i: (0, 0)),
                pl.BlockSpec((n_seq * tq, ATT_WIDTH), lambda b, i: (b * n_q + i, 0))]
    args = [lam_params, subln_g.reshape(1, HEAD_W), q]
    if has_ctx:
        past = ctx_k.shape[1]
        ctx_spec = pl.BlockSpec((None, past, ATT_WIDTH), lambda b, i: (b, 0, 0))
        in_specs += [ctx_spec, ctx_spec]
        args += [ctx_k, ctx_v]
    kv_spec = pl.BlockSpec((n_seq * seq_len, ATT_WIDTH), lambda b, i: (b, 0))
    in_specs += [kv_spec, kv_spec]
    args += [k, v]
    out_specs = [pl.BlockSpec((n_seq * tq, ATT_WIDTH), lambda b, i: (b * n_q + i, 0))]
    out_shape = [jax.ShapeDtypeStruct((n_batch * seq_len, ATT_WIDTH), BF16)]
    n_steps = (n_batch // n_seq) * n_q
    for w in cast_weights:
        rows = w.shape[0] // n_steps
        spec = pl.BlockSpec((rows, w.shape[1]), lambda b, i: (b * n_q + i, 0))
        in_specs.append(spec)
        args.append(w)
        out_specs.append(spec)
        out_shape.append(jax.ShapeDtypeStruct(w.shape, BF16))
    outs = pl.pallas_call(
        functools.partial(_attn_kernel, has_ctx=has_ctx, lam_init=lam_init,
                          n_seq=n_seq, seq_len=seq_len, tq=tq, n_cast=len(cast_weights)),
        grid=(n_batch // n_seq, n_q),
        in_specs=in_specs,
        out_specs=out_specs,
        out_shape=out_shape,
        compiler_params=_cparams(2),
        name="diff_attention",
    )(*args)
    return outs[0], tuple(outs[1:])


def _cmul(ar, ai, br, bi):
    return ar * br - ai * bi, ar * bi + ai * br


def _ssm_prep_kernel(lre_ref, lim_ref, ls_ref, bre_ref, bim_ref, cre_ref, cim_ref, d_ref,
                     mt_ref, gt_ref, wo_ref, at_ref):
    lane = lax.broadcasted_iota(jnp.int32, (SSM_GROUP, CW), 1)
    chan = lax.broadcasted_iota(jnp.int32, (SSM_GROUP, CW), 0)
    for gi in range(GROUP_BLOCK):
        gt_cols, wo_cols, at_cols, toeplitz = [], [], [], []
        for dr in range(2):
            lr = jnp.minimum(lre_ref[dr, gi], -1e-4)
            li = lim_ref[dr, gi]
            step = jnp.exp(ls_ref[dr, gi])
            mag = jnp.exp(lr * step)
            a_re = mag * jnp.cos(li * step)
            a_im = mag * jnp.sin(li * step)
            den = lr * lr + li * li
            nr = a_re - 1.0
            f_re = (nr * lr + a_im * li) / den
            f_im = (a_im * lr - nr * li) / den
            bt_re, bt_im = bre_ref[dr, gi], bim_ref[dr, gi]
            bb_re, bb_im = _cmul(f_re, f_im, bt_re, bt_im)
            c_re, c_im = cre_ref[dr, gi], cim_ref[dr, gi]
            pw = [(jnp.ones_like(a_re), jnp.zeros_like(a_im))]
            for _ in range(CHUNK):
                pw.append(_cmul(pw[-1][0], pw[-1][1], a_re, a_im))
            g_re, g_im, e_re, e_im = [], [], [], []
            for t in range(CHUNK):
                pr, pi = pw[CHUNK - 1 - t] if dr == 0 else pw[t]
                r, i = _cmul(bb_re, bb_im, pr, pi)
                g_re.append(r)
                g_im.append(i)
                pr, pi = pw[t + 1] if dr == 0 else pw[CHUNK - t]
                r, i = _cmul(c_re, c_im, pr, pi)
                e_re.append(r)
                e_im.append(-i)
            g_cat = jnp.concatenate([jnp.concatenate(g_re, axis=0),
                                     jnp.concatenate(g_im, axis=0)], axis=1)
            gt_cols.append(g_cat)
            wo_cols.append(jnp.concatenate([jnp.concatenate(e_re, axis=0),
                                            jnp.concatenate(e_im, axis=0)], axis=1))
            c_cat = jnp.concatenate([c_re, -c_im], axis=1)
            toeplitz.append(lax.dot_general(c_cat, g_cat, NT_DIMS,
                                            precision=lax.Precision.HIGHEST,
                                            preferred_element_type=F32))
            pr, pi = pw[CHUNK]
            rows_p, rows_q = [], []
            for _ in range(N_LEVELS - 1):
                rows_p.append(jnp.concatenate([pr, pr], axis=1))
                rows_q.append(jnp.concatenate([-pi, pi], axis=1))
                pr, pi = _cmul(pr, pi, pr, pi)
            rows_p.append(jnp.zeros_like(rows_p[0]))
            rows_q.append(jnp.zeros_like(rows_q[0]))
            at_cols += [jnp.concatenate(rows_p, axis=0), jnp.concatenate(rows_q, axis=0)]
        kf_rev, kb = toeplitz
        d_skip = d_ref[gi]
        blocks = []
        for t in range(CHUNK):
            fwd = pltpu.roll(kf_rev, (CW - (CHUNK - 1 - t) * SSM_GROUP) % CW, axis=1)
            bwd = pltpu.roll(kb, t * SSM_GROUP, axis=1)
            blocks.append(jnp.where(lane < (t + 1) * SSM_GROUP, fwd, 0.0)
                          + jnp.where(lane >= t * SSM_GROUP, bwd, 0.0)
                          + jnp.where(lane == chan + t * SSM_GROUP, d_skip, 0.0))
        mt_ref[gi] = jnp.concatenate(blocks, axis=0).astype(mt_ref.dtype)
        gt_ref[gi] = jnp.concatenate(gt_cols, axis=1).astype(gt_ref.dtype)
        wo_ref[gi] = jnp.concatenate(wo_cols, axis=1).astype(wo_ref.dtype)
        at_ref[gi] = jnp.concatenate(at_cols, axis=1)


def _ssm_prep(lam_re, lam_im, log_step, b_re, b_im, c_re, c_im, d_skip):
    row = lambda a: a.reshape(2, N_GROUPS, 1, SSM_STATE)
    bt = lambda a: jnp.swapaxes(a, 2, 3)
    d_row = jnp.tile((d_skip[0] + d_skip[1]).reshape(N_GROUPS, 1, SSM_GROUP), (1, 1, CHUNK))
    gb = GROUP_BLOCK
    vec_spec = pl.BlockSpec((2, gb, 1, SSM_STATE), lambda i: (0, i, 0, 0))
    mat_spec = pl.BlockSpec((2, gb, SSM_GROUP, SSM_STATE), lambda i: (0, i, 0, 0))
    w_spec = pl.BlockSpec((gb, CW, CW), lambda i: (i, 0, 0))
    w_shape = jax.ShapeDtypeStruct((N_GROUPS, CW, CW), BF16)
    return pl.pallas_call(
        _ssm_prep_kernel,
        grid=(N_GROUPS // gb,),
        in_specs=[vec_spec, vec_spec,
                  pl.BlockSpec((2, gb, 1, 1), lambda i: (0, i, 0, 0)),
                  mat_spec, mat_spec, mat_spec, mat_spec,
                  pl.BlockSpec((gb, 1, CW), lambda i: (i, 0, 0))],
        out_specs=[w_spec, w_spec, w_spec,
                   pl.BlockSpec((gb, N_LEVELS, 4 * 2 * SSM_STATE), lambda i: (i, 0, 0))],
        out_shape=[w_shape, w_shape, w_shape,
                   jax.ShapeDtypeStruct((N_GROUPS, N_LEVELS, 4 * 2 * SSM_STATE), F32)],
        compiler_params=_cparams(1),
        name="ssm_prep",
    )(row(lam_re), row(lam_im), log_step.reshape(2, N_GROUPS, 1, 1),
      bt(b_re), bt(b_im), c_re, c_im, d_row)


def _shift_rows(x, m, down):
    n = x.shape[0]
    return pltpu.roll(x, m if down else n - m, axis=0)


def _ssm_kernel(*refs, n_seq, has_h0):
    if has_h0:
        (xt_ref, mt_ref, gt_ref, wo_ref, at_ref, wg_ref, bg_ref, h0_ref,
         z_ref, zs_ref) = refs
    else:
        (xt_ref, mt_ref, gt_ref, wo_ref, at_ref, wg_ref, bg_ref,
         z_ref, st_ref, zs_ref, fin_ref) = refs
    step = pl.program_id(0)
    n_rows = xt_ref.shape[-1]
    seg = n_rows // n_seq
    cw2 = 2 * SSM_STATE
    pos = lax.broadcasted_iota(jnp.int32, (n_rows, cw2), 0) % seg
    for gi in range(GROUP_BLOCK):
        xt = xt_ref[gi]
        at = at_ref[gi]
        s_all = lax.dot_general(xt, gt_ref[gi], TN_DIMS, preferred_element_type=F32)
        h_parts = []
        for dr in range(2):
            s = s_all[:, dr * cw2:(dr + 1) * cw2]
            p_tab = at[:, (2 * dr) * cw2:(2 * dr + 1) * cw2]
            q_tab = at[:, (2 * dr + 1) * cw2:(2 * dr + 2) * cw2]
            down = dr == 0
            if has_h0:
                h0 = jnp.zeros((n_rows, cw2), F32)
                riota = lax.broadcasted_iota(jnp.int32, (n_rows, cw2), 0)
                for b in range(n_seq):
                    edge = b * seg if down else b * seg + seg - 1
                    h0 = jnp.where(riota == edge, h0_ref[gi, b:b + 1, dr * cw2:(dr + 1) * cw2], h0)
                s = s + p_tab[0:1] * h0 + q_tab[0:1] * pltpu.roll(h0, SSM_STATE, axis=1)
            m, lvl = 1, 0
            while m < seg:
                sh = _shift_rows(s, m, down)
                valid = (pos >= m) if down else (pos < seg - m)
                sh = jnp.where(valid, sh, 0.0)
                s = s + p_tab[lvl:lvl + 1] * sh + q_tab[lvl:lvl + 1] * pltpu.roll(sh, SSM_STATE, axis=1)
                m, lvl = 2 * m, lvl + 1
            if not has_h0:
                fin_ref[dr] = s
                st_ref[gi, :, dr * cw2:(dr + 1) * cw2] = (
                    fin_ref[dr, pl.ds(seg - 1 if down else 0, n_seq, stride=seg), :])
            ent = _shift_rows(s, 1, down)
            ent = jnp.where((pos >= 1) if down else (pos < seg - 1), ent, 0.0)
            if has_h0:
                ent = ent + h0
            h_parts.append(ent)
        h_all = jnp.concatenate(h_parts, axis=1).astype(BF16)
        yt = (jnp.dot(mt_ref[gi], xt, preferred_element_type=F32)
              + lax.dot_general(wo_ref[gi], h_all, NT_DIMS, preferred_element_type=F32))
        z = jax.nn.gelu(yt, approximate=True)
        grp = step * GROUP_BLOCK + gi
        for t in range(CHUNK):
            zs_ref[t, pl.ds(pl.multiple_of(grp * SSM_GROUP, SSM_GROUP), SSM_GROUP), :] = (
                z[t * SSM_GROUP:(t + 1) * SSM_GROUP, :])

    @pl.when(step == pl.num_programs(0) - 1)
    def _():
        for t in range(CHUNK):
            zt = zs_ref[t]
            gate = jnp.dot(wg_ref[...], zt.astype(BF16),
                           preferred_element_type=F32) + bg_ref[...]
            z_ref[t] = (zt * jax.nn.sigmoid(gate)).astype(z_ref.dtype)


def _ssm(xt, mt, gt, wo, at, w_glu_t, b_glu_col, h0, n_seq):
    n_rows = xt.shape[-1]
    has_h0 = h0 is not None
    gb = GROUP_BLOCK
    w_spec = pl.BlockSpec((gb, CW, CW), lambda i: (i, 0, 0))
    in_specs = [pl.BlockSpec((gb, CW, n_rows), lambda i: (i, 0, 0)),
                w_spec, w_spec, w_spec,
                pl.BlockSpec((gb, N_LEVELS, 4 * 2 * SSM_STATE), lambda i: (i, 0, 0)),
                pl.BlockSpec((SSM_WIDTH, SSM_WIDTH), lambda i: (0, 0)),
                pl.BlockSpec((SSM_WIDTH, 1), lambda i: (0, 0))]
    args = [xt, mt, gt, wo, at, w_glu_t, b_glu_col]
    out_specs = [pl.BlockSpec((CHUNK, SSM_WIDTH, n_rows), lambda i: (0, 0, 0))]
    out_shape = [jax.ShapeDtypeStruct((CHUNK, SSM_WIDTH, n_rows), BF16)]
    scratch = [pltpu.VMEM((CHUNK, SSM_WIDTH, n_rows), F32)]
    if has_h0:
        in_specs.append(pl.BlockSpec((gb, n_seq, CW), lambda i: (i, 0, 0)))
        args.append(h0)
    else:
        out_specs.append(pl.BlockSpec((gb, n_seq, CW), lambda i: (i, 0, 0)))
        out_shape.append(jax.ShapeDtypeStruct((N_GROUPS, n_seq, CW), F32))
        scratch.append(pltpu.VMEM((2, n_rows, 2 * SSM_STATE), F32))
    return pl.pallas_call(
        functools.partial(_ssm_kernel, n_seq=n_seq, has_h0=has_h0),
        grid=(N_GROUPS // gb,),
        in_specs=in_specs,
        out_specs=out_specs,
        out_shape=out_shape,
        scratch_shapes=scratch,
        compiler_params=_cparams(1),
        name="ssm_scan_glu",
    )(*args)


FF_CHUNK = 256


def _out_ffn_kernel(x_ref, attn_ref, zt_ref, mod_ref, g_ref, wo_ref, wfi_ref, wfo_ref,
                    o_ref, mixz_ref, act_ref):
    j = pl.program_id(1)

    @pl.when(j == 0)
    def _():
        for t in range(CHUNK):
            part = lax.dot_general(zt_ref[t], wo_ref[ATT_WIDTH:, :], TN_DIMS,
                                   preferred_element_type=F32)
            for cb in range(N_COL_BLOCKS):
                mixz_ref[cb, pl.ds(t, ROWS_PER_TILE, stride=CHUNK), :] = (
                    part[:, cb * LANES:(cb + 1) * LANES])

    gate1 = mod_ref[:, 2 * D_MODEL:3 * D_MODEL]
    shift2 = mod_ref[:, 3 * D_MODEL:4 * D_MODEL]
    scale2 = mod_ref[:, 4 * D_MODEL:5 * D_MODEL]
    gate2 = mod_ref[:, 5 * D_MODEL:6 * D_MODEL]
    rows = pl.ds(pl.multiple_of(j * SUB_TOK, SUB_TOK), SUB_TOK)
    mix = (jnp.dot(attn_ref[...], wo_ref[0:ATT_WIDTH, :], preferred_element_type=F32)
           + jnp.concatenate([mixz_ref[cb, rows, :] for cb in range(N_COL_BLOCKS)], axis=1))
    x1 = x_ref[...] + gate1 * _rms(mix, g_ref[1:2, :])
    h = (_rms(x1, g_ref[2:3, :]) * (1.0 + scale2) + shift2).astype(BF16)
    for c in range(D_FF // FF_CHUNK):
        lo = c * FF_CHUNK
        gt = jnp.dot(h, wfi_ref[:, lo:lo + FF_CHUNK], preferred_element_type=F32)
        up = jnp.dot(h, wfi_ref[:, D_FF + lo:D_FF + lo + FF_CHUNK], preferred_element_type=F32)
        act_ref[:, lo:lo + FF_CHUNK] = (_silu(gt) * up).astype(BF16)
    f = jnp.dot(act_ref[...], wfo_ref[...], preferred_element_type=F32)
    o_ref[...] = x1 + gate2 * _rms(f, g_ref[3:4, :])


def _out_ffn(x2d, attn, zt, mods, mod_rows, norm_g, w_o, w_ffn_in, w_ffn_out):
    n_tok = x2d.shape[0]
    n_tiles = n_tok // TOK_PER_TILE
    mod_row0, n_mod = mod_rows
    tiles_per_mod = n_tiles // n_mod
    const = lambda i, j: (0, 0)
    row_spec = lambda w: pl.BlockSpec((SUB_TOK, w), lambda i, j: (i * N_SUB + j, 0))
    return pl.pallas_call(
        _out_ffn_kernel,
        grid=(n_tiles, N_SUB),
        in_specs=[row_spec(D_MODEL), row_spec(ATT_WIDTH),
                  pl.BlockSpec((CHUNK, SSM_WIDTH, ROWS_PER_TILE), lambda i, j: (0, 0, i)),
                  pl.BlockSpec((None, 1, N_MOD * D_MODEL),
                               lambda i, j: (mod_row0 + i // tiles_per_mod, 0, 0)),
                  pl.BlockSpec((4, D_MODEL), const),
                  pl.BlockSpec((2 * ATT_WIDTH, D_MODEL), const, pipeline_mode=pl.Buffered(1)),
                  pl.BlockSpec((D_MODEL, 2 * D_FF), const, pipeline_mode=pl.Buffered(1)),
                  pl.BlockSpec((D_FF, D_MODEL), const, pipeline_mode=pl.Buffered(1))],
        out_specs=row_spec(D_MODEL),
        out_shape=jax.ShapeDtypeStruct((n_tok, D_MODEL), F32),
        scratch_shapes=[pltpu.VMEM((N_COL_BLOCKS, TOK_PER_TILE, LANES), F32),
                        pltpu.VMEM((SUB_TOK, D_FF), BF16)],
        compiler_params=_cparams(2),
        name="out_proj_ffn",
    )(x2d, attn, zt, mods, norm_g, w_o, w_ffn_in, w_ffn_out)


def _rope_tables(seq_len):
    t = np.arange(seq_len)
    row = (t // GRID_W).astype(np.float32)
    col = (t % GRID_W).astype(np.float32)
    half = HEAD_DIM // 2
    inv_freq = (np.float32(ROPE_BASE)
                ** (-np.arange(0, half, 2, dtype=np.float32) / np.float32(half))).astype(np.float32)
    ang_r = row[:, None] * inv_freq
    ang_c = col[:, None] * inv_freq
    ang = np.concatenate([ang_r, ang_r, ang_c, ang_c], axis=-1)
    cos, sin = np.cos(ang), np.sin(ang)
    upper = (np.arange(HEAD_DIM) % 32) < 16
    sa = np.where(upper, -sin, 0.0)
    sb = np.where(upper, 0.0, sin)
    two = lambda a: jnp.asarray(np.concatenate([a, a], axis=-1), dtype=F32)
    return two(cos), two(sa), two(sb)


def _layer(x, mods, mod_rows, lam_init, rope_tabs, ctx_k, ctx_v, h0, weights, prep):
    n_batch, seq_len = x.shape[:2]
    g = weights['norm_g']
    outs = _in_proj(x, mods, mod_rows, g[0:1], weights['w_in'], rope_tabs)
    q, k, v = outs[:3]
    pending = () if 'late_bf16' in weights else weights['late_f32']
    attn, cast = _attention(q, k, v, ctx_k, ctx_v, weights['lam'], weights['subln_g'],
                            n_batch, seq_len, lam_init, cast_weights=pending)
    if pending:
        weights['late_bf16'] = cast
    ssm_out = _ssm(outs[-1], *prep, weights['w_glu_t'], weights['b_glu_col'], h0, n_batch)
    y = _out_ffn(x.reshape(n_batch * seq_len, D_MODEL), attn, ssm_out[0], mods, mod_rows, g,
                 *weights['late_bf16'])
    return y.reshape(x.shape), outs[3:-1], ssm_out[1:]


def kernel(x_prompt, x_sample, cache_k, cache_v, state_ssm_re, state_ssm_im, c, c_ctx, w_mod, b_mod, norm_g, w_in, lam_params, subln_g, ssm_lambda_re, ssm_lambda_im, ssm_log_step, ssm_b_re, ssm_b_im, ssm_c_re, ssm_c_im, ssm_d, w_glu, b_glu, w_o, w_ffn_in, w_ffn_out):
    depth = w_mod.shape[0]
    assert depth == 1
    bp = x_prompt.shape[0]
    bd, ld_len = x_sample.shape[:2]
    past = cache_k.shape[2]
    xp, xs = x_prompt, x_sample
    rope_tabs = _rope_tables(ld_len)
    ks_out, vs_out, hr_out, hi_out = [], [], [], []
    for l in range(depth):
        lam_init = 0.8 - 0.6 * math.exp(-0.3 * l)
        mods = _modulation(c_ctx, c, w_mod[l], b_mod[l])
        weights = {
            'norm_g': norm_g[l],
            'w_in': w_in[l].astype(BF16),
            'lam': lam_params[l], 'subln_g': subln_g[l],
            'w_glu_t': w_glu[l].T.astype(BF16), 'b_glu_col': b_glu[l].reshape(SSM_WIDTH, 1),
            'late_f32': (w_o[l], w_ffn_in[l], w_ffn_out[l]),
        }
        prep = _ssm_prep(ssm_lambda_re[l], ssm_lambda_im[l], ssm_log_step[l],
                         ssm_b_re[l], ssm_b_im[l], ssm_c_re[l], ssm_c_im[l], ssm_d[l])
        ck = cache_k[:, l].reshape(bd, past, ATT_WIDTH)
        cv = cache_v[:, l].reshape(bd, past, ATT_WIDTH)
        h0 = jnp.stack([state_ssm_re[:, l], state_ssm_im[:, l]], axis=2)
        h0 = h0.transpose(3, 0, 1, 2, 4).reshape(N_GROUPS, bd, CW)
        xs, _, _ = _layer(xs, mods, (1, bd), lam_init, rope_tabs, ck, cv, h0, weights, prep)
        xp, (k_ctx, v_ctx), (st,) = _layer(xp, mods, (0, 1), lam_init, None, None, None, None,
                                           weights, prep)
        ks_out.append(k_ctx)
        vs_out.append(v_ctx)
        fin = st.reshape(N_GROUPS, bp, 2, 2, SSM_STATE).transpose(1, 2, 3, 0, 4)
        hr_out.append(fin[:, :, 0])
        hi_out.append(fin[:, :, 1])
    return (xp, xs, jnp.stack(ks_out, axis=1), jnp.stack(vs_out, axis=1),
            jnp.stack(hr_out, axis=1), jnp.stack(hi_out, axis=1))
```

```python
import functools
import math

import jax
import jax.numpy as jnp
import numpy as np
from jax import lax
from jax.experimental import pallas as pl
from jax.experimental.pallas import tpu as pltpu

F32 = jnp.float32
BF16 = jnp.bfloat16

D_MODEL = 1024
GRID_W = 64
ATT_WIDTH = 512
SSM_WIDTH = 512
HEAD_DIM = 64
N_HEADS = 4
HEAD_W = 2 * HEAD_DIM
SSM_GROUP = 16
N_GROUPS = 32
SSM_STATE = 64
D_FF = 2816
N_MOD = 6
ROPE_BASE = 10000.0
NORM_EPS = 1e-6

LANES = 128
N_COL_BLOCKS = D_MODEL // LANES
CHUNK = 16
CW = CHUNK * SSM_GROUP
N_LEVELS = 8
GROUP_BLOCK = 4

ROWS_PER_TILE = 128
TOK_PER_TILE = ROWS_PER_TILE * CHUNK
SUB_TOK = 512
N_SUB = TOK_PER_TILE // SUB_TOK

VMEM_LIMIT = 56 * 1024 * 1024

NT_DIMS = (((1,), (1,)), ((), ()))
TN_DIMS = (((0,), (0,)), ((), ()))


def _cparams(n_axes):
    return pltpu.CompilerParams(
        dimension_semantics=("arbitrary",) * n_axes,
        vmem_limit_bytes=VMEM_LIMIT)


def _rms(x, g):
    ms = jnp.mean(x * x, axis=-1, keepdims=True)
    return x * lax.rsqrt(ms + NORM_EPS) * g


def _silu(x):
    return x * jax.nn.sigmoid(x)


MOD_ROWS = 8


def _mod_kernel(ctx_ref, c_ref, w_ref, b_ref, o_ref):
    n_lat = c_ref.shape[0]
    row = lax.broadcasted_iota(jnp.int32, (MOD_ROWS, D_MODEL), 0)
    cond = jnp.where(row == 0, ctx_ref[...], 0.0)
    for b in range(n_lat):
        cond = jnp.where(row == 1 + b, c_ref[b:b + 1, :], cond)
    m = jnp.dot(_silu(cond).astype(BF16), w_ref[...].astype(BF16),
                preferred_element_type=F32) + b_ref[...]
    o_ref[:, 0, :] = m


def _modulation(c_ctx, c, w_mod, b_mod):
    n = w_mod.shape[1]
    tn = 2048
    assert 1 + c.shape[0] <= MOD_ROWS
    return pl.pallas_call(
        _mod_kernel,
        grid=(n // tn,),
        in_specs=[pl.BlockSpec((1, D_MODEL), lambda i: (0, 0)),
                  pl.BlockSpec(c.shape, lambda i: (0, 0)),
                  pl.BlockSpec((D_MODEL, tn), lambda i: (0, i)),
                  pl.BlockSpec((1, tn), lambda i: (0, i))],
        out_specs=pl.BlockSpec((MOD_ROWS, 1, tn), lambda i: (0, 0, i)),
        out_shape=jax.ShapeDtypeStruct((MOD_ROWS, 1, n), F32),
        compiler_params=_cparams(1),
        name="modulation",
    )(c_ctx.reshape(1, D_MODEL), c, w_mod, b_mod.reshape(1, n))


def _rope(x, cos, sa, sb):
    return (x * cos + pltpu.roll(x, HEAD_W - 16, axis=1) * sa
            + pltpu.roll(x, 16, axis=1) * sb)


def _in_proj_kernel(*refs, rope, seqs_per_sub):
    x_ref, x3_hbm, mod_ref, g_ref, w_ref = refs[:5]
    refs = refs[5:]
    if rope:
        cos_ref, sa_ref, sb_ref = refs[:3]
        refs = refs[3:]
        q_ref, k_ref, v_ref, ut_ref, wut_ref, xt_ref, xt_sem = refs
    else:
        q_ref, k_ref, v_ref, kc_ref, vc_ref, ut_ref, wut_ref, xt_ref, xt_sem = refs
    tile = pl.program_id(0)
    j = pl.program_id(1)
    t_per_sub = CHUNK // N_SUB
    t_early = CHUNK - t_per_sub

    def gather(tile_idx, t):
        src = x3_hbm.at[pl.ds(tile_idx * ROWS_PER_TILE, ROWS_PER_TILE), t, :]
        return pltpu.make_async_copy(src, xt_ref.at[t], xt_sem.at[t])

    @pl.when(j == 0)
    def _():
        @pl.when(tile == 0)
        def _():
            for t in range(t_early):
                gather(0, t).start()
        for t in range(t_early, CHUNK):
            gather(tile, t).start()

    t_base = j * t_per_sub
    for d in range(t_per_sub):
        gather(tile, t_base + d).wait()

    @pl.when((j == N_SUB - 1) & (tile + 1 < pl.num_programs(0)))
    def _():
        for t in range(t_early):
            gather(tile + 1, t).start()

    @pl.when((tile == 0) & (j == 0))
    def _():
        wut_ref[...] = w_ref[:, 3 * ATT_WIDTH:].T

    shift = mod_ref[:, 0:D_MODEL]
    gain = g_ref[...] * (1.0 + mod_ref[:, D_MODEL:2 * D_MODEL])

    def norm_mod(xv):
        ms = jnp.mean(xv * xv, axis=-1, keepdims=True)
        return (xv * lax.rsqrt(ms + NORM_EPS) * gain + shift).astype(BF16)

    tok = pl.ds(pl.multiple_of(j * SUB_TOK, SUB_TOK), SUB_TOK)
    proj = jnp.dot(norm_mod(x_ref[tok, :]), w_ref[:, 0:3 * ATT_WIDTH],
                   preferred_element_type=F32)
    q = proj[:, 0:ATT_WIDTH]
    k = proj[:, ATT_WIDTH:2 * ATT_WIDTH]
    v = proj[:, 2 * ATT_WIDTH:3 * ATT_WIDTH]
    qscale = HEAD_DIM ** -0.5 * math.log2(math.e)
    if rope:
        cos, sa, sb = cos_ref[...], sa_ref[...], sb_ref[...]
        for hd in range(N_HEADS):
            sl = slice(hd * HEAD_W, (hd + 1) * HEAD_W)
            q_ref[:, sl] = (_rope(q[:, sl], cos, sa, sb) * qscale).astype(q_ref.dtype)
            k_ref[:, sl] = _rope(k[:, sl], cos, sa, sb).astype(k_ref.dtype)
    else:
        q_ref[...] = (q * qscale).astype(q_ref.dtype)
        k_ref[...] = k.astype(k_ref.dtype)
        seq = SUB_TOK // seqs_per_sub
        k_t = k.T
        for b in range(seqs_per_sub):
            kc_ref[b] = k_t[:, b * seq:(b + 1) * seq]
            for hd in range(N_HEADS):
                vc_ref[b, :, hd, :] = v[b * seq:(b + 1) * seq, hd * HEAD_W:(hd + 1) * HEAD_W]
    v_ref[...] = v.astype(v_ref.dtype)

    for d0 in range(0, t_per_sub, 2):
        xt = jnp.concatenate([xt_ref[t_base + d0], xt_ref[t_base + d0 + 1]], axis=0)
        ut = lax.dot_general(wut_ref[...], norm_mod(xt), NT_DIMS,
                             preferred_element_type=F32)
        for d in range(2):
            blk = ut[:, d * ROWS_PER_TILE:(d + 1) * ROWS_PER_TILE]
            row0 = pl.multiple_of((t_base + d0 + d) * SSM_GROUP, SSM_GROUP)
            ut_ref[:, pl.ds(row0, SSM_GROUP), :] = (
                blk.reshape(N_GROUPS, SSM_GROUP, ROWS_PER_TILE).astype(ut_ref.dtype))


def _in_proj(x, mods, mod_rows, g0, w_in, rope_tabs):
    n_batch, seq_len = x.shape[:2]
    n_tok = n_batch * seq_len
    n_rows = n_tok // CHUNK
    n_tiles = n_tok // TOK_PER_TILE
    mod_row0, n_mod = mod_rows
    tiles_per_mod = n_tiles // n_mod
    rope = rope_tabs is not None
    seqs_per_sub = max(1, SUB_TOK // seq_len)
    in_specs = [pl.BlockSpec((TOK_PER_TILE, D_MODEL), lambda i, j: (i, 0)),
                pl.BlockSpec(memory_space=pl.ANY),
                pl.BlockSpec((None, 1, 2 * D_MODEL),
                             lambda i, j: (mod_row0 + i // tiles_per_mod, 0, 0)),
                pl.BlockSpec((1, D_MODEL), lambda i, j: (0, 0)),
                pl.BlockSpec((D_MODEL, 4 * ATT_WIDTH), lambda i, j: (0, 0))]
    args = [x.reshape(n_tok, D_MODEL), x.reshape(n_rows, CHUNK, D_MODEL), mods, g0, w_in]
    row_spec = pl.BlockSpec((SUB_TOK, ATT_WIDTH), lambda i, j: (i * N_SUB + j, 0))
    row_shape = jax.ShapeDtypeStruct((n_tok, ATT_WIDTH), BF16)
    out_specs = [row_spec, row_spec, row_spec]
    out_shape = [row_shape, row_shape, row_shape]
    if rope:
        assert seq_len == TOK_PER_TILE
        for tab in rope_tabs:
            in_specs.append(pl.BlockSpec((SUB_TOK, HEAD_W), lambda i, j: (j, 0)))
            args.append(tab)
    else:
        out_specs += [pl.BlockSpec((seqs_per_sub, ATT_WIDTH, seq_len),
                                   lambda i, j: (i * N_SUB + j, 0, 0)),
                      pl.BlockSpec((seqs_per_sub, seq_len, N_HEADS, HEAD_W),
                                   lambda i, j: (i * N_SUB + j, 0, 0, 0))]
        out_shape += [jax.ShapeDtypeStruct((n_batch, ATT_WIDTH, seq_len), F32),
                      jax.ShapeDtypeStruct((n_batch, seq_len, N_HEADS, HEAD_W), F32)]
    out_specs.append(pl.BlockSpec((N_GROUPS, CW, ROWS_PER_TILE), lambda i, j: (0, 0, i)))
    out_shape.append(jax.ShapeDtypeStruct((N_GROUPS, CW, n_rows), BF16))
    return pl.pallas_call(
        functools.partial(_in_proj_kernel, rope=rope, seqs_per_sub=seqs_per_sub),
        grid=(n_tiles, N_SUB),
        in_specs=in_specs,
        out_specs=out_specs,
        out_shape=out_shape,
        scratch_shapes=[pltpu.VMEM((SSM_WIDTH, D_MODEL), BF16),
                        pltpu.VMEM((CHUNK, ROWS_PER_TILE, D_MODEL), F32),
                        pltpu.SemaphoreType.DMA((CHUNK,))],
        compiler_params=_cparams(2),
        name="in_proj",
    )(*args)


def _attn_kernel(*refs, has_ctx, lam_init, n_seq, seq_len, tq, n_cast):
    if n_cast:
        cast_in = refs[len(refs) - 2 * n_cast - 1:len(refs) - n_cast - 1]
        cast_out = refs[len(refs) - n_cast:]
        refs = refs[:len(refs) - 2 * n_cast - 1] + (refs[len(refs) - n_cast - 1],)
        for src, dst in zip(cast_in, cast_out):
            dst[...] = src[...].astype(dst.dtype)
    if has_ctx:
        lam_ref, sg_ref, q_ref, ck_ref, cv_ref, k_ref, v_ref, o_ref = refs
    else:
        lam_ref, sg_ref, q_ref, k_ref, v_ref, o_ref = refs
    lp = lam_ref[...]
    lam = (jnp.exp(jnp.sum(lp[0:1] * lp[1:2], axis=-1, keepdims=True))
           - jnp.exp(jnp.sum(lp[2:3] * lp[3:4], axis=-1, keepdims=True)) + lam_init)
    first_map = lax.broadcasted_iota(jnp.int32, (1, HEAD_W), 1) < HEAD_DIM
    for b in range(n_seq):
        for hd in range(N_HEADS):
            sl = slice(hd * HEAD_W, (hd + 1) * HEAD_W)
            qh = q_ref[b * tq:(b + 1) * tq, sl]
            zero = jnp.zeros_like(qh)
            qs = jnp.concatenate([jnp.where(first_map, qh, zero),
                                  jnp.where(first_map, zero, qh)], axis=0)
            kv_rows = slice(b * seq_len, (b + 1) * seq_len)
            parts = [(k_ref[kv_rows, sl], v_ref[kv_rows, sl])]
            if has_ctx:
                parts.insert(0, (ck_ref[:, sl].astype(BF16), cv_ref[:, sl].astype(BF16)))
            scores = [lax.dot_general(qs, kk, NT_DIMS, preferred_element_type=F32)
                      for kk, _ in parts]
            mx = scores[0].max(axis=-1, keepdims=True)
            for s in scores[1:]:
                mx = jnp.maximum(mx, s.max(axis=-1, keepdims=True))
            acc = None
            for s, (_, vv) in zip(scores, parts):
                e = jnp.exp2(s - mx).astype(BF16)
                v_one = jnp.concatenate([vv, jnp.ones_like(vv)], axis=1)
                pv = jnp.dot(e, v_one, preferred_element_type=F32)
                acc = pv if acc is None else acc + pv
            num = acc[:, 0:HEAD_W] / acc[:, HEAD_W:2 * HEAD_W]
            o = num[0:tq] - lam * num[tq:2 * tq]
            o = _rms(o, sg_ref[...]) * (1.0 - lam_init)
            o_ref[b * tq:(b + 1) * tq, sl] = o.astype(o_ref.dtype)


def _attention(q, k, v, ctx_k, ctx_v, lam_params, subln_g, n_batch, seq_len, lam_init,
               cast_weights=()):
    has_ctx = ctx_k is not None
    tq = min(512, seq_len)
    n_q = seq_len // tq
    n_seq = 1 if n_q > 1 else min(4, n_batch)
    in_specs = [pl.BlockSpec((4, HEAD_DIM), lambda b, i: (0, 0)),
                pl.BlockSpec((1, HEAD_W), lambda b, i: (0, 0)),
                pl.BlockSpec((n_seq * tq, ATT_WIDTH), lambda b, i: (b * n_q + i, 0))]
    args = [lam_params, subln_g.reshape(1, HEAD_W), q]
    if has_ctx:
        past = ctx_k.shape[1]
        ctx_spec = pl.BlockSpec((None, past, ATT_WIDTH), lambda b, i: (b, 0, 0))
        in_specs += [ctx_spec, ctx_spec]
        args += [ctx_k, ctx_v]
    kv_spec = pl.BlockSpec((n_seq * seq_len, ATT_WIDTH), lambda b, i: (b, 0))
    in_specs += [kv_spec, kv_spec]
    args += [k, v]
    out_specs = [pl.BlockSpec((n_seq * tq, ATT_WIDTH), lambda b, i: (b * n_q + i, 0))]
    out_shape = [jax.ShapeDtypeStruct((n_batch * seq_len, ATT_WIDTH), BF16)]
    n_steps = (n_batch // n_seq) * n_q
    for w in cast_weights:
        rows = w.shape[0] // n_steps
        spec = pl.BlockSpec((rows, w.shape[1]), lambda b, i: (b * n_q + i, 0))
        in_specs.append(spec)
        args.append(w)
        out_specs.append(spec)
        out_shape.append(jax.ShapeDtypeStruct(w.shape, BF16))
    outs = pl.pallas_call(
        functools.partial(_attn_kernel, has_ctx=has_ctx, lam_init=lam_init,
                          n_seq=n_seq, seq_len=seq_len, tq=tq, n_cast=len(cast_weights)),
        grid=(n_batch // n_seq, n_q),
        in_specs=in_specs,
        out_specs=out_specs,
        out_shape=out_shape,
        compiler_params=_cparams(2),
        name="diff_attention",
    )(*args)
    return outs[0], tuple(outs[1:])


def _cmul(ar, ai, br, bi):
    return ar * br - ai * bi, ar * bi + ai * br


def _ssm_prep_kernel(lre_ref, lim_ref, ls_ref, bre_ref, bim_ref, cre_ref, cim_ref, d_ref,
                     mt_ref, gt_ref, wo_ref, at_ref):
    lane = lax.broadcasted_iota(jnp.int32, (SSM_GROUP, CW), 1)
    chan = lax.broadcasted_iota(jnp.int32, (SSM_GROUP, CW), 0)
    for gi in range(GROUP_BLOCK):
        gt_cols, wo_cols, at_cols, toeplitz = [], [], [], []
        for dr in range(2):
            lr = jnp.minimum(lre_ref[dr, gi], -1e-4)
            li = lim_ref[dr, gi]
            step = jnp.exp(ls_ref[dr, gi])
            mag = jnp.exp(lr * step)
            a_re = mag * jnp.cos(li * step)
            a_im = mag * jnp.sin(li * step)
            den = lr * lr + li * li
            nr = a_re - 1.0
            f_re = (nr * lr + a_im * li) / den
            f_im = (a_im * lr - nr * li) / den
            bt_re, bt_im = bre_ref[dr, gi], bim_ref[dr, gi]
            bb_re, bb_im = _cmul(f_re, f_im, bt_re, bt_im)
            c_re, c_im = cre_ref[dr, gi], cim_ref[dr, gi]
            pw = [(jnp.ones_like(a_re), jnp.zeros_like(a_im))]
            for _ in range(CHUNK):
                pw.append(_cmul(pw[-1][0], pw[-1][1], a_re, a_im))
            g_re, g_im, e_re, e_im = [], [], [], []
            for t in range(CHUNK):
                pr, pi = pw[CHUNK - 1 - t] if dr == 0 else pw[t]
                r, i = _cmul(bb_re, bb_im, pr, pi)
                g_re.append(r)
                g_im.append(i)
                pr, pi = pw[t + 1] if dr == 0 else pw[CHUNK - t]
                r, i = _cmul(c_re, c_im, pr, pi)
                e_re.append(r)
                e_im.append(-i)
            g_cat = jnp.concatenate([jnp.concatenate(g_re, axis=0),
                                     jnp.concatenate(g_im, axis=0)], axis=1)
            gt_cols.append(g_cat)
            wo_cols.append(jnp.concatenate([jnp.concatenate(e_re, axis=0),
                                            jnp.concatenate(e_im, axis=0)], axis=1))
            c_cat = jnp.concatenate([c_re, -c_im], axis=1)
            toeplitz.append(lax.dot_general(c_cat, g_cat, NT_DIMS,
                                            precision=lax.Precision.HIGHEST,
                                            preferred_element_type=F32))
            pr, pi = pw[CHUNK]
            rows_p, rows_q = [], []
            for _ in range(N_LEVELS - 1):
                rows_p.append(jnp.concatenate([pr, pr], axis=1))
                rows_q.append(jnp.concatenate([-pi, pi], axis=1))
                pr, pi = _cmul(pr, pi, pr, pi)
            rows_p.append(jnp.zeros_like(rows_p[0]))
            rows_q.append(jnp.zeros_like(rows_q[0]))
            at_cols += [jnp.concatenate(rows_p, axis=0), jnp.concatenate(rows_q, axis=0)]
        kf_rev, kb = toeplitz
        d_skip = d_ref[gi]
        blocks = []
        for t in range(CHUNK):
            fwd = pltpu.roll(kf_rev, (CW - (CHUNK - 1 - t) * SSM_GROUP) % CW, axis=1)
            bwd = pltpu.roll(kb, t * SSM_GROUP, axis=1)
            blocks.append(jnp.where(lane < (t + 1) * SSM_GROUP, fwd, 0.0)
                          + jnp.where(lane >= t * SSM_GROUP, bwd, 0.0)
                          + jnp.where(lane == chan + t * SSM_GROUP, d_skip, 0.0))
        mt_ref[gi] = jnp.concatenate(blocks, axis=0).astype(mt_ref.dtype)
        gt_ref[gi] = jnp.concatenate(gt_cols, axis=1).astype(gt_ref.dtype)
        wo_ref[gi] = jnp.concatenate(wo_cols, axis=1).astype(wo_ref.dtype)
        at_ref[gi] = jnp.concatenate(at_cols, axis=1)


def _ssm_prep(lam_re, lam_im, log_step, b_re, b_im, c_re, c_im, d_skip):
    row = lambda a: a.reshape(2, N_GROUPS, 1, SSM_STATE)
    bt = lambda a: jnp.swapaxes(a, 2, 3)
    d_row = jnp.tile((d_skip[0] + d_skip[1]).reshape(N_GROUPS, 1, SSM_GROUP), (1, 1, CHUNK))
    gb = GROUP_BLOCK
    vec_spec = pl.BlockSpec((2, gb, 1, SSM_STATE), lambda i: (0, i, 0, 0))
    mat_spec = pl.BlockSpec((2, gb, SSM_GROUP, SSM_STATE), lambda i: (0, i, 0, 0))
    w_spec = pl.BlockSpec((gb, CW, CW), lambda i: (i, 0, 0))
    w_shape = jax.ShapeDtypeStruct((N_GROUPS, CW, CW), BF16)
    return pl.pallas_call(
        _ssm_prep_kernel,
        grid=(N_GROUPS // gb,),
        in_specs=[vec_spec, vec_spec,
                  pl.BlockSpec((2, gb, 1, 1), lambda i: (0, i, 0, 0)),
                  mat_spec, mat_spec, mat_spec, mat_spec,
                  pl.BlockSpec((gb, 1, CW), lambda i: (i, 0, 0))],
        out_specs=[w_spec, w_spec, w_spec,
                   pl.BlockSpec((gb, N_LEVELS, 4 * 2 * SSM_STATE), lambda i: (i, 0, 0))],
        out_shape=[w_shape, w_shape, w_shape,
                   jax.ShapeDtypeStruct((N_GROUPS, N_LEVELS, 4 * 2 * SSM_STATE), F32)],
        compiler_params=_cparams(1),
        name="ssm_prep",
    )(row(lam_re), row(lam_im), log_step.reshape(2, N_GROUPS, 1, 1),
      bt(b_re), bt(b_im), c_re, c_im, d_row)


def _shift_rows(x, m, down):
    n = x.shape[0]
    return pltpu.roll(x, m if down else n - m, axis=0)


def _ssm_kernel(*refs, n_seq, has_h0):
    if has_h0:
        (xt_ref, mt_ref, gt_ref, wo_ref, at_ref, wg_ref, bg_ref, h0_ref,
         z_ref, zs_ref) = refs
    else:
        (xt_ref, mt_ref, gt_ref, wo_ref, at_ref, wg_ref, bg_ref,
         z_ref, st_ref, zs_ref, fin_ref) = refs
    step = pl.program_id(0)
    n_rows = xt_ref.shape[-1]
    seg = n_rows // n_seq
    cw2 = 2 * SSM_STATE
    pos = lax.broadcasted_iota(jnp.int32, (n_rows, cw2), 0) % seg
    for gi in range(GROUP_BLOCK):
        xt = xt_ref[gi]
        at = at_ref[gi]
        s_all = lax.dot_general(xt, gt_ref[gi], TN_DIMS, preferred_element_type=F32)
        h_parts = []
        for dr in range(2):
            s = s_all[:, dr * cw2:(dr + 1) * cw2]
            p_tab = at[:, (2 * dr) * cw2:(2 * dr + 1) * cw2]
            q_tab = at[:, (2 * dr + 1) * cw2:(2 * dr + 2) * cw2]
            down = dr == 0
            if has_h0:
                h0 = jnp.zeros((n_rows, cw2), F32)
                riota = lax.broadcasted_iota(jnp.int32, (n_rows, cw2), 0)
                for b in range(n_seq):
                    edge = b * seg if down else b * seg + seg - 1
                    h0 = jnp.where(riota == edge, h0_ref[gi, b:b + 1, dr * cw2:(dr + 1) * cw2], h0)
                s = s + p_tab[0:1] * h0 + q_tab[0:1] * pltpu.roll(h0, SSM_STATE, axis=1)
            m, lvl = 1, 0
            while m < seg:
                sh = _shift_rows(s, m, down)
                valid = (pos >= m) if down else (pos < seg - m)
                sh = jnp.where(valid, sh, 0.0)
                s = s + p_tab[lvl:lvl + 1] * sh + q_tab[lvl:lvl + 1] * pltpu.roll(sh, SSM_STATE, axis=1)
                m, lvl = 2 * m, lvl + 1
            if not has_h0:
                fin_ref[dr] = s
                st_ref[gi, :, dr * cw2:(dr + 1) * cw2] = (
                    fin_ref[dr, pl.ds(seg - 1 if down else 0, n_seq, stride=seg), :])
            ent = _shift_rows(s, 1, down)
            ent = jnp.where((pos >= 1) if down else (pos < seg - 1), ent, 0.0)
            if has_h0:
                ent = ent + h0
            h_parts.append(ent)
        h_all = jnp.concatenate(h_parts, axis=1).astype(BF16)
        yt = (jnp.dot(mt_ref[gi], xt, preferred_element_type=F32)
              + lax.dot_general(wo_ref[gi], h_all, NT_DIMS, preferred_element_type=F32))
        z = jax.nn.gelu(yt, approximate=True)
        grp = step * GROUP_BLOCK + gi
        for t in range(CHUNK):
            zs_ref[t, pl.ds(pl.multiple_of(grp * SSM_GROUP, SSM_GROUP), SSM_GROUP), :] = (
                z[t * SSM_GROUP:(t + 1) * SSM_GROUP, :])

    @pl.when(step == pl.num_programs(0) - 1)
    def _():
        for t in range(CHUNK):
            zt = zs_ref[t]
            gate = jnp.dot(wg_ref[...], zt.astype(BF16),
                           preferred_element_type=F32) + bg_ref[...]
            z_ref[t] = (zt * jax.nn.sigmoid(gate)).astype(z_ref.dtype)


def _ssm(xt, mt, gt, wo, at, w_glu_t, b_glu_col, h0, n_seq):
    n_rows = xt.shape[-1]
    has_h0 = h0 is not None
    gb = GROUP_BLOCK
    w_spec = pl.BlockSpec((gb, CW, CW), lambda i: (i, 0, 0))
    in_specs = [pl.BlockSpec((gb, CW, n_rows), lambda i: (i, 0, 0)),
                w_spec, w_spec, w_spec,
                pl.BlockSpec((gb, N_LEVELS, 4 * 2 * SSM_STATE), lambda i: (i, 0, 0)),
                pl.BlockSpec((SSM_WIDTH, SSM_WIDTH), lambda i: (0, 0)),
                pl.BlockSpec((SSM_WIDTH, 1), lambda i: (0, 0))]
    args = [xt, mt, gt, wo, at, w_glu_t, b_glu_col]
    out_specs = [pl.BlockSpec((CHUNK, SSM_WIDTH, n_rows), lambda i: (0, 0, 0))]
    out_shape = [jax.ShapeDtypeStruct((CHUNK, SSM_WIDTH, n_rows), BF16)]
    scratch = [pltpu.VMEM((CHUNK, SSM_WIDTH, n_rows), F32)]
    if has_h0:
        in_specs.append(pl.BlockSpec((gb, n_seq, CW), lambda i: (i, 0, 0)))
        args.append(h0)
    else:
        out_specs.append(pl.BlockSpec((gb, n_seq, CW), lambda i: (i, 0, 0)))
        out_shape.append(jax.ShapeDtypeStruct((N_GROUPS, n_seq, CW), F32))
        scratch.append(pltpu.VMEM((2, n_rows, 2 * SSM_STATE), F32))
    return pl.pallas_call(
        functools.partial(_ssm_kernel, n_seq=n_seq, has_h0=has_h0),
        grid=(N_GROUPS // gb,),
        in_specs=in_specs,
        out_specs=out_specs,
        out_shape=out_shape,
        scratch_shapes=scratch,
        compiler_params=_cparams(1),
        name="ssm_scan_glu",
    )(*args)


FF_CHUNK = 256


def _out_ffn_kernel(x_ref, attn_ref, zt_ref, mod_ref, g_ref, wo_ref, wfi_ref, wfo_ref,
                    o_ref, mixz_ref, act_ref):
    j = pl.program_id(1)

    @pl.when(j == 0)
    def _():
        for t in range(CHUNK):
            part = lax.dot_general(zt_ref[t], wo_ref[ATT_WIDTH:, :], TN_DIMS,
                                   preferred_element_type=F32)
            for cb in range(N_COL_BLOCKS):
                mixz_ref[cb, pl.ds(t, ROWS_PER_TILE, stride=CHUNK), :] = (
                    part[:, cb * LANES:(cb + 1) * LANES])

    gate1 = mod_ref[:, 2 * D_MODEL:3 * D_MODEL]
    shift2 = mod_ref[:, 3 * D_MODEL:4 * D_MODEL]
    scale2 = mod_ref[:, 4 * D_MODEL:5 * D_MODEL]
    gate2 = mod_ref[:, 5 * D_MODEL:6 * D_MODEL]
    rows = pl.ds(pl.multiple_of(j * SUB_TOK, SUB_TOK), SUB_TOK)
    mix = (jnp.dot(attn_ref[...], wo_ref[0:ATT_WIDTH, :], preferred_element_type=F32)
           + jnp.concatenate([mixz_ref[cb, rows, :] for cb in range(N_COL_BLOCKS)], axis=1))
    x1 = x_ref[...] + gate1 * _rms(mix, g_ref[1:2, :])
    h = (_rms(x1, g_ref[2:3, :]) * (1.0 + scale2) + shift2).astype(BF16)
    for c in range(D_FF // FF_CHUNK):
        lo = c * FF_CHUNK
        gt = jnp.dot(h, wfi_ref[:, lo:lo + FF_CHUNK], preferred_element_type=F32)
        up = jnp.dot(h, wfi_ref[:, D_FF + lo:D_FF + lo + FF_CHUNK], preferred_element_type=F32)
        act_ref[:, lo:lo + FF_CHUNK] = (_silu(gt) * up).astype(BF16)
    f = jnp.dot(act_ref[...], wfo_ref[...], preferred_element_type=F32)
    o_ref[...] = x1 + gate2 * _rms(f, g_ref[3:4, :])


def _out_ffn(x2d, attn, zt, mods, mod_rows, norm_g, w_o, w_ffn_in, w_ffn_out):
    n_tok = x2d.shape[0]
    n_tiles = n_tok // TOK_PER_TILE
    mod_row0, n_mod = mod_rows
    tiles_per_mod = n_tiles // n_mod
    const = lambda i, j: (0, 0)
    row_spec = lambda w: pl.BlockSpec((SUB_TOK, w), lambda i, j: (i * N_SUB + j, 0))
    return pl.pallas_call(
        _out_ffn_kernel,
        grid=(n_tiles, N_SUB),
        in_specs=[row_spec(D_MODEL), row_spec(ATT_WIDTH),
                  pl.BlockSpec((CHUNK, SSM_WIDTH, ROWS_PER_TILE), lambda i, j: (0, 0, i)),
                  pl.BlockSpec((None, 1, N_MOD * D_MODEL),
                               lambda i, j: (mod_row0 + i // tiles_per_mod, 0, 0)),
                  pl.BlockSpec((4, D_MODEL), const),
                  pl.BlockSpec((2 * ATT_WIDTH, D_MODEL), const, pipeline_mode=pl.Buffered(1)),
                  pl.BlockSpec((D_MODEL, 2 * D_FF), const, pipeline_mode=pl.Buffered(1)),
                  pl.BlockSpec((D_FF, D_MODEL), const, pipeline_mode=pl.Buffered(1))],
        out_specs=row_spec(D_MODEL),
        out_shape=jax.ShapeDtypeStruct((n_tok, D_MODEL), F32),
        scratch_shapes=[pltpu.VMEM((N_COL_BLOCKS, TOK_PER_TILE, LANES), F32),
                        pltpu.VMEM((SUB_TOK, D_FF), BF16)],
        compiler_params=_cparams(2),
        name="out_proj_ffn",
    )(x2d, attn, zt, mods, norm_g, w_o, w_ffn_in, w_ffn_out)


def _rope_tables(seq_len):
    t = np.arange(seq_len)
    row = (t // GRID_W).astype(np.float32)
    col = (t % GRID_W).astype(np.float32)
    half = HEAD_DIM // 2
    inv_freq = (np.float32(ROPE_BASE)
                ** (-np.arange(0, half, 2, dtype=np.float32) / np.float32(half))).astype(np.float32)
    ang_r = row[:, None] * inv_freq
    ang_c = col[:, None] * inv_freq
    ang = np.concatenate([ang_r, ang_r, ang_c, ang_c], axis=-1)
    cos, sin = np.cos(ang), np.sin(ang)
    upper = (np.arange(HEAD_DIM) % 32) < 16
    sa = np.where(upper, -sin, 0.0)
    sb = np.where(upper, 0.0, sin)
    two = lambda a: jnp.asarray(np.concatenate([a, a], axis=-1), dtype=F32)
    return two(cos), two(sa), two(sb)


def _layer(x, mods, mod_rows, lam_init, rope_tabs, ctx_k, ctx_v, h0, weights, prep):
    n_batch, seq_len = x.shape[:2]
    g = weights['norm_g']
    outs = _in_proj(x, mods, mod_rows, g[0:1], weights['w_in'], rope_tabs)
    q, k, v = outs[:3]
    pending = () if 'late_bf16' in weights else weights['late_f32']
    attn, cast = _attention(q, k, v, ctx_k, ctx_v, weights['lam'], weights['subln_g'],
                            n_batch, seq_len, lam_init, cast_weights=pending)
    if pending:
        weights['late_bf16'] = cast
    ssm_out = _ssm(outs[-1], *prep, weights['w_glu_t'], weights['b_glu_col'], h0, n_batch)
    y = _out_ffn(x.reshape(n_batch * seq_len, D_MODEL), attn, ssm_out[0], mods, mod_rows, g,
                 *weights['late_bf16'])
    return y.reshape(x.shape), outs[3:-1], ssm_out[1:]


def kernel(x_prompt, x_sample, cache_k, cache_v, state_ssm_re, state_ssm_im, c, c_ctx, w_mod, b_mod, norm_g, w_in, lam_params, subln_g, ssm_lambda_re, ssm_lambda_im, ssm_log_step, ssm_b_re, ssm_b_im, ssm_c_re, ssm_c_im, ssm_d, w_glu, b_glu, w_o, w_ffn_in, w_ffn_out):
    depth = w_mod.shape[0]
    assert depth == 1
    bp = x_prompt.shape[0]
    bd, ld_len = x_sample.shape[:2]
    past = cache_k.shape[2]
    xp, xs = x_prompt, x_sample
    rope_tabs = _rope_tables(ld_len)
    ks_out, vs_out, hr_out, hi_out = [], [], [], []
    for l in range(depth):
        lam_init = 0.8 - 0.6 * math.exp(-0.3 * l)
        mods = _modulation(c_ctx, c, w_mod[l], b_mod[l])
        weights = {
            'norm_g': norm_g[l],
            'w_in': w_in[l].astype(BF16),
            'lam': lam_params[l], 'subln_g': subln_g[l],
            'w_glu_t': w_glu[l].T.astype(BF16), 'b_glu_col': b_glu[l].reshape(SSM_WIDTH, 1),
            'late_f32': (w_o[l], w_ffn_in[l], w_ffn_out[l]),
        }
        prep = _ssm_prep(ssm_lambda_re[l], ssm_lambda_im[l], ssm_log_step[l],
                         ssm_b_re[l], ssm_b_im[l], ssm_c_re[l], ssm_c_im[l], ssm_d[l])
        ck = cache_k[:, l].reshape(bd, past, ATT_WIDTH)
        cv = cache_v[:, l].reshape(bd, past, ATT_WIDTH)
        h0 = jnp.stack([state_ssm_re[:, l], state_ssm_im[:, l]], axis=2)
        h0 = h0.transpose(3, 0, 1, 2, 4).reshape(N_GROUPS, bd, CW)
        xs, _, _ = _layer(xs, mods, (1, bd), lam_init, rope_tabs, ck, cv, h0, weights, prep)
        xp, (k_ctx, v_ctx), (st,) = _layer(xp, mods, (0, 1), lam_init, None, None, None, None,
                                           weights, prep)
        ks_out.append(jnp.swapaxes(k_ctx, 1, 2).reshape(bp, -1, 2 * N_HEADS, HEAD_DIM))
        vs_out.append(v_ctx)
        fin = st.reshape(N_GROUPS, bp, 2, 2, SSM_STATE).transpose(1, 2, 3, 0, 4)
        hr_out.append(fin[:, :, 0])
        hi_out.append(fin[:, :, 1])
    return (xp, xs, jnp.stack(ks_out, axis=1), jnp.stack(vs_out, axis=1),
            jnp.stack(hr_out, axis=1), jnp.stack(hi_out, axis=1))
```

```python
import functools
import math

import jax
import jax.numpy as jnp
import numpy as np
from jax import lax
from jax.experimental import pallas as pl
from jax.experimental.pallas import tpu as pltpu

F32 = jnp.float32
BF16 = jnp.bfloat16

D_MODEL = 1024
GRID_W = 64
ATT_WIDTH = 512
SSM_WIDTH = 512
HEAD_DIM = 64
N_HEADS = 4
HEAD_W = 2 * HEAD_DIM
SSM_GROUP = 16
N_GROUPS = 32
SSM_STATE = 64
D_FF = 2816
N_MOD = 6
ROPE_BASE = 10000.0
NORM_EPS = 1e-6

LANES = 128
N_COL_BLOCKS = D_MODEL // LANES
CHUNK = 16
CW = CHUNK * SSM_GROUP
SCAN_BLOCK = 8
SCAN_LEVELS = 3
AT_ROWS = 16
GROUP_BLOCK = 4

ROWS_PER_TILE = 128
TOK_PER_TILE = ROWS_PER_TILE * CHUNK
SUB_TOK = 512
N_SUB = TOK_PER_TILE // SUB_TOK

VMEM_LIMIT = 56 * 1024 * 1024

NT_DIMS = (((1,), (1,)), ((), ()))
TN_DIMS = (((0,), (0,)), ((), ()))


def _cparams(n_axes):
    return pltpu.CompilerParams(
        dimension_semantics=("arbitrary",) * n_axes,
        vmem_limit_bytes=VMEM_LIMIT)


def _rms(x, g):
    ms = jnp.mean(x * x, axis=-1, keepdims=True)
    return x * lax.rsqrt(ms + NORM_EPS) * g


def _silu(x):
    return x * jax.nn.sigmoid(x)


MOD_ROWS = 8


def _mod_kernel(ctx_ref, c_ref, w_ref, b_ref, o_ref):
    n_lat = c_ref.shape[0]
    row = lax.broadcasted_iota(jnp.int32, (MOD_ROWS, D_MODEL), 0)
    cond = jnp.where(row == 0, ctx_ref[...], 0.0)
    for b in range(n_lat):
        cond = jnp.where(row == 1 + b, c_ref[b:b + 1, :], cond)
    m = jnp.dot(_silu(cond).astype(BF16), w_ref[...].astype(BF16),
                preferred_element_type=F32) + b_ref[...]
    o_ref[:, 0, :] = m


def _modulation(c_ctx, c, w_mod, b_mod):
    n = w_mod.shape[1]
    tn = 2048
    assert 1 + c.shape[0] <= MOD_ROWS
    return pl.pallas_call(
        _mod_kernel,
        grid=(n // tn,),
        in_specs=[pl.BlockSpec((1, D_MODEL), lambda i: (0, 0)),
                  pl.BlockSpec(c.shape, lambda i: (0, 0)),
                  pl.BlockSpec((D_MODEL, tn), lambda i: (0, i)),
                  pl.BlockSpec((1, tn), lambda i: (0, i))],
        out_specs=pl.BlockSpec((MOD_ROWS, 1, tn), lambda i: (0, 0, i)),
        out_shape=jax.ShapeDtypeStruct((MOD_ROWS, 1, n), F32),
        compiler_params=_cparams(1),
        name="modulation",
    )(c_ctx.reshape(1, D_MODEL), c, w_mod, b_mod.reshape(1, n))


def _rope(x, cos, sa, sb):
    return (x * cos + pltpu.roll(x, HEAD_W - 16, axis=1) * sa
            + pltpu.roll(x, 16, axis=1) * sb)


def _in_proj_kernel(*refs, rope, seqs_per_sub):
    x_ref, x3_hbm, mod_ref, g_ref, w_ref = refs[:5]
    refs = refs[5:]
    if rope:
        cos_ref, sa_ref, sb_ref = refs[:3]
        refs = refs[3:]
        q_ref, k_ref, v_ref, ut_ref, wut_ref, xt_ref, xt_sem = refs
    else:
        q_ref, k_ref, v_ref, kc_ref, vc_ref, ut_ref, wut_ref, xt_ref, xt_sem = refs
    tile = pl.program_id(0)
    j = pl.program_id(1)
    t_per_sub = CHUNK // N_SUB
    t_early = CHUNK - t_per_sub

    def gather(tile_idx, t):
        src = x3_hbm.at[pl.ds(tile_idx * ROWS_PER_TILE, ROWS_PER_TILE), t, :]
        return pltpu.make_async_copy(src, xt_ref.at[t], xt_sem.at[t])

    @pl.when(j == 0)
    def _():
        @pl.when(tile == 0)
        def _():
            for t in range(t_early):
                gather(0, t).start()
        for t in range(t_early, CHUNK):
            gather(tile, t).start()

    t_base = j * t_per_sub
    for d in range(t_per_sub):
        gather(tile, t_base + d).wait()

    @pl.when((j == N_SUB - 1) & (tile + 1 < pl.num_programs(0)))
    def _():
        for t in range(t_early):
            gather(tile + 1, t).start()

    @pl.when((tile == 0) & (j == 0))
    def _():
        wut_ref[...] = w_ref[:, 3 * ATT_WIDTH:].T

    shift = mod_ref[:, 0:D_MODEL]
    gain = g_ref[...] * (1.0 + mod_ref[:, D_MODEL:2 * D_MODEL])

    def norm_mod(xv):
        ms = jnp.mean(xv * xv, axis=-1, keepdims=True)
        return (xv * lax.rsqrt(ms + NORM_EPS) * gain + shift).astype(BF16)

    proj = jnp.dot(norm_mod(x_ref[...]), w_ref[:, 0:3 * ATT_WIDTH],
                   preferred_element_type=F32)
    q = proj[:, 0:ATT_WIDTH]
    k = proj[:, ATT_WIDTH:2 * ATT_WIDTH]
    v = proj[:, 2 * ATT_WIDTH:3 * ATT_WIDTH]
    qscale = HEAD_DIM ** -0.5 * math.log2(math.e)
    if rope:
        cos, sa, sb = cos_ref[...], sa_ref[...], sb_ref[...]
        for hd in range(N_HEADS):
            sl = slice(hd * HEAD_W, (hd + 1) * HEAD_W)
            q_ref[:, sl] = (_rope(q[:, sl], cos, sa, sb) * qscale).astype(q_ref.dtype)
            k_ref[:, sl] = _rope(k[:, sl], cos, sa, sb).astype(k_ref.dtype)
    else:
        q_ref[...] = (q * qscale).astype(q_ref.dtype)
        k_ref[...] = k.astype(k_ref.dtype)
        seq = SUB_TOK // seqs_per_sub
        k_t = k.T
        for b in range(seqs_per_sub):
            kc_ref[b] = k_t[:, b * seq:(b + 1) * seq]
            for hd in range(N_HEADS):
                vc_ref[b, :, hd, :] = v[b * seq:(b + 1) * seq, hd * HEAD_W:(hd + 1) * HEAD_W]
    v_ref[...] = v.astype(v_ref.dtype)

    for d0 in range(0, t_per_sub, 2):
        xt = jnp.concatenate([xt_ref[t_base + d0], xt_ref[t_base + d0 + 1]], axis=0)
        ut = lax.dot_general(wut_ref[...], norm_mod(xt), NT_DIMS,
                             preferred_element_type=F32)
        for d in range(2):
            blk = ut[:, d * ROWS_PER_TILE:(d + 1) * ROWS_PER_TILE]
            row0 = pl.multiple_of((t_base + d0 + d) * SSM_GROUP, SSM_GROUP)
            ut_ref[:, pl.ds(row0, SSM_GROUP), :] = (
                blk.reshape(N_GROUPS, SSM_GROUP, ROWS_PER_TILE).astype(ut_ref.dtype))


def _in_proj(x, mods, mod_rows, g0, w_in, rope_tabs):
    n_batch, seq_len = x.shape[:2]
    n_tok = n_batch * seq_len
    n_rows = n_tok // CHUNK
    n_tiles = n_tok // TOK_PER_TILE
    mod_row0, n_mod = mod_rows
    tiles_per_mod = n_tiles // n_mod
    rope = rope_tabs is not None
    seqs_per_sub = max(1, SUB_TOK // seq_len)
    in_specs = [pl.BlockSpec((SUB_TOK, D_MODEL), lambda i, j: (i * N_SUB + j, 0)),
                pl.BlockSpec(memory_space=pl.ANY),
                pl.BlockSpec((None, 1, 2 * D_MODEL),
                             lambda i, j: (mod_row0 + i // tiles_per_mod, 0, 0)),
                pl.BlockSpec((1, D_MODEL), lambda i, j: (0, 0)),
                pl.BlockSpec((D_MODEL, 4 * ATT_WIDTH), lambda i, j: (0, 0))]
    args = [x.reshape(n_tok, D_MODEL), x.reshape(n_rows, CHUNK, D_MODEL), mods, g0, w_in]
    row_spec = pl.BlockSpec((SUB_TOK, ATT_WIDTH), lambda i, j: (i * N_SUB + j, 0))
    row_shape = jax.ShapeDtypeStruct((n_tok, ATT_WIDTH), BF16)
    out_specs = [row_spec, row_spec, row_spec]
    out_shape = [row_shape, row_shape, row_shape]
    if rope:
        assert seq_len == TOK_PER_TILE
        for tab in rope_tabs:
            in_specs.append(pl.BlockSpec((SUB_TOK, HEAD_W), lambda i, j: (j, 0)))
            args.append(tab)
    else:
        out_specs += [pl.BlockSpec((seqs_per_sub, ATT_WIDTH, seq_len),
                                   lambda i, j: (i * N_SUB + j, 0, 0)),
                      pl.BlockSpec((seqs_per_sub, seq_len, N_HEADS, HEAD_W),
                                   lambda i, j: (i * N_SUB + j, 0, 0, 0))]
        out_shape += [jax.ShapeDtypeStruct((n_batch, ATT_WIDTH, seq_len), F32),
                      jax.ShapeDtypeStruct((n_batch, seq_len, N_HEADS, HEAD_W), F32)]
    out_specs.append(pl.BlockSpec((N_GROUPS, CW, ROWS_PER_TILE), lambda i, j: (0, 0, i)))
    out_shape.append(jax.ShapeDtypeStruct((N_GROUPS, CW, n_rows), BF16))
    return pl.pallas_call(
        functools.partial(_in_proj_kernel, rope=rope, seqs_per_sub=seqs_per_sub),
        grid=(n_tiles, N_SUB),
        in_specs=in_specs,
        out_specs=out_specs,
        out_shape=out_shape,
        scratch_shapes=[pltpu.VMEM((SSM_WIDTH, D_MODEL), BF16),
                        pltpu.VMEM((CHUNK, ROWS_PER_TILE, D_MODEL), F32),
                        pltpu.SemaphoreType.DMA((CHUNK,))],
        compiler_params=_cparams(2),
        name="in_proj",
    )(*args)


def _attn_kernel(*refs, has_ctx, lam_init, n_seq, seq_len, tq, n_cast):
    if n_cast:
        cast_in = refs[len(refs) - 2 * n_cast - 1:len(refs) - n_cast - 1]
        cast_out = refs[len(refs) - n_cast:]
        refs = refs[:len(refs) - 2 * n_cast - 1] + (refs[len(refs) - n_cast - 1],)
        for src, dst in zip(cast_in, cast_out):
            dst[...] = src[...].astype(dst.dtype)
    if has_ctx:
        lam_ref, sg_ref, q_ref, ck_ref, cv_ref, k_ref, v_ref, o_ref = refs
    else:
        lam_ref, sg_ref, q_ref, k_ref, v_ref, o_ref = refs
    lp = lam_ref[...]
    lam = (jnp.exp(jnp.sum(lp[0:1] * lp[1:2], axis=-1, keepdims=True))
           - jnp.exp(jnp.sum(lp[2:3] * lp[3:4], axis=-1, keepdims=True)) + lam_init)
    first_map = lax.broadcasted_iota(jnp.int32, (1, HEAD_W), 1) < HEAD_DIM
    for b in range(n_seq):
        for hd in range(N_HEADS):
            sl = slice(hd * HEAD_W, (hd + 1) * HEAD_W)
            qh = q_ref[b * tq:(b + 1) * tq, sl]
            zero = jnp.zeros_like(qh)
            qs = jnp.concatenate([jnp.where(first_map, qh, zero),
                                  jnp.where(first_map, zero, qh)], axis=0)
            kv_rows = slice(b * seq_len, (b + 1) * seq_len)
            parts = [(k_ref[kv_rows, sl], v_ref[kv_rows, sl])]
            if has_ctx:
                parts.insert(0, (ck_ref[:, sl].astype(BF16), cv_ref[:, sl].astype(BF16)))
            scores = [lax.dot_general(qs, kk, NT_DIMS, preferred_element_type=F32)
                      for kk, _ in parts]
            mx = scores[0].max(axis=-1, keepdims=True)
            for s in scores[1:]:
                mx = jnp.maximum(mx, s.max(axis=-1, keepdims=True))
            acc = None
            for s, (_, vv) in zip(scores, parts):
                e = jnp.exp2(s - mx).astype(BF16)
                v_one = jnp.concatenate([vv, jnp.ones_like(vv)], axis=1)
                pv = jnp.dot(e, v_one, preferred_element_type=F32)
                acc = pv if acc is None else acc + pv
            num = acc[:, 0:HEAD_W] / acc[:, HEAD_W:2 * HEAD_W]
            o = num[0:tq] - lam * num[tq:2 * tq]
            o = _rms(o, sg_ref[...]) * (1.0 - lam_init)
            o_ref[b * tq:(b + 1) * tq, sl] = o.astype(o_ref.dtype)


def _attention(q, k, v, ctx_k, ctx_v, lam_params, subln_g, n_batch, seq_len, lam_init,
               cast_weights=()):
    has_ctx = ctx_k is not None
    tq = min(512, seq_len)
    n_q = seq_len // tq
    n_seq = 1 if n_q > 1 else min(4, n_batch)
    in_specs = [pl.BlockSpec((4, HEAD_DIM), lambda b, i: (0, 0)),
                pl.BlockSpec((1, HEAD_W), lambda b, i: (0, 0)),
                pl.BlockSpec((n_seq * tq, ATT_WIDTH), lambda b, i: (b * n_q + i, 0))]
    args = [lam_params, subln_g.reshape(1, HEAD_W), q]
    if has_ctx:
        past = ctx_k.shape[1]
        ctx_spec = pl.BlockSpec((None, past, ATT_WIDTH), lambda b, i: (b, 0, 0))
        in_specs += [ctx_spec, ctx_spec]
        args += [ctx_k, ctx_v]
    kv_spec = pl.BlockSpec((n_seq * seq_len, ATT_WIDTH), lambda b, i: (b, 0))
    in_specs += [kv_spec, kv_spec]
    args += [k, v]
    out_specs = [pl.BlockSpec((n_seq * tq, ATT_WIDTH), lambda b, i: (b * n_q + i, 0))]
    out_shape = [jax.ShapeDtypeStruct((n_batch * seq_len, ATT_WIDTH), BF16)]
    n_steps = (n_batch // n_seq) * n_q
    for w in cast_weights:
        rows = w.shape[0] // n_steps
        spec = pl.BlockSpec((rows, w.shape[1]), lambda b, i: (b * n_q + i, 0))
        in_specs.append(spec)
        args.append(w)
        out_specs.append(spec)
        out_shape.append(jax.ShapeDtypeStruct(w.shape, BF16))
    outs = pl.pallas_call(
        functools.partial(_attn_kernel, has_ctx=has_ctx, lam_init=lam_init,
                          n_seq=n_seq, seq_len=seq_len, tq=tq, n_cast=len(cast_weights)),
        grid=(n_batch // n_seq, n_q),
        in_specs=in_specs,
        out_specs=out_specs,
        out_shape=out_shape,
        compiler_params=_cparams(2),
        name="diff_attention",
    )(*args)
    return outs[0], tuple(outs[1:])


def _cmul(ar, ai, br, bi):
    return ar * br - ai * bi, ar * bi + ai * br


def _ssm_prep_kernel(lre_ref, lim_ref, ls_ref, bre_ref, bim_ref, cre_ref, cim_ref, d_ref,
                     mt_ref, gt_ref, wo_ref, at_ref):
    lane = lax.broadcasted_iota(jnp.int32, (SSM_GROUP, CW), 1)
    chan = lax.broadcasted_iota(jnp.int32, (SSM_GROUP, CW), 0)
    for gi in range(GROUP_BLOCK):
        gt_cols, wo_cols, at_cols, toeplitz = [], [], [], []
        for dr in range(2):
            lr = jnp.minimum(lre_ref[dr, gi], -1e-4)
            li = lim_ref[dr, gi]
            step = jnp.exp(ls_ref[dr, gi])
            mag = jnp.exp(lr * step)
            a_re = mag * jnp.cos(li * step)
            a_im = mag * jnp.sin(li * step)
            den = lr * lr + li * li
            nr = a_re - 1.0
            f_re = (nr * lr + a_im * li) / den
            f_im = (a_im * lr - nr * li) / den
            bt_re, bt_im = bre_ref[dr, gi], bim_ref[dr, gi]
            bb_re, bb_im = _cmul(f_re, f_im, bt_re, bt_im)
            c_re, c_im = cre_ref[dr, gi], cim_ref[dr, gi]
            pw = [(jnp.ones_like(a_re), jnp.zeros_like(a_im))]
            for _ in range(CHUNK):
                pw.append(_cmul(pw[-1][0], pw[-1][1], a_re, a_im))
            g_re, g_im, e_re, e_im = [], [], [], []
            for t in range(CHUNK):
                pr, pi = pw[CHUNK - 1 - t] if dr == 0 else pw[t]
                r, i = _cmul(bb_re, bb_im, pr, pi)
                g_re.append(r)
                g_im.append(i)
                pr, pi = pw[t + 1] if dr == 0 else pw[CHUNK - t]
                r, i = _cmul(c_re, c_im, pr, pi)
                e_re.append(r)
                e_im.append(-i)
            g_cat = jnp.concatenate([jnp.concatenate(g_re, axis=0),
                                     jnp.concatenate(g_im, axis=0)], axis=1)
            gt_cols.append(g_cat)
            wo_cols.append(jnp.concatenate([jnp.concatenate(e_re, axis=0),
                                            jnp.concatenate(e_im, axis=0)], axis=1))
            c_cat = jnp.concatenate([c_re, -c_im], axis=1)
            toeplitz.append(lax.dot_general(c_cat, g_cat, NT_DIMS,
                                            precision=lax.Precision.HIGHEST,
                                            preferred_element_type=F32))
            apw = [pw[CHUNK]]
            for _ in range(SCAN_BLOCK - 1):
                apw.append(_cmul(apw[-1][0], apw[-1][1], apw[0][0], apw[0][1]))
            order = list(range(SCAN_BLOCK)) if dr == 0 else list(range(SCAN_BLOCK - 1, -1, -1))
            order += [2 ** l - 1 for l in range(SCAN_LEVELS)]
            order += [0] * (AT_ROWS - len(order))
            at_cols += [jnp.concatenate([jnp.concatenate([apw[i][0], apw[i][0]], axis=1)
                                         for i in order], axis=0),
                        jnp.concatenate([jnp.concatenate([-apw[i][1], apw[i][1]], axis=1)
                                         for i in order], axis=0)]
        kf_rev, kb = toeplitz
        d_skip = d_ref[gi]
        blocks = []
        for t in range(CHUNK):
            fwd = pltpu.roll(kf_rev, (CW - (CHUNK - 1 - t) * SSM_GROUP) % CW, axis=1)
            bwd = pltpu.roll(kb, t * SSM_GROUP, axis=1)
            blocks.append(jnp.where(lane < (t + 1) * SSM_GROUP, fwd, 0.0)
                          + jnp.where(lane >= t * SSM_GROUP, bwd, 0.0)
                          + jnp.where(lane == chan + t * SSM_GROUP, d_skip, 0.0))
        mt_ref[gi] = jnp.concatenate(blocks, axis=0).astype(mt_ref.dtype)
        gt_ref[gi] = jnp.concatenate(gt_cols, axis=1).astype(gt_ref.dtype)
        wo_ref[gi] = jnp.concatenate(wo_cols, axis=1).astype(wo_ref.dtype)
        at_ref[gi] = jnp.concatenate(at_cols, axis=1)


def _ssm_prep(lam_re, lam_im, log_step, b_re, b_im, c_re, c_im, d_skip):
    row = lambda a: a.reshape(2, N_GROUPS, 1, SSM_STATE)
    bt = lambda a: jnp.swapaxes(a, 2, 3)
    d_row = jnp.tile((d_skip[0] + d_skip[1]).reshape(N_GROUPS, 1, SSM_GROUP), (1, 1, CHUNK))
    gb = GROUP_BLOCK
    vec_spec = pl.BlockSpec((2, gb, 1, SSM_STATE), lambda i: (0, i, 0, 0))
    mat_spec = pl.BlockSpec((2, gb, SSM_GROUP, SSM_STATE), lambda i: (0, i, 0, 0))
    w_spec = pl.BlockSpec((gb, CW, CW), lambda i: (i, 0, 0))
    w_shape = jax.ShapeDtypeStruct((N_GROUPS, CW, CW), BF16)
    return pl.pallas_call(
        _ssm_prep_kernel,
        grid=(N_GROUPS // gb,),
        in_specs=[vec_spec, vec_spec,
                  pl.BlockSpec((2, gb, 1, 1), lambda i: (0, i, 0, 0)),
                  mat_spec, mat_spec, mat_spec, mat_spec,
                  pl.BlockSpec((gb, 1, CW), lambda i: (i, 0, 0))],
        out_specs=[w_spec, w_spec, w_spec,
                   pl.BlockSpec((gb, AT_ROWS, 4 * 2 * SSM_STATE), lambda i: (i, 0, 0))],
        out_shape=[w_shape, w_shape, w_shape,
                   jax.ShapeDtypeStruct((N_GROUPS, AT_ROWS, 4 * 2 * SSM_STATE), F32)],
        compiler_params=_cparams(1),
        name="ssm_prep",
    )(row(lam_re), row(lam_im), log_step.reshape(2, N_GROUPS, 1, 1),
      bt(b_re), bt(b_im), c_re, c_im, d_row)


def _shift_rows(x, m, down):
    n = x.shape[0]
    return pltpu.roll(x, m if down else n - m, axis=0)


def _ssm_kernel(*refs, n_seq, has_h0):
    if has_h0:
        (xt_ref, mt_ref, gt_ref, wo_ref, at_ref, wg_ref, bg_ref, h0_ref,
         z_ref, zs_ref) = refs
    else:
        (xt_ref, mt_ref, gt_ref, wo_ref, at_ref, wg_ref, bg_ref,
         z_ref, st_ref, zs_ref, fin_ref) = refs
    step = pl.program_id(0)
    n_rows = xt_ref.shape[-1]
    seg = n_rows // n_seq
    cw2 = 2 * SSM_STATE
    assert seg % SCAN_BLOCK == 0
    pos = lax.broadcasted_iota(jnp.int32, (n_rows, cw2), 0) % seg
    row_blk = lax.broadcasted_iota(jnp.int32, (SCAN_BLOCK, cw2), 0)
    for gi in range(GROUP_BLOCK):
        xt = xt_ref[gi]
        at = at_ref[gi]
        s_all = lax.dot_general(xt, gt_ref[gi], TN_DIMS, preferred_element_type=F32)
        h_parts = []
        for dr in range(2):
            s = s_all[:, dr * cw2:(dr + 1) * cw2]
            p_tab = at[:, (2 * dr) * cw2:(2 * dr + 1) * cw2]
            q_tab = at[:, (2 * dr + 1) * cw2:(2 * dr + 2) * cw2]
            down = dr == 0
            if has_h0:
                h0 = jnp.zeros((n_rows, cw2), F32)
                riota = lax.broadcasted_iota(jnp.int32, (n_rows, cw2), 0)
                for b in range(n_seq):
                    edge = b * seg if down else b * seg + seg - 1
                    h0 = jnp.where(riota == edge, h0_ref[gi, b:b + 1, dr * cw2:(dr + 1) * cw2], h0)
                lv0 = SCAN_BLOCK
                s = (s + p_tab[lv0:lv0 + 1] * h0
                     + q_tab[lv0:lv0 + 1] * pltpu.roll(h0, SSM_STATE, axis=1))
            n_blk = n_rows // SCAN_BLOCK
            s3 = s.reshape(n_blk, SCAN_BLOCK, cw2)
            for lvl in range(SCAN_LEVELS):
                m = 2 ** lvl
                valid = (row_blk >= m) if down else (row_blk < SCAN_BLOCK - m)
                row = SCAN_BLOCK + lvl
                p = jnp.where(valid, p_tab[row:row + 1], 0.0)
                q = jnp.where(valid, q_tab[row:row + 1], 0.0)
                sh = pltpu.roll(s3, m if down else SCAN_BLOCK - m, axis=1)
                s3 = s3 + p * sh + q * pltpu.roll(sh, SSM_STATE, axis=2)
            sw3 = pltpu.roll(s3, SSM_STATE, axis=2)
            p_blk, q_blk = p_tab[0:SCAN_BLOCK], q_tab[0:SCAN_BLOCK]
            blocks = [s3[i] for i in range(n_blk)]
            blocks_w = [sw3[i] for i in range(n_blk)]
            blk_per_seq = seg // SCAN_BLOCK
            edge = slice(SCAN_BLOCK - 1, SCAN_BLOCK) if down else slice(0, 1)
            for q_i in range(n_seq):
                idxs = list(range(q_i * blk_per_seq, (q_i + 1) * blk_per_seq))
                idxs = idxs if down else idxs[::-1]
                for prev, cur in zip(idxs[:-1], idxs[1:]):
                    c = jnp.broadcast_to(blocks[prev][edge], (SCAN_BLOCK, cw2))
                    cw = jnp.broadcast_to(blocks_w[prev][edge], (SCAN_BLOCK, cw2))
                    blocks[cur] = blocks[cur] + p_blk * c + q_blk * cw
                    blocks_w[cur] = blocks_w[cur] + p_blk * cw - q_blk * c
            s = jnp.concatenate(blocks, axis=0)
            if not has_h0:
                fin_ref[dr] = s
                st_ref[gi, :, dr * cw2:(dr + 1) * cw2] = (
                    fin_ref[dr, pl.ds(seg - 1 if down else 0, n_seq, stride=seg), :])
            ent = _shift_rows(s, 1, down)
            ent = jnp.where((pos >= 1) if down else (pos < seg - 1), ent, 0.0)
            if has_h0:
                ent = ent + h0
            h_parts.append(ent)
        h_all = jnp.concatenate(h_parts, axis=1).astype(BF16)
        yt = (jnp.dot(mt_ref[gi], xt, preferred_element_type=F32)
              + lax.dot_general(wo_ref[gi], h_all, NT_DIMS, preferred_element_type=F32))
        z = jax.nn.gelu(yt, approximate=True)
        grp = step * GROUP_BLOCK + gi
        for t in range(CHUNK):
            zs_ref[t, pl.ds(pl.multiple_of(grp * SSM_GROUP, SSM_GROUP), SSM_GROUP), :] = (
                z[t * SSM_GROUP:(t + 1) * SSM_GROUP, :])

    @pl.when(step == pl.num_programs(0) - 1)
    def _():
        for t in range(CHUNK):
            zt = zs_ref[t]
            gate = jnp.dot(wg_ref[...], zt.astype(BF16),
                           preferred_element_type=F32) + bg_ref[...]
            z_ref[t] = (zt * jax.nn.sigmoid(gate)).astype(z_ref.dtype)


def _ssm(xt, mt, gt, wo, at, w_glu_t, b_glu_col, h0, n_seq):
    n_rows = xt.shape[-1]
    has_h0 = h0 is not None
    gb = GROUP_BLOCK
    w_spec = pl.BlockSpec((gb, CW, CW), lambda i: (i, 0, 0))
    in_specs = [pl.BlockSpec((gb, CW, n_rows), lambda i: (i, 0, 0)),
                w_spec, w_spec, w_spec,
                pl.BlockSpec((gb, AT_ROWS, 4 * 2 * SSM_STATE), lambda i: (i, 0, 0)),
                pl.BlockSpec((SSM_WIDTH, SSM_WIDTH), lambda i: (0, 0)),
                pl.BlockSpec((SSM_WIDTH, 1), lambda i: (0, 0))]
    args = [xt, mt, gt, wo, at, w_glu_t, b_glu_col]
    out_specs = [pl.BlockSpec((CHUNK, SSM_WIDTH, n_rows), lambda i: (0, 0, 0))]
    out_shape = [jax.ShapeDtypeStruct((CHUNK, SSM_WIDTH, n_rows), BF16)]
    scratch = [pltpu.VMEM((CHUNK, SSM_WIDTH, n_rows), F32)]
    if has_h0:
        in_specs.append(pl.BlockSpec((gb, n_seq, CW), lambda i: (i, 0, 0)))
        args.append(h0)
    else:
        out_specs.append(pl.BlockSpec((gb, n_seq, CW), lambda i: (i, 0, 0)))
        out_shape.append(jax.ShapeDtypeStruct((N_GROUPS, n_seq, CW), F32))
        scratch.append(pltpu.VMEM((2, n_rows, 2 * SSM_STATE), F32))
    return pl.pallas_call(
        functools.partial(_ssm_kernel, n_seq=n_seq, has_h0=has_h0),
        grid=(N_GROUPS // gb,),
        in_specs=in_specs,
        out_specs=out_specs,
        out_shape=out_shape,
        scratch_shapes=scratch,
        compiler_params=_cparams(1),
        name="ssm_scan_glu",
    )(*args)


FF_CHUNK = 256


def _out_ffn_kernel(x_ref, attn_ref, zt_ref, mod_ref, g_ref, wo_ref, wfi_ref, wfo_ref,
                    o_ref, mixz_ref, act_ref):
    j = pl.program_id(1)

    @pl.when(j == 0)
    def _():
        for t in range(CHUNK):
            part = lax.dot_general(zt_ref[t], wo_ref[ATT_WIDTH:, :], TN_DIMS,
                                   preferred_element_type=F32)
            for cb in range(N_COL_BLOCKS):
                mixz_ref[cb, pl.ds(t, ROWS_PER_TILE, stride=CHUNK), :] = (
                    part[:, cb * LANES:(cb + 1) * LANES])

    gate1 = mod_ref[:, 2 * D_MODEL:3 * D_MODEL]
    shift2 = mod_ref[:, 3 * D_MODEL:4 * D_MODEL]
    scale2 = mod_ref[:, 4 * D_MODEL:5 * D_MODEL]
    gate2 = mod_ref[:, 5 * D_MODEL:6 * D_MODEL]
    half = SUB_TOK // 2
    gain2 = g_ref[2:3, :] * (1.0 + scale2)

    def pre_ffn(hf):
        r = slice(hf * half, (hf + 1) * half)
        rows = pl.ds(pl.multiple_of(j * SUB_TOK + hf * half, half), half)
        mix = (jnp.dot(attn_ref[r, :], wo_ref[0:ATT_WIDTH, :], preferred_element_type=F32)
               + jnp.concatenate([mixz_ref[cb, rows, :] for cb in range(N_COL_BLOCKS)], axis=1))
        x1 = x_ref[r, :] + gate1 * _rms(mix, g_ref[1:2, :])
        ms = jnp.mean(x1 * x1, axis=-1, keepdims=True)
        return x1, (x1 * lax.rsqrt(ms + NORM_EPS) * gain2 + shift2).astype(BF16)

    def ffn_in(hf, h, chunks):
        r = slice(hf * half, (hf + 1) * half)
        for c in chunks:
            lo = c * FF_CHUNK
            gt = jnp.dot(h, wfi_ref[:, lo:lo + FF_CHUNK], preferred_element_type=F32)
            up = jnp.dot(h, wfi_ref[:, D_FF + lo:D_FF + lo + FF_CHUNK],
                         preferred_element_type=F32)
            act_ref[r, lo:lo + FF_CHUNK] = (_silu(gt) * up).astype(BF16)

    def ffn_out(hf, x1):
        r = slice(hf * half, (hf + 1) * half)
        f = jnp.dot(act_ref[r, :], wfo_ref[...], preferred_element_type=F32)
        o_ref[r, :] = x1 + gate2 * _rms(f, g_ref[3:4, :])

    n_chunks = D_FF // FF_CHUNK
    x1_a, h_a = pre_ffn(0)
    x1_b, h_b = pre_ffn(1)
    ffn_in(0, h_a, range(n_chunks))
    ffn_out(0, x1_a)
    ffn_in(1, h_b, range(n_chunks))
    ffn_out(1, x1_b)


def _out_ffn(x2d, attn, zt, mods, mod_rows, norm_g, w_o, w_ffn_in, w_ffn_out):
    n_tok = x2d.shape[0]
    n_tiles = n_tok // TOK_PER_TILE
    mod_row0, n_mod = mod_rows
    tiles_per_mod = n_tiles // n_mod
    const = lambda i, j: (0, 0)
    row_spec = lambda w: pl.BlockSpec((SUB_TOK, w), lambda i, j: (i * N_SUB + j, 0))
    return pl.pallas_call(
        _out_ffn_kernel,
        grid=(n_tiles, N_SUB),
        in_specs=[row_spec(D_MODEL), row_spec(ATT_WIDTH),
                  pl.BlockSpec((CHUNK, SSM_WIDTH, ROWS_PER_TILE), lambda i, j: (0, 0, i)),
                  pl.BlockSpec((None, 1, N_MOD * D_MODEL),
                               lambda i, j: (mod_row0 + i // tiles_per_mod, 0, 0)),
                  pl.BlockSpec((4, D_MODEL), const),
                  pl.BlockSpec((2 * ATT_WIDTH, D_MODEL), const, pipeline_mode=pl.Buffered(1)),
                  pl.BlockSpec((D_MODEL, 2 * D_FF), const, pipeline_mode=pl.Buffered(1)),
                  pl.BlockSpec((D_FF, D_MODEL), const, pipeline_mode=pl.Buffered(1))],
        out_specs=row_spec(D_MODEL),
        out_shape=jax.ShapeDtypeStruct((n_tok, D_MODEL), F32),
        scratch_shapes=[pltpu.VMEM((N_COL_BLOCKS, TOK_PER_TILE, LANES), F32),
                        pltpu.VMEM((SUB_TOK, D_FF), BF16)],
        compiler_params=_cparams(2),
        name="out_proj_ffn",
    )(x2d, attn, zt, mods, norm_g, w_o, w_ffn_in, w_ffn_out)


def _rope_tables(seq_len):
    t = np.arange(seq_len)
    row = (t // GRID_W).astype(np.float32)
    col = (t % GRID_W).astype(np.float32)
    half = HEAD_DIM // 2
    inv_freq = (np.float32(ROPE_BASE)
                ** (-np.arange(0, half, 2, dtype=np.float32) / np.float32(half))).astype(np.float32)
    ang_r = row[:, None] * inv_freq
    ang_c = col[:, None] * inv_freq
    ang = np.concatenate([ang_r, ang_r, ang_c, ang_c], axis=-1)
    cos, sin = np.cos(ang), np.sin(ang)
    upper = (np.arange(HEAD_DIM) % 32) < 16
    sa = np.where(upper, -sin, 0.0)
    sb = np.where(upper, 0.0, sin)
    two = lambda a: jnp.asarray(np.concatenate([a, a], axis=-1), dtype=F32)
    return two(cos), two(sa), two(sb)


def _layer(x, mods, mod_rows, lam_init, rope_tabs, ctx_k, ctx_v, h0, weights, prep):
    n_batch, seq_len = x.shape[:2]
    g = weights['norm_g']
    outs = _in_proj(x, mods, mod_rows, g[0:1], weights['w_in'], rope_tabs)
    q, k, v = outs[:3]
    pending = () if 'late_bf16' in weights else weights['late_f32']
    attn, cast = _attention(q, k, v, ctx_k, ctx_v, weights['lam'], weights['subln_g'],
                            n_batch, seq_len, lam_init, cast_weights=pending)
    if pending:
        weights['late_bf16'] = cast
    ssm_out = _ssm(outs[-1], *prep, weights['w_glu_t'], weights['b_glu_col'], h0, n_batch)
    y = _out_ffn(x.reshape(n_batch * seq_len, D_MODEL), attn, ssm_out[0], mods, mod_rows, g,
                 *weights['late_bf16'])
    return y.reshape(x.shape), outs[3:-1], ssm_out[1:]


def kernel(x_prompt, x_sample, cache_k, cache_v, state_ssm_re, state_ssm_im, c, c_ctx, w_mod, b_mod, norm_g, w_in, lam_params, subln_g, ssm_lambda_re, ssm_lambda_im, ssm_log_step, ssm_b_re, ssm_b_im, ssm_c_re, ssm_c_im, ssm_d, w_glu, b_glu, w_o, w_ffn_in, w_ffn_out):
    depth = w_mod.shape[0]
    assert depth == 1
    bp = x_prompt.shape[0]
    bd, ld_len = x_sample.shape[:2]
    past = cache_k.shape[2]
    xp, xs = x_prompt, x_sample
    rope_tabs = _rope_tables(ld_len)
    ks_out, vs_out, hr_out, hi_out = [], [], [], []
    for l in range(depth):
        lam_init = 0.8 - 0.6 * math.exp(-0.3 * l)
        mods = _modulation(c_ctx, c, w_mod[l], b_mod[l])
        weights = {
            'norm_g': norm_g[l],
            'w_in': w_in[l].astype(BF16),
            'lam': lam_params[l], 'subln_g': subln_g[l],
            'w_glu_t': w_glu[l].T.astype(BF16), 'b_glu_col': b_glu[l].reshape(SSM_WIDTH, 1),
            'late_f32': (w_o[l], w_ffn_in[l], w_ffn_out[l]),
        }
        prep = _ssm_prep(ssm_lambda_re[l], ssm_lambda_im[l], ssm_log_step[l],
                         ssm_b_re[l], ssm_b_im[l], ssm_c_re[l], ssm_c_im[l], ssm_d[l])
        ck = cache_k[:, l].reshape(bd, past, ATT_WIDTH)
        cv = cache_v[:, l].reshape(bd, past, ATT_WIDTH)
        h0 = jnp.stack([state_ssm_re[:, l], state_ssm_im[:, l]], axis=2)
        h0 = h0.transpose(3, 0, 1, 2, 4).reshape(N_GROUPS, bd, CW)
        xs, _, _ = _layer(xs, mods, (1, bd), lam_init, rope_tabs, ck, cv, h0, weights, prep)
        xp, (k_ctx, v_ctx), (st,) = _layer(xp, mods, (0, 1), lam_init, None, None, None, None,
                                           weights, prep)
        ks_out.append(jnp.swapaxes(k_ctx, 1, 2).reshape(bp, -1, 2 * N_HEADS, HEAD_DIM))
        vs_out.append(v_ctx)
        fin = st.reshape(N_GROUPS, bp, 2, 2, SSM_STATE).transpose(1, 2, 3, 0, 4)
        hr_out.append(fin[:, :, 0])
        hi_out.append(fin[:, :, 1])
    return (xp, xs, jnp.stack(ks_out, axis=1), jnp.stack(vs_out, axis=1),
            jnp.stack(hr_out, axis=1), jnp.stack(hi_out, axis=1))
```

```python
import functools
import math

import jax
import jax.numpy as jnp
import numpy as np
from jax import lax
from jax.experimental import pallas as pl
from jax.experimental.pallas import tpu as pltpu

F32 = jnp.float32
BF16 = jnp.bfloat16

D_MODEL = 1024
GRID_W = 64
ATT_WIDTH = 512
SSM_WIDTH = 512
HEAD_DIM = 64
N_HEADS = 4
HEAD_W = 2 * HEAD_DIM
SSM_GROUP = 16
N_GROUPS = 32
SSM_STATE = 64
D_FF = 2816
N_MOD = 6
ROPE_BASE = 10000.0
NORM_EPS = 1e-6

LANES = 128
N_COL_BLOCKS = D_MODEL // LANES
CHUNK = 16
CW = CHUNK * SSM_GROUP
SCAN_BLOCK = 8
SCAN_LEVELS = 3
AT_ROWS = 16
GROUP_BLOCK = 4

ROWS_PER_TILE = 128
TOK_PER_TILE = ROWS_PER_TILE * CHUNK
SUB_TOK = 512
TQ_ITEM = 256
N_SUB = TOK_PER_TILE // SUB_TOK

VMEM_LIMIT = 56 * 1024 * 1024

NT_DIMS = (((1,), (1,)), ((), ()))
TN_DIMS = (((0,), (0,)), ((), ()))


def _cparams(n_axes):
    return pltpu.CompilerParams(
        dimension_semantics=("arbitrary",) * n_axes,
        vmem_limit_bytes=VMEM_LIMIT)


def _rms(x, g):
    ms = jnp.mean(x * x, axis=-1, keepdims=True)
    return x * lax.rsqrt(ms + NORM_EPS) * g


def _silu(x):
    return x * jax.nn.sigmoid(x)


MOD_ROWS = 8


def _mod_kernel(ctx_ref, c_ref, w_ref, b_ref, o_ref):
    n_lat = c_ref.shape[0]
    row = lax.broadcasted_iota(jnp.int32, (MOD_ROWS, D_MODEL), 0)
    cond = jnp.where(row == 0, ctx_ref[...], 0.0)
    for b in range(n_lat):
        cond = jnp.where(row == 1 + b, c_ref[b:b + 1, :], cond)
    m = jnp.dot(_silu(cond).astype(BF16), w_ref[...].astype(BF16),
                preferred_element_type=F32) + b_ref[...]
    o_ref[:, 0, :] = m


def _modulation(c_ctx, c, w_mod, b_mod):
    n = w_mod.shape[1]
    tn = 2048
    assert 1 + c.shape[0] <= MOD_ROWS
    return pl.pallas_call(
        _mod_kernel,
        grid=(n // tn,),
        in_specs=[pl.BlockSpec((1, D_MODEL), lambda i: (0, 0)),
                  pl.BlockSpec(c.shape, lambda i: (0, 0)),
                  pl.BlockSpec((D_MODEL, tn), lambda i: (0, i)),
                  pl.BlockSpec((1, tn), lambda i: (0, i))],
        out_specs=pl.BlockSpec((MOD_ROWS, 1, tn), lambda i: (0, 0, i)),
        out_shape=jax.ShapeDtypeStruct((MOD_ROWS, 1, n), F32),
        compiler_params=_cparams(1),
        name="modulation",
    )(c_ctx.reshape(1, D_MODEL), c, w_mod, b_mod.reshape(1, n))


def _rope(x, cos, sa, sb):
    return (x * cos + pltpu.roll(x, HEAD_W - 16, axis=1) * sa
            + pltpu.roll(x, 16, axis=1) * sb)


def _in_proj_kernel(*refs, rope, seqs_per_sub):
    x_ref, x3_hbm, mod_ref, g_ref, w_ref = refs[:5]
    refs = refs[5:]
    if rope:
        cos_ref, sa_ref, sb_ref = refs[:3]
        refs = refs[3:]
        q_ref, k_ref, v_ref, ut_ref, wut_ref, xt_ref, xt_sem = refs
    else:
        q_ref, k_ref, v_ref, kc_ref, vc_ref, ut_ref, wut_ref, xt_ref, xt_sem = refs
    tile = pl.program_id(0)
    j = pl.program_id(1)
    t_per_sub = CHUNK // N_SUB
    t_early = CHUNK - t_per_sub

    def gather(tile_idx, t):
        src = x3_hbm.at[pl.ds(tile_idx * ROWS_PER_TILE, ROWS_PER_TILE), t, :]
        return pltpu.make_async_copy(src, xt_ref.at[t], xt_sem.at[t])

    @pl.when(j == 0)
    def _():
        @pl.when(tile == 0)
        def _():
            for t in range(t_early):
                gather(0, t).start()
        for t in range(t_early, CHUNK):
            gather(tile, t).start()

    t_base = j * t_per_sub
    for d in range(t_per_sub):
        gather(tile, t_base + d).wait()

    @pl.when((j == N_SUB - 1) & (tile + 1 < pl.num_programs(0)))
    def _():
        for t in range(t_early):
            gather(tile + 1, t).start()

    @pl.when((tile == 0) & (j == 0))
    def _():
        wut_ref[...] = w_ref[:, 3 * ATT_WIDTH:].T

    shift = mod_ref[:, 0:D_MODEL]
    gain = g_ref[...] * (1.0 + mod_ref[:, D_MODEL:2 * D_MODEL])

    def norm_mod(xv):
        ms = jnp.mean(xv * xv, axis=-1, keepdims=True)
        return (xv * lax.rsqrt(ms + NORM_EPS) * gain + shift).astype(BF16)

    proj = jnp.dot(norm_mod(x_ref[...]), w_ref[:, 0:3 * ATT_WIDTH],
                   preferred_element_type=F32)
    q = proj[:, 0:ATT_WIDTH]
    k = proj[:, ATT_WIDTH:2 * ATT_WIDTH]
    v = proj[:, 2 * ATT_WIDTH:3 * ATT_WIDTH]
    qscale = HEAD_DIM ** -0.5 * math.log2(math.e)
    if rope:
        cos, sa, sb = cos_ref[...], sa_ref[...], sb_ref[...]
        for hd in range(N_HEADS):
            sl = slice(hd * HEAD_W, (hd + 1) * HEAD_W)
            q_ref[:, sl] = (_rope(q[:, sl], cos, sa, sb) * qscale).astype(q_ref.dtype)
            k_ref[:, sl] = _rope(k[:, sl], cos, sa, sb).astype(k_ref.dtype)
    else:
        q_ref[...] = (q * qscale).astype(q_ref.dtype)
        k_ref[...] = k.astype(k_ref.dtype)
        seq = SUB_TOK // seqs_per_sub
        k_t = k.T
        for b in range(seqs_per_sub):
            kc_ref[b] = k_t[:, b * seq:(b + 1) * seq]
            for hd in range(N_HEADS):
                vc_ref[b, :, hd, :] = v[b * seq:(b + 1) * seq, hd * HEAD_W:(hd + 1) * HEAD_W]
    v_ref[...] = v.astype(v_ref.dtype)

    for d0 in range(0, t_per_sub, 2):
        xt = jnp.concatenate([xt_ref[t_base + d0], xt_ref[t_base + d0 + 1]], axis=0)
        ut = lax.dot_general(wut_ref[...], norm_mod(xt), NT_DIMS,
                             preferred_element_type=F32)
        for d in range(2):
            blk = ut[:, d * ROWS_PER_TILE:(d + 1) * ROWS_PER_TILE]
            row0 = pl.multiple_of((t_base + d0 + d) * SSM_GROUP, SSM_GROUP)
            ut_ref[:, pl.ds(row0, SSM_GROUP), :] = (
                blk.reshape(N_GROUPS, SSM_GROUP, ROWS_PER_TILE).astype(ut_ref.dtype))


def _in_proj(x, mods, mod_rows, g0, w_in, rope_tabs):
    n_batch, seq_len = x.shape[:2]
    n_tok = n_batch * seq_len
    n_rows = n_tok // CHUNK
    n_tiles = n_tok // TOK_PER_TILE
    mod_row0, n_mod = mod_rows
    tiles_per_mod = n_tiles // n_mod
    rope = rope_tabs is not None
    seqs_per_sub = max(1, SUB_TOK // seq_len)
    in_specs = [pl.BlockSpec((SUB_TOK, D_MODEL), lambda i, j: (i * N_SUB + j, 0)),
                pl.BlockSpec(memory_space=pl.ANY),
                pl.BlockSpec((None, 1, 2 * D_MODEL),
                             lambda i, j: (mod_row0 + i // tiles_per_mod, 0, 0)),
                pl.BlockSpec((1, D_MODEL), lambda i, j: (0, 0)),
                pl.BlockSpec((D_MODEL, 4 * ATT_WIDTH), lambda i, j: (0, 0))]
    args = [x.reshape(n_tok, D_MODEL), x.reshape(n_rows, CHUNK, D_MODEL), mods, g0, w_in]
    row_spec = pl.BlockSpec((SUB_TOK, ATT_WIDTH), lambda i, j: (i * N_SUB + j, 0))
    row_shape = jax.ShapeDtypeStruct((n_tok, ATT_WIDTH), BF16)
    out_specs = [row_spec, row_spec, row_spec]
    out_shape = [row_shape, row_shape, row_shape]
    if rope:
        assert seq_len == TOK_PER_TILE
        for tab in rope_tabs:
            in_specs.append(pl.BlockSpec((SUB_TOK, HEAD_W), lambda i, j: (j, 0)))
            args.append(tab)
    else:
        out_specs += [pl.BlockSpec((seqs_per_sub, ATT_WIDTH, seq_len),
                                   lambda i, j: (i * N_SUB + j, 0, 0)),
                      pl.BlockSpec((seqs_per_sub, seq_len, N_HEADS, HEAD_W),
                                   lambda i, j: (i * N_SUB + j, 0, 0, 0))]
        out_shape += [jax.ShapeDtypeStruct((n_batch, ATT_WIDTH, seq_len), F32),
                      jax.ShapeDtypeStruct((n_batch, seq_len, N_HEADS, HEAD_W), F32)]
    out_specs.append(pl.BlockSpec((N_GROUPS, CW, ROWS_PER_TILE), lambda i, j: (0, 0, i)))
    out_shape.append(jax.ShapeDtypeStruct((N_GROUPS, CW, n_rows), BF16))
    return pl.pallas_call(
        functools.partial(_in_proj_kernel, rope=rope, seqs_per_sub=seqs_per_sub),
        grid=(n_tiles, N_SUB),
        in_specs=in_specs,
        out_specs=out_specs,
        out_shape=out_shape,
        scratch_shapes=[pltpu.VMEM((SSM_WIDTH, D_MODEL), BF16),
                        pltpu.VMEM((CHUNK, ROWS_PER_TILE, D_MODEL), F32),
                        pltpu.SemaphoreType.DMA((CHUNK,))],
        compiler_params=_cparams(2),
        name="in_proj",
    )(*args)


def _attn_kernel(*refs, has_ctx, lam_init, n_seq, seq_len, tq, n_cast):
    if n_cast:
        cast_in = refs[len(refs) - 2 * n_cast - 1:len(refs) - n_cast - 1]
        cast_out = refs[len(refs) - n_cast:]
        refs = refs[:len(refs) - 2 * n_cast - 1] + (refs[len(refs) - n_cast - 1],)
        for src, dst in zip(cast_in, cast_out):
            dst[...] = src[...].astype(dst.dtype)
    if has_ctx:
        lam_ref, sg_ref, q_ref, ck_ref, cv_ref, k_ref, v_ref, o_ref = refs
    else:
        lam_ref, sg_ref, q_ref, k_ref, v_ref, o_ref = refs
    lp = lam_ref[...]
    lam = (jnp.exp(jnp.sum(lp[0:1] * lp[1:2], axis=-1, keepdims=True))
           - jnp.exp(jnp.sum(lp[2:3] * lp[3:4], axis=-1, keepdims=True)) + lam_init)
    first_map = lax.broadcasted_iota(jnp.int32, (1, HEAD_W), 1) < HEAD_DIM
    ti = min(TQ_ITEM, tq)
    for row0 in range(0, n_seq * tq, ti):
        b = row0 // tq
        q_rows = slice(row0, row0 + ti)
        for hd in range(N_HEADS):
            sl = slice(hd * HEAD_W, (hd + 1) * HEAD_W)
            qh = q_ref[q_rows, sl]
            zero = jnp.zeros_like(qh)
            qs = jnp.concatenate([jnp.where(first_map, qh, zero),
                                  jnp.where(first_map, zero, qh)], axis=0)
            kv_rows = slice(b * seq_len, (b + 1) * seq_len)
            parts = [(k_ref[kv_rows, sl], v_ref[kv_rows, sl])]
            if has_ctx:
                parts.insert(0, (ck_ref[:, sl].astype(BF16), cv_ref[:, sl].astype(BF16)))
            scores = [lax.dot_general(qs, kk, NT_DIMS, preferred_element_type=F32)
                      for kk, _ in parts]
            mx = scores[0].max(axis=-1, keepdims=True)
            for s in scores[1:]:
                mx = jnp.maximum(mx, s.max(axis=-1, keepdims=True))
            acc = None
            for s, (_, vv) in zip(scores, parts):
                e = jnp.exp2(s - mx).astype(BF16)
                v_one = jnp.concatenate([vv, jnp.ones_like(vv)], axis=1)
                pv = jnp.dot(e, v_one, preferred_element_type=F32)
                acc = pv if acc is None else acc + pv
            num = acc[:, 0:HEAD_W] / acc[:, HEAD_W:2 * HEAD_W]
            o = num[0:ti] - lam * num[ti:2 * ti]
            o = _rms(o, sg_ref[...]) * (1.0 - lam_init)
            o_ref[q_rows, sl] = o.astype(o_ref.dtype)


def _attention(q, k, v, ctx_k, ctx_v, lam_params, subln_g, n_batch, seq_len, lam_init,
               cast_weights=()):
    has_ctx = ctx_k is not None
    tq = min(1024, seq_len)
    n_q = seq_len // tq
    n_seq = 1 if n_q > 1 else min(4, n_batch)
    in_specs = [pl.BlockSpec((4, HEAD_DIM), lambda b, i: (0, 0)),
                pl.BlockSpec((1, HEAD_W), lambda b, i: (0, 0)),
                pl.BlockSpec((n_seq * tq, ATT_WIDTH), lambda b, i: (b * n_q + i, 0))]
    args = [lam_params, subln_g.reshape(1, HEAD_W), q]
    if has_ctx:
        past = ctx_k.shape[1]
        ctx_spec = pl.BlockSpec((None, past, ATT_WIDTH), lambda b, i: (b, 0, 0))
        in_specs += [ctx_spec, ctx_spec]
        args += [ctx_k, ctx_v]
    kv_spec = pl.BlockSpec((n_seq * seq_len, ATT_WIDTH), lambda b, i: (b, 0))
    in_specs += [kv_spec, kv_spec]
    args += [k, v]
    out_specs = [pl.BlockSpec((n_seq * tq, ATT_WIDTH), lambda b, i: (b * n_q + i, 0))]
    out_shape = [jax.ShapeDtypeStruct((n_batch * seq_len, ATT_WIDTH), BF16)]
    n_steps = (n_batch // n_seq) * n_q
    for w in cast_weights:
        rows = w.shape[0] // n_steps
        spec = pl.BlockSpec((rows, w.shape[1]), lambda b, i: (b * n_q + i, 0))
        in_specs.append(spec)
        args.append(w)
        out_specs.append(spec)
        out_shape.append(jax.ShapeDtypeStruct(w.shape, BF16))
    outs = pl.pallas_call(
        functools.partial(_attn_kernel, has_ctx=has_ctx, lam_init=lam_init,
                          n_seq=n_seq, seq_len=seq_len, tq=tq, n_cast=len(cast_weights)),
        grid=(n_batch // n_seq, n_q),
        in_specs=in_specs,
        out_specs=out_specs,
        out_shape=out_shape,
        compiler_params=_cparams(2),
        name="diff_attention",
    )(*args)
    return outs[0], tuple(outs[1:])


def _cmul(ar, ai, br, bi):
    return ar * br - ai * bi, ar * bi + ai * br


def _ssm_prep_kernel(lre_ref, lim_ref, ls_ref, bre_ref, bim_ref, cre_ref, cim_ref, d_ref,
                     mt_ref, gt_ref, wo_ref, at_ref):
    lane = lax.broadcasted_iota(jnp.int32, (SSM_GROUP, CW), 1)
    chan = lax.broadcasted_iota(jnp.int32, (SSM_GROUP, CW), 0)
    for gi in range(GROUP_BLOCK):
        gt_cols, wo_cols, at_cols, toeplitz = [], [], [], []
        for dr in range(2):
            lr = jnp.minimum(lre_ref[dr, gi], -1e-4)
            li = lim_ref[dr, gi]
            step = jnp.exp(ls_ref[dr, gi])
            mag = jnp.exp(lr * step)
            a_re = mag * jnp.cos(li * step)
            a_im = mag * jnp.sin(li * step)
            den = lr * lr + li * li
            nr = a_re - 1.0
            f_re = (nr * lr + a_im * li) / den
            f_im = (a_im * lr - nr * li) / den
            bt_re, bt_im = bre_ref[dr, gi], bim_ref[dr, gi]
            bb_re, bb_im = _cmul(f_re, f_im, bt_re, bt_im)
            c_re, c_im = cre_ref[dr, gi], cim_ref[dr, gi]
            pw = [(jnp.ones_like(a_re), jnp.zeros_like(a_im))]
            for _ in range(CHUNK):
                pw.append(_cmul(pw[-1][0], pw[-1][1], a_re, a_im))
            g_re, g_im, e_re, e_im = [], [], [], []
            for t in range(CHUNK):
                pr, pi = pw[CHUNK - 1 - t] if dr == 0 else pw[t]
                r, i = _cmul(bb_re, bb_im, pr, pi)
                g_re.append(r)
                g_im.append(i)
                pr, pi = pw[t + 1] if dr == 0 else pw[CHUNK - t]
                r, i = _cmul(c_re, c_im, pr, pi)
                e_re.append(r)
                e_im.append(-i)
            g_cat = jnp.concatenate([jnp.concatenate(g_re, axis=0),
                                     jnp.concatenate(g_im, axis=0)], axis=1)
            gt_cols.append(g_cat)
            wo_cols.append(jnp.concatenate([jnp.concatenate(e_re, axis=0),
                                            jnp.concatenate(e_im, axis=0)], axis=1))
            c_cat = jnp.concatenate([c_re, -c_im], axis=1)
            toeplitz.append(lax.dot_general(c_cat, g_cat, NT_DIMS,
                                            precision=lax.Precision.HIGHEST,
                                            preferred_element_type=F32))
            apw = [pw[CHUNK]]
            for _ in range(SCAN_BLOCK - 1):
                apw.append(_cmul(apw[-1][0], apw[-1][1], apw[0][0], apw[0][1]))
            order = list(range(SCAN_BLOCK)) if dr == 0 else list(range(SCAN_BLOCK - 1, -1, -1))
            order += [2 ** l - 1 for l in range(SCAN_LEVELS)]
            order += [0] * (AT_ROWS - len(order))
            at_cols += [jnp.concatenate([jnp.concatenate([apw[i][0], apw[i][0]], axis=1)
                                         for i in order], axis=0),
                        jnp.concatenate([jnp.concatenate([-apw[i][1], apw[i][1]], axis=1)
                                         for i in order], axis=0)]
        kf_rev, kb = toeplitz
        d_skip = d_ref[gi]
        blocks = []
        for t in range(CHUNK):
            fwd = pltpu.roll(kf_rev, (CW - (CHUNK - 1 - t) * SSM_GROUP) % CW, axis=1)
            bwd = pltpu.roll(kb, t * SSM_GROUP, axis=1)
            blocks.append(jnp.where(lane < (t + 1) * SSM_GROUP, fwd, 0.0)
                          + jnp.where(lane >= t * SSM_GROUP, bwd, 0.0)
                          + jnp.where(lane == chan + t * SSM_GROUP, d_skip, 0.0))
        mt_ref[gi] = jnp.concatenate(blocks, axis=0).astype(mt_ref.dtype)
        gt_ref[gi] = jnp.concatenate(gt_cols, axis=1).astype(gt_ref.dtype)
        wo_ref[gi] = jnp.concatenate(wo_cols, axis=1).astype(wo_ref.dtype)
        at_ref[gi] = jnp.concatenate(at_cols, axis=1)


def _ssm_prep(lam_re, lam_im, log_step, b_re, b_im, c_re, c_im, d_skip):
    row = lambda a: a.reshape(2, N_GROUPS, 1, SSM_STATE)
    bt = lambda a: jnp.swapaxes(a, 2, 3)
    d_row = jnp.tile((d_skip[0] + d_skip[1]).reshape(N_GROUPS, 1, SSM_GROUP), (1, 1, CHUNK))
    gb = GROUP_BLOCK
    vec_spec = pl.BlockSpec((2, gb, 1, SSM_STATE), lambda i: (0, i, 0, 0))
    mat_spec = pl.BlockSpec((2, gb, SSM_GROUP, SSM_STATE), lambda i: (0, i, 0, 0))
    w_spec = pl.BlockSpec((gb, CW, CW), lambda i: (i, 0, 0))
    w_shape = jax.ShapeDtypeStruct((N_GROUPS, CW, CW), BF16)
    return pl.pallas_call(
        _ssm_prep_kernel,
        grid=(N_GROUPS // gb,),
        in_specs=[vec_spec, vec_spec,
                  pl.BlockSpec((2, gb, 1, 1), lambda i: (0, i, 0, 0)),
                  mat_spec, mat_spec, mat_spec, mat_spec,
                  pl.BlockSpec((gb, 1, CW), lambda i: (i, 0, 0))],
        out_specs=[w_spec, w_spec, w_spec,
                   pl.BlockSpec((gb, AT_ROWS, 4 * 2 * SSM_STATE), lambda i: (i, 0, 0))],
        out_shape=[w_shape, w_shape, w_shape,
                   jax.ShapeDtypeStruct((N_GROUPS, AT_ROWS, 4 * 2 * SSM_STATE), F32)],
        compiler_params=_cparams(1),
        name="ssm_prep",
    )(row(lam_re), row(lam_im), log_step.reshape(2, N_GROUPS, 1, 1),
      bt(b_re), bt(b_im), c_re, c_im, d_row)


def _shift_rows(x, m, down):
    n = x.shape[0]
    return pltpu.roll(x, m if down else n - m, axis=0)


def _ssm_kernel(*refs, n_seq, has_h0):
    if has_h0:
        (xt_ref, mt_ref, gt_ref, wo_ref, at_ref, wg_ref, bg_ref, h0_ref,
         z_ref, zs_ref) = refs
    else:
        (xt_ref, mt_ref, gt_ref, wo_ref, at_ref, wg_ref, bg_ref,
         z_ref, st_ref, zs_ref, fin_ref) = refs
    step = pl.program_id(0)
    n_rows = xt_ref.shape[-1]
    seg = n_rows // n_seq
    cw2 = 2 * SSM_STATE
    assert seg % SCAN_BLOCK == 0
    pos = lax.broadcasted_iota(jnp.int32, (n_rows, cw2), 0) % seg
    row_blk = lax.broadcasted_iota(jnp.int32, (SCAN_BLOCK, cw2), 0)
    for gi in range(GROUP_BLOCK):
        xt = xt_ref[gi]
        at = at_ref[gi]
        s_all = lax.dot_general(xt, gt_ref[gi], TN_DIMS, preferred_element_type=F32)
        h_parts = []
        for dr in range(2):
            s = s_all[:, dr * cw2:(dr + 1) * cw2]
            p_tab = at[:, (2 * dr) * cw2:(2 * dr + 1) * cw2]
            q_tab = at[:, (2 * dr + 1) * cw2:(2 * dr + 2) * cw2]
            down = dr == 0
            if has_h0:
                h0 = jnp.zeros((n_rows, cw2), F32)
                riota = lax.broadcasted_iota(jnp.int32, (n_rows, cw2), 0)
                for b in range(n_seq):
                    edge = b * seg if down else b * seg + seg - 1
                    h0 = jnp.where(riota == edge, h0_ref[gi, b:b + 1, dr * cw2:(dr + 1) * cw2], h0)
                lv0 = SCAN_BLOCK
                s = (s + p_tab[lv0:lv0 + 1] * h0
                     + q_tab[lv0:lv0 + 1] * pltpu.roll(h0, SSM_STATE, axis=1))
            n_blk = n_rows // SCAN_BLOCK
            s3 = s.reshape(n_blk, SCAN_BLOCK, cw2)
            for lvl in range(SCAN_LEVELS):
                m = 2 ** lvl
                valid = (row_blk >= m) if down else (row_blk < SCAN_BLOCK - m)
                row = SCAN_BLOCK + lvl
                p = jnp.where(valid, p_tab[row:row + 1], 0.0)
                q = jnp.where(valid, q_tab[row:row + 1], 0.0)
                sh = pltpu.roll(s3, m if down else SCAN_BLOCK - m, axis=1)
                s3 = s3 + p * sh + q * pltpu.roll(sh, SSM_STATE, axis=2)
            sw3 = pltpu.roll(s3, SSM_STATE, axis=2)
            p_blk, q_blk = p_tab[0:SCAN_BLOCK], q_tab[0:SCAN_BLOCK]
            blocks = [s3[i] for i in range(n_blk)]
            blocks_w = [sw3[i] for i in range(n_blk)]
            blk_per_seq = seg // SCAN_BLOCK
            edge = slice(SCAN_BLOCK - 1, SCAN_BLOCK) if down else slice(0, 1)
            for q_i in range(n_seq):
                idxs = list(range(q_i * blk_per_seq, (q_i + 1) * blk_per_seq))
                idxs = idxs if down else idxs[::-1]
                for prev, cur in zip(idxs[:-1], idxs[1:]):
                    c = jnp.broadcast_to(blocks[prev][edge], (SCAN_BLOCK, cw2))
                    cw = jnp.broadcast_to(blocks_w[prev][edge], (SCAN_BLOCK, cw2))
                    blocks[cur] = blocks[cur] + p_blk * c + q_blk * cw
                    blocks_w[cur] = blocks_w[cur] + p_blk * cw - q_blk * c
            s = jnp.concatenate(blocks, axis=0)
            if not has_h0:
                fin_ref[dr] = s
                st_ref[gi, :, dr * cw2:(dr + 1) * cw2] = (
                    fin_ref[dr, pl.ds(seg - 1 if down else 0, n_seq, stride=seg), :])
            ent = _shift_rows(s, 1, down)
            ent = jnp.where((pos >= 1) if down else (pos < seg - 1), ent, 0.0)
            if has_h0:
                ent = ent + h0
            h_parts.append(ent)
        h_all = jnp.concatenate(h_parts, axis=1).astype(BF16)
        yt = (jnp.dot(mt_ref[gi], xt, preferred_element_type=F32)
              + lax.dot_general(wo_ref[gi], h_all, NT_DIMS, preferred_element_type=F32))
        z = jax.nn.gelu(yt, approximate=True)
        grp = step * GROUP_BLOCK + gi
        for t in range(CHUNK):
            zs_ref[t, pl.ds(pl.multiple_of(grp * SSM_GROUP, SSM_GROUP), SSM_GROUP), :] = (
                z[t * SSM_GROUP:(t + 1) * SSM_GROUP, :])

    @pl.when(step == pl.num_programs(0) - 1)
    def _():
        for t in range(CHUNK):
            zt = zs_ref[t]
            gate = jnp.dot(wg_ref[...], zt.astype(BF16),
                           preferred_element_type=F32) + bg_ref[...]
            z_ref[t] = (zt * jax.nn.sigmoid(gate)).astype(z_ref.dtype)


def _ssm(xt, mt, gt, wo, at, w_glu_t, b_glu_col, h0, n_seq):
    n_rows = xt.shape[-1]
    has_h0 = h0 is not None
    gb = GROUP_BLOCK
    w_spec = pl.BlockSpec((gb, CW, CW), lambda i: (i, 0, 0))
    in_specs = [pl.BlockSpec((gb, CW, n_rows), lambda i: (i, 0, 0)),
                w_spec, w_spec, w_spec,
                pl.BlockSpec((gb, AT_ROWS, 4 * 2 * SSM_STATE), lambda i: (i, 0, 0)),
                pl.BlockSpec((SSM_WIDTH, SSM_WIDTH), lambda i: (0, 0)),
                pl.BlockSpec((SSM_WIDTH, 1), lambda i: (0, 0))]
    args = [xt, mt, gt, wo, at, w_glu_t, b_glu_col]
    out_specs = [pl.BlockSpec((CHUNK, SSM_WIDTH, n_rows), lambda i: (0, 0, 0))]
    out_shape = [jax.ShapeDtypeStruct((CHUNK, SSM_WIDTH, n_rows), BF16)]
    scratch = [pltpu.VMEM((CHUNK, SSM_WIDTH, n_rows), F32)]
    if has_h0:
        in_specs.append(pl.BlockSpec((gb, n_seq, CW), lambda i: (i, 0, 0)))
        args.append(h0)
    else:
        out_specs.append(pl.BlockSpec((gb, n_seq, CW), lambda i: (i, 0, 0)))
        out_shape.append(jax.ShapeDtypeStruct((N_GROUPS, n_seq, CW), F32))
        scratch.append(pltpu.VMEM((2, n_rows, 2 * SSM_STATE), F32))
    return pl.pallas_call(
        functools.partial(_ssm_kernel, n_seq=n_seq, has_h0=has_h0),
        grid=(N_GROUPS // gb,),
        in_specs=in_specs,
        out_specs=out_specs,
        out_shape=out_shape,
        scratch_shapes=scratch,
        compiler_params=_cparams(1),
        name="ssm_scan_glu",
    )(*args)


FF_CHUNK = 256


def _out_ffn_kernel(x_ref, attn_ref, zt_ref, mod_ref, g_ref, wo_ref, wfi_ref, wfo_ref,
                    o_ref, mixz_ref, act_ref):
    j = pl.program_id(1)

    @pl.when(j == 0)
    def _():
        for t in range(CHUNK):
            part = lax.dot_general(zt_ref[t], wo_ref[ATT_WIDTH:, :], TN_DIMS,
                                   preferred_element_type=F32)
            for cb in range(N_COL_BLOCKS):
                mixz_ref[cb, pl.ds(t, ROWS_PER_TILE, stride=CHUNK), :] = (
                    part[:, cb * LANES:(cb + 1) * LANES])

    gate1 = mod_ref[:, 2 * D_MODEL:3 * D_MODEL]
    shift2 = mod_ref[:, 3 * D_MODEL:4 * D_MODEL]
    scale2 = mod_ref[:, 4 * D_MODEL:5 * D_MODEL]
    gate2 = mod_ref[:, 5 * D_MODEL:6 * D_MODEL]
    half = SUB_TOK // 2
    gain2 = g_ref[2:3, :] * (1.0 + scale2)

    def pre_ffn(hf):
        r = slice(hf * half, (hf + 1) * half)
        rows = pl.ds(pl.multiple_of(j * SUB_TOK + hf * half, half), half)
        mix = (jnp.dot(attn_ref[r, :], wo_ref[0:ATT_WIDTH, :], preferred_element_type=F32)
               + jnp.concatenate([mixz_ref[cb, rows, :] for cb in range(N_COL_BLOCKS)], axis=1))
        x1 = x_ref[r, :] + gate1 * _rms(mix, g_ref[1:2, :])
        ms = jnp.mean(x1 * x1, axis=-1, keepdims=True)
        return x1, (x1 * lax.rsqrt(ms + NORM_EPS) * gain2 + shift2).astype(BF16)

    def ffn_in(hf, h, chunks):
        r = slice(hf * half, (hf + 1) * half)
        for c in chunks:
            lo = c * FF_CHUNK
            gt = jnp.dot(h, wfi_ref[:, lo:lo + FF_CHUNK], preferred_element_type=F32)
            up = jnp.dot(h, wfi_ref[:, D_FF + lo:D_FF + lo + FF_CHUNK],
                         preferred_element_type=F32)
            act_ref[r, lo:lo + FF_CHUNK] = (_silu(gt) * up).astype(BF16)

    def ffn_out(hf, x1):
        r = slice(hf * half, (hf + 1) * half)
        f = jnp.dot(act_ref[r, :], wfo_ref[...], preferred_element_type=F32)
        o_ref[r, :] = x1 + gate2 * _rms(f, g_ref[3:4, :])

    n_chunks = D_FF // FF_CHUNK
    x1_a, h_a = pre_ffn(0)
    x1_b, h_b = pre_ffn(1)
    ffn_in(0, h_a, range(n_chunks))
    ffn_out(0, x1_a)
    ffn_in(1, h_b, range(n_chunks))
    ffn_out(1, x1_b)


def _out_ffn(x2d, attn, zt, mods, mod_rows, norm_g, w_o, w_ffn_in, w_ffn_out):
    n_tok = x2d.shape[0]
    n_tiles = n_tok // TOK_PER_TILE
    mod_row0, n_mod = mod_rows
    tiles_per_mod = n_tiles // n_mod
    const = lambda i, j: (0, 0)
    row_spec = lambda w: pl.BlockSpec((SUB_TOK, w), lambda i, j: (i * N_SUB + j, 0))
    return pl.pallas_call(
        _out_ffn_kernel,
        grid=(n_tiles, N_SUB),
        in_specs=[row_spec(D_MODEL), row_spec(ATT_WIDTH),
                  pl.BlockSpec((CHUNK, SSM_WIDTH, ROWS_PER_TILE), lambda i, j: (0, 0, i)),
                  pl.BlockSpec((None, 1, N_MOD * D_MODEL),
                               lambda i, j: (mod_row0 + i // tiles_per_mod, 0, 0)),
                  pl.BlockSpec((4, D_MODEL), const),
                  pl.BlockSpec((2 * ATT_WIDTH, D_MODEL), const, pipeline_mode=pl.Buffered(1)),
                  pl.BlockSpec((D_MODEL, 2 * D_FF), const, pipeline_mode=pl.Buffered(1)),
                  pl.BlockSpec((D_FF, D_MODEL), const, pipeline_mode=pl.Buffered(1))],
        out_specs=row_spec(D_MODEL),
        out_shape=jax.ShapeDtypeStruct((n_tok, D_MODEL), F32),
        scratch_shapes=[pltpu.VMEM((N_COL_BLOCKS, TOK_PER_TILE, LANES), F32),
                        pltpu.VMEM((SUB_TOK, D_FF), BF16)],
        compiler_params=_cparams(2),
        name="out_proj_ffn",
    )(x2d, attn, zt, mods, norm_g, w_o, w_ffn_in, w_ffn_out)


def _rope_tables(seq_len):
    t = np.arange(seq_len)
    row = (t // GRID_W).astype(np.float32)
    col = (t % GRID_W).astype(np.float32)
    half = HEAD_DIM // 2
    inv_freq = (np.float32(ROPE_BASE)
                ** (-np.arange(0, half, 2, dtype=np.float32) / np.float32(half))).astype(np.float32)
    ang_r = row[:, None] * inv_freq
    ang_c = col[:, None] * inv_freq
    ang = np.concatenate([ang_r, ang_r, ang_c, ang_c], axis=-1)
    cos, sin = np.cos(ang), np.sin(ang)
    upper = (np.arange(HEAD_DIM) % 32) < 16
    sa = np.where(upper, -sin, 0.0)
    sb = np.where(upper, 0.0, sin)
    two = lambda a: jnp.asarray(np.concatenate([a, a], axis=-1), dtype=F32)
    return two(cos), two(sa), two(sb)


def _layer(x, mods, mod_rows, lam_init, rope_tabs, ctx_k, ctx_v, h0, weights, prep):
    n_batch, seq_len = x.shape[:2]
    g = weights['norm_g']
    outs = _in_proj(x, mods, mod_rows, g[0:1], weights['w_in'], rope_tabs)
    q, k, v = outs[:3]
    pending = () if 'late_bf16' in weights else weights['late_f32']
    attn, cast = _attention(q, k, v, ctx_k, ctx_v, weights['lam'], weights['subln_g'],
                            n_batch, seq_len, lam_init, cast_weights=pending)
    if pending:
        weights['late_bf16'] = cast
    ssm_out = _ssm(outs[-1], *prep, weights['w_glu_t'], weights['b_glu_col'], h0, n_batch)
    y = _out_ffn(x.reshape(n_batch * seq_len, D_MODEL), attn, ssm_out[0], mods, mod_rows, g,
                 *weights['late_bf16'])
    return y.reshape(x.shape), outs[3:-1], ssm_out[1:]


def kernel(x_prompt, x_sample, cache_k, cache_v, state_ssm_re, state_ssm_im, c, c_ctx, w_mod, b_mod, norm_g, w_in, lam_params, subln_g, ssm_lambda_re, ssm_lambda_im, ssm_log_step, ssm_b_re, ssm_b_im, ssm_c_re, ssm_c_im, ssm_d, w_glu, b_glu, w_o, w_ffn_in, w_ffn_out):
    depth = w_mod.shape[0]
    assert depth == 1
    bp = x_prompt.shape[0]
    bd, ld_len = x_sample.shape[:2]
    past = cache_k.shape[2]
    xp, xs = x_prompt, x_sample
    rope_tabs = _rope_tables(ld_len)
    ks_out, vs_out, hr_out, hi_out = [], [], [], []
    for l in range(depth):
        lam_init = 0.8 - 0.6 * math.exp(-0.3 * l)
        mods = _modulation(c_ctx, c, w_mod[l], b_mod[l])
        weights = {
            'norm_g': norm_g[l],
            'w_in': w_in[l].astype(BF16),
            'lam': lam_params[l], 'subln_g': subln_g[l],
            'w_glu_t': w_glu[l].T.astype(BF16), 'b_glu_col': b_glu[l].reshape(SSM_WIDTH, 1),
            'late_f32': (w_o[l], w_ffn_in[l], w_ffn_out[l]),
        }
        prep = _ssm_prep(ssm_lambda_re[l], ssm_lambda_im[l], ssm_log_step[l],
                         ssm_b_re[l], ssm_b_im[l], ssm_c_re[l], ssm_c_im[l], ssm_d[l])
        ck = cache_k[:, l].reshape(bd, past, ATT_WIDTH)
        cv = cache_v[:, l].reshape(bd, past, ATT_WIDTH)
        h0 = jnp.stack([state_ssm_re[:, l], state_ssm_im[:, l]], axis=2)
        h0 = h0.transpose(3, 0, 1, 2, 4).reshape(N_GROUPS, bd, CW)
        xs, _, _ = _layer(xs, mods, (1, bd), lam_init, rope_tabs, ck, cv, h0, weights, prep)
        xp, (k_ctx, v_ctx), (st,) = _layer(xp, mods, (0, 1), lam_init, None, None, None, None,
                                           weights, prep)
        ks_out.append(jnp.swapaxes(k_ctx, 1, 2).reshape(bp, -1, 2 * N_HEADS, HEAD_DIM))
        vs_out.append(v_ctx)
        fin = st.reshape(N_GROUPS, bp, 2, 2, SSM_STATE).transpose(1, 2, 3, 0, 4)
        hr_out.append(fin[:, :, 0])
        hi_out.append(fin[:, :, 1])
    return (xp, xs, jnp.stack(ks_out, axis=1), jnp.stack(vs_out, axis=1),
            jnp.stack(hr_out, axis=1), jnp.stack(hi_out, axis=1))
```

```python
import functools
import math

import jax
import jax.numpy as jnp
import numpy as np
from jax import lax
from jax.experimental import pallas as pl
from jax.experimental.pallas import tpu as pltpu

F32 = jnp.float32
BF16 = jnp.bfloat16

D_MODEL = 1024
GRID_W = 64
ATT_WIDTH = 512
SSM_WIDTH = 512
HEAD_DIM = 64
N_HEADS = 4
HEAD_W = 2 * HEAD_DIM
SSM_GROUP = 16
N_GROUPS = 32
SSM_STATE = 64
D_FF = 2816
N_MOD = 6
ROPE_BASE = 10000.0
NORM_EPS = 1e-6

LANES = 128
N_COL_BLOCKS = D_MODEL // LANES
CHUNK = 16
CW = CHUNK * SSM_GROUP
SCAN_BLOCK = 8
SCAN_LEVELS = 3
AT_ROWS = 16
GROUP_BLOCK = 4

ROWS_PER_TILE = 128
TOK_PER_TILE = ROWS_PER_TILE * CHUNK
SUB_TOK = 512
TQ_ITEM = 256
N_SUB = TOK_PER_TILE // SUB_TOK

VMEM_LIMIT = 56 * 1024 * 1024

NT_DIMS = (((1,), (1,)), ((), ()))
TN_DIMS = (((0,), (0,)), ((), ()))


def _cparams(n_axes):
    return pltpu.CompilerParams(
        dimension_semantics=("arbitrary",) * n_axes,
        vmem_limit_bytes=VMEM_LIMIT)


def _rms(x, g):
    ms = jnp.mean(x * x, axis=-1, keepdims=True)
    return x * lax.rsqrt(ms + NORM_EPS) * g


def _silu(x):
    return x * jax.nn.sigmoid(x)


MOD_ROWS = 8


def _mod_kernel(ctx_ref, c_ref, w_ref, b_ref, o_ref):
    n_lat = c_ref.shape[0]
    row = lax.broadcasted_iota(jnp.int32, (MOD_ROWS, D_MODEL), 0)
    cond = jnp.where(row == 0, ctx_ref[...], 0.0)
    for b in range(n_lat):
        cond = jnp.where(row == 1 + b, c_ref[b:b + 1, :], cond)
    m = jnp.dot(_silu(cond).astype(BF16), w_ref[...].astype(BF16),
                preferred_element_type=F32) + b_ref[...]
    o_ref[:, 0, :] = m


def _modulation(c_ctx, c, w_mod, b_mod):
    n = w_mod.shape[1]
    tn = 2048
    assert 1 + c.shape[0] <= MOD_ROWS
    return pl.pallas_call(
        _mod_kernel,
        grid=(n // tn,),
        in_specs=[pl.BlockSpec((1, D_MODEL), lambda i: (0, 0)),
                  pl.BlockSpec(c.shape, lambda i: (0, 0)),
                  pl.BlockSpec((D_MODEL, tn), lambda i: (0, i)),
                  pl.BlockSpec((1, tn), lambda i: (0, i))],
        out_specs=pl.BlockSpec((MOD_ROWS, 1, tn), lambda i: (0, 0, i)),
        out_shape=jax.ShapeDtypeStruct((MOD_ROWS, 1, n), F32),
        compiler_params=_cparams(1),
        name="modulation",
    )(c_ctx.reshape(1, D_MODEL), c, w_mod, b_mod.reshape(1, n))


def _rope(x, cos, sa, sb):
    return (x * cos + pltpu.roll(x, HEAD_W - 16, axis=1) * sa
            + pltpu.roll(x, 16, axis=1) * sb)


def _in_proj_kernel(*refs, rope, seqs_per_sub):
    x0_ref, xn_ref, x3_hbm, mod_ref, modn_ref, g_ref, w_ref = refs[:7]
    refs = refs[7:]
    if rope:
        cos_ref, sa_ref, sb_ref = refs[:3]
        refs = refs[3:]
        q_ref, k_ref, v_ref, ut_ref, wut_ref, xt_ref, xt_sem, hn_ref = refs
    else:
        q_ref, k_ref, v_ref, kc_ref, vc_ref, ut_ref, wut_ref, xt_ref, xt_sem, hn_ref = refs
    tile = pl.program_id(0)
    j = pl.program_id(1)
    t_per_sub = CHUNK // N_SUB
    t_early = CHUNK - t_per_sub

    def gather(tile_idx, t):
        src = x3_hbm.at[pl.ds(tile_idx * ROWS_PER_TILE, ROWS_PER_TILE), t, :]
        return pltpu.make_async_copy(src, xt_ref.at[t], xt_sem.at[t])

    @pl.when(j == 0)
    def _():
        @pl.when(tile == 0)
        def _():
            for t in range(t_early):
                gather(0, t).start()
        for t in range(t_early, CHUNK):
            gather(tile, t).start()

    t_base = j * t_per_sub
    for d in range(t_per_sub):
        gather(tile, t_base + d).wait()

    @pl.when((j == N_SUB - 1) & (tile + 1 < pl.num_programs(0)))
    def _():
        for t in range(t_early):
            gather(tile + 1, t).start()

    def norm_mod(xv, m_ref):
        gain = g_ref[...] * (1.0 + m_ref[:, D_MODEL:2 * D_MODEL])
        ms = jnp.mean(xv * xv, axis=-1, keepdims=True)
        return (xv * lax.rsqrt(ms + NORM_EPS) * gain + m_ref[:, 0:D_MODEL]).astype(BF16)

    slot = (tile * N_SUB + j) % 2

    @pl.when((tile == 0) & (j == 0))
    def _():
        wut_ref[...] = w_ref[:, 3 * ATT_WIDTH:].T
        hn_ref[0] = norm_mod(x0_ref[...], mod_ref)

    proj = jnp.dot(hn_ref[slot], w_ref[:, 0:3 * ATT_WIDTH], preferred_element_type=F32)
    hn_ref[1 - slot] = norm_mod(xn_ref[...], modn_ref)
    q = proj[:, 0:ATT_WIDTH]
    k = proj[:, ATT_WIDTH:2 * ATT_WIDTH]
    v = proj[:, 2 * ATT_WIDTH:3 * ATT_WIDTH]
    qscale = HEAD_DIM ** -0.5 * math.log2(math.e)
    if rope:
        cos, sa, sb = cos_ref[...], sa_ref[...], sb_ref[...]
        for hd in range(N_HEADS):
            sl = slice(hd * HEAD_W, (hd + 1) * HEAD_W)
            q_ref[:, sl] = (_rope(q[:, sl], cos, sa, sb) * qscale).astype(q_ref.dtype)
            k_ref[:, sl] = _rope(k[:, sl], cos, sa, sb).astype(k_ref.dtype)
    else:
        q_ref[...] = (q * qscale).astype(q_ref.dtype)
        k_ref[...] = k.astype(k_ref.dtype)
        seq = SUB_TOK // seqs_per_sub
        k_t = k.T
        for b in range(seqs_per_sub):
            kc_ref[b] = k_t[:, b * seq:(b + 1) * seq]
            for hd in range(N_HEADS):
                vc_ref[b, :, hd, :] = v[b * seq:(b + 1) * seq, hd * HEAD_W:(hd + 1) * HEAD_W]
    v_ref[...] = v.astype(v_ref.dtype)

    for d0 in range(0, t_per_sub, 2):
        xt = jnp.concatenate([xt_ref[t_base + d0], xt_ref[t_base + d0 + 1]], axis=0)
        ut = lax.dot_general(wut_ref[...], norm_mod(xt, mod_ref), NT_DIMS,
                             preferred_element_type=F32)
        for d in range(2):
            blk = ut[:, d * ROWS_PER_TILE:(d + 1) * ROWS_PER_TILE]
            row0 = pl.multiple_of((t_base + d0 + d) * SSM_GROUP, SSM_GROUP)
            ut_ref[:, pl.ds(row0, SSM_GROUP), :] = (
                blk.reshape(N_GROUPS, SSM_GROUP, ROWS_PER_TILE).astype(ut_ref.dtype))


def _in_proj(x, mods, mod_rows, g0, w_in, rope_tabs):
    n_batch, seq_len = x.shape[:2]
    n_tok = n_batch * seq_len
    n_rows = n_tok // CHUNK
    n_tiles = n_tok // TOK_PER_TILE
    mod_row0, n_mod = mod_rows
    tiles_per_mod = n_tiles // n_mod
    rope = rope_tabs is not None
    seqs_per_sub = max(1, SUB_TOK // seq_len)
    last = n_tiles * N_SUB - 1
    nxt = lambda i, j: jnp.minimum(i * N_SUB + j + 1, last)
    x2d = x.reshape(n_tok, D_MODEL)
    in_specs = [pl.BlockSpec((SUB_TOK, D_MODEL), lambda i, j: (0, 0)),
                pl.BlockSpec((SUB_TOK, D_MODEL), lambda i, j: (nxt(i, j), 0)),
                pl.BlockSpec(memory_space=pl.ANY),
                pl.BlockSpec((None, 1, 2 * D_MODEL),
                             lambda i, j: (mod_row0 + i // tiles_per_mod, 0, 0)),
                pl.BlockSpec((None, 1, 2 * D_MODEL),
                             lambda i, j: (mod_row0 + nxt(i, j) // (N_SUB * tiles_per_mod), 0, 0)),
                pl.BlockSpec((1, D_MODEL), lambda i, j: (0, 0)),
                pl.BlockSpec((D_MODEL, 4 * ATT_WIDTH), lambda i, j: (0, 0))]
    args = [x2d, x2d, x.reshape(n_rows, CHUNK, D_MODEL), mods, mods, g0, w_in]
    row_spec = pl.BlockSpec((SUB_TOK, ATT_WIDTH), lambda i, j: (i * N_SUB + j, 0))
    row_shape = jax.ShapeDtypeStruct((n_tok, ATT_WIDTH), BF16)
    out_specs = [row_spec, row_spec, row_spec]
    out_shape = [row_shape, row_shape, row_shape]
    if rope:
        assert seq_len == TOK_PER_TILE
        for tab in rope_tabs:
            in_specs.append(pl.BlockSpec((SUB_TOK, HEAD_W), lambda i, j: (j, 0)))
            args.append(tab)
    else:
        out_specs += [pl.BlockSpec((seqs_per_sub, ATT_WIDTH, seq_len),
                                   lambda i, j: (i * N_SUB + j, 0, 0)),
                      pl.BlockSpec((seqs_per_sub, seq_len, N_HEADS, HEAD_W),
                                   lambda i, j: (i * N_SUB + j, 0, 0, 0))]
        out_shape += [jax.ShapeDtypeStruct((n_batch, ATT_WIDTH, seq_len), F32),
                      jax.ShapeDtypeStruct((n_batch, seq_len, N_HEADS, HEAD_W), F32)]
    out_specs.append(pl.BlockSpec((N_GROUPS, CW, ROWS_PER_TILE), lambda i, j: (0, 0, i)))
    out_shape.append(jax.ShapeDtypeStruct((N_GROUPS, CW, n_rows), BF16))
    return pl.pallas_call(
        functools.partial(_in_proj_kernel, rope=rope, seqs_per_sub=seqs_per_sub),
        grid=(n_tiles, N_SUB),
        in_specs=in_specs,
        out_specs=out_specs,
        out_shape=out_shape,
        scratch_shapes=[pltpu.VMEM((SSM_WIDTH, D_MODEL), BF16),
                        pltpu.VMEM((CHUNK, ROWS_PER_TILE, D_MODEL), F32),
                        pltpu.SemaphoreType.DMA((CHUNK,)),
                        pltpu.VMEM((2, SUB_TOK, D_MODEL), BF16)],
        compiler_params=_cparams(2),
        name="in_proj",
    )(*args)


def _attn_kernel(*refs, has_ctx, lam_init, n_seq, seq_len, tq, n_cast):
    if n_cast:
        cast_in = refs[len(refs) - 2 * n_cast - 1:len(refs) - n_cast - 1]
        cast_out = refs[len(refs) - n_cast:]
        refs = refs[:len(refs) - 2 * n_cast - 1] + (refs[len(refs) - n_cast - 1],)
        for src, dst in zip(cast_in, cast_out):
            dst[...] = src[...].astype(dst.dtype)
    if has_ctx:
        lam_ref, sg_ref, q_ref, ck_ref, cv_ref, k_ref, v_ref, o_ref = refs
    else:
        lam_ref, sg_ref, q_ref, k_ref, v_ref, o_ref = refs
    lp = lam_ref[...]
    lam = (jnp.exp(jnp.sum(lp[0:1] * lp[1:2], axis=-1, keepdims=True))
           - jnp.exp(jnp.sum(lp[2:3] * lp[3:4], axis=-1, keepdims=True)) + lam_init)
    first_map = lax.broadcasted_iota(jnp.int32, (1, HEAD_W), 1) < HEAD_DIM
    ti = min(TQ_ITEM, tq)
    for row0 in range(0, n_seq * tq, ti):
        b = row0 // tq
        q_rows = slice(row0, row0 + ti)
        for hd in range(N_HEADS):
            sl = slice(hd * HEAD_W, (hd + 1) * HEAD_W)
            qh = q_ref[q_rows, sl]
            zero = jnp.zeros_like(qh)
            qs = jnp.concatenate([jnp.where(first_map, qh, zero),
                                  jnp.where(first_map, zero, qh)], axis=0)
            kv_rows = slice(b * seq_len, (b + 1) * seq_len)
            parts = [(k_ref[kv_rows, sl], v_ref[kv_rows, sl])]
            if has_ctx:
                parts.insert(0, (ck_ref[:, sl].astype(BF16), cv_ref[:, sl].astype(BF16)))
            scores = [lax.dot_general(qs, kk, NT_DIMS, preferred_element_type=F32)
                      for kk, _ in parts]
            mx = scores[0].max(axis=-1, keepdims=True)
            for s in scores[1:]:
                mx = jnp.maximum(mx, s.max(axis=-1, keepdims=True))
            acc = None
            for s, (_, vv) in zip(scores, parts):
                e = jnp.exp2(s - mx).astype(BF16)
                v_one = jnp.concatenate([vv, jnp.ones_like(vv)], axis=1)
                pv = jnp.dot(e, v_one, preferred_element_type=F32)
                acc = pv if acc is None else acc + pv
            num = acc[:, 0:HEAD_W] / acc[:, HEAD_W:2 * HEAD_W]
            o = num[0:ti] - lam * num[ti:2 * ti]
            o = _rms(o, sg_ref[...]) * (1.0 - lam_init)
            o_ref[q_rows, sl] = o.astype(o_ref.dtype)


def _attention(q, k, v, ctx_k, ctx_v, lam_params, subln_g, n_batch, seq_len, lam_init,
               cast_weights=()):
    has_ctx = ctx_k is not None
    tq = min(1024, seq_len)
    n_q = seq_len // tq
    n_seq = 1 if n_q > 1 else min(4, n_batch)
    in_specs = [pl.BlockSpec((4, HEAD_DIM), lambda b, i: (0, 0)),
                pl.BlockSpec((1, HEAD_W), lambda b, i: (0, 0)),
                pl.BlockSpec((n_seq * tq, ATT_WIDTH), lambda b, i: (b * n_q + i, 0))]
    args = [lam_params, subln_g.reshape(1, HEAD_W), q]
    if has_ctx:
        past = ctx_k.shape[1]
        ctx_spec = pl.BlockSpec((None, past, ATT_WIDTH), lambda b, i: (b, 0, 0))
        in_specs += [ctx_spec, ctx_spec]
        args += [ctx_k, ctx_v]
    kv_spec = pl.BlockSpec((n_seq * seq_len, ATT_WIDTH), lambda b, i: (b, 0))
    in_specs += [kv_spec, kv_spec]
    args += [k, v]
    out_specs = [pl.BlockSpec((n_seq * tq, ATT_WIDTH), lambda b, i: (b * n_q + i, 0))]
    out_shape = [jax.ShapeDtypeStruct((n_batch * seq_len, ATT_WIDTH), BF16)]
    n_steps = (n_batch // n_seq) * n_q
    for w in cast_weights:
        rows = w.shape[0] // n_steps
        spec = pl.BlockSpec((rows, w.shape[1]), lambda b, i: (b * n_q + i, 0))
        in_specs.append(spec)
        args.append(w)
        out_specs.append(spec)
        out_shape.append(jax.ShapeDtypeStruct(w.shape, BF16))
    outs = pl.pallas_call(
        functools.partial(_attn_kernel, has_ctx=has_ctx, lam_init=lam_init,
                          n_seq=n_seq, seq_len=seq_len, tq=tq, n_cast=len(cast_weights)),
        grid=(n_batch // n_seq, n_q),
        in_specs=in_specs,
        out_specs=out_specs,
        out_shape=out_shape,
        compiler_params=_cparams(2),
        name="diff_attention",
    )(*args)
    return outs[0], tuple(outs[1:])


def _cmul(ar, ai, br, bi):
    return ar * br - ai * bi, ar * bi + ai * br


def _ssm_prep_kernel(lre_ref, lim_ref, ls_ref, bre_ref, bim_ref, cre_ref, cim_ref, d_ref,
                     mt_ref, gt_ref, wo_ref, at_ref):
    lane = lax.broadcasted_iota(jnp.int32, (SSM_GROUP, CW), 1)
    chan = lax.broadcasted_iota(jnp.int32, (SSM_GROUP, CW), 0)
    for gi in range(GROUP_BLOCK):
        gt_cols, wo_cols, at_cols, toeplitz = [], [], [], []
        for dr in range(2):
            lr = jnp.minimum(lre_ref[dr, gi], -1e-4)
            li = lim_ref[dr, gi]
            step = jnp.exp(ls_ref[dr, gi])
            mag = jnp.exp(lr * step)
            a_re = mag * jnp.cos(li * step)
            a_im = mag * jnp.sin(li * step)
            den = lr * lr + li * li
            nr = a_re - 1.0
            f_re = (nr * lr + a_im * li) / den
            f_im = (a_im * lr - nr * li) / den
            bt_re, bt_im = bre_ref[dr, gi], bim_ref[dr, gi]
            bb_re, bb_im = _cmul(f_re, f_im, bt_re, bt_im)
            c_re, c_im = cre_ref[dr, gi], cim_ref[dr, gi]
            pw = [(jnp.ones_like(a_re), jnp.zeros_like(a_im))]
            for _ in range(CHUNK):
                pw.append(_cmul(pw[-1][0], pw[-1][1], a_re, a_im))
            g_re, g_im, e_re, e_im = [], [], [], []
            for t in range(CHUNK):
                pr, pi = pw[CHUNK - 1 - t] if dr == 0 else pw[t]
                r, i = _cmul(bb_re, bb_im, pr, pi)
                g_re.append(r)
                g_im.append(i)
                pr, pi = pw[t + 1] if dr == 0 else pw[CHUNK - t]
                r, i = _cmul(c_re, c_im, pr, pi)
                e_re.append(r)
                e_im.append(-i)
            g_cat = jnp.concatenate([jnp.concatenate(g_re, axis=0),
                                     jnp.concatenate(g_im, axis=0)], axis=1)
            gt_cols.append(g_cat)
            wo_cols.append(jnp.concatenate([jnp.concatenate(e_re, axis=0),
                                            jnp.concatenate(e_im, axis=0)], axis=1))
            c_cat = jnp.concatenate([c_re, -c_im], axis=1)
            toeplitz.append(lax.dot_general(c_cat, g_cat, NT_DIMS,
                                            precision=lax.Precision.HIGHEST,
                                            preferred_element_type=F32))
            apw = [pw[CHUNK]]
            for _ in range(SCAN_BLOCK - 1):
                apw.append(_cmul(apw[-1][0], apw[-1][1], apw[0][0], apw[0][1]))
            order = list(range(SCAN_BLOCK)) if dr == 0 else list(range(SCAN_BLOCK - 1, -1, -1))
            order += [2 ** l - 1 for l in range(SCAN_LEVELS)]
            order += [0] * (AT_ROWS - len(order))
            at_cols += [jnp.concatenate([jnp.concatenate([apw[i][0], apw[i][0]], axis=1)
                                         for i in order], axis=0),
                        jnp.concatenate([jnp.concatenate([-apw[i][1], apw[i][1]], axis=1)
                                         for i in order], axis=0)]
        kf_rev, kb = toeplitz
        d_skip = d_ref[gi]
        blocks = []
        for t in range(CHUNK):
            fwd = pltpu.roll(kf_rev, (CW - (CHUNK - 1 - t) * SSM_GROUP) % CW, axis=1)
            bwd = pltpu.roll(kb, t * SSM_GROUP, axis=1)
            blocks.append(jnp.where(lane < (t + 1) * SSM_GROUP, fwd, 0.0)
                          + jnp.where(lane >= t * SSM_GROUP, bwd, 0.0)
                          + jnp.where(lane == chan + t * SSM_GROUP, d_skip, 0.0))
        mt_ref[gi] = jnp.concatenate(blocks, axis=0).astype(mt_ref.dtype)
        gt_ref[gi] = jnp.concatenate(gt_cols, axis=1).astype(gt_ref.dtype)
        wo_ref[gi] = jnp.concatenate(wo_cols, axis=1).astype(wo_ref.dtype)
        at_ref[gi] = jnp.concatenate(at_cols, axis=1)


def _ssm_prep(lam_re, lam_im, log_step, b_re, b_im, c_re, c_im, d_skip):
    row = lambda a: a.reshape(2, N_GROUPS, 1, SSM_STATE)
    bt = lambda a: jnp.swapaxes(a, 2, 3)
    d_row = jnp.tile((d_skip[0] + d_skip[1]).reshape(N_GROUPS, 1, SSM_GROUP), (1, 1, CHUNK))
    gb = GROUP_BLOCK
    vec_spec = pl.BlockSpec((2, gb, 1, SSM_STATE), lambda i: (0, i, 0, 0))
    mat_spec = pl.BlockSpec((2, gb, SSM_GROUP, SSM_STATE), lambda i: (0, i, 0, 0))
    w_spec = pl.BlockSpec((gb, CW, CW), lambda i: (i, 0, 0))
    w_shape = jax.ShapeDtypeStruct((N_GROUPS, CW, CW), BF16)
    return pl.pallas_call(
        _ssm_prep_kernel,
        grid=(N_GROUPS // gb,),
        in_specs=[vec_spec, vec_spec,
                  pl.BlockSpec((2, gb, 1, 1), lambda i: (0, i, 0, 0)),
                  mat_spec, mat_spec, mat_spec, mat_spec,
                  pl.BlockSpec((gb, 1, CW), lambda i: (i, 0, 0))],
        out_specs=[w_spec, w_spec, w_spec,
                   pl.BlockSpec((gb, AT_ROWS, 4 * 2 * SSM_STATE), lambda i: (i, 0, 0))],
        out_shape=[w_shape, w_shape, w_shape,
                   jax.ShapeDtypeStruct((N_GROUPS, AT_ROWS, 4 * 2 * SSM_STATE), F32)],
        compiler_params=_cparams(1),
        name="ssm_prep",
    )(row(lam_re), row(lam_im), log_step.reshape(2, N_GROUPS, 1, 1),
      bt(b_re), bt(b_im), c_re, c_im, d_row)


def _shift_rows(x, m, down):
    n = x.shape[0]
    return pltpu.roll(x, m if down else n - m, axis=0)


def _ssm_kernel(*refs, n_seq, has_h0):
    if has_h0:
        (xt_ref, mt_ref, gt_ref, wo_ref, at_ref, wg_ref, bg_ref, h0_ref,
         z_ref, zs_ref) = refs
    else:
        (xt_ref, mt_ref, gt_ref, wo_ref, at_ref, wg_ref, bg_ref,
         z_ref, st_ref, zs_ref, fin_ref) = refs
    step = pl.program_id(0)
    n_rows = xt_ref.shape[-1]
    seg = n_rows // n_seq
    cw2 = 2 * SSM_STATE
    assert seg % SCAN_BLOCK == 0
    pos = lax.broadcasted_iota(jnp.int32, (n_rows, cw2), 0) % seg
    row_blk = lax.broadcasted_iota(jnp.int32, (SCAN_BLOCK, cw2), 0)
    for gi in range(GROUP_BLOCK):
        xt = xt_ref[gi]
        at = at_ref[gi]
        s_all = lax.dot_general(xt, gt_ref[gi], TN_DIMS, preferred_element_type=F32)
        h_parts = []
        for dr in range(2):
            s = s_all[:, dr * cw2:(dr + 1) * cw2]
            p_tab = at[:, (2 * dr) * cw2:(2 * dr + 1) * cw2]
            q_tab = at[:, (2 * dr + 1) * cw2:(2 * dr + 2) * cw2]
            down = dr == 0
            if has_h0:
                h0 = jnp.zeros((n_rows, cw2), F32)
                riota = lax.broadcasted_iota(jnp.int32, (n_rows, cw2), 0)
                for b in range(n_seq):
                    edge = b * seg if down else b * seg + seg - 1
                    h0 = jnp.where(riota == edge, h0_ref[gi, b:b + 1, dr * cw2:(dr + 1) * cw2], h0)
                lv0 = SCAN_BLOCK
                s = (s + p_tab[lv0:lv0 + 1] * h0
                     + q_tab[lv0:lv0 + 1] * pltpu.roll(h0, SSM_STATE, axis=1))
            n_blk = n_rows // SCAN_BLOCK
            s3 = s.reshape(n_blk, SCAN_BLOCK, cw2)
            for lvl in range(SCAN_LEVELS):
                m = 2 ** lvl
                valid = (row_blk >= m) if down else (row_blk < SCAN_BLOCK - m)
                row = SCAN_BLOCK + lvl
                p = jnp.where(valid, p_tab[row:row + 1], 0.0)
                q = jnp.where(valid, q_tab[row:row + 1], 0.0)
                sh = pltpu.roll(s3, m if down else SCAN_BLOCK - m, axis=1)
                s3 = s3 + p * sh + q * pltpu.roll(sh, SSM_STATE, axis=2)
            sw3 = pltpu.roll(s3, SSM_STATE, axis=2)
            p_blk, q_blk = p_tab[0:SCAN_BLOCK], q_tab[0:SCAN_BLOCK]
            blocks = [s3[i] for i in range(n_blk)]
            blocks_w = [sw3[i] for i in range(n_blk)]
            blk_per_seq = seg // SCAN_BLOCK
            edge = slice(SCAN_BLOCK - 1, SCAN_BLOCK) if down else slice(0, 1)
            for q_i in range(n_seq):
                idxs = list(range(q_i * blk_per_seq, (q_i + 1) * blk_per_seq))
                idxs = idxs if down else idxs[::-1]
                for prev, cur in zip(idxs[:-1], idxs[1:]):
                    c = jnp.broadcast_to(blocks[prev][edge], (SCAN_BLOCK, cw2))
                    cw = jnp.broadcast_to(blocks_w[prev][edge], (SCAN_BLOCK, cw2))
                    blocks[cur] = blocks[cur] + p_blk * c + q_blk * cw
                    blocks_w[cur] = blocks_w[cur] + p_blk * cw - q_blk * c
            s = jnp.concatenate(blocks, axis=0)
            if not has_h0:
                fin_ref[dr] = s
                st_ref[gi, :, dr * cw2:(dr + 1) * cw2] = (
                    fin_ref[dr, pl.ds(seg - 1 if down else 0, n_seq, stride=seg), :])
            ent = _shift_rows(s, 1, down)
            ent = jnp.where((pos >= 1) if down else (pos < seg - 1), ent, 0.0)
            if has_h0:
                ent = ent + h0
            h_parts.append(ent)
        h_all = jnp.concatenate(h_parts, axis=1).astype(BF16)
        yt = (jnp.dot(mt_ref[gi], xt, preferred_element_type=F32)
              + lax.dot_general(wo_ref[gi], h_all, NT_DIMS, preferred_element_type=F32))
        z = jax.nn.gelu(yt, approximate=True)
        grp = step * GROUP_BLOCK + gi
        for t in range(CHUNK):
            zs_ref[t, pl.ds(pl.multiple_of(grp * SSM_GROUP, SSM_GROUP), SSM_GROUP), :] = (
                z[t * SSM_GROUP:(t + 1) * SSM_GROUP, :])

    @pl.when(step == pl.num_programs(0) - 1)
    def _():
        for t in range(CHUNK):
            zt = zs_ref[t]
            gate = jnp.dot(wg_ref[...], zt.astype(BF16),
                           preferred_element_type=F32) + bg_ref[...]
            z_ref[t] = (zt * jax.nn.sigmoid(gate)).astype(z_ref.dtype)


def _ssm(xt, mt, gt, wo, at, w_glu_t, b_glu_col, h0, n_seq):
    n_rows = xt.shape[-1]
    has_h0 = h0 is not None
    gb = GROUP_BLOCK
    w_spec = pl.BlockSpec((gb, CW, CW), lambda i: (i, 0, 0))
    in_specs = [pl.BlockSpec((gb, CW, n_rows), lambda i: (i, 0, 0)),
                w_spec, w_spec, w_spec,
                pl.BlockSpec((gb, AT_ROWS, 4 * 2 * SSM_STATE), lambda i: (i, 0, 0)),
                pl.BlockSpec((SSM_WIDTH, SSM_WIDTH), lambda i: (0, 0)),
                pl.BlockSpec((SSM_WIDTH, 1), lambda i: (0, 0))]
    args = [xt, mt, gt, wo, at, w_glu_t, b_glu_col]
    out_specs = [pl.BlockSpec((CHUNK, SSM_WIDTH, n_rows), lambda i: (0, 0, 0))]
    out_shape = [jax.ShapeDtypeStruct((CHUNK, SSM_WIDTH, n_rows), BF16)]
    scratch = [pltpu.VMEM((CHUNK, SSM_WIDTH, n_rows), F32)]
    if has_h0:
        in_specs.append(pl.BlockSpec((gb, n_seq, CW), lambda i: (i, 0, 0)))
        args.append(h0)
    else:
        out_specs.append(pl.BlockSpec((gb, n_seq, CW), lambda i: (i, 0, 0)))
        out_shape.append(jax.ShapeDtypeStruct((N_GROUPS, n_seq, CW), F32))
        scratch.append(pltpu.VMEM((2, n_rows, 2 * SSM_STATE), F32))
    return pl.pallas_call(
        functools.partial(_ssm_kernel, n_seq=n_seq, has_h0=has_h0),
        grid=(N_GROUPS // gb,),
        in_specs=in_specs,
        out_specs=out_specs,
        out_shape=out_shape,
        scratch_shapes=scratch,
        compiler_params=_cparams(1),
        name="ssm_scan_glu",
    )(*args)


FF_CHUNK = 256


def _out_ffn_kernel(x_ref, attn_ref, zt_ref, mod_ref, g_ref, wo_ref, wfi_ref, wfo_ref,
                    o_ref, mixz_ref, act_ref):
    j = pl.program_id(1)

    @pl.when(j == 0)
    def _():
        for t in range(CHUNK):
            part = lax.dot_general(zt_ref[t], wo_ref[ATT_WIDTH:, :], TN_DIMS,
                                   preferred_element_type=F32)
            for cb in range(N_COL_BLOCKS):
                mixz_ref[cb, pl.ds(t, ROWS_PER_TILE, stride=CHUNK), :] = (
                    part[:, cb * LANES:(cb + 1) * LANES])

    gate1 = mod_ref[:, 2 * D_MODEL:3 * D_MODEL]
    shift2 = mod_ref[:, 3 * D_MODEL:4 * D_MODEL]
    scale2 = mod_ref[:, 4 * D_MODEL:5 * D_MODEL]
    gate2 = mod_ref[:, 5 * D_MODEL:6 * D_MODEL]
    half = SUB_TOK // 2
    gain2 = g_ref[2:3, :] * (1.0 + scale2)

    def pre_ffn(hf):
        r = slice(hf * half, (hf + 1) * half)
        rows = pl.ds(pl.multiple_of(j * SUB_TOK + hf * half, half), half)
        mix = (jnp.dot(attn_ref[r, :], wo_ref[0:ATT_WIDTH, :], preferred_element_type=F32)
               + jnp.concatenate([mixz_ref[cb, rows, :] for cb in range(N_COL_BLOCKS)], axis=1))
        x1 = x_ref[r, :] + gate1 * _rms(mix, g_ref[1:2, :])
        ms = jnp.mean(x1 * x1, axis=-1, keepdims=True)
        return x1, (x1 * lax.rsqrt(ms + NORM_EPS) * gain2 + shift2).astype(BF16)

    def ffn_in(hf, h, chunks):
        r = slice(hf * half, (hf + 1) * half)
        for c in chunks:
            lo = c * FF_CHUNK
            gt = jnp.dot(h, wfi_ref[:, lo:lo + FF_CHUNK], preferred_element_type=F32)
            up = jnp.dot(h, wfi_ref[:, D_FF + lo:D_FF + lo + FF_CHUNK],
                         preferred_element_type=F32)
            act_ref[r, lo:lo + FF_CHUNK] = (_silu(gt) * up).astype(BF16)

    def ffn_out(hf, x1):
        r = slice(hf * half, (hf + 1) * half)
        f = jnp.dot(act_ref[r, :], wfo_ref[...], preferred_element_type=F32)
        o_ref[r, :] = x1 + gate2 * _rms(f, g_ref[3:4, :])

    n_chunks = D_FF // FF_CHUNK
    x1_a, h_a = pre_ffn(0)
    x1_b, h_b = pre_ffn(1)
    ffn_in(0, h_a, range(n_chunks))
    ffn_out(0, x1_a)
    ffn_in(1, h_b, range(n_chunks))
    ffn_out(1, x1_b)


def _out_ffn(x2d, attn, zt, mods, mod_rows, norm_g, w_o, w_ffn_in, w_ffn_out):
    n_tok = x2d.shape[0]
    n_tiles = n_tok // TOK_PER_TILE
    mod_row0, n_mod = mod_rows
    tiles_per_mod = n_tiles // n_mod
    const = lambda i, j: (0, 0)
    row_spec = lambda w: pl.BlockSpec((SUB_TOK, w), lambda i, j: (i * N_SUB + j, 0))
    return pl.pallas_call(
        _out_ffn_kernel,
        grid=(n_tiles, N_SUB),
        in_specs=[row_spec(D_MODEL), row_spec(ATT_WIDTH),
                  pl.BlockSpec((CHUNK, SSM_WIDTH, ROWS_PER_TILE), lambda i, j: (0, 0, i)),
                  pl.BlockSpec((None, 1, N_MOD * D_MODEL),
                               lambda i, j: (mod_row0 + i // tiles_per_mod, 0, 0)),
                  pl.BlockSpec((4, D_MODEL), const),
                  pl.BlockSpec((2 * ATT_WIDTH, D_MODEL), const, pipeline_mode=pl.Buffered(1)),
                  pl.BlockSpec((D_MODEL, 2 * D_FF), const, pipeline_mode=pl.Buffered(1)),
                  pl.BlockSpec((D_FF, D_MODEL), const, pipeline_mode=pl.Buffered(1))],
        out_specs=row_spec(D_MODEL),
        out_shape=jax.ShapeDtypeStruct((n_tok, D_MODEL), F32),
        scratch_shapes=[pltpu.VMEM((N_COL_BLOCKS, TOK_PER_TILE, LANES), F32),
                        pltpu.VMEM((SUB_TOK, D_FF), BF16)],
        compiler_params=_cparams(2),
        name="out_proj_ffn",
    )(x2d, attn, zt, mods, norm_g, w_o, w_ffn_in, w_ffn_out)


def _rope_tables(seq_len):
    t = np.arange(seq_len)
    row = (t // GRID_W).astype(np.float32)
    col = (t % GRID_W).astype(np.float32)
    half = HEAD_DIM // 2
    inv_freq = (np.float32(ROPE_BASE)
                ** (-np.arange(0, half, 2, dtype=np.float32) / np.float32(half))).astype(np.float32)
    ang_r = row[:, None] * inv_freq
    ang_c = col[:, None] * inv_freq
    ang = np.concatenate([ang_r, ang_r, ang_c, ang_c], axis=-1)
    cos, sin = np.cos(ang), np.sin(ang)
    upper = (np.arange(HEAD_DIM) % 32) < 16
    sa = np.where(upper, -sin, 0.0)
    sb = np.where(upper, 0.0, sin)
    two = lambda a: jnp.asarray(np.concatenate([a, a], axis=-1), dtype=F32)
    return two(cos), two(sa), two(sb)


def _layer(x, mods, mod_rows, lam_init, rope_tabs, ctx_k, ctx_v, h0, weights, prep):
    n_batch, seq_len = x.shape[:2]
    g = weights['norm_g']
    outs = _in_proj(x, mods, mod_rows, g[0:1], weights['w_in'], rope_tabs)
    q, k, v = outs[:3]
    pending = () if 'late_bf16' in weights else weights['late_f32']
    attn, cast = _attention(q, k, v, ctx_k, ctx_v, weights['lam'], weights['subln_g'],
                            n_batch, seq_len, lam_init, cast_weights=pending)
    if pending:
        weights['late_bf16'] = cast
    ssm_out = _ssm(outs[-1], *prep, weights['w_glu_t'], weights['b_glu_col'], h0, n_batch)
    y = _out_ffn(x.reshape(n_batch * seq_len, D_MODEL), attn, ssm_out[0], mods, mod_rows, g,
                 *weights['late_bf16'])
    return y.reshape(x.shape), outs[3:-1], ssm_out[1:]


def kernel(x_prompt, x_sample, cache_k, cache_v, state_ssm_re, state_ssm_im, c, c_ctx, w_mod, b_mod, norm_g, w_in, lam_params, subln_g, ssm_lambda_re, ssm_lambda_im, ssm_log_step, ssm_b_re, ssm_b_im, ssm_c_re, ssm_c_im, ssm_d, w_glu, b_glu, w_o, w_ffn_in, w_ffn_out):
    depth = w_mod.shape[0]
    assert depth == 1
    bp = x_prompt.shape[0]
    bd, ld_len = x_sample.shape[:2]
    past = cache_k.shape[2]
    xp, xs = x_prompt, x_sample
    rope_tabs = _rope_tables(ld_len)
    ks_out, vs_out, hr_out, hi_out = [], [], [], []
    for l in range(depth):
        lam_init = 0.8 - 0.6 * math.exp(-0.3 * l)
        mods = _modulation(c_ctx, c, w_mod[l], b_mod[l])
        weights = {
            'norm_g': norm_g[l],
            'w_in': w_in[l].astype(BF16),
            'lam': lam_params[l], 'subln_g': subln_g[l],
            'w_glu_t': w_glu[l].T.astype(BF16), 'b_glu_col': b_glu[l].reshape(SSM_WIDTH, 1),
            'late_f32': (w_o[l], w_ffn_in[l], w_ffn_out[l]),
        }
        prep = _ssm_prep(ssm_lambda_re[l], ssm_lambda_im[l], ssm_log_step[l],
                         ssm_b_re[l], ssm_b_im[l], ssm_c_re[l], ssm_c_im[l], ssm_d[l])
        ck = cache_k[:, l].reshape(bd, past, ATT_WIDTH)
        cv = cache_v[:, l].reshape(bd, past, ATT_WIDTH)
        h0 = jnp.stack([state_ssm_re[:, l], state_ssm_im[:, l]], axis=2)
        h0 = h0.transpose(3, 0, 1, 2, 4).reshape(N_GROUPS, bd, CW)
        xs, _, _ = _layer(xs, mods, (1, bd), lam_init, rope_tabs, ck, cv, h0, weights, prep)
        xp, (k_ctx, v_ctx), (st,) = _layer(xp, mods, (0, 1), lam_init, None, None, None, None,
                                           weights, prep)
        ks_out.append(jnp.swapaxes(k_ctx, 1, 2).reshape(bp, -1, 2 * N_HEADS, HEAD_DIM))
        vs_out.append(v_ctx)
        fin = st.reshape(N_GROUPS, bp, 2, 2, SSM_STATE).transpose(1, 2, 3, 0, 4)
        hr_out.append(fin[:, :, 0])
        hi_out.append(fin[:, :, 1])
    return (xp, xs, jnp.stack(ks_out, axis=1), jnp.stack(vs_out, axis=1),
            jnp.stack(hr_out, axis=1), jnp.stack(hi_out, axis=1))
```

```python
import functools
import math

import jax
import jax.numpy as jnp
import numpy as np
from jax import lax
from jax.experimental import pallas as pl
from jax.experimental.pallas import tpu as pltpu

F32 = jnp.float32
BF16 = jnp.bfloat16

D_MODEL = 1024
GRID_W = 64
ATT_WIDTH = 512
SSM_WIDTH = 512
HEAD_DIM = 64
N_HEADS = 4
HEAD_W = 2 * HEAD_DIM
SSM_GROUP = 16
N_GROUPS = 32
SSM_STATE = 64
D_FF = 2816
N_MOD = 6
ROPE_BASE = 10000.0
NORM_EPS = 1e-6

LANES = 128
N_COL_BLOCKS = D_MODEL // LANES
CHUNK = 16
CW = CHUNK * SSM_GROUP
SCAN_BLOCK = 8
SCAN_LEVELS = 3
AT_ROWS = 16
GROUP_BLOCK = 4

ROWS_PER_TILE = 128
TOK_PER_TILE = ROWS_PER_TILE * CHUNK
SUB_TOK = 512
TQ_ITEM = 256
N_SUB = TOK_PER_TILE // SUB_TOK

VMEM_LIMIT = 56 * 1024 * 1024

NT_DIMS = (((1,), (1,)), ((), ()))
TN_DIMS = (((0,), (0,)), ((), ()))


def _cparams(n_axes):
    return pltpu.CompilerParams(
        dimension_semantics=("arbitrary",) * n_axes,
        vmem_limit_bytes=VMEM_LIMIT)


def _rms(x, g):
    ms = jnp.mean(x * x, axis=-1, keepdims=True)
    return x * lax.rsqrt(ms + NORM_EPS) * g


def _silu(x):
    return x * jax.nn.sigmoid(x)


MOD_ROWS = 8


def _mod_kernel(ctx_ref, c_ref, w_ref, b_ref, o_ref):
    n_lat, tk = c_ref.shape
    row = lax.broadcasted_iota(jnp.int32, (MOD_ROWS, tk), 0)
    cond = jnp.where(row == 0, ctx_ref[...], 0.0)
    for b in range(n_lat):
        cond = jnp.where(row == 1 + b, c_ref[b:b + 1, :], cond)
    part = jnp.dot(_silu(cond).astype(BF16), w_ref[...].astype(BF16),
                   preferred_element_type=F32)

    @pl.when(pl.program_id(0) == 0)
    def _():
        o_ref[:, 0, :] = part + b_ref[...]

    @pl.when(pl.program_id(0) > 0)
    def _():
        o_ref[:, 0, :] += part


def _modulation(c_ctx, c, w_mod, b_mod):
    n = w_mod.shape[1]
    tk = 256
    assert 1 + c.shape[0] <= MOD_ROWS
    return pl.pallas_call(
        _mod_kernel,
        grid=(D_MODEL // tk,),
        in_specs=[pl.BlockSpec((1, tk), lambda k: (0, k)),
                  pl.BlockSpec((c.shape[0], tk), lambda k: (0, k)),
                  pl.BlockSpec((tk, n), lambda k: (k, 0)),
                  pl.BlockSpec((1, n), lambda k: (0, 0))],
        out_specs=pl.BlockSpec((MOD_ROWS, 1, n), lambda k: (0, 0, 0)),
        out_shape=jax.ShapeDtypeStruct((MOD_ROWS, 1, n), F32),
        compiler_params=_cparams(1),
        name="modulation",
    )(c_ctx.reshape(1, D_MODEL), c, w_mod, b_mod.reshape(1, n))


def _rope(x, cos, sa, sb):
    return (x * cos + pltpu.roll(x, HEAD_W - 16, axis=1) * sa
            + pltpu.roll(x, 16, axis=1) * sb)


def _in_proj_kernel(*refs, rope, seqs_per_sub):
    x_ref, x3_hbm, mod_ref, g_ref, w_ref = refs[:5]
    refs = refs[5:]
    if rope:
        cos_ref, sa_ref, sb_ref = refs[:3]
        refs = refs[3:]
        q_ref, k_ref, v_ref, ut_ref, wut_ref, xt_ref, xt_sem = refs
    else:
        q_ref, k_ref, v_ref, kc_ref, vc_ref, ut_ref, wut_ref, xt_ref, xt_sem = refs
    tile = pl.program_id(0)
    j = pl.program_id(1)
    t_per_sub = CHUNK // N_SUB
    t_early = CHUNK - t_per_sub

    def gather(tile_idx, t):
        src = x3_hbm.at[pl.ds(tile_idx * ROWS_PER_TILE, ROWS_PER_TILE), t, :]
        return pltpu.make_async_copy(src, xt_ref.at[t], xt_sem.at[t])

    @pl.when(j == 0)
    def _():
        @pl.when(tile == 0)
        def _():
            for t in range(t_early):
                gather(0, t).start()
        for t in range(t_early, CHUNK):
            gather(tile, t).start()

    t_base = j * t_per_sub
    for d in range(t_per_sub):
        gather(tile, t_base + d).wait()

    @pl.when((j == N_SUB - 1) & (tile + 1 < pl.num_programs(0)))
    def _():
        for t in range(t_early):
            gather(tile + 1, t).start()

    @pl.when((tile == 0) & (j == 0))
    def _():
        wut_ref[...] = w_ref[:, 3 * ATT_WIDTH:].T

    shift = mod_ref[:, 0:D_MODEL]
    gain = g_ref[...] * (1.0 + mod_ref[:, D_MODEL:2 * D_MODEL])

    def norm_mod(xv):
        ms = jnp.mean(xv * xv, axis=-1, keepdims=True)
        return (xv * lax.rsqrt(ms + NORM_EPS) * gain + shift).astype(BF16)

    proj = jnp.dot(norm_mod(x_ref[...]), w_ref[:, 0:3 * ATT_WIDTH],
                   preferred_element_type=F32)
    q = proj[:, 0:ATT_WIDTH]
    k = proj[:, ATT_WIDTH:2 * ATT_WIDTH]
    v = proj[:, 2 * ATT_WIDTH:3 * ATT_WIDTH]
    qscale = HEAD_DIM ** -0.5 * math.log2(math.e)
    if rope:
        cos, sa, sb = cos_ref[...], sa_ref[...], sb_ref[...]
        for hd in range(N_HEADS):
            sl = slice(hd * HEAD_W, (hd + 1) * HEAD_W)
            q_ref[:, sl] = (_rope(q[:, sl], cos, sa, sb) * qscale).astype(q_ref.dtype)
            k_ref[:, sl] = _rope(k[:, sl], cos, sa, sb).astype(k_ref.dtype)
    else:
        q_ref[...] = (q * qscale).astype(q_ref.dtype)
        k_ref[...] = k.astype(k_ref.dtype)
        seq = SUB_TOK // seqs_per_sub
        k_t = k.T
        for b in range(seqs_per_sub):
            kc_ref[b] = k_t[:, b * seq:(b + 1) * seq]
            for hd in range(N_HEADS):
                vc_ref[b, :, hd, :] = v[b * seq:(b + 1) * seq, hd * HEAD_W:(hd + 1) * HEAD_W]
    v_ref[...] = v.astype(v_ref.dtype)

    for d0 in range(0, t_per_sub, 2):
        xt = jnp.concatenate([xt_ref[t_base + d0], xt_ref[t_base + d0 + 1]], axis=0)
        ut = lax.dot_general(wut_ref[...], norm_mod(xt), NT_DIMS,
                             preferred_element_type=F32)
        for d in range(2):
            blk = ut[:, d * ROWS_PER_TILE:(d + 1) * ROWS_PER_TILE]
            row0 = pl.multiple_of((t_base + d0 + d) * SSM_GROUP, SSM_GROUP)
            ut_ref[:, pl.ds(row0, SSM_GROUP), :] = (
                blk.reshape(N_GROUPS, SSM_GROUP, ROWS_PER_TILE).astype(ut_ref.dtype))


def _in_proj(x, mods, mod_rows, g0, w_in, rope_tabs):
    n_batch, seq_len = x.shape[:2]
    n_tok = n_batch * seq_len
    n_rows = n_tok // CHUNK
    n_tiles = n_tok // TOK_PER_TILE
    mod_row0, n_mod = mod_rows
    tiles_per_mod = n_tiles // n_mod
    rope = rope_tabs is not None
    seqs_per_sub = max(1, SUB_TOK // seq_len)
    in_specs = [pl.BlockSpec((SUB_TOK, D_MODEL), lambda i, j: (i * N_SUB + j, 0)),
                pl.BlockSpec(memory_space=pl.ANY),
                pl.BlockSpec((None, 1, 2 * D_MODEL),
                             lambda i, j: (mod_row0 + i // tiles_per_mod, 0, 0)),
                pl.BlockSpec((1, D_MODEL), lambda i, j: (0, 0)),
                pl.BlockSpec((D_MODEL, 4 * ATT_WIDTH), lambda i, j: (0, 0))]
    args = [x.reshape(n_tok, D_MODEL), x.reshape(n_rows, CHUNK, D_MODEL), mods, g0, w_in]
    row_spec = pl.BlockSpec((SUB_TOK, ATT_WIDTH), lambda i, j: (i * N_SUB + j, 0))
    row_shape = jax.ShapeDtypeStruct((n_tok, ATT_WIDTH), BF16)
    out_specs = [row_spec, row_spec, row_spec]
    out_shape = [row_shape, row_shape, row_shape]
    if rope:
        assert seq_len == TOK_PER_TILE
        for tab in rope_tabs:
            in_specs.append(pl.BlockSpec((SUB_TOK, HEAD_W), lambda i, j: (j, 0)))
            args.append(tab)
    else:
        out_specs += [pl.BlockSpec((seqs_per_sub, ATT_WIDTH, seq_len),
                                   lambda i, j: (i * N_SUB + j, 0, 0)),
                      pl.BlockSpec((seqs_per_sub, seq_len, N_HEADS, HEAD_W),
                                   lambda i, j: (i * N_SUB + j, 0, 0, 0))]
        out_shape += [jax.ShapeDtypeStruct((n_batch, ATT_WIDTH, seq_len), F32),
                      jax.ShapeDtypeStruct((n_batch, seq_len, N_HEADS, HEAD_W), F32)]
    out_specs.append(pl.BlockSpec((N_GROUPS, CW, ROWS_PER_TILE), lambda i, j: (0, 0, i)))
    out_shape.append(jax.ShapeDtypeStruct((N_GROUPS, CW, n_rows), BF16))
    return pl.pallas_call(
        functools.partial(_in_proj_kernel, rope=rope, seqs_per_sub=seqs_per_sub),
        grid=(n_tiles, N_SUB),
        in_specs=in_specs,
        out_specs=out_specs,
        out_shape=out_shape,
        scratch_shapes=[pltpu.VMEM((SSM_WIDTH, D_MODEL), BF16),
                        pltpu.VMEM((CHUNK, ROWS_PER_TILE, D_MODEL), F32),
                        pltpu.SemaphoreType.DMA((CHUNK,))],
        compiler_params=_cparams(2),
        name="in_proj",
    )(*args)


def _attn_kernel(*refs, has_ctx, lam_init, n_seq, seq_len, tq, n_cast):
    if n_cast:
        cast_in = refs[len(refs) - 2 * n_cast - 1:len(refs) - n_cast - 1]
        cast_out = refs[len(refs) - n_cast:]
        refs = refs[:len(refs) - 2 * n_cast - 1] + (refs[len(refs) - n_cast - 1],)
        for src, dst in zip(cast_in, cast_out):
            dst[...] = src[...].astype(dst.dtype)
    if has_ctx:
        lam_ref, sg_ref, q_ref, ck_ref, cv_ref, k_ref, v_ref, o_ref = refs
    else:
        lam_ref, sg_ref, q_ref, k_ref, v_ref, o_ref = refs
    lp = lam_ref[...]
    lam = (jnp.exp(jnp.sum(lp[0:1] * lp[1:2], axis=-1, keepdims=True))
           - jnp.exp(jnp.sum(lp[2:3] * lp[3:4], axis=-1, keepdims=True)) + lam_init)
    first_map = lax.broadcasted_iota(jnp.int32, (1, HEAD_W), 1) < HEAD_DIM
    ti = min(TQ_ITEM, tq)
    for row0 in range(0, n_seq * tq, ti):
        b = row0 // tq
        q_rows = slice(row0, row0 + ti)
        for hd in range(N_HEADS):
            sl = slice(hd * HEAD_W, (hd + 1) * HEAD_W)
            qh = q_ref[q_rows, sl]
            zero = jnp.zeros_like(qh)
            qs = jnp.concatenate([jnp.where(first_map, qh, zero),
                                  jnp.where(first_map, zero, qh)], axis=0)
            kv_rows = slice(b * seq_len, (b + 1) * seq_len)
            parts = [(k_ref[kv_rows, sl], v_ref[kv_rows, sl])]
            if has_ctx:
                parts.insert(0, (ck_ref[:, sl].astype(BF16), cv_ref[:, sl].astype(BF16)))
            scores = [lax.dot_general(qs, kk, NT_DIMS, preferred_element_type=F32)
                      for kk, _ in parts]
            mx = scores[0].max(axis=-1, keepdims=True)
            for s in scores[1:]:
                mx = jnp.maximum(mx, s.max(axis=-1, keepdims=True))
            acc = None
            for s, (_, vv) in zip(scores, parts):
                e = jnp.exp2(s - mx).astype(BF16)
                v_one = jnp.concatenate([vv, jnp.ones_like(vv)], axis=1)
                pv = jnp.dot(e, v_one, preferred_element_type=F32)
                acc = pv if acc is None else acc + pv
            num = acc[:, 0:HEAD_W] / acc[:, HEAD_W:2 * HEAD_W]
            o = num[0:ti] - lam * num[ti:2 * ti]
            o = _rms(o, sg_ref[...]) * (1.0 - lam_init)
            o_ref[q_rows, sl] = o.astype(o_ref.dtype)


def _attention(q, k, v, ctx_k, ctx_v, lam_params, subln_g, n_batch, seq_len, lam_init,
               cast_weights=()):
    has_ctx = ctx_k is not None
    tq = min(1024, seq_len)
    n_q = seq_len // tq
    n_seq = 1 if n_q > 1 else min(4, n_batch)
    in_specs = [pl.BlockSpec((4, HEAD_DIM), lambda b, i: (0, 0)),
                pl.BlockSpec((1, HEAD_W), lambda b, i: (0, 0)),
                pl.BlockSpec((n_seq * tq, ATT_WIDTH), lambda b, i: (b * n_q + i, 0))]
    args = [lam_params, subln_g.reshape(1, HEAD_W), q]
    if has_ctx:
        past = ctx_k.shape[1]
        ctx_spec = pl.BlockSpec((None, past, ATT_WIDTH), lambda b, i: (b, 0, 0))
        in_specs += [ctx_spec, ctx_spec]
        args += [ctx_k, ctx_v]
    kv_spec = pl.BlockSpec((n_seq * seq_len, ATT_WIDTH), lambda b, i: (b, 0))
    in_specs += [kv_spec, kv_spec]
    args += [k, v]
    out_specs = [pl.BlockSpec((n_seq * tq, ATT_WIDTH), lambda b, i: (b * n_q + i, 0))]
    out_shape = [jax.ShapeDtypeStruct((n_batch * seq_len, ATT_WIDTH), BF16)]
    n_steps = (n_batch // n_seq) * n_q
    for w in cast_weights:
        rows = w.shape[0] // n_steps
        spec = pl.BlockSpec((rows, w.shape[1]), lambda b, i: (b * n_q + i, 0))
        in_specs.append(spec)
        args.append(w)
        out_specs.append(spec)
        out_shape.append(jax.ShapeDtypeStruct(w.shape, BF16))
    outs = pl.pallas_call(
        functools.partial(_attn_kernel, has_ctx=has_ctx, lam_init=lam_init,
                          n_seq=n_seq, seq_len=seq_len, tq=tq, n_cast=len(cast_weights)),
        grid=(n_batch // n_seq, n_q),
        in_specs=in_specs,
        out_specs=out_specs,
        out_shape=out_shape,
        compiler_params=_cparams(2),
        name="diff_attention",
    )(*args)
    return outs[0], tuple(outs[1:])


def _cmul(ar, ai, br, bi):
    return ar * br - ai * bi, ar * bi + ai * br


def _ssm_prep_kernel(lre_ref, lim_ref, ls_ref, bre_ref, bim_ref, cre_ref, cim_ref, d_ref,
                     mt_ref, gt_ref, wo_ref, at_ref):
    lane = lax.broadcasted_iota(jnp.int32, (SSM_GROUP, CW), 1)
    chan = lax.broadcasted_iota(jnp.int32, (SSM_GROUP, CW), 0)
    for gi in range(GROUP_BLOCK):
        gt_cols, wo_cols, at_cols, toeplitz = [], [], [], []
        for dr in range(2):
            lr = jnp.minimum(lre_ref[dr, gi], -1e-4)
            li = lim_ref[dr, gi]
            step = jnp.exp(ls_ref[dr, gi])
            mag = jnp.exp(lr * step)
            a_re = mag * jnp.cos(li * step)
            a_im = mag * jnp.sin(li * step)
            den = lr * lr + li * li
            nr = a_re - 1.0
            f_re = (nr * lr + a_im * li) / den
            f_im = (a_im * lr - nr * li) / den
            bt_re, bt_im = bre_ref[dr, gi], bim_ref[dr, gi]
            bb_re, bb_im = _cmul(f_re, f_im, bt_re, bt_im)
            c_re, c_im = cre_ref[dr, gi], cim_ref[dr, gi]
            pw = [(jnp.ones_like(a_re), jnp.zeros_like(a_im))]
            for _ in range(CHUNK):
                pw.append(_cmul(pw[-1][0], pw[-1][1], a_re, a_im))
            g_re, g_im, e_re, e_im = [], [], [], []
            for t in range(CHUNK):
                pr, pi = pw[CHUNK - 1 - t] if dr == 0 else pw[t]
                r, i = _cmul(bb_re, bb_im, pr, pi)
                g_re.append(r)
                g_im.append(i)
                pr, pi = pw[t + 1] if dr == 0 else pw[CHUNK - t]
                r, i = _cmul(c_re, c_im, pr, pi)
                e_re.append(r)
                e_im.append(-i)
            g_cat = jnp.concatenate([jnp.concatenate(g_re, axis=0),
                                     jnp.concatenate(g_im, axis=0)], axis=1)
            gt_cols.append(g_cat)
            wo_cols.append(jnp.concatenate([jnp.concatenate(e_re, axis=0),
                                            jnp.concatenate(e_im, axis=0)], axis=1))
            c_cat = jnp.concatenate([c_re, -c_im], axis=1)
            toeplitz.append(lax.dot_general(c_cat, g_cat, NT_DIMS,
                                            precision=lax.Precision.HIGHEST,
                                            preferred_element_type=F32))
            apw = [pw[CHUNK]]
            for _ in range(SCAN_BLOCK - 1):
                apw.append(_cmul(apw[-1][0], apw[-1][1], apw[0][0], apw[0][1]))
            order = list(range(SCAN_BLOCK)) if dr == 0 else list(range(SCAN_BLOCK - 1, -1, -1))
            order += [2 ** l - 1 for l in range(SCAN_LEVELS)]
            order += [0] * (AT_ROWS - len(order))
            at_cols += [jnp.concatenate([jnp.concatenate([apw[i][0], apw[i][0]], axis=1)
                                         for i in order], axis=0),
                        jnp.concatenate([jnp.concatenate([-apw[i][1], apw[i][1]], axis=1)
                                         for i in order], axis=0)]
        kf_rev, kb = toeplitz
        d_skip = d_ref[gi]
        blocks = []
        for t in range(CHUNK):
            fwd = pltpu.roll(kf_rev, (CW - (CHUNK - 1 - t) * SSM_GROUP) % CW, axis=1)
            bwd = pltpu.roll(kb, t * SSM_GROUP, axis=1)
            blocks.append(jnp.where(lane < (t + 1) * SSM_GROUP, fwd, 0.0)
                          + jnp.where(lane >= t * SSM_GROUP, bwd, 0.0)
                          + jnp.where(lane == chan + t * SSM_GROUP, d_skip, 0.0))
        mt_ref[gi] = jnp.concatenate(blocks, axis=0).astype(mt_ref.dtype)
        gt_ref[gi] = jnp.concatenate(gt_cols, axis=1).astype(gt_ref.dtype)
        wo_ref[gi] = jnp.concatenate(wo_cols, axis=1).astype(wo_ref.dtype)
        at_ref[gi] = jnp.concatenate(at_cols, axis=1)


def _ssm_prep(lam_re, lam_im, log_step, b_re, b_im, c_re, c_im, d_skip):
    row = lambda a: a.reshape(2, N_GROUPS, 1, SSM_STATE)
    bt = lambda a: jnp.swapaxes(a, 2, 3)
    d_row = jnp.tile((d_skip[0] + d_skip[1]).reshape(N_GROUPS, 1, SSM_GROUP), (1, 1, CHUNK))
    gb = GROUP_BLOCK
    vec_spec = pl.BlockSpec((2, gb, 1, SSM_STATE), lambda i: (0, i, 0, 0))
    mat_spec = pl.BlockSpec((2, gb, SSM_GROUP, SSM_STATE), lambda i: (0, i, 0, 0))
    w_spec = pl.BlockSpec((gb, CW, CW), lambda i: (i, 0, 0))
    w_shape = jax.ShapeDtypeStruct((N_GROUPS, CW, CW), BF16)
    return pl.pallas_call(
        _ssm_prep_kernel,
        grid=(N_GROUPS // gb,),
        in_specs=[vec_spec, vec_spec,
                  pl.BlockSpec((2, gb, 1, 1), lambda i: (0, i, 0, 0)),
                  mat_spec, mat_spec, mat_spec, mat_spec,
                  pl.BlockSpec((gb, 1, CW), lambda i: (i, 0, 0))],
        out_specs=[w_spec, w_spec, w_spec,
                   pl.BlockSpec((gb, AT_ROWS, 4 * 2 * SSM_STATE), lambda i: (i, 0, 0))],
        out_shape=[w_shape, w_shape, w_shape,
                   jax.ShapeDtypeStruct((N_GROUPS, AT_ROWS, 4 * 2 * SSM_STATE), F32)],
        compiler_params=_cparams(1),
        name="ssm_prep",
    )(row(lam_re), row(lam_im), log_step.reshape(2, N_GROUPS, 1, 1),
      bt(b_re), bt(b_im), c_re, c_im, d_row)


def _shift_rows(x, m, down):
    n = x.shape[0]
    return pltpu.roll(x, m if down else n - m, axis=0)


def _ssm_kernel(*refs, n_seq, has_h0):
    if has_h0:
        (xt_ref, mt_ref, gt_ref, wo_ref, at_ref, wg_ref, bg_ref, h0_ref,
         z_ref, zs_ref) = refs
    else:
        (xt_ref, mt_ref, gt_ref, wo_ref, at_ref, wg_ref, bg_ref,
         z_ref, st_ref, zs_ref, fin_ref) = refs
    step = pl.program_id(0)
    n_rows = xt_ref.shape[-1]
    seg = n_rows // n_seq
    cw2 = 2 * SSM_STATE
    assert seg % SCAN_BLOCK == 0
    pos = lax.broadcasted_iota(jnp.int32, (n_rows, cw2), 0) % seg
    row_blk = lax.broadcasted_iota(jnp.int32, (SCAN_BLOCK, cw2), 0)
    for gi in range(GROUP_BLOCK):
        xt = xt_ref[gi]
        at = at_ref[gi]
        s_all = lax.dot_general(xt, gt_ref[gi], TN_DIMS, preferred_element_type=F32)
        h_parts = []
        for dr in range(2):
            s = s_all[:, dr * cw2:(dr + 1) * cw2]
            p_tab = at[:, (2 * dr) * cw2:(2 * dr + 1) * cw2]
            q_tab = at[:, (2 * dr + 1) * cw2:(2 * dr + 2) * cw2]
            down = dr == 0
            if has_h0:
                h0 = jnp.zeros((n_rows, cw2), F32)
                riota = lax.broadcasted_iota(jnp.int32, (n_rows, cw2), 0)
                for b in range(n_seq):
                    edge = b * seg if down else b * seg + seg - 1
                    h0 = jnp.where(riota == edge, h0_ref[gi, b:b + 1, dr * cw2:(dr + 1) * cw2], h0)
                lv0 = SCAN_BLOCK
                s = (s + p_tab[lv0:lv0 + 1] * h0
                     + q_tab[lv0:lv0 + 1] * pltpu.roll(h0, SSM_STATE, axis=1))
            n_blk = n_rows // SCAN_BLOCK
            s3 = s.reshape(n_blk, SCAN_BLOCK, cw2)
            for lvl in range(SCAN_LEVELS):
                m = 2 ** lvl
                valid = (row_blk >= m) if down else (row_blk < SCAN_BLOCK - m)
                row = SCAN_BLOCK + lvl
                p = jnp.where(valid, p_tab[row:row + 1], 0.0)
                q = jnp.where(valid, q_tab[row:row + 1], 0.0)
                sh = pltpu.roll(s3, m if down else SCAN_BLOCK - m, axis=1)
                s3 = s3 + p * sh + q * pltpu.roll(sh, SSM_STATE, axis=2)
            sw3 = pltpu.roll(s3, SSM_STATE, axis=2)
            p_blk, q_blk = p_tab[0:SCAN_BLOCK], q_tab[0:SCAN_BLOCK]
            blocks = [s3[i] for i in range(n_blk)]
            blocks_w = [sw3[i] for i in range(n_blk)]
            blk_per_seq = seg // SCAN_BLOCK
            edge = slice(SCAN_BLOCK - 1, SCAN_BLOCK) if down else slice(0, 1)
            for q_i in range(n_seq):
                idxs = list(range(q_i * blk_per_seq, (q_i + 1) * blk_per_seq))
                idxs = idxs if down else idxs[::-1]
                for prev, cur in zip(idxs[:-1], idxs[1:]):
                    c = jnp.broadcast_to(blocks[prev][edge], (SCAN_BLOCK, cw2))
                    cw = jnp.broadcast_to(blocks_w[prev][edge], (SCAN_BLOCK, cw2))
                    blocks[cur] = blocks[cur] + p_blk * c + q_blk * cw
                    blocks_w[cur] = blocks_w[cur] + p_blk * cw - q_blk * c
            s = jnp.concatenate(blocks, axis=0)
            if not has_h0:
                fin_ref[dr] = s
                st_ref[gi, :, dr * cw2:(dr + 1) * cw2] = (
                    fin_ref[dr, pl.ds(seg - 1 if down else 0, n_seq, stride=seg), :])
            ent = _shift_rows(s, 1, down)
            ent = jnp.where((pos >= 1) if down else (pos < seg - 1), ent, 0.0)
            if has_h0:
                ent = ent + h0
            h_parts.append(ent)
        h_all = jnp.concatenate(h_parts, axis=1).astype(BF16)
        yt = (jnp.dot(mt_ref[gi], xt, preferred_element_type=F32)
              + lax.dot_general(wo_ref[gi], h_all, NT_DIMS, preferred_element_type=F32))
        z = jax.nn.gelu(yt, approximate=True)
        grp = step * GROUP_BLOCK + gi
        for t in range(CHUNK):
            zs_ref[t, pl.ds(pl.multiple_of(grp * SSM_GROUP, SSM_GROUP), SSM_GROUP), :] = (
                z[t * SSM_GROUP:(t + 1) * SSM_GROUP, :])

    @pl.when(step == pl.num_programs(0) - 1)
    def _():
        for t in range(CHUNK):
            zt = zs_ref[t]
            gate = jnp.dot(wg_ref[...], zt.astype(BF16),
                           preferred_element_type=F32) + bg_ref[...]
            z_ref[t] = (zt * jax.nn.sigmoid(gate)).T.astype(z_ref.dtype)


def _ssm(xt, mt, gt, wo, at, w_glu_t, b_glu_col, h0, n_seq):
    n_rows = xt.shape[-1]
    has_h0 = h0 is not None
    gb = GROUP_BLOCK
    w_spec = pl.BlockSpec((gb, CW, CW), lambda i: (i, 0, 0))
    in_specs = [pl.BlockSpec((gb, CW, n_rows), lambda i: (i, 0, 0)),
                w_spec, w_spec, w_spec,
                pl.BlockSpec((gb, AT_ROWS, 4 * 2 * SSM_STATE), lambda i: (i, 0, 0)),
                pl.BlockSpec((SSM_WIDTH, SSM_WIDTH), lambda i: (0, 0)),
                pl.BlockSpec((SSM_WIDTH, 1), lambda i: (0, 0))]
    args = [xt, mt, gt, wo, at, w_glu_t, b_glu_col]
    out_specs = [pl.BlockSpec((CHUNK, n_rows, SSM_WIDTH), lambda i: (0, 0, 0))]
    out_shape = [jax.ShapeDtypeStruct((CHUNK, n_rows, SSM_WIDTH), BF16)]
    scratch = [pltpu.VMEM((CHUNK, SSM_WIDTH, n_rows), F32)]
    if has_h0:
        in_specs.append(pl.BlockSpec((gb, n_seq, CW), lambda i: (i, 0, 0)))
        args.append(h0)
    else:
        out_specs.append(pl.BlockSpec((gb, n_seq, CW), lambda i: (i, 0, 0)))
        out_shape.append(jax.ShapeDtypeStruct((N_GROUPS, n_seq, CW), F32))
        scratch.append(pltpu.VMEM((2, n_rows, 2 * SSM_STATE), F32))
    return pl.pallas_call(
        functools.partial(_ssm_kernel, n_seq=n_seq, has_h0=has_h0),
        grid=(N_GROUPS // gb,),
        in_specs=in_specs,
        out_specs=out_specs,
        out_shape=out_shape,
        scratch_shapes=scratch,
        compiler_params=_cparams(1),
        name="ssm_scan_glu",
    )(*args)


FF_CHUNK = 256


def _out_ffn_kernel(x_ref, attn_ref, zt_ref, mod_ref, g_ref, wo_ref, wfi_ref, wfo_ref,
                    o_ref, mixz_ref, act_ref):
    j = pl.program_id(1)

    @pl.when(j == 0)
    def _():
        z_all = zt_ref[...].reshape(TOK_PER_TILE, SSM_WIDTH)
        part = jnp.dot(z_all, wo_ref[ATT_WIDTH:, :], preferred_element_type=F32)
        for t in range(CHUNK):
            for cb in range(N_COL_BLOCKS):
                mixz_ref[cb, pl.ds(t, ROWS_PER_TILE, stride=CHUNK), :] = (
                    part[t * ROWS_PER_TILE:(t + 1) * ROWS_PER_TILE, cb * LANES:(cb + 1) * LANES])

    gate1 = mod_ref[:, 2 * D_MODEL:3 * D_MODEL]
    shift2 = mod_ref[:, 3 * D_MODEL:4 * D_MODEL]
    scale2 = mod_ref[:, 4 * D_MODEL:5 * D_MODEL]
    gate2 = mod_ref[:, 5 * D_MODEL:6 * D_MODEL]
    half = SUB_TOK // 2
    gain2 = g_ref[2:3, :] * (1.0 + scale2)

    def pre_ffn(hf):
        r = slice(hf * half, (hf + 1) * half)
        rows = pl.ds(pl.multiple_of(j * SUB_TOK + hf * half, half), half)
        mix = (jnp.dot(attn_ref[r, :], wo_ref[0:ATT_WIDTH, :], preferred_element_type=F32)
               + jnp.concatenate([mixz_ref[cb, rows, :] for cb in range(N_COL_BLOCKS)], axis=1))
        x1 = x_ref[r, :] + gate1 * _rms(mix, g_ref[1:2, :])
        ms = jnp.mean(x1 * x1, axis=-1, keepdims=True)
        return x1, (x1 * lax.rsqrt(ms + NORM_EPS) * gain2 + shift2).astype(BF16)

    def ffn_in(hf, h, chunks):
        r = slice(hf * half, (hf + 1) * half)
        for c in chunks:
            lo = c * FF_CHUNK
            gt = jnp.dot(h, wfi_ref[:, lo:lo + FF_CHUNK], preferred_element_type=F32)
            up = jnp.dot(h, wfi_ref[:, D_FF + lo:D_FF + lo + FF_CHUNK],
                         preferred_element_type=F32)
            act_ref[r, lo:lo + FF_CHUNK] = (_silu(gt) * up).astype(BF16)

    def ffn_out(hf, x1):
        r = slice(hf * half, (hf + 1) * half)
        f = jnp.dot(act_ref[r, :], wfo_ref[...], preferred_element_type=F32)
        o_ref[r, :] = x1 + gate2 * _rms(f, g_ref[3:4, :])

    n_chunks = D_FF // FF_CHUNK
    x1_a, h_a = pre_ffn(0)
    x1_b, h_b = pre_ffn(1)
    ffn_in(0, h_a, range(n_chunks))
    ffn_out(0, x1_a)
    ffn_in(1, h_b, range(n_chunks))
    ffn_out(1, x1_b)


def _out_ffn(x2d, attn, zt, mods, mod_rows, norm_g, w_o, w_ffn_in, w_ffn_out):
    n_tok = x2d.shape[0]
    n_tiles = n_tok // TOK_PER_TILE
    mod_row0, n_mod = mod_rows
    tiles_per_mod = n_tiles // n_mod
    const = lambda i, j: (0, 0)
    row_spec = lambda w: pl.BlockSpec((SUB_TOK, w), lambda i, j: (i * N_SUB + j, 0))
    return pl.pallas_call(
        _out_ffn_kernel,
        grid=(n_tiles, N_SUB),
        in_specs=[row_spec(D_MODEL), row_spec(ATT_WIDTH),
                  pl.BlockSpec((CHUNK, ROWS_PER_TILE, SSM_WIDTH), lambda i, j: (0, i, 0)),
                  pl.BlockSpec((None, 1, N_MOD * D_MODEL),
                               lambda i, j: (mod_row0 + i // tiles_per_mod, 0, 0)),
                  pl.BlockSpec((4, D_MODEL), const),
                  pl.BlockSpec((2 * ATT_WIDTH, D_MODEL), const, pipeline_mode=pl.Buffered(1)),
                  pl.BlockSpec((D_MODEL, 2 * D_FF), const, pipeline_mode=pl.Buffered(1)),
                  pl.BlockSpec((D_FF, D_MODEL), const, pipeline_mode=pl.Buffered(1))],
        out_specs=row_spec(D_MODEL),
        out_shape=jax.ShapeDtypeStruct((n_tok, D_MODEL), F32),
        scratch_shapes=[pltpu.VMEM((N_COL_BLOCKS, TOK_PER_TILE, LANES), F32),
                        pltpu.VMEM((SUB_TOK, D_FF), BF16)],
        compiler_params=_cparams(2),
        name="out_proj_ffn",
    )(x2d, attn, zt, mods, norm_g, w_o, w_ffn_in, w_ffn_out)


def _rope_tables(seq_len):
    t = np.arange(seq_len)
    row = (t // GRID_W).astype(np.float32)
    col = (t % GRID_W).astype(np.float32)
    half = HEAD_DIM // 2
    inv_freq = (np.float32(ROPE_BASE)
                ** (-np.arange(0, half, 2, dtype=np.float32) / np.float32(half))).astype(np.float32)
    ang_r = row[:, None] * inv_freq
    ang_c = col[:, None] * inv_freq
    ang = np.concatenate([ang_r, ang_r, ang_c, ang_c], axis=-1)
    cos, sin = np.cos(ang), np.sin(ang)
    upper = (np.arange(HEAD_DIM) % 32) < 16
    sa = np.where(upper, -sin, 0.0)
    sb = np.where(upper, 0.0, sin)
    two = lambda a: jnp.asarray(np.concatenate([a, a], axis=-1), dtype=F32)
    return two(cos), two(sa), two(sb)


def _layer(x, mods, mod_rows, lam_init, rope_tabs, ctx_k, ctx_v, h0, weights, prep):
    n_batch, seq_len = x.shape[:2]
    g = weights['norm_g']
    outs = _in_proj(x, mods, mod_rows, g[0:1], weights['w_in'], rope_tabs)
    q, k, v = outs[:3]
    pending = () if 'late_bf16' in weights else weights['late_f32']
    attn, cast = _attention(q, k, v, ctx_k, ctx_v, weights['lam'], weights['subln_g'],
                            n_batch, seq_len, lam_init, cast_weights=pending)
    if pending:
        weights['late_bf16'] = cast
    ssm_out = _ssm(outs[-1], *prep, weights['w_glu_t'], weights['b_glu_col'], h0, n_batch)
    y = _out_ffn(x.reshape(n_batch * seq_len, D_MODEL), attn, ssm_out[0], mods, mod_rows, g,
                 *weights['late_bf16'])
    return y.reshape(x.shape), outs[3:-1], ssm_out[1:]


def kernel(x_prompt, x_sample, cache_k, cache_v, state_ssm_re, state_ssm_im, c, c_ctx, w_mod, b_mod, norm_g, w_in, lam_params, subln_g, ssm_lambda_re, ssm_lambda_im, ssm_log_step, ssm_b_re, ssm_b_im, ssm_c_re, ssm_c_im, ssm_d, w_glu, b_glu, w_o, w_ffn_in, w_ffn_out):
    depth = w_mod.shape[0]
    assert depth == 1
    bp = x_prompt.shape[0]
    bd, ld_len = x_sample.shape[:2]
    past = cache_k.shape[2]
    xp, xs = x_prompt, x_sample
    rope_tabs = _rope_tables(ld_len)
    ks_out, vs_out, hr_out, hi_out = [], [], [], []
    for l in range(depth):
        lam_init = 0.8 - 0.6 * math.exp(-0.3 * l)
        mods = _modulation(c_ctx, c, w_mod[l], b_mod[l])
        weights = {
            'norm_g': norm_g[l],
            'w_in': w_in[l].astype(BF16),
            'lam': lam_params[l], 'subln_g': subln_g[l],
            'w_glu_t': w_glu[l].T.astype(BF16), 'b_glu_col': b_glu[l].reshape(SSM_WIDTH, 1),
            'late_f32': (w_o[l], w_ffn_in[l], w_ffn_out[l]),
        }
        prep = _ssm_prep(ssm_lambda_re[l], ssm_lambda_im[l], ssm_log_step[l],
                         ssm_b_re[l], ssm_b_im[l], ssm_c_re[l], ssm_c_im[l], ssm_d[l])
        ck = cache_k[:, l].reshape(bd, past, ATT_WIDTH)
        cv = cache_v[:, l].reshape(bd, past, ATT_WIDTH)
        h0 = jnp.stack([state_ssm_re[:, l], state_ssm_im[:, l]], axis=2)
        h0 = h0.transpose(3, 0, 1, 2, 4).reshape(N_GROUPS, bd, CW)
        xs, _, _ = _layer(xs, mods, (1, bd), lam_init, rope_tabs, ck, cv, h0, weights, prep)
        xp, (k_ctx, v_ctx), (st,) = _layer(xp, mods, (0, 1), lam_init, None, None, None, None,
                                           weights, prep)
        ks_out.append(jnp.swapaxes(k_ctx, 1, 2).reshape(bp, -1, 2 * N_HEADS, HEAD_DIM))
        vs_out.append(v_ctx)
        fin = st.reshape(N_GROUPS, bp, 2, 2, SSM_STATE).transpose(1, 2, 3, 0, 4)
        hr_out.append(fin[:, :, 0])
        hi_out.append(fin[:, :, 1])
    return (xp, xs, jnp.stack(ks_out, axis=1), jnp.stack(vs_out, axis=1),
            jnp.stack(hr_out, axis=1), jnp.stack(hi_out, axis=1))
```

```python
import functools
import math

import jax
import jax.numpy as jnp
import numpy as np
from jax import lax
from jax.experimental import pallas as pl
from jax.experimental.pallas import tpu as pltpu

F32 = jnp.float32
BF16 = jnp.bfloat16

D_MODEL = 1024
GRID_W = 64
ATT_WIDTH = 512
SSM_WIDTH = 512
HEAD_DIM = 64
N_HEADS = 4
HEAD_W = 2 * HEAD_DIM
SSM_GROUP = 16
N_GROUPS = 32
SSM_STATE = 64
D_FF = 2816
N_MOD = 6
ROPE_BASE = 10000.0
NORM_EPS = 1e-6

LANES = 128
N_COL_BLOCKS = D_MODEL // LANES
CHUNK = 16
CW = CHUNK * SSM_GROUP
SCAN_BLOCK = 8
SCAN_LEVELS = 3
AT_ROWS = 16
GROUP_BLOCK = 8

ROWS_PER_TILE = 128
TOK_PER_TILE = ROWS_PER_TILE * CHUNK
SUB_TOK = 512
TQ_ITEM = 256
N_SUB = TOK_PER_TILE // SUB_TOK

VMEM_LIMIT = 56 * 1024 * 1024

NT_DIMS = (((1,), (1,)), ((), ()))
TN_DIMS = (((0,), (0,)), ((), ()))


def _cparams(n_axes):
    return pltpu.CompilerParams(
        dimension_semantics=("arbitrary",) * n_axes,
        vmem_limit_bytes=VMEM_LIMIT)


def _rms(x, g):
    ms = jnp.mean(x * x, axis=-1, keepdims=True)
    return x * lax.rsqrt(ms + NORM_EPS) * g


def _silu(x):
    return x * jax.nn.sigmoid(x)


MOD_ROWS = 8


def _mod_kernel(ctx_ref, c_ref, w_ref, b_ref, o_ref):
    n_lat, tk = c_ref.shape
    row = lax.broadcasted_iota(jnp.int32, (MOD_ROWS, tk), 0)
    cond = jnp.where(row == 0, ctx_ref[...], 0.0)
    for b in range(n_lat):
        cond = jnp.where(row == 1 + b, c_ref[b:b + 1, :], cond)
    part = jnp.dot(_silu(cond).astype(BF16), w_ref[...].astype(BF16),
                   preferred_element_type=F32)

    @pl.when(pl.program_id(0) == 0)
    def _():
        o_ref[:, 0, :] = part + b_ref[...]

    @pl.when(pl.program_id(0) > 0)
    def _():
        o_ref[:, 0, :] += part


def _modulation(c_ctx, c, w_mod, b_mod):
    n = w_mod.shape[1]
    tk = 256
    assert 1 + c.shape[0] <= MOD_ROWS
    return pl.pallas_call(
        _mod_kernel,
        grid=(D_MODEL // tk,),
        in_specs=[pl.BlockSpec((1, tk), lambda k: (0, k)),
                  pl.BlockSpec((c.shape[0], tk), lambda k: (0, k)),
                  pl.BlockSpec((tk, n), lambda k: (k, 0)),
                  pl.BlockSpec((1, n), lambda k: (0, 0))],
        out_specs=pl.BlockSpec((MOD_ROWS, 1, n), lambda k: (0, 0, 0)),
        out_shape=jax.ShapeDtypeStruct((MOD_ROWS, 1, n), F32),
        compiler_params=_cparams(1),
        name="modulation",
    )(c_ctx.reshape(1, D_MODEL), c, w_mod, b_mod.reshape(1, n))


def _rope(x, cos, sa, sb):
    return (x * cos + pltpu.roll(x, HEAD_W - 16, axis=1) * sa
            + pltpu.roll(x, 16, axis=1) * sb)


def _in_proj_kernel(*refs, rope, seqs_per_sub):
    x_ref, x3_hbm, mod_ref, g_ref, w_ref = refs[:5]
    refs = refs[5:]
    if rope:
        cos_ref, sa_ref, sb_ref = refs[:3]
        refs = refs[3:]
        q_ref, k_ref, v_ref, ut_ref, wut_ref, xt_ref, xt_sem = refs
    else:
        q_ref, k_ref, v_ref, kc_ref, vc_ref, ut_ref, wut_ref, xt_ref, xt_sem = refs
    tile = pl.program_id(0)
    j = pl.program_id(1)
    t_per_sub = CHUNK // N_SUB
    t_early = CHUNK - t_per_sub

    def gather(tile_idx, t):
        src = x3_hbm.at[pl.ds(tile_idx * ROWS_PER_TILE, ROWS_PER_TILE), t, :]
        return pltpu.make_async_copy(src, xt_ref.at[t], xt_sem.at[t])

    @pl.when(j == 0)
    def _():
        @pl.when(tile == 0)
        def _():
            for t in range(t_early):
                gather(0, t).start()
        for t in range(t_early, CHUNK):
            gather(tile, t).start()

    t_base = j * t_per_sub
    for d in range(t_per_sub):
        gather(tile, t_base + d).wait()

    @pl.when((j == N_SUB - 1) & (tile + 1 < pl.num_programs(0)))
    def _():
        for t in range(t_early):
            gather(tile + 1, t).start()

    @pl.when((tile == 0) & (j == 0))
    def _():
        wut_ref[...] = w_ref[:, 3 * ATT_WIDTH:].T

    shift = mod_ref[:, 0:D_MODEL]
    gain = g_ref[...] * (1.0 + mod_ref[:, D_MODEL:2 * D_MODEL])

    def norm_mod(xv):
        ms = jnp.mean(xv * xv, axis=-1, keepdims=True)
        return (xv * lax.rsqrt(ms + NORM_EPS) * gain + shift).astype(BF16)

    half = SUB_TOK // 2
    proj = jnp.concatenate(
        [jnp.dot(norm_mod(x_ref[hf * half:(hf + 1) * half, :]), w_ref[:, 0:3 * ATT_WIDTH],
                 preferred_element_type=F32) for hf in range(2)], axis=0)
    q = proj[:, 0:ATT_WIDTH]
    k = proj[:, ATT_WIDTH:2 * ATT_WIDTH]
    v = proj[:, 2 * ATT_WIDTH:3 * ATT_WIDTH]
    qscale = HEAD_DIM ** -0.5 * math.log2(math.e)
    if rope:
        cos, sa, sb = cos_ref[...], sa_ref[...], sb_ref[...]
        for hd in range(N_HEADS):
            sl = slice(hd * HEAD_W, (hd + 1) * HEAD_W)
            q_ref[:, sl] = (_rope(q[:, sl], cos, sa, sb) * qscale).astype(q_ref.dtype)
            k_ref[:, sl] = _rope(k[:, sl], cos, sa, sb).astype(k_ref.dtype)
    else:
        q_ref[...] = (q * qscale).astype(q_ref.dtype)
        k_ref[...] = k.astype(k_ref.dtype)
        seq = SUB_TOK // seqs_per_sub
        k_t = k.T
        for b in range(seqs_per_sub):
            kc_ref[b] = k_t[:, b * seq:(b + 1) * seq]
            for hd in range(N_HEADS):
                vc_ref[b, :, hd, :] = v[b * seq:(b + 1) * seq, hd * HEAD_W:(hd + 1) * HEAD_W]
    v_ref[...] = v.astype(v_ref.dtype)

    for d0 in range(0, t_per_sub, 2):
        xt = jnp.concatenate([xt_ref[t_base + d0], xt_ref[t_base + d0 + 1]], axis=0)
        ut = lax.dot_general(wut_ref[...], norm_mod(xt), NT_DIMS,
                             preferred_element_type=F32)
        for d in range(2):
            blk = ut[:, d * ROWS_PER_TILE:(d + 1) * ROWS_PER_TILE]
            row0 = pl.multiple_of((t_base + d0 + d) * SSM_GROUP, SSM_GROUP)
            ut_ref[:, pl.ds(row0, SSM_GROUP), :] = (
                blk.reshape(N_GROUPS, SSM_GROUP, ROWS_PER_TILE).astype(ut_ref.dtype))


def _in_proj(x, mods, mod_rows, g0, w_in, rope_tabs):
    n_batch, seq_len = x.shape[:2]
    n_tok = n_batch * seq_len
    n_rows = n_tok // CHUNK
    n_tiles = n_tok // TOK_PER_TILE
    mod_row0, n_mod = mod_rows
    tiles_per_mod = n_tiles // n_mod
    rope = rope_tabs is not None
    seqs_per_sub = max(1, SUB_TOK // seq_len)
    in_specs = [pl.BlockSpec((SUB_TOK, D_MODEL), lambda i, j: (i * N_SUB + j, 0)),
                pl.BlockSpec(memory_space=pl.ANY),
                pl.BlockSpec((None, 1, 2 * D_MODEL),
                             lambda i, j: (mod_row0 + i // tiles_per_mod, 0, 0)),
                pl.BlockSpec((1, D_MODEL), lambda i, j: (0, 0)),
                pl.BlockSpec((D_MODEL, 4 * ATT_WIDTH), lambda i, j: (0, 0))]
    args = [x.reshape(n_tok, D_MODEL), x.reshape(n_rows, CHUNK, D_MODEL), mods, g0, w_in]
    row_spec = pl.BlockSpec((SUB_TOK, ATT_WIDTH), lambda i, j: (i * N_SUB + j, 0))
    row_shape = jax.ShapeDtypeStruct((n_tok, ATT_WIDTH), BF16)
    out_specs = [row_spec, row_spec, row_spec]
    out_shape = [row_shape, row_shape, row_shape]
    if rope:
        assert seq_len == TOK_PER_TILE
        for tab in rope_tabs:
            in_specs.append(pl.BlockSpec((SUB_TOK, HEAD_W), lambda i, j: (j, 0)))
            args.append(tab)
    else:
        out_specs += [pl.BlockSpec((seqs_per_sub, ATT_WIDTH, seq_len),
                                   lambda i, j: (i * N_SUB + j, 0, 0)),
                      pl.BlockSpec((seqs_per_sub, seq_len, N_HEADS, HEAD_W),
                                   lambda i, j: (i * N_SUB + j, 0, 0, 0))]
        out_shape += [jax.ShapeDtypeStruct((n_batch, ATT_WIDTH, seq_len), F32),
                      jax.ShapeDtypeStruct((n_batch, seq_len, N_HEADS, HEAD_W), F32)]
    out_specs.append(pl.BlockSpec((N_GROUPS, CW, ROWS_PER_TILE), lambda i, j: (0, 0, i)))
    out_shape.append(jax.ShapeDtypeStruct((N_GROUPS, CW, n_rows), BF16))
    return pl.pallas_call(
        functools.partial(_in_proj_kernel, rope=rope, seqs_per_sub=seqs_per_sub),
        grid=(n_tiles, N_SUB),
        in_specs=in_specs,
        out_specs=out_specs,
        out_shape=out_shape,
        scratch_shapes=[pltpu.VMEM((SSM_WIDTH, D_MODEL), BF16),
                        pltpu.VMEM((CHUNK, ROWS_PER_TILE, D_MODEL), F32),
                        pltpu.SemaphoreType.DMA((CHUNK,))],
        compiler_params=_cparams(2),
        name="in_proj",
    )(*args)


def _attn_kernel(*refs, has_ctx, lam_init, n_seq, seq_len, tq, n_cast):
    if n_cast:
        cast_in = refs[len(refs) - 2 * n_cast - 1:len(refs) - n_cast - 1]
        cast_out = refs[len(refs) - n_cast:]
        refs = refs[:len(refs) - 2 * n_cast - 1] + (refs[len(refs) - n_cast - 1],)
        for src, dst in zip(cast_in, cast_out):
            dst[...] = src[...].astype(dst.dtype)
    if has_ctx:
        lam_ref, sg_ref, q_ref, ck_ref, cv_ref, k_ref, v_ref, o_ref = refs
    else:
        lam_ref, sg_ref, q_ref, k_ref, v_ref, o_ref = refs
    lp = lam_ref[...]
    lam = (jnp.exp(jnp.sum(lp[0:1] * lp[1:2], axis=-1, keepdims=True))
           - jnp.exp(jnp.sum(lp[2:3] * lp[3:4], axis=-1, keepdims=True)) + lam_init)
    first_map = lax.broadcasted_iota(jnp.int32, (1, HEAD_W), 1) < HEAD_DIM
    ti = min(TQ_ITEM, tq)
    for row0 in range(0, n_seq * tq, ti):
        b = row0 // tq
        q_rows = slice(row0, row0 + ti)
        for hd in range(N_HEADS):
            sl = slice(hd * HEAD_W, (hd + 1) * HEAD_W)
            qh = q_ref[q_rows, sl]
            zero = jnp.zeros_like(qh)
            qs = jnp.concatenate([jnp.where(first_map, qh, zero),
                                  jnp.where(first_map, zero, qh)], axis=0)
            kv_rows = slice(b * seq_len, (b + 1) * seq_len)
            parts = [(k_ref[kv_rows, sl], v_ref[kv_rows, sl])]
            if has_ctx:
                parts.insert(0, (ck_ref[:, sl].astype(BF16), cv_ref[:, sl].astype(BF16)))
            scores = [lax.dot_general(qs, kk, NT_DIMS, preferred_element_type=F32)
                      for kk, _ in parts]
            mx = scores[0].max(axis=-1, keepdims=True)
            for s in scores[1:]:
                mx = jnp.maximum(mx, s.max(axis=-1, keepdims=True))
            acc = None
            for s, (_, vv) in zip(scores, parts):
                e = jnp.exp2(s - mx).astype(BF16)
                v_one = jnp.concatenate([vv, jnp.ones_like(vv)], axis=1)
                pv = jnp.dot(e, v_one, preferred_element_type=F32)
                acc = pv if acc is None else acc + pv
            num = acc[:, 0:HEAD_W] / acc[:, HEAD_W:2 * HEAD_W]
            o = num[0:ti] - lam * num[ti:2 * ti]
            o = _rms(o, sg_ref[...]) * (1.0 - lam_init)
            o_ref[q_rows, sl] = o.astype(o_ref.dtype)


def _attention(q, k, v, ctx_k, ctx_v, lam_params, subln_g, n_batch, seq_len, lam_init,
               cast_weights=()):
    has_ctx = ctx_k is not None
    tq = min(1024, seq_len)
    n_q = seq_len // tq
    n_seq = 1 if n_q > 1 else min(4, n_batch)
    in_specs = [pl.BlockSpec((4, HEAD_DIM), lambda b, i: (0, 0)),
                pl.BlockSpec((1, HEAD_W), lambda b, i: (0, 0)),
                pl.BlockSpec((n_seq * tq, ATT_WIDTH), lambda b, i: (b * n_q + i, 0))]
    args = [lam_params, subln_g.reshape(1, HEAD_W), q]
    if has_ctx:
        past = ctx_k.shape[1]
        ctx_spec = pl.BlockSpec((None, past, ATT_WIDTH), lambda b, i: (b, 0, 0))
        in_specs += [ctx_spec, ctx_spec]
        args += [ctx_k, ctx_v]
    kv_spec = pl.BlockSpec((n_seq * seq_len, ATT_WIDTH), lambda b, i: (b, 0))
    in_specs += [kv_spec, kv_spec]
    args += [k, v]
    out_specs = [pl.BlockSpec((n_seq * tq, ATT_WIDTH), lambda b, i: (b * n_q + i, 0))]
    out_shape = [jax.ShapeDtypeStruct((n_batch * seq_len, ATT_WIDTH), BF16)]
    n_steps = (n_batch // n_seq) * n_q
    for w in cast_weights:
        rows = w.shape[0] // n_steps
        spec = pl.BlockSpec((rows, w.shape[1]), lambda b, i: (b * n_q + i, 0))
        in_specs.append(spec)
        args.append(w)
        out_specs.append(spec)
        out_shape.append(jax.ShapeDtypeStruct(w.shape, BF16))
    outs = pl.pallas_call(
        functools.partial(_attn_kernel, has_ctx=has_ctx, lam_init=lam_init,
                          n_seq=n_seq, seq_len=seq_len, tq=tq, n_cast=len(cast_weights)),
        grid=(n_batch // n_seq, n_q),
        in_specs=in_specs,
        out_specs=out_specs,
        out_shape=out_shape,
        compiler_params=_cparams(2),
        name="diff_attention",
    )(*args)
    return outs[0], tuple(outs[1:])


def _cmul(ar, ai, br, bi):
    return ar * br - ai * bi, ar * bi + ai * br


def _ssm_prep_kernel(lre_ref, lim_ref, ls_ref, bre_ref, bim_ref, cre_ref, cim_ref, d_ref,
                     mt_ref, gt_ref, wo_ref, at_ref):
    lane = lax.broadcasted_iota(jnp.int32, (SSM_GROUP, CW), 1)
    chan = lax.broadcasted_iota(jnp.int32, (SSM_GROUP, CW), 0)
    for gi in range(GROUP_BLOCK):
        gt_cols, wo_cols, at_cols, toeplitz = [], [], [], []
        for dr in range(2):
            lr = jnp.minimum(lre_ref[dr, gi], -1e-4)
            li = lim_ref[dr, gi]
            step = jnp.exp(ls_ref[dr, gi])
            mag = jnp.exp(lr * step)
            a_re = mag * jnp.cos(li * step)
            a_im = mag * jnp.sin(li * step)
            den = lr * lr + li * li
            nr = a_re - 1.0
            f_re = (nr * lr + a_im * li) / den
            f_im = (a_im * lr - nr * li) / den
            bt_re, bt_im = bre_ref[dr, gi], bim_ref[dr, gi]
            bb_re, bb_im = _cmul(f_re, f_im, bt_re, bt_im)
            c_re, c_im = cre_ref[dr, gi], cim_ref[dr, gi]
            pw = [(jnp.ones_like(a_re), jnp.zeros_like(a_im))]
            for _ in range(CHUNK):
                pw.append(_cmul(pw[-1][0], pw[-1][1], a_re, a_im))
            g_re, g_im, e_re, e_im = [], [], [], []
            for t in range(CHUNK):
                pr, pi = pw[CHUNK - 1 - t] if dr == 0 else pw[t]
                r, i = _cmul(bb_re, bb_im, pr, pi)
                g_re.append(r)
                g_im.append(i)
                pr, pi = pw[t + 1] if dr == 0 else pw[CHUNK - t]
                r, i = _cmul(c_re, c_im, pr, pi)
                e_re.append(r)
                e_im.append(-i)
            g_cat = jnp.concatenate([jnp.concatenate(g_re, axis=0),
                                     jnp.concatenate(g_im, axis=0)], axis=1)
            gt_cols.append(g_cat)
            wo_cols.append(jnp.concatenate([jnp.concatenate(e_re, axis=0),
                                            jnp.concatenate(e_im, axis=0)], axis=1))
            c_cat = jnp.concatenate([c_re, -c_im], axis=1)
            toeplitz.append(lax.dot_general(c_cat, g_cat, NT_DIMS,
                                            precision=lax.Precision.HIGHEST,
                                            preferred_element_type=F32))
            apw = [pw[CHUNK]]
            for _ in range(SCAN_BLOCK - 1):
                apw.append(_cmul(apw[-1][0], apw[-1][1], apw[0][0], apw[0][1]))
            order = list(range(SCAN_BLOCK)) if dr == 0 else list(range(SCAN_BLOCK - 1, -1, -1))
            order += [2 ** l - 1 for l in range(SCAN_LEVELS)]
            order += [0] * (AT_ROWS - len(order))
            at_cols += [jnp.concatenate([jnp.concatenate([apw[i][0], apw[i][0]], axis=1)
                                         for i in order], axis=0),
                        jnp.concatenate([jnp.concatenate([-apw[i][1], apw[i][1]], axis=1)
                                         for i in order], axis=0)]
        kf_rev, kb = toeplitz
        d_skip = d_ref[gi]
        blocks = []
        for t in range(CHUNK):
            fwd = pltpu.roll(kf_rev, (CW - (CHUNK - 1 - t) * SSM_GROUP) % CW, axis=1)
            bwd = pltpu.roll(kb, t * SSM_GROUP, axis=1)
            blocks.append(jnp.where(lane < (t + 1) * SSM_GROUP, fwd, 0.0)
                          + jnp.where(lane >= t * SSM_GROUP, bwd, 0.0)
                          + jnp.where(lane == chan + t * SSM_GROUP, d_skip, 0.0))
        mt_ref[gi] = jnp.concatenate(blocks, axis=0).astype(mt_ref.dtype)
        gt_ref[gi] = jnp.concatenate(gt_cols, axis=1).astype(gt_ref.dtype)
        wo_ref[gi] = jnp.concatenate(wo_cols, axis=1).astype(wo_ref.dtype)
        at_ref[gi] = jnp.concatenate(at_cols, axis=1)


def _ssm_prep(lam_re, lam_im, log_step, b_re, b_im, c_re, c_im, d_skip):
    row = lambda a: a.reshape(2, N_GROUPS, 1, SSM_STATE)
    bt = lambda a: jnp.swapaxes(a, 2, 3)
    d_row = jnp.tile((d_skip[0] + d_skip[1]).reshape(N_GROUPS, 1, SSM_GROUP), (1, 1, CHUNK))
    gb = GROUP_BLOCK
    vec_spec = pl.BlockSpec((2, gb, 1, SSM_STATE), lambda i: (0, i, 0, 0))
    mat_spec = pl.BlockSpec((2, gb, SSM_GROUP, SSM_STATE), lambda i: (0, i, 0, 0))
    w_spec = pl.BlockSpec((gb, CW, CW), lambda i: (i, 0, 0))
    w_shape = jax.ShapeDtypeStruct((N_GROUPS, CW, CW), BF16)
    return pl.pallas_call(
        _ssm_prep_kernel,
        grid=(N_GROUPS // gb,),
        in_specs=[vec_spec, vec_spec,
                  pl.BlockSpec((2, gb, 1, 1), lambda i: (0, i, 0, 0)),
                  mat_spec, mat_spec, mat_spec, mat_spec,
                  pl.BlockSpec((gb, 1, CW), lambda i: (i, 0, 0))],
        out_specs=[w_spec, w_spec, w_spec,
                   pl.BlockSpec((gb, AT_ROWS, 4 * 2 * SSM_STATE), lambda i: (i, 0, 0))],
        out_shape=[w_shape, w_shape, w_shape,
                   jax.ShapeDtypeStruct((N_GROUPS, AT_ROWS, 4 * 2 * SSM_STATE), F32)],
        compiler_params=_cparams(1),
        name="ssm_prep",
    )(row(lam_re), row(lam_im), log_step.reshape(2, N_GROUPS, 1, 1),
      bt(b_re), bt(b_im), c_re, c_im, d_row)


def _shift_rows(x, m, down):
    n = x.shape[0]
    return pltpu.roll(x, m if down else n - m, axis=0)


def _ssm_kernel(*refs, n_seq, has_h0):
    if has_h0:
        (xt_ref, mt_ref, gt_ref, wo_ref, at_ref, wg_ref, bg_ref, h0_ref,
         z_ref, zs_ref) = refs
    else:
        (xt_ref, mt_ref, gt_ref, wo_ref, at_ref, wg_ref, bg_ref,
         z_ref, st_ref, zs_ref, fin_ref) = refs
    step = pl.program_id(0)
    n_rows = xt_ref.shape[-1]
    seg = n_rows // n_seq
    cw2 = 2 * SSM_STATE
    assert seg % SCAN_BLOCK == 0
    pos = lax.broadcasted_iota(jnp.int32, (n_rows, cw2), 0) % seg
    row_blk = lax.broadcasted_iota(jnp.int32, (SCAN_BLOCK, cw2), 0)
    for gi in range(GROUP_BLOCK):
        xt = xt_ref[gi]
        at = at_ref[gi]
        s_all = lax.dot_general(xt, gt_ref[gi], TN_DIMS, preferred_element_type=F32)
        h_parts = []
        for dr in range(2):
            s = s_all[:, dr * cw2:(dr + 1) * cw2]
            p_tab = at[:, (2 * dr) * cw2:(2 * dr + 1) * cw2]
            q_tab = at[:, (2 * dr + 1) * cw2:(2 * dr + 2) * cw2]
            down = dr == 0
            if has_h0:
                h0 = jnp.zeros((n_rows, cw2), F32)
                riota = lax.broadcasted_iota(jnp.int32, (n_rows, cw2), 0)
                for b in range(n_seq):
                    edge = b * seg if down else b * seg + seg - 1
                    h0 = jnp.where(riota == edge, h0_ref[gi, b:b + 1, dr * cw2:(dr + 1) * cw2], h0)
                lv0 = SCAN_BLOCK
                s = (s + p_tab[lv0:lv0 + 1] * h0
                     + q_tab[lv0:lv0 + 1] * pltpu.roll(h0, SSM_STATE, axis=1))
            n_blk = n_rows // SCAN_BLOCK
            s3 = s.reshape(n_blk, SCAN_BLOCK, cw2)
            for lvl in range(SCAN_LEVELS):
                m = 2 ** lvl
                valid = (row_blk >= m) if down else (row_blk < SCAN_BLOCK - m)
                row = SCAN_BLOCK + lvl
                p = jnp.where(valid, p_tab[row:row + 1], 0.0)
                q = jnp.where(valid, q_tab[row:row + 1], 0.0)
                sh = pltpu.roll(s3, m if down else SCAN_BLOCK - m, axis=1)
                s3 = s3 + p * sh + q * pltpu.roll(sh, SSM_STATE, axis=2)
            sw3 = pltpu.roll(s3, SSM_STATE, axis=2)
            p_blk, q_blk = p_tab[0:SCAN_BLOCK], q_tab[0:SCAN_BLOCK]
            blocks = [s3[i] for i in range(n_blk)]
            blocks_w = [sw3[i] for i in range(n_blk)]
            blk_per_seq = seg // SCAN_BLOCK
            edge = slice(SCAN_BLOCK - 1, SCAN_BLOCK) if down else slice(0, 1)
            for q_i in range(n_seq):
                idxs = list(range(q_i * blk_per_seq, (q_i + 1) * blk_per_seq))
                idxs = idxs if down else idxs[::-1]
                for prev, cur in zip(idxs[:-1], idxs[1:]):
                    c = jnp.broadcast_to(blocks[prev][edge], (SCAN_BLOCK, cw2))
                    cw = jnp.broadcast_to(blocks_w[prev][edge], (SCAN_BLOCK, cw2))
                    blocks[cur] = blocks[cur] + p_blk * c + q_blk * cw
                    blocks_w[cur] = blocks_w[cur] + p_blk * cw - q_blk * c
            s = jnp.concatenate(blocks, axis=0)
            if not has_h0:
                fin_ref[dr] = s
                st_ref[gi, :, dr * cw2:(dr + 1) * cw2] = (
                    fin_ref[dr, pl.ds(seg - 1 if down else 0, n_seq, stride=seg), :])
            ent = _shift_rows(s, 1, down)
            ent = jnp.where((pos >= 1) if down else (pos < seg - 1), ent, 0.0)
            if has_h0:
                ent = ent + h0
            h_parts.append(ent)
        h_all = jnp.concatenate(h_parts, axis=1).astype(BF16)
        yt = (jnp.dot(mt_ref[gi], xt, preferred_element_type=F32)
              + lax.dot_general(wo_ref[gi], h_all, NT_DIMS, preferred_element_type=F32))
        z = jax.nn.gelu(yt, approximate=True)
        grp = step * GROUP_BLOCK + gi
        for t in range(CHUNK):
            zs_ref[t, pl.ds(pl.multiple_of(grp * SSM_GROUP, SSM_GROUP), SSM_GROUP), :] = (
                z[t * SSM_GROUP:(t + 1) * SSM_GROUP, :])

    @pl.when(step == pl.num_programs(0) - 1)
    def _():
        for t in range(CHUNK):
            zt = zs_ref[t]
            gate = jnp.dot(wg_ref[...], zt.astype(BF16),
                           preferred_element_type=F32) + bg_ref[...]
            z_ref[t] = (zt * jax.nn.sigmoid(gate)).T.astype(z_ref.dtype)


def _ssm(xt, mt, gt, wo, at, w_glu_t, b_glu_col, h0, n_seq):
    n_rows = xt.shape[-1]
    has_h0 = h0 is not None
    gb = GROUP_BLOCK
    w_spec = pl.BlockSpec((gb, CW, CW), lambda i: (i, 0, 0))
    in_specs = [pl.BlockSpec((gb, CW, n_rows), lambda i: (i, 0, 0)),
                w_spec, w_spec, w_spec,
                pl.BlockSpec((gb, AT_ROWS, 4 * 2 * SSM_STATE), lambda i: (i, 0, 0)),
                pl.BlockSpec((SSM_WIDTH, SSM_WIDTH), lambda i: (0, 0)),
                pl.BlockSpec((SSM_WIDTH, 1), lambda i: (0, 0))]
    args = [xt, mt, gt, wo, at, w_glu_t, b_glu_col]
    out_specs = [pl.BlockSpec((CHUNK, n_rows, SSM_WIDTH), lambda i: (0, 0, 0))]
    out_shape = [jax.ShapeDtypeStruct((CHUNK, n_rows, SSM_WIDTH), BF16)]
    scratch = [pltpu.VMEM((CHUNK, SSM_WIDTH, n_rows), F32)]
    if has_h0:
        in_specs.append(pl.BlockSpec((gb, n_seq, CW), lambda i: (i, 0, 0)))
        args.append(h0)
    else:
        out_specs.append(pl.BlockSpec((gb, n_seq, CW), lambda i: (i, 0, 0)))
        out_shape.append(jax.ShapeDtypeStruct((N_GROUPS, n_seq, CW), F32))
        scratch.append(pltpu.VMEM((2, n_rows, 2 * SSM_STATE), F32))
    return pl.pallas_call(
        functools.partial(_ssm_kernel, n_seq=n_seq, has_h0=has_h0),
        grid=(N_GROUPS // gb,),
        in_specs=in_specs,
        out_specs=out_specs,
        out_shape=out_shape,
        scratch_shapes=scratch,
        compiler_params=_cparams(1),
        name="ssm_scan_glu",
    )(*args)


FF_CHUNK = 256


def _out_ffn_kernel(x_ref, attn_ref, zt_ref, mod_ref, g_ref, wo_ref, wfi_ref, wfo_ref,
                    o_ref, mixz_ref, act_ref):
    j = pl.program_id(1)

    @pl.when(j == 0)
    def _():
        z_all = zt_ref[...].reshape(TOK_PER_TILE, SSM_WIDTH)
        part = jnp.dot(z_all, wo_ref[ATT_WIDTH:, :], preferred_element_type=F32)
        for t in range(CHUNK):
            for cb in range(N_COL_BLOCKS):
                mixz_ref[cb, pl.ds(t, ROWS_PER_TILE, stride=CHUNK), :] = (
                    part[t * ROWS_PER_TILE:(t + 1) * ROWS_PER_TILE, cb * LANES:(cb + 1) * LANES])

    gate1 = mod_ref[:, 2 * D_MODEL:3 * D_MODEL]
    shift2 = mod_ref[:, 3 * D_MODEL:4 * D_MODEL]
    scale2 = mod_ref[:, 4 * D_MODEL:5 * D_MODEL]
    gate2 = mod_ref[:, 5 * D_MODEL:6 * D_MODEL]
    half = SUB_TOK // 2
    gain2 = g_ref[2:3, :] * (1.0 + scale2)

    def pre_ffn(hf):
        r = slice(hf * half, (hf + 1) * half)
        rows = pl.ds(pl.multiple_of(j * SUB_TOK + hf * half, half), half)
        mix = (jnp.dot(attn_ref[r, :], wo_ref[0:ATT_WIDTH, :], preferred_element_type=F32)
               + jnp.concatenate([mixz_ref[cb, rows, :] for cb in range(N_COL_BLOCKS)], axis=1))
        x1 = x_ref[r, :] + gate1 * _rms(mix, g_ref[1:2, :])
        ms = jnp.mean(x1 * x1, axis=-1, keepdims=True)
        return x1, (x1 * lax.rsqrt(ms + NORM_EPS) * gain2 + shift2).astype(BF16)

    def ffn_in(hf, h, chunks):
        r = slice(hf * half, (hf + 1) * half)
        for c in chunks:
            lo = c * FF_CHUNK
            gt = jnp.dot(h, wfi_ref[:, lo:lo + FF_CHUNK], preferred_element_type=F32)
            up = jnp.dot(h, wfi_ref[:, D_FF + lo:D_FF + lo + FF_CHUNK],
                         preferred_element_type=F32)
            act_ref[r, lo:lo + FF_CHUNK] = (_silu(gt) * up).astype(BF16)

    def ffn_out(hf, x1):
        r = slice(hf * half, (hf + 1) * half)
        f = jnp.dot(act_ref[r, :], wfo_ref[...], preferred_element_type=F32)
        o_ref[r, :] = x1 + gate2 * _rms(f, g_ref[3:4, :])

    n_chunks = D_FF // FF_CHUNK
    x1_a, h_a = pre_ffn(0)
    x1_b, h_b = pre_ffn(1)
    ffn_in(0, h_a, range(n_chunks))
    ffn_out(0, x1_a)
    ffn_in(1, h_b, range(n_chunks))
    ffn_out(1, x1_b)


def _out_ffn(x2d, attn, zt, mods, mod_rows, norm_g, w_o, w_ffn_in, w_ffn_out):
    n_tok = x2d.shape[0]
    n_tiles = n_tok // TOK_PER_TILE
    mod_row0, n_mod = mod_rows
    tiles_per_mod = n_tiles // n_mod
    const = lambda i, j: (0, 0)
    row_spec = lambda w: pl.BlockSpec((SUB_TOK, w), lambda i, j: (i * N_SUB + j, 0))
    return pl.pallas_call(
        _out_ffn_kernel,
        grid=(n_tiles, N_SUB),
        in_specs=[row_spec(D_MODEL), row_spec(ATT_WIDTH),
                  pl.BlockSpec((CHUNK, ROWS_PER_TILE, SSM_WIDTH), lambda i, j: (0, i, 0)),
                  pl.BlockSpec((None, 1, N_MOD * D_MODEL),
                               lambda i, j: (mod_row0 + i // tiles_per_mod, 0, 0)),
                  pl.BlockSpec((4, D_MODEL), const),
                  pl.BlockSpec((2 * ATT_WIDTH, D_MODEL), const, pipeline_mode=pl.Buffered(1)),
                  pl.BlockSpec((D_MODEL, 2 * D_FF), const, pipeline_mode=pl.Buffered(1)),
                  pl.BlockSpec((D_FF, D_MODEL), const, pipeline_mode=pl.Buffered(1))],
        out_specs=row_spec(D_MODEL),
        out_shape=jax.ShapeDtypeStruct((n_tok, D_MODEL), F32),
        scratch_shapes=[pltpu.VMEM((N_COL_BLOCKS, TOK_PER_TILE, LANES), F32),
                        pltpu.VMEM((SUB_TOK, D_FF), BF16)],
        compiler_params=_cparams(2),
        name="out_proj_ffn",
    )(x2d, attn, zt, mods, norm_g, w_o, w_ffn_in, w_ffn_out)


def _rope_tables(seq_len):
    t = np.arange(seq_len)
    row = (t // GRID_W).astype(np.float32)
    col = (t % GRID_W).astype(np.float32)
    half = HEAD_DIM // 2
    inv_freq = (np.float32(ROPE_BASE)
                ** (-np.arange(0, half, 2, dtype=np.float32) / np.float32(half))).astype(np.float32)
    ang_r = row[:, None] * inv_freq
    ang_c = col[:, None] * inv_freq
    ang = np.concatenate([ang_r, ang_r, ang_c, ang_c], axis=-1)
    cos, sin = np.cos(ang), np.sin(ang)
    upper = (np.arange(HEAD_DIM) % 32) < 16
    sa = np.where(upper, -sin, 0.0)
    sb = np.where(upper, 0.0, sin)
    two = lambda a: jnp.asarray(np.concatenate([a, a], axis=-1), dtype=F32)
    return two(cos), two(sa), two(sb)


def _layer(x, mods, mod_rows, lam_init, rope_tabs, ctx_k, ctx_v, h0, weights, prep):
    n_batch, seq_len = x.shape[:2]
    g = weights['norm_g']
    outs = _in_proj(x, mods, mod_rows, g[0:1], weights['w_in'], rope_tabs)
    q, k, v = outs[:3]
    pending = () if 'late_bf16' in weights else weights['late_f32']
    attn, cast = _attention(q, k, v, ctx_k, ctx_v, weights['lam'], weights['subln_g'],
                            n_batch, seq_len, lam_init, cast_weights=pending)
    if pending:
        weights['late_bf16'] = cast
    ssm_out = _ssm(outs[-1], *prep, weights['w_glu_t'], weights['b_glu_col'], h0, n_batch)
    y = _out_ffn(x.reshape(n_batch * seq_len, D_MODEL), attn, ssm_out[0], mods, mod_rows, g,
                 *weights['late_bf16'])
    return y.reshape(x.shape), outs[3:-1], ssm_out[1:]


def kernel(x_prompt, x_sample, cache_k, cache_v, state_ssm_re, state_ssm_im, c, c_ctx, w_mod, b_mod, norm_g, w_in, lam_params, subln_g, ssm_lambda_re, ssm_lambda_im, ssm_log_step, ssm_b_re, ssm_b_im, ssm_c_re, ssm_c_im, ssm_d, w_glu, b_glu, w_o, w_ffn_in, w_ffn_out):
    depth = w_mod.shape[0]
    assert depth == 1
    bp = x_prompt.shape[0]
    bd, ld_len = x_sample.shape[:2]
    past = cache_k.shape[2]
    xp, xs = x_prompt, x_sample
    rope_tabs = _rope_tables(ld_len)
    ks_out, vs_out, hr_out, hi_out = [], [], [], []
    for l in range(depth):
        lam_init = 0.8 - 0.6 * math.exp(-0.3 * l)
        mods = _modulation(c_ctx, c, w_mod[l], b_mod[l])
        weights = {
            'norm_g': norm_g[l],
            'w_in': w_in[l].astype(BF16),
            'lam': lam_params[l], 'subln_g': subln_g[l],
            'w_glu_t': w_glu[l].T.astype(BF16), 'b_glu_col': b_glu[l].reshape(SSM_WIDTH, 1),
            'late_f32': (w_o[l], w_ffn_in[l], w_ffn_out[l]),
        }
        prep = _ssm_prep(ssm_lambda_re[l], ssm_lambda_im[l], ssm_log_step[l],
                         ssm_b_re[l], ssm_b_im[l], ssm_c_re[l], ssm_c_im[l], ssm_d[l])
        ck = cache_k[:, l].reshape(bd, past, ATT_WIDTH)
        cv = cache_v[:, l].reshape(bd, past, ATT_WIDTH)
        h0 = jnp.stack([state_ssm_re[:, l], state_ssm_im[:, l]], axis=2)
        h0 = h0.transpose(3, 0, 1, 2, 4).reshape(N_GROUPS, bd, CW)
        xs, _, _ = _layer(xs, mods, (1, bd), lam_init, rope_tabs, ck, cv, h0, weights, prep)
        xp, (k_ctx, v_ctx), (st,) = _layer(xp, mods, (0, 1), lam_init, None, None, None, None,
                                           weights, prep)
        ks_out.append(jnp.swapaxes(k_ctx, 1, 2).reshape(bp, -1, 2 * N_HEADS, HEAD_DIM))
        vs_out.append(v_ctx)
        fin = st.reshape(N_GROUPS, bp, 2, 2, SSM_STATE).transpose(1, 2, 3, 0, 4)
        hr_out.append(fin[:, :, 0])
        hi_out.append(fin[:, :, 1])
    return (xp, xs, jnp.stack(ks_out, axis=1), jnp.stack(vs_out, axis=1),
            jnp.stack(hr_out, axis=1), jnp.stack(hi_out, axis=1))
```

```python
import functools
import math

import jax
import jax.numpy as jnp
import numpy as np
from jax import lax
from jax.experimental import pallas as pl
from jax.experimental.pallas import tpu as pltpu

F32 = jnp.float32
BF16 = jnp.bfloat16

D_MODEL = 1024
GRID_W = 64
ATT_WIDTH = 512
SSM_WIDTH = 512
HEAD_DIM = 64
N_HEADS = 4
HEAD_W = 2 * HEAD_DIM
SSM_GROUP = 16
N_GROUPS = 32
SSM_STATE = 64
D_FF = 2816
N_MOD = 6
ROPE_BASE = 10000.0
NORM_EPS = 1e-6

CHUNK = 16
CW = CHUNK * SSM_GROUP
SCAN_BLOCK = 8
SCAN_LEVELS = 3
AT_ROWS = 16
GROUP_BLOCK = 4

ROWS_PER_TILE = 128
TOK_PER_TILE = ROWS_PER_TILE * CHUNK
SUB_TOK = 512
TQ_ITEM = 256
N_SUB = TOK_PER_TILE // SUB_TOK

VMEM_LIMIT = 56 * 1024 * 1024

NT_DIMS = (((1,), (1,)), ((), ()))
TN_DIMS = (((0,), (0,)), ((), ()))


def _cparams(n_axes):
    return pltpu.CompilerParams(
        dimension_semantics=("arbitrary",) * n_axes,
        vmem_limit_bytes=VMEM_LIMIT)


def _rms(x, g):
    ms = jnp.mean(x * x, axis=-1, keepdims=True)
    return x * lax.rsqrt(ms + NORM_EPS) * g


def _silu(x):
    return x * jax.nn.sigmoid(x)


MOD_ROWS = 8


def _mod_kernel(ctx_ref, c_ref, w_ref, b_ref, o_ref):
    n_lat, tk = c_ref.shape
    row = lax.broadcasted_iota(jnp.int32, (MOD_ROWS, tk), 0)
    cond = jnp.where(row == 0, ctx_ref[...], 0.0)
    for b in range(n_lat):
        cond = jnp.where(row == 1 + b, c_ref[b:b + 1, :], cond)
    part = jnp.dot(_silu(cond).astype(BF16), w_ref[...].astype(BF16),
                   preferred_element_type=F32)

    @pl.when(pl.program_id(0) == 0)
    def _():
        o_ref[:, 0, :] = part + b_ref[...]

    @pl.when(pl.program_id(0) > 0)
    def _():
        o_ref[:, 0, :] += part


def _modulation(c_ctx, c, w_mod, b_mod):
    n = w_mod.shape[1]
    tk = 256
    assert 1 + c.shape[0] <= MOD_ROWS
    return pl.pallas_call(
        _mod_kernel,
        grid=(D_MODEL // tk,),
        in_specs=[pl.BlockSpec((1, tk), lambda k: (0, k)),
                  pl.BlockSpec((c.shape[0], tk), lambda k: (0, k)),
                  pl.BlockSpec((tk, n), lambda k: (k, 0)),
                  pl.BlockSpec((1, n), lambda k: (0, 0))],
        out_specs=pl.BlockSpec((MOD_ROWS, 1, n), lambda k: (0, 0, 0)),
        out_shape=jax.ShapeDtypeStruct((MOD_ROWS, 1, n), F32),
        compiler_params=_cparams(1),
        name="modulation",
    )(c_ctx.reshape(1, D_MODEL), c, w_mod, b_mod.reshape(1, n))


def _rope(x, cos, sa, sb):
    return (x * cos + pltpu.roll(x, HEAD_W - 16, axis=1) * sa
            + pltpu.roll(x, 16, axis=1) * sb)


def _in_proj_kernel(*refs, rope, seqs_per_sub):
    x_ref, x3_hbm, mod_ref, g_ref, w_ref = refs[:5]
    refs = refs[5:]
    if rope:
        cos_ref, sa_ref, sb_ref = refs[:3]
        refs = refs[3:]
        q_ref, k_ref, v_ref, ut_ref, wut_ref, xt_ref, xt_sem = refs
    else:
        q_ref, k_ref, v_ref, kc_ref, vc_ref, ut_ref, wut_ref, xt_ref, xt_sem = refs
    tile = pl.program_id(0)
    j = pl.program_id(1)
    t_per_sub = CHUNK // N_SUB
    t_early = CHUNK - t_per_sub

    def gather(tile_idx, t):
        src = x3_hbm.at[pl.ds(tile_idx * ROWS_PER_TILE, ROWS_PER_TILE), t, :]
        return pltpu.make_async_copy(src, xt_ref.at[t], xt_sem.at[t])

    @pl.when(j == 0)
    def _():
        @pl.when(tile == 0)
        def _():
            for t in range(t_early):
                gather(0, t).start()
        for t in range(t_early, CHUNK):
            gather(tile, t).start()

    t_base = j * t_per_sub
    for d in range(t_per_sub):
        gather(tile, t_base + d).wait()

    @pl.when((j == N_SUB - 1) & (tile + 1 < pl.num_programs(0)))
    def _():
        for t in range(t_early):
            gather(tile + 1, t).start()

    @pl.when((tile == 0) & (j == 0))
    def _():
        wut_ref[...] = w_ref[:, 3 * ATT_WIDTH:].T

    shift = mod_ref[:, 0:D_MODEL]
    gain = g_ref[...] * (1.0 + mod_ref[:, D_MODEL:2 * D_MODEL])

    def norm_mod(xv):
        ms = jnp.mean(xv * xv, axis=-1, keepdims=True)
        return (xv * lax.rsqrt(ms + NORM_EPS) * gain + shift).astype(BF16)

    proj = jnp.dot(norm_mod(x_ref[...]), w_ref[:, 0:3 * ATT_WIDTH],
                   preferred_element_type=F32)
    q = proj[:, 0:ATT_WIDTH]
    k = proj[:, ATT_WIDTH:2 * ATT_WIDTH]
    v = proj[:, 2 * ATT_WIDTH:3 * ATT_WIDTH]
    qscale = HEAD_DIM ** -0.5 * math.log2(math.e)
    if rope:
        cos, sa, sb = cos_ref[...], sa_ref[...], sb_ref[...]
        for hd in range(N_HEADS):
            sl = slice(hd * HEAD_W, (hd + 1) * HEAD_W)
            q_ref[:, sl] = (_rope(q[:, sl], cos, sa, sb) * qscale).astype(q_ref.dtype)
            k_ref[:, sl] = _rope(k[:, sl], cos, sa, sb).astype(k_ref.dtype)
    else:
        q_ref[...] = (q * qscale).astype(q_ref.dtype)
        k_ref[...] = k.astype(k_ref.dtype)
        seq = SUB_TOK // seqs_per_sub
        k_t = k.T
        for b in range(seqs_per_sub):
            kc_ref[b] = k_t[:, b * seq:(b + 1) * seq]
            for hd in range(N_HEADS):
                vc_ref[b, :, hd, :] = v[b * seq:(b + 1) * seq, hd * HEAD_W:(hd + 1) * HEAD_W]
    v_ref[...] = v.astype(v_ref.dtype)

    for d0 in range(0, t_per_sub, 2):
        xt = jnp.concatenate([xt_ref[t_base + d0], xt_ref[t_base + d0 + 1]], axis=0)
        ut = lax.dot_general(wut_ref[...], norm_mod(xt), NT_DIMS,
                             preferred_element_type=F32)
        for d in range(2):
            blk = ut[:, d * ROWS_PER_TILE:(d + 1) * ROWS_PER_TILE]
            row0 = pl.multiple_of((t_base + d0 + d) * SSM_GROUP, SSM_GROUP)
            ut_ref[:, pl.ds(row0, SSM_GROUP), :] = (
                blk.reshape(N_GROUPS, SSM_GROUP, ROWS_PER_TILE).astype(ut_ref.dtype))


def _in_proj(x, mods, mod_rows, g0, w_in, rope_tabs):
    n_batch, seq_len = x.shape[:2]
    n_tok = n_batch * seq_len
    n_rows = n_tok // CHUNK
    n_tiles = n_tok // TOK_PER_TILE
    mod_row0, n_mod = mod_rows
    tiles_per_mod = n_tiles // n_mod
    rope = rope_tabs is not None
    seqs_per_sub = max(1, SUB_TOK // seq_len)
    in_specs = [pl.BlockSpec((SUB_TOK, D_MODEL), lambda i, j: (i * N_SUB + j, 0)),
                pl.BlockSpec(memory_space=pl.ANY),
                pl.BlockSpec((None, 1, 2 * D_MODEL),
                             lambda i, j: (mod_row0 + i // tiles_per_mod, 0, 0)),
                pl.BlockSpec((1, D_MODEL), lambda i, j: (0, 0)),
                pl.BlockSpec((D_MODEL, 4 * ATT_WIDTH), lambda i, j: (0, 0))]
    args = [x.reshape(n_tok, D_MODEL), x.reshape(n_rows, CHUNK, D_MODEL), mods, g0, w_in]
    row_spec = pl.BlockSpec((SUB_TOK, ATT_WIDTH), lambda i, j: (i * N_SUB + j, 0))
    row_shape = jax.ShapeDtypeStruct((n_tok, ATT_WIDTH), BF16)
    out_specs = [row_spec, row_spec, row_spec]
    out_shape = [row_shape, row_shape, row_shape]
    if rope:
        assert seq_len == TOK_PER_TILE
        for tab in rope_tabs:
            in_specs.append(pl.BlockSpec((SUB_TOK, HEAD_W), lambda i, j: (j, 0)))
            args.append(tab)
    else:
        out_specs += [pl.BlockSpec((seqs_per_sub, ATT_WIDTH, seq_len),
                                   lambda i, j: (i * N_SUB + j, 0, 0)),
                      pl.BlockSpec((seqs_per_sub, seq_len, N_HEADS, HEAD_W),
                                   lambda i, j: (i * N_SUB + j, 0, 0, 0))]
        out_shape += [jax.ShapeDtypeStruct((n_batch, ATT_WIDTH, seq_len), F32),
                      jax.ShapeDtypeStruct((n_batch, seq_len, N_HEADS, HEAD_W), F32)]
    out_specs.append(pl.BlockSpec((N_GROUPS, CW, ROWS_PER_TILE), lambda i, j: (0, 0, i)))
    out_shape.append(jax.ShapeDtypeStruct((N_GROUPS, CW, n_rows), BF16))
    return pl.pallas_call(
        functools.partial(_in_proj_kernel, rope=rope, seqs_per_sub=seqs_per_sub),
        grid=(n_tiles, N_SUB),
        in_specs=in_specs,
        out_specs=out_specs,
        out_shape=out_shape,
        scratch_shapes=[pltpu.VMEM((SSM_WIDTH, D_MODEL), BF16),
                        pltpu.VMEM((CHUNK, ROWS_PER_TILE, D_MODEL), F32),
                        pltpu.SemaphoreType.DMA((CHUNK,))],
        compiler_params=_cparams(2),
        name="in_proj",
    )(*args)


def _attn_kernel(*refs, has_ctx, lam_init, n_seq, seq_len, tq, n_cast):
    if n_cast:
        cast_in = refs[len(refs) - 2 * n_cast - 1:len(refs) - n_cast - 1]
        cast_out = refs[len(refs) - n_cast:]
        refs = refs[:len(refs) - 2 * n_cast - 1] + (refs[len(refs) - n_cast - 1],)
        for src, dst in zip(cast_in, cast_out):
            dst[...] = src[...].astype(dst.dtype)
    if has_ctx:
        lam_ref, sg_ref, q_ref, ck_ref, cv_ref, k_ref, v_ref, o_ref = refs
    else:
        lam_ref, sg_ref, q_ref, k_ref, v_ref, o_ref = refs
    lp = lam_ref[...]
    lam = (jnp.exp(jnp.sum(lp[0:1] * lp[1:2], axis=-1, keepdims=True))
           - jnp.exp(jnp.sum(lp[2:3] * lp[3:4], axis=-1, keepdims=True)) + lam_init)
    first_map = lax.broadcasted_iota(jnp.int32, (1, HEAD_W), 1) < HEAD_DIM
    ti = min(TQ_ITEM, tq)
    for row0 in range(0, n_seq * tq, ti):
        b = row0 // tq
        q_rows = slice(row0, row0 + ti)
        for hd in range(N_HEADS):
            sl = slice(hd * HEAD_W, (hd + 1) * HEAD_W)
            qh = q_ref[q_rows, sl]
            zero = jnp.zeros_like(qh)
            qs = jnp.concatenate([jnp.where(first_map, qh, zero),
                                  jnp.where(first_map, zero, qh)], axis=0)
            kv_rows = slice(b * seq_len, (b + 1) * seq_len)
            parts = [(k_ref[kv_rows, sl], v_ref[kv_rows, sl])]
            if has_ctx:
                parts.insert(0, (ck_ref[:, sl].astype(BF16), cv_ref[:, sl].astype(BF16)))
            scores = [lax.dot_general(qs, kk, NT_DIMS, preferred_element_type=F32)
                      for kk, _ in parts]
            mx = scores[0].max(axis=-1, keepdims=True)
            for s in scores[1:]:
                mx = jnp.maximum(mx, s.max(axis=-1, keepdims=True))
            acc = None
            for s, (_, vv) in zip(scores, parts):
                e = jnp.exp2(s - mx).astype(BF16)
                v_one = jnp.concatenate([vv, jnp.ones_like(vv)], axis=1)
                pv = jnp.dot(e, v_one, preferred_element_type=F32)
                acc = pv if acc is None else acc + pv
            num = acc[:, 0:HEAD_W] / acc[:, HEAD_W:2 * HEAD_W]
            o = num[0:ti] - lam * num[ti:2 * ti]
            o = _rms(o, sg_ref[...]) * (1.0 - lam_init)
            o_ref[q_rows, sl] = o.astype(o_ref.dtype)


def _attention(q, k, v, ctx_k, ctx_v, lam_params, subln_g, n_batch, seq_len, lam_init,
               cast_weights=()):
    has_ctx = ctx_k is not None
    tq = min(1024, seq_len)
    n_q = seq_len // tq
    n_seq = 1 if n_q > 1 else min(4, n_batch)
    in_specs = [pl.BlockSpec((4, HEAD_DIM), lambda b, i: (0, 0)),
                pl.BlockSpec((1, HEAD_W), lambda b, i: (0, 0)),
                pl.BlockSpec((n_seq * tq, ATT_WIDTH), lambda b, i: (b * n_q + i, 0))]
    args = [lam_params, subln_g.reshape(1, HEAD_W), q]
    if has_ctx:
        past = ctx_k.shape[1]
        ctx_spec = pl.BlockSpec((None, past, ATT_WIDTH), lambda b, i: (b, 0, 0))
        in_specs += [ctx_spec, ctx_spec]
        args += [ctx_k, ctx_v]
    kv_spec = pl.BlockSpec((n_seq * seq_len, ATT_WIDTH), lambda b, i: (b, 0))
    in_specs += [kv_spec, kv_spec]
    args += [k, v]
    out_specs = [pl.BlockSpec((n_seq * tq, ATT_WIDTH), lambda b, i: (b * n_q + i, 0))]
    out_shape = [jax.ShapeDtypeStruct((n_batch * seq_len, ATT_WIDTH), BF16)]
    n_steps = (n_batch // n_seq) * n_q
    for w in cast_weights:
        rows = w.shape[0] // n_steps
        spec = pl.BlockSpec((rows, w.shape[1]), lambda b, i: (b * n_q + i, 0))
        in_specs.append(spec)
        args.append(w)
        out_specs.append(spec)
        out_shape.append(jax.ShapeDtypeStruct(w.shape, BF16))
    outs = pl.pallas_call(
        functools.partial(_attn_kernel, has_ctx=has_ctx, lam_init=lam_init,
                          n_seq=n_seq, seq_len=seq_len, tq=tq, n_cast=len(cast_weights)),
        grid=(n_batch // n_seq, n_q),
        in_specs=in_specs,
        out_specs=out_specs,
        out_shape=out_shape,
        compiler_params=_cparams(2),
        name="diff_attention",
    )(*args)
    return outs[0], tuple(outs[1:])


def _cmul(ar, ai, br, bi):
    return ar * br - ai * bi, ar * bi + ai * br


def _ssm_prep_kernel(lre_ref, lim_ref, ls_ref, bre_ref, bim_ref, cre_ref, cim_ref, d_ref,
                     mt_ref, gt_ref, wo_ref, at_ref):
    lane = lax.broadcasted_iota(jnp.int32, (SSM_GROUP, CW), 1)
    chan = lax.broadcasted_iota(jnp.int32, (SSM_GROUP, CW), 0)
    for gi in range(GROUP_BLOCK):
        gt_cols, wo_cols, at_cols, toeplitz = [], [], [], []
        for dr in range(2):
            lr = jnp.minimum(lre_ref[dr, gi], -1e-4)
            li = lim_ref[dr, gi]
            step = jnp.exp(ls_ref[dr, gi])
            mag = jnp.exp(lr * step)
            a_re = mag * jnp.cos(li * step)
            a_im = mag * jnp.sin(li * step)
            den = lr * lr + li * li
            nr = a_re - 1.0
            f_re = (nr * lr + a_im * li) / den
            f_im = (a_im * lr - nr * li) / den
            bt_re, bt_im = bre_ref[dr, gi], bim_ref[dr, gi]
            bb_re, bb_im = _cmul(f_re, f_im, bt_re, bt_im)
            c_re, c_im = cre_ref[dr, gi], cim_ref[dr, gi]
            pw = [(jnp.ones_like(a_re), jnp.zeros_like(a_im))]
            for _ in range(CHUNK):
                pw.append(_cmul(pw[-1][0], pw[-1][1], a_re, a_im))
            g_re, g_im, e_re, e_im = [], [], [], []
            for t in range(CHUNK):
                pr, pi = pw[CHUNK - 1 - t] if dr == 0 else pw[t]
                r, i = _cmul(bb_re, bb_im, pr, pi)
                g_re.append(r)
                g_im.append(i)
                pr, pi = pw[t + 1] if dr == 0 else pw[CHUNK - t]
                r, i = _cmul(c_re, c_im, pr, pi)
                e_re.append(r)
                e_im.append(-i)
            g_cat = jnp.concatenate([jnp.concatenate(g_re, axis=0),
                                     jnp.concatenate(g_im, axis=0)], axis=1)
            gt_cols.append(g_cat)
            wo_cols.append(jnp.concatenate([jnp.concatenate(e_re, axis=0),
                                            jnp.concatenate(e_im, axis=0)], axis=1))
            c_cat = jnp.concatenate([c_re, -c_im], axis=1)
            toeplitz.append(lax.dot_general(c_cat, g_cat, NT_DIMS,
                                            precision=lax.Precision.HIGHEST,
                                            preferred_element_type=F32))
            apw = [pw[CHUNK]]
            for _ in range(SCAN_BLOCK - 1):
                apw.append(_cmul(apw[-1][0], apw[-1][1], apw[0][0], apw[0][1]))
            order = list(range(SCAN_BLOCK)) if dr == 0 else list(range(SCAN_BLOCK - 1, -1, -1))
            order += [2 ** l - 1 for l in range(SCAN_LEVELS)]
            order += [0] * (AT_ROWS - len(order))
            at_cols += [jnp.concatenate([jnp.concatenate([apw[i][0], apw[i][0]], axis=1)
                                         for i in order], axis=0),
                        jnp.concatenate([jnp.concatenate([-apw[i][1], apw[i][1]], axis=1)
                                         for i in order], axis=0)]
        kf_rev, kb = toeplitz
        d_skip = d_ref[gi]
        blocks = []
        for t in range(CHUNK):
            fwd = pltpu.roll(kf_rev, (CW - (CHUNK - 1 - t) * SSM_GROUP) % CW, axis=1)
            bwd = pltpu.roll(kb, t * SSM_GROUP, axis=1)
            blocks.append(jnp.where(lane < (t + 1) * SSM_GROUP, fwd, 0.0)
                          + jnp.where(lane >= t * SSM_GROUP, bwd, 0.0)
                          + jnp.where(lane == chan + t * SSM_GROUP, d_skip, 0.0))
        mt_ref[gi] = jnp.concatenate(blocks, axis=0).astype(mt_ref.dtype)
        gt_ref[gi] = jnp.concatenate(gt_cols, axis=1).astype(gt_ref.dtype)
        wo_ref[gi] = jnp.concatenate(wo_cols, axis=1).astype(wo_ref.dtype)
        at_ref[gi] = jnp.concatenate(at_cols, axis=1)


def _ssm_prep(lam_re, lam_im, log_step, b_re, b_im, c_re, c_im, d_skip):
    row = lambda a: a.reshape(2, N_GROUPS, 1, SSM_STATE)
    bt = lambda a: jnp.swapaxes(a, 2, 3)
    d_row = jnp.tile((d_skip[0] + d_skip[1]).reshape(N_GROUPS, 1, SSM_GROUP), (1, 1, CHUNK))
    gb = GROUP_BLOCK
    vec_spec = pl.BlockSpec((2, gb, 1, SSM_STATE), lambda i: (0, i, 0, 0))
    mat_spec = pl.BlockSpec((2, gb, SSM_GROUP, SSM_STATE), lambda i: (0, i, 0, 0))
    w_spec = pl.BlockSpec((gb, CW, CW), lambda i: (i, 0, 0))
    w_shape = jax.ShapeDtypeStruct((N_GROUPS, CW, CW), BF16)
    return pl.pallas_call(
        _ssm_prep_kernel,
        grid=(N_GROUPS // gb,),
        in_specs=[vec_spec, vec_spec,
                  pl.BlockSpec((2, gb, 1, 1), lambda i: (0, i, 0, 0)),
                  mat_spec, mat_spec, mat_spec, mat_spec,
                  pl.BlockSpec((gb, 1, CW), lambda i: (i, 0, 0))],
        out_specs=[w_spec, w_spec, w_spec,
                   pl.BlockSpec((gb, AT_ROWS, 4 * 2 * SSM_STATE), lambda i: (i, 0, 0))],
        out_shape=[w_shape, w_shape, w_shape,
                   jax.ShapeDtypeStruct((N_GROUPS, AT_ROWS, 4 * 2 * SSM_STATE), F32)],
        compiler_params=_cparams(1),
        name="ssm_prep",
    )(row(lam_re), row(lam_im), log_step.reshape(2, N_GROUPS, 1, 1),
      bt(b_re), bt(b_im), c_re, c_im, d_row)


def _shift_rows(x, m, down):
    n = x.shape[0]
    return pltpu.roll(x, m if down else n - m, axis=0)


def _ssm_kernel(*refs, n_seq, has_h0):
    if has_h0:
        (xt_ref, mt_ref, gt_ref, wo_ref, at_ref, wg_ref, bg_ref, h0_ref,
         z_hbm, zs_ref, zb_ref, zb_sem) = refs
    else:
        (xt_ref, mt_ref, gt_ref, wo_ref, at_ref, wg_ref, bg_ref,
         z_hbm, st_ref, zs_ref, zb_ref, zb_sem, fin_ref) = refs
    step = pl.program_id(0)
    n_rows = xt_ref.shape[-1]
    seg = n_rows // n_seq
    cw2 = 2 * SSM_STATE
    assert seg % SCAN_BLOCK == 0
    pos = lax.broadcasted_iota(jnp.int32, (n_rows, cw2), 0) % seg
    row_blk = lax.broadcasted_iota(jnp.int32, (SCAN_BLOCK, cw2), 0)
    for gi in range(GROUP_BLOCK):
        xt = xt_ref[gi]
        at = at_ref[gi]
        s_all = lax.dot_general(xt, gt_ref[gi], TN_DIMS, preferred_element_type=F32)
        h_parts = []
        for dr in range(2):
            s = s_all[:, dr * cw2:(dr + 1) * cw2]
            p_tab = at[:, (2 * dr) * cw2:(2 * dr + 1) * cw2]
            q_tab = at[:, (2 * dr + 1) * cw2:(2 * dr + 2) * cw2]
            down = dr == 0
            if has_h0:
                h0 = jnp.zeros((n_rows, cw2), F32)
                riota = lax.broadcasted_iota(jnp.int32, (n_rows, cw2), 0)
                for b in range(n_seq):
                    edge = b * seg if down else b * seg + seg - 1
                    h0 = jnp.where(riota == edge, h0_ref[gi, b:b + 1, dr * cw2:(dr + 1) * cw2], h0)
                lv0 = SCAN_BLOCK
                s = (s + p_tab[lv0:lv0 + 1] * h0
                     + q_tab[lv0:lv0 + 1] * pltpu.roll(h0, SSM_STATE, axis=1))
            n_blk = n_rows // SCAN_BLOCK
            s3 = s.reshape(n_blk, SCAN_BLOCK, cw2)
            for lvl in range(SCAN_LEVELS):
                m = 2 ** lvl
                valid = (row_blk >= m) if down else (row_blk < SCAN_BLOCK - m)
                row = SCAN_BLOCK + lvl
                p = jnp.where(valid, p_tab[row:row + 1], 0.0)
                q = jnp.where(valid, q_tab[row:row + 1], 0.0)
                sh = pltpu.roll(s3, m if down else SCAN_BLOCK - m, axis=1)
                s3 = s3 + p * sh + q * pltpu.roll(sh, SSM_STATE, axis=2)
            sw3 = pltpu.roll(s3, SSM_STATE, axis=2)
            p_blk, q_blk = p_tab[0:SCAN_BLOCK], q_tab[0:SCAN_BLOCK]
            blocks = [s3[i] for i in range(n_blk)]
            blocks_w = [sw3[i] for i in range(n_blk)]
            blk_per_seq = seg // SCAN_BLOCK
            edge = slice(SCAN_BLOCK - 1, SCAN_BLOCK) if down else slice(0, 1)
            for q_i in range(n_seq):
                idxs = list(range(q_i * blk_per_seq, (q_i + 1) * blk_per_seq))
                idxs = idxs if down else idxs[::-1]
                for prev, cur in zip(idxs[:-1], idxs[1:]):
                    c = jnp.broadcast_to(blocks[prev][edge], (SCAN_BLOCK, cw2))
                    cw = jnp.broadcast_to(blocks_w[prev][edge], (SCAN_BLOCK, cw2))
                    blocks[cur] = blocks[cur] + p_blk * c + q_blk * cw
                    blocks_w[cur] = blocks_w[cur] + p_blk * cw - q_blk * c
            s = jnp.concatenate(blocks, axis=0)
            if not has_h0:
                fin_ref[dr] = s
                st_ref[gi, :, dr * cw2:(dr + 1) * cw2] = (
                    fin_ref[dr, pl.ds(seg - 1 if down else 0, n_seq, stride=seg), :])
            ent = _shift_rows(s, 1, down)
            ent = jnp.where((pos >= 1) if down else (pos < seg - 1), ent, 0.0)
            if has_h0:
                ent = ent + h0
            h_parts.append(ent)
        h_all = jnp.concatenate(h_parts, axis=1).astype(BF16)
        yt = (jnp.dot(mt_ref[gi], xt, preferred_element_type=F32)
              + lax.dot_general(wo_ref[gi], h_all, NT_DIMS, preferred_element_type=F32))
        z = jax.nn.gelu(yt, approximate=True)
        grp = step * GROUP_BLOCK + gi
        for t in range(CHUNK):
            zs_ref[t, pl.ds(pl.multiple_of(grp * SSM_GROUP, SSM_GROUP), SSM_GROUP), :] = (
                z[t * SSM_GROUP:(t + 1) * SSM_GROUP, :])

    @pl.when(step == pl.num_programs(0) - 1)
    def _():
        def put(t):
            return pltpu.make_async_copy(zb_ref.at[t], z_hbm.at[:, t, :], zb_sem.at[t])

        for t in range(CHUNK):
            zt = zs_ref[t]
            gate = jnp.dot(wg_ref[...], zt.astype(BF16),
                           preferred_element_type=F32) + bg_ref[...]
            zb_ref[t] = (zt * jax.nn.sigmoid(gate)).T
            put(t).start()
        for t in range(CHUNK):
            put(t).wait()


def _ssm(xt, mt, gt, wo, at, w_glu_t, b_glu_col, h0, n_seq):
    n_rows = xt.shape[-1]
    has_h0 = h0 is not None
    gb = GROUP_BLOCK
    w_spec = pl.BlockSpec((gb, CW, CW), lambda i: (i, 0, 0))
    in_specs = [pl.BlockSpec((gb, CW, n_rows), lambda i: (i, 0, 0)),
                w_spec, w_spec, w_spec,
                pl.BlockSpec((gb, AT_ROWS, 4 * 2 * SSM_STATE), lambda i: (i, 0, 0)),
                pl.BlockSpec((SSM_WIDTH, SSM_WIDTH), lambda i: (0, 0)),
                pl.BlockSpec((SSM_WIDTH, 1), lambda i: (0, 0))]
    args = [xt, mt, gt, wo, at, w_glu_t, b_glu_col]
    out_specs = [pl.BlockSpec(memory_space=pl.ANY)]
    out_shape = [jax.ShapeDtypeStruct((n_rows, CHUNK, SSM_WIDTH), F32)]
    scratch = [pltpu.VMEM((CHUNK, SSM_WIDTH, n_rows), F32),
               pltpu.VMEM((CHUNK, n_rows, SSM_WIDTH), F32),
               pltpu.SemaphoreType.DMA((CHUNK,))]
    if has_h0:
        in_specs.append(pl.BlockSpec((gb, n_seq, CW), lambda i: (i, 0, 0)))
        args.append(h0)
    else:
        out_specs.append(pl.BlockSpec((gb, n_seq, CW), lambda i: (i, 0, 0)))
        out_shape.append(jax.ShapeDtypeStruct((N_GROUPS, n_seq, CW), F32))
        scratch.append(pltpu.VMEM((2, n_rows, 2 * SSM_STATE), F32))
    return pl.pallas_call(
        functools.partial(_ssm_kernel, n_seq=n_seq, has_h0=has_h0),
        grid=(N_GROUPS // gb,),
        in_specs=in_specs,
        out_specs=out_specs,
        out_shape=out_shape,
        scratch_shapes=scratch,
        compiler_params=_cparams(1),
        name="ssm_scan_glu",
    )(*args)


FF_CHUNK = 256


def _out_ffn_kernel(x_ref, attn_ref, z_ref, mod_ref, g_ref, wo_ref, wfi_ref, wfo_ref,
                    o_ref, act_ref):
    gate1 = mod_ref[:, 2 * D_MODEL:3 * D_MODEL]
    shift2 = mod_ref[:, 3 * D_MODEL:4 * D_MODEL]
    scale2 = mod_ref[:, 4 * D_MODEL:5 * D_MODEL]
    gate2 = mod_ref[:, 5 * D_MODEL:6 * D_MODEL]
    half = SUB_TOK // 2
    gain2 = g_ref[2:3, :] * (1.0 + scale2)

    def pre_ffn(hf):
        r = slice(hf * half, (hf + 1) * half)
        mixer = jnp.concatenate([attn_ref[r, :], z_ref[r, :].astype(BF16)], axis=1)
        mix = jnp.dot(mixer, wo_ref[...], preferred_element_type=F32)
        x1 = x_ref[r, :] + gate1 * _rms(mix, g_ref[1:2, :])
        ms = jnp.mean(x1 * x1, axis=-1, keepdims=True)
        return x1, (x1 * lax.rsqrt(ms + NORM_EPS) * gain2 + shift2).astype(BF16)

    def ffn_in(hf, h, chunks):
        r = slice(hf * half, (hf + 1) * half)
        for c in chunks:
            lo = c * FF_CHUNK
            gt = jnp.dot(h, wfi_ref[:, lo:lo + FF_CHUNK], preferred_element_type=F32)
            up = jnp.dot(h, wfi_ref[:, D_FF + lo:D_FF + lo + FF_CHUNK],
                         preferred_element_type=F32)
            act_ref[r, lo:lo + FF_CHUNK] = (_silu(gt) * up).astype(BF16)

    def ffn_out(hf, x1):
        r = slice(hf * half, (hf + 1) * half)
        f = jnp.dot(act_ref[r, :], wfo_ref[...], preferred_element_type=F32)
        o_ref[r, :] = x1 + gate2 * _rms(f, g_ref[3:4, :])

    n_chunks = D_FF // FF_CHUNK
    x1_a, h_a = pre_ffn(0)
    x1_b, h_b = pre_ffn(1)
    ffn_in(0, h_a, range(n_chunks))
    ffn_out(0, x1_a)
    ffn_in(1, h_b, range(n_chunks))
    ffn_out(1, x1_b)


def _out_ffn(x2d, attn, z, mods, mod_rows, norm_g, w_o, w_ffn_in, w_ffn_out):
    n_tok = x2d.shape[0]
    z2d = z.reshape(n_tok, SSM_WIDTH)
    n_tiles = n_tok // TOK_PER_TILE
    mod_row0, n_mod = mod_rows
    tiles_per_mod = n_tiles // n_mod
    const = lambda i, j: (0, 0)
    row_spec = lambda w: pl.BlockSpec((SUB_TOK, w), lambda i, j: (i * N_SUB + j, 0))
    return pl.pallas_call(
        _out_ffn_kernel,
        grid=(n_tiles, N_SUB),
        in_specs=[row_spec(D_MODEL), row_spec(ATT_WIDTH), row_spec(SSM_WIDTH),
                  pl.BlockSpec((None, 1, N_MOD * D_MODEL),
                               lambda i, j: (mod_row0 + i // tiles_per_mod, 0, 0)),
                  pl.BlockSpec((4, D_MODEL), const),
                  pl.BlockSpec((2 * ATT_WIDTH, D_MODEL), const, pipeline_mode=pl.Buffered(1)),
                  pl.BlockSpec((D_MODEL, 2 * D_FF), const, pipeline_mode=pl.Buffered(1)),
                  pl.BlockSpec((D_FF, D_MODEL), const, pipeline_mode=pl.Buffered(1))],
        out_specs=row_spec(D_MODEL),
        out_shape=jax.ShapeDtypeStruct((n_tok, D_MODEL), F32),
        scratch_shapes=[pltpu.VMEM((SUB_TOK, D_FF), BF16)],
        compiler_params=_cparams(2),
        name="out_proj_ffn",
    )(x2d, attn, z2d, mods, norm_g, w_o, w_ffn_in, w_ffn_out)


def _rope_tables(seq_len):
    t = np.arange(seq_len)
    row = (t // GRID_W).astype(np.float32)
    col = (t % GRID_W).astype(np.float32)
    half = HEAD_DIM // 2
    inv_freq = (np.float32(ROPE_BASE)
                ** (-np.arange(0, half, 2, dtype=np.float32) / np.float32(half))).astype(np.float32)
    ang_r = row[:, None] * inv_freq
    ang_c = col[:, None] * inv_freq
    ang = np.concatenate([ang_r, ang_r, ang_c, ang_c], axis=-1)
    cos, sin = np.cos(ang), np.sin(ang)
    upper = (np.arange(HEAD_DIM) % 32) < 16
    sa = np.where(upper, -sin, 0.0)
    sb = np.where(upper, 0.0, sin)
    two = lambda a: jnp.asarray(np.concatenate([a, a], axis=-1), dtype=F32)
    return two(cos), two(sa), two(sb)


def _layer(x, mods, mod_rows, lam_init, rope_tabs, ctx_k, ctx_v, h0, weights, prep):
    n_batch, seq_len = x.shape[:2]
    g = weights['norm_g']
    outs = _in_proj(x, mods, mod_rows, g[0:1], weights['w_in'], rope_tabs)
    q, k, v = outs[:3]
    pending = () if 'late_bf16' in weights else weights['late_f32']
    attn, cast = _attention(q, k, v, ctx_k, ctx_v, weights['lam'], weights['subln_g'],
                            n_batch, seq_len, lam_init, cast_weights=pending)
    if pending:
        weights['late_bf16'] = cast
    ssm_out = _ssm(outs[-1], *prep, weights['w_glu_t'], weights['b_glu_col'], h0, n_batch)
    y = _out_ffn(x.reshape(n_batch * seq_len, D_MODEL), attn, ssm_out[0], mods, mod_rows, g,
                 *weights['late_bf16'])
    return y.reshape(x.shape), outs[3:-1], ssm_out[1:]


def kernel(x_prompt, x_sample, cache_k, cache_v, state_ssm_re, state_ssm_im, c, c_ctx, w_mod, b_mod, norm_g, w_in, lam_params, subln_g, ssm_lambda_re, ssm_lambda_im, ssm_log_step, ssm_b_re, ssm_b_im, ssm_c_re, ssm_c_im, ssm_d, w_glu, b_glu, w_o, w_ffn_in, w_ffn_out):
    depth = w_mod.shape[0]
    assert depth == 1
    bp = x_prompt.shape[0]
    bd, ld_len = x_sample.shape[:2]
    past = cache_k.shape[2]
    xp, xs = x_prompt, x_sample
    rope_tabs = _rope_tables(ld_len)
    ks_out, vs_out, hr_out, hi_out = [], [], [], []
    for l in range(depth):
        lam_init = 0.8 - 0.6 * math.exp(-0.3 * l)
        mods = _modulation(c_ctx, c, w_mod[l], b_mod[l])
        weights = {
            'norm_g': norm_g[l],
            'w_in': w_in[l].astype(BF16),
            'lam': lam_params[l], 'subln_g': subln_g[l],
            'w_glu_t': w_glu[l].T.astype(BF16), 'b_glu_col': b_glu[l].reshape(SSM_WIDTH, 1),
            'late_f32': (w_o[l], w_ffn_in[l], w_ffn_out[l]),
        }
        prep = _ssm_prep(ssm_lambda_re[l], ssm_lambda_im[l], ssm_log_step[l],
                         ssm_b_re[l], ssm_b_im[l], ssm_c_re[l], ssm_c_im[l], ssm_d[l])
        ck = cache_k[:, l].reshape(bd, past, ATT_WIDTH)
        cv = cache_v[:, l].reshape(bd, past, ATT_WIDTH)
        h0 = jnp.stack([state_ssm_re[:, l], state_ssm_im[:, l]], axis=2)
        h0 = h0.transpose(3, 0, 1, 2, 4).reshape(N_GROUPS, bd, CW)
        xs, _, _ = _layer(xs, mods, (1, bd), lam_init, rope_tabs, ck, cv, h0, weights, prep)
        xp, (k_ctx, v_ctx), (st,) = _layer(xp, mods, (0, 1), lam_init, None, None, None, None,
                                           weights, prep)
        ks_out.append(jnp.swapaxes(k_ctx, 1, 2).reshape(bp, -1, 2 * N_HEADS, HEAD_DIM))
        vs_out.append(v_ctx)
        fin = st.reshape(N_GROUPS, bp, 2, 2, SSM_STATE).transpose(1, 2, 3, 0, 4)
        hr_out.append(fin[:, :, 0])
        hi_out.append(fin[:, :, 1])
    return (xp, xs, jnp.stack(ks_out, axis=1), jnp.stack(vs_out, axis=1),
            jnp.stack(hr_out, axis=1), jnp.stack(hi_out, axis=1))
```

```python
import functools
import math

import jax
import jax.numpy as jnp
import numpy as np
from jax import lax
from jax.experimental import pallas as pl
from jax.experimental.pallas import tpu as pltpu

F32 = jnp.float32
BF16 = jnp.bfloat16

D_MODEL = 1024
GRID_W = 64
ATT_WIDTH = 512
SSM_WIDTH = 512
HEAD_DIM = 64
N_HEADS = 4
HEAD_W = 2 * HEAD_DIM
SSM_GROUP = 16
N_GROUPS = 32
SSM_STATE = 64
D_FF = 2816
N_MOD = 6
ROPE_BASE = 10000.0
NORM_EPS = 1e-6

CHUNK = 16
CW = CHUNK * SSM_GROUP
SCAN_BLOCK = 8
SCAN_LEVELS = 3
AT_ROWS = 16
GROUP_BLOCK = 4

ROWS_PER_TILE = 128
TOK_PER_TILE = ROWS_PER_TILE * CHUNK
SUB_TOK = 512
TQ_ITEM = 256
N_SUB = TOK_PER_TILE // SUB_TOK

VMEM_LIMIT = 56 * 1024 * 1024

NT_DIMS = (((1,), (1,)), ((), ()))
TN_DIMS = (((0,), (0,)), ((), ()))


def _cparams(n_axes):
    return pltpu.CompilerParams(
        dimension_semantics=("arbitrary",) * n_axes,
        vmem_limit_bytes=VMEM_LIMIT)


def _rms(x, g):
    ms = jnp.mean(x * x, axis=-1, keepdims=True)
    return x * lax.rsqrt(ms + NORM_EPS) * g


def _silu(x):
    return x * jax.nn.sigmoid(x)


MOD_ROWS = 8


def _mod_kernel(ctx_ref, c_ref, w_ref, b_ref, o_ref):
    n_lat, tk = c_ref.shape
    row = lax.broadcasted_iota(jnp.int32, (MOD_ROWS, tk), 0)
    cond = jnp.where(row == 0, ctx_ref[...], 0.0)
    for b in range(n_lat):
        cond = jnp.where(row == 1 + b, c_ref[b:b + 1, :], cond)
    part = jnp.dot(_silu(cond).astype(BF16), w_ref[...].astype(BF16),
                   preferred_element_type=F32)

    @pl.when(pl.program_id(0) == 0)
    def _():
        o_ref[:, 0, :] = part + b_ref[...]

    @pl.when(pl.program_id(0) > 0)
    def _():
        o_ref[:, 0, :] += part


def _modulation(c_ctx, c, w_mod, b_mod):
    n = w_mod.shape[1]
    tk = 256
    assert 1 + c.shape[0] <= MOD_ROWS
    return pl.pallas_call(
        _mod_kernel,
        grid=(D_MODEL // tk,),
        in_specs=[pl.BlockSpec((1, tk), lambda k: (0, k)),
                  pl.BlockSpec((c.shape[0], tk), lambda k: (0, k)),
                  pl.BlockSpec((tk, n), lambda k: (k, 0)),
                  pl.BlockSpec((1, n), lambda k: (0, 0))],
        out_specs=pl.BlockSpec((MOD_ROWS, 1, n), lambda k: (0, 0, 0)),
        out_shape=jax.ShapeDtypeStruct((MOD_ROWS, 1, n), F32),
        compiler_params=_cparams(1),
        name="modulation",
    )(c_ctx.reshape(1, D_MODEL), c, w_mod, b_mod.reshape(1, n))


def _rope(x, cos, sa, sb):
    return (x * cos + pltpu.roll(x, HEAD_W - 16, axis=1) * sa
            + pltpu.roll(x, 16, axis=1) * sb)


def _in_proj_kernel(*refs, rope, seqs_per_sub):
    x_ref, x3_hbm, mod_ref, g_ref, w_ref = refs[:5]
    refs = refs[5:]
    if rope:
        cos_ref, sa_ref, sb_ref = refs[:3]
        refs = refs[3:]
        q_ref, k_ref, v_ref, ut_ref, wut_ref, xt_ref, xt_sem = refs
    else:
        q_ref, k_ref, v_ref, kc_ref, vc_ref, ut_ref, wut_ref, xt_ref, xt_sem = refs
    tile = pl.program_id(0)
    j = pl.program_id(1)
    t_per_sub = CHUNK // N_SUB
    t_early = CHUNK - t_per_sub

    def gather(tile_idx, t):
        src = x3_hbm.at[pl.ds(tile_idx * ROWS_PER_TILE, ROWS_PER_TILE), t, :]
        return pltpu.make_async_copy(src, xt_ref.at[t], xt_sem.at[t])

    @pl.when(j == 0)
    def _():
        @pl.when(tile == 0)
        def _():
            for t in range(t_early):
                gather(0, t).start()
        for t in range(t_early, CHUNK):
            gather(tile, t).start()

    t_base = j * t_per_sub
    for d in range(t_per_sub):
        gather(tile, t_base + d).wait()

    @pl.when((j == N_SUB - 1) & (tile + 1 < pl.num_programs(0)))
    def _():
        for t in range(t_early):
            gather(tile + 1, t).start()

    @pl.when((tile == 0) & (j == 0))
    def _():
        wut_ref[...] = w_ref[:, 3 * ATT_WIDTH:].T

    shift = mod_ref[:, 0:D_MODEL]
    gain = g_ref[...] * (1.0 + mod_ref[:, D_MODEL:2 * D_MODEL])

    def norm_mod(xv):
        ms = jnp.mean(xv * xv, axis=-1, keepdims=True)
        return (xv * lax.rsqrt(ms + NORM_EPS) * gain + shift).astype(BF16)

    proj = jnp.dot(norm_mod(x_ref[...]), w_ref[:, 0:3 * ATT_WIDTH],
                   preferred_element_type=F32)
    q = proj[:, 0:ATT_WIDTH]
    k = proj[:, ATT_WIDTH:2 * ATT_WIDTH]
    v = proj[:, 2 * ATT_WIDTH:3 * ATT_WIDTH]
    qscale = HEAD_DIM ** -0.5 * math.log2(math.e)
    if rope:
        cos, sa, sb = cos_ref[...], sa_ref[...], sb_ref[...]
        for hd in range(N_HEADS):
            sl = slice(hd * HEAD_W, (hd + 1) * HEAD_W)
            q_ref[:, sl] = (_rope(q[:, sl], cos, sa, sb) * qscale).astype(q_ref.dtype)
            k_ref[:, sl] = _rope(k[:, sl], cos, sa, sb).astype(k_ref.dtype)
    else:
        q_ref[...] = (q * qscale).astype(q_ref.dtype)
        k_ref[...] = k.astype(k_ref.dtype)
        seq = SUB_TOK // seqs_per_sub
        k_t = k.T
        for b in range(seqs_per_sub):
            kc_ref[b] = k_t[:, b * seq:(b + 1) * seq]
            for hd in range(N_HEADS):
                vc_ref[b, :, hd, :] = v[b * seq:(b + 1) * seq, hd * HEAD_W:(hd + 1) * HEAD_W]
    v_ref[...] = v.astype(v_ref.dtype)

    for d0 in range(0, t_per_sub, 2):
        xt = jnp.concatenate([xt_ref[t_base + d0], xt_ref[t_base + d0 + 1]], axis=0)
        ut = lax.dot_general(wut_ref[...], norm_mod(xt), NT_DIMS,
                             preferred_element_type=F32)
        for d in range(2):
            blk = ut[:, d * ROWS_PER_TILE:(d + 1) * ROWS_PER_TILE]
            row0 = pl.multiple_of((t_base + d0 + d) * SSM_GROUP, SSM_GROUP)
            ut_ref[:, pl.ds(row0, SSM_GROUP), :] = (
                blk.reshape(N_GROUPS, SSM_GROUP, ROWS_PER_TILE).astype(ut_ref.dtype))


def _in_proj(x, mods, mod_rows, g0, w_in, rope_tabs):
    n_batch, seq_len = x.shape[:2]
    n_tok = n_batch * seq_len
    n_rows = n_tok // CHUNK
    n_tiles = n_tok // TOK_PER_TILE
    mod_row0, n_mod = mod_rows
    tiles_per_mod = n_tiles // n_mod
    rope = rope_tabs is not None
    seqs_per_sub = max(1, SUB_TOK // seq_len)
    in_specs = [pl.BlockSpec((SUB_TOK, D_MODEL), lambda i, j: (i * N_SUB + j, 0)),
                pl.BlockSpec(memory_space=pl.ANY),
                pl.BlockSpec((None, 1, 2 * D_MODEL),
                             lambda i, j: (mod_row0 + i // tiles_per_mod, 0, 0)),
                pl.BlockSpec((1, D_MODEL), lambda i, j: (0, 0)),
                pl.BlockSpec((D_MODEL, 4 * ATT_WIDTH), lambda i, j: (0, 0))]
    args = [x.reshape(n_tok, D_MODEL), x.reshape(n_rows, CHUNK, D_MODEL), mods, g0, w_in]
    row_spec = pl.BlockSpec((SUB_TOK, ATT_WIDTH), lambda i, j: (i * N_SUB + j, 0))
    row_shape = jax.ShapeDtypeStruct((n_tok, ATT_WIDTH), BF16)
    out_specs = [row_spec, row_spec, row_spec]
    out_shape = [row_shape, row_shape, row_shape]
    if rope:
        assert seq_len == TOK_PER_TILE
        for tab in rope_tabs:
            in_specs.append(pl.BlockSpec((SUB_TOK, HEAD_W), lambda i, j: (j, 0)))
            args.append(tab)
    else:
        out_specs += [pl.BlockSpec((seqs_per_sub, ATT_WIDTH, seq_len),
                                   lambda i, j: (i * N_SUB + j, 0, 0)),
                      pl.BlockSpec((seqs_per_sub, seq_len, N_HEADS, HEAD_W),
                                   lambda i, j: (i * N_SUB + j, 0, 0, 0))]
        out_shape += [jax.ShapeDtypeStruct((n_batch, ATT_WIDTH, seq_len), F32),
                      jax.ShapeDtypeStruct((n_batch, seq_len, N_HEADS, HEAD_W), F32)]
    out_specs.append(pl.BlockSpec((N_GROUPS, CW, ROWS_PER_TILE), lambda i, j: (0, 0, i)))
    out_shape.append(jax.ShapeDtypeStruct((N_GROUPS, CW, n_rows), BF16))
    return pl.pallas_call(
        functools.partial(_in_proj_kernel, rope=rope, seqs_per_sub=seqs_per_sub),
        grid=(n_tiles, N_SUB),
        in_specs=in_specs,
        out_specs=out_specs,
        out_shape=out_shape,
        scratch_shapes=[pltpu.VMEM((SSM_WIDTH, D_MODEL), BF16),
                        pltpu.VMEM((CHUNK, ROWS_PER_TILE, D_MODEL), F32),
                        pltpu.SemaphoreType.DMA((CHUNK,))],
        compiler_params=_cparams(2),
        name="in_proj",
    )(*args)


def _attn_kernel(*refs, has_ctx, lam_init, n_seq, seq_len, tq, n_cast):
    if n_cast:
        cast_in = refs[len(refs) - 2 * n_cast - 1:len(refs) - n_cast - 1]
        cast_out = refs[len(refs) - n_cast:]
        refs = refs[:len(refs) - 2 * n_cast - 1] + (refs[len(refs) - n_cast - 1],)
        for src, dst in zip(cast_in, cast_out):
            dst[...] = src[...].astype(dst.dtype)
    if has_ctx:
        lam_ref, sg_ref, q_ref, ck_ref, cv_ref, k_ref, v_ref, o_ref = refs
    else:
        lam_ref, sg_ref, q_ref, k_ref, v_ref, o_ref = refs
    lp = lam_ref[...]
    lam = (jnp.exp(jnp.sum(lp[0:1] * lp[1:2], axis=-1, keepdims=True))
           - jnp.exp(jnp.sum(lp[2:3] * lp[3:4], axis=-1, keepdims=True)) + lam_init)
    first_map = lax.broadcasted_iota(jnp.int32, (1, HEAD_W), 1) < HEAD_DIM
    ti = min(TQ_ITEM, tq)
    for row0 in range(0, n_seq * tq, ti):
        b = row0 // tq
        q_rows = slice(row0, row0 + ti)
        for hd in range(N_HEADS):
            sl = slice(hd * HEAD_W, (hd + 1) * HEAD_W)
            qh = q_ref[q_rows, sl]
            zero = jnp.zeros_like(qh)
            qs = jnp.concatenate([jnp.where(first_map, qh, zero),
                                  jnp.where(first_map, zero, qh)], axis=0)
            kv_rows = slice(b * seq_len, (b + 1) * seq_len)
            parts = [(k_ref[kv_rows, sl], v_ref[kv_rows, sl])]
            if has_ctx:
                parts.insert(0, (ck_ref[:, sl].astype(BF16), cv_ref[:, sl].astype(BF16)))
            scores = [lax.dot_general(qs, kk, NT_DIMS, preferred_element_type=F32)
                      for kk, _ in parts]
            mx = scores[0].max(axis=-1, keepdims=True)
            for s in scores[1:]:
                mx = jnp.maximum(mx, s.max(axis=-1, keepdims=True))
            acc = None
            for s, (_, vv) in zip(scores, parts):
                e = jnp.exp2(s - mx).astype(BF16)
                v_one = jnp.concatenate([vv, jnp.ones_like(vv)], axis=1)
                pv = jnp.dot(e, v_one, preferred_element_type=F32)
                acc = pv if acc is None else acc + pv
            num = acc[:, 0:HEAD_W] / acc[:, HEAD_W:2 * HEAD_W]
            o = num[0:ti] - lam * num[ti:2 * ti]
            o = _rms(o, sg_ref[...]) * (1.0 - lam_init)
            o_ref[q_rows, sl] = o.astype(o_ref.dtype)


def _attention(q, k, v, ctx_k, ctx_v, lam_params, subln_g, n_batch, seq_len, lam_init,
               cast_weights=()):
    has_ctx = ctx_k is not None
    tq = min(1024, seq_len)
    n_q = seq_len // tq
    n_seq = 1 if n_q > 1 else min(4, n_batch)
    in_specs = [pl.BlockSpec((4, HEAD_DIM), lambda b, i: (0, 0)),
                pl.BlockSpec((1, HEAD_W), lambda b, i: (0, 0)),
                pl.BlockSpec((n_seq * tq, ATT_WIDTH), lambda b, i: (b * n_q + i, 0))]
    args = [lam_params, subln_g.reshape(1, HEAD_W), q]
    if has_ctx:
        past = ctx_k.shape[1]
        ctx_spec = pl.BlockSpec((None, past, ATT_WIDTH), lambda b, i: (b, 0, 0))
        in_specs += [ctx_spec, ctx_spec]
        args += [ctx_k, ctx_v]
    kv_spec = pl.BlockSpec((n_seq * seq_len, ATT_WIDTH), lambda b, i: (b, 0))
    in_specs += [kv_spec, kv_spec]
    args += [k, v]
    out_specs = [pl.BlockSpec((n_seq * tq, ATT_WIDTH), lambda b, i: (b * n_q + i, 0))]
    out_shape = [jax.ShapeDtypeStruct((n_batch * seq_len, ATT_WIDTH), BF16)]
    n_steps = (n_batch // n_seq) * n_q
    for w in cast_weights:
        rows = w.shape[0] // n_steps
        spec = pl.BlockSpec((rows, w.shape[1]), lambda b, i: (b * n_q + i, 0))
        in_specs.append(spec)
        args.append(w)
        out_specs.append(spec)
        out_shape.append(jax.ShapeDtypeStruct(w.shape, BF16))
    outs = pl.pallas_call(
        functools.partial(_attn_kernel, has_ctx=has_ctx, lam_init=lam_init,
                          n_seq=n_seq, seq_len=seq_len, tq=tq, n_cast=len(cast_weights)),
        grid=(n_batch // n_seq, n_q),
        in_specs=in_specs,
        out_specs=out_specs,
        out_shape=out_shape,
        compiler_params=_cparams(2),
        name="diff_attention",
    )(*args)
    return outs[0], tuple(outs[1:])


def _cmul(ar, ai, br, bi):
    return ar * br - ai * bi, ar * bi + ai * br


def _ssm_prep_kernel(lre_ref, lim_ref, ls_ref, bre_ref, bim_ref, cre_ref, cim_ref, d_ref,
                     mt_ref, gt_ref, wo_ref, at_ref):
    lane = lax.broadcasted_iota(jnp.int32, (SSM_GROUP, CW), 1)
    chan = lax.broadcasted_iota(jnp.int32, (SSM_GROUP, CW), 0)
    for gi in range(GROUP_BLOCK):
        gt_cols, wo_cols, at_cols, toeplitz = [], [], [], []
        for dr in range(2):
            lr = jnp.minimum(lre_ref[dr, gi], -1e-4)
            li = lim_ref[dr, gi]
            step = jnp.exp(ls_ref[dr, gi])
            mag = jnp.exp(lr * step)
            a_re = mag * jnp.cos(li * step)
            a_im = mag * jnp.sin(li * step)
            den = lr * lr + li * li
            nr = a_re - 1.0
            f_re = (nr * lr + a_im * li) / den
            f_im = (a_im * lr - nr * li) / den
            bt_re, bt_im = bre_ref[dr, gi], bim_ref[dr, gi]
            bb_re, bb_im = _cmul(f_re, f_im, bt_re, bt_im)
            c_re, c_im = cre_ref[dr, gi], cim_ref[dr, gi]
            pw = [(jnp.ones_like(a_re), jnp.zeros_like(a_im))]
            for _ in range(CHUNK):
                pw.append(_cmul(pw[-1][0], pw[-1][1], a_re, a_im))
            g_re, g_im, e_re, e_im = [], [], [], []
            for t in range(CHUNK):
                pr, pi = pw[CHUNK - 1 - t] if dr == 0 else pw[t]
                r, i = _cmul(bb_re, bb_im, pr, pi)
                g_re.append(r)
                g_im.append(i)
                pr, pi = pw[t + 1] if dr == 0 else pw[CHUNK - t]
                r, i = _cmul(c_re, c_im, pr, pi)
                e_re.append(r)
                e_im.append(-i)
            g_cat = jnp.concatenate([jnp.concatenate(g_re, axis=0),
                                     jnp.concatenate(g_im, axis=0)], axis=1)
            gt_cols.append(g_cat)
            wo_cols.append(jnp.concatenate([jnp.concatenate(e_re, axis=0),
                                            jnp.concatenate(e_im, axis=0)], axis=1))
            c_cat = jnp.concatenate([c_re, -c_im], axis=1)
            toeplitz.append(lax.dot_general(c_cat, g_cat, NT_DIMS,
                                            precision=lax.Precision.HIGHEST,
                                            preferred_element_type=F32))
            apw = [pw[CHUNK]]
            for _ in range(SCAN_BLOCK - 1):
                apw.append(_cmul(apw[-1][0], apw[-1][1], apw[0][0], apw[0][1]))
            order = list(range(SCAN_BLOCK)) if dr == 0 else list(range(SCAN_BLOCK - 1, -1, -1))
            order += [2 ** l - 1 for l in range(SCAN_LEVELS)]
            order += [0] * (AT_ROWS - len(order))
            at_cols += [jnp.concatenate([jnp.concatenate([apw[i][0], apw[i][0]], axis=1)
                                         for i in order], axis=0),
                        jnp.concatenate([jnp.concatenate([-apw[i][1], apw[i][1]], axis=1)
                                         for i in order], axis=0)]
        kf_rev, kb = toeplitz
        d_skip = d_ref[gi]
        blocks = []
        for t in range(CHUNK):
            fwd = pltpu.roll(kf_rev, (CW - (CHUNK - 1 - t) * SSM_GROUP) % CW, axis=1)
            bwd = pltpu.roll(kb, t * SSM_GROUP, axis=1)
            blocks.append(jnp.where(lane < (t + 1) * SSM_GROUP, fwd, 0.0)
                          + jnp.where(lane >= t * SSM_GROUP, bwd, 0.0)
                          + jnp.where(lane == chan + t * SSM_GROUP, d_skip, 0.0))
        mt_ref[gi] = jnp.concatenate(blocks, axis=0).astype(mt_ref.dtype)
        gt_ref[gi] = jnp.concatenate(gt_cols, axis=1).astype(gt_ref.dtype)
        wo_ref[gi] = jnp.concatenate(wo_cols, axis=1).astype(wo_ref.dtype)
        at_ref[gi] = jnp.concatenate(at_cols, axis=1)


def _ssm_prep(lam_re, lam_im, log_step, b_re, b_im, c_re, c_im, d_skip):
    row = lambda a: a.reshape(2, N_GROUPS, 1, SSM_STATE)
    bt = lambda a: jnp.swapaxes(a, 2, 3)
    d_row = jnp.tile((d_skip[0] + d_skip[1]).reshape(N_GROUPS, 1, SSM_GROUP), (1, 1, CHUNK))
    gb = GROUP_BLOCK
    vec_spec = pl.BlockSpec((2, gb, 1, SSM_STATE), lambda i: (0, i, 0, 0))
    mat_spec = pl.BlockSpec((2, gb, SSM_GROUP, SSM_STATE), lambda i: (0, i, 0, 0))
    w_spec = pl.BlockSpec((gb, CW, CW), lambda i: (i, 0, 0))
    w_shape = jax.ShapeDtypeStruct((N_GROUPS, CW, CW), BF16)
    return pl.pallas_call(
        _ssm_prep_kernel,
        grid=(N_GROUPS // gb,),
        in_specs=[vec_spec, vec_spec,
                  pl.BlockSpec((2, gb, 1, 1), lambda i: (0, i, 0, 0)),
                  mat_spec, mat_spec, mat_spec, mat_spec,
                  pl.BlockSpec((gb, 1, CW), lambda i: (i, 0, 0))],
        out_specs=[w_spec, w_spec, w_spec,
                   pl.BlockSpec((gb, AT_ROWS, 4 * 2 * SSM_STATE), lambda i: (i, 0, 0))],
        out_shape=[w_shape, w_shape, w_shape,
                   jax.ShapeDtypeStruct((N_GROUPS, AT_ROWS, 4 * 2 * SSM_STATE), F32)],
        compiler_params=_cparams(1),
        name="ssm_prep",
    )(row(lam_re), row(lam_im), log_step.reshape(2, N_GROUPS, 1, 1),
      bt(b_re), bt(b_im), c_re, c_im, d_row)


def _shift_rows(x, m, down):
    n = x.shape[0]
    return pltpu.roll(x, m if down else n - m, axis=0)


def _ssm_kernel(*refs, n_seq, has_h0):
    if has_h0:
        (xt_ref, mt_ref, gt_ref, wo_ref, at_ref, wg_ref, bg_ref, h0_ref,
         z_hbm, zs_ref, zb_ref, zb_sem) = refs
    else:
        (xt_ref, mt_ref, gt_ref, wo_ref, at_ref, wg_ref, bg_ref,
         z_hbm, st_ref, zs_ref, zb_ref, zb_sem, fin_ref) = refs
    step = pl.program_id(0)
    n_rows = xt_ref.shape[-1]
    seg = n_rows // n_seq
    cw2 = 2 * SSM_STATE
    assert seg % SCAN_BLOCK == 0
    pos = lax.broadcasted_iota(jnp.int32, (n_rows, cw2), 0) % seg
    row_blk = lax.broadcasted_iota(jnp.int32, (SCAN_BLOCK, cw2), 0)
    for gi in range(GROUP_BLOCK):
        xt = xt_ref[gi]
        at = at_ref[gi]
        s_all = lax.dot_general(xt, gt_ref[gi], TN_DIMS, preferred_element_type=F32)
        h_parts = []
        for dr in range(2):
            s = s_all[:, dr * cw2:(dr + 1) * cw2]
            p_tab = at[:, (2 * dr) * cw2:(2 * dr + 1) * cw2]
            q_tab = at[:, (2 * dr + 1) * cw2:(2 * dr + 2) * cw2]
            down = dr == 0
            if has_h0:
                h0 = jnp.zeros((n_rows, cw2), F32)
                riota = lax.broadcasted_iota(jnp.int32, (n_rows, cw2), 0)
                for b in range(n_seq):
                    edge = b * seg if down else b * seg + seg - 1
                    h0 = jnp.where(riota == edge, h0_ref[gi, b:b + 1, dr * cw2:(dr + 1) * cw2], h0)
                lv0 = SCAN_BLOCK
                s = (s + p_tab[lv0:lv0 + 1] * h0
                     + q_tab[lv0:lv0 + 1] * pltpu.roll(h0, SSM_STATE, axis=1))
            n_blk = n_rows // SCAN_BLOCK
            s3 = s.reshape(n_blk, SCAN_BLOCK, cw2)
            for lvl in range(SCAN_LEVELS):
                m = 2 ** lvl
                valid = (row_blk >= m) if down else (row_blk < SCAN_BLOCK - m)
                row = SCAN_BLOCK + lvl
                p = jnp.where(valid, p_tab[row:row + 1], 0.0)
                q = jnp.where(valid, q_tab[row:row + 1], 0.0)
                sh = pltpu.roll(s3, m if down else SCAN_BLOCK - m, axis=1)
                s3 = s3 + p * sh + q * pltpu.roll(sh, SSM_STATE, axis=2)
            sw3 = pltpu.roll(s3, SSM_STATE, axis=2)
            p_blk, q_blk = p_tab[0:SCAN_BLOCK], q_tab[0:SCAN_BLOCK]
            blocks = [s3[i] for i in range(n_blk)]
            blocks_w = [sw3[i] for i in range(n_blk)]
            blk_per_seq = seg // SCAN_BLOCK
            edge = slice(SCAN_BLOCK - 1, SCAN_BLOCK) if down else slice(0, 1)
            for q_i in range(n_seq):
                idxs = list(range(q_i * blk_per_seq, (q_i + 1) * blk_per_seq))
                idxs = idxs if down else idxs[::-1]
                for prev, cur in zip(idxs[:-1], idxs[1:]):
                    c = jnp.broadcast_to(blocks[prev][edge], (SCAN_BLOCK, cw2))
                    cw = jnp.broadcast_to(blocks_w[prev][edge], (SCAN_BLOCK, cw2))
                    blocks[cur] = blocks[cur] + p_blk * c + q_blk * cw
                    blocks_w[cur] = blocks_w[cur] + p_blk * cw - q_blk * c
            s = jnp.concatenate(blocks, axis=0)
            if not has_h0:
                fin_ref[dr] = s
                st_ref[gi, :, dr * cw2:(dr + 1) * cw2] = (
                    fin_ref[dr, pl.ds(seg - 1 if down else 0, n_seq, stride=seg), :])
            ent = _shift_rows(s, 1, down)
            ent = jnp.where((pos >= 1) if down else (pos < seg - 1), ent, 0.0)
            if has_h0:
                ent = ent + h0
            h_parts.append(ent)
        h_all = jnp.concatenate(h_parts, axis=1).astype(BF16)
        yt = (jnp.dot(mt_ref[gi], xt, preferred_element_type=F32)
              + lax.dot_general(wo_ref[gi], h_all, NT_DIMS, preferred_element_type=F32))
        z = jax.nn.gelu(yt, approximate=True)
        grp = step * GROUP_BLOCK + gi
        for t in range(CHUNK):
            zs_ref[t, pl.ds(pl.multiple_of(grp * SSM_GROUP, SSM_GROUP), SSM_GROUP), :] = (
                z[t * SSM_GROUP:(t + 1) * SSM_GROUP, :])

    @pl.when(step == pl.num_programs(0) - 1)
    def _():
        def put(t):
            return pltpu.make_async_copy(zb_ref.at[t], z_hbm.at[:, t, :], zb_sem.at[t])

        for t in range(CHUNK):
            zt = zs_ref[t]
            gate = jnp.dot(wg_ref[...], zt.astype(BF16),
                           preferred_element_type=F32) + bg_ref[...]
            zb_ref[t] = (zt * jax.nn.sigmoid(gate)).T
            put(t).start()
        for t in range(CHUNK):
            put(t).wait()


def _ssm(xt, mt, gt, wo, at, w_glu_t, b_glu_col, h0, n_seq):
    n_rows = xt.shape[-1]
    has_h0 = h0 is not None
    gb = GROUP_BLOCK
    w_spec = pl.BlockSpec((gb, CW, CW), lambda i: (i, 0, 0))
    in_specs = [pl.BlockSpec((gb, CW, n_rows), lambda i: (i, 0, 0)),
                w_spec, w_spec, w_spec,
                pl.BlockSpec((gb, AT_ROWS, 4 * 2 * SSM_STATE), lambda i: (i, 0, 0)),
                pl.BlockSpec((SSM_WIDTH, SSM_WIDTH), lambda i: (0, 0)),
                pl.BlockSpec((SSM_WIDTH, 1), lambda i: (0, 0))]
    args = [xt, mt, gt, wo, at, w_glu_t, b_glu_col]
    out_specs = [pl.BlockSpec(memory_space=pl.ANY)]
    out_shape = [jax.ShapeDtypeStruct((n_rows, CHUNK, SSM_WIDTH), F32)]
    scratch = [pltpu.VMEM((CHUNK, SSM_WIDTH, n_rows), F32),
               pltpu.VMEM((CHUNK, n_rows, SSM_WIDTH), F32),
               pltpu.SemaphoreType.DMA((CHUNK,))]
    if has_h0:
        in_specs.append(pl.BlockSpec((gb, n_seq, CW), lambda i: (i, 0, 0)))
        args.append(h0)
    else:
        out_specs.append(pl.BlockSpec((gb, n_seq, CW), lambda i: (i, 0, 0)))
        out_shape.append(jax.ShapeDtypeStruct((N_GROUPS, n_seq, CW), F32))
        scratch.append(pltpu.VMEM((2, n_rows, 2 * SSM_STATE), F32))
    return pl.pallas_call(
        functools.partial(_ssm_kernel, n_seq=n_seq, has_h0=has_h0),
        grid=(N_GROUPS // gb,),
        in_specs=in_specs,
        out_specs=out_specs,
        out_shape=out_shape,
        scratch_shapes=scratch,
        compiler_params=_cparams(1),
        name="ssm_scan_glu",
    )(*args)


FF_CHUNK = 256
FFN_TOK = 1024


def _out_ffn_kernel(x_ref, attn_ref, z_ref, mod_ref, g_ref, wo_ref, wfi_ref, wfo_ref,
                    o_ref, act_ref):
    gate1 = mod_ref[:, 2 * D_MODEL:3 * D_MODEL]
    shift2 = mod_ref[:, 3 * D_MODEL:4 * D_MODEL]
    scale2 = mod_ref[:, 4 * D_MODEL:5 * D_MODEL]
    gate2 = mod_ref[:, 5 * D_MODEL:6 * D_MODEL]
    half = FFN_TOK // 2
    gain2 = g_ref[2:3, :] * (1.0 + scale2)

    def pre_ffn(hf):
        r = slice(hf * half, (hf + 1) * half)
        mixer = jnp.concatenate([attn_ref[r, :], z_ref[r, :].astype(BF16)], axis=1)
        mix = jnp.dot(mixer, wo_ref[...], preferred_element_type=F32)
        x1 = x_ref[r, :] + gate1 * _rms(mix, g_ref[1:2, :])
        ms = jnp.mean(x1 * x1, axis=-1, keepdims=True)
        return x1, (x1 * lax.rsqrt(ms + NORM_EPS) * gain2 + shift2).astype(BF16)

    def ffn_in(hf, h, chunks):
        r = slice(hf * half, (hf + 1) * half)
        for c in chunks:
            lo = c * FF_CHUNK
            gt = jnp.dot(h, wfi_ref[:, lo:lo + FF_CHUNK], preferred_element_type=F32)
            up = jnp.dot(h, wfi_ref[:, D_FF + lo:D_FF + lo + FF_CHUNK],
                         preferred_element_type=F32)
            act_ref[r, lo:lo + FF_CHUNK] = (_silu(gt) * up).astype(BF16)

    def ffn_out(hf, x1):
        r = slice(hf * half, (hf + 1) * half)
        f = jnp.dot(act_ref[r, :], wfo_ref[...], preferred_element_type=F32)
        o_ref[r, :] = x1 + gate2 * _rms(f, g_ref[3:4, :])

    n_chunks = D_FF // FF_CHUNK
    x1_a, h_a = pre_ffn(0)
    x1_b, h_b = pre_ffn(1)
    ffn_in(0, h_a, range(n_chunks))
    ffn_out(0, x1_a)
    ffn_in(1, h_b, range(n_chunks))
    ffn_out(1, x1_b)


def _out_ffn(x2d, attn, z, mods, mod_rows, norm_g, w_o, w_ffn_in, w_ffn_out):
    n_tok = x2d.shape[0]
    z2d = z.reshape(n_tok, SSM_WIDTH)
    n_steps = n_tok // FFN_TOK
    mod_row0, n_mod = mod_rows
    steps_per_mod = n_steps // n_mod
    const = lambda i: (0, 0)
    row_spec = lambda w: pl.BlockSpec((FFN_TOK, w), lambda i: (i, 0))
    return pl.pallas_call(
        _out_ffn_kernel,
        grid=(n_steps,),
        in_specs=[row_spec(D_MODEL), row_spec(ATT_WIDTH), row_spec(SSM_WIDTH),
                  pl.BlockSpec((None, 1, N_MOD * D_MODEL),
                               lambda i: (mod_row0 + i // steps_per_mod, 0, 0)),
                  pl.BlockSpec((4, D_MODEL), const),
                  pl.BlockSpec((2 * ATT_WIDTH, D_MODEL), const, pipeline_mode=pl.Buffered(1)),
                  pl.BlockSpec((D_MODEL, 2 * D_FF), const, pipeline_mode=pl.Buffered(1)),
                  pl.BlockSpec((D_FF, D_MODEL), const, pipeline_mode=pl.Buffered(1))],
        out_specs=row_spec(D_MODEL),
        out_shape=jax.ShapeDtypeStruct((n_tok, D_MODEL), F32),
        scratch_shapes=[pltpu.VMEM((FFN_TOK, D_FF), BF16)],
        compiler_params=_cparams(1),
        name="out_proj_ffn",
    )(x2d, attn, z2d, mods, norm_g, w_o, w_ffn_in, w_ffn_out)


def _rope_tables(seq_len):
    t = np.arange(seq_len)
    row = (t // GRID_W).astype(np.float32)
    col = (t % GRID_W).astype(np.float32)
    half = HEAD_DIM // 2
    inv_freq = (np.float32(ROPE_BASE)
                ** (-np.arange(0, half, 2, dtype=np.float32) / np.float32(half))).astype(np.float32)
    ang_r = row[:, None] * inv_freq
    ang_c = col[:, None] * inv_freq
    ang = np.concatenate([ang_r, ang_r, ang_c, ang_c], axis=-1)
    cos, sin = np.cos(ang), np.sin(ang)
    upper = (np.arange(HEAD_DIM) % 32) < 16
    sa = np.where(upper, -sin, 0.0)
    sb = np.where(upper, 0.0, sin)
    two = lambda a: jnp.asarray(np.concatenate([a, a], axis=-1), dtype=F32)
    return two(cos), two(sa), two(sb)


def _layer(x, mods, mod_rows, lam_init, rope_tabs, ctx_k, ctx_v, h0, weights, prep):
    n_batch, seq_len = x.shape[:2]
    g = weights['norm_g']
    outs = _in_proj(x, mods, mod_rows, g[0:1], weights['w_in'], rope_tabs)
    q, k, v = outs[:3]
    pending = () if 'late_bf16' in weights else weights['late_f32']
    attn, cast = _attention(q, k, v, ctx_k, ctx_v, weights['lam'], weights['subln_g'],
                            n_batch, seq_len, lam_init, cast_weights=pending)
    if pending:
        weights['late_bf16'] = cast
    ssm_out = _ssm(outs[-1], *prep, weights['w_glu_t'], weights['b_glu_col'], h0, n_batch)
    y = _out_ffn(x.reshape(n_batch * seq_len, D_MODEL), attn, ssm_out[0], mods, mod_rows, g,
                 *weights['late_bf16'])
    return y.reshape(x.shape), outs[3:-1], ssm_out[1:]


def kernel(x_prompt, x_sample, cache_k, cache_v, state_ssm_re, state_ssm_im, c, c_ctx, w_mod, b_mod, norm_g, w_in, lam_params, subln_g, ssm_lambda_re, ssm_lambda_im, ssm_log_step, ssm_b_re, ssm_b_im, ssm_c_re, ssm_c_im, ssm_d, w_glu, b_glu, w_o, w_ffn_in, w_ffn_out):
    depth = w_mod.shape[0]
    assert depth == 1
    bp = x_prompt.shape[0]
    bd, ld_len = x_sample.shape[:2]
    past = cache_k.shape[2]
    xp, xs = x_prompt, x_sample
    rope_tabs = _rope_tables(ld_len)
    ks_out, vs_out, hr_out, hi_out = [], [], [], []
    for l in range(depth):
        lam_init = 0.8 - 0.6 * math.exp(-0.3 * l)
        mods = _modulation(c_ctx, c, w_mod[l], b_mod[l])
        weights = {
            'norm_g': norm_g[l],
            'w_in': w_in[l].astype(BF16),
            'lam': lam_params[l], 'subln_g': subln_g[l],
            'w_glu_t': w_glu[l].T.astype(BF16), 'b_glu_col': b_glu[l].reshape(SSM_WIDTH, 1),
            'late_f32': (w_o[l], w_ffn_in[l], w_ffn_out[l]),
        }
        prep = _ssm_prep(ssm_lambda_re[l], ssm_lambda_im[l], ssm_log_step[l],
                         ssm_b_re[l], ssm_b_im[l], ssm_c_re[l], ssm_c_im[l], ssm_d[l])
        ck = cache_k[:, l].reshape(bd, past, ATT_WIDTH)
        cv = cache_v[:, l].reshape(bd, past, ATT_WIDTH)
        h0 = jnp.stack([state_ssm_re[:, l], state_ssm_im[:, l]], axis=2)
        h0 = h0.transpose(3, 0, 1, 2, 4).reshape(N_GROUPS, bd, CW)
        xs, _, _ = _layer(xs, mods, (1, bd), lam_init, rope_tabs, ck, cv, h0, weights, prep)
        xp, (k_ctx, v_ctx), (st,) = _layer(xp, mods, (0, 1), lam_init, None, None, None, None,
                                           weights, prep)
        ks_out.append(jnp.swapaxes(k_ctx, 1, 2).reshape(bp, -1, 2 * N_HEADS, HEAD_DIM))
        vs_out.append(v_ctx)
        fin = st.reshape(N_GROUPS, bp, 2, 2, SSM_STATE).transpose(1, 2, 3, 0, 4)
        hr_out.append(fin[:, :, 0])
        hi_out.append(fin[:, :, 1])
    return (xp, xs, jnp.stack(ks_out, axis=1), jnp.stack(vs_out, axis=1),
            jnp.stack(hr_out, axis=1), jnp.stack(hi_out, axis=1))
```

```python
import functools
import math

import jax
import jax.numpy as jnp
import numpy as np
from jax import lax
from jax.experimental import pallas as pl
from jax.experimental.pallas import tpu as pltpu

F32 = jnp.float32
BF16 = jnp.bfloat16

D_MODEL = 1024
GRID_W = 64
ATT_WIDTH = 512
SSM_WIDTH = 512
HEAD_DIM = 64
N_HEADS = 4
HEAD_W = 2 * HEAD_DIM
SSM_GROUP = 16
N_GROUPS = 32
SSM_STATE = 64
D_FF = 2816
N_MOD = 6
ROPE_BASE = 10000.0
NORM_EPS = 1e-6

CHUNK = 16
CW = CHUNK * SSM_GROUP
SCAN_BLOCK = 8
SCAN_LEVELS = 3
AT_ROWS = 16
GROUP_BLOCK = 4
Z_BATCH = 8

ROWS_PER_TILE = 128
TOK_PER_TILE = ROWS_PER_TILE * CHUNK
SUB_TOK = 512
TQ_ITEM = 256
N_SUB = TOK_PER_TILE // SUB_TOK

VMEM_LIMIT = 56 * 1024 * 1024

NT_DIMS = (((1,), (1,)), ((), ()))
TN_DIMS = (((0,), (0,)), ((), ()))


def _cparams(n_axes):
    return pltpu.CompilerParams(
        dimension_semantics=("arbitrary",) * n_axes,
        vmem_limit_bytes=VMEM_LIMIT)


def _rms(x, g):
    ms = jnp.mean(x * x, axis=-1, keepdims=True)
    return x * lax.rsqrt(ms + NORM_EPS) * g


def _silu(x):
    return x * jax.nn.sigmoid(x)


MOD_ROWS = 8


def _mod_kernel(ctx_ref, c_ref, w_ref, b_ref, o_ref):
    n_lat, tk = c_ref.shape
    row = lax.broadcasted_iota(jnp.int32, (MOD_ROWS, tk), 0)
    cond = jnp.where(row == 0, ctx_ref[...], 0.0)
    for b in range(n_lat):
        cond = jnp.where(row == 1 + b, c_ref[b:b + 1, :], cond)
    part = jnp.dot(_silu(cond).astype(BF16), w_ref[...].astype(BF16),
                   preferred_element_type=F32)

    @pl.when(pl.program_id(0) == 0)
    def _():
        o_ref[:, 0, :] = part + b_ref[...]

    @pl.when(pl.program_id(0) > 0)
    def _():
        o_ref[:, 0, :] += part


def _modulation(c_ctx, c, w_mod, b_mod):
    n = w_mod.shape[1]
    tk = 256
    assert 1 + c.shape[0] <= MOD_ROWS
    return pl.pallas_call(
        _mod_kernel,
        grid=(D_MODEL // tk,),
        in_specs=[pl.BlockSpec((1, tk), lambda k: (0, k)),
                  pl.BlockSpec((c.shape[0], tk), lambda k: (0, k)),
                  pl.BlockSpec((tk, n), lambda k: (k, 0)),
                  pl.BlockSpec((1, n), lambda k: (0, 0))],
        out_specs=pl.BlockSpec((MOD_ROWS, 1, n), lambda k: (0, 0, 0)),
        out_shape=jax.ShapeDtypeStruct((MOD_ROWS, 1, n), F32),
        compiler_params=_cparams(1),
        name="modulation",
    )(c_ctx.reshape(1, D_MODEL), c, w_mod, b_mod.reshape(1, n))


def _rope(x, cos, sa, sb):
    return (x * cos + pltpu.roll(x, HEAD_W - 16, axis=1) * sa
            + pltpu.roll(x, 16, axis=1) * sb)


def _in_proj_kernel(*refs, rope, seqs_per_sub):
    x_ref, x3_hbm, mod_ref, g_ref, w_ref = refs[:5]
    refs = refs[5:]
    if rope:
        cos_ref, sa_ref, sb_ref = refs[:3]
        refs = refs[3:]
        q_ref, k_ref, v_ref, ut_ref, wut_ref, xt_ref, xt_sem = refs
    else:
        q_ref, k_ref, v_ref, kc_ref, vc_ref, ut_ref, wut_ref, xt_ref, xt_sem = refs
    tile = pl.program_id(0)
    j = pl.program_id(1)
    t_per_sub = CHUNK // N_SUB
    t_early = CHUNK - t_per_sub

    def gather(tile_idx, t):
        src = x3_hbm.at[pl.ds(tile_idx * ROWS_PER_TILE, ROWS_PER_TILE), t, :]
        return pltpu.make_async_copy(src, xt_ref.at[t], xt_sem.at[t])

    @pl.when(j == 0)
    def _():
        @pl.when(tile == 0)
        def _():
            for t in range(t_early):
                gather(0, t).start()
        for t in range(t_early, CHUNK):
            gather(tile, t).start()

    t_base = j * t_per_sub
    for d in range(t_per_sub):
        gather(tile, t_base + d).wait()

    @pl.when((j == N_SUB - 1) & (tile + 1 < pl.num_programs(0)))
    def _():
        for t in range(t_early):
            gather(tile + 1, t).start()

    @pl.when((tile == 0) & (j == 0))
    def _():
        wut_ref[...] = w_ref[:, 3 * ATT_WIDTH:].T

    shift = mod_ref[:, 0:D_MODEL]
    gain = g_ref[...] * (1.0 + mod_ref[:, D_MODEL:2 * D_MODEL])

    def norm_mod(xv):
        ms = jnp.mean(xv * xv, axis=-1, keepdims=True)
        return (xv * lax.rsqrt(ms + NORM_EPS) * gain + shift).astype(BF16)

    def ssm_input(d0):
        xt = jnp.concatenate([xt_ref[t_base + d0], xt_ref[t_base + d0 + 1]], axis=0)
        ut = lax.dot_general(wut_ref[...], norm_mod(xt), NT_DIMS,
                             preferred_element_type=F32)
        for d in range(2):
            blk = ut[:, d * ROWS_PER_TILE:(d + 1) * ROWS_PER_TILE]
            row0 = pl.multiple_of((t_base + d0 + d) * SSM_GROUP, SSM_GROUP)
            ut_ref[:, pl.ds(row0, SSM_GROUP), :] = (
                blk.reshape(N_GROUPS, SSM_GROUP, ROWS_PER_TILE).astype(ut_ref.dtype))

    proj = jnp.dot(norm_mod(x_ref[...]), w_ref[:, 0:3 * ATT_WIDTH],
                   preferred_element_type=F32)
    q = proj[:, 0:ATT_WIDTH]
    k = proj[:, ATT_WIDTH:2 * ATT_WIDTH]
    v = proj[:, 2 * ATT_WIDTH:3 * ATT_WIDTH]
    qscale = HEAD_DIM ** -0.5 * math.log2(math.e)
    if rope:
        cos, sa, sb = cos_ref[...], sa_ref[...], sb_ref[...]
        for hd in range(N_HEADS):
            sl = slice(hd * HEAD_W, (hd + 1) * HEAD_W)
            q_ref[:, sl] = (_rope(q[:, sl], cos, sa, sb) * qscale).astype(q_ref.dtype)
            k_ref[:, sl] = _rope(k[:, sl], cos, sa, sb).astype(k_ref.dtype)
    else:
        q_ref[...] = (q * qscale).astype(q_ref.dtype)
        k_ref[...] = k.astype(k_ref.dtype)
        seq = SUB_TOK // seqs_per_sub
        k_t = k.T
        for b in range(seqs_per_sub):
            kc_ref[b] = k_t[:, b * seq:(b + 1) * seq]
            for hd in range(N_HEADS):
                vc_ref[b, :, hd, :] = v[b * seq:(b + 1) * seq, hd * HEAD_W:(hd + 1) * HEAD_W]
    v_ref[...] = v.astype(v_ref.dtype)

    for d0 in range(0, t_per_sub, 2):
        ssm_input(d0)


def _in_proj(x, mods, mod_rows, g0, w_in, rope_tabs):
    n_batch, seq_len = x.shape[:2]
    n_tok = n_batch * seq_len
    n_rows = n_tok // CHUNK
    n_tiles = n_tok // TOK_PER_TILE
    mod_row0, n_mod = mod_rows
    tiles_per_mod = n_tiles // n_mod
    rope = rope_tabs is not None
    seqs_per_sub = max(1, SUB_TOK // seq_len)
    in_specs = [pl.BlockSpec((SUB_TOK, D_MODEL), lambda i, j: (i * N_SUB + j, 0)),
                pl.BlockSpec(memory_space=pl.ANY),
                pl.BlockSpec((None, 1, 2 * D_MODEL),
                             lambda i, j: (mod_row0 + i // tiles_per_mod, 0, 0)),
                pl.BlockSpec((1, D_MODEL), lambda i, j: (0, 0)),
                pl.BlockSpec((D_MODEL, 4 * ATT_WIDTH), lambda i, j: (0, 0))]
    args = [x.reshape(n_tok, D_MODEL), x.reshape(n_rows, CHUNK, D_MODEL), mods, g0, w_in]
    row_spec = pl.BlockSpec((SUB_TOK, ATT_WIDTH), lambda i, j: (i * N_SUB + j, 0))
    row_shape = jax.ShapeDtypeStruct((n_tok, ATT_WIDTH), BF16)
    out_specs = [row_spec, row_spec, row_spec]
    out_shape = [row_shape, row_shape, row_shape]
    if rope:
        assert seq_len == TOK_PER_TILE
        for tab in rope_tabs:
            in_specs.append(pl.BlockSpec((SUB_TOK, HEAD_W), lambda i, j: (j, 0)))
            args.append(tab)
    else:
        out_specs += [pl.BlockSpec((seqs_per_sub, ATT_WIDTH, seq_len),
                                   lambda i, j: (i * N_SUB + j, 0, 0)),
                      pl.BlockSpec((seqs_per_sub, seq_len, N_HEADS, HEAD_W),
                                   lambda i, j: (i * N_SUB + j, 0, 0, 0))]
        out_shape += [jax.ShapeDtypeStruct((n_batch, ATT_WIDTH, seq_len), F32),
                      jax.ShapeDtypeStruct((n_batch, seq_len, N_HEADS, HEAD_W), F32)]
    out_specs.append(pl.BlockSpec((N_GROUPS, CW, ROWS_PER_TILE), lambda i, j: (0, 0, i)))
    out_shape.append(jax.ShapeDtypeStruct((N_GROUPS, CW, n_rows), BF16))
    return pl.pallas_call(
        functools.partial(_in_proj_kernel, rope=rope, seqs_per_sub=seqs_per_sub),
        grid=(n_tiles, N_SUB),
        in_specs=in_specs,
        out_specs=out_specs,
        out_shape=out_shape,
        scratch_shapes=[pltpu.VMEM((SSM_WIDTH, D_MODEL), BF16),
                        pltpu.VMEM((CHUNK, ROWS_PER_TILE, D_MODEL), F32),
                        pltpu.SemaphoreType.DMA((CHUNK,))],
        compiler_params=_cparams(2),
        name="in_proj",
    )(*args)


def _attn_kernel(*refs, has_ctx, lam_init, n_seq, seq_len, tq, n_cast):
    if n_cast:
        cast_in = refs[len(refs) - 2 * n_cast - 1:len(refs) - n_cast - 1]
        cast_out = refs[len(refs) - n_cast:]
        refs = refs[:len(refs) - 2 * n_cast - 1] + (refs[len(refs) - n_cast - 1],)
        for src, dst in zip(cast_in, cast_out):
            dst[...] = src[...].astype(dst.dtype)
    if has_ctx:
        lam_ref, sg_ref, q_ref, ck_ref, cv_ref, k_ref, v_ref, o_ref = refs
    else:
        lam_ref, sg_ref, q_ref, k_ref, v_ref, o_ref = refs
    lp = lam_ref[...]
    lam = (jnp.exp(jnp.sum(lp[0:1] * lp[1:2], axis=-1, keepdims=True))
           - jnp.exp(jnp.sum(lp[2:3] * lp[3:4], axis=-1, keepdims=True)) + lam_init)
    first_map = lax.broadcasted_iota(jnp.int32, (1, HEAD_W), 1) < HEAD_DIM
    ti = min(TQ_ITEM, tq)
    for row0 in range(0, n_seq * tq, ti):
        b = row0 // tq
        q_rows = slice(row0, row0 + ti)
        for hd in range(N_HEADS):
            sl = slice(hd * HEAD_W, (hd + 1) * HEAD_W)
            qh = q_ref[q_rows, sl]
            zero = jnp.zeros_like(qh)
            qs = jnp.concatenate([jnp.where(first_map, qh, zero),
                                  jnp.where(first_map, zero, qh)], axis=0)
            kv_rows = slice(b * seq_len, (b + 1) * seq_len)
            parts = [(k_ref[kv_rows, sl], v_ref[kv_rows, sl])]
            if has_ctx:
                parts.insert(0, (ck_ref[:, sl].astype(BF16), cv_ref[:, sl].astype(BF16)))
            scores = [lax.dot_general(qs, kk, NT_DIMS, preferred_element_type=F32)
                      for kk, _ in parts]
            mx = scores[0].max(axis=-1, keepdims=True)
            for s in scores[1:]:
                mx = jnp.maximum(mx, s.max(axis=-1, keepdims=True))
            acc = None
            for s, (_, vv) in zip(scores, parts):
                e = jnp.exp2(s - mx).astype(BF16)
                v_one = jnp.concatenate([vv, jnp.ones_like(vv)], axis=1)
                pv = jnp.dot(e, v_one, preferred_element_type=F32)
                acc = pv if acc is None else acc + pv
            num = acc[:, 0:HEAD_W] / acc[:, HEAD_W:2 * HEAD_W]
            o = num[0:ti] - lam * num[ti:2 * ti]
            o = _rms(o, sg_ref[...]) * (1.0 - lam_init)
            o_ref[q_rows, sl] = o.astype(o_ref.dtype)


def _attention(q, k, v, ctx_k, ctx_v, lam_params, subln_g, n_batch, seq_len, lam_init,
               cast_weights=()):
    has_ctx = ctx_k is not None
    tq = min(1024, seq_len)
    n_q = seq_len // tq
    n_seq = 1 if n_q > 1 else min(4, n_batch)
    in_specs = [pl.BlockSpec((4, HEAD_DIM), lambda b, i: (0, 0)),
                pl.BlockSpec((1, HEAD_W), lambda b, i: (0, 0)),
                pl.BlockSpec((n_seq * tq, ATT_WIDTH), lambda b, i: (b * n_q + i, 0))]
    args = [lam_params, subln_g.reshape(1, HEAD_W), q]
    if has_ctx:
        past = ctx_k.shape[1]
        ctx_spec = pl.BlockSpec((None, past, ATT_WIDTH), lambda b, i: (b, 0, 0))
        in_specs += [ctx_spec, ctx_spec]
        args += [ctx_k, ctx_v]
    kv_spec = pl.BlockSpec((n_seq * seq_len, ATT_WIDTH), lambda b, i: (b, 0))
    in_specs += [kv_spec, kv_spec]
    args += [k, v]
    out_specs = [pl.BlockSpec((n_seq * tq, ATT_WIDTH), lambda b, i: (b * n_q + i, 0))]
    out_shape = [jax.ShapeDtypeStruct((n_batch * seq_len, ATT_WIDTH), BF16)]
    n_steps = (n_batch // n_seq) * n_q
    for w in cast_weights:
        rows = w.shape[0] // n_steps
        spec = pl.BlockSpec((rows, w.shape[1]), lambda b, i: (b * n_q + i, 0))
        in_specs.append(spec)
        args.append(w)
        out_specs.append(spec)
        out_shape.append(jax.ShapeDtypeStruct(w.shape, BF16))
    outs = pl.pallas_call(
        functools.partial(_attn_kernel, has_ctx=has_ctx, lam_init=lam_init,
                          n_seq=n_seq, seq_len=seq_len, tq=tq, n_cast=len(cast_weights)),
        grid=(n_batch // n_seq, n_q),
        in_specs=in_specs,
        out_specs=out_specs,
        out_shape=out_shape,
        compiler_params=_cparams(2),
        name="diff_attention",
    )(*args)
    return outs[0], tuple(outs[1:])


def _cmul(ar, ai, br, bi):
    return ar * br - ai * bi, ar * bi + ai * br


def _ssm_prep_kernel(lre_ref, lim_ref, ls_ref, bre_ref, bim_ref, cre_ref, cim_ref, d_ref,
                     mt_ref, gt_ref, wo_ref, at_ref):
    lane = lax.broadcasted_iota(jnp.int32, (SSM_GROUP, CW), 1)
    chan = lax.broadcasted_iota(jnp.int32, (SSM_GROUP, CW), 0)
    for gi in range(GROUP_BLOCK):
        gt_cols, wo_cols, at_cols, toeplitz = [], [], [], []
        for dr in range(2):
            lr = jnp.minimum(lre_ref[dr, gi], -1e-4)
            li = lim_ref[dr, gi]
            step = jnp.exp(ls_ref[dr, gi])
            mag = jnp.exp(lr * step)
            a_re = mag * jnp.cos(li * step)
            a_im = mag * jnp.sin(li * step)
            den = lr * lr + li * li
            nr = a_re - 1.0
            f_re = (nr * lr + a_im * li) / den
            f_im = (a_im * lr - nr * li) / den
            bt_re, bt_im = bre_ref[dr, gi], bim_ref[dr, gi]
            bb_re, bb_im = _cmul(f_re, f_im, bt_re, bt_im)
            c_re, c_im = cre_ref[dr, gi], cim_ref[dr, gi]
            pw = [(jnp.ones_like(a_re), jnp.zeros_like(a_im))]
            for _ in range(CHUNK):
                pw.append(_cmul(pw[-1][0], pw[-1][1], a_re, a_im))
            g_re, g_im, e_re, e_im = [], [], [], []
            for t in range(CHUNK):
                pr, pi = pw[CHUNK - 1 - t] if dr == 0 else pw[t]
                r, i = _cmul(bb_re, bb_im, pr, pi)
                g_re.append(r)
                g_im.append(i)
                pr, pi = pw[t + 1] if dr == 0 else pw[CHUNK - t]
                r, i = _cmul(c_re, c_im, pr, pi)
                e_re.append(r)
                e_im.append(-i)
            g_cat = jnp.concatenate([jnp.concatenate(g_re, axis=0),
                                     jnp.concatenate(g_im, axis=0)], axis=1)
            gt_cols.append(g_cat)
            wo_cols.append(jnp.concatenate([jnp.concatenate(e_re, axis=0),
                                            jnp.concatenate(e_im, axis=0)], axis=1))
            c_cat = jnp.concatenate([c_re, -c_im], axis=1)
            toeplitz.append(lax.dot_general(c_cat, g_cat, NT_DIMS,
                                            precision=lax.Precision.HIGHEST,
                                            preferred_element_type=F32))
            apw = [pw[CHUNK]]
            for _ in range(SCAN_BLOCK - 1):
                apw.append(_cmul(apw[-1][0], apw[-1][1], apw[0][0], apw[0][1]))
            order = list(range(SCAN_BLOCK)) if dr == 0 else list(range(SCAN_BLOCK - 1, -1, -1))
            order += [2 ** l - 1 for l in range(SCAN_LEVELS)]
            order += [0] * (AT_ROWS - len(order))
            at_cols += [jnp.concatenate([jnp.concatenate([apw[i][0], apw[i][0]], axis=1)
                                         for i in order], axis=0),
                        jnp.concatenate([jnp.concatenate([-apw[i][1], apw[i][1]], axis=1)
                                         for i in order], axis=0)]
        kf_rev, kb = toeplitz
        d_skip = d_ref[gi]
        blocks = []
        for t in range(CHUNK):
            fwd = pltpu.roll(kf_rev, (CW - (CHUNK - 1 - t) * SSM_GROUP) % CW, axis=1)
            bwd = pltpu.roll(kb, t * SSM_GROUP, axis=1)
            blocks.append(jnp.where(lane < (t + 1) * SSM_GROUP, fwd, 0.0)
                          + jnp.where(lane >= t * SSM_GROUP, bwd, 0.0)
                          + jnp.where(lane == chan + t * SSM_GROUP, d_skip, 0.0))
        mt_ref[gi] = jnp.concatenate(blocks, axis=0).astype(mt_ref.dtype)
        gt_ref[gi] = jnp.concatenate(gt_cols, axis=1).astype(gt_ref.dtype)
        wo_ref[gi] = jnp.concatenate(wo_cols, axis=1).astype(wo_ref.dtype)
        at_ref[gi] = jnp.concatenate(at_cols, axis=1)


def _ssm_prep(lam_re, lam_im, log_step, b_re, b_im, c_re, c_im, d_skip):
    row = lambda a: a.reshape(2, N_GROUPS, 1, SSM_STATE)
    bt = lambda a: jnp.swapaxes(a, 2, 3)
    d_row = jnp.tile((d_skip[0] + d_skip[1]).reshape(N_GROUPS, 1, SSM_GROUP), (1, 1, CHUNK))
    gb = GROUP_BLOCK
    vec_spec = pl.BlockSpec((2, gb, 1, SSM_STATE), lambda i: (0, i, 0, 0))
    mat_spec = pl.BlockSpec((2, gb, SSM_GROUP, SSM_STATE), lambda i: (0, i, 0, 0))
    w_spec = pl.BlockSpec((gb, CW, CW), lambda i: (i, 0, 0))
    w_shape = jax.ShapeDtypeStruct((N_GROUPS, CW, CW), BF16)
    return pl.pallas_call(
        _ssm_prep_kernel,
        grid=(N_GROUPS // gb,),
        in_specs=[vec_spec, vec_spec,
                  pl.BlockSpec((2, gb, 1, 1), lambda i: (0, i, 0, 0)),
                  mat_spec, mat_spec, mat_spec, mat_spec,
                  pl.BlockSpec((gb, 1, CW), lambda i: (i, 0, 0))],
        out_specs=[w_spec, w_spec, w_spec,
                   pl.BlockSpec((gb, AT_ROWS, 4 * 2 * SSM_STATE), lambda i: (i, 0, 0))],
        out_shape=[w_shape, w_shape, w_shape,
                   jax.ShapeDtypeStruct((N_GROUPS, AT_ROWS, 4 * 2 * SSM_STATE), F32)],
        compiler_params=_cparams(1),
        name="ssm_prep",
    )(row(lam_re), row(lam_im), log_step.reshape(2, N_GROUPS, 1, 1),
      bt(b_re), bt(b_im), c_re, c_im, d_row)


def _shift_rows(x, m, down):
    n = x.shape[0]
    return pltpu.roll(x, m if down else n - m, axis=0)


def _ssm_kernel(*refs, n_seq, has_h0):
    if has_h0:
        (xt_ref, mt_ref, gt_ref, wo_ref, at_ref, wg_ref, bg_ref, h0_ref,
         z_hbm, zs_ref, zb_ref, zb_sem) = refs
    else:
        (xt_ref, mt_ref, gt_ref, wo_ref, at_ref, wg_ref, bg_ref,
         z_hbm, st_ref, zs_ref, zb_ref, zb_sem, fin_ref) = refs
    step = pl.program_id(0)
    n_rows = xt_ref.shape[-1]
    seg = n_rows // n_seq
    cw2 = 2 * SSM_STATE
    assert seg % SCAN_BLOCK == 0
    pos = lax.broadcasted_iota(jnp.int32, (n_rows, cw2), 0) % seg
    row_blk = lax.broadcasted_iota(jnp.int32, (SCAN_BLOCK, cw2), 0)
    for gi in range(GROUP_BLOCK):
        xt = xt_ref[gi]
        at = at_ref[gi]
        s_all = lax.dot_general(xt, gt_ref[gi], TN_DIMS, preferred_element_type=F32)
        h_parts = []
        for dr in range(2):
            s = s_all[:, dr * cw2:(dr + 1) * cw2]
            p_tab = at[:, (2 * dr) * cw2:(2 * dr + 1) * cw2]
            q_tab = at[:, (2 * dr + 1) * cw2:(2 * dr + 2) * cw2]
            down = dr == 0
            n_blk = n_rows // SCAN_BLOCK
            s3 = s.reshape(n_blk, SCAN_BLOCK, cw2)
            for lvl in range(SCAN_LEVELS):
                m = 2 ** lvl
                valid = (row_blk >= m) if down else (row_blk < SCAN_BLOCK - m)
                row = SCAN_BLOCK + lvl
                p = jnp.where(valid, p_tab[row:row + 1], 0.0)
                q = jnp.where(valid, q_tab[row:row + 1], 0.0)
                sh = pltpu.roll(s3, m if down else SCAN_BLOCK - m, axis=1)
                s3 = s3 + p * sh + q * pltpu.roll(sh, SSM_STATE, axis=2)
            sw3 = pltpu.roll(s3, SSM_STATE, axis=2)
            p_blk, q_blk = p_tab[0:SCAN_BLOCK], q_tab[0:SCAN_BLOCK]
            blocks = [s3[i] for i in range(n_blk)]
            blocks_w = [sw3[i] for i in range(n_blk)]
            blk_per_seq = seg // SCAN_BLOCK
            edge = slice(SCAN_BLOCK - 1, SCAN_BLOCK) if down else slice(0, 1)
            for q_i in range(n_seq):
                idxs = list(range(q_i * blk_per_seq, (q_i + 1) * blk_per_seq))
                idxs = idxs if down else idxs[::-1]
                carries = [None] if not has_h0 else [
                    (h0_ref[gi, q_i:q_i + 1, dr * cw2:(dr + 1) * cw2], None)]
                for prev, cur in zip(carries + idxs[:-1], idxs):
                    if prev is None:
                        continue
                    if isinstance(prev, tuple):
                        c_row = prev[0]
                        cw_row = pltpu.roll(c_row, SSM_STATE, axis=1)
                    else:
                        c_row, cw_row = blocks[prev][edge], blocks_w[prev][edge]
                    c = jnp.broadcast_to(c_row, (SCAN_BLOCK, cw2))
                    cw = jnp.broadcast_to(cw_row, (SCAN_BLOCK, cw2))
                    blocks[cur] = blocks[cur] + p_blk * c + q_blk * cw
                    blocks_w[cur] = blocks_w[cur] + p_blk * cw - q_blk * c
            s = jnp.concatenate(blocks, axis=0)
            if not has_h0:
                fin_ref[dr] = s
                st_ref[gi, :, dr * cw2:(dr + 1) * cw2] = (
                    fin_ref[dr, pl.ds(seg - 1 if down else 0, n_seq, stride=seg), :])
            ent = _shift_rows(s, 1, down)
            ent = jnp.where((pos >= 1) if down else (pos < seg - 1), ent, 0.0)
            if has_h0:
                ent_blk = [ent[i * SCAN_BLOCK:(i + 1) * SCAN_BLOCK] for i in range(n_blk)]
                for q_i in range(n_seq):
                    bi = q_i * blk_per_seq if down else (q_i + 1) * blk_per_seq - 1
                    at_edge = row_blk == (0 if down else SCAN_BLOCK - 1)
                    ent_blk[bi] = jnp.where(
                        at_edge, h0_ref[gi, q_i:q_i + 1, dr * cw2:(dr + 1) * cw2], ent_blk[bi])
                ent = jnp.concatenate(ent_blk, axis=0)
            h_parts.append(ent)
        h_all = jnp.concatenate(h_parts, axis=1).astype(BF16)
        yt = (jnp.dot(mt_ref[gi], xt, preferred_element_type=F32)
              + lax.dot_general(wo_ref[gi], h_all, NT_DIMS, preferred_element_type=F32))
        z = jax.nn.gelu(yt, approximate=True)
        grp = step * GROUP_BLOCK + gi
        for t in range(CHUNK):
            zs_ref[t, pl.ds(pl.multiple_of(grp * SSM_GROUP, SSM_GROUP), SSM_GROUP), :] = (
                z[t * SSM_GROUP:(t + 1) * SSM_GROUP, :])

    @pl.when(step == pl.num_programs(0) - 1)
    def _():
        def put(t):
            return pltpu.make_async_copy(zb_ref.at[t], z_hbm.at[:, t, :], zb_sem.at[t])

        for t0 in range(0, CHUNK, Z_BATCH):
            for t in range(t0, t0 + Z_BATCH):
                zt = zs_ref[t]
                gate = jnp.dot(wg_ref[...], zt.astype(BF16),
                               preferred_element_type=F32) + bg_ref[...]
                zb_ref[t] = (zt * jax.nn.sigmoid(gate)).T
            for t in range(t0, t0 + Z_BATCH):
                put(t).start()
        for t in range(CHUNK):
            put(t).wait()


def _ssm(xt, mt, gt, wo, at, w_glu_t, b_glu_col, h0, n_seq):
    n_rows = xt.shape[-1]
    has_h0 = h0 is not None
    gb = GROUP_BLOCK
    w_spec = pl.BlockSpec((gb, CW, CW), lambda i: (i, 0, 0))
    in_specs = [pl.BlockSpec((gb, CW, n_rows), lambda i: (i, 0, 0)),
                w_spec, w_spec, w_spec,
                pl.BlockSpec((gb, AT_ROWS, 4 * 2 * SSM_STATE), lambda i: (i, 0, 0)),
                pl.BlockSpec((SSM_WIDTH, SSM_WIDTH), lambda i: (0, 0)),
                pl.BlockSpec((SSM_WIDTH, 1), lambda i: (0, 0))]
    args = [xt, mt, gt, wo, at, w_glu_t, b_glu_col]
    out_specs = [pl.BlockSpec(memory_space=pl.ANY)]
    out_shape = [jax.ShapeDtypeStruct((n_rows, CHUNK, SSM_WIDTH), F32)]
    scratch = [pltpu.VMEM((CHUNK, SSM_WIDTH, n_rows), F32),
               pltpu.VMEM((CHUNK, n_rows, SSM_WIDTH), F32),
               pltpu.SemaphoreType.DMA((CHUNK,))]
    if has_h0:
        in_specs.append(pl.BlockSpec((gb, n_seq, CW), lambda i: (i, 0, 0)))
        args.append(h0)
    else:
        out_specs.append(pl.BlockSpec((gb, n_seq, CW), lambda i: (i, 0, 0)))
        out_shape.append(jax.ShapeDtypeStruct((N_GROUPS, n_seq, CW), F32))
        scratch.append(pltpu.VMEM((2, n_rows, 2 * SSM_STATE), F32))
    return pl.pallas_call(
        functools.partial(_ssm_kernel, n_seq=n_seq, has_h0=has_h0),
        grid=(N_GROUPS // gb,),
        in_specs=in_specs,
        out_specs=out_specs,
        out_shape=out_shape,
        scratch_shapes=scratch,
        compiler_params=_cparams(1),
        name="ssm_scan_glu",
    )(*args)


FF_CHUNK = 256
FFN_TOK = 1024


def _out_ffn_kernel(x_ref, attn_ref, z_ref, mod_ref, g_ref, wo_ref, wfi_ref, wfo_ref,
                    o_ref, act_ref):
    gate1 = mod_ref[:, 2 * D_MODEL:3 * D_MODEL]
    shift2 = mod_ref[:, 3 * D_MODEL:4 * D_MODEL]
    scale2 = mod_ref[:, 4 * D_MODEL:5 * D_MODEL]
    gate2 = mod_ref[:, 5 * D_MODEL:6 * D_MODEL]
    half = FFN_TOK // 2
    gain2 = g_ref[2:3, :] * (1.0 + scale2)

    def pre_ffn(hf):
        r = slice(hf * half, (hf + 1) * half)
        mixer = jnp.concatenate([attn_ref[r, :], z_ref[r, :].astype(BF16)], axis=1)
        mix = jnp.dot(mixer, wo_ref[...], preferred_element_type=F32)
        x1 = x_ref[r, :] + gate1 * _rms(mix, g_ref[1:2, :])
        ms = jnp.mean(x1 * x1, axis=-1, keepdims=True)
        return x1, (x1 * lax.rsqrt(ms + NORM_EPS) * gain2 + shift2).astype(BF16)

    def ffn_in(hf, h, chunks):
        r = slice(hf * half, (hf + 1) * half)
        for c in chunks:
            lo = c * FF_CHUNK
            gt = jnp.dot(h, wfi_ref[:, lo:lo + FF_CHUNK], preferred_element_type=F32)
            up = jnp.dot(h, wfi_ref[:, D_FF + lo:D_FF + lo + FF_CHUNK],
                         preferred_element_type=F32)
            act_ref[r, lo:lo + FF_CHUNK] = (_silu(gt) * up).astype(BF16)

    def ffn_out(hf, x1):
        r = slice(hf * half, (hf + 1) * half)
        f = jnp.dot(act_ref[r, :], wfo_ref[...], preferred_element_type=F32)
        o_ref[r, :] = x1 + gate2 * _rms(f, g_ref[3:4, :])

    n_chunks = D_FF // FF_CHUNK
    x1_a, h_a = pre_ffn(0)
    x1_b, h_b = pre_ffn(1)
    ffn_in(0, h_a, range(n_chunks))
    ffn_out(0, x1_a)
    ffn_in(1, h_b, range(n_chunks))
    ffn_out(1, x1_b)


def _out_ffn(x2d, attn, z, mods, mod_rows, norm_g, w_o, w_ffn_in, w_ffn_out):
    n_tok = x2d.shape[0]
    z2d = z.reshape(n_tok, SSM_WIDTH)
    n_steps = n_tok // FFN_TOK
    mod_row0, n_mod = mod_rows
    steps_per_mod = n_steps // n_mod
    const = lambda i: (0, 0)
    row_spec = lambda w: pl.BlockSpec((FFN_TOK, w), lambda i: (i, 0))
    return pl.pallas_call(
        _out_ffn_kernel,
        grid=(n_steps,),
        in_specs=[row_spec(D_MODEL), row_spec(ATT_WIDTH), row_spec(SSM_WIDTH),
                  pl.BlockSpec((None, 1, N_MOD * D_MODEL),
                               lambda i: (mod_row0 + i // steps_per_mod, 0, 0)),
                  pl.BlockSpec((4, D_MODEL), const),
                  pl.BlockSpec((2 * ATT_WIDTH, D_MODEL), const, pipeline_mode=pl.Buffered(1)),
                  pl.BlockSpec((D_MODEL, 2 * D_FF), const, pipeline_mode=pl.Buffered(1)),
                  pl.BlockSpec((D_FF, D_MODEL), const, pipeline_mode=pl.Buffered(1))],
        out_specs=row_spec(D_MODEL),
        out_shape=jax.ShapeDtypeStruct((n_tok, D_MODEL), F32),
        scratch_shapes=[pltpu.VMEM((FFN_TOK, D_FF), BF16)],
        compiler_params=_cparams(1),
        name="out_proj_ffn",
    )(x2d, attn, z2d, mods, norm_g, w_o, w_ffn_in, w_ffn_out)


def _rope_tables(seq_len):
    t = np.arange(seq_len)
    row = (t // GRID_W).astype(np.float32)
    col = (t % GRID_W).astype(np.float32)
    half = HEAD_DIM // 2
    inv_freq = (np.float32(ROPE_BASE)
                ** (-np.arange(0, half, 2, dtype=np.float32) / np.float32(half))).astype(np.float32)
    ang_r = row[:, None] * inv_freq
    ang_c = col[:, None] * inv_freq
    ang = np.concatenate([ang_r, ang_r, ang_c, ang_c], axis=-1)
    cos, sin = np.cos(ang), np.sin(ang)
    upper = (np.arange(HEAD_DIM) % 32) < 16
    sa = np.where(upper, -sin, 0.0)
    sb = np.where(upper, 0.0, sin)
    two = lambda a: jnp.asarray(np.concatenate([a, a], axis=-1), dtype=F32)
    return two(cos), two(sa), two(sb)


def _layer(x, mods, mod_rows, lam_init, rope_tabs, ctx_k, ctx_v, h0, weights, prep):
    n_batch, seq_len = x.shape[:2]
    g = weights['norm_g']
    outs = _in_proj(x, mods, mod_rows, g[0:1], weights['w_in'], rope_tabs)
    q, k, v = outs[:3]
    pending = () if 'late_bf16' in weights else weights['late_f32']
    attn, cast = _attention(q, k, v, ctx_k, ctx_v, weights['lam'], weights['subln_g'],
                            n_batch, seq_len, lam_init, cast_weights=pending)
    if pending:
        weights['late_bf16'] = cast
    ssm_out = _ssm(outs[-1], *prep, weights['w_glu_t'], weights['b_glu_col'], h0, n_batch)
    y = _out_ffn(x.reshape(n_batch * seq_len, D_MODEL), attn, ssm_out[0], mods, mod_rows, g,
                 *weights['late_bf16'])
    return y.reshape(x.shape), outs[3:-1], ssm_out[1:]


def kernel(x_prompt, x_sample, cache_k, cache_v, state_ssm_re, state_ssm_im, c, c_ctx, w_mod, b_mod, norm_g, w_in, lam_params, subln_g, ssm_lambda_re, ssm_lambda_im, ssm_log_step, ssm_b_re, ssm_b_im, ssm_c_re, ssm_c_im, ssm_d, w_glu, b_glu, w_o, w_ffn_in, w_ffn_out):
    depth = w_mod.shape[0]
    assert depth == 1
    bp = x_prompt.shape[0]
    bd, ld_len = x_sample.shape[:2]
    past = cache_k.shape[2]
    xp, xs = x_prompt, x_sample
    rope_tabs = _rope_tables(ld_len)
    ks_out, vs_out, hr_out, hi_out = [], [], [], []
    for l in range(depth):
        lam_init = 0.8 - 0.6 * math.exp(-0.3 * l)
        mods = _modulation(c_ctx, c, w_mod[l], b_mod[l])
        weights = {
            'norm_g': norm_g[l],
            'w_in': w_in[l].astype(BF16),
            'lam': lam_params[l], 'subln_g': subln_g[l],
            'w_glu_t': w_glu[l].T.astype(BF16), 'b_glu_col': b_glu[l].reshape(SSM_WIDTH, 1),
            'late_f32': (w_o[l], w_ffn_in[l], w_ffn_out[l]),
        }
        prep = _ssm_prep(ssm_lambda_re[l], ssm_lambda_im[l], ssm_log_step[l],
                         ssm_b_re[l], ssm_b_im[l], ssm_c_re[l], ssm_c_im[l], ssm_d[l])
        ck = cache_k[:, l].reshape(bd, past, ATT_WIDTH)
        cv = cache_v[:, l].reshape(bd, past, ATT_WIDTH)
        h0 = jnp.stack([state_ssm_re[:, l], state_ssm_im[:, l]], axis=2)
        h0 = h0.transpose(3, 0, 1, 2, 4).reshape(N_GROUPS, bd, CW)
        xs, _, _ = _layer(xs, mods, (1, bd), lam_init, rope_tabs, ck, cv, h0, weights, prep)
        xp, (k_ctx, v_ctx), (st,) = _layer(xp, mods, (0, 1), lam_init, None, None, None, None,
                                           weights, prep)
        ks_out.append(jnp.swapaxes(k_ctx, 1, 2).reshape(bp, -1, 2 * N_HEADS, HEAD_DIM))
        vs_out.append(v_ctx)
        fin = st.reshape(N_GROUPS, bp, 2, 2, SSM_STATE).transpose(1, 2, 3, 0, 4)
        hr_out.append(fin[:, :, 0])
        hi_out.append(fin[:, :, 1])
    return (xp, xs, jnp.stack(ks_out, axis=1), jnp.stack(vs_out, axis=1),
            jnp.stack(hr_out, axis=1), jnp.stack(hi_out, axis=1))
```

```python
import functools
import math

import jax
import jax.numpy as jnp
import numpy as np
from jax import lax
from jax.experimental import pallas as pl
from jax.experimental.pallas import tpu as pltpu

F32 = jnp.float32
BF16 = jnp.bfloat16

D_MODEL = 1024
GRID_W = 64
ATT_WIDTH = 512
SSM_WIDTH = 512
HEAD_DIM = 64
N_HEADS = 4
HEAD_W = 2 * HEAD_DIM
SSM_GROUP = 16
N_GROUPS = 32
SSM_STATE = 64
D_FF = 2816
N_MOD = 6
ROPE_BASE = 10000.0
NORM_EPS = 1e-6

CHUNK = 16
CW = CHUNK * SSM_GROUP
SCAN_BLOCK = 8
SCAN_LEVELS = 3
AT_ROWS = 16
GROUP_BLOCK = 4
Z_BATCH = 8

ROWS_PER_TILE = 128
TOK_PER_TILE = ROWS_PER_TILE * CHUNK
SUB_TOK = 512
TQ_ITEM = 256
N_SUB = TOK_PER_TILE // SUB_TOK

VMEM_LIMIT = 56 * 1024 * 1024

NT_DIMS = (((1,), (1,)), ((), ()))
TN_DIMS = (((0,), (0,)), ((), ()))


def _cparams(n_axes):
    return pltpu.CompilerParams(
        dimension_semantics=("arbitrary",) * n_axes,
        vmem_limit_bytes=VMEM_LIMIT)


def _rms(x, g):
    ms = jnp.mean(x * x, axis=-1, keepdims=True)
    return x * lax.rsqrt(ms + NORM_EPS) * g


def _silu(x):
    return x * jax.nn.sigmoid(x)


MOD_ROWS = 8


def _mod_kernel(ctx_ref, c_ref, w_ref, b_ref, o_ref):
    n_lat, tk = c_ref.shape
    row = lax.broadcasted_iota(jnp.int32, (MOD_ROWS, tk), 0)
    cond = jnp.where(row == 0, ctx_ref[...], 0.0)
    for b in range(n_lat):
        cond = jnp.where(row == 1 + b, c_ref[b:b + 1, :], cond)
    part = jnp.dot(_silu(cond).astype(BF16), w_ref[...].astype(BF16),
                   preferred_element_type=F32)

    @pl.when(pl.program_id(0) == 0)
    def _():
        o_ref[:, 0, :] = part + b_ref[...]

    @pl.when(pl.program_id(0) > 0)
    def _():
        o_ref[:, 0, :] += part


def _modulation(c_ctx, c, w_mod, b_mod):
    n = w_mod.shape[1]
    tk = 256
    assert 1 + c.shape[0] <= MOD_ROWS
    return pl.pallas_call(
        _mod_kernel,
        grid=(D_MODEL // tk,),
        in_specs=[pl.BlockSpec((1, tk), lambda k: (0, k)),
                  pl.BlockSpec((c.shape[0], tk), lambda k: (0, k)),
                  pl.BlockSpec((tk, n), lambda k: (k, 0)),
                  pl.BlockSpec((1, n), lambda k: (0, 0))],
        out_specs=pl.BlockSpec((MOD_ROWS, 1, n), lambda k: (0, 0, 0)),
        out_shape=jax.ShapeDtypeStruct((MOD_ROWS, 1, n), F32),
        compiler_params=_cparams(1),
        name="modulation",
    )(c_ctx.reshape(1, D_MODEL), c, w_mod, b_mod.reshape(1, n))


def _rope(x, cos, sa, sb):
    return (x * cos + pltpu.roll(x, HEAD_W - 16, axis=1) * sa
            + pltpu.roll(x, 16, axis=1) * sb)


def _in_proj_kernel(*refs, rope, seqs_per_sub):
    x_ref, x3_hbm, mod_ref, g_ref, w_ref = refs[:5]
    refs = refs[5:]
    if rope:
        cos_ref, sa_ref, sb_ref = refs[:3]
        refs = refs[3:]
        q_ref, k_ref, v_ref, ut_ref, wut_ref, xt_ref, xt_sem = refs
    else:
        q_ref, k_ref, v_ref, kc_ref, vc_ref, ut_ref, wut_ref, xt_ref, xt_sem = refs
    tile = pl.program_id(0)
    j = pl.program_id(1)
    t_per_sub = CHUNK // N_SUB
    t_early = CHUNK - t_per_sub

    def gather(tile_idx, t):
        src = x3_hbm.at[pl.ds(tile_idx * ROWS_PER_TILE, ROWS_PER_TILE), t, :]
        return pltpu.make_async_copy(src, xt_ref.at[t], xt_sem.at[t])

    @pl.when(j == 0)
    def _():
        @pl.when(tile == 0)
        def _():
            for t in range(t_early):
                gather(0, t).start()
        for t in range(t_early, CHUNK):
            gather(tile, t).start()

    t_base = j * t_per_sub
    for d in range(t_per_sub):
        gather(tile, t_base + d).wait()

    @pl.when((j == N_SUB - 1) & (tile + 1 < pl.num_programs(0)))
    def _():
        for t in range(t_early):
            gather(tile + 1, t).start()

    @pl.when((tile == 0) & (j == 0))
    def _():
        wut_ref[...] = w_ref[:, 3 * ATT_WIDTH:].T

    shift = mod_ref[:, 0:D_MODEL]
    gain = g_ref[...] * (1.0 + mod_ref[:, D_MODEL:2 * D_MODEL])

    def norm_mod(xv):
        ms = jnp.mean(xv * xv, axis=-1, keepdims=True)
        return (xv * lax.rsqrt(ms + NORM_EPS) * gain + shift).astype(BF16)

    def ssm_input(d0):
        xt = jnp.concatenate([xt_ref[t_base + d0], xt_ref[t_base + d0 + 1]], axis=0)
        ut = lax.dot_general(wut_ref[...], norm_mod(xt), NT_DIMS,
                             preferred_element_type=F32)
        for d in range(2):
            blk = ut[:, d * ROWS_PER_TILE:(d + 1) * ROWS_PER_TILE]
            row0 = pl.multiple_of((t_base + d0 + d) * SSM_GROUP, SSM_GROUP)
            ut_ref[:, pl.ds(row0, SSM_GROUP), :] = (
                blk.reshape(N_GROUPS, SSM_GROUP, ROWS_PER_TILE).astype(ut_ref.dtype))

    proj = jnp.dot(norm_mod(x_ref[...]), w_ref[:, 0:3 * ATT_WIDTH],
                   preferred_element_type=F32)
    q = proj[:, 0:ATT_WIDTH]
    k = proj[:, ATT_WIDTH:2 * ATT_WIDTH]
    v = proj[:, 2 * ATT_WIDTH:3 * ATT_WIDTH]
    qscale = HEAD_DIM ** -0.5 * math.log2(math.e)
    if rope:
        cos, sa, sb = cos_ref[...], sa_ref[...], sb_ref[...]
        for hd in range(N_HEADS):
            sl = slice(hd * HEAD_W, (hd + 1) * HEAD_W)
            q_ref[:, sl] = (_rope(q[:, sl], cos, sa, sb) * qscale).astype(q_ref.dtype)
            k_ref[:, sl] = _rope(k[:, sl], cos, sa, sb).astype(k_ref.dtype)
    else:
        q_ref[...] = (q * qscale).astype(q_ref.dtype)
        k_ref[...] = k.astype(k_ref.dtype)
        seq = SUB_TOK // seqs_per_sub
        k_t = k.T
        for b in range(seqs_per_sub):
            kc_ref[b] = k_t[:, b * seq:(b + 1) * seq]
            for hd in range(N_HEADS):
                vc_ref[b, :, hd, :] = v[b * seq:(b + 1) * seq, hd * HEAD_W:(hd + 1) * HEAD_W]
    v_ref[...] = v.astype(v_ref.dtype)

    for d0 in range(0, t_per_sub, 2):
        ssm_input(d0)


def _in_proj(x, mods, mod_rows, g0, w_in, rope_tabs):
    n_batch, seq_len = x.shape[:2]
    n_tok = n_batch * seq_len
    n_rows = n_tok // CHUNK
    n_tiles = n_tok // TOK_PER_TILE
    mod_row0, n_mod = mod_rows
    tiles_per_mod = n_tiles // n_mod
    rope = rope_tabs is not None
    seqs_per_sub = max(1, SUB_TOK // seq_len)
    in_specs = [pl.BlockSpec((SUB_TOK, D_MODEL), lambda i, j: (i * N_SUB + j, 0)),
                pl.BlockSpec(memory_space=pl.ANY),
                pl.BlockSpec((None, 1, 2 * D_MODEL),
                             lambda i, j: (mod_row0 + i // tiles_per_mod, 0, 0)),
                pl.BlockSpec((1, D_MODEL), lambda i, j: (0, 0)),
                pl.BlockSpec((D_MODEL, 4 * ATT_WIDTH), lambda i, j: (0, 0))]
    args = [x.reshape(n_tok, D_MODEL), x.reshape(n_rows, CHUNK, D_MODEL), mods, g0, w_in]
    row_spec = pl.BlockSpec((SUB_TOK, ATT_WIDTH), lambda i, j: (i * N_SUB + j, 0))
    row_shape = jax.ShapeDtypeStruct((n_tok, ATT_WIDTH), BF16)
    out_specs = [row_spec, row_spec, row_spec]
    out_shape = [row_shape, row_shape, row_shape]
    if rope:
        assert seq_len == TOK_PER_TILE
        for tab in rope_tabs:
            in_specs.append(pl.BlockSpec((SUB_TOK, HEAD_W), lambda i, j: (j, 0)))
            args.append(tab)
    else:
        out_specs += [pl.BlockSpec((seqs_per_sub, ATT_WIDTH, seq_len),
                                   lambda i, j: (i * N_SUB + j, 0, 0)),
                      pl.BlockSpec((seqs_per_sub, seq_len, N_HEADS, HEAD_W),
                                   lambda i, j: (i * N_SUB + j, 0, 0, 0))]
        out_shape += [jax.ShapeDtypeStruct((n_batch, ATT_WIDTH, seq_len), F32),
                      jax.ShapeDtypeStruct((n_batch, seq_len, N_HEADS, HEAD_W), F32)]
    out_specs.append(pl.BlockSpec((N_GROUPS, CW, ROWS_PER_TILE), lambda i, j: (0, 0, i)))
    out_shape.append(jax.ShapeDtypeStruct((N_GROUPS, CW, n_rows), BF16))
    return pl.pallas_call(
        functools.partial(_in_proj_kernel, rope=rope, seqs_per_sub=seqs_per_sub),
        grid=(n_tiles, N_SUB),
        in_specs=in_specs,
        out_specs=out_specs,
        out_shape=out_shape,
        scratch_shapes=[pltpu.VMEM((SSM_WIDTH, D_MODEL), BF16),
                        pltpu.VMEM((CHUNK, ROWS_PER_TILE, D_MODEL), F32),
                        pltpu.SemaphoreType.DMA((CHUNK,))],
        compiler_params=_cparams(2),
        name="in_proj",
    )(*args)


def _attn_kernel(*refs, has_ctx, lam_init, n_seq, seq_len, tq, n_cast):
    if n_cast:
        cast_in = refs[len(refs) - 2 * n_cast - 1:len(refs) - n_cast - 1]
        cast_out = refs[len(refs) - n_cast:]
        refs = refs[:len(refs) - 2 * n_cast - 1] + (refs[len(refs) - n_cast - 1],)
        for src, dst in zip(cast_in, cast_out):
            dst[...] = src[...].astype(dst.dtype)
    if has_ctx:
        lam_ref, sg_ref, q_ref, ck_ref, cv_ref, k_ref, v_ref, o_ref = refs
    else:
        lam_ref, sg_ref, q_ref, k_ref, v_ref, o_ref = refs
    lp = lam_ref[...]
    lam = (jnp.exp(jnp.sum(lp[0:1] * lp[1:2], axis=-1, keepdims=True))
           - jnp.exp(jnp.sum(lp[2:3] * lp[3:4], axis=-1, keepdims=True)) + lam_init)
    first_map = lax.broadcasted_iota(jnp.int32, (1, HEAD_W), 1) < HEAD_DIM
    ti = min(TQ_ITEM, tq)
    for row0 in range(0, n_seq * tq, ti):
        b = row0 // tq
        q_rows = slice(row0, row0 + ti)
        for hd in range(N_HEADS):
            sl = slice(hd * HEAD_W, (hd + 1) * HEAD_W)
            qh = q_ref[q_rows, sl]
            zero = jnp.zeros_like(qh)
            qs = jnp.concatenate([jnp.where(first_map, qh, zero),
                                  jnp.where(first_map, zero, qh)], axis=0)
            kv_rows = slice(b * seq_len, (b + 1) * seq_len)
            parts = [(k_ref[kv_rows, sl], v_ref[kv_rows, sl])]
            if has_ctx:
                parts.insert(0, (ck_ref[:, sl].astype(BF16), cv_ref[:, sl].astype(BF16)))
            scores = [lax.dot_general(qs, kk, NT_DIMS, preferred_element_type=F32)
                      for kk, _ in parts]
            mx = scores[0].max(axis=-1, keepdims=True)
            for s in scores[1:]:
                mx = jnp.maximum(mx, s.max(axis=-1, keepdims=True))
            acc = None
            for s, (_, vv) in zip(scores, parts):
                e = jnp.exp2(s - mx).astype(BF16)
                v_one = jnp.concatenate([vv, jnp.ones_like(vv)], axis=1)
                pv = jnp.dot(e, v_one, preferred_element_type=F32)
                acc = pv if acc is None else acc + pv
            num = acc[:, 0:HEAD_W] / acc[:, HEAD_W:2 * HEAD_W]
            o = num[0:ti] - lam * num[ti:2 * ti]
            o = _rms(o, sg_ref[...]) * (1.0 - lam_init)
            o_ref[q_rows, sl] = o.astype(o_ref.dtype)


def _attention(q, k, v, ctx_k, ctx_v, lam_params, subln_g, n_batch, seq_len, lam_init,
               cast_weights=()):
    has_ctx = ctx_k is not None
    tq = min(1024, seq_len)
    n_q = seq_len // tq
    n_seq = 1 if n_q > 1 else min(4, n_batch)
    in_specs = [pl.BlockSpec((4, HEAD_DIM), lambda b, i: (0, 0)),
                pl.BlockSpec((1, HEAD_W), lambda b, i: (0, 0)),
                pl.BlockSpec((n_seq * tq, ATT_WIDTH), lambda b, i: (b * n_q + i, 0))]
    args = [lam_params, subln_g.reshape(1, HEAD_W), q]
    if has_ctx:
        past = ctx_k.shape[1]
        ctx_spec = pl.BlockSpec((None, past, ATT_WIDTH), lambda b, i: (b, 0, 0))
        in_specs += [ctx_spec, ctx_spec]
        args += [ctx_k, ctx_v]
    kv_spec = pl.BlockSpec((n_seq * seq_len, ATT_WIDTH), lambda b, i: (b, 0))
    in_specs += [kv_spec, kv_spec]
    args += [k, v]
    out_specs = [pl.BlockSpec((n_seq * tq, ATT_WIDTH), lambda b, i: (b * n_q + i, 0))]
    out_shape = [jax.ShapeDtypeStruct((n_batch * seq_len, ATT_WIDTH), BF16)]
    n_steps = (n_batch // n_seq) * n_q
    for w in cast_weights:
        rows = w.shape[0] // n_steps
        spec = pl.BlockSpec((rows, w.shape[1]), lambda b, i: (b * n_q + i, 0))
        in_specs.append(spec)
        args.append(w)
        out_specs.append(spec)
        out_shape.append(jax.ShapeDtypeStruct(w.shape, BF16))
    outs = pl.pallas_call(
        functools.partial(_attn_kernel, has_ctx=has_ctx, lam_init=lam_init,
                          n_seq=n_seq, seq_len=seq_len, tq=tq, n_cast=len(cast_weights)),
        grid=(n_batch // n_seq, n_q),
        in_specs=in_specs,
        out_specs=out_specs,
        out_shape=out_shape,
        compiler_params=_cparams(2),
        name="diff_attention",
    )(*args)
    return outs[0], tuple(outs[1:])


def _cmul(ar, ai, br, bi):
    return ar * br - ai * bi, ar * bi + ai * br


def _ssm_prep_kernel(lre_ref, lim_ref, ls_ref, bre_ref, bim_ref, cre_ref, cim_ref, d_ref,
                     mt_ref, gt_ref, wo_ref, at_ref):
    lane = lax.broadcasted_iota(jnp.int32, (SSM_GROUP, CW), 1)
    chan = lax.broadcasted_iota(jnp.int32, (SSM_GROUP, CW), 0)
    for gi in range(GROUP_BLOCK):
        gt_cols, wo_cols, at_cols, toeplitz = [], [], [], []
        for dr in range(2):
            lr = jnp.minimum(lre_ref[dr, gi], -1e-4)
            li = lim_ref[dr, gi]
            step = jnp.exp(ls_ref[dr, gi])
            mag = jnp.exp(lr * step)
            a_re = mag * jnp.cos(li * step)
            a_im = mag * jnp.sin(li * step)
            den = lr * lr + li * li
            nr = a_re - 1.0
            f_re = (nr * lr + a_im * li) / den
            f_im = (a_im * lr - nr * li) / den
            bt_re, bt_im = bre_ref[dr, gi], bim_ref[dr, gi]
            bb_re, bb_im = _cmul(f_re, f_im, bt_re, bt_im)
            c_re, c_im = cre_ref[dr, gi], cim_ref[dr, gi]
            pw = [(jnp.ones_like(a_re), jnp.zeros_like(a_im))]
            for _ in range(CHUNK):
                pw.append(_cmul(pw[-1][0], pw[-1][1], a_re, a_im))
            g_re, g_im, e_re, e_im = [], [], [], []
            for t in range(CHUNK):
                pr, pi = pw[CHUNK - 1 - t] if dr == 0 else pw[t]
                r, i = _cmul(bb_re, bb_im, pr, pi)
                g_re.append(r)
                g_im.append(i)
                pr, pi = pw[t + 1] if dr == 0 else pw[CHUNK - t]
                r, i = _cmul(c_re, c_im, pr, pi)
                e_re.append(r)
                e_im.append(-i)
            g_cat = jnp.concatenate([jnp.concatenate(g_re, axis=0),
                                     jnp.concatenate(g_im, axis=0)], axis=1)
            gt_cols.append(g_cat)
            wo_cols.append(jnp.concatenate([jnp.concatenate(e_re, axis=0),
                                            jnp.concatenate(e_im, axis=0)], axis=1))
            c_cat = jnp.concatenate([c_re, -c_im], axis=1)
            toeplitz.append(lax.dot_general(c_cat, g_cat, NT_DIMS,
                                            precision=lax.Precision.HIGHEST,
                                            preferred_element_type=F32))
            apw = [pw[CHUNK]]
            for _ in range(SCAN_BLOCK - 1):
                apw.append(_cmul(apw[-1][0], apw[-1][1], apw[0][0], apw[0][1]))
            order = list(range(SCAN_BLOCK)) if dr == 0 else list(range(SCAN_BLOCK - 1, -1, -1))
            order += [2 ** l - 1 for l in range(SCAN_LEVELS)]
            order += [0] * (AT_ROWS - len(order))
            at_cols += [jnp.concatenate([jnp.concatenate([apw[i][0], apw[i][0]], axis=1)
                                         for i in order], axis=0),
                        jnp.concatenate([jnp.concatenate([-apw[i][1], apw[i][1]], axis=1)
                                         for i in order], axis=0)]
        kf_rev, kb = toeplitz
        d_skip = d_ref[gi]
        blocks = []
        for t in range(CHUNK):
            fwd = pltpu.roll(kf_rev, (CW - (CHUNK - 1 - t) * SSM_GROUP) % CW, axis=1)
            bwd = pltpu.roll(kb, t * SSM_GROUP, axis=1)
            blocks.append(jnp.where(lane < (t + 1) * SSM_GROUP, fwd, 0.0)
                          + jnp.where(lane >= t * SSM_GROUP, bwd, 0.0)
                          + jnp.where(lane == chan + t * SSM_GROUP, d_skip, 0.0))
        mt_ref[gi] = jnp.concatenate(blocks, axis=0).astype(mt_ref.dtype)
        gt_ref[gi] = jnp.concatenate(gt_cols, axis=1).astype(gt_ref.dtype)
        wo_ref[gi] = jnp.concatenate(wo_cols, axis=1).astype(wo_ref.dtype)
        at_ref[gi] = jnp.concatenate(at_cols, axis=1)


def _ssm_prep(lam_re, lam_im, log_step, b_re, b_im, c_re, c_im, d_skip):
    row = lambda a: a.reshape(2, N_GROUPS, 1, SSM_STATE)
    bt = lambda a: jnp.swapaxes(a, 2, 3)
    d_row = jnp.tile((d_skip[0] + d_skip[1]).reshape(N_GROUPS, 1, SSM_GROUP), (1, 1, CHUNK))
    gb = GROUP_BLOCK
    vec_spec = pl.BlockSpec((2, gb, 1, SSM_STATE), lambda i: (0, i, 0, 0))
    mat_spec = pl.BlockSpec((2, gb, SSM_GROUP, SSM_STATE), lambda i: (0, i, 0, 0))
    w_spec = pl.BlockSpec((gb, CW, CW), lambda i: (i, 0, 0))
    w_shape = jax.ShapeDtypeStruct((N_GROUPS, CW, CW), BF16)
    return pl.pallas_call(
        _ssm_prep_kernel,
        grid=(N_GROUPS // gb,),
        in_specs=[vec_spec, vec_spec,
                  pl.BlockSpec((2, gb, 1, 1), lambda i: (0, i, 0, 0)),
                  mat_spec, mat_spec, mat_spec, mat_spec,
                  pl.BlockSpec((gb, 1, CW), lambda i: (i, 0, 0))],
        out_specs=[w_spec, w_spec, w_spec,
                   pl.BlockSpec((gb, AT_ROWS, 4 * 2 * SSM_STATE), lambda i: (i, 0, 0))],
        out_shape=[w_shape, w_shape, w_shape,
                   jax.ShapeDtypeStruct((N_GROUPS, AT_ROWS, 4 * 2 * SSM_STATE), F32)],
        compiler_params=_cparams(1),
        name="ssm_prep",
    )(row(lam_re), row(lam_im), log_step.reshape(2, N_GROUPS, 1, 1),
      bt(b_re), bt(b_im), c_re, c_im, d_row)


def _shift_rows(x, m, down):
    n = x.shape[0]
    return pltpu.roll(x, m if down else n - m, axis=0)


def _ssm_kernel(*refs, n_seq, has_h0):
    if has_h0:
        (xt_ref, mt_ref, gt_ref, wo_ref, at_ref, wg_ref, bg_ref, h0_ref,
         z_hbm, zs_ref, zb_ref, zb_sem) = refs
    else:
        (xt_ref, mt_ref, gt_ref, wo_ref, at_ref, wg_ref, bg_ref,
         z_hbm, st_ref, zs_ref, zb_ref, zb_sem, fin_ref) = refs
    step = pl.program_id(0)
    n_rows = xt_ref.shape[-1]
    seg = n_rows // n_seq
    cw2 = 2 * SSM_STATE
    assert seg % SCAN_BLOCK == 0
    n_blk = n_rows // SCAN_BLOCK
    blk_per_seq = seg // SCAN_BLOCK
    pos = lax.broadcasted_iota(jnp.int32, (n_rows, cw2), 0) % seg
    row_blk = lax.broadcasted_iota(jnp.int32, (SCAN_BLOCK, cw2), 0)

    def low_half(shape):
        return lax.broadcasted_iota(jnp.int32, shape, 1) < SSM_STATE

    def swap(v):
        return pltpu.roll(v, SSM_STATE, axis=1)

    def to_planes(va, vb):
        lo = low_half((va.shape[0], cw2))
        va_l = pltpu.roll(va, 3 * SSM_STATE, axis=1)
        vb_r = pltpu.roll(vb, SSM_STATE, axis=1)
        return (jnp.where(lo, va[:, 0:cw2], vb_r[:, 0:cw2]),
                jnp.where(lo, va_l[:, 0:cw2], vb[:, 0:cw2]),
                jnp.where(lo, va[:, cw2:2 * cw2], vb_r[:, cw2:2 * cw2]),
                jnp.where(lo, va_l[:, cw2:2 * cw2], vb[:, cw2:2 * cw2]))

    def from_planes(f_re, f_im, b_re, b_im):
        lo = low_half(f_re.shape)
        va = jnp.concatenate([jnp.where(lo, f_re, swap(f_im)), jnp.where(lo, b_re, swap(b_im))], axis=1)
        vb = jnp.concatenate([jnp.where(lo, swap(f_re), f_im), jnp.where(lo, swap(b_re), b_im)], axis=1)
        return va, vb

    for ga in range(0, GROUP_BLOCK, 2):
        gb = ga + 1
        xts = [xt_ref[ga], xt_ref[gb]]
        s_pair = [lax.dot_general(xts[i], gt_ref[g], TN_DIMS, preferred_element_type=F32)
                  for i, g in enumerate((ga, gb))]
        planes = to_planes(*s_pair)
        lo_t = low_half((AT_ROWS, cw2))
        if has_h0:
            h0_planes = to_planes(h0_ref[ga], h0_ref[gb])
        ent_planes = []
        for dr in range(2):
            down = dr == 0
            pa, pb = (at_ref[g][:, (2 * dr) * cw2:(2 * dr + 1) * cw2] for g in (ga, gb))
            qa, qb = (at_ref[g][:, (2 * dr + 1) * cw2:(2 * dr + 2) * cw2] for g in (ga, gb))
            ar_tab = jnp.where(lo_t, pa, pb)
            ai_tab = jnp.where(lo_t, -qa, qb)
            re3 = planes[2 * dr].reshape(n_blk, SCAN_BLOCK, cw2)
            im3 = planes[2 * dr + 1].reshape(n_blk, SCAN_BLOCK, cw2)
            for lvl in range(SCAN_LEVELS):
                m = 2 ** lvl
                valid = (row_blk >= m) if down else (row_blk < SCAN_BLOCK - m)
                row = SCAN_BLOCK + lvl
                ar = jnp.where(valid, ar_tab[row:row + 1], 0.0)
                ai = jnp.where(valid, ai_tab[row:row + 1], 0.0)
                shift = m if down else SCAN_BLOCK - m
                sh_re = pltpu.roll(re3, shift, axis=1)
                sh_im = pltpu.roll(im3, shift, axis=1)
                re3, im3 = re3 + ar * sh_re - ai * sh_im, im3 + ar * sh_im + ai * sh_re
            ar_blk, ai_blk = ar_tab[0:SCAN_BLOCK], ai_tab[0:SCAN_BLOCK]
            blk_re = [re3[i] for i in range(n_blk)]
            blk_im = [im3[i] for i in range(n_blk)]
            edge = slice(SCAN_BLOCK - 1, SCAN_BLOCK) if down else slice(0, 1)
            for q_i in range(n_seq):
                idxs = list(range(q_i * blk_per_seq, (q_i + 1) * blk_per_seq))
                idxs = idxs if down else idxs[::-1]
                for prev, cur in zip([None] + idxs[:-1], idxs):
                    if prev is not None:
                        c_re, c_im = blk_re[prev][edge], blk_im[prev][edge]
                    elif has_h0:
                        c_re = h0_planes[2 * dr][q_i:q_i + 1]
                        c_im = h0_planes[2 * dr + 1][q_i:q_i + 1]
                    else:
                        continue
                    c_re = jnp.broadcast_to(c_re, (SCAN_BLOCK, cw2))
                    c_im = jnp.broadcast_to(c_im, (SCAN_BLOCK, cw2))
                    blk_re[cur] = blk_re[cur] + ar_blk * c_re - ai_blk * c_im
                    blk_im[cur] = blk_im[cur] + ar_blk * c_im + ai_blk * c_re
            for part, blks in ((0, blk_re), (1, blk_im)):
                s = jnp.concatenate(blks, axis=0)
                if not has_h0:
                    fin_ref[part] = s
                ent = _shift_rows(s, 1, down)
                ent = jnp.where((pos >= 1) if down else (pos < seg - 1), ent, 0.0)
                if has_h0:
                    ent_blk = [ent[i * SCAN_BLOCK:(i + 1) * SCAN_BLOCK] for i in range(n_blk)]
                    at_edge = row_blk == (0 if down else SCAN_BLOCK - 1)
                    for q_i in range(n_seq):
                        bi = q_i * blk_per_seq if down else (q_i + 1) * blk_per_seq - 1
                        ent_blk[bi] = jnp.where(
                            at_edge, h0_planes[2 * dr + part][q_i:q_i + 1], ent_blk[bi])
                    ent = jnp.concatenate(ent_blk, axis=0)
                ent_planes.append(ent)
            if not has_h0:
                rows = pl.ds(seg - 1 if down else 0, n_seq, stride=seg)
                f_re, f_im = fin_ref[0, rows, :], fin_ref[1, rows, :]
                lo_s = low_half((n_seq, cw2))
                st_ref[ga, :, dr * cw2:(dr + 1) * cw2] = jnp.where(lo_s, f_re, swap(f_im))
                st_ref[gb, :, dr * cw2:(dr + 1) * cw2] = jnp.where(lo_s, swap(f_re), f_im)
        h_pair = from_planes(*ent_planes)
        for i, g in enumerate((ga, gb)):
            yt = (jnp.dot(mt_ref[g], xts[i], preferred_element_type=F32)
                  + lax.dot_general(wo_ref[g], h_pair[i].astype(BF16), NT_DIMS,
                                    preferred_element_type=F32))
            z = jax.nn.gelu(yt, approximate=True)
            grp = step * GROUP_BLOCK + g
            for t in range(CHUNK):
                zs_ref[t, pl.ds(pl.multiple_of(grp * SSM_GROUP, SSM_GROUP), SSM_GROUP), :] = (
                    z[t * SSM_GROUP:(t + 1) * SSM_GROUP, :])

    @pl.when(step == pl.num_programs(0) - 1)
    def _():
        def put(t):
            return pltpu.make_async_copy(zb_ref.at[t], z_hbm.at[:, t, :], zb_sem.at[t])

        for t0 in range(0, CHUNK, Z_BATCH):
            for t in range(t0, t0 + Z_BATCH):
                zt = zs_ref[t]
                gate = jnp.dot(wg_ref[...], zt.astype(BF16),
                               preferred_element_type=F32) + bg_ref[...]
                zb_ref[t] = (zt * jax.nn.sigmoid(gate)).T
            for t in range(t0, t0 + Z_BATCH):
                put(t).start()
        for t in range(CHUNK):
            put(t).wait()


def _ssm(xt, mt, gt, wo, at, w_glu_t, b_glu_col, h0, n_seq):
    n_rows = xt.shape[-1]
    has_h0 = h0 is not None
    gb = GROUP_BLOCK
    w_spec = pl.BlockSpec((gb, CW, CW), lambda i: (i, 0, 0))
    in_specs = [pl.BlockSpec((gb, CW, n_rows), lambda i: (i, 0, 0)),
                w_spec, w_spec, w_spec,
                pl.BlockSpec((gb, AT_ROWS, 4 * 2 * SSM_STATE), lambda i: (i, 0, 0)),
                pl.BlockSpec((SSM_WIDTH, SSM_WIDTH), lambda i: (0, 0)),
                pl.BlockSpec((SSM_WIDTH, 1), lambda i: (0, 0))]
    args = [xt, mt, gt, wo, at, w_glu_t, b_glu_col]
    out_specs = [pl.BlockSpec(memory_space=pl.ANY)]
    out_shape = [jax.ShapeDtypeStruct((n_rows, CHUNK, SSM_WIDTH), F32)]
    scratch = [pltpu.VMEM((CHUNK, SSM_WIDTH, n_rows), F32),
               pltpu.VMEM((CHUNK, n_rows, SSM_WIDTH), F32),
               pltpu.SemaphoreType.DMA((CHUNK,))]
    if has_h0:
        in_specs.append(pl.BlockSpec((gb, n_seq, CW), lambda i: (i, 0, 0)))
        args.append(h0)
    else:
        out_specs.append(pl.BlockSpec((gb, n_seq, CW), lambda i: (i, 0, 0)))
        out_shape.append(jax.ShapeDtypeStruct((N_GROUPS, n_seq, CW), F32))
        scratch.append(pltpu.VMEM((2, n_rows, 2 * SSM_STATE), F32))
    return pl.pallas_call(
        functools.partial(_ssm_kernel, n_seq=n_seq, has_h0=has_h0),
        grid=(N_GROUPS // gb,),
        in_specs=in_specs,
        out_specs=out_specs,
        out_shape=out_shape,
        scratch_shapes=scratch,
        compiler_params=_cparams(1),
        name="ssm_scan_glu",
    )(*args)


FF_CHUNK = 256
FFN_TOK = 1024


def _out_ffn_kernel(x_ref, attn_ref, z_ref, mod_ref, g_ref, wo_ref, wfi_ref, wfo_ref,
                    o_ref, act_ref):
    gate1 = mod_ref[:, 2 * D_MODEL:3 * D_MODEL]
    shift2 = mod_ref[:, 3 * D_MODEL:4 * D_MODEL]
    scale2 = mod_ref[:, 4 * D_MODEL:5 * D_MODEL]
    gate2 = mod_ref[:, 5 * D_MODEL:6 * D_MODEL]
    half = FFN_TOK // 2
    gain2 = g_ref[2:3, :] * (1.0 + scale2)

    def pre_ffn(hf):
        r = slice(hf * half, (hf + 1) * half)
        mixer = jnp.concatenate([attn_ref[r, :], z_ref[r, :].astype(BF16)], axis=1)
        mix = jnp.dot(mixer, wo_ref[...], preferred_element_type=F32)
        x1 = x_ref[r, :] + gate1 * _rms(mix, g_ref[1:2, :])
        ms = jnp.mean(x1 * x1, axis=-1, keepdims=True)
        return x1, (x1 * lax.rsqrt(ms + NORM_EPS) * gain2 + shift2).astype(BF16)

    def ffn_in(hf, h, chunks):
        r = slice(hf * half, (hf + 1) * half)
        for c in chunks:
            lo = c * FF_CHUNK
            gt = jnp.dot(h, wfi_ref[:, lo:lo + FF_CHUNK], preferred_element_type=F32)
            up = jnp.dot(h, wfi_ref[:, D_FF + lo:D_FF + lo + FF_CHUNK],
                         preferred_element_type=F32)
            act_ref[r, lo:lo + FF_CHUNK] = (_silu(gt) * up).astype(BF16)

    def ffn_out(hf, x1):
        r = slice(hf * half, (hf + 1) * half)
        f = jnp.dot(act_ref[r, :], wfo_ref[...], preferred_element_type=F32)
        o_ref[r, :] = x1 + gate2 * _rms(f, g_ref[3:4, :])

    n_chunks = D_FF // FF_CHUNK
    x1_a, h_a = pre_ffn(0)
    x1_b, h_b = pre_ffn(1)
    ffn_in(0, h_a, range(n_chunks))
    ffn_out(0, x1_a)
    ffn_in(1, h_b, range(n_chunks))
    ffn_out(1, x1_b)


def _out_ffn(x2d, attn, z, mods, mod_rows, norm_g, w_o, w_ffn_in, w_ffn_out):
    n_tok = x2d.shape[0]
    z2d = z.reshape(n_tok, SSM_WIDTH)
    n_steps = n_tok // FFN_TOK
    mod_row0, n_mod = mod_rows
    steps_per_mod = n_steps // n_mod
    const = lambda i: (0, 0)
    row_spec = lambda w: pl.BlockSpec((FFN_TOK, w), lambda i: (i, 0))
    return pl.pallas_call(
        _out_ffn_kernel,
        grid=(n_steps,),
        in_specs=[row_spec(D_MODEL), row_spec(ATT_WIDTH), row_spec(SSM_WIDTH),
                  pl.BlockSpec((None, 1, N_MOD * D_MODEL),
                               lambda i: (mod_row0 + i // steps_per_mod, 0, 0)),
                  pl.BlockSpec((4, D_MODEL), const),
                  pl.BlockSpec((2 * ATT_WIDTH, D_MODEL), const, pipeline_mode=pl.Buffered(1)),
                  pl.BlockSpec((D_MODEL, 2 * D_FF), const, pipeline_mode=pl.Buffered(1)),
                  pl.BlockSpec((D_FF, D_MODEL), const, pipeline_mode=pl.Buffered(1))],
        out_specs=row_spec(D_MODEL),
        out_shape=jax.ShapeDtypeStruct((n_tok, D_MODEL), F32),
        scratch_shapes=[pltpu.VMEM((FFN_TOK, D_FF), BF16)],
        compiler_params=_cparams(1),
        name="out_proj_ffn",
    )(x2d, attn, z2d, mods, norm_g, w_o, w_ffn_in, w_ffn_out)


def _rope_tables(seq_len):
    t = np.arange(seq_len)
    row = (t // GRID_W).astype(np.float32)
    col = (t % GRID_W).astype(np.float32)
    half = HEAD_DIM // 2
    inv_freq = (np.float32(ROPE_BASE)
                ** (-np.arange(0, half, 2, dtype=np.float32) / np.float32(half))).astype(np.float32)
    ang_r = row[:, None] * inv_freq
    ang_c = col[:, None] * inv_freq
    ang = np.concatenate([ang_r, ang_r, ang_c, ang_c], axis=-1)
    cos, sin = np.cos(ang), np.sin(ang)
    upper = (np.arange(HEAD_DIM) % 32) < 16
    sa = np.where(upper, -sin, 0.0)
    sb = np.where(upper, 0.0, sin)
    two = lambda a: jnp.asarray(np.concatenate([a, a], axis=-1), dtype=F32)
    return two(cos), two(sa), two(sb)


def _layer(x, mods, mod_rows, lam_init, rope_tabs, ctx_k, ctx_v, h0, weights, prep):
    n_batch, seq_len = x.shape[:2]
    g = weights['norm_g']
    outs = _in_proj(x, mods, mod_rows, g[0:1], weights['w_in'], rope_tabs)
    q, k, v = outs[:3]
    pending = () if 'late_bf16' in weights else weights['late_f32']
    attn, cast = _attention(q, k, v, ctx_k, ctx_v, weights['lam'], weights['subln_g'],
                            n_batch, seq_len, lam_init, cast_weights=pending)
    if pending:
        weights['late_bf16'] = cast
    ssm_out = _ssm(outs[-1], *prep, weights['w_glu_t'], weights['b_glu_col'], h0, n_batch)
    y = _out_ffn(x.reshape(n_batch * seq_len, D_MODEL), attn, ssm_out[0], mods, mod_rows, g,
                 *weights['late_bf16'])
    return y.reshape(x.shape), outs[3:-1], ssm_out[1:]


def kernel(x_prompt, x_sample, cache_k, cache_v, state_ssm_re, state_ssm_im, c, c_ctx, w_mod, b_mod, norm_g, w_in, lam_params, subln_g, ssm_lambda_re, ssm_lambda_im, ssm_log_step, ssm_b_re, ssm_b_im, ssm_c_re, ssm_c_im, ssm_d, w_glu, b_glu, w_o, w_ffn_in, w_ffn_out):
    depth = w_mod.shape[0]
    assert depth == 1
    bp = x_prompt.shape[0]
    bd, ld_len = x_sample.shape[:2]
    past = cache_k.shape[2]
    xp, xs = x_prompt, x_sample
    rope_tabs = _rope_tables(ld_len)
    ks_out, vs_out, hr_out, hi_out = [], [], [], []
    for l in range(depth):
        lam_init = 0.8 - 0.6 * math.exp(-0.3 * l)
        mods = _modulation(c_ctx, c, w_mod[l], b_mod[l])
        weights = {
            'norm_g': norm_g[l],
            'w_in': w_in[l].astype(BF16),
            'lam': lam_params[l], 'subln_g': subln_g[l],
            'w_glu_t': w_glu[l].T.astype(BF16), 'b_glu_col': b_glu[l].reshape(SSM_WIDTH, 1),
            'late_f32': (w_o[l], w_ffn_in[l], w_ffn_out[l]),
        }
        prep = _ssm_prep(ssm_lambda_re[l], ssm_lambda_im[l], ssm_log_step[l],
                         ssm_b_re[l], ssm_b_im[l], ssm_c_re[l], ssm_c_im[l], ssm_d[l])
        ck = cache_k[:, l].reshape(bd, past, ATT_WIDTH)
        cv = cache_v[:, l].reshape(bd, past, ATT_WIDTH)
        h0 = jnp.stack([state_ssm_re[:, l], state_ssm_im[:, l]], axis=2)
        h0 = h0.transpose(3, 0, 1, 2, 4).reshape(N_GROUPS, bd, CW)
        xs, _, _ = _layer(xs, mods, (1, bd), lam_init, rope_tabs, ck, cv, h0, weights, prep)
        xp, (k_ctx, v_ctx), (st,) = _layer(xp, mods, (0, 1), lam_init, None, None, None, None,
                                           weights, prep)
        ks_out.append(jnp.swapaxes(k_ctx, 1, 2).reshape(bp, -1, 2 * N_HEADS, HEAD_DIM))
        vs_out.append(v_ctx)
        fin = st.reshape(N_GROUPS, bp, 2, 2, SSM_STATE).transpose(1, 2, 3, 0, 4)
        hr_out.append(fin[:, :, 0])
        hi_out.append(fin[:, :, 1])
    return (xp, xs, jnp.stack(ks_out, axis=1), jnp.stack(vs_out, axis=1),
            jnp.stack(hr_out, axis=1), jnp.stack(hi_out, axis=1))
```

```python
import functools
import math

import jax
import jax.numpy as jnp
import numpy as np
from jax import lax
from jax.experimental import pallas as pl
from jax.experimental.pallas import tpu as pltpu

F32 = jnp.float32
BF16 = jnp.bfloat16

D_MODEL = 1024
GRID_W = 64
ATT_WIDTH = 512
SSM_WIDTH = 512
HEAD_DIM = 64
N_HEADS = 4
HEAD_W = 2 * HEAD_DIM
SSM_GROUP = 16
N_GROUPS = 32
SSM_STATE = 64
D_FF = 2816
N_MOD = 6
ROPE_BASE = 10000.0
NORM_EPS = 1e-6

CHUNK = 16
CW = CHUNK * SSM_GROUP
SCAN_BLOCK = 8
SCAN_LEVELS = 3
AT_ROWS = 16
GROUP_BLOCK = 4
Z_BATCH = 8

ROWS_PER_TILE = 128
TOK_PER_TILE = ROWS_PER_TILE * CHUNK
SUB_TOK = 512
TQ_ITEM = 256
N_SUB = TOK_PER_TILE // SUB_TOK

VMEM_LIMIT = 56 * 1024 * 1024

NT_DIMS = (((1,), (1,)), ((), ()))
TN_DIMS = (((0,), (0,)), ((), ()))


def _cparams(n_axes):
    return pltpu.CompilerParams(
        dimension_semantics=("arbitrary",) * n_axes,
        vmem_limit_bytes=VMEM_LIMIT)


def _rms(x, g):
    ms = jnp.mean(x * x, axis=-1, keepdims=True)
    return x * lax.rsqrt(ms + NORM_EPS) * g


def _silu(x):
    return x * jax.nn.sigmoid(x)


MOD_ROWS = 8


def _mod_kernel(ctx_ref, c_ref, w_ref, b_ref, o_ref):
    n_lat, tk = c_ref.shape
    row = lax.broadcasted_iota(jnp.int32, (MOD_ROWS, tk), 0)
    cond = jnp.where(row == 0, ctx_ref[...], 0.0)
    for b in range(n_lat):
        cond = jnp.where(row == 1 + b, c_ref[b:b + 1, :], cond)
    part = jnp.dot(_silu(cond).astype(BF16), w_ref[...].astype(BF16),
                   preferred_element_type=F32)

    @pl.when(pl.program_id(0) == 0)
    def _():
        o_ref[:, 0, :] = part + b_ref[...]

    @pl.when(pl.program_id(0) > 0)
    def _():
        o_ref[:, 0, :] += part


def _rope(x, cos, sa, sb):
    return (x * cos + pltpu.roll(x, HEAD_W - 16, axis=1) * sa
            + pltpu.roll(x, 16, axis=1) * sb)


def _in_proj_kernel(*refs, rope, seqs_per_sub):
    x_ref, x3_hbm, mod_ref, g_ref, w_ref = refs[:5]
    refs = refs[5:]
    if rope:
        cos_ref, sa_ref, sb_ref = refs[:3]
        refs = refs[3:]
        q_ref, k_ref, v_ref, ut_ref, wut_ref, xt_ref, xt_sem = refs
    else:
        q_ref, k_ref, v_ref, kc_ref, vc_ref, ut_ref, wut_ref, xt_ref, xt_sem = refs
    tile = pl.program_id(0)
    j = pl.program_id(1)
    t_per_sub = CHUNK // N_SUB
    t_early = CHUNK - t_per_sub

    def gather(tile_idx, t):
        src = x3_hbm.at[pl.ds(tile_idx * ROWS_PER_TILE, ROWS_PER_TILE), t, :]
        return pltpu.make_async_copy(src, xt_ref.at[t], xt_sem.at[t])

    @pl.when(j == 0)
    def _():
        @pl.when(tile == 0)
        def _():
            for t in range(t_early):
                gather(0, t).start()
        for t in range(t_early, CHUNK):
            gather(tile, t).start()

    t_base = j * t_per_sub
    for d in range(t_per_sub):
        gather(tile, t_base + d).wait()

    @pl.when((j == N_SUB - 1) & (tile + 1 < pl.num_programs(0)))
    def _():
        for t in range(t_early):
            gather(tile + 1, t).start()

    @pl.when((tile == 0) & (j == 0))
    def _():
        wut_ref[...] = w_ref[:, 3 * ATT_WIDTH:].T

    shift = mod_ref[:, 0:D_MODEL]
    gain = g_ref[...] * (1.0 + mod_ref[:, D_MODEL:2 * D_MODEL])

    def norm_mod(xv):
        ms = jnp.mean(xv * xv, axis=-1, keepdims=True)
        return (xv * lax.rsqrt(ms + NORM_EPS) * gain + shift).astype(BF16)

    def ssm_input(d0):
        xt = jnp.concatenate([xt_ref[t_base + d0], xt_ref[t_base + d0 + 1]], axis=0)
        ut = lax.dot_general(wut_ref[...], norm_mod(xt), NT_DIMS,
                             preferred_element_type=F32)
        for d in range(2):
            blk = ut[:, d * ROWS_PER_TILE:(d + 1) * ROWS_PER_TILE]
            row0 = pl.multiple_of((t_base + d0 + d) * SSM_GROUP, SSM_GROUP)
            ut_ref[:, pl.ds(row0, SSM_GROUP), :] = (
                blk.reshape(N_GROUPS, SSM_GROUP, ROWS_PER_TILE).astype(ut_ref.dtype))

    proj = jnp.dot(norm_mod(x_ref[...]), w_ref[:, 0:3 * ATT_WIDTH],
                   preferred_element_type=F32)
    q = proj[:, 0:ATT_WIDTH]
    k = proj[:, ATT_WIDTH:2 * ATT_WIDTH]
    v = proj[:, 2 * ATT_WIDTH:3 * ATT_WIDTH]
    qscale = HEAD_DIM ** -0.5 * math.log2(math.e)
    if rope:
        cos, sa, sb = cos_ref[...], sa_ref[...], sb_ref[...]
        for hd in range(N_HEADS):
            sl = slice(hd * HEAD_W, (hd + 1) * HEAD_W)
            q_ref[:, sl] = (_rope(q[:, sl], cos, sa, sb) * qscale).astype(q_ref.dtype)
            k_ref[:, sl] = _rope(k[:, sl], cos, sa, sb).astype(k_ref.dtype)
    else:
        q_ref[...] = (q * qscale).astype(q_ref.dtype)
        k_ref[...] = k.astype(k_ref.dtype)
        seq = SUB_TOK // seqs_per_sub
        k_t = k.T
        for b in range(seqs_per_sub):
            kc_ref[b] = k_t[:, b * seq:(b + 1) * seq]
            for hd in range(N_HEADS):
                vc_ref[b, :, hd, :] = v[b * seq:(b + 1) * seq, hd * HEAD_W:(hd + 1) * HEAD_W]
    v_ref[...] = v.astype(v_ref.dtype)

    for d0 in range(0, t_per_sub, 2):
        ssm_input(d0)


def _in_proj(x, mods, mod_rows, g0, w_in, rope_tabs):
    n_batch, seq_len = x.shape[:2]
    n_tok = n_batch * seq_len
    n_rows = n_tok // CHUNK
    n_tiles = n_tok // TOK_PER_TILE
    mod_row0, n_mod = mod_rows
    tiles_per_mod = n_tiles // n_mod
    rope = rope_tabs is not None
    seqs_per_sub = max(1, SUB_TOK // seq_len)
    in_specs = [pl.BlockSpec((SUB_TOK, D_MODEL), lambda i, j: (i * N_SUB + j, 0)),
                pl.BlockSpec(memory_space=pl.ANY),
                pl.BlockSpec((None, 1, 2 * D_MODEL),
                             lambda i, j: (mod_row0 + i // tiles_per_mod, 0, 0)),
                pl.BlockSpec((1, D_MODEL), lambda i, j: (0, 0)),
                pl.BlockSpec((D_MODEL, 4 * ATT_WIDTH), lambda i, j: (0, 0))]
    args = [x.reshape(n_tok, D_MODEL), x.reshape(n_rows, CHUNK, D_MODEL), mods, g0, w_in]
    row_spec = pl.BlockSpec((SUB_TOK, ATT_WIDTH), lambda i, j: (i * N_SUB + j, 0))
    row_shape = jax.ShapeDtypeStruct((n_tok, ATT_WIDTH), BF16)
    out_specs = [row_spec, row_spec, row_spec]
    out_shape = [row_shape, row_shape, row_shape]
    if rope:
        assert seq_len == TOK_PER_TILE
        for tab in rope_tabs:
            in_specs.append(pl.BlockSpec((SUB_TOK, HEAD_W), lambda i, j: (j, 0)))
            args.append(tab)
    else:
        out_specs += [pl.BlockSpec((seqs_per_sub, ATT_WIDTH, seq_len),
                                   lambda i, j: (i * N_SUB + j, 0, 0)),
                      pl.BlockSpec((seqs_per_sub, seq_len, N_HEADS, HEAD_W),
                                   lambda i, j: (i * N_SUB + j, 0, 0, 0))]
        out_shape += [jax.ShapeDtypeStruct((n_batch, ATT_WIDTH, seq_len), F32),
                      jax.ShapeDtypeStruct((n_batch, seq_len, N_HEADS, HEAD_W), F32)]
    out_specs.append(pl.BlockSpec((N_GROUPS, CW, ROWS_PER_TILE), lambda i, j: (0, 0, i)))
    out_shape.append(jax.ShapeDtypeStruct((N_GROUPS, CW, n_rows), BF16))
    return pl.pallas_call(
        functools.partial(_in_proj_kernel, rope=rope, seqs_per_sub=seqs_per_sub),
        grid=(n_tiles, N_SUB),
        in_specs=in_specs,
        out_specs=out_specs,
        out_shape=out_shape,
        scratch_shapes=[pltpu.VMEM((SSM_WIDTH, D_MODEL), BF16),
                        pltpu.VMEM((CHUNK, ROWS_PER_TILE, D_MODEL), F32),
                        pltpu.SemaphoreType.DMA((CHUNK,))],
        compiler_params=_cparams(2),
        name="in_proj",
    )(*args)


def _attn_kernel(*refs, has_ctx, lam_init, n_seq, seq_len, tq, n_cast):
    if n_cast:
        cast_in = refs[len(refs) - 2 * n_cast - 1:len(refs) - n_cast - 1]
        cast_out = refs[len(refs) - n_cast:]
        refs = refs[:len(refs) - 2 * n_cast - 1] + (refs[len(refs) - n_cast - 1],)
        for src, dst in zip(cast_in, cast_out):
            dst[...] = src[...].astype(dst.dtype)
    if has_ctx:
        lam_ref, sg_ref, q_ref, ck_ref, cv_ref, k_ref, v_ref, o_ref = refs
    else:
        lam_ref, sg_ref, q_ref, k_ref, v_ref, o_ref = refs
    lp = lam_ref[...]
    lam = (jnp.exp(jnp.sum(lp[0:1] * lp[1:2], axis=-1, keepdims=True))
           - jnp.exp(jnp.sum(lp[2:3] * lp[3:4], axis=-1, keepdims=True)) + lam_init)
    first_map = lax.broadcasted_iota(jnp.int32, (1, HEAD_W), 1) < HEAD_DIM
    ti = min(TQ_ITEM, tq)
    for row0 in range(0, n_seq * tq, ti):
        b = row0 // tq
        q_rows = slice(row0, row0 + ti)
        for hd in range(N_HEADS):
            sl = slice(hd * HEAD_W, (hd + 1) * HEAD_W)
            qh = q_ref[q_rows, sl]
            zero = jnp.zeros_like(qh)
            qs = jnp.concatenate([jnp.where(first_map, qh, zero),
                                  jnp.where(first_map, zero, qh)], axis=0)
            kv_rows = slice(b * seq_len, (b + 1) * seq_len)
            parts = [(k_ref[kv_rows, sl], v_ref[kv_rows, sl])]
            if has_ctx:
                parts.insert(0, (ck_ref[:, sl].astype(BF16), cv_ref[:, sl].astype(BF16)))
            scores = [lax.dot_general(qs, kk, NT_DIMS, preferred_element_type=F32)
                      for kk, _ in parts]
            mx = scores[0].max(axis=-1, keepdims=True)
            for s in scores[1:]:
                mx = jnp.maximum(mx, s.max(axis=-1, keepdims=True))
            acc = None
            for s, (_, vv) in zip(scores, parts):
                e = jnp.exp2(s - mx).astype(BF16)
                v_one = jnp.concatenate([vv, jnp.ones_like(vv)], axis=1)
                pv = jnp.dot(e, v_one, preferred_element_type=F32)
                acc = pv if acc is None else acc + pv
            num = acc[:, 0:HEAD_W] / acc[:, HEAD_W:2 * HEAD_W]
            o = num[0:ti] - lam * num[ti:2 * ti]
            o = _rms(o, sg_ref[...]) * (1.0 - lam_init)
            o_ref[q_rows, sl] = o.astype(o_ref.dtype)


def _attention(q, k, v, ctx_k, ctx_v, lam_params, subln_g, n_batch, seq_len, lam_init,
               cast_weights=()):
    has_ctx = ctx_k is not None
    tq = min(1024, seq_len)
    n_q = seq_len // tq
    n_seq = 1 if n_q > 1 else min(4, n_batch)
    in_specs = [pl.BlockSpec((4, HEAD_DIM), lambda b, i: (0, 0)),
                pl.BlockSpec((1, HEAD_W), lambda b, i: (0, 0)),
                pl.BlockSpec((n_seq * tq, ATT_WIDTH), lambda b, i: (b * n_q + i, 0))]
    args = [lam_params, subln_g.reshape(1, HEAD_W), q]
    if has_ctx:
        past = ctx_k.shape[1]
        ctx_spec = pl.BlockSpec((None, past, ATT_WIDTH), lambda b, i: (b, 0, 0))
        in_specs += [ctx_spec, ctx_spec]
        args += [ctx_k, ctx_v]
    kv_spec = pl.BlockSpec((n_seq * seq_len, ATT_WIDTH), lambda b, i: (b, 0))
    in_specs += [kv_spec, kv_spec]
    args += [k, v]
    out_specs = [pl.BlockSpec((n_seq * tq, ATT_WIDTH), lambda b, i: (b * n_q + i, 0))]
    out_shape = [jax.ShapeDtypeStruct((n_batch * seq_len, ATT_WIDTH), BF16)]
    n_steps = (n_batch // n_seq) * n_q
    for w in cast_weights:
        rows = w.shape[0] // n_steps
        spec = pl.BlockSpec((rows, w.shape[1]), lambda b, i: (b * n_q + i, 0))
        in_specs.append(spec)
        args.append(w)
        out_specs.append(spec)
        out_shape.append(jax.ShapeDtypeStruct(w.shape, BF16))
    outs = pl.pallas_call(
        functools.partial(_attn_kernel, has_ctx=has_ctx, lam_init=lam_init,
                          n_seq=n_seq, seq_len=seq_len, tq=tq, n_cast=len(cast_weights)),
        grid=(n_batch // n_seq, n_q),
        in_specs=in_specs,
        out_specs=out_specs,
        out_shape=out_shape,
        compiler_params=_cparams(2),
        name="diff_attention",
    )(*args)
    return outs[0], tuple(outs[1:])


def _cmul(ar, ai, br, bi):
    return ar * br - ai * bi, ar * bi + ai * br


def _ssm_prep_kernel(lre_ref, lim_ref, ls_ref, bre_ref, bim_ref, cre_ref, cim_ref, d_ref,
                     mt_ref, gt_ref, wo_ref, at_ref):
    lane = lax.broadcasted_iota(jnp.int32, (SSM_GROUP, CW), 1)
    chan = lax.broadcasted_iota(jnp.int32, (SSM_GROUP, CW), 0)
    for gi in range(GROUP_BLOCK):
        gt_cols, wo_cols, at_cols, toeplitz = [], [], [], []
        for dr in range(2):
            lr = jnp.minimum(lre_ref[dr, gi], -1e-4)
            li = lim_ref[dr, gi]
            step = jnp.exp(ls_ref[dr, gi])
            mag = jnp.exp(lr * step)
            a_re = mag * jnp.cos(li * step)
            a_im = mag * jnp.sin(li * step)
            den = lr * lr + li * li
            nr = a_re - 1.0
            f_re = (nr * lr + a_im * li) / den
            f_im = (a_im * lr - nr * li) / den
            bt_re, bt_im = bre_ref[dr, gi], bim_ref[dr, gi]
            bb_re, bb_im = _cmul(f_re, f_im, bt_re, bt_im)
            c_re, c_im = cre_ref[dr, gi], cim_ref[dr, gi]
            pw = [(jnp.ones_like(a_re), jnp.zeros_like(a_im))]
            for _ in range(CHUNK):
                pw.append(_cmul(pw[-1][0], pw[-1][1], a_re, a_im))
            g_re, g_im, e_re, e_im = [], [], [], []
            for t in range(CHUNK):
                pr, pi = pw[CHUNK - 1 - t] if dr == 0 else pw[t]
                r, i = _cmul(bb_re, bb_im, pr, pi)
                g_re.append(r)
                g_im.append(i)
                pr, pi = pw[t + 1] if dr == 0 else pw[CHUNK - t]
                r, i = _cmul(c_re, c_im, pr, pi)
                e_re.append(r)
                e_im.append(-i)
            g_cat = jnp.concatenate([jnp.concatenate(g_re, axis=0),
                                     jnp.concatenate(g_im, axis=0)], axis=1)
            gt_cols.append(g_cat)
            wo_cols.append(jnp.concatenate([jnp.concatenate(e_re, axis=0),
                                            jnp.concatenate(e_im, axis=0)], axis=1))
            c_cat = jnp.concatenate([c_re, -c_im], axis=1)
            toeplitz.append(lax.dot_general(c_cat, g_cat, NT_DIMS,
                                            precision=lax.Precision.HIGHEST,
                                            preferred_element_type=F32))
            apw = [pw[CHUNK]]
            for _ in range(SCAN_BLOCK - 1):
                apw.append(_cmul(apw[-1][0], apw[-1][1], apw[0][0], apw[0][1]))
            order = list(range(SCAN_BLOCK)) if dr == 0 else list(range(SCAN_BLOCK - 1, -1, -1))
            order += [2 ** l - 1 for l in range(SCAN_LEVELS)]
            order += [0] * (AT_ROWS - len(order))
            at_cols += [jnp.concatenate([jnp.concatenate([apw[i][0], apw[i][0]], axis=1)
                                         for i in order], axis=0),
                        jnp.concatenate([jnp.concatenate([-apw[i][1], apw[i][1]], axis=1)
                                         for i in order], axis=0)]
        kf_rev, kb = toeplitz
        d_skip = d_ref[gi]
        blocks = []
        for t in range(CHUNK):
            fwd = pltpu.roll(kf_rev, (CW - (CHUNK - 1 - t) * SSM_GROUP) % CW, axis=1)
            bwd = pltpu.roll(kb, t * SSM_GROUP, axis=1)
            blocks.append(jnp.where(lane < (t + 1) * SSM_GROUP, fwd, 0.0)
                          + jnp.where(lane >= t * SSM_GROUP, bwd, 0.0)
                          + jnp.where(lane == chan + t * SSM_GROUP, d_skip, 0.0))
        mt_ref[gi] = jnp.concatenate(blocks, axis=0).astype(mt_ref.dtype)
        gt_ref[gi] = jnp.concatenate(gt_cols, axis=1).astype(gt_ref.dtype)
        wo_ref[gi] = jnp.concatenate(wo_cols, axis=1).astype(wo_ref.dtype)
        at_ref[gi] = jnp.concatenate(at_cols, axis=1)


N_PREP_IN, N_PREP_OUT, N_MOD_IN = 8, 4, 4


def _prep_mod_kernel(*refs):
    prep_in = refs[:N_PREP_IN]
    mod_in = refs[N_PREP_IN:N_PREP_IN + N_MOD_IN]
    prep_out = refs[N_PREP_IN + N_MOD_IN:N_PREP_IN + N_MOD_IN + N_PREP_OUT]
    _ssm_prep_kernel(*prep_in, *prep_out)
    _mod_kernel(*mod_in, refs[-1])


def _ssm_prep_and_modulation(lam_re, lam_im, log_step, b_re, b_im, c_re, c_im, d_skip,
                             c_ctx, c, w_mod, b_mod):
    row = lambda a: a.reshape(2, N_GROUPS, 1, SSM_STATE)
    bt = lambda a: jnp.swapaxes(a, 2, 3)
    d_row = jnp.tile((d_skip[0] + d_skip[1]).reshape(N_GROUPS, 1, SSM_GROUP), (1, 1, CHUNK))
    gb = GROUP_BLOCK
    n_steps = N_GROUPS // gb
    vec_spec = pl.BlockSpec((2, gb, 1, SSM_STATE), lambda i: (0, i, 0, 0))
    mat_spec = pl.BlockSpec((2, gb, SSM_GROUP, SSM_STATE), lambda i: (0, i, 0, 0))
    w_spec = pl.BlockSpec((gb, CW, CW), lambda i: (i, 0, 0))
    w_shape = jax.ShapeDtypeStruct((N_GROUPS, CW, CW), BF16)
    n_mod = w_mod.shape[1]
    tk = D_MODEL // n_steps
    assert 1 + c.shape[0] <= MOD_ROWS
    outs = pl.pallas_call(
        _prep_mod_kernel,
        grid=(n_steps,),
        in_specs=[vec_spec, vec_spec,
                  pl.BlockSpec((2, gb, 1, 1), lambda i: (0, i, 0, 0)),
                  mat_spec, mat_spec, mat_spec, mat_spec,
                  pl.BlockSpec((gb, 1, CW), lambda i: (i, 0, 0)),
                  pl.BlockSpec((1, tk), lambda k: (0, k)),
                  pl.BlockSpec((c.shape[0], tk), lambda k: (0, k)),
                  pl.BlockSpec((tk, n_mod), lambda k: (k, 0)),
                  pl.BlockSpec((1, n_mod), lambda k: (0, 0))],
        out_specs=[w_spec, w_spec, w_spec,
                   pl.BlockSpec((gb, AT_ROWS, 4 * 2 * SSM_STATE), lambda i: (i, 0, 0)),
                   pl.BlockSpec((MOD_ROWS, 1, n_mod), lambda k: (0, 0, 0))],
        out_shape=[w_shape, w_shape, w_shape,
                   jax.ShapeDtypeStruct((N_GROUPS, AT_ROWS, 4 * 2 * SSM_STATE), F32),
                   jax.ShapeDtypeStruct((MOD_ROWS, 1, n_mod), F32)],
        compiler_params=_cparams(1),
        name="ssm_prep_modulation",
    )(row(lam_re), row(lam_im), log_step.reshape(2, N_GROUPS, 1, 1),
      bt(b_re), bt(b_im), c_re, c_im, d_row,
      c_ctx.reshape(1, D_MODEL), c, w_mod, b_mod.reshape(1, n_mod))
    return outs[:N_PREP_OUT], outs[N_PREP_OUT]


def _shift_rows(x, m, down):
    n = x.shape[0]
    return pltpu.roll(x, m if down else n - m, axis=0)


def _ssm_kernel(*refs, n_seq, has_h0):
    if has_h0:
        (xt_ref, mt_ref, gt_ref, wo_ref, at_ref, wg_ref, bg_ref, h0_ref,
         z_hbm, zs_ref, zb_ref, zb_sem) = refs
    else:
        (xt_ref, mt_ref, gt_ref, wo_ref, at_ref, wg_ref, bg_ref,
         z_hbm, st_ref, zs_ref, zb_ref, zb_sem, fin_ref) = refs
    step = pl.program_id(0)
    n_rows = xt_ref.shape[-1]
    seg = n_rows // n_seq
    cw2 = 2 * SSM_STATE
    assert seg % SCAN_BLOCK == 0
    n_blk = n_rows // SCAN_BLOCK
    blk_per_seq = seg // SCAN_BLOCK
    pos = lax.broadcasted_iota(jnp.int32, (n_rows, cw2), 0) % seg
    row_blk = lax.broadcasted_iota(jnp.int32, (SCAN_BLOCK, cw2), 0)

    def low_half(shape):
        return lax.broadcasted_iota(jnp.int32, shape, 1) < SSM_STATE

    def swap(v):
        return pltpu.roll(v, SSM_STATE, axis=1)

    def to_planes(va, vb):
        lo = low_half((va.shape[0], cw2))
        va_l = pltpu.roll(va, 3 * SSM_STATE, axis=1)
        vb_r = pltpu.roll(vb, SSM_STATE, axis=1)
        return (jnp.where(lo, va[:, 0:cw2], vb_r[:, 0:cw2]),
                jnp.where(lo, va_l[:, 0:cw2], vb[:, 0:cw2]),
                jnp.where(lo, va[:, cw2:2 * cw2], vb_r[:, cw2:2 * cw2]),
                jnp.where(lo, va_l[:, cw2:2 * cw2], vb[:, cw2:2 * cw2]))

    def from_planes(f_re, f_im, b_re, b_im):
        lo = low_half(f_re.shape)
        va = jnp.concatenate([jnp.where(lo, f_re, swap(f_im)), jnp.where(lo, b_re, swap(b_im))], axis=1)
        vb = jnp.concatenate([jnp.where(lo, swap(f_re), f_im), jnp.where(lo, swap(b_re), b_im)], axis=1)
        return va, vb

    for ga in range(0, GROUP_BLOCK, 2):
        gb = ga + 1
        xts = [xt_ref[ga], xt_ref[gb]]
        s_pair = [lax.dot_general(xts[i], gt_ref[g], TN_DIMS, preferred_element_type=F32)
                  for i, g in enumerate((ga, gb))]
        planes = to_planes(*s_pair)
        lo_t = low_half((AT_ROWS, cw2))
        if has_h0:
            h0_planes = to_planes(h0_ref[ga], h0_ref[gb])
        ent_planes = []
        for dr in range(2):
            down = dr == 0
            pa, pb = (at_ref[g][:, (2 * dr) * cw2:(2 * dr + 1) * cw2] for g in (ga, gb))
            qa, qb = (at_ref[g][:, (2 * dr + 1) * cw2:(2 * dr + 2) * cw2] for g in (ga, gb))
            ar_tab = jnp.where(lo_t, pa, pb)
            ai_tab = jnp.where(lo_t, -qa, qb)
            re3 = planes[2 * dr].reshape(n_blk, SCAN_BLOCK, cw2)
            im3 = planes[2 * dr + 1].reshape(n_blk, SCAN_BLOCK, cw2)
            for lvl in range(SCAN_LEVELS):
                m = 2 ** lvl
                valid = (row_blk >= m) if down else (row_blk < SCAN_BLOCK - m)
                row = SCAN_BLOCK + lvl
                ar = jnp.where(valid, ar_tab[row:row + 1], 0.0)
                ai = jnp.where(valid, ai_tab[row:row + 1], 0.0)
                shift = m if down else SCAN_BLOCK - m
                sh_re = pltpu.roll(re3, shift, axis=1)
                sh_im = pltpu.roll(im3, shift, axis=1)
                re3, im3 = re3 + ar * sh_re - ai * sh_im, im3 + ar * sh_im + ai * sh_re
            ar_blk, ai_blk = ar_tab[0:SCAN_BLOCK], ai_tab[0:SCAN_BLOCK]
            blk_re = [re3[i] for i in range(n_blk)]
            blk_im = [im3[i] for i in range(n_blk)]
            edge = slice(SCAN_BLOCK - 1, SCAN_BLOCK) if down else slice(0, 1)
            for q_i in range(n_seq):
                idxs = list(range(q_i * blk_per_seq, (q_i + 1) * blk_per_seq))
                idxs = idxs if down else idxs[::-1]
                for prev, cur in zip([None] + idxs[:-1], idxs):
                    if prev is not None:
                        c_re, c_im = blk_re[prev][edge], blk_im[prev][edge]
                    elif has_h0:
                        c_re = h0_planes[2 * dr][q_i:q_i + 1]
                        c_im = h0_planes[2 * dr + 1][q_i:q_i + 1]
                    else:
                        continue
                    c_re = jnp.broadcast_to(c_re, (SCAN_BLOCK, cw2))
                    c_im = jnp.broadcast_to(c_im, (SCAN_BLOCK, cw2))
                    blk_re[cur] = blk_re[cur] + ar_blk * c_re - ai_blk * c_im
                    blk_im[cur] = blk_im[cur] + ar_blk * c_im + ai_blk * c_re
            for part, blks in ((0, blk_re), (1, blk_im)):
                s = jnp.concatenate(blks, axis=0)
                if not has_h0:
                    fin_ref[part] = s
                ent = _shift_rows(s, 1, down)
                ent = jnp.where((pos >= 1) if down else (pos < seg - 1), ent, 0.0)
                if has_h0:
                    ent_blk = [ent[i * SCAN_BLOCK:(i + 1) * SCAN_BLOCK] for i in range(n_blk)]
                    at_edge = row_blk == (0 if down else SCAN_BLOCK - 1)
                    for q_i in range(n_seq):
                        bi = q_i * blk_per_seq if down else (q_i + 1) * blk_per_seq - 1
                        ent_blk[bi] = jnp.where(
                            at_edge, h0_planes[2 * dr + part][q_i:q_i + 1], ent_blk[bi])
                    ent = jnp.concatenate(ent_blk, axis=0)
                ent_planes.append(ent)
            if not has_h0:
                rows = pl.ds(seg - 1 if down else 0, n_seq, stride=seg)
                f_re, f_im = fin_ref[0, rows, :], fin_ref[1, rows, :]
                lo_s = low_half((n_seq, cw2))
                st_ref[ga, :, dr * cw2:(dr + 1) * cw2] = jnp.where(lo_s, f_re, swap(f_im))
                st_ref[gb, :, dr * cw2:(dr + 1) * cw2] = jnp.where(lo_s, swap(f_re), f_im)
        h_pair = from_planes(*ent_planes)
        for i, g in enumerate((ga, gb)):
            yt = (jnp.dot(mt_ref[g], xts[i], preferred_element_type=F32)
                  + lax.dot_general(wo_ref[g], h_pair[i].astype(BF16), NT_DIMS,
                                    preferred_element_type=F32))
            z = jax.nn.gelu(yt, approximate=True)
            grp = step * GROUP_BLOCK + g
            for t in range(CHUNK):
                zs_ref[t, pl.ds(pl.multiple_of(grp * SSM_GROUP, SSM_GROUP), SSM_GROUP), :] = (
                    z[t * SSM_GROUP:(t + 1) * SSM_GROUP, :])

    @pl.when(step == pl.num_programs(0) - 1)
    def _():
        def put(t):
            return pltpu.make_async_copy(zb_ref.at[t], z_hbm.at[:, t, :], zb_sem.at[t])

        for t0 in range(0, CHUNK, Z_BATCH):
            for t in range(t0, t0 + Z_BATCH):
                zt = zs_ref[t]
                gate = jnp.dot(wg_ref[...], zt.astype(BF16),
                               preferred_element_type=F32) + bg_ref[...]
                zb_ref[t] = (zt * jax.nn.sigmoid(gate)).T
            for t in range(t0, t0 + Z_BATCH):
                put(t).start()
        for t in range(CHUNK):
            put(t).wait()


def _ssm(xt, mt, gt, wo, at, w_glu_t, b_glu_col, h0, n_seq):
    n_rows = xt.shape[-1]
    has_h0 = h0 is not None
    gb = GROUP_BLOCK
    w_spec = pl.BlockSpec((gb, CW, CW), lambda i: (i, 0, 0))
    in_specs = [pl.BlockSpec((gb, CW, n_rows), lambda i: (i, 0, 0)),
                w_spec, w_spec, w_spec,
                pl.BlockSpec((gb, AT_ROWS, 4 * 2 * SSM_STATE), lambda i: (i, 0, 0)),
                pl.BlockSpec((SSM_WIDTH, SSM_WIDTH), lambda i: (0, 0)),
                pl.BlockSpec((SSM_WIDTH, 1), lambda i: (0, 0))]
    args = [xt, mt, gt, wo, at, w_glu_t, b_glu_col]
    out_specs = [pl.BlockSpec(memory_space=pl.ANY)]
    out_shape = [jax.ShapeDtypeStruct((n_rows, CHUNK, SSM_WIDTH), F32)]
    scratch = [pltpu.VMEM((CHUNK, SSM_WIDTH, n_rows), F32),
               pltpu.VMEM((CHUNK, n_rows, SSM_WIDTH), F32),
               pltpu.SemaphoreType.DMA((CHUNK,))]
    if has_h0:
        in_specs.append(pl.BlockSpec((gb, n_seq, CW), lambda i: (i, 0, 0)))
        args.append(h0)
    else:
        out_specs.append(pl.BlockSpec((gb, n_seq, CW), lambda i: (i, 0, 0)))
        out_shape.append(jax.ShapeDtypeStruct((N_GROUPS, n_seq, CW), F32))
        scratch.append(pltpu.VMEM((2, n_rows, 2 * SSM_STATE), F32))
    return pl.pallas_call(
        functools.partial(_ssm_kernel, n_seq=n_seq, has_h0=has_h0),
        grid=(N_GROUPS // gb,),
        in_specs=in_specs,
        out_specs=out_specs,
        out_shape=out_shape,
        scratch_shapes=scratch,
        compiler_params=_cparams(1),
        name="ssm_scan_glu",
    )(*args)


FF_CHUNK = 256
FFN_TOK = 1024


def _out_ffn_kernel(x_ref, attn_ref, z_ref, mod_ref, g_ref, wo_ref, wfi_ref, wfo_ref,
                    o_ref, act_ref):
    gate1 = mod_ref[:, 2 * D_MODEL:3 * D_MODEL]
    shift2 = mod_ref[:, 3 * D_MODEL:4 * D_MODEL]
    scale2 = mod_ref[:, 4 * D_MODEL:5 * D_MODEL]
    gate2 = mod_ref[:, 5 * D_MODEL:6 * D_MODEL]
    half = FFN_TOK // 2
    gain2 = g_ref[2:3, :] * (1.0 + scale2)

    def pre_ffn(hf):
        r = slice(hf * half, (hf + 1) * half)
        mixer = jnp.concatenate([attn_ref[r, :], z_ref[r, :].astype(BF16)], axis=1)
        mix = jnp.dot(mixer, wo_ref[...], preferred_element_type=F32)
        x1 = x_ref[r, :] + gate1 * _rms(mix, g_ref[1:2, :])
        ms = jnp.mean(x1 * x1, axis=-1, keepdims=True)
        return x1, (x1 * lax.rsqrt(ms + NORM_EPS) * gain2 + shift2).astype(BF16)

    def ffn_in(hf, h, chunks):
        r = slice(hf * half, (hf + 1) * half)
        for c in chunks:
            lo = c * FF_CHUNK
            gt = jnp.dot(h, wfi_ref[:, lo:lo + FF_CHUNK], preferred_element_type=F32)
            up = jnp.dot(h, wfi_ref[:, D_FF + lo:D_FF + lo + FF_CHUNK],
                         preferred_element_type=F32)
            act_ref[r, lo:lo + FF_CHUNK] = (_silu(gt) * up).astype(BF16)

    def ffn_out(hf, x1):
        r = slice(hf * half, (hf + 1) * half)
        f = jnp.dot(act_ref[r, :], wfo_ref[...], preferred_element_type=F32)
        o_ref[r, :] = x1 + gate2 * _rms(f, g_ref[3:4, :])

    n_chunks = D_FF // FF_CHUNK
    x1_a, h_a = pre_ffn(0)
    x1_b, h_b = pre_ffn(1)
    ffn_in(0, h_a, range(n_chunks))
    ffn_out(0, x1_a)
    ffn_in(1, h_b, range(n_chunks))
    ffn_out(1, x1_b)


def _out_ffn(x2d, attn, z, mods, mod_rows, norm_g, w_o, w_ffn_in, w_ffn_out):
    n_tok = x2d.shape[0]
    z2d = z.reshape(n_tok, SSM_WIDTH)
    n_steps = n_tok // FFN_TOK
    mod_row0, n_mod = mod_rows
    steps_per_mod = n_steps // n_mod
    const = lambda i: (0, 0)
    row_spec = lambda w: pl.BlockSpec((FFN_TOK, w), lambda i: (i, 0))
    return pl.pallas_call(
        _out_ffn_kernel,
        grid=(n_steps,),
        in_specs=[row_spec(D_MODEL), row_spec(ATT_WIDTH), row_spec(SSM_WIDTH),
                  pl.BlockSpec((None, 1, N_MOD * D_MODEL),
                               lambda i: (mod_row0 + i // steps_per_mod, 0, 0)),
                  pl.BlockSpec((4, D_MODEL), const),
                  pl.BlockSpec((2 * ATT_WIDTH, D_MODEL), const, pipeline_mode=pl.Buffered(1)),
                  pl.BlockSpec((D_MODEL, 2 * D_FF), const, pipeline_mode=pl.Buffered(1)),
                  pl.BlockSpec((D_FF, D_MODEL), const, pipeline_mode=pl.Buffered(1))],
        out_specs=row_spec(D_MODEL),
        out_shape=jax.ShapeDtypeStruct((n_tok, D_MODEL), F32),
        scratch_shapes=[pltpu.VMEM((FFN_TOK, D_FF), BF16)],
        compiler_params=_cparams(1),
        name="out_proj_ffn",
    )(x2d, attn, z2d, mods, norm_g, w_o, w_ffn_in, w_ffn_out)


def _rope_tables(seq_len):
    t = np.arange(seq_len)
    row = (t // GRID_W).astype(np.float32)
    col = (t % GRID_W).astype(np.float32)
    half = HEAD_DIM // 2
    inv_freq = (np.float32(ROPE_BASE)
                ** (-np.arange(0, half, 2, dtype=np.float32) / np.float32(half))).astype(np.float32)
    ang_r = row[:, None] * inv_freq
    ang_c = col[:, None] * inv_freq
    ang = np.concatenate([ang_r, ang_r, ang_c, ang_c], axis=-1)
    cos, sin = np.cos(ang), np.sin(ang)
    upper = (np.arange(HEAD_DIM) % 32) < 16
    sa = np.where(upper, -sin, 0.0)
    sb = np.where(upper, 0.0, sin)
    two = lambda a: jnp.asarray(np.concatenate([a, a], axis=-1), dtype=F32)
    return two(cos), two(sa), two(sb)


def _layer(x, mods, mod_rows, lam_init, rope_tabs, ctx_k, ctx_v, h0, weights, prep):
    n_batch, seq_len = x.shape[:2]
    g = weights['norm_g']
    outs = _in_proj(x, mods, mod_rows, g[0:1], weights['w_in'], rope_tabs)
    q, k, v = outs[:3]
    pending = () if 'late_bf16' in weights else weights['late_f32']
    attn, cast = _attention(q, k, v, ctx_k, ctx_v, weights['lam'], weights['subln_g'],
                            n_batch, seq_len, lam_init, cast_weights=pending)
    if pending:
        weights['late_bf16'] = cast
    ssm_out = _ssm(outs[-1], *prep, weights['w_glu_t'], weights['b_glu_col'], h0, n_batch)
    y = _out_ffn(x.reshape(n_batch * seq_len, D_MODEL), attn, ssm_out[0], mods, mod_rows, g,
                 *weights['late_bf16'])
    return y.reshape(x.shape), outs[3:-1], ssm_out[1:]


def kernel(x_prompt, x_sample, cache_k, cache_v, state_ssm_re, state_ssm_im, c, c_ctx, w_mod, b_mod, norm_g, w_in, lam_params, subln_g, ssm_lambda_re, ssm_lambda_im, ssm_log_step, ssm_b_re, ssm_b_im, ssm_c_re, ssm_c_im, ssm_d, w_glu, b_glu, w_o, w_ffn_in, w_ffn_out):
    depth = w_mod.shape[0]
    assert depth == 1
    bp = x_prompt.shape[0]
    bd, ld_len = x_sample.shape[:2]
    past = cache_k.shape[2]
    xp, xs = x_prompt, x_sample
    rope_tabs = _rope_tables(ld_len)
    ks_out, vs_out, hr_out, hi_out = [], [], [], []
    for l in range(depth):
        lam_init = 0.8 - 0.6 * math.exp(-0.3 * l)
        prep, mods = _ssm_prep_and_modulation(
            ssm_lambda_re[l], ssm_lambda_im[l], ssm_log_step[l], ssm_b_re[l], ssm_b_im[l],
            ssm_c_re[l], ssm_c_im[l], ssm_d[l], c_ctx, c, w_mod[l], b_mod[l])
        weights = {
            'norm_g': norm_g[l],
            'w_in': w_in[l].astype(BF16),
            'lam': lam_params[l], 'subln_g': subln_g[l],
            'w_glu_t': w_glu[l].T.astype(BF16), 'b_glu_col': b_glu[l].reshape(SSM_WIDTH, 1),
            'late_f32': (w_o[l], w_ffn_in[l], w_ffn_out[l]),
        }
        ck = cache_k[:, l].reshape(bd, past, ATT_WIDTH)
        cv = cache_v[:, l].reshape(bd, past, ATT_WIDTH)
        h0 = jnp.stack([state_ssm_re[:, l], state_ssm_im[:, l]], axis=2)
        h0 = h0.transpose(3, 0, 1, 2, 4).reshape(N_GROUPS, bd, CW)
        xs, _, _ = _layer(xs, mods, (1, bd), lam_init, rope_tabs, ck, cv, h0, weights, prep)
        xp, (k_ctx, v_ctx), (st,) = _layer(xp, mods, (0, 1), lam_init, None, None, None, None,
                                           weights, prep)
        ks_out.append(jnp.swapaxes(k_ctx, 1, 2).reshape(bp, -1, 2 * N_HEADS, HEAD_DIM))
        vs_out.append(v_ctx)
        fin = st.reshape(N_GROUPS, bp, 2, 2, SSM_STATE).transpose(1, 2, 3, 0, 4)
        hr_out.append(fin[:, :, 0])
        hi_out.append(fin[:, :, 1])
    return (xp, xs, jnp.stack(ks_out, axis=1), jnp.stack(vs_out, axis=1),
            jnp.stack(hr_out, axis=1), jnp.stack(hi_out, axis=1))
```

```python
import functools
import math

import jax
import jax.numpy as jnp
import numpy as np
from jax import lax
from jax.experimental import pallas as pl
from jax.experimental.pallas import tpu as pltpu

F32 = jnp.float32
BF16 = jnp.bfloat16

D_MODEL = 1024
GRID_W = 64
ATT_WIDTH = 512
SSM_WIDTH = 512
HEAD_DIM = 64
N_HEADS = 4
HEAD_W = 2 * HEAD_DIM
SSM_GROUP = 16
N_GROUPS = 32
SSM_STATE = 64
D_FF = 2816
N_MOD = 6
ROPE_BASE = 10000.0
NORM_EPS = 1e-6

CHUNK = 16
CW = CHUNK * SSM_GROUP
SCAN_BLOCK = 8
SCAN_LEVELS = 3
AT_ROWS = 16
GROUP_BLOCK = 4
Z_BATCH = 8

ROWS_PER_TILE = 128
TOK_PER_TILE = ROWS_PER_TILE * CHUNK
SUB_TOK = 512
TQ_ITEM = 256
N_SUB = TOK_PER_TILE // SUB_TOK

VMEM_LIMIT = 56 * 1024 * 1024

NT_DIMS = (((1,), (1,)), ((), ()))
TN_DIMS = (((0,), (0,)), ((), ()))


def _cparams(n_axes):
    return pltpu.CompilerParams(
        dimension_semantics=("arbitrary",) * n_axes,
        vmem_limit_bytes=VMEM_LIMIT)


def _rms(x, g):
    ms = jnp.mean(x * x, axis=-1, keepdims=True)
    return x * lax.rsqrt(ms + NORM_EPS) * g


def _silu(x):
    return x * jax.nn.sigmoid(x)


MOD_ROWS = 8


def _mod_kernel(ctx_ref, c_ref, w_ref, b_ref, o_ref):
    n_lat, tk = c_ref.shape
    row = lax.broadcasted_iota(jnp.int32, (MOD_ROWS, tk), 0)
    cond = jnp.where(row == 0, ctx_ref[...], 0.0)
    for b in range(n_lat):
        cond = jnp.where(row == 1 + b, c_ref[b:b + 1, :], cond)
    part = jnp.dot(_silu(cond).astype(BF16), w_ref[...].astype(BF16),
                   preferred_element_type=F32)

    @pl.when(pl.program_id(0) == 0)
    def _():
        o_ref[:, 0, :] = part + b_ref[...]

    @pl.when(pl.program_id(0) > 0)
    def _():
        o_ref[:, 0, :] += part


def _rope(x, cos, sa, sb):
    return (x * cos + pltpu.roll(x, HEAD_W - 16, axis=1) * sa
            + pltpu.roll(x, 16, axis=1) * sb)


def _in_proj_kernel(*refs, rope, seqs_per_sub):
    x_ref, x3_hbm, mod_ref, g_ref, w_ref = refs[:5]
    refs = refs[5:]
    if rope:
        cos_ref, sa_ref, sb_ref = refs[:3]
        refs = refs[3:]
        q_ref, k_ref, v_ref, ut_ref, wut_ref, xt_ref, xt_sem = refs
    else:
        q_ref, k_ref, v_ref, kc_ref, vc_ref, ut_ref, wut_ref, xt_ref, xt_sem = refs
    tile = pl.program_id(0)
    j = pl.program_id(1)
    t_per_sub = CHUNK // N_SUB
    t_early = CHUNK - t_per_sub

    def gather(tile_idx, t):
        src = x3_hbm.at[pl.ds(tile_idx * ROWS_PER_TILE, ROWS_PER_TILE), t, :]
        return pltpu.make_async_copy(src, xt_ref.at[t], xt_sem.at[t])

    @pl.when(j == 0)
    def _():
        @pl.when(tile == 0)
        def _():
            for t in range(t_early):
                gather(0, t).start()
        for t in range(t_early, CHUNK):
            gather(tile, t).start()

    t_base = j * t_per_sub
    for d in range(t_per_sub):
        gather(tile, t_base + d).wait()

    @pl.when((j == N_SUB - 1) & (tile + 1 < pl.num_programs(0)))
    def _():
        for t in range(t_early):
            gather(tile + 1, t).start()

    @pl.when((tile == 0) & (j == 0))
    def _():
        wut_ref[...] = w_ref[:, 3 * ATT_WIDTH:].T

    shift = mod_ref[:, 0:D_MODEL]
    gain = g_ref[...] * (1.0 + mod_ref[:, D_MODEL:2 * D_MODEL])

    def norm_mod(xv):
        ms = jnp.mean(xv * xv, axis=-1, keepdims=True)
        return (xv * lax.rsqrt(ms + NORM_EPS) * gain + shift).astype(BF16)

    def ssm_input(d0):
        xt = jnp.concatenate([xt_ref[t_base + d0], xt_ref[t_base + d0 + 1]], axis=0)
        ut = lax.dot_general(wut_ref[...], norm_mod(xt), NT_DIMS,
                             preferred_element_type=F32)
        for d in range(2):
            blk = ut[:, d * ROWS_PER_TILE:(d + 1) * ROWS_PER_TILE]
            row0 = pl.multiple_of((t_base + d0 + d) * SSM_GROUP, SSM_GROUP)
            ut_ref[:, pl.ds(row0, SSM_GROUP), :] = (
                blk.reshape(N_GROUPS, SSM_GROUP, ROWS_PER_TILE).astype(ut_ref.dtype))

    proj = jnp.dot(norm_mod(x_ref[...]), w_ref[:, 0:3 * ATT_WIDTH],
                   preferred_element_type=F32)
    q = proj[:, 0:ATT_WIDTH]
    k = proj[:, ATT_WIDTH:2 * ATT_WIDTH]
    v = proj[:, 2 * ATT_WIDTH:3 * ATT_WIDTH]
    qscale = HEAD_DIM ** -0.5 * math.log2(math.e)
    if rope:
        cos, sa, sb = cos_ref[...], sa_ref[...], sb_ref[...]
        for hd in range(N_HEADS):
            sl = slice(hd * HEAD_W, (hd + 1) * HEAD_W)
            q_ref[:, sl] = (_rope(q[:, sl], cos, sa, sb) * qscale).astype(q_ref.dtype)
            k_ref[:, sl] = _rope(k[:, sl], cos, sa, sb).astype(k_ref.dtype)
    else:
        q_ref[...] = (q * qscale).astype(q_ref.dtype)
        k_ref[...] = k.astype(k_ref.dtype)
        seq = SUB_TOK // seqs_per_sub
        k_t = k.T
        for b in range(seqs_per_sub):
            kc_ref[b] = k_t[:, b * seq:(b + 1) * seq]
            for hd in range(N_HEADS):
                vc_ref[b, :, hd, :] = v[b * seq:(b + 1) * seq, hd * HEAD_W:(hd + 1) * HEAD_W]
    v_ref[...] = v.astype(v_ref.dtype)

    for d0 in range(0, t_per_sub, 2):
        ssm_input(d0)


def _in_proj(x, mods, mod_rows, g0, w_in, rope_tabs):
    n_batch, seq_len = x.shape[:2]
    n_tok = n_batch * seq_len
    n_rows = n_tok // CHUNK
    n_tiles = n_tok // TOK_PER_TILE
    mod_row0, n_mod = mod_rows
    tiles_per_mod = n_tiles // n_mod
    rope = rope_tabs is not None
    seqs_per_sub = max(1, SUB_TOK // seq_len)
    in_specs = [pl.BlockSpec((SUB_TOK, D_MODEL), lambda i, j: (i * N_SUB + j, 0)),
                pl.BlockSpec(memory_space=pl.ANY),
                pl.BlockSpec((None, 1, 2 * D_MODEL),
                             lambda i, j: (mod_row0 + i // tiles_per_mod, 0, 0)),
                pl.BlockSpec((1, D_MODEL), lambda i, j: (0, 0)),
                pl.BlockSpec((D_MODEL, 4 * ATT_WIDTH), lambda i, j: (0, 0))]
    args = [x.reshape(n_tok, D_MODEL), x.reshape(n_rows, CHUNK, D_MODEL), mods, g0, w_in]
    row_spec = pl.BlockSpec((SUB_TOK, ATT_WIDTH), lambda i, j: (i * N_SUB + j, 0))
    row_shape = jax.ShapeDtypeStruct((n_tok, ATT_WIDTH), BF16)
    out_specs = [row_spec, row_spec, row_spec]
    out_shape = [row_shape, row_shape, row_shape]
    if rope:
        assert seq_len == TOK_PER_TILE
        for tab in rope_tabs:
            in_specs.append(pl.BlockSpec((SUB_TOK, HEAD_W), lambda i, j: (j, 0)))
            args.append(tab)
    else:
        out_specs += [pl.BlockSpec((seqs_per_sub, ATT_WIDTH, seq_len),
                                   lambda i, j: (i * N_SUB + j, 0, 0)),
                      pl.BlockSpec((seqs_per_sub, seq_len, N_HEADS, HEAD_W),
                                   lambda i, j: (i * N_SUB + j, 0, 0, 0))]
        out_shape += [jax.ShapeDtypeStruct((n_batch, ATT_WIDTH, seq_len), F32),
                      jax.ShapeDtypeStruct((n_batch, seq_len, N_HEADS, HEAD_W), F32)]
    out_specs.append(pl.BlockSpec((N_GROUPS, CW, ROWS_PER_TILE), lambda i, j: (0, 0, i)))
    out_shape.append(jax.ShapeDtypeStruct((N_GROUPS, CW, n_rows), BF16))
    return pl.pallas_call(
        functools.partial(_in_proj_kernel, rope=rope, seqs_per_sub=seqs_per_sub),
        grid=(n_tiles, N_SUB),
        in_specs=in_specs,
        out_specs=out_specs,
        out_shape=out_shape,
        scratch_shapes=[pltpu.VMEM((SSM_WIDTH, D_MODEL), BF16),
                        pltpu.VMEM((CHUNK, ROWS_PER_TILE, D_MODEL), F32),
                        pltpu.SemaphoreType.DMA((CHUNK,))],
        compiler_params=_cparams(2),
        name="in_proj",
    )(*args)


def _attn_kernel(*refs, has_ctx, lam_init, n_seq, seq_len, tq, n_cast):
    if n_cast:
        cast_in = refs[len(refs) - 2 * n_cast - 1:len(refs) - n_cast - 1]
        cast_out = refs[len(refs) - n_cast:]
        refs = refs[:len(refs) - 2 * n_cast - 1] + (refs[len(refs) - n_cast - 1],)
        for src, dst in zip(cast_in, cast_out):
            dst[...] = src[...].astype(dst.dtype)
    if has_ctx:
        lam_ref, sg_ref, q_ref, ck_ref, cv_ref, k_ref, v_ref, o_ref = refs
    else:
        lam_ref, sg_ref, q_ref, k_ref, v_ref, o_ref = refs
    lp = lam_ref[...]
    lam = (jnp.exp(jnp.sum(lp[0:1] * lp[1:2], axis=-1, keepdims=True))
           - jnp.exp(jnp.sum(lp[2:3] * lp[3:4], axis=-1, keepdims=True)) + lam_init)
    first_map = lax.broadcasted_iota(jnp.int32, (1, HEAD_W), 1) < HEAD_DIM
    ti = min(TQ_ITEM, tq)
    for row0 in range(0, n_seq * tq, ti):
        b = row0 // tq
        q_rows = slice(row0, row0 + ti)
        for hd in range(N_HEADS):
            sl = slice(hd * HEAD_W, (hd + 1) * HEAD_W)
            qh = q_ref[q_rows, sl]
            zero = jnp.zeros_like(qh)
            qs = jnp.concatenate([jnp.where(first_map, qh, zero),
                                  jnp.where(first_map, zero, qh)], axis=0)
            kv_rows = slice(b * seq_len, (b + 1) * seq_len)
            parts = [(k_ref[kv_rows, sl], v_ref[kv_rows, sl])]
            if has_ctx:
                parts.insert(0, (ck_ref[:, sl].astype(BF16), cv_ref[:, sl].astype(BF16)))
            scores = [lax.dot_general(qs, kk, NT_DIMS, preferred_element_type=F32)
                      for kk, _ in parts]
            mx = scores[0].max(axis=-1, keepdims=True)
            for s in scores[1:]:
                mx = jnp.maximum(mx, s.max(axis=-1, keepdims=True))
            acc = None
            for s, (_, vv) in zip(scores, parts):
                e = jnp.exp2(s - mx).astype(BF16)
                v_one = jnp.concatenate([vv, jnp.ones_like(vv)], axis=1)
                pv = jnp.dot(e, v_one, preferred_element_type=F32)
                acc = pv if acc is None else acc + pv
            num = acc[:, 0:HEAD_W] / acc[:, HEAD_W:2 * HEAD_W]
            o = num[0:ti] - lam * num[ti:2 * ti]
            o = _rms(o, sg_ref[...]) * (1.0 - lam_init)
            o_ref[q_rows, sl] = o.astype(o_ref.dtype)


def _attention(q, k, v, ctx_k, ctx_v, lam_params, subln_g, n_batch, seq_len, lam_init,
               cast_weights=()):
    has_ctx = ctx_k is not None
    tq = min(1024, seq_len)
    n_q = seq_len // tq
    n_seq = 1 if n_q > 1 else min(4, n_batch)
    in_specs = [pl.BlockSpec((4, HEAD_DIM), lambda b, i: (0, 0)),
                pl.BlockSpec((1, HEAD_W), lambda b, i: (0, 0)),
                pl.BlockSpec((n_seq * tq, ATT_WIDTH), lambda b, i: (b * n_q + i, 0))]
    args = [lam_params, subln_g.reshape(1, HEAD_W), q]
    if has_ctx:
        past = ctx_k.shape[1]
        ctx_spec = pl.BlockSpec((None, past, ATT_WIDTH), lambda b, i: (b, 0, 0))
        in_specs += [ctx_spec, ctx_spec]
        args += [ctx_k, ctx_v]
    kv_spec = pl.BlockSpec((n_seq * seq_len, ATT_WIDTH), lambda b, i: (b, 0))
    in_specs += [kv_spec, kv_spec]
    args += [k, v]
    out_specs = [pl.BlockSpec((n_seq * tq, ATT_WIDTH), lambda b, i: (b * n_q + i, 0))]
    out_shape = [jax.ShapeDtypeStruct((n_batch * seq_len, ATT_WIDTH), BF16)]
    n_steps = (n_batch // n_seq) * n_q
    for w in cast_weights:
        rows = w.shape[0] // n_steps
        spec = pl.BlockSpec((rows, w.shape[1]), lambda b, i: (b * n_q + i, 0))
        in_specs.append(spec)
        args.append(w)
        out_specs.append(spec)
        out_shape.append(jax.ShapeDtypeStruct(w.shape, BF16))
    outs = pl.pallas_call(
        functools.partial(_attn_kernel, has_ctx=has_ctx, lam_init=lam_init,
                          n_seq=n_seq, seq_len=seq_len, tq=tq, n_cast=len(cast_weights)),
        grid=(n_batch // n_seq, n_q),
        in_specs=in_specs,
        out_specs=out_specs,
        out_shape=out_shape,
        compiler_params=_cparams(2),
        name="diff_attention",
    )(*args)
    return outs[0], tuple(outs[1:])


def _cmul(ar, ai, br, bi):
    return ar * br - ai * bi, ar * bi + ai * br


def _ssm_prep_kernel(lre_ref, lim_ref, ls_ref, bre_ref, bim_ref, cre_ref, cim_ref, d_ref,
                     mt_ref, gt_ref, wo_ref, at_ref):
    lane = lax.broadcasted_iota(jnp.int32, (SSM_GROUP, CW), 1)
    chan = lax.broadcasted_iota(jnp.int32, (SSM_GROUP, CW), 0)
    for gi in range(GROUP_BLOCK):
        gt_cols, wo_cols, at_cols, toeplitz = [], [], [], []
        for dr in range(2):
            lr = jnp.minimum(lre_ref[dr, gi], -1e-4)
            li = lim_ref[dr, gi]
            step = jnp.exp(ls_ref[dr, gi])
            mag = jnp.exp(lr * step)
            a_re = mag * jnp.cos(li * step)
            a_im = mag * jnp.sin(li * step)
            den = lr * lr + li * li
            nr = a_re - 1.0
            f_re = (nr * lr + a_im * li) / den
            f_im = (a_im * lr - nr * li) / den
            bt_re, bt_im = bre_ref[dr, gi], bim_ref[dr, gi]
            bb_re, bb_im = _cmul(f_re, f_im, bt_re, bt_im)
            c_re, c_im = cre_ref[dr, gi], cim_ref[dr, gi]
            pw = [(jnp.ones_like(a_re), jnp.zeros_like(a_im))]
            for _ in range(CHUNK):
                pw.append(_cmul(pw[-1][0], pw[-1][1], a_re, a_im))
            g_re, g_im, e_re, e_im = [], [], [], []
            for t in range(CHUNK):
                pr, pi = pw[CHUNK - 1 - t] if dr == 0 else pw[t]
                r, i = _cmul(bb_re, bb_im, pr, pi)
                g_re.append(r)
                g_im.append(i)
                pr, pi = pw[t + 1] if dr == 0 else pw[CHUNK - t]
                r, i = _cmul(c_re, c_im, pr, pi)
                e_re.append(r)
                e_im.append(-i)
            g_cat = jnp.concatenate([jnp.concatenate(g_re, axis=0),
                                     jnp.concatenate(g_im, axis=0)], axis=1)
            gt_cols.append(g_cat)
            wo_cols.append(jnp.concatenate([jnp.concatenate(e_re, axis=0),
                                            jnp.concatenate(e_im, axis=0)], axis=1))
            c_cat = jnp.concatenate([c_re, -c_im], axis=1)
            toeplitz.append(lax.dot_general(c_cat, g_cat, NT_DIMS,
                                            precision=lax.Precision.HIGHEST,
                                            preferred_element_type=F32))
            apw = [pw[CHUNK]]
            for _ in range(SCAN_BLOCK - 1):
                apw.append(_cmul(apw[-1][0], apw[-1][1], apw[0][0], apw[0][1]))
            order = list(range(SCAN_BLOCK)) if dr == 0 else list(range(SCAN_BLOCK - 1, -1, -1))
            order += [2 ** l - 1 for l in range(SCAN_LEVELS)]
            order += [0] * (AT_ROWS - len(order))
            at_cols += [jnp.concatenate([jnp.concatenate([apw[i][0], apw[i][0]], axis=1)
                                         for i in order], axis=0),
                        jnp.concatenate([jnp.concatenate([-apw[i][1], apw[i][1]], axis=1)
                                         for i in order], axis=0)]
        kf_rev, kb = toeplitz
        d_skip = d_ref[gi]
        blocks = []
        for t in range(CHUNK):
            fwd = pltpu.roll(kf_rev, (CW - (CHUNK - 1 - t) * SSM_GROUP) % CW, axis=1)
            bwd = pltpu.roll(kb, t * SSM_GROUP, axis=1)
            blocks.append(jnp.where(lane < (t + 1) * SSM_GROUP, fwd, 0.0)
                          + jnp.where(lane >= t * SSM_GROUP, bwd, 0.0)
                          + jnp.where(lane == chan + t * SSM_GROUP, d_skip, 0.0))
        mt_ref[gi] = jnp.concatenate(blocks, axis=0).astype(mt_ref.dtype)
        gt_ref[gi] = jnp.concatenate(gt_cols, axis=1).astype(gt_ref.dtype)
        wo_ref[gi] = jnp.concatenate(wo_cols, axis=1).astype(wo_ref.dtype)
        at_ref[gi] = jnp.concatenate(at_cols, axis=1)


N_PREP_IN, N_PREP_OUT, N_MOD_IN = 8, 4, 4


def _prep_mod_kernel(*refs):
    prep_in = refs[:N_PREP_IN]
    mod_in = refs[N_PREP_IN:N_PREP_IN + N_MOD_IN]
    w_in_ref = refs[N_PREP_IN + N_MOD_IN]
    outs = refs[N_PREP_IN + N_MOD_IN + 1:]
    _ssm_prep_kernel(*prep_in, *outs[:N_PREP_OUT])
    _mod_kernel(*mod_in, outs[N_PREP_OUT])
    outs[N_PREP_OUT + 1][...] = w_in_ref[...].astype(BF16)


def _ssm_prep_and_modulation(lam_re, lam_im, log_step, b_re, b_im, c_re, c_im, d_skip,
                             c_ctx, c, w_mod, b_mod, w_in):
    row = lambda a: a.reshape(2, N_GROUPS, 1, SSM_STATE)
    bt = lambda a: jnp.swapaxes(a, 2, 3)
    d_row = jnp.tile((d_skip[0] + d_skip[1]).reshape(N_GROUPS, 1, SSM_GROUP), (1, 1, CHUNK))
    gb = GROUP_BLOCK
    n_steps = N_GROUPS // gb
    vec_spec = pl.BlockSpec((2, gb, 1, SSM_STATE), lambda i: (0, i, 0, 0))
    mat_spec = pl.BlockSpec((2, gb, SSM_GROUP, SSM_STATE), lambda i: (0, i, 0, 0))
    w_spec = pl.BlockSpec((gb, CW, CW), lambda i: (i, 0, 0))
    w_shape = jax.ShapeDtypeStruct((N_GROUPS, CW, CW), BF16)
    n_mod = w_mod.shape[1]
    tk = D_MODEL // n_steps
    assert 1 + c.shape[0] <= MOD_ROWS
    outs = pl.pallas_call(
        _prep_mod_kernel,
        grid=(n_steps,),
        in_specs=[vec_spec, vec_spec,
                  pl.BlockSpec((2, gb, 1, 1), lambda i: (0, i, 0, 0)),
                  mat_spec, mat_spec, mat_spec, mat_spec,
                  pl.BlockSpec((gb, 1, CW), lambda i: (i, 0, 0)),
                  pl.BlockSpec((1, tk), lambda k: (0, k)),
                  pl.BlockSpec((c.shape[0], tk), lambda k: (0, k)),
                  pl.BlockSpec((tk, n_mod), lambda k: (k, 0)),
                  pl.BlockSpec((1, n_mod), lambda k: (0, 0)),
                  pl.BlockSpec((tk, w_in.shape[1]), lambda k: (k, 0))],
        out_specs=[w_spec, w_spec, w_spec,
                   pl.BlockSpec((gb, AT_ROWS, 4 * 2 * SSM_STATE), lambda i: (i, 0, 0)),
                   pl.BlockSpec((MOD_ROWS, 1, n_mod), lambda k: (0, 0, 0)),
                   pl.BlockSpec((tk, w_in.shape[1]), lambda k: (k, 0))],
        out_shape=[w_shape, w_shape, w_shape,
                   jax.ShapeDtypeStruct((N_GROUPS, AT_ROWS, 4 * 2 * SSM_STATE), F32),
                   jax.ShapeDtypeStruct((MOD_ROWS, 1, n_mod), F32),
                   jax.ShapeDtypeStruct(w_in.shape, BF16)],
        compiler_params=_cparams(1),
        name="ssm_prep_modulation",
    )(row(lam_re), row(lam_im), log_step.reshape(2, N_GROUPS, 1, 1),
      bt(b_re), bt(b_im), c_re, c_im, d_row,
      c_ctx.reshape(1, D_MODEL), c, w_mod, b_mod.reshape(1, n_mod), w_in)
    return outs[:N_PREP_OUT], outs[N_PREP_OUT], outs[N_PREP_OUT + 1]


def _shift_rows(x, m, down):
    n = x.shape[0]
    return pltpu.roll(x, m if down else n - m, axis=0)


def _ssm_kernel(*refs, n_seq, has_h0):
    if has_h0:
        (xt_ref, mt_ref, gt_ref, wo_ref, at_ref, wg_ref, bg_ref, h0_ref,
         z_hbm, zs_ref, zb_ref, zb_sem) = refs
    else:
        (xt_ref, mt_ref, gt_ref, wo_ref, at_ref, wg_ref, bg_ref,
         z_hbm, st_ref, zs_ref, zb_ref, zb_sem, fin_ref) = refs
    step = pl.program_id(0)
    n_rows = xt_ref.shape[-1]
    seg = n_rows // n_seq
    cw2 = 2 * SSM_STATE
    assert seg % SCAN_BLOCK == 0
    n_blk = n_rows // SCAN_BLOCK
    blk_per_seq = seg // SCAN_BLOCK
    pos = lax.broadcasted_iota(jnp.int32, (n_rows, cw2), 0) % seg
    row_blk = lax.broadcasted_iota(jnp.int32, (SCAN_BLOCK, cw2), 0)

    def low_half(shape):
        return lax.broadcasted_iota(jnp.int32, shape, 1) < SSM_STATE

    def swap(v):
        return pltpu.roll(v, SSM_STATE, axis=1)

    def to_planes(va, vb):
        lo = low_half((va.shape[0], cw2))
        va_l = pltpu.roll(va, 3 * SSM_STATE, axis=1)
        vb_r = pltpu.roll(vb, SSM_STATE, axis=1)
        return (jnp.where(lo, va[:, 0:cw2], vb_r[:, 0:cw2]),
                jnp.where(lo, va_l[:, 0:cw2], vb[:, 0:cw2]),
                jnp.where(lo, va[:, cw2:2 * cw2], vb_r[:, cw2:2 * cw2]),
                jnp.where(lo, va_l[:, cw2:2 * cw2], vb[:, cw2:2 * cw2]))

    def from_planes(f_re, f_im, b_re, b_im):
        lo = low_half(f_re.shape)
        va = jnp.concatenate([jnp.where(lo, f_re, swap(f_im)), jnp.where(lo, b_re, swap(b_im))], axis=1)
        vb = jnp.concatenate([jnp.where(lo, swap(f_re), f_im), jnp.where(lo, swap(b_re), b_im)], axis=1)
        return va, vb

    for ga in range(0, GROUP_BLOCK, 2):
        gb = ga + 1
        xts = [xt_ref[ga], xt_ref[gb]]
        s_pair = [lax.dot_general(xts[i], gt_ref[g], TN_DIMS, preferred_element_type=F32)
                  for i, g in enumerate((ga, gb))]
        planes = to_planes(*s_pair)
        lo_t = low_half((AT_ROWS, cw2))
        if has_h0:
            h0_planes = to_planes(h0_ref[ga], h0_ref[gb])
        ent_planes = []
        for dr in range(2):
            down = dr == 0
            pa, pb = (at_ref[g][:, (2 * dr) * cw2:(2 * dr + 1) * cw2] for g in (ga, gb))
            qa, qb = (at_ref[g][:, (2 * dr + 1) * cw2:(2 * dr + 2) * cw2] for g in (ga, gb))
            ar_tab = jnp.where(lo_t, pa, pb)
            ai_tab = jnp.where(lo_t, -qa, qb)
            re3 = planes[2 * dr].reshape(n_blk, SCAN_BLOCK, cw2)
            im3 = planes[2 * dr + 1].reshape(n_blk, SCAN_BLOCK, cw2)
            for lvl in range(SCAN_LEVELS):
                m = 2 ** lvl
                valid = (row_blk >= m) if down else (row_blk < SCAN_BLOCK - m)
                row = SCAN_BLOCK + lvl
                ar = jnp.where(valid, ar_tab[row:row + 1], 0.0)
                ai = jnp.where(valid, ai_tab[row:row + 1], 0.0)
                shift = m if down else SCAN_BLOCK - m
                sh_re = pltpu.roll(re3, shift, axis=1)
                sh_im = pltpu.roll(im3, shift, axis=1)
                re3, im3 = re3 + ar * sh_re - ai * sh_im, im3 + ar * sh_im + ai * sh_re
            ar_blk, ai_blk = ar_tab[0:SCAN_BLOCK], ai_tab[0:SCAN_BLOCK]
            blk_re = [re3[i] for i in range(n_blk)]
            blk_im = [im3[i] for i in range(n_blk)]
            edge = slice(SCAN_BLOCK - 1, SCAN_BLOCK) if down else slice(0, 1)
            for q_i in range(n_seq):
                idxs = list(range(q_i * blk_per_seq, (q_i + 1) * blk_per_seq))
                idxs = idxs if down else idxs[::-1]
                for prev, cur in zip([None] + idxs[:-1], idxs):
                    if prev is not None:
                        c_re, c_im = blk_re[prev][edge], blk_im[prev][edge]
                    elif has_h0:
                        c_re = h0_planes[2 * dr][q_i:q_i + 1]
                        c_im = h0_planes[2 * dr + 1][q_i:q_i + 1]
                    else:
                        continue
                    c_re = jnp.broadcast_to(c_re, (SCAN_BLOCK, cw2))
                    c_im = jnp.broadcast_to(c_im, (SCAN_BLOCK, cw2))
                    blk_re[cur] = blk_re[cur] + ar_blk * c_re - ai_blk * c_im
                    blk_im[cur] = blk_im[cur] + ar_blk * c_im + ai_blk * c_re
            for part, blks in ((0, blk_re), (1, blk_im)):
                s = jnp.concatenate(blks, axis=0)
                if not has_h0:
                    fin_ref[part] = s
                ent = _shift_rows(s, 1, down)
                ent = jnp.where((pos >= 1) if down else (pos < seg - 1), ent, 0.0)
                if has_h0:
                    ent_blk = [ent[i * SCAN_BLOCK:(i + 1) * SCAN_BLOCK] for i in range(n_blk)]
                    at_edge = row_blk == (0 if down else SCAN_BLOCK - 1)
                    for q_i in range(n_seq):
                        bi = q_i * blk_per_seq if down else (q_i + 1) * blk_per_seq - 1
                        ent_blk[bi] = jnp.where(
                            at_edge, h0_planes[2 * dr + part][q_i:q_i + 1], ent_blk[bi])
                    ent = jnp.concatenate(ent_blk, axis=0)
                ent_planes.append(ent)
            if not has_h0:
                rows = pl.ds(seg - 1 if down else 0, n_seq, stride=seg)
                f_re, f_im = fin_ref[0, rows, :], fin_ref[1, rows, :]
                lo_s = low_half((n_seq, cw2))
                st_ref[ga, :, dr * cw2:(dr + 1) * cw2] = jnp.where(lo_s, f_re, swap(f_im))
                st_ref[gb, :, dr * cw2:(dr + 1) * cw2] = jnp.where(lo_s, swap(f_re), f_im)
        h_pair = from_planes(*ent_planes)
        for i, g in enumerate((ga, gb)):
            yt = (jnp.dot(mt_ref[g], xts[i], preferred_element_type=F32)
                  + lax.dot_general(wo_ref[g], h_pair[i].astype(BF16), NT_DIMS,
                                    preferred_element_type=F32))
            z = jax.nn.gelu(yt, approximate=True)
            grp = step * GROUP_BLOCK + g
            for t in range(CHUNK):
                zs_ref[t, pl.ds(pl.multiple_of(grp * SSM_GROUP, SSM_GROUP), SSM_GROUP), :] = (
                    z[t * SSM_GROUP:(t + 1) * SSM_GROUP, :])

    @pl.when(step == pl.num_programs(0) - 1)
    def _():
        def put(t):
            return pltpu.make_async_copy(zb_ref.at[t], z_hbm.at[:, t, :], zb_sem.at[t])

        for t0 in range(0, CHUNK, Z_BATCH):
            for t in range(t0, t0 + Z_BATCH):
                zt = zs_ref[t]
                gate = jnp.dot(wg_ref[...], zt.astype(BF16),
                               preferred_element_type=F32) + bg_ref[...]
                zb_ref[t] = (zt * jax.nn.sigmoid(gate)).T
            for t in range(t0, t0 + Z_BATCH):
                put(t).start()
        for t in range(CHUNK):
            put(t).wait()


def _ssm(xt, mt, gt, wo, at, w_glu_t, b_glu_col, h0, n_seq):
    n_rows = xt.shape[-1]
    has_h0 = h0 is not None
    gb = GROUP_BLOCK
    w_spec = pl.BlockSpec((gb, CW, CW), lambda i: (i, 0, 0))
    in_specs = [pl.BlockSpec((gb, CW, n_rows), lambda i: (i, 0, 0)),
                w_spec, w_spec, w_spec,
                pl.BlockSpec((gb, AT_ROWS, 4 * 2 * SSM_STATE), lambda i: (i, 0, 0)),
                pl.BlockSpec((SSM_WIDTH, SSM_WIDTH), lambda i: (0, 0)),
                pl.BlockSpec((SSM_WIDTH, 1), lambda i: (0, 0))]
    args = [xt, mt, gt, wo, at, w_glu_t, b_glu_col]
    out_specs = [pl.BlockSpec(memory_space=pl.ANY)]
    out_shape = [jax.ShapeDtypeStruct((n_rows, CHUNK, SSM_WIDTH), F32)]
    scratch = [pltpu.VMEM((CHUNK, SSM_WIDTH, n_rows), F32),
               pltpu.VMEM((CHUNK, n_rows, SSM_WIDTH), F32),
               pltpu.SemaphoreType.DMA((CHUNK,))]
    if has_h0:
        in_specs.append(pl.BlockSpec((gb, n_seq, CW), lambda i: (i, 0, 0)))
        args.append(h0)
    else:
        out_specs.append(pl.BlockSpec((gb, n_seq, CW), lambda i: (i, 0, 0)))
        out_shape.append(jax.ShapeDtypeStruct((N_GROUPS, n_seq, CW), F32))
        scratch.append(pltpu.VMEM((2, n_rows, 2 * SSM_STATE), F32))
    return pl.pallas_call(
        functools.partial(_ssm_kernel, n_seq=n_seq, has_h0=has_h0),
        grid=(N_GROUPS // gb,),
        in_specs=in_specs,
        out_specs=out_specs,
        out_shape=out_shape,
        scratch_shapes=scratch,
        compiler_params=_cparams(1),
        name="ssm_scan_glu",
    )(*args)


FF_CHUNK = 256
FFN_TOK = 1024


def _out_ffn_kernel(x_ref, attn_ref, z_ref, mod_ref, g_ref, wo_ref, wfi_ref, wfo_ref,
                    o_ref, act_ref):
    gate1 = mod_ref[:, 2 * D_MODEL:3 * D_MODEL]
    shift2 = mod_ref[:, 3 * D_MODEL:4 * D_MODEL]
    scale2 = mod_ref[:, 4 * D_MODEL:5 * D_MODEL]
    gate2 = mod_ref[:, 5 * D_MODEL:6 * D_MODEL]
    half = FFN_TOK // 2
    gain2 = g_ref[2:3, :] * (1.0 + scale2)

    def pre_ffn(hf):
        r = slice(hf * half, (hf + 1) * half)
        mixer = jnp.concatenate([attn_ref[r, :], z_ref[r, :].astype(BF16)], axis=1)
        mix = jnp.dot(mixer, wo_ref[...], preferred_element_type=F32)
        x1 = x_ref[r, :] + gate1 * _rms(mix, g_ref[1:2, :])
        ms = jnp.mean(x1 * x1, axis=-1, keepdims=True)
        return x1, (x1 * lax.rsqrt(ms + NORM_EPS) * gain2 + shift2).astype(BF16)

    def ffn_in(hf, h, chunks):
        r = slice(hf * half, (hf + 1) * half)
        for c in chunks:
            lo = c * FF_CHUNK
            gt = jnp.dot(h, wfi_ref[:, lo:lo + FF_CHUNK], preferred_element_type=F32)
            up = jnp.dot(h, wfi_ref[:, D_FF + lo:D_FF + lo + FF_CHUNK],
                         preferred_element_type=F32)
            act_ref[r, lo:lo + FF_CHUNK] = (_silu(gt) * up).astype(BF16)

    def ffn_out(hf, x1):
        r = slice(hf * half, (hf + 1) * half)
        f = jnp.dot(act_ref[r, :], wfo_ref[...], preferred_element_type=F32)
        o_ref[r, :] = x1 + gate2 * _rms(f, g_ref[3:4, :])

    n_chunks = D_FF // FF_CHUNK
    x1_a, h_a = pre_ffn(0)
    x1_b, h_b = pre_ffn(1)
    ffn_in(0, h_a, range(n_chunks))
    ffn_out(0, x1_a)
    ffn_in(1, h_b, range(n_chunks))
    ffn_out(1, x1_b)


def _out_ffn(x2d, attn, z, mods, mod_rows, norm_g, w_o, w_ffn_in, w_ffn_out):
    n_tok = x2d.shape[0]
    z2d = z.reshape(n_tok, SSM_WIDTH)
    n_steps = n_tok // FFN_TOK
    mod_row0, n_mod = mod_rows
    steps_per_mod = n_steps // n_mod
    const = lambda i: (0, 0)
    row_spec = lambda w: pl.BlockSpec((FFN_TOK, w), lambda i: (i, 0))
    return pl.pallas_call(
        _out_ffn_kernel,
        grid=(n_steps,),
        in_specs=[row_spec(D_MODEL), row_spec(ATT_WIDTH), row_spec(SSM_WIDTH),
                  pl.BlockSpec((None, 1, N_MOD * D_MODEL),
                               lambda i: (mod_row0 + i // steps_per_mod, 0, 0)),
                  pl.BlockSpec((4, D_MODEL), const),
                  pl.BlockSpec((2 * ATT_WIDTH, D_MODEL), const, pipeline_mode=pl.Buffered(1)),
                  pl.BlockSpec((D_MODEL, 2 * D_FF), const, pipeline_mode=pl.Buffered(1)),
                  pl.BlockSpec((D_FF, D_MODEL), const, pipeline_mode=pl.Buffered(1))],
        out_specs=row_spec(D_MODEL),
        out_shape=jax.ShapeDtypeStruct((n_tok, D_MODEL), F32),
        scratch_shapes=[pltpu.VMEM((FFN_TOK, D_FF), BF16)],
        compiler_params=_cparams(1),
        name="out_proj_ffn",
    )(x2d, attn, z2d, mods, norm_g, w_o, w_ffn_in, w_ffn_out)


def _rope_tables(seq_len):
    t = np.arange(seq_len)
    row = (t // GRID_W).astype(np.float32)
    col = (t % GRID_W).astype(np.float32)
    half = HEAD_DIM // 2
    inv_freq = (np.float32(ROPE_BASE)
                ** (-np.arange(0, half, 2, dtype=np.float32) / np.float32(half))).astype(np.float32)
    ang_r = row[:, None] * inv_freq
    ang_c = col[:, None] * inv_freq
    ang = np.concatenate([ang_r, ang_r, ang_c, ang_c], axis=-1)
    cos, sin = np.cos(ang), np.sin(ang)
    upper = (np.arange(HEAD_DIM) % 32) < 16
    sa = np.where(upper, -sin, 0.0)
    sb = np.where(upper, 0.0, sin)
    two = lambda a: jnp.asarray(np.concatenate([a, a], axis=-1), dtype=F32)
    return two(cos), two(sa), two(sb)


def _layer(x, mods, mod_rows, lam_init, rope_tabs, ctx_k, ctx_v, h0, weights, prep):
    n_batch, seq_len = x.shape[:2]
    g = weights['norm_g']
    outs = _in_proj(x, mods, mod_rows, g[0:1], weights['w_in'], rope_tabs)
    q, k, v = outs[:3]
    pending = () if 'late_bf16' in weights else weights['late_f32']
    attn, cast = _attention(q, k, v, ctx_k, ctx_v, weights['lam'], weights['subln_g'],
                            n_batch, seq_len, lam_init, cast_weights=pending)
    if pending:
        weights['late_bf16'] = cast
    ssm_out = _ssm(outs[-1], *prep, weights['w_glu_t'], weights['b_glu_col'], h0, n_batch)
    y = _out_ffn(x.reshape(n_batch * seq_len, D_MODEL), attn, ssm_out[0], mods, mod_rows, g,
                 *weights['late_bf16'])
    return y.reshape(x.shape), outs[3:-1], ssm_out[1:]


def kernel(x_prompt, x_sample, cache_k, cache_v, state_ssm_re, state_ssm_im, c, c_ctx, w_mod, b_mod, norm_g, w_in, lam_params, subln_g, ssm_lambda_re, ssm_lambda_im, ssm_log_step, ssm_b_re, ssm_b_im, ssm_c_re, ssm_c_im, ssm_d, w_glu, b_glu, w_o, w_ffn_in, w_ffn_out):
    depth = w_mod.shape[0]
    assert depth == 1
    bp = x_prompt.shape[0]
    bd, ld_len = x_sample.shape[:2]
    past = cache_k.shape[2]
    xp, xs = x_prompt, x_sample
    rope_tabs = _rope_tables(ld_len)
    ks_out, vs_out, hr_out, hi_out = [], [], [], []
    for l in range(depth):
        lam_init = 0.8 - 0.6 * math.exp(-0.3 * l)
        prep, mods, w_in_bf16 = _ssm_prep_and_modulation(
            ssm_lambda_re[l], ssm_lambda_im[l], ssm_log_step[l], ssm_b_re[l], ssm_b_im[l],
            ssm_c_re[l], ssm_c_im[l], ssm_d[l], c_ctx, c, w_mod[l], b_mod[l], w_in[l])
        weights = {
            'norm_g': norm_g[l],
            'w_in': w_in_bf16,
            'lam': lam_params[l], 'subln_g': subln_g[l],
            'w_glu_t': w_glu[l].T.astype(BF16), 'b_glu_col': b_glu[l].reshape(SSM_WIDTH, 1),
            'late_f32': (w_o[l], w_ffn_in[l], w_ffn_out[l]),
        }
        ck = cache_k[:, l].reshape(bd, past, ATT_WIDTH)
        cv = cache_v[:, l].reshape(bd, past, ATT_WIDTH)
        h0 = jnp.stack([state_ssm_re[:, l], state_ssm_im[:, l]], axis=2)
        h0 = h0.transpose(3, 0, 1, 2, 4).reshape(N_GROUPS, bd, CW)
        xs, _, _ = _layer(xs, mods, (1, bd), lam_init, rope_tabs, ck, cv, h0, weights, prep)
        xp, (k_ctx, v_ctx), (st,) = _layer(xp, mods, (0, 1), lam_init, None, None, None, None,
                                           weights, prep)
        ks_out.append(jnp.swapaxes(k_ctx, 1, 2).reshape(bp, -1, 2 * N_HEADS, HEAD_DIM))
        vs_out.append(v_ctx)
        fin = st.reshape(N_GROUPS, bp, 2, 2, SSM_STATE).transpose(1, 2, 3, 0, 4)
        hr_out.append(fin[:, :, 0])
        hi_out.append(fin[:, :, 1])
    return (xp, xs, jnp.stack(ks_out, axis=1), jnp.stack(vs_out, axis=1),
            jnp.stack(hr_out, axis=1), jnp.stack(hi_out, axis=1))
```

```python
import functools
import math

import jax
import jax.numpy as jnp
import numpy as np
from jax import lax
from jax.experimental import pallas as pl
from jax.experimental.pallas import tpu as pltpu

F32 = jnp.float32
BF16 = jnp.bfloat16

D_MODEL = 1024
GRID_W = 64
ATT_WIDTH = 512
SSM_WIDTH = 512
HEAD_DIM = 64
N_HEADS = 4
HEAD_W = 2 * HEAD_DIM
SSM_GROUP = 16
N_GROUPS = 32
SSM_STATE = 64
D_FF = 2816
N_MOD = 6
ROPE_BASE = 10000.0
NORM_EPS = 1e-6

CHUNK = 16
CW = CHUNK * SSM_GROUP
SCAN_BLOCK = 8
SCAN_LEVELS = 3
AT_ROWS = 16
GROUP_BLOCK = 4
Z_BATCH = 8

ROWS_PER_TILE = 128
TOK_PER_TILE = ROWS_PER_TILE * CHUNK
SUB_TOK = 512
TQ_ITEM = 256
N_SUB = TOK_PER_TILE // SUB_TOK

VMEM_LIMIT = 56 * 1024 * 1024

NT_DIMS = (((1,), (1,)), ((), ()))
TN_DIMS = (((0,), (0,)), ((), ()))


def _cparams(n_axes):
    return pltpu.CompilerParams(
        dimension_semantics=("arbitrary",) * n_axes,
        vmem_limit_bytes=VMEM_LIMIT)


def _rms(x, g):
    ms = jnp.mean(x * x, axis=-1, keepdims=True)
    return x * lax.rsqrt(ms + NORM_EPS) * g


def _silu(x):
    return x * jax.nn.sigmoid(x)


MOD_ROWS = 8


def _mod_kernel(ctx_ref, c_ref, w_ref, b_ref, o_ref):
    n_lat, tk = c_ref.shape
    row = lax.broadcasted_iota(jnp.int32, (MOD_ROWS, tk), 0)
    cond = jnp.where(row == 0, ctx_ref[...], 0.0)
    for b in range(n_lat):
        cond = jnp.where(row == 1 + b, c_ref[b:b + 1, :], cond)
    part = jnp.dot(_silu(cond).astype(BF16), w_ref[...].astype(BF16),
                   preferred_element_type=F32)

    @pl.when(pl.program_id(0) == 0)
    def _():
        o_ref[:, 0, :] = part + b_ref[...]

    @pl.when(pl.program_id(0) > 0)
    def _():
        o_ref[:, 0, :] += part


def _rope(x, cos, sa, sb):
    return (x * cos + pltpu.roll(x, HEAD_W - 16, axis=1) * sa
            + pltpu.roll(x, 16, axis=1) * sb)


def _in_proj_kernel(*refs, rope, seqs_per_sub):
    x_ref, x3_hbm, mod_ref, g_ref, w_ref = refs[:5]
    refs = refs[5:]
    if rope:
        cos_ref, sa_ref, sb_ref = refs[:3]
        refs = refs[3:]
        q_ref, k_ref, v_ref, ut_ref, wut_ref, xt_ref, xt_sem = refs
    else:
        q_ref, k_ref, v_ref, kc_ref, vc_ref, ut_ref, wut_ref, xt_ref, xt_sem = refs
    tile = pl.program_id(0)
    j = pl.program_id(1)
    t_per_sub = CHUNK // N_SUB
    t_early = CHUNK - t_per_sub

    def gather(tile_idx, t):
        src = x3_hbm.at[pl.ds(tile_idx * ROWS_PER_TILE, ROWS_PER_TILE), t, :]
        return pltpu.make_async_copy(src, xt_ref.at[t], xt_sem.at[t])

    @pl.when(j == 0)
    def _():
        @pl.when(tile == 0)
        def _():
            for t in range(t_early):
                gather(0, t).start()
        for t in range(t_early, CHUNK):
            gather(tile, t).start()

    t_base = j * t_per_sub
    for d in range(t_per_sub):
        gather(tile, t_base + d).wait()

    @pl.when((j == N_SUB - 1) & (tile + 1 < pl.num_programs(0)))
    def _():
        for t in range(t_early):
            gather(tile + 1, t).start()

    @pl.when((tile == 0) & (j == 0))
    def _():
        wut_ref[...] = w_ref[:, 3 * ATT_WIDTH:].T

    shift = mod_ref[:, 0:D_MODEL]
    gain = g_ref[...] * (1.0 + mod_ref[:, D_MODEL:2 * D_MODEL])

    def norm_mod(xv):
        ms = jnp.mean(xv * xv, axis=-1, keepdims=True)
        return (xv * lax.rsqrt(ms + NORM_EPS) * gain + shift).astype(BF16)

    def ssm_input(d0):
        xt = jnp.concatenate([xt_ref[t_base + d0], xt_ref[t_base + d0 + 1]], axis=0)
        ut = lax.dot_general(wut_ref[...], norm_mod(xt), NT_DIMS,
                             preferred_element_type=F32)
        for d in range(2):
            blk = ut[:, d * ROWS_PER_TILE:(d + 1) * ROWS_PER_TILE]
            row0 = pl.multiple_of((t_base + d0 + d) * SSM_GROUP, SSM_GROUP)
            ut_ref[:, pl.ds(row0, SSM_GROUP), :] = (
                blk.reshape(N_GROUPS, SSM_GROUP, ROWS_PER_TILE).astype(ut_ref.dtype))

    proj = jnp.dot(norm_mod(x_ref[...]), w_ref[:, 0:3 * ATT_WIDTH],
                   preferred_element_type=F32)
    q = proj[:, 0:ATT_WIDTH]
    k = proj[:, ATT_WIDTH:2 * ATT_WIDTH]
    v = proj[:, 2 * ATT_WIDTH:3 * ATT_WIDTH]
    qscale = HEAD_DIM ** -0.5 * math.log2(math.e)
    if rope:
        cos, sa, sb = cos_ref[...], sa_ref[...], sb_ref[...]
        for hd in range(N_HEADS):
            sl = slice(hd * HEAD_W, (hd + 1) * HEAD_W)
            q_ref[:, sl] = (_rope(q[:, sl], cos, sa, sb) * qscale).astype(q_ref.dtype)
            k_ref[:, sl] = _rope(k[:, sl], cos, sa, sb).astype(k_ref.dtype)
    else:
        q_ref[...] = (q * qscale).astype(q_ref.dtype)
        k_ref[...] = k.astype(k_ref.dtype)
        seq = SUB_TOK // seqs_per_sub
        k_t = k.T
        for b in range(seqs_per_sub):
            kc_ref[b] = k_t[:, b * seq:(b + 1) * seq]
            for hd in range(N_HEADS):
                vc_ref[b, :, hd, :] = v[b * seq:(b + 1) * seq, hd * HEAD_W:(hd + 1) * HEAD_W]
    v_ref[...] = v.astype(v_ref.dtype)

    for d0 in range(0, t_per_sub, 2):
        ssm_input(d0)


def _in_proj(x, mods, mod_rows, g0, w_in, rope_tabs):
    n_batch, seq_len = x.shape[:2]
    n_tok = n_batch * seq_len
    n_rows = n_tok // CHUNK
    n_tiles = n_tok // TOK_PER_TILE
    mod_row0, n_mod = mod_rows
    tiles_per_mod = n_tiles // n_mod
    rope = rope_tabs is not None
    seqs_per_sub = max(1, SUB_TOK // seq_len)
    in_specs = [pl.BlockSpec((SUB_TOK, D_MODEL), lambda i, j: (i * N_SUB + j, 0)),
                pl.BlockSpec(memory_space=pl.ANY),
                pl.BlockSpec((None, 1, 2 * D_MODEL),
                             lambda i, j: (mod_row0 + i // tiles_per_mod, 0, 0)),
                pl.BlockSpec((1, D_MODEL), lambda i, j: (0, 0)),
                pl.BlockSpec((D_MODEL, 4 * ATT_WIDTH), lambda i, j: (0, 0))]
    args = [x.reshape(n_tok, D_MODEL), x.reshape(n_rows, CHUNK, D_MODEL), mods, g0, w_in]
    row_spec = pl.BlockSpec((SUB_TOK, ATT_WIDTH), lambda i, j: (i * N_SUB + j, 0))
    row_shape = jax.ShapeDtypeStruct((n_tok, ATT_WIDTH), BF16)
    out_specs = [row_spec, row_spec, row_spec]
    out_shape = [row_shape, row_shape, row_shape]
    if rope:
        assert seq_len == TOK_PER_TILE
        for tab in rope_tabs:
            in_specs.append(pl.BlockSpec((SUB_TOK, HEAD_W), lambda i, j: (j, 0)))
            args.append(tab)
    else:
        out_specs += [pl.BlockSpec((seqs_per_sub, ATT_WIDTH, seq_len),
                                   lambda i, j: (i * N_SUB + j, 0, 0)),
                      pl.BlockSpec((seqs_per_sub, seq_len, N_HEADS, HEAD_W),
                                   lambda i, j: (i * N_SUB + j, 0, 0, 0))]
        out_shape += [jax.ShapeDtypeStruct((n_batch, ATT_WIDTH, seq_len), F32),
                      jax.ShapeDtypeStruct((n_batch, seq_len, N_HEADS, HEAD_W), F32)]
    out_specs.append(pl.BlockSpec((N_GROUPS, CW, ROWS_PER_TILE), lambda i, j: (0, 0, i)))
    out_shape.append(jax.ShapeDtypeStruct((N_GROUPS, CW, n_rows), BF16))
    return pl.pallas_call(
        functools.partial(_in_proj_kernel, rope=rope, seqs_per_sub=seqs_per_sub),
        grid=(n_tiles, N_SUB),
        in_specs=in_specs,
        out_specs=out_specs,
        out_shape=out_shape,
        scratch_shapes=[pltpu.VMEM((SSM_WIDTH, D_MODEL), BF16),
                        pltpu.VMEM((CHUNK, ROWS_PER_TILE, D_MODEL), F32),
                        pltpu.SemaphoreType.DMA((CHUNK,))],
        compiler_params=_cparams(2),
        name="in_proj",
    )(*args)


def _attn_kernel(*refs, has_ctx, lam_init, n_seq, seq_len, tq, n_cast):
    if n_cast:
        cast_in = refs[len(refs) - 2 * n_cast - 1:len(refs) - n_cast - 1]
        cast_out = refs[len(refs) - n_cast:]
        refs = refs[:len(refs) - 2 * n_cast - 1] + (refs[len(refs) - n_cast - 1],)
        for src, dst in zip(cast_in, cast_out):
            dst[...] = src[...].astype(dst.dtype)
    if has_ctx:
        lam_ref, sg_ref, q_ref, ck_ref, cv_ref, k_ref, v_ref, o_ref = refs
    else:
        lam_ref, sg_ref, q_ref, k_ref, v_ref, o_ref = refs
    lp = lam_ref[...]
    lam = (jnp.exp(jnp.sum(lp[0:1] * lp[1:2], axis=-1, keepdims=True))
           - jnp.exp(jnp.sum(lp[2:3] * lp[3:4], axis=-1, keepdims=True)) + lam_init)
    first_map = lax.broadcasted_iota(jnp.int32, (1, HEAD_W), 1) < HEAD_DIM
    ti = min(TQ_ITEM, tq)
    for row0 in range(0, n_seq * tq, ti):
        b = row0 // tq
        q_rows = slice(row0, row0 + ti)
        for hd in range(N_HEADS):
            sl = slice(hd * HEAD_W, (hd + 1) * HEAD_W)
            qh = q_ref[q_rows, sl]
            zero = jnp.zeros_like(qh)
            qs = jnp.concatenate([jnp.where(first_map, qh, zero),
                                  jnp.where(first_map, zero, qh)], axis=0)
            kv_rows = slice(b * seq_len, (b + 1) * seq_len)
            parts = [(k_ref[kv_rows, sl], v_ref[kv_rows, sl])]
            scores = [lax.dot_general(qs, parts[0][0], NT_DIMS, preferred_element_type=F32)]
            if has_ctx:
                parts.insert(0, (None, cv_ref[:, hd, :].astype(BF16)))
                scores.insert(0, jnp.dot(qs, ck_ref[sl, :].astype(BF16),
                                         preferred_element_type=F32))
            mx = scores[0].max(axis=-1, keepdims=True)
            for s in scores[1:]:
                mx = jnp.maximum(mx, s.max(axis=-1, keepdims=True))
            acc = None
            for s, (_, vv) in zip(scores, parts):
                e = jnp.exp2(s - mx).astype(BF16)
                v_one = jnp.concatenate([vv, jnp.ones_like(vv)], axis=1)
                pv = jnp.dot(e, v_one, preferred_element_type=F32)
                acc = pv if acc is None else acc + pv
            num = acc[:, 0:HEAD_W] / acc[:, HEAD_W:2 * HEAD_W]
            o = num[0:ti] - lam * num[ti:2 * ti]
            o = _rms(o, sg_ref[...]) * (1.0 - lam_init)
            o_ref[q_rows, sl] = o.astype(o_ref.dtype)


def _attention(q, k, v, ctx_k, ctx_v, lam_params, subln_g, n_batch, seq_len, lam_init,
               cast_weights=()):
    has_ctx = ctx_k is not None
    tq = min(1024, seq_len)
    n_q = seq_len // tq
    n_seq = 1 if n_q > 1 else min(4, n_batch)
    in_specs = [pl.BlockSpec((4, HEAD_DIM), lambda b, i: (0, 0)),
                pl.BlockSpec((1, HEAD_W), lambda b, i: (0, 0)),
                pl.BlockSpec((n_seq * tq, ATT_WIDTH), lambda b, i: (b * n_q + i, 0))]
    args = [lam_params, subln_g.reshape(1, HEAD_W), q]
    if has_ctx:
        past = ctx_v.shape[1]
        in_specs +=[pl.BlockSpec((None, ATT_WIDTH, past), lambda b, i: (b, 0, 0)),
                     pl.BlockSpec((None, past, N_HEADS, HEAD_W), lambda b, i: (b, 0, 0, 0))]
        args += [ctx_k, ctx_v]
    kv_spec = pl.BlockSpec((n_seq * seq_len, ATT_WIDTH), lambda b, i: (b, 0))
    in_specs += [kv_spec, kv_spec]
    args += [k, v]
    out_specs = [pl.BlockSpec((n_seq * tq, ATT_WIDTH), lambda b, i: (b * n_q + i, 0))]
    out_shape = [jax.ShapeDtypeStruct((n_batch * seq_len, ATT_WIDTH), BF16)]
    n_steps = (n_batch // n_seq) * n_q
    for w in cast_weights:
        rows = w.shape[0] // n_steps
        spec = pl.BlockSpec((rows, w.shape[1]), lambda b, i: (b * n_q + i, 0))
        in_specs.append(spec)
        args.append(w)
        out_specs.append(spec)
        out_shape.append(jax.ShapeDtypeStruct(w.shape, BF16))
    outs = pl.pallas_call(
        functools.partial(_attn_kernel, has_ctx=has_ctx, lam_init=lam_init,
                          n_seq=n_seq, seq_len=seq_len, tq=tq, n_cast=len(cast_weights)),
        grid=(n_batch // n_seq, n_q),
        in_specs=in_specs,
        out_specs=out_specs,
        out_shape=out_shape,
        compiler_params=_cparams(2),
        name="diff_attention",
    )(*args)
    return outs[0], tuple(outs[1:])


def _cmul(ar, ai, br, bi):
    return ar * br - ai * bi, ar * bi + ai * br


def _ssm_prep_kernel(lre_ref, lim_ref, ls_ref, bre_ref, bim_ref, cre_ref, cim_ref, d_ref,
                     mt_ref, gt_ref, wo_ref, at_ref):
    lane = lax.broadcasted_iota(jnp.int32, (SSM_GROUP, CW), 1)
    chan = lax.broadcasted_iota(jnp.int32, (SSM_GROUP, CW), 0)
    for gi in range(GROUP_BLOCK):
        gt_cols, wo_cols, at_cols, toeplitz = [], [], [], []
        for dr in range(2):
            lr = jnp.minimum(lre_ref[dr, gi], -1e-4)
            li = lim_ref[dr, gi]
            step = jnp.exp(ls_ref[dr, gi])
            mag = jnp.exp(lr * step)
            a_re = mag * jnp.cos(li * step)
            a_im = mag * jnp.sin(li * step)
            den = lr * lr + li * li
            nr = a_re - 1.0
            f_re = (nr * lr + a_im * li) / den
            f_im = (a_im * lr - nr * li) / den
            bt_re, bt_im = bre_ref[dr, gi], bim_ref[dr, gi]
            bb_re, bb_im = _cmul(f_re, f_im, bt_re, bt_im)
            c_re, c_im = cre_ref[dr, gi], cim_ref[dr, gi]
            pw = [(jnp.ones_like(a_re), jnp.zeros_like(a_im))]
            for _ in range(CHUNK):
                pw.append(_cmul(pw[-1][0], pw[-1][1], a_re, a_im))
            g_re, g_im, e_re, e_im = [], [], [], []
            for t in range(CHUNK):
                pr, pi = pw[CHUNK - 1 - t] if dr == 0 else pw[t]
                r, i = _cmul(bb_re, bb_im, pr, pi)
                g_re.append(r)
                g_im.append(i)
                pr, pi = pw[t + 1] if dr == 0 else pw[CHUNK - t]
                r, i = _cmul(c_re, c_im, pr, pi)
                e_re.append(r)
                e_im.append(-i)
            g_cat = jnp.concatenate([jnp.concatenate(g_re, axis=0),
                                     jnp.concatenate(g_im, axis=0)], axis=1)
            gt_cols.append(g_cat)
            wo_cols.append(jnp.concatenate([jnp.concatenate(e_re, axis=0),
                                            jnp.concatenate(e_im, axis=0)], axis=1))
            c_cat = jnp.concatenate([c_re, -c_im], axis=1)
            toeplitz.append(lax.dot_general(c_cat, g_cat, NT_DIMS,
                                            precision=lax.Precision.HIGHEST,
                                            preferred_element_type=F32))
            apw = [pw[CHUNK]]
            for _ in range(SCAN_BLOCK - 1):
                apw.append(_cmul(apw[-1][0], apw[-1][1], apw[0][0], apw[0][1]))
            order = list(range(SCAN_BLOCK)) if dr == 0 else list(range(SCAN_BLOCK - 1, -1, -1))
            order += [2 ** l - 1 for l in range(SCAN_LEVELS)]
            order += [0] * (AT_ROWS - len(order))
            at_cols += [jnp.concatenate([jnp.concatenate([apw[i][0], apw[i][0]], axis=1)
                                         for i in order], axis=0),
                        jnp.concatenate([jnp.concatenate([-apw[i][1], apw[i][1]], axis=1)
                                         for i in order], axis=0)]
        kf_rev, kb = toeplitz
        d_skip = d_ref[gi]
        blocks = []
        for t in range(CHUNK):
            fwd = pltpu.roll(kf_rev, (CW - (CHUNK - 1 - t) * SSM_GROUP) % CW, axis=1)
            bwd = pltpu.roll(kb, t * SSM_GROUP, axis=1)
            blocks.append(jnp.where(lane < (t + 1) * SSM_GROUP, fwd, 0.0)
                          + jnp.where(lane >= t * SSM_GROUP, bwd, 0.0)
                          + jnp.where(lane == chan + t * SSM_GROUP, d_skip, 0.0))
        mt_ref[gi] = jnp.concatenate(blocks, axis=0).astype(mt_ref.dtype)
        gt_ref[gi] = jnp.concatenate(gt_cols, axis=1).astype(gt_ref.dtype)
        wo_ref[gi] = jnp.concatenate(wo_cols, axis=1).astype(wo_ref.dtype)
        at_ref[gi] = jnp.concatenate(at_cols, axis=1)


N_PREP_IN, N_PREP_OUT, N_MOD_IN = 8, 4, 4


def _prep_mod_kernel(*refs):
    prep_in = refs[:N_PREP_IN]
    mod_in = refs[N_PREP_IN:N_PREP_IN + N_MOD_IN]
    w_in_ref = refs[N_PREP_IN + N_MOD_IN]
    outs = refs[N_PREP_IN + N_MOD_IN + 1:]
    _ssm_prep_kernel(*prep_in, *outs[:N_PREP_OUT])
    _mod_kernel(*mod_in, outs[N_PREP_OUT])
    outs[N_PREP_OUT + 1][...] = w_in_ref[...].astype(BF16)


def _ssm_prep_and_modulation(lam_re, lam_im, log_step, b_re, b_im, c_re, c_im, d_skip,
                             c_ctx, c, w_mod, b_mod, w_in):
    row = lambda a: a.reshape(2, N_GROUPS, 1, SSM_STATE)
    bt = lambda a: jnp.swapaxes(a, 2, 3)
    d_row = jnp.tile((d_skip[0] + d_skip[1]).reshape(N_GROUPS, 1, SSM_GROUP), (1, 1, CHUNK))
    gb = GROUP_BLOCK
    n_steps = N_GROUPS // gb
    vec_spec = pl.BlockSpec((2, gb, 1, SSM_STATE), lambda i: (0, i, 0, 0))
    mat_spec = pl.BlockSpec((2, gb, SSM_GROUP, SSM_STATE), lambda i: (0, i, 0, 0))
    w_spec = pl.BlockSpec((gb, CW, CW), lambda i: (i, 0, 0))
    w_shape = jax.ShapeDtypeStruct((N_GROUPS, CW, CW), BF16)
    n_mod = w_mod.shape[1]
    tk = D_MODEL // n_steps
    assert 1 + c.shape[0] <= MOD_ROWS
    outs = pl.pallas_call(
        _prep_mod_kernel,
        grid=(n_steps,),
        in_specs=[vec_spec, vec_spec,
                  pl.BlockSpec((2, gb, 1, 1), lambda i: (0, i, 0, 0)),
                  mat_spec, mat_spec, mat_spec, mat_spec,
                  pl.BlockSpec((gb, 1, CW), lambda i: (i, 0, 0)),
                  pl.BlockSpec((1, tk), lambda k: (0, k)),
                  pl.BlockSpec((c.shape[0], tk), lambda k: (0, k)),
                  pl.BlockSpec((tk, n_mod), lambda k: (k, 0)),
                  pl.BlockSpec((1, n_mod), lambda k: (0, 0)),
                  pl.BlockSpec((tk, w_in.shape[1]), lambda k: (k, 0))],
        out_specs=[w_spec, w_spec, w_spec,
                   pl.BlockSpec((gb, AT_ROWS, 4 * 2 * SSM_STATE), lambda i: (i, 0, 0)),
                   pl.BlockSpec((MOD_ROWS, 1, n_mod), lambda k: (0, 0, 0)),
                   pl.BlockSpec((tk, w_in.shape[1]), lambda k: (k, 0))],
        out_shape=[w_shape, w_shape, w_shape,
                   jax.ShapeDtypeStruct((N_GROUPS, AT_ROWS, 4 * 2 * SSM_STATE), F32),
                   jax.ShapeDtypeStruct((MOD_ROWS, 1, n_mod), F32),
                   jax.ShapeDtypeStruct(w_in.shape, BF16)],
        compiler_params=_cparams(1),
        name="ssm_prep_modulation",
    )(row(lam_re), row(lam_im), log_step.reshape(2, N_GROUPS, 1, 1),
      bt(b_re), bt(b_im), c_re, c_im, d_row,
      c_ctx.reshape(1, D_MODEL), c, w_mod, b_mod.reshape(1, n_mod), w_in)
    return outs[:N_PREP_OUT], outs[N_PREP_OUT], outs[N_PREP_OUT + 1]


def _shift_rows(x, m, down):
    n = x.shape[0]
    return pltpu.roll(x, m if down else n - m, axis=0)


def _ssm_kernel(*refs, n_seq, has_h0):
    if has_h0:
        (xt_ref, mt_ref, gt_ref, wo_ref, at_ref, wg_ref, bg_ref, h0_ref,
         z_hbm, zs_ref, zb_ref, zb_sem) = refs
    else:
        (xt_ref, mt_ref, gt_ref, wo_ref, at_ref, wg_ref, bg_ref,
         z_hbm, st_ref, zs_ref, zb_ref, zb_sem, fin_ref) = refs
    step = pl.program_id(0)
    n_rows = xt_ref.shape[-1]
    seg = n_rows // n_seq
    cw2 = 2 * SSM_STATE
    assert seg % SCAN_BLOCK == 0
    n_blk = n_rows // SCAN_BLOCK
    blk_per_seq = seg // SCAN_BLOCK
    pos = lax.broadcasted_iota(jnp.int32, (n_rows, cw2), 0) % seg
    row_blk = lax.broadcasted_iota(jnp.int32, (SCAN_BLOCK, cw2), 0)

    def low_half(shape):
        return lax.broadcasted_iota(jnp.int32, shape, 1) < SSM_STATE

    def swap(v):
        return pltpu.roll(v, SSM_STATE, axis=1)

    def to_planes(va, vb):
        lo = low_half((va.shape[0], cw2))
        va_l = pltpu.roll(va, 3 * SSM_STATE, axis=1)
        vb_r = pltpu.roll(vb, SSM_STATE, axis=1)
        return (jnp.where(lo, va[:, 0:cw2], vb_r[:, 0:cw2]),
                jnp.where(lo, va_l[:, 0:cw2], vb[:, 0:cw2]),
                jnp.where(lo, va[:, cw2:2 * cw2], vb_r[:, cw2:2 * cw2]),
                jnp.where(lo, va_l[:, cw2:2 * cw2], vb[:, cw2:2 * cw2]))

    def from_planes(f_re, f_im, b_re, b_im):
        lo = low_half(f_re.shape)
        va = jnp.concatenate([jnp.where(lo, f_re, swap(f_im)), jnp.where(lo, b_re, swap(b_im))], axis=1)
        vb = jnp.concatenate([jnp.where(lo, swap(f_re), f_im), jnp.where(lo, swap(b_re), b_im)], axis=1)
        return va, vb

    for ga in range(0, GROUP_BLOCK, 2):
        gb = ga + 1
        xts = [xt_ref[ga], xt_ref[gb]]
        s_pair = [lax.dot_general(xts[i], gt_ref[g], TN_DIMS, preferred_element_type=F32)
                  for i, g in enumerate((ga, gb))]
        planes = to_planes(*s_pair)
        lo_t = low_half((AT_ROWS, cw2))
        if has_h0:
            h0_planes = to_planes(h0_ref[ga], h0_ref[gb])
        ent_planes = []
        for dr in range(2):
            down = dr == 0
            pa, pb = (at_ref[g][:, (2 * dr) * cw2:(2 * dr + 1) * cw2] for g in (ga, gb))
            qa, qb = (at_ref[g][:, (2 * dr + 1) * cw2:(2 * dr + 2) * cw2] for g in (ga, gb))
            ar_tab = jnp.where(lo_t, pa, pb)
            ai_tab = jnp.where(lo_t, -qa, qb)
            re3 = planes[2 * dr].reshape(n_blk, SCAN_BLOCK, cw2)
            im3 = planes[2 * dr + 1].reshape(n_blk, SCAN_BLOCK, cw2)
            for lvl in range(SCAN_LEVELS):
                m = 2 ** lvl
                valid = (row_blk >= m) if down else (row_blk < SCAN_BLOCK - m)
                row = SCAN_BLOCK + lvl
                ar = jnp.where(valid, ar_tab[row:row + 1], 0.0)
                ai = jnp.where(valid, ai_tab[row:row + 1], 0.0)
                shift = m if down else SCAN_BLOCK - m
                sh_re = pltpu.roll(re3, shift, axis=1)
                sh_im = pltpu.roll(im3, shift, axis=1)
                re3, im3 = re3 + ar * sh_re - ai * sh_im, im3 + ar * sh_im + ai * sh_re
            ar_blk, ai_blk = ar_tab[0:SCAN_BLOCK], ai_tab[0:SCAN_BLOCK]
            blk_re = [re3[i] for i in range(n_blk)]
            blk_im = [im3[i] for i in range(n_blk)]
            edge = slice(SCAN_BLOCK - 1, SCAN_BLOCK) if down else slice(0, 1)
            for q_i in range(n_seq):
                idxs = list(range(q_i * blk_per_seq, (q_i + 1) * blk_per_seq))
                idxs = idxs if down else idxs[::-1]
                for prev, cur in zip([None] + idxs[:-1], idxs):
                    if prev is not None:
                        c_re, c_im = blk_re[prev][edge], blk_im[prev][edge]
                    elif has_h0:
                        c_re = h0_planes[2 * dr][q_i:q_i + 1]
                        c_im = h0_planes[2 * dr + 1][q_i:q_i + 1]
                    else:
                        continue
                    c_re = jnp.broadcast_to(c_re, (SCAN_BLOCK, cw2))
                    c_im = jnp.broadcast_to(c_im, (SCAN_BLOCK, cw2))
                    blk_re[cur] = blk_re[cur] + ar_blk * c_re - ai_blk * c_im
                    blk_im[cur] = blk_im[cur] + ar_blk * c_im + ai_blk * c_re
            for part, blks in ((0, blk_re), (1, blk_im)):
                s = jnp.concatenate(blks, axis=0)
                if not has_h0:
                    fin_ref[part] = s
                ent = _shift_rows(s, 1, down)
                ent = jnp.where((pos >= 1) if down else (pos < seg - 1), ent, 0.0)
                if has_h0:
                    ent_blk = [ent[i * SCAN_BLOCK:(i + 1) * SCAN_BLOCK] for i in range(n_blk)]
                    at_edge = row_blk == (0 if down else SCAN_BLOCK - 1)
                    for q_i in range(n_seq):
                        bi = q_i * blk_per_seq if down else (q_i + 1) * blk_per_seq - 1
                        ent_blk[bi] = jnp.where(
                            at_edge, h0_planes[2 * dr + part][q_i:q_i + 1], ent_blk[bi])
                    ent = jnp.concatenate(ent_blk, axis=0)
                ent_planes.append(ent)
            if not has_h0:
                rows = pl.ds(seg - 1 if down else 0, n_seq, stride=seg)
                f_re, f_im = fin_ref[0, rows, :], fin_ref[1, rows, :]
                lo_s = low_half((n_seq, cw2))
                st_ref[ga, :, dr * cw2:(dr + 1) * cw2] = jnp.where(lo_s, f_re, swap(f_im))
                st_ref[gb, :, dr * cw2:(dr + 1) * cw2] = jnp.where(lo_s, swap(f_re), f_im)
        h_pair = from_planes(*ent_planes)
        for i, g in enumerate((ga, gb)):
            yt = (jnp.dot(mt_ref[g], xts[i], preferred_element_type=F32)
                  + lax.dot_general(wo_ref[g], h_pair[i].astype(BF16), NT_DIMS,
                                    preferred_element_type=F32))
            z = jax.nn.gelu(yt, approximate=True)
            grp = step * GROUP_BLOCK + g
            for t in range(CHUNK):
                zs_ref[t, pl.ds(pl.multiple_of(grp * SSM_GROUP, SSM_GROUP), SSM_GROUP), :] = (
                    z[t * SSM_GROUP:(t + 1) * SSM_GROUP, :])

    @pl.when(step == pl.num_programs(0) - 1)
    def _():
        def put(t):
            return pltpu.make_async_copy(zb_ref.at[t], z_hbm.at[:, t, :], zb_sem.at[t])

        for t0 in range(0, CHUNK, Z_BATCH):
            for t in range(t0, t0 + Z_BATCH):
                zt = zs_ref[t]
                gate = jnp.dot(wg_ref[...], zt.astype(BF16),
                               preferred_element_type=F32) + bg_ref[...]
                zb_ref[t] = (zt * jax.nn.sigmoid(gate)).T
            for t in range(t0, t0 + Z_BATCH):
                put(t).start()
        for t in range(CHUNK):
            put(t).wait()


def _ssm(xt, mt, gt, wo, at, w_glu_t, b_glu_col, h0, n_seq):
    n_rows = xt.shape[-1]
    has_h0 = h0 is not None
    gb = GROUP_BLOCK
    w_spec = pl.BlockSpec((gb, CW, CW), lambda i: (i, 0, 0))
    in_specs = [pl.BlockSpec((gb, CW, n_rows), lambda i: (i, 0, 0)),
                w_spec, w_spec, w_spec,
                pl.BlockSpec((gb, AT_ROWS, 4 * 2 * SSM_STATE), lambda i: (i, 0, 0)),
                pl.BlockSpec((SSM_WIDTH, SSM_WIDTH), lambda i: (0, 0)),
                pl.BlockSpec((SSM_WIDTH, 1), lambda i: (0, 0))]
    args = [xt, mt, gt, wo, at, w_glu_t, b_glu_col]
    out_specs = [pl.BlockSpec(memory_space=pl.ANY)]
    out_shape = [jax.ShapeDtypeStruct((n_rows, CHUNK, SSM_WIDTH), F32)]
    scratch = [pltpu.VMEM((CHUNK, SSM_WIDTH, n_rows), F32),
               pltpu.VMEM((CHUNK, n_rows, SSM_WIDTH), F32),
               pltpu.SemaphoreType.DMA((CHUNK,))]
    if has_h0:
        in_specs.append(pl.BlockSpec((gb, n_seq, CW), lambda i: (i, 0, 0)))
        args.append(h0)
    else:
        out_specs.append(pl.BlockSpec((gb, n_seq, CW), lambda i: (i, 0, 0)))
        out_shape.append(jax.ShapeDtypeStruct((N_GROUPS, n_seq, CW), F32))
        scratch.append(pltpu.VMEM((2, n_rows, 2 * SSM_STATE), F32))
    return pl.pallas_call(
        functools.partial(_ssm_kernel, n_seq=n_seq, has_h0=has_h0),
        grid=(N_GROUPS // gb,),
        in_specs=in_specs,
        out_specs=out_specs,
        out_shape=out_shape,
        scratch_shapes=scratch,
        compiler_params=_cparams(1),
        name="ssm_scan_glu",
    )(*args)


FF_CHUNK = 256
FFN_TOK = 1024


def _out_ffn_kernel(x_ref, attn_ref, z_ref, mod_ref, g_ref, wo_ref, wfi_ref, wfo_ref,
                    o_ref, act_ref):
    gate1 = mod_ref[:, 2 * D_MODEL:3 * D_MODEL]
    shift2 = mod_ref[:, 3 * D_MODEL:4 * D_MODEL]
    scale2 = mod_ref[:, 4 * D_MODEL:5 * D_MODEL]
    gate2 = mod_ref[:, 5 * D_MODEL:6 * D_MODEL]
    half = FFN_TOK // 2
    gain2 = g_ref[2:3, :] * (1.0 + scale2)

    def pre_ffn(hf):
        r = slice(hf * half, (hf + 1) * half)
        mixer = jnp.concatenate([attn_ref[r, :], z_ref[r, :].astype(BF16)], axis=1)
        mix = jnp.dot(mixer, wo_ref[...], preferred_element_type=F32)
        x1 = x_ref[r, :] + gate1 * _rms(mix, g_ref[1:2, :])
        ms = jnp.mean(x1 * x1, axis=-1, keepdims=True)
        return x1, (x1 * lax.rsqrt(ms + NORM_EPS) * gain2 + shift2).astype(BF16)

    def ffn_in(hf, h, chunks):
        r = slice(hf * half, (hf + 1) * half)
        for c in chunks:
            lo = c * FF_CHUNK
            gt = jnp.dot(h, wfi_ref[:, lo:lo + FF_CHUNK], preferred_element_type=F32)
            up = jnp.dot(h, wfi_ref[:, D_FF + lo:D_FF + lo + FF_CHUNK],
                         preferred_element_type=F32)
            act_ref[r, lo:lo + FF_CHUNK] = (_silu(gt) * up).astype(BF16)

    def ffn_out(hf, x1):
        r = slice(hf * half, (hf + 1) * half)
        f = jnp.dot(act_ref[r, :], wfo_ref[...], preferred_element_type=F32)
        o_ref[r, :] = x1 + gate2 * _rms(f, g_ref[3:4, :])

    n_chunks = D_FF // FF_CHUNK
    x1_a, h_a = pre_ffn(0)
    x1_b, h_b = pre_ffn(1)
    ffn_in(0, h_a, range(n_chunks))
    ffn_out(0, x1_a)
    ffn_in(1, h_b, range(n_chunks))
    ffn_out(1, x1_b)


def _out_ffn(x2d, attn, z, mods, mod_rows, norm_g, w_o, w_ffn_in, w_ffn_out):
    n_tok = x2d.shape[0]
    z2d = z.reshape(n_tok, SSM_WIDTH)
    n_steps = n_tok // FFN_TOK
    mod_row0, n_mod = mod_rows
    steps_per_mod = n_steps // n_mod
    const = lambda i: (0, 0)
    row_spec = lambda w: pl.BlockSpec((FFN_TOK, w), lambda i: (i, 0))
    return pl.pallas_call(
        _out_ffn_kernel,
        grid=(n_steps,),
        in_specs=[row_spec(D_MODEL), row_spec(ATT_WIDTH), row_spec(SSM_WIDTH),
                  pl.BlockSpec((None, 1, N_MOD * D_MODEL),
                               lambda i: (mod_row0 + i // steps_per_mod, 0, 0)),
                  pl.BlockSpec((4, D_MODEL), const),
                  pl.BlockSpec((2 * ATT_WIDTH, D_MODEL), const, pipeline_mode=pl.Buffered(1)),
                  pl.BlockSpec((D_MODEL, 2 * D_FF), const, pipeline_mode=pl.Buffered(1)),
                  pl.BlockSpec((D_FF, D_MODEL), const, pipeline_mode=pl.Buffered(1))],
        out_specs=row_spec(D_MODEL),
        out_shape=jax.ShapeDtypeStruct((n_tok, D_MODEL), F32),
        scratch_shapes=[pltpu.VMEM((FFN_TOK, D_FF), BF16)],
        compiler_params=_cparams(1),
        name="out_proj_ffn",
    )(x2d, attn, z2d, mods, norm_g, w_o, w_ffn_in, w_ffn_out)


def _rope_tables(seq_len):
    t = np.arange(seq_len)
    row = (t // GRID_W).astype(np.float32)
    col = (t % GRID_W).astype(np.float32)
    half = HEAD_DIM // 2
    inv_freq = (np.float32(ROPE_BASE)
                ** (-np.arange(0, half, 2, dtype=np.float32) / np.float32(half))).astype(np.float32)
    ang_r = row[:, None] * inv_freq
    ang_c = col[:, None] * inv_freq
    ang = np.concatenate([ang_r, ang_r, ang_c, ang_c], axis=-1)
    cos, sin = np.cos(ang), np.sin(ang)
    upper = (np.arange(HEAD_DIM) % 32) < 16
    sa = np.where(upper, -sin, 0.0)
    sb = np.where(upper, 0.0, sin)
    two = lambda a: jnp.asarray(np.concatenate([a, a], axis=-1), dtype=F32)
    return two(cos), two(sa), two(sb)


def _layer(x, mods, mod_rows, lam_init, rope_tabs, ctx_k, ctx_v, h0, weights, prep):
    n_batch, seq_len = x.shape[:2]
    g = weights['norm_g']
    outs = _in_proj(x, mods, mod_rows, g[0:1], weights['w_in'], rope_tabs)
    q, k, v = outs[:3]
    pending = () if 'late_bf16' in weights else weights['late_f32']
    attn, cast = _attention(q, k, v, ctx_k, ctx_v, weights['lam'], weights['subln_g'],
                            n_batch, seq_len, lam_init, cast_weights=pending)
    if pending:
        weights['late_bf16'] = cast
    ssm_out = _ssm(outs[-1], *prep, weights['w_glu_t'], weights['b_glu_col'], h0, n_batch)
    y = _out_ffn(x.reshape(n_batch * seq_len, D_MODEL), attn, ssm_out[0], mods, mod_rows, g,
                 *weights['late_bf16'])
    return y.reshape(x.shape), outs[3:-1], ssm_out[1:]


def kernel(x_prompt, x_sample, cache_k, cache_v, state_ssm_re, state_ssm_im, c, c_ctx, w_mod, b_mod, norm_g, w_in, lam_params, subln_g, ssm_lambda_re, ssm_lambda_im, ssm_log_step, ssm_b_re, ssm_b_im, ssm_c_re, ssm_c_im, ssm_d, w_glu, b_glu, w_o, w_ffn_in, w_ffn_out):
    depth = w_mod.shape[0]
    assert depth == 1
    bp = x_prompt.shape[0]
    bd, ld_len = x_sample.shape[:2]
    past = cache_k.shape[2]
    xp, xs = x_prompt, x_sample
    rope_tabs = _rope_tables(ld_len)
    ks_out, vs_out, hr_out, hi_out = [], [], [], []
    for l in range(depth):
        lam_init = 0.8 - 0.6 * math.exp(-0.3 * l)
        prep, mods, w_in_bf16 = _ssm_prep_and_modulation(
            ssm_lambda_re[l], ssm_lambda_im[l], ssm_log_step[l], ssm_b_re[l], ssm_b_im[l],
            ssm_c_re[l], ssm_c_im[l], ssm_d[l], c_ctx, c, w_mod[l], b_mod[l], w_in[l])
        weights = {
            'norm_g': norm_g[l],
            'w_in': w_in_bf16,
            'lam': lam_params[l], 'subln_g': subln_g[l],
            'w_glu_t': w_glu[l].T.astype(BF16), 'b_glu_col': b_glu[l].reshape(SSM_WIDTH, 1),
            'late_f32': (w_o[l], w_ffn_in[l], w_ffn_out[l]),
        }
        ck = jnp.transpose(cache_k[:, l], (0, 2, 3, 1)).reshape(bd, ATT_WIDTH, past)
        cv = cache_v[:, l]
        h0 = jnp.stack([state_ssm_re[:, l], state_ssm_im[:, l]], axis=2)
        h0 = h0.transpose(3, 0, 1, 2, 4).reshape(N_GROUPS, bd, CW)
        xs, _, _ = _layer(xs, mods, (1, bd), lam_init, rope_tabs, ck, cv, h0, weights, prep)
        xp, (k_ctx, v_ctx), (st,) = _layer(xp, mods, (0, 1), lam_init, None, None, None, None,
                                           weights, prep)
        ks_out.append(jnp.swapaxes(k_ctx, 1, 2).reshape(bp, -1, 2 * N_HEADS, HEAD_DIM))
        vs_out.append(v_ctx)
        fin = st.reshape(N_GROUPS, bp, 2, 2, SSM_STATE).transpose(1, 2, 3, 0, 4)
        hr_out.append(fin[:, :, 0])
        hi_out.append(fin[:, :, 1])
    return (xp, xs, jnp.stack(ks_out, axis=1), jnp.stack(vs_out, axis=1),
            jnp.stack(hr_out, axis=1), jnp.stack(hi_out, axis=1))
```

```python
import functools
import math

import jax
import jax.numpy as jnp
import numpy as np
from jax import lax
from jax.experimental import pallas as pl
from jax.experimental.pallas import tpu as pltpu

F32 = jnp.float32
BF16 = jnp.bfloat16

D_MODEL = 1024
GRID_W = 64
ATT_WIDTH = 512
SSM_WIDTH = 512
HEAD_DIM = 64
N_HEADS = 4
HEAD_W = 2 * HEAD_DIM
SSM_GROUP = 16
N_GROUPS = 32
SSM_STATE = 64
D_FF = 2816
N_MOD = 6
ROPE_BASE = 10000.0
NORM_EPS = 1e-6

CHUNK = 16
CW = CHUNK * SSM_GROUP
SCAN_BLOCK = 8
SCAN_LEVELS = 3
AT_ROWS = 16
GROUP_BLOCK = 4
Z_BATCH = 8

ROWS_PER_TILE = 128
TOK_PER_TILE = ROWS_PER_TILE * CHUNK
SUB_TOK = 512
TQ_ITEM = 256
N_SUB = TOK_PER_TILE // SUB_TOK

VMEM_LIMIT = 56 * 1024 * 1024

NT_DIMS = (((1,), (1,)), ((), ()))
TN_DIMS = (((0,), (0,)), ((), ()))


def _cparams(n_axes):
    return pltpu.CompilerParams(
        dimension_semantics=("arbitrary",) * n_axes,
        vmem_limit_bytes=VMEM_LIMIT)


def _rms(x, g):
    ms = jnp.mean(x * x, axis=-1, keepdims=True)
    return x * lax.rsqrt(ms + NORM_EPS) * g


def _silu(x):
    return x * jax.nn.sigmoid(x)


MOD_ROWS = 8


def _mod_kernel(ctx_ref, c_ref, w_ref, b_ref, o_ref):
    n_lat, tk = c_ref.shape
    row = lax.broadcasted_iota(jnp.int32, (MOD_ROWS, tk), 0)
    cond = jnp.where(row == 0, ctx_ref[...], 0.0)
    for b in range(n_lat):
        cond = jnp.where(row == 1 + b, c_ref[b:b + 1, :], cond)
    part = jnp.dot(_silu(cond).astype(BF16), w_ref[...].astype(BF16),
                   preferred_element_type=F32)

    @pl.when(pl.program_id(0) == 0)
    def _():
        o_ref[:, 0, :] = part + b_ref[...]

    @pl.when(pl.program_id(0) > 0)
    def _():
        o_ref[:, 0, :] += part


def _rope(x, cos, sa, sb):
    return (x * cos + pltpu.roll(x, HEAD_W - 16, axis=1) * sa
            + pltpu.roll(x, 16, axis=1) * sb)


def _in_proj_kernel(*refs, rope, seqs_per_sub):
    x_ref, x3_hbm, mod_ref, g_ref, w_ref = refs[:5]
    refs = refs[5:]
    if rope:
        cos_ref, sa_ref, sb_ref = refs[:3]
        refs = refs[3:]
        q_ref, k_ref, v_ref, ut_ref, wut_ref, xt_ref, xt_sem = refs
    else:
        q_ref, k_ref, v_ref, kc_ref, vc_ref, ut_ref, wut_ref, xt_ref, xt_sem = refs
    tile = pl.program_id(0)
    j = pl.program_id(1)
    t_per_sub = CHUNK // N_SUB
    t_early = CHUNK - t_per_sub

    def gather(tile_idx, t):
        src = x3_hbm.at[pl.ds(tile_idx * ROWS_PER_TILE, ROWS_PER_TILE), t, :]
        return pltpu.make_async_copy(src, xt_ref.at[t], xt_sem.at[t])

    @pl.when(j == 0)
    def _():
        @pl.when(tile == 0)
        def _():
            for t in range(t_early):
                gather(0, t).start()
        for t in range(t_early, CHUNK):
            gather(tile, t).start()

    t_base = j * t_per_sub
    for d in range(t_per_sub):
        gather(tile, t_base + d).wait()

    @pl.when((j == N_SUB - 1) & (tile + 1 < pl.num_programs(0)))
    def _():
        for t in range(t_early):
            gather(tile + 1, t).start()

    @pl.when((tile == 0) & (j == 0))
    def _():
        wut_ref[...] = w_ref[:, 3 * ATT_WIDTH:].T

    shift = mod_ref[:, 0:D_MODEL]
    gain = g_ref[...] * (1.0 + mod_ref[:, D_MODEL:2 * D_MODEL])

    def norm_mod(xv):
        ms = jnp.mean(xv * xv, axis=-1, keepdims=True)
        return (xv * lax.rsqrt(ms + NORM_EPS) * gain + shift).astype(BF16)

    def ssm_input(d0):
        xt = jnp.concatenate([xt_ref[t_base + d0], xt_ref[t_base + d0 + 1]], axis=0)
        ut = lax.dot_general(wut_ref[...], norm_mod(xt), NT_DIMS,
                             preferred_element_type=F32)
        for d in range(2):
            blk = ut[:, d * ROWS_PER_TILE:(d + 1) * ROWS_PER_TILE]
            row0 = pl.multiple_of((t_base + d0 + d) * SSM_GROUP, SSM_GROUP)
            ut_ref[:, pl.ds(row0, SSM_GROUP), :] = (
                blk.reshape(N_GROUPS, SSM_GROUP, ROWS_PER_TILE).astype(ut_ref.dtype))

    proj = jnp.dot(norm_mod(x_ref[...]), w_ref[:, 0:3 * ATT_WIDTH],
                   preferred_element_type=F32)
    q = proj[:, 0:ATT_WIDTH]
    k = proj[:, ATT_WIDTH:2 * ATT_WIDTH]
    v = proj[:, 2 * ATT_WIDTH:3 * ATT_WIDTH]
    qscale = HEAD_DIM ** -0.5 * math.log2(math.e)
    if rope:
        cos, sa, sb = cos_ref[...], sa_ref[...], sb_ref[...]
        for hd in range(N_HEADS):
            sl = slice(hd * HEAD_W, (hd + 1) * HEAD_W)
            q_ref[:, sl] = (_rope(q[:, sl], cos, sa, sb) * qscale).astype(q_ref.dtype)
            k_ref[:, sl] = _rope(k[:, sl], cos, sa, sb).astype(k_ref.dtype)
    else:
        q_ref[...] = (q * qscale).astype(q_ref.dtype)
        k_ref[...] = k.astype(k_ref.dtype)
        seq = SUB_TOK // seqs_per_sub
        k_t = k.T
        for b in range(seqs_per_sub):
            kc_ref[b] = k_t[:, b * seq:(b + 1) * seq]
            for hd in range(N_HEADS):
                vc_ref[b, :, hd, :] = v[b * seq:(b + 1) * seq, hd * HEAD_W:(hd + 1) * HEAD_W]
    v_ref[...] = v.astype(v_ref.dtype)

    for d0 in range(0, t_per_sub, 2):
        ssm_input(d0)


def _in_proj(x, mods, mod_rows, g0, w_in, rope_tabs):
    n_batch, seq_len = x.shape[:2]
    n_tok = n_batch * seq_len
    n_rows = n_tok // CHUNK
    n_tiles = n_tok // TOK_PER_TILE
    mod_row0, n_mod = mod_rows
    tiles_per_mod = n_tiles // n_mod
    rope = rope_tabs is not None
    seqs_per_sub = max(1, SUB_TOK // seq_len)
    in_specs = [pl.BlockSpec((SUB_TOK, D_MODEL), lambda i, j: (i * N_SUB + j, 0)),
                pl.BlockSpec(memory_space=pl.ANY),
                pl.BlockSpec((None, 1, 2 * D_MODEL),
                             lambda i, j: (mod_row0 + i // tiles_per_mod, 0, 0)),
                pl.BlockSpec((1, D_MODEL), lambda i, j: (0, 0)),
                pl.BlockSpec((D_MODEL, 4 * ATT_WIDTH), lambda i, j: (0, 0))]
    args = [x.reshape(n_tok, D_MODEL), x.reshape(n_rows, CHUNK, D_MODEL), mods, g0, w_in]
    row_spec = pl.BlockSpec((SUB_TOK, ATT_WIDTH), lambda i, j: (i * N_SUB + j, 0))
    row_shape = jax.ShapeDtypeStruct((n_tok, ATT_WIDTH), BF16)
    out_specs = [row_spec, row_spec, row_spec]
    out_shape = [row_shape, row_shape, row_shape]
    if rope:
        assert seq_len == TOK_PER_TILE
        for tab in rope_tabs:
            in_specs.append(pl.BlockSpec((SUB_TOK, HEAD_W), lambda i, j: (j, 0)))
            args.append(tab)
    else:
        out_specs += [pl.BlockSpec((seqs_per_sub, ATT_WIDTH, seq_len),
                                   lambda i, j: (i * N_SUB + j, 0, 0)),
                      pl.BlockSpec((seqs_per_sub, seq_len, N_HEADS, HEAD_W),
                                   lambda i, j: (i * N_SUB + j, 0, 0, 0))]
        out_shape += [jax.ShapeDtypeStruct((n_batch, ATT_WIDTH, seq_len), F32),
                      jax.ShapeDtypeStruct((n_batch, seq_len, N_HEADS, HEAD_W), F32)]
    out_specs.append(pl.BlockSpec((N_GROUPS, CW, ROWS_PER_TILE), lambda i, j: (0, 0, i)))
    out_shape.append(jax.ShapeDtypeStruct((N_GROUPS, CW, n_rows), BF16))
    return pl.pallas_call(
        functools.partial(_in_proj_kernel, rope=rope, seqs_per_sub=seqs_per_sub),
        grid=(n_tiles, N_SUB),
        in_specs=in_specs,
        out_specs=out_specs,
        out_shape=out_shape,
        scratch_shapes=[pltpu.VMEM((SSM_WIDTH, D_MODEL), BF16),
                        pltpu.VMEM((CHUNK, ROWS_PER_TILE, D_MODEL), F32),
                        pltpu.SemaphoreType.DMA((CHUNK,))],
        compiler_params=_cparams(2),
        name="in_proj",
    )(*args)


def _attn_kernel(*refs, has_ctx, lam_init, n_seq, seq_len, tq):
    if has_ctx:
        lam_ref, sg_ref, q_ref, ck_ref, cv_ref, k_ref, v_ref, o_ref = refs
    else:
        lam_ref, sg_ref, q_ref, k_ref, v_ref, o_ref = refs
    lp = lam_ref[...]
    lam = (jnp.exp(jnp.sum(lp[0:1] * lp[1:2], axis=-1, keepdims=True))
           - jnp.exp(jnp.sum(lp[2:3] * lp[3:4], axis=-1, keepdims=True)) + lam_init)
    first_map = lax.broadcasted_iota(jnp.int32, (1, HEAD_W), 1) < HEAD_DIM
    ti = min(TQ_ITEM, tq)
    for row0 in range(0, n_seq * tq, ti):
        b = row0 // tq
        q_rows = slice(row0, row0 + ti)
        for hd in range(N_HEADS):
            sl = slice(hd * HEAD_W, (hd + 1) * HEAD_W)
            qh = q_ref[q_rows, sl]
            zero = jnp.zeros_like(qh)
            qs = jnp.concatenate([jnp.where(first_map, qh, zero),
                                  jnp.where(first_map, zero, qh)], axis=0)
            kv_rows = slice(b * seq_len, (b + 1) * seq_len)
            parts = [(k_ref[kv_rows, sl], v_ref[kv_rows, sl])]
            scores = [lax.dot_general(qs, parts[0][0], NT_DIMS, preferred_element_type=F32)]
            if has_ctx:
                parts.insert(0, (None, cv_ref[:, hd, :].astype(BF16)))
                scores.insert(0, jnp.dot(qs, ck_ref[sl, :].astype(BF16),
                                         preferred_element_type=F32))
            mx = scores[0].max(axis=-1, keepdims=True)
            for s in scores[1:]:
                mx = jnp.maximum(mx, s.max(axis=-1, keepdims=True))
            acc = None
            for s, (_, vv) in zip(scores, parts):
                e = jnp.exp2(s - mx).astype(BF16)
                v_one = jnp.concatenate([vv, jnp.ones_like(vv)], axis=1)
                pv = jnp.dot(e, v_one, preferred_element_type=F32)
                acc = pv if acc is None else acc + pv
            num = acc[:, 0:HEAD_W] / acc[:, HEAD_W:2 * HEAD_W]
            o = num[0:ti] - lam * num[ti:2 * ti]
            o = _rms(o, sg_ref[...]) * (1.0 - lam_init)
            o_ref[q_rows, sl] = o.astype(o_ref.dtype)


def _attention(q, k, v, ctx_k, ctx_v, lam_params, subln_g, n_batch, seq_len, lam_init):
    has_ctx = ctx_k is not None
    tq = min(1024, seq_len)
    n_q = seq_len // tq
    n_seq = 1 if n_q > 1 else min(4, n_batch)
    in_specs = [pl.BlockSpec((4, HEAD_DIM), lambda b, i: (0, 0)),
                pl.BlockSpec((1, HEAD_W), lambda b, i: (0, 0)),
                pl.BlockSpec((n_seq * tq, ATT_WIDTH), lambda b, i: (b * n_q + i, 0))]
    args = [lam_params, subln_g.reshape(1, HEAD_W), q]
    if has_ctx:
        past = ctx_v.shape[1]
        in_specs +=[pl.BlockSpec((None, ATT_WIDTH, past), lambda b, i: (b, 0, 0)),
                     pl.BlockSpec((None, past, N_HEADS, HEAD_W), lambda b, i: (b, 0, 0, 0))]
        args += [ctx_k, ctx_v]
    kv_spec = pl.BlockSpec((n_seq * seq_len, ATT_WIDTH), lambda b, i: (b, 0))
    in_specs += [kv_spec, kv_spec]
    args += [k, v]
    return pl.pallas_call(
        functools.partial(_attn_kernel, has_ctx=has_ctx, lam_init=lam_init,
                          n_seq=n_seq, seq_len=seq_len, tq=tq),
        grid=(n_batch // n_seq, n_q),
        in_specs=in_specs,
        out_specs=pl.BlockSpec((n_seq * tq, ATT_WIDTH), lambda b, i: (b * n_q + i, 0)),
        out_shape=jax.ShapeDtypeStruct((n_batch * seq_len, ATT_WIDTH), BF16),
        compiler_params=_cparams(2),
        name="diff_attention",
    )(*args)


def _cmul(ar, ai, br, bi):
    return ar * br - ai * bi, ar * bi + ai * br


def _ssm_prep_kernel(lre_ref, lim_ref, ls_ref, bre_ref, bim_ref, cre_ref, cim_ref, d_ref,
                     mt_ref, gt_ref, wo_ref, at_ref):
    lane = lax.broadcasted_iota(jnp.int32, (SSM_GROUP, CW), 1)
    chan = lax.broadcasted_iota(jnp.int32, (SSM_GROUP, CW), 0)
    for gi in range(GROUP_BLOCK):
        gt_cols, wo_cols, at_cols, toeplitz = [], [], [], []
        for dr in range(2):
            lr = jnp.minimum(lre_ref[dr, gi], -1e-4)
            li = lim_ref[dr, gi]
            step = jnp.exp(ls_ref[dr, gi])
            mag = jnp.exp(lr * step)
            a_re = mag * jnp.cos(li * step)
            a_im = mag * jnp.sin(li * step)
            den = lr * lr + li * li
            nr = a_re - 1.0
            f_re = (nr * lr + a_im * li) / den
            f_im = (a_im * lr - nr * li) / den
            bt_re, bt_im = bre_ref[dr, gi], bim_ref[dr, gi]
            bb_re, bb_im = _cmul(f_re, f_im, bt_re, bt_im)
            c_re, c_im = cre_ref[dr, gi], cim_ref[dr, gi]
            pw = [(jnp.ones_like(a_re), jnp.zeros_like(a_im))]
            for _ in range(CHUNK):
                pw.append(_cmul(pw[-1][0], pw[-1][1], a_re, a_im))
            g_re, g_im, e_re, e_im = [], [], [], []
            for t in range(CHUNK):
                pr, pi = pw[CHUNK - 1 - t] if dr == 0 else pw[t]
                r, i = _cmul(bb_re, bb_im, pr, pi)
                g_re.append(r)
                g_im.append(i)
                pr, pi = pw[t + 1] if dr == 0 else pw[CHUNK - t]
                r, i = _cmul(c_re, c_im, pr, pi)
                e_re.append(r)
                e_im.append(-i)
            g_cat = jnp.concatenate([jnp.concatenate(g_re, axis=0),
                                     jnp.concatenate(g_im, axis=0)], axis=1)
            gt_cols.append(g_cat)
            wo_cols.append(jnp.concatenate([jnp.concatenate(e_re, axis=0),
                                            jnp.concatenate(e_im, axis=0)], axis=1))
            c_cat = jnp.concatenate([c_re, -c_im], axis=1)
            toeplitz.append(lax.dot_general(c_cat, g_cat, NT_DIMS,
                                            precision=lax.Precision.HIGHEST,
                                            preferred_element_type=F32))
            apw = [pw[CHUNK]]
            for _ in range(SCAN_BLOCK - 1):
                apw.append(_cmul(apw[-1][0], apw[-1][1], apw[0][0], apw[0][1]))
            order = list(range(SCAN_BLOCK)) if dr == 0 else list(range(SCAN_BLOCK - 1, -1, -1))
            order += [2 ** l - 1 for l in range(SCAN_LEVELS)]
            order += [0] * (AT_ROWS - len(order))
            at_cols += [jnp.concatenate([jnp.concatenate([apw[i][0], apw[i][0]], axis=1)
                                         for i in order], axis=0),
                        jnp.concatenate([jnp.concatenate([-apw[i][1], apw[i][1]], axis=1)
                                         for i in order], axis=0)]
        kf_rev, kb = toeplitz
        d_skip = d_ref[gi]
        blocks = []
        for t in range(CHUNK):
            fwd = pltpu.roll(kf_rev, (CW - (CHUNK - 1 - t) * SSM_GROUP) % CW, axis=1)
            bwd = pltpu.roll(kb, t * SSM_GROUP, axis=1)
            blocks.append(jnp.where(lane < (t + 1) * SSM_GROUP, fwd, 0.0)
                          + jnp.where(lane >= t * SSM_GROUP, bwd, 0.0)
                          + jnp.where(lane == chan + t * SSM_GROUP, d_skip, 0.0))
        mt_ref[gi] = jnp.concatenate(blocks, axis=0).astype(mt_ref.dtype)
        gt_ref[gi] = jnp.concatenate(gt_cols, axis=1).astype(gt_ref.dtype)
        wo_ref[gi] = jnp.concatenate(wo_cols, axis=1).astype(wo_ref.dtype)
        at_ref[gi] = jnp.concatenate(at_cols, axis=1)


N_PREP_IN, N_PREP_OUT, N_MOD_IN = 8, 4, 4


def _prep_mod_kernel(*refs, n_cast):
    prep_in = refs[:N_PREP_IN]
    mod_in = refs[N_PREP_IN:N_PREP_IN + N_MOD_IN]
    cast_in = refs[N_PREP_IN + N_MOD_IN:N_PREP_IN + N_MOD_IN + n_cast]
    outs = refs[N_PREP_IN + N_MOD_IN + n_cast:]
    _ssm_prep_kernel(*prep_in, *outs[:N_PREP_OUT])
    _mod_kernel(*mod_in, outs[N_PREP_OUT])
    for src, dst in zip(cast_in, outs[N_PREP_OUT + 1:]):
        dst[...] = src[...].astype(dst.dtype)


def _ssm_prep_and_modulation(lam_re, lam_im, log_step, b_re, b_im, c_re, c_im, d_skip,
                             c_ctx, c, w_mod, b_mod, cast_weights):
    row = lambda a: a.reshape(2, N_GROUPS, 1, SSM_STATE)
    bt = lambda a: jnp.swapaxes(a, 2, 3)
    d_row = jnp.tile((d_skip[0] + d_skip[1]).reshape(N_GROUPS, 1, SSM_GROUP), (1, 1, CHUNK))
    gb = GROUP_BLOCK
    n_steps = N_GROUPS // gb
    vec_spec = pl.BlockSpec((2, gb, 1, SSM_STATE), lambda i: (0, i, 0, 0))
    mat_spec = pl.BlockSpec((2, gb, SSM_GROUP, SSM_STATE), lambda i: (0, i, 0, 0))
    w_spec = pl.BlockSpec((gb, CW, CW), lambda i: (i, 0, 0))
    w_shape = jax.ShapeDtypeStruct((N_GROUPS, CW, CW), BF16)
    n_mod = w_mod.shape[1]
    tk = D_MODEL // n_steps
    assert 1 + c.shape[0] <= MOD_ROWS
    cast_specs = [pl.BlockSpec((w.shape[0] // n_steps, w.shape[1]), lambda k: (k, 0))
                  for w in cast_weights]
    outs = pl.pallas_call(
        functools.partial(_prep_mod_kernel, n_cast=len(cast_weights)),
        grid=(n_steps,),
        in_specs=[vec_spec, vec_spec,
                  pl.BlockSpec((2, gb, 1, 1), lambda i: (0, i, 0, 0)),
                  mat_spec, mat_spec, mat_spec, mat_spec,
                  pl.BlockSpec((gb, 1, CW), lambda i: (i, 0, 0)),
                  pl.BlockSpec((1, tk), lambda k: (0, k)),
                  pl.BlockSpec((c.shape[0], tk), lambda k: (0, k)),
                  pl.BlockSpec((tk, n_mod), lambda k: (k, 0)),
                  pl.BlockSpec((1, n_mod), lambda k: (0, 0))] + cast_specs,
        out_specs=[w_spec, w_spec, w_spec,
                   pl.BlockSpec((gb, AT_ROWS, 4 * 2 * SSM_STATE), lambda i: (i, 0, 0)),
                   pl.BlockSpec((MOD_ROWS, 1, n_mod), lambda k: (0, 0, 0))] + cast_specs,
        out_shape=[w_shape, w_shape, w_shape,
                   jax.ShapeDtypeStruct((N_GROUPS, AT_ROWS, 4 * 2 * SSM_STATE), F32),
                   jax.ShapeDtypeStruct((MOD_ROWS, 1, n_mod), F32)]
        + [jax.ShapeDtypeStruct(w.shape, BF16) for w in cast_weights],
        compiler_params=_cparams(1),
        name="ssm_prep_modulation",
    )(row(lam_re), row(lam_im), log_step.reshape(2, N_GROUPS, 1, 1),
      bt(b_re), bt(b_im), c_re, c_im, d_row,
      c_ctx.reshape(1, D_MODEL), c, w_mod, b_mod.reshape(1, n_mod), *cast_weights)
    return outs[:N_PREP_OUT], outs[N_PREP_OUT], outs[N_PREP_OUT + 1:]


def _shift_rows(x, m, down):
    n = x.shape[0]
    return pltpu.roll(x, m if down else n - m, axis=0)


def _ssm_kernel(*refs, n_seq, has_h0):
    if has_h0:
        (xt_ref, mt_ref, gt_ref, wo_ref, at_ref, wg_ref, bg_ref, h0_ref,
         z_hbm, zs_ref, zb_ref, zb_sem) = refs
    else:
        (xt_ref, mt_ref, gt_ref, wo_ref, at_ref, wg_ref, bg_ref,
         z_hbm, st_ref, zs_ref, zb_ref, zb_sem, fin_ref) = refs
    step = pl.program_id(0)
    n_rows = xt_ref.shape[-1]
    seg = n_rows // n_seq
    cw2 = 2 * SSM_STATE
    assert seg % SCAN_BLOCK == 0
    n_blk = n_rows // SCAN_BLOCK
    blk_per_seq = seg // SCAN_BLOCK
    pos = lax.broadcasted_iota(jnp.int32, (n_rows, cw2), 0) % seg
    row_blk = lax.broadcasted_iota(jnp.int32, (SCAN_BLOCK, cw2), 0)

    def low_half(shape):
        return lax.broadcasted_iota(jnp.int32, shape, 1) < SSM_STATE

    def swap(v):
        return pltpu.roll(v, SSM_STATE, axis=1)

    def to_planes(va, vb):
        lo = low_half((va.shape[0], cw2))
        va_l = pltpu.roll(va, 3 * SSM_STATE, axis=1)
        vb_r = pltpu.roll(vb, SSM_STATE, axis=1)
        return (jnp.where(lo, va[:, 0:cw2], vb_r[:, 0:cw2]),
                jnp.where(lo, va_l[:, 0:cw2], vb[:, 0:cw2]),
                jnp.where(lo, va[:, cw2:2 * cw2], vb_r[:, cw2:2 * cw2]),
                jnp.where(lo, va_l[:, cw2:2 * cw2], vb[:, cw2:2 * cw2]))

    def from_planes(f_re, f_im, b_re, b_im):
        lo = low_half(f_re.shape)
        va = jnp.concatenate([jnp.where(lo, f_re, swap(f_im)), jnp.where(lo, b_re, swap(b_im))], axis=1)
        vb = jnp.concatenate([jnp.where(lo, swap(f_re), f_im), jnp.where(lo, swap(b_re), b_im)], axis=1)
        return va, vb

    for ga in range(0, GROUP_BLOCK, 2):
        gb = ga + 1
        xts = [xt_ref[ga], xt_ref[gb]]
        s_pair = [lax.dot_general(xts[i], gt_ref[g], TN_DIMS, preferred_element_type=F32)
                  for i, g in enumerate((ga, gb))]
        planes = to_planes(*s_pair)
        lo_t = low_half((AT_ROWS, cw2))
        if has_h0:
            h0_planes = to_planes(h0_ref[ga], h0_ref[gb])
        ent_planes = []
        for dr in range(2):
            down = dr == 0
            pa, pb = (at_ref[g][:, (2 * dr) * cw2:(2 * dr + 1) * cw2] for g in (ga, gb))
            qa, qb = (at_ref[g][:, (2 * dr + 1) * cw2:(2 * dr + 2) * cw2] for g in (ga, gb))
            ar_tab = jnp.where(lo_t, pa, pb)
            ai_tab = jnp.where(lo_t, -qa, qb)
            re3 = planes[2 * dr].reshape(n_blk, SCAN_BLOCK, cw2)
            im3 = planes[2 * dr + 1].reshape(n_blk, SCAN_BLOCK, cw2)
            for lvl in range(SCAN_LEVELS):
                m = 2 ** lvl
                valid = (row_blk >= m) if down else (row_blk < SCAN_BLOCK - m)
                row = SCAN_BLOCK + lvl
                ar = jnp.where(valid, ar_tab[row:row + 1], 0.0)
                ai = jnp.where(valid, ai_tab[row:row + 1], 0.0)
                shift = m if down else SCAN_BLOCK - m
                sh_re = pltpu.roll(re3, shift, axis=1)
                sh_im = pltpu.roll(im3, shift, axis=1)
                re3, im3 = re3 + ar * sh_re - ai * sh_im, im3 + ar * sh_im + ai * sh_re
            ar_blk, ai_blk = ar_tab[0:SCAN_BLOCK], ai_tab[0:SCAN_BLOCK]
            blk_re = [re3[i] for i in range(n_blk)]
            blk_im = [im3[i] for i in range(n_blk)]
            edge = slice(SCAN_BLOCK - 1, SCAN_BLOCK) if down else slice(0, 1)
            for q_i in range(n_seq):
                idxs = list(range(q_i * blk_per_seq, (q_i + 1) * blk_per_seq))
                idxs = idxs if down else idxs[::-1]
                for prev, cur in zip([None] + idxs[:-1], idxs):
                    if prev is not None:
                        c_re, c_im = blk_re[prev][edge], blk_im[prev][edge]
                    elif has_h0:
                        c_re = h0_planes[2 * dr][q_i:q_i + 1]
                        c_im = h0_planes[2 * dr + 1][q_i:q_i + 1]
                    else:
                        continue
                    c_re = jnp.broadcast_to(c_re, (SCAN_BLOCK, cw2))
                    c_im = jnp.broadcast_to(c_im, (SCAN_BLOCK, cw2))
                    blk_re[cur] = blk_re[cur] + ar_blk * c_re - ai_blk * c_im
                    blk_im[cur] = blk_im[cur] + ar_blk * c_im + ai_blk * c_re
            for part, blks in ((0, blk_re), (1, blk_im)):
                s = jnp.concatenate(blks, axis=0)
                if not has_h0:
                    fin_ref[part] = s
                ent = _shift_rows(s, 1, down)
                ent = jnp.where((pos >= 1) if down else (pos < seg - 1), ent, 0.0)
                if has_h0:
                    ent_blk = [ent[i * SCAN_BLOCK:(i + 1) * SCAN_BLOCK] for i in range(n_blk)]
                    at_edge = row_blk == (0 if down else SCAN_BLOCK - 1)
                    for q_i in range(n_seq):
                        bi = q_i * blk_per_seq if down else (q_i + 1) * blk_per_seq - 1
                        ent_blk[bi] = jnp.where(
                            at_edge, h0_planes[2 * dr + part][q_i:q_i + 1], ent_blk[bi])
                    ent = jnp.concatenate(ent_blk, axis=0)
                ent_planes.append(ent)
            if not has_h0:
                rows = pl.ds(seg - 1 if down else 0, n_seq, stride=seg)
                f_re, f_im = fin_ref[0, rows, :], fin_ref[1, rows, :]
                lo_s = low_half((n_seq, cw2))
                st_ref[ga, :, dr * cw2:(dr + 1) * cw2] = jnp.where(lo_s, f_re, swap(f_im))
                st_ref[gb, :, dr * cw2:(dr + 1) * cw2] = jnp.where(lo_s, swap(f_re), f_im)
        h_pair = from_planes(*ent_planes)
        for i, g in enumerate((ga, gb)):
            yt = (jnp.dot(mt_ref[g], xts[i], preferred_element_type=F32)
                  + lax.dot_general(wo_ref[g], h_pair[i].astype(BF16), NT_DIMS,
                                    preferred_element_type=F32))
            z = jax.nn.gelu(yt, approximate=True)
            grp = step * GROUP_BLOCK + g
            for t in range(CHUNK):
                zs_ref[t, pl.ds(pl.multiple_of(grp * SSM_GROUP, SSM_GROUP), SSM_GROUP), :] = (
                    z[t * SSM_GROUP:(t + 1) * SSM_GROUP, :])

    @pl.when(step == pl.num_programs(0) - 1)
    def _():
        def put(t):
            return pltpu.make_async_copy(zb_ref.at[t], z_hbm.at[:, t, :], zb_sem.at[t])

        for t0 in range(0, CHUNK, Z_BATCH):
            for t in range(t0, t0 + Z_BATCH):
                zt = zs_ref[t]
                gate = jnp.dot(wg_ref[...], zt.astype(BF16),
                               preferred_element_type=F32) + bg_ref[...]
                zb_ref[t] = (zt * jax.nn.sigmoid(gate)).T
            for t in range(t0, t0 + Z_BATCH):
                put(t).start()
        for t in range(CHUNK):
            put(t).wait()


def _ssm(xt, mt, gt, wo, at, w_glu_t, b_glu_col, h0, n_seq):
    n_rows = xt.shape[-1]
    has_h0 = h0 is not None
    gb = GROUP_BLOCK
    w_spec = pl.BlockSpec((gb, CW, CW), lambda i: (i, 0, 0))
    in_specs = [pl.BlockSpec((gb, CW, n_rows), lambda i: (i, 0, 0)),
                w_spec, w_spec, w_spec,
                pl.BlockSpec((gb, AT_ROWS, 4 * 2 * SSM_STATE), lambda i: (i, 0, 0)),
                pl.BlockSpec((SSM_WIDTH, SSM_WIDTH), lambda i: (0, 0)),
                pl.BlockSpec((SSM_WIDTH, 1), lambda i: (0, 0))]
    args = [xt, mt, gt, wo, at, w_glu_t, b_glu_col]
    out_specs = [pl.BlockSpec(memory_space=pl.ANY)]
    out_shape = [jax.ShapeDtypeStruct((n_rows, CHUNK, SSM_WIDTH), F32)]
    scratch = [pltpu.VMEM((CHUNK, SSM_WIDTH, n_rows), F32),
               pltpu.VMEM((CHUNK, n_rows, SSM_WIDTH), F32),
               pltpu.SemaphoreType.DMA((CHUNK,))]
    if has_h0:
        in_specs.append(pl.BlockSpec((gb, n_seq, CW), lambda i: (i, 0, 0)))
        args.append(h0)
    else:
        out_specs.append(pl.BlockSpec((gb, n_seq, CW), lambda i: (i, 0, 0)))
        out_shape.append(jax.ShapeDtypeStruct((N_GROUPS, n_seq, CW), F32))
        scratch.append(pltpu.VMEM((2, n_rows, 2 * SSM_STATE), F32))
    return pl.pallas_call(
        functools.partial(_ssm_kernel, n_seq=n_seq, has_h0=has_h0),
        grid=(N_GROUPS // gb,),
        in_specs=in_specs,
        out_specs=out_specs,
        out_shape=out_shape,
        scratch_shapes=scratch,
        compiler_params=_cparams(1),
        name="ssm_scan_glu",
    )(*args)


FF_CHUNK = 256
FFN_TOK = 1024


def _out_ffn_kernel(x_ref, attn_ref, z_ref, mod_ref, g_ref, wo_ref, wfi_ref, wfo_ref,
                    o_ref, act_ref):
    gate1 = mod_ref[:, 2 * D_MODEL:3 * D_MODEL]
    shift2 = mod_ref[:, 3 * D_MODEL:4 * D_MODEL]
    scale2 = mod_ref[:, 4 * D_MODEL:5 * D_MODEL]
    gate2 = mod_ref[:, 5 * D_MODEL:6 * D_MODEL]
    half = FFN_TOK // 2
    gain2 = g_ref[2:3, :] * (1.0 + scale2)

    def pre_ffn(hf):
        r = slice(hf * half, (hf + 1) * half)
        mixer = jnp.concatenate([attn_ref[r, :], z_ref[r, :].astype(BF16)], axis=1)
        mix = jnp.dot(mixer, wo_ref[...], preferred_element_type=F32)
        x1 = x_ref[r, :] + gate1 * _rms(mix, g_ref[1:2, :])
        ms = jnp.mean(x1 * x1, axis=-1, keepdims=True)
        return x1, (x1 * lax.rsqrt(ms + NORM_EPS) * gain2 + shift2).astype(BF16)

    def ffn_in(hf, h, chunks):
        r = slice(hf * half, (hf + 1) * half)
        for c in chunks:
            lo = c * FF_CHUNK
            gt = jnp.dot(h, wfi_ref[:, lo:lo + FF_CHUNK], preferred_element_type=F32)
            up = jnp.dot(h, wfi_ref[:, D_FF + lo:D_FF + lo + FF_CHUNK],
                         preferred_element_type=F32)
            act_ref[r, lo:lo + FF_CHUNK] = (_silu(gt) * up).astype(BF16)

    def ffn_out(hf, x1):
        r = slice(hf * half, (hf + 1) * half)
        f = jnp.dot(act_ref[r, :], wfo_ref[...], preferred_element_type=F32)
        o_ref[r, :] = x1 + gate2 * _rms(f, g_ref[3:4, :])

    n_chunks = D_FF // FF_CHUNK
    x1_a, h_a = pre_ffn(0)
    x1_b, h_b = pre_ffn(1)
    ffn_in(0, h_a, range(n_chunks))
    ffn_out(0, x1_a)
    ffn_in(1, h_b, range(n_chunks))
    ffn_out(1, x1_b)


def _out_ffn(x2d, attn, z, mods, mod_rows, norm_g, w_o, w_ffn_in, w_ffn_out):
    n_tok = x2d.shape[0]
    z2d = z.reshape(n_tok, SSM_WIDTH)
    n_steps = n_tok // FFN_TOK
    mod_row0, n_mod = mod_rows
    steps_per_mod = n_steps // n_mod
    const = lambda i: (0, 0)
    row_spec = lambda w: pl.BlockSpec((FFN_TOK, w), lambda i: (i, 0))
    return pl.pallas_call(
        _out_ffn_kernel,
        grid=(n_steps,),
        in_specs=[row_spec(D_MODEL), row_spec(ATT_WIDTH), row_spec(SSM_WIDTH),
                  pl.BlockSpec((None, 1, N_MOD * D_MODEL),
                               lambda i: (mod_row0 + i // steps_per_mod, 0, 0)),
                  pl.BlockSpec((4, D_MODEL), const),
                  pl.BlockSpec((2 * ATT_WIDTH, D_MODEL), const, pipeline_mode=pl.Buffered(1)),
                  pl.BlockSpec((D_MODEL, 2 * D_FF), const, pipeline_mode=pl.Buffered(1)),
                  pl.BlockSpec((D_FF, D_MODEL), const, pipeline_mode=pl.Buffered(1))],
        out_specs=row_spec(D_MODEL),
        out_shape=jax.ShapeDtypeStruct((n_tok, D_MODEL), F32),
        scratch_shapes=[pltpu.VMEM((FFN_TOK, D_FF), BF16)],
        compiler_params=_cparams(1),
        name="out_proj_ffn",
    )(x2d, attn, z2d, mods, norm_g, w_o, w_ffn_in, w_ffn_out)


def _rope_tables(seq_len):
    t = np.arange(seq_len)
    row = (t // GRID_W).astype(np.float32)
    col = (t % GRID_W).astype(np.float32)
    half = HEAD_DIM // 2
    inv_freq = (np.float32(ROPE_BASE)
                ** (-np.arange(0, half, 2, dtype=np.float32) / np.float32(half))).astype(np.float32)
    ang_r = row[:, None] * inv_freq
    ang_c = col[:, None] * inv_freq
    ang = np.concatenate([ang_r, ang_r, ang_c, ang_c], axis=-1)
    cos, sin = np.cos(ang), np.sin(ang)
    upper = (np.arange(HEAD_DIM) % 32) < 16
    sa = np.where(upper, -sin, 0.0)
    sb = np.where(upper, 0.0, sin)
    two = lambda a: jnp.asarray(np.concatenate([a, a], axis=-1), dtype=F32)
    return two(cos), two(sa), two(sb)


def _layer(x, mods, mod_rows, lam_init, rope_tabs, ctx_k, ctx_v, h0, weights, prep):
    n_batch, seq_len = x.shape[:2]
    g = weights['norm_g']
    outs = _in_proj(x, mods, mod_rows, g[0:1], weights['w_in'], rope_tabs)
    q, k, v = outs[:3]
    attn = _attention(q, k, v, ctx_k, ctx_v, weights['lam'], weights['subln_g'],
                      n_batch, seq_len, lam_init)
    ssm_out = _ssm(outs[-1], *prep, weights['w_glu_t'], weights['b_glu_col'], h0, n_batch)
    y = _out_ffn(x.reshape(n_batch * seq_len, D_MODEL), attn, ssm_out[0], mods, mod_rows, g,
                 *weights['late_bf16'])
    return y.reshape(x.shape), outs[3:-1], ssm_out[1:]


def kernel(x_prompt, x_sample, cache_k, cache_v, state_ssm_re, state_ssm_im, c, c_ctx, w_mod, b_mod, norm_g, w_in, lam_params, subln_g, ssm_lambda_re, ssm_lambda_im, ssm_log_step, ssm_b_re, ssm_b_im, ssm_c_re, ssm_c_im, ssm_d, w_glu, b_glu, w_o, w_ffn_in, w_ffn_out):
    depth = w_mod.shape[0]
    assert depth == 1
    bp = x_prompt.shape[0]
    bd, ld_len = x_sample.shape[:2]
    past = cache_k.shape[2]
    xp, xs = x_prompt, x_sample
    rope_tabs = _rope_tables(ld_len)
    ks_out, vs_out, hr_out, hi_out = [], [], [], []
    for l in range(depth):
        lam_init = 0.8 - 0.6 * math.exp(-0.3 * l)
        prep, mods, (w_in_bf16, *late_bf16) = _ssm_prep_and_modulation(
            ssm_lambda_re[l], ssm_lambda_im[l], ssm_log_step[l], ssm_b_re[l], ssm_b_im[l],
            ssm_c_re[l], ssm_c_im[l], ssm_d[l], c_ctx, c, w_mod[l], b_mod[l],
            (w_in[l], w_o[l], w_ffn_in[l], w_ffn_out[l]))
        weights = {
            'norm_g': norm_g[l],
            'w_in': w_in_bf16,
            'lam': lam_params[l], 'subln_g': subln_g[l],
            'w_glu_t': w_glu[l].T.astype(BF16), 'b_glu_col': b_glu[l].reshape(SSM_WIDTH, 1),
            'late_bf16': late_bf16,
        }
        ck = jnp.transpose(cache_k[:, l], (0, 2, 3, 1)).reshape(bd, ATT_WIDTH, past)
        cv = cache_v[:, l]
        h0 = jnp.stack([state_ssm_re[:, l], state_ssm_im[:, l]], axis=2)
        h0 = h0.transpose(3, 0, 1, 2, 4).reshape(N_GROUPS, bd, CW)
        xs, _, _ = _layer(xs, mods, (1, bd), lam_init, rope_tabs, ck, cv, h0, weights, prep)
        xp, (k_ctx, v_ctx), (st,) = _layer(xp, mods, (0, 1), lam_init, None, None, None, None,
                                           weights, prep)
        ks_out.append(jnp.swapaxes(k_ctx, 1, 2).reshape(bp, -1, 2 * N_HEADS, HEAD_DIM))
        vs_out.append(v_ctx)
        fin = st.reshape(N_GROUPS, bp, 2, 2, SSM_STATE).transpose(1, 2, 3, 0, 4)
        hr_out.append(fin[:, :, 0])
        hi_out.append(fin[:, :, 1])
    return (xp, xs, jnp.stack(ks_out, axis=1), jnp.stack(vs_out, axis=1),
            jnp.stack(hr_out, axis=1), jnp.stack(hi_out, axis=1))
```

```python
import functools
import math

import jax
import jax.numpy as jnp
import numpy as np
from jax import lax
from jax.experimental import pallas as pl
from jax.experimental.pallas import tpu as pltpu

F32 = jnp.float32
BF16 = jnp.bfloat16

D_MODEL = 1024
GRID_W = 64
ATT_WIDTH = 512
SSM_WIDTH = 512
HEAD_DIM = 64
N_HEADS = 4
HEAD_W = 2 * HEAD_DIM
SSM_GROUP = 16
N_GROUPS = 32
SSM_STATE = 64
D_FF = 2816
N_MOD = 6
ROPE_BASE = 10000.0
NORM_EPS = 1e-6

CHUNK = 16
CW = CHUNK * SSM_GROUP
SCAN_BLOCK = 8
SCAN_LEVELS = 3
AT_ROWS = 16
GROUP_BLOCK = 4
Z_BATCH = 8

ROWS_PER_TILE = 128
TOK_PER_TILE = ROWS_PER_TILE * CHUNK
SUB_TOK = 512
TQ_ITEM = 256
N_SUB = TOK_PER_TILE // SUB_TOK

VMEM_LIMIT = 56 * 1024 * 1024

NT_DIMS = (((1,), (1,)), ((), ()))
TN_DIMS = (((0,), (0,)), ((), ()))


def _cparams(n_axes):
    return pltpu.CompilerParams(
        dimension_semantics=("arbitrary",) * n_axes,
        vmem_limit_bytes=VMEM_LIMIT)


def _rms(x, g):
    ms = jnp.mean(x * x, axis=-1, keepdims=True)
    return x * lax.rsqrt(ms + NORM_EPS) * g


def _silu(x):
    return x * jax.nn.sigmoid(x)


MOD_ROWS = 8


def _mod_kernel(ctx_ref, c_ref, w_ref, b_ref, o_ref):
    n_lat, tk = c_ref.shape
    row = lax.broadcasted_iota(jnp.int32, (MOD_ROWS, tk), 0)
    cond = jnp.where(row == 0, ctx_ref[...], 0.0)
    for b in range(n_lat):
        cond = jnp.where(row == 1 + b, c_ref[b:b + 1, :], cond)
    part = jnp.dot(_silu(cond).astype(BF16), w_ref[...].astype(BF16),
                   preferred_element_type=F32)

    @pl.when(pl.program_id(0) == 0)
    def _():
        o_ref[:, 0, :] = part + b_ref[...]

    @pl.when(pl.program_id(0) > 0)
    def _():
        o_ref[:, 0, :] += part


def _rope(x, cos, sa, sb):
    return (x * cos + pltpu.roll(x, HEAD_W - 16, axis=1) * sa
            + pltpu.roll(x, 16, axis=1) * sb)


def _in_proj_kernel(*refs, rope, seqs_per_sub):
    x_ref, x3_hbm, mod_ref, g_ref, w_ref = refs[:5]
    refs = refs[5:]
    if rope:
        cos_ref, sa_ref, sb_ref = refs[:3]
        refs = refs[3:]
        q_ref, k_ref, v_ref, ut_ref, wut_ref, xt_ref, xt_sem = refs
    else:
        q_ref, k_ref, v_ref, kc_ref, vc_ref, ut_ref, wut_ref, xt_ref, xt_sem = refs
    tile = pl.program_id(0)
    j = pl.program_id(1)
    t_per_sub = CHUNK // N_SUB
    t_early = CHUNK - t_per_sub

    def gather(tile_idx, t):
        src = x3_hbm.at[pl.ds(tile_idx * ROWS_PER_TILE, ROWS_PER_TILE), t, :]
        return pltpu.make_async_copy(src, xt_ref.at[t], xt_sem.at[t])

    @pl.when(j == 0)
    def _():
        @pl.when(tile == 0)
        def _():
            for t in range(t_early):
                gather(0, t).start()
        for t in range(t_early, CHUNK):
            gather(tile, t).start()

    t_base = j * t_per_sub
    for d in range(t_per_sub):
        gather(tile, t_base + d).wait()

    @pl.when((j == N_SUB - 1) & (tile + 1 < pl.num_programs(0)))
    def _():
        for t in range(t_early):
            gather(tile + 1, t).start()

    @pl.when((tile == 0) & (j == 0))
    def _():
        wut_ref[...] = w_ref[:, 3 * ATT_WIDTH:].T

    shift = mod_ref[:, 0:D_MODEL]
    gain = g_ref[...] * (1.0 + mod_ref[:, D_MODEL:2 * D_MODEL])

    def norm_mod(xv):
        ms = jnp.mean(xv * xv, axis=-1, keepdims=True)
        return (xv * lax.rsqrt(ms + NORM_EPS) * gain + shift).astype(BF16)

    def ssm_input(d0):
        xt = jnp.concatenate([xt_ref[t_base + d0], xt_ref[t_base + d0 + 1]], axis=0)
        ut = lax.dot_general(wut_ref[...], norm_mod(xt), NT_DIMS,
                             preferred_element_type=F32)
        for d in range(2):
            blk = ut[:, d * ROWS_PER_TILE:(d + 1) * ROWS_PER_TILE]
            row0 = pl.multiple_of((t_base + d0 + d) * SSM_GROUP, SSM_GROUP)
            ut_ref[:, pl.ds(row0, SSM_GROUP), :] = (
                blk.reshape(N_GROUPS, SSM_GROUP, ROWS_PER_TILE).astype(ut_ref.dtype))

    proj = jnp.dot(norm_mod(x_ref[...]), w_ref[:, 0:3 * ATT_WIDTH],
                   preferred_element_type=F32)
    q = proj[:, 0:ATT_WIDTH]
    k = proj[:, ATT_WIDTH:2 * ATT_WIDTH]
    v = proj[:, 2 * ATT_WIDTH:3 * ATT_WIDTH]
    qscale = HEAD_DIM ** -0.5 * math.log2(math.e)
    if rope:
        cos, sa, sb = cos_ref[...], sa_ref[...], sb_ref[...]
        for hd in range(N_HEADS):
            sl = slice(hd * HEAD_W, (hd + 1) * HEAD_W)
            q_ref[:, sl] = (_rope(q[:, sl], cos, sa, sb) * qscale).astype(q_ref.dtype)
            k_ref[:, sl] = _rope(k[:, sl], cos, sa, sb).astype(k_ref.dtype)
    else:
        q_ref[...] = (q * qscale).astype(q_ref.dtype)
        k_ref[...] = k.astype(k_ref.dtype)
        seq = SUB_TOK // seqs_per_sub
        k_t = k.T
        for b in range(seqs_per_sub):
            kc_ref[b] = k_t[:, b * seq:(b + 1) * seq]
            for hd in range(N_HEADS):
                vc_ref[b, :, hd, :] = v[b * seq:(b + 1) * seq, hd * HEAD_W:(hd + 1) * HEAD_W]
    v_ref[...] = v.astype(v_ref.dtype)

    for d0 in range(0, t_per_sub, 2):
        ssm_input(d0)


def _in_proj(x, mods, mod_rows, g0, w_in, rope_tabs):
    n_batch, seq_len = x.shape[:2]
    n_tok = n_batch * seq_len
    n_rows = n_tok // CHUNK
    n_tiles = n_tok // TOK_PER_TILE
    mod_row0, n_mod = mod_rows
    tiles_per_mod = n_tiles // n_mod
    rope = rope_tabs is not None
    seqs_per_sub = max(1, SUB_TOK // seq_len)
    in_specs = [pl.BlockSpec((SUB_TOK, D_MODEL), lambda i, j: (i * N_SUB + j, 0)),
                pl.BlockSpec(memory_space=pl.ANY),
                pl.BlockSpec((None, 1, 2 * D_MODEL),
                             lambda i, j: (mod_row0 + i // tiles_per_mod, 0, 0)),
                pl.BlockSpec((1, D_MODEL), lambda i, j: (0, 0)),
                pl.BlockSpec((D_MODEL, 4 * ATT_WIDTH), lambda i, j: (0, 0))]
    args = [x.reshape(n_tok, D_MODEL), x.reshape(n_rows, CHUNK, D_MODEL), mods, g0, w_in]
    row_spec = pl.BlockSpec((SUB_TOK, ATT_WIDTH), lambda i, j: (i * N_SUB + j, 0))
    row_shape = jax.ShapeDtypeStruct((n_tok, ATT_WIDTH), BF16)
    out_specs = [row_spec, row_spec, row_spec]
    out_shape = [row_shape, row_shape, row_shape]
    if rope:
        assert seq_len == TOK_PER_TILE
        for tab in rope_tabs:
            in_specs.append(pl.BlockSpec((SUB_TOK, HEAD_W), lambda i, j: (j, 0)))
            args.append(tab)
    else:
        out_specs += [pl.BlockSpec((seqs_per_sub, ATT_WIDTH, seq_len),
                                   lambda i, j: (i * N_SUB + j, 0, 0)),
                      pl.BlockSpec((seqs_per_sub, seq_len, N_HEADS, HEAD_W),
                                   lambda i, j: (i * N_SUB + j, 0, 0, 0))]
        out_shape += [jax.ShapeDtypeStruct((n_batch, ATT_WIDTH, seq_len), F32),
                      jax.ShapeDtypeStruct((n_batch, seq_len, N_HEADS, HEAD_W), F32)]
    out_specs.append(pl.BlockSpec((N_GROUPS, CW, ROWS_PER_TILE), lambda i, j: (0, 0, i)))
    out_shape.append(jax.ShapeDtypeStruct((N_GROUPS, CW, n_rows), BF16))
    return pl.pallas_call(
        functools.partial(_in_proj_kernel, rope=rope, seqs_per_sub=seqs_per_sub),
        grid=(n_tiles, N_SUB),
        in_specs=in_specs,
        out_specs=out_specs,
        out_shape=out_shape,
        scratch_shapes=[pltpu.VMEM((SSM_WIDTH, D_MODEL), BF16),
                        pltpu.VMEM((CHUNK, ROWS_PER_TILE, D_MODEL), F32),
                        pltpu.SemaphoreType.DMA((CHUNK,))],
        compiler_params=_cparams(2),
        name="in_proj",
    )(*args)


def _attn_kernel(*refs, has_ctx, lam_init, n_seq, seq_len, tq):
    if has_ctx:
        lam_ref, sg_ref, q_ref, ck_ref, cv_ref, k_ref, v_ref, o_ref = refs
    else:
        lam_ref, sg_ref, q_ref, k_ref, v_ref, o_ref = refs
    lp = lam_ref[...]
    lam = (jnp.exp(jnp.sum(lp[0:1] * lp[1:2], axis=-1, keepdims=True))
           - jnp.exp(jnp.sum(lp[2:3] * lp[3:4], axis=-1, keepdims=True)) + lam_init)
    first_map = lax.broadcasted_iota(jnp.int32, (1, HEAD_W), 1) < HEAD_DIM
    ti = min(TQ_ITEM, tq)
    for row0 in range(0, n_seq * tq, ti):
        b = row0 // tq
        q_rows = slice(row0, row0 + ti)
        for hd in range(N_HEADS):
            sl = slice(hd * HEAD_W, (hd + 1) * HEAD_W)
            qh = q_ref[q_rows, sl]
            zero = jnp.zeros_like(qh)
            qs = jnp.concatenate([jnp.where(first_map, qh, zero),
                                  jnp.where(first_map, zero, qh)], axis=0)
            kv_rows = slice(b * seq_len, (b + 1) * seq_len)
            parts = [(k_ref[kv_rows, sl], v_ref[kv_rows, sl])]
            scores = [lax.dot_general(qs, parts[0][0], NT_DIMS, preferred_element_type=F32)]
            if has_ctx:
                parts.insert(0, (None, cv_ref[:, hd, :].astype(BF16)))
                scores.insert(0, jnp.dot(qs, ck_ref[sl, :].astype(BF16),
                                         preferred_element_type=F32))
            mx = scores[0].max(axis=-1, keepdims=True)
            for s in scores[1:]:
                mx = jnp.maximum(mx, s.max(axis=-1, keepdims=True))
            acc = None
            for s, (_, vv) in zip(scores, parts):
                e = jnp.exp2(s - mx).astype(BF16)
                v_one = jnp.concatenate([vv, jnp.ones_like(vv)], axis=1)
                pv = jnp.dot(e, v_one, preferred_element_type=F32)
                acc = pv if acc is None else acc + pv
            num = acc[:, 0:HEAD_W] / acc[:, HEAD_W:2 * HEAD_W]
            o = num[0:ti] - lam * num[ti:2 * ti]
            o = _rms(o, sg_ref[...]) * (1.0 - lam_init)
            o_ref[q_rows, sl] = o.astype(o_ref.dtype)


def _attention(q, k, v, ctx_k, ctx_v, lam_params, subln_g, n_batch, seq_len, lam_init):
    has_ctx = ctx_k is not None
    tq = min(1024, seq_len)
    n_q = seq_len // tq
    n_seq = 1 if n_q > 1 else min(4, n_batch)
    in_specs = [pl.BlockSpec((4, HEAD_DIM), lambda b, i: (0, 0)),
                pl.BlockSpec((1, HEAD_W), lambda b, i: (0, 0)),
                pl.BlockSpec((n_seq * tq, ATT_WIDTH), lambda b, i: (b * n_q + i, 0))]
    args = [lam_params, subln_g.reshape(1, HEAD_W), q]
    if has_ctx:
        past = ctx_v.shape[1]
        in_specs +=[pl.BlockSpec((None, ATT_WIDTH, past), lambda b, i: (b, 0, 0)),
                     pl.BlockSpec((None, past, N_HEADS, HEAD_W), lambda b, i: (b, 0, 0, 0))]
        args += [ctx_k, ctx_v]
    kv_spec = pl.BlockSpec((n_seq * seq_len, ATT_WIDTH), lambda b, i: (b, 0))
    in_specs += [kv_spec, kv_spec]
    args += [k, v]
    return pl.pallas_call(
        functools.partial(_attn_kernel, has_ctx=has_ctx, lam_init=lam_init,
                          n_seq=n_seq, seq_len=seq_len, tq=tq),
        grid=(n_batch // n_seq, n_q),
        in_specs=in_specs,
        out_specs=pl.BlockSpec((n_seq * tq, ATT_WIDTH), lambda b, i: (b * n_q + i, 0)),
        out_shape=jax.ShapeDtypeStruct((n_batch * seq_len, ATT_WIDTH), BF16),
        compiler_params=_cparams(2),
        name="diff_attention",
    )(*args)


def _cmul(ar, ai, br, bi):
    return ar * br - ai * bi, ar * bi + ai * br


def _ssm_prep_kernel(lre_ref, lim_ref, ls_ref, bre_ref, bim_ref, cre_ref, cim_ref, d_ref,
                     mt_ref, gt_ref, wo_ref, at_ref):
    lane = lax.broadcasted_iota(jnp.int32, (SSM_GROUP, CW), 1)
    chan = lax.broadcasted_iota(jnp.int32, (SSM_GROUP, CW), 0)
    for gi in range(GROUP_BLOCK):
        gt_cols, wo_cols, at_cols, toeplitz = [], [], [], []
        for dr in range(2):
            lr = jnp.minimum(lre_ref[dr, gi], -1e-4)
            li = lim_ref[dr, gi]
            step = jnp.exp(ls_ref[dr, gi])
            mag = jnp.exp(lr * step)
            a_re = mag * jnp.cos(li * step)
            a_im = mag * jnp.sin(li * step)
            den = lr * lr + li * li
            nr = a_re - 1.0
            f_re = (nr * lr + a_im * li) / den
            f_im = (a_im * lr - nr * li) / den
            bt_re, bt_im = bre_ref[dr, gi], bim_ref[dr, gi]
            bb_re, bb_im = _cmul(f_re, f_im, bt_re, bt_im)
            c_re, c_im = cre_ref[dr, gi], cim_ref[dr, gi]
            pw = [(jnp.ones_like(a_re), jnp.zeros_like(a_im))]
            for _ in range(CHUNK):
                pw.append(_cmul(pw[-1][0], pw[-1][1], a_re, a_im))
            g_re, g_im, e_re, e_im = [], [], [], []
            for t in range(CHUNK):
                pr, pi = pw[CHUNK - 1 - t] if dr == 0 else pw[t]
                r, i = _cmul(bb_re, bb_im, pr, pi)
                g_re.append(r)
                g_im.append(i)
                pr, pi = pw[t + 1] if dr == 0 else pw[CHUNK - t]
                r, i = _cmul(c_re, c_im, pr, pi)
                e_re.append(r)
                e_im.append(-i)
            g_cat = jnp.concatenate([jnp.concatenate(g_re, axis=0),
                                     jnp.concatenate(g_im, axis=0)], axis=1)
            gt_cols.append(g_cat)
            wo_cols.append(jnp.concatenate([jnp.concatenate(e_re, axis=0),
                                            jnp.concatenate(e_im, axis=0)], axis=1))
            c_cat = jnp.concatenate([c_re, -c_im], axis=1)
            toeplitz.append(lax.dot_general(c_cat, g_cat, NT_DIMS,
                                            precision=lax.Precision.HIGHEST,
                                            preferred_element_type=F32))
            apw = [pw[CHUNK]]
            for _ in range(SCAN_BLOCK - 1):
                apw.append(_cmul(apw[-1][0], apw[-1][1], apw[0][0], apw[0][1]))
            order = list(range(SCAN_BLOCK)) if dr == 0 else list(range(SCAN_BLOCK - 1, -1, -1))
            order += [2 ** l - 1 for l in range(SCAN_LEVELS)]
            order += [0] * (AT_ROWS - len(order))
            at_cols += [jnp.concatenate([jnp.concatenate([apw[i][0], apw[i][0]], axis=1)
                                         for i in order], axis=0),
                        jnp.concatenate([jnp.concatenate([-apw[i][1], apw[i][1]], axis=1)
                                         for i in order], axis=0)]
        kf_rev, kb = toeplitz
        d_skip = d_ref[gi]
        blocks = []
        for t in range(CHUNK):
            fwd = pltpu.roll(kf_rev, (CW - (CHUNK - 1 - t) * SSM_GROUP) % CW, axis=1)
            bwd = pltpu.roll(kb, t * SSM_GROUP, axis=1)
            blocks.append(jnp.where(lane < (t + 1) * SSM_GROUP, fwd, 0.0)
                          + jnp.where(lane >= t * SSM_GROUP, bwd, 0.0)
                          + jnp.where(lane == chan + t * SSM_GROUP, d_skip, 0.0))
        mt_ref[gi] = jnp.concatenate(blocks, axis=0).astype(mt_ref.dtype)
        gt_ref[gi] = jnp.concatenate(gt_cols, axis=1).astype(gt_ref.dtype)
        wo_ref[gi] = jnp.concatenate(wo_cols, axis=1).astype(wo_ref.dtype)
        at_ref[gi] = jnp.concatenate(at_cols, axis=1)


N_PREP_IN, N_PREP_OUT, N_MOD_IN = 8, 4, 4


def _prep_mod_kernel(*refs, n_cast):
    prep_in = refs[:N_PREP_IN]
    mod_in = refs[N_PREP_IN:N_PREP_IN + N_MOD_IN]
    cast_in = refs[N_PREP_IN + N_MOD_IN:N_PREP_IN + N_MOD_IN + n_cast]
    outs = refs[N_PREP_IN + N_MOD_IN + n_cast:]
    _ssm_prep_kernel(*prep_in, *outs[:N_PREP_OUT])
    _mod_kernel(*mod_in, outs[N_PREP_OUT])
    for src, dst in zip(cast_in, outs[N_PREP_OUT + 1:]):
        dst[...] = src[...].astype(dst.dtype)


def _ssm_prep_and_modulation(lam_re, lam_im, log_step, b_re, b_im, c_re, c_im, d_skip,
                             c_ctx, c, w_mod, b_mod, cast_weights):
    row = lambda a: a.reshape(2, N_GROUPS, 1, SSM_STATE)
    bt = lambda a: jnp.swapaxes(a, 2, 3)
    d_row = jnp.tile((d_skip[0] + d_skip[1]).reshape(N_GROUPS, 1, SSM_GROUP), (1, 1, CHUNK))
    gb = GROUP_BLOCK
    n_steps = N_GROUPS // gb
    vec_spec = pl.BlockSpec((2, gb, 1, SSM_STATE), lambda i: (0, i, 0, 0))
    mat_spec = pl.BlockSpec((2, gb, SSM_GROUP, SSM_STATE), lambda i: (0, i, 0, 0))
    w_spec = pl.BlockSpec((gb, CW, CW), lambda i: (i, 0, 0))
    w_shape = jax.ShapeDtypeStruct((N_GROUPS, CW, CW), BF16)
    n_mod = w_mod.shape[1]
    tk = D_MODEL // n_steps
    assert 1 + c.shape[0] <= MOD_ROWS
    cast_specs = [pl.BlockSpec((w.shape[0] // n_steps, w.shape[1]), lambda k: (k, 0))
                  for w in cast_weights]
    outs = pl.pallas_call(
        functools.partial(_prep_mod_kernel, n_cast=len(cast_weights)),
        grid=(n_steps,),
        in_specs=[vec_spec, vec_spec,
                  pl.BlockSpec((2, gb, 1, 1), lambda i: (0, i, 0, 0)),
                  mat_spec, mat_spec, mat_spec, mat_spec,
                  pl.BlockSpec((gb, 1, CW), lambda i: (i, 0, 0)),
                  pl.BlockSpec((1, tk), lambda k: (0, k)),
                  pl.BlockSpec((c.shape[0], tk), lambda k: (0, k)),
                  pl.BlockSpec((tk, n_mod), lambda k: (k, 0)),
                  pl.BlockSpec((1, n_mod), lambda k: (0, 0))] + cast_specs,
        out_specs=[w_spec, w_spec, w_spec,
                   pl.BlockSpec((gb, AT_ROWS, 4 * 2 * SSM_STATE), lambda i: (i, 0, 0)),
                   pl.BlockSpec((MOD_ROWS, 1, n_mod), lambda k: (0, 0, 0))] + cast_specs,
        out_shape=[w_shape, w_shape, w_shape,
                   jax.ShapeDtypeStruct((N_GROUPS, AT_ROWS, 4 * 2 * SSM_STATE), F32),
                   jax.ShapeDtypeStruct((MOD_ROWS, 1, n_mod), F32)]
        + [jax.ShapeDtypeStruct(w.shape, BF16) for w in cast_weights],
        compiler_params=_cparams(1),
        name="ssm_prep_modulation",
    )(row(lam_re), row(lam_im), log_step.reshape(2, N_GROUPS, 1, 1),
      bt(b_re), bt(b_im), c_re, c_im, d_row,
      c_ctx.reshape(1, D_MODEL), c, w_mod, b_mod.reshape(1, n_mod), *cast_weights)
    return outs[:N_PREP_OUT], outs[N_PREP_OUT], outs[N_PREP_OUT + 1:]


def _shift_rows(x, m, down):
    n = x.shape[0]
    return pltpu.roll(x, m if down else n - m, axis=0)


def _ssm_kernel(*refs, n_seq, has_h0, n_cast):
    n_in = 7 + (1 if has_h0 else 0)
    n_out = 1 + (0 if has_h0 else 1)
    if n_cast:
        cast_in = refs[n_in:n_in + n_cast]
        cast_out = refs[n_in + n_cast + n_out:n_in + 2 * n_cast + n_out]
        refs = (refs[:n_in] + refs[n_in + n_cast:n_in + n_cast + n_out]
                + refs[n_in + 2 * n_cast + n_out:])
        for src, dst in zip(cast_in, cast_out):
            dst[...] = src[...].astype(dst.dtype)
    if has_h0:
        (xt_ref, mt_ref, gt_ref, wo_ref, at_ref, wg_ref, bg_ref, h0_ref,
         z_hbm, zs_ref, zb_ref, zb_sem) = refs
    else:
        (xt_ref, mt_ref, gt_ref, wo_ref, at_ref, wg_ref, bg_ref,
         z_hbm, st_ref, zs_ref, zb_ref, zb_sem, fin_ref) = refs
    step = pl.program_id(0)
    n_rows = xt_ref.shape[-1]
    seg = n_rows // n_seq
    cw2 = 2 * SSM_STATE
    assert seg % SCAN_BLOCK == 0
    n_blk = n_rows // SCAN_BLOCK
    blk_per_seq = seg // SCAN_BLOCK
    pos = lax.broadcasted_iota(jnp.int32, (n_rows, cw2), 0) % seg
    row_blk = lax.broadcasted_iota(jnp.int32, (SCAN_BLOCK, cw2), 0)

    def low_half(shape):
        return lax.broadcasted_iota(jnp.int32, shape, 1) < SSM_STATE

    def swap(v):
        return pltpu.roll(v, SSM_STATE, axis=1)

    def to_planes(va, vb):
        lo = low_half((va.shape[0], cw2))
        va_l = pltpu.roll(va, 3 * SSM_STATE, axis=1)
        vb_r = pltpu.roll(vb, SSM_STATE, axis=1)
        return (jnp.where(lo, va[:, 0:cw2], vb_r[:, 0:cw2]),
                jnp.where(lo, va_l[:, 0:cw2], vb[:, 0:cw2]),
                jnp.where(lo, va[:, cw2:2 * cw2], vb_r[:, cw2:2 * cw2]),
                jnp.where(lo, va_l[:, cw2:2 * cw2], vb[:, cw2:2 * cw2]))

    def from_planes(f_re, f_im, b_re, b_im):
        lo = low_half(f_re.shape)
        va = jnp.concatenate([jnp.where(lo, f_re, swap(f_im)), jnp.where(lo, b_re, swap(b_im))], axis=1)
        vb = jnp.concatenate([jnp.where(lo, swap(f_re), f_im), jnp.where(lo, swap(b_re), b_im)], axis=1)
        return va, vb

    for ga in range(0, GROUP_BLOCK, 2):
        gb = ga + 1
        xts = [xt_ref[ga], xt_ref[gb]]
        s_pair = [lax.dot_general(xts[i], gt_ref[g], TN_DIMS, preferred_element_type=F32)
                  for i, g in enumerate((ga, gb))]
        planes = to_planes(*s_pair)
        lo_t = low_half((AT_ROWS, cw2))
        if has_h0:
            h0_planes = to_planes(h0_ref[ga], h0_ref[gb])
        ent_planes = []
        for dr in range(2):
            down = dr == 0
            pa, pb = (at_ref[g][:, (2 * dr) * cw2:(2 * dr + 1) * cw2] for g in (ga, gb))
            qa, qb = (at_ref[g][:, (2 * dr + 1) * cw2:(2 * dr + 2) * cw2] for g in (ga, gb))
            ar_tab = jnp.where(lo_t, pa, pb)
            ai_tab = jnp.where(lo_t, -qa, qb)
            re3 = planes[2 * dr].reshape(n_blk, SCAN_BLOCK, cw2)
            im3 = planes[2 * dr + 1].reshape(n_blk, SCAN_BLOCK, cw2)
            for lvl in range(SCAN_LEVELS):
                m = 2 ** lvl
                valid = (row_blk >= m) if down else (row_blk < SCAN_BLOCK - m)
                row = SCAN_BLOCK + lvl
                ar = jnp.where(valid, ar_tab[row:row + 1], 0.0)
                ai = jnp.where(valid, ai_tab[row:row + 1], 0.0)
                shift = m if down else SCAN_BLOCK - m
                sh_re = pltpu.roll(re3, shift, axis=1)
                sh_im = pltpu.roll(im3, shift, axis=1)
                re3, im3 = re3 + ar * sh_re - ai * sh_im, im3 + ar * sh_im + ai * sh_re
            ar_blk, ai_blk = ar_tab[0:SCAN_BLOCK], ai_tab[0:SCAN_BLOCK]
            blk_re = [re3[i] for i in range(n_blk)]
            blk_im = [im3[i] for i in range(n_blk)]
            edge = slice(SCAN_BLOCK - 1, SCAN_BLOCK) if down else slice(0, 1)
            for q_i in range(n_seq):
                idxs = list(range(q_i * blk_per_seq, (q_i + 1) * blk_per_seq))
                idxs = idxs if down else idxs[::-1]
                for prev, cur in zip([None] + idxs[:-1], idxs):
                    if prev is not None:
                        c_re, c_im = blk_re[prev][edge], blk_im[prev][edge]
                    elif has_h0:
                        c_re = h0_planes[2 * dr][q_i:q_i + 1]
                        c_im = h0_planes[2 * dr + 1][q_i:q_i + 1]
                    else:
                        continue
                    c_re = jnp.broadcast_to(c_re, (SCAN_BLOCK, cw2))
                    c_im = jnp.broadcast_to(c_im, (SCAN_BLOCK, cw2))
                    blk_re[cur] = blk_re[cur] + ar_blk * c_re - ai_blk * c_im
                    blk_im[cur] = blk_im[cur] + ar_blk * c_im + ai_blk * c_re
            for part, blks in ((0, blk_re), (1, blk_im)):
                s = jnp.concatenate(blks, axis=0)
                if not has_h0:
                    fin_ref[part] = s
                ent = _shift_rows(s, 1, down)
                ent = jnp.where((pos >= 1) if down else (pos < seg - 1), ent, 0.0)
                if has_h0:
                    ent_blk = [ent[i * SCAN_BLOCK:(i + 1) * SCAN_BLOCK] for i in range(n_blk)]
                    at_edge = row_blk == (0 if down else SCAN_BLOCK - 1)
                    for q_i in range(n_seq):
                        bi = q_i * blk_per_seq if down else (q_i + 1) * blk_per_seq - 1
                        ent_blk[bi] = jnp.where(
                            at_edge, h0_planes[2 * dr + part][q_i:q_i + 1], ent_blk[bi])
                    ent = jnp.concatenate(ent_blk, axis=0)
                ent_planes.append(ent)
            if not has_h0:
                rows = pl.ds(seg - 1 if down else 0, n_seq, stride=seg)
                f_re, f_im = fin_ref[0, rows, :], fin_ref[1, rows, :]
                lo_s = low_half((n_seq, cw2))
                st_ref[ga, :, dr * cw2:(dr + 1) * cw2] = jnp.where(lo_s, f_re, swap(f_im))
                st_ref[gb, :, dr * cw2:(dr + 1) * cw2] = jnp.where(lo_s, swap(f_re), f_im)
        h_pair = from_planes(*ent_planes)
        for i, g in enumerate((ga, gb)):
            yt = (jnp.dot(mt_ref[g], xts[i], preferred_element_type=F32)
                  + lax.dot_general(wo_ref[g], h_pair[i].astype(BF16), NT_DIMS,
                                    preferred_element_type=F32))
            z = jax.nn.gelu(yt, approximate=True)
            grp = step * GROUP_BLOCK + g
            for t in range(CHUNK):
                zs_ref[t, pl.ds(pl.multiple_of(grp * SSM_GROUP, SSM_GROUP), SSM_GROUP), :] = (
                    z[t * SSM_GROUP:(t + 1) * SSM_GROUP, :])

    @pl.when(step == pl.num_programs(0) - 1)
    def _():
        def put(t):
            return pltpu.make_async_copy(zb_ref.at[t], z_hbm.at[:, t, :], zb_sem.at[t])

        for t0 in range(0, CHUNK, Z_BATCH):
            for t in range(t0, t0 + Z_BATCH):
                zt = zs_ref[t]
                gate = jnp.dot(wg_ref[...], zt.astype(BF16),
                               preferred_element_type=F32) + bg_ref[...]
                zb_ref[t] = (zt * jax.nn.sigmoid(gate)).T
            for t in range(t0, t0 + Z_BATCH):
                put(t).start()
        for t in range(CHUNK):
            put(t).wait()


def _ssm(xt, mt, gt, wo, at, w_glu_t, b_glu_col, h0, n_seq, cast_weights=()):
    n_rows = xt.shape[-1]
    has_h0 = h0 is not None
    gb = GROUP_BLOCK
    w_spec = pl.BlockSpec((gb, CW, CW), lambda i: (i, 0, 0))
    in_specs = [pl.BlockSpec((gb, CW, n_rows), lambda i: (i, 0, 0)),
                w_spec, w_spec, w_spec,
                pl.BlockSpec((gb, AT_ROWS, 4 * 2 * SSM_STATE), lambda i: (i, 0, 0)),
                pl.BlockSpec((SSM_WIDTH, SSM_WIDTH), lambda i: (0, 0)),
                pl.BlockSpec((SSM_WIDTH, 1), lambda i: (0, 0))]
    args = [xt, mt, gt, wo, at, w_glu_t, b_glu_col]
    out_specs = [pl.BlockSpec(memory_space=pl.ANY)]
    out_shape = [jax.ShapeDtypeStruct((n_rows, CHUNK, SSM_WIDTH), F32)]
    scratch = [pltpu.VMEM((CHUNK, SSM_WIDTH, n_rows), F32),
               pltpu.VMEM((CHUNK, n_rows, SSM_WIDTH), F32),
               pltpu.SemaphoreType.DMA((CHUNK,))]
    if has_h0:
        in_specs.append(pl.BlockSpec((gb, n_seq, CW), lambda i: (i, 0, 0)))
        args.append(h0)
    else:
        out_specs.append(pl.BlockSpec((gb, n_seq, CW), lambda i: (i, 0, 0)))
        out_shape.append(jax.ShapeDtypeStruct((N_GROUPS, n_seq, CW), F32))
        scratch.append(pltpu.VMEM((2, n_rows, 2 * SSM_STATE), F32))
    for w in cast_weights:
        spec = pl.BlockSpec((w.shape[0] // (N_GROUPS // gb), w.shape[1]), lambda i: (i, 0))
        in_specs.append(spec)
        args.append(w)
        out_specs.append(spec)
        out_shape.append(jax.ShapeDtypeStruct(w.shape, BF16))
    return pl.pallas_call(
        functools.partial(_ssm_kernel, n_seq=n_seq, has_h0=has_h0, n_cast=len(cast_weights)),
        grid=(N_GROUPS // gb,),
        in_specs=in_specs,
        out_specs=out_specs,
        out_shape=out_shape,
        scratch_shapes=scratch,
        compiler_params=_cparams(1),
        name="ssm_scan_glu",
    )(*args)


FF_CHUNK = 256
FFN_TOK = 1024


def _out_ffn_kernel(x_ref, attn_ref, z_ref, mod_ref, g_ref, wo_ref, wfi_ref, wfo_ref,
                    o_ref, act_ref):
    gate1 = mod_ref[:, 2 * D_MODEL:3 * D_MODEL]
    shift2 = mod_ref[:, 3 * D_MODEL:4 * D_MODEL]
    scale2 = mod_ref[:, 4 * D_MODEL:5 * D_MODEL]
    gate2 = mod_ref[:, 5 * D_MODEL:6 * D_MODEL]
    half = FFN_TOK // 2
    gain2 = g_ref[2:3, :] * (1.0 + scale2)

    def pre_ffn(hf):
        r = slice(hf * half, (hf + 1) * half)
        mixer = jnp.concatenate([attn_ref[r, :], z_ref[r, :].astype(BF16)], axis=1)
        mix = jnp.dot(mixer, wo_ref[...], preferred_element_type=F32)
        x1 = x_ref[r, :] + gate1 * _rms(mix, g_ref[1:2, :])
        ms = jnp.mean(x1 * x1, axis=-1, keepdims=True)
        return x1, (x1 * lax.rsqrt(ms + NORM_EPS) * gain2 + shift2).astype(BF16)

    def ffn_in(hf, h, chunks):
        r = slice(hf * half, (hf + 1) * half)
        for c in chunks:
            lo = c * FF_CHUNK
            gt = jnp.dot(h, wfi_ref[:, lo:lo + FF_CHUNK], preferred_element_type=F32)
            up = jnp.dot(h, wfi_ref[:, D_FF + lo:D_FF + lo + FF_CHUNK],
                         preferred_element_type=F32)
            act_ref[r, lo:lo + FF_CHUNK] = (_silu(gt) * up).astype(BF16)

    def ffn_out(hf, x1):
        r = slice(hf * half, (hf + 1) * half)
        f = jnp.dot(act_ref[r, :], wfo_ref[...], preferred_element_type=F32)
        o_ref[r, :] = x1 + gate2 * _rms(f, g_ref[3:4, :])

    n_chunks = D_FF // FF_CHUNK
    x1_a, h_a = pre_ffn(0)
    x1_b, h_b = pre_ffn(1)
    ffn_in(0, h_a, range(n_chunks))
    ffn_out(0, x1_a)
    ffn_in(1, h_b, range(n_chunks))
    ffn_out(1, x1_b)


def _out_ffn(x2d, attn, z, mods, mod_rows, norm_g, w_o, w_ffn_in, w_ffn_out):
    n_tok = x2d.shape[0]
    z2d = z.reshape(n_tok, SSM_WIDTH)
    n_steps = n_tok // FFN_TOK
    mod_row0, n_mod = mod_rows
    steps_per_mod = n_steps // n_mod
    const = lambda i: (0, 0)
    row_spec = lambda w: pl.BlockSpec((FFN_TOK, w), lambda i: (i, 0))
    return pl.pallas_call(
        _out_ffn_kernel,
        grid=(n_steps,),
        in_specs=[row_spec(D_MODEL), row_spec(ATT_WIDTH), row_spec(SSM_WIDTH),
                  pl.BlockSpec((None, 1, N_MOD * D_MODEL),
                               lambda i: (mod_row0 + i // steps_per_mod, 0, 0)),
                  pl.BlockSpec((4, D_MODEL), const),
                  pl.BlockSpec((2 * ATT_WIDTH, D_MODEL), const, pipeline_mode=pl.Buffered(1)),
                  pl.BlockSpec((D_MODEL, 2 * D_FF), const, pipeline_mode=pl.Buffered(1)),
                  pl.BlockSpec((D_FF, D_MODEL), const, pipeline_mode=pl.Buffered(1))],
        out_specs=row_spec(D_MODEL),
        out_shape=jax.ShapeDtypeStruct((n_tok, D_MODEL), F32),
        scratch_shapes=[pltpu.VMEM((FFN_TOK, D_FF), BF16)],
        compiler_params=_cparams(1),
        name="out_proj_ffn",
    )(x2d, attn, z2d, mods, norm_g, w_o, w_ffn_in, w_ffn_out)


def _rope_tables(seq_len):
    t = np.arange(seq_len)
    row = (t // GRID_W).astype(np.float32)
    col = (t % GRID_W).astype(np.float32)
    half = HEAD_DIM // 2
    inv_freq = (np.float32(ROPE_BASE)
                ** (-np.arange(0, half, 2, dtype=np.float32) / np.float32(half))).astype(np.float32)
    ang_r = row[:, None] * inv_freq
    ang_c = col[:, None] * inv_freq
    ang = np.concatenate([ang_r, ang_r, ang_c, ang_c], axis=-1)
    cos, sin = np.cos(ang), np.sin(ang)
    upper = (np.arange(HEAD_DIM) % 32) < 16
    sa = np.where(upper, -sin, 0.0)
    sb = np.where(upper, 0.0, sin)
    two = lambda a: jnp.asarray(np.concatenate([a, a], axis=-1), dtype=F32)
    return two(cos), two(sa), two(sb)


def _layer(x, mods, mod_rows, lam_init, rope_tabs, ctx_k, ctx_v, h0, weights, prep):
    n_batch, seq_len = x.shape[:2]
    g = weights['norm_g']
    outs = _in_proj(x, mods, mod_rows, g[0:1], weights['w_in'], rope_tabs)
    q, k, v = outs[:3]
    attn = _attention(q, k, v, ctx_k, ctx_v, weights['lam'], weights['subln_g'],
                      n_batch, seq_len, lam_init)
    pending = () if 'w_ffn_in' in weights else (weights['w_ffn_in_f32'],)
    ssm_out = _ssm(outs[-1], *prep, weights['w_glu_t'], weights['b_glu_col'], h0, n_batch,
                   cast_weights=pending)
    n_state = 0 if h0 is not None else 1
    if pending:
        weights['w_ffn_in'] = ssm_out[1 + n_state]
    y = _out_ffn(x.reshape(n_batch * seq_len, D_MODEL), attn, ssm_out[0], mods, mod_rows, g,
                 weights['w_o'], weights['w_ffn_in'], weights['w_ffn_out'])
    return y.reshape(x.shape), outs[3:-1], ssm_out[1:1 + n_state]


def kernel(x_prompt, x_sample, cache_k, cache_v, state_ssm_re, state_ssm_im, c, c_ctx, w_mod, b_mod, norm_g, w_in, lam_params, subln_g, ssm_lambda_re, ssm_lambda_im, ssm_log_step, ssm_b_re, ssm_b_im, ssm_c_re, ssm_c_im, ssm_d, w_glu, b_glu, w_o, w_ffn_in, w_ffn_out):
    depth = w_mod.shape[0]
    assert depth == 1
    bp = x_prompt.shape[0]
    bd, ld_len = x_sample.shape[:2]
    past = cache_k.shape[2]
    xp, xs = x_prompt, x_sample
    rope_tabs = _rope_tables(ld_len)
    ks_out, vs_out, hr_out, hi_out = [], [], [], []
    for l in range(depth):
        lam_init = 0.8 - 0.6 * math.exp(-0.3 * l)
        prep, mods, (w_in_bf16, w_o_bf16, w_ffn_out_bf16) = _ssm_prep_and_modulation(
            ssm_lambda_re[l], ssm_lambda_im[l], ssm_log_step[l], ssm_b_re[l], ssm_b_im[l],
            ssm_c_re[l], ssm_c_im[l], ssm_d[l], c_ctx, c, w_mod[l], b_mod[l],
            (w_in[l], w_o[l], w_ffn_out[l]))
        weights = {
            'norm_g': norm_g[l],
            'w_in': w_in_bf16, 'w_o': w_o_bf16, 'w_ffn_out': w_ffn_out_bf16,
            'w_ffn_in_f32': w_ffn_in[l],
            'lam': lam_params[l], 'subln_g': subln_g[l],
            'w_glu_t': w_glu[l].T.astype(BF16), 'b_glu_col': b_glu[l].reshape(SSM_WIDTH, 1),
        }
        ck = jnp.transpose(cache_k[:, l], (0, 2, 3, 1)).reshape(bd, ATT_WIDTH, past)
        cv = cache_v[:, l]
        h0 = jnp.stack([state_ssm_re[:, l], state_ssm_im[:, l]], axis=2)
        h0 = h0.transpose(3, 0, 1, 2, 4).reshape(N_GROUPS, bd, CW)
        xs, _, _ = _layer(xs, mods, (1, bd), lam_init, rope_tabs, ck, cv, h0, weights, prep)
        xp, (k_ctx, v_ctx), (st,) = _layer(xp, mods, (0, 1), lam_init, None, None, None, None,
                                           weights, prep)
        ks_out.append(jnp.swapaxes(k_ctx, 1, 2).reshape(bp, -1, 2 * N_HEADS, HEAD_DIM))
        vs_out.append(v_ctx)
        fin = st.reshape(N_GROUPS, bp, 2, 2, SSM_STATE).transpose(1, 2, 3, 0, 4)
        hr_out.append(fin[:, :, 0])
        hi_out.append(fin[:, :, 1])
    return (xp, xs, jnp.stack(ks_out, axis=1), jnp.stack(vs_out, axis=1),
            jnp.stack(hr_out, axis=1), jnp.stack(hi_out, axis=1))
```

```python
import functools
import math

import jax
import jax.numpy as jnp
import numpy as np
from jax import lax
from jax.experimental import pallas as pl
from jax.experimental.pallas import tpu as pltpu

F32 = jnp.float32
BF16 = jnp.bfloat16

D_MODEL = 1024
GRID_W = 64
ATT_WIDTH = 512
SSM_WIDTH = 512
HEAD_DIM = 64
N_HEADS = 4
HEAD_W = 2 * HEAD_DIM
SSM_GROUP = 16
N_GROUPS = 32
SSM_STATE = 64
D_FF = 2816
N_MOD = 6
ROPE_BASE = 10000.0
NORM_EPS = 1e-6

CHUNK = 16
CW = CHUNK * SSM_GROUP
SCAN_BLOCK = 8
SCAN_LEVELS = 3
AT_ROWS = 16
GROUP_BLOCK = 4
Z_BATCH = 8

ROWS_PER_TILE = 128
TOK_PER_TILE = ROWS_PER_TILE * CHUNK
SUB_TOK = 1024
TQ_ITEM = 256
N_SUB = TOK_PER_TILE // SUB_TOK

VMEM_LIMIT = 56 * 1024 * 1024

NT_DIMS = (((1,), (1,)), ((), ()))
TN_DIMS = (((0,), (0,)), ((), ()))


def _cparams(n_axes):
    return pltpu.CompilerParams(
        dimension_semantics=("arbitrary",) * n_axes,
        vmem_limit_bytes=VMEM_LIMIT)


def _rms(x, g):
    ms = jnp.mean(x * x, axis=-1, keepdims=True)
    return x * lax.rsqrt(ms + NORM_EPS) * g


def _silu(x):
    return x * jax.nn.sigmoid(x)


MOD_ROWS = 8


def _mod_kernel(ctx_ref, c_ref, w_ref, b_ref, o_ref):
    n_lat, tk = c_ref.shape
    row = lax.broadcasted_iota(jnp.int32, (MOD_ROWS, tk), 0)
    cond = jnp.where(row == 0, ctx_ref[...], 0.0)
    for b in range(n_lat):
        cond = jnp.where(row == 1 + b, c_ref[b:b + 1, :], cond)
    part = jnp.dot(_silu(cond).astype(BF16), w_ref[...].astype(BF16),
                   preferred_element_type=F32)

    @pl.when(pl.program_id(0) == 0)
    def _():
        o_ref[:, 0, :] = part + b_ref[...]

    @pl.when(pl.program_id(0) > 0)
    def _():
        o_ref[:, 0, :] += part


def _rope(x, cos, sa, sb):
    return (x * cos + pltpu.roll(x, HEAD_W - 16, axis=1) * sa
            + pltpu.roll(x, 16, axis=1) * sb)


def _in_proj_kernel(*refs, rope, seqs_per_sub):
    x_ref, x3_hbm, mod_ref, g_ref, w_ref = refs[:5]
    refs = refs[5:]
    if rope:
        cos_ref, sa_ref, sb_ref = refs[:3]
        refs = refs[3:]
        q_ref, k_ref, v_ref, ut_ref, wut_ref, xt_ref, xt_sem = refs
    else:
        q_ref, k_ref, v_ref, kc_ref, vc_ref, ut_ref, wut_ref, xt_ref, xt_sem = refs
    tile = pl.program_id(0)
    j = pl.program_id(1)
    t_per_sub = CHUNK // N_SUB
    t_early = CHUNK - t_per_sub

    def gather(tile_idx, t):
        src = x3_hbm.at[pl.ds(tile_idx * ROWS_PER_TILE, ROWS_PER_TILE), t, :]
        return pltpu.make_async_copy(src, xt_ref.at[t], xt_sem.at[t])

    @pl.when(j == 0)
    def _():
        @pl.when(tile == 0)
        def _():
            for t in range(t_early):
                gather(0, t).start()
        for t in range(t_early, CHUNK):
            gather(tile, t).start()

    t_base = j * t_per_sub
    for d in range(t_per_sub):
        gather(tile, t_base + d).wait()

    @pl.when((j == N_SUB - 1) & (tile + 1 < pl.num_programs(0)))
    def _():
        for t in range(t_early):
            gather(tile + 1, t).start()

    @pl.when((tile == 0) & (j == 0))
    def _():
        wut_ref[...] = w_ref[:, 3 * ATT_WIDTH:].T

    shift = mod_ref[:, 0:D_MODEL]
    gain = g_ref[...] * (1.0 + mod_ref[:, D_MODEL:2 * D_MODEL])

    def norm_mod(xv):
        ms = jnp.mean(xv * xv, axis=-1, keepdims=True)
        return (xv * lax.rsqrt(ms + NORM_EPS) * gain + shift).astype(BF16)

    def ssm_input(d0):
        xt = jnp.concatenate([xt_ref[t_base + d0], xt_ref[t_base + d0 + 1]], axis=0)
        ut = lax.dot_general(wut_ref[...], norm_mod(xt), NT_DIMS,
                             preferred_element_type=F32)
        for d in range(2):
            blk = ut[:, d * ROWS_PER_TILE:(d + 1) * ROWS_PER_TILE]
            row0 = pl.multiple_of((t_base + d0 + d) * SSM_GROUP, SSM_GROUP)
            ut_ref[:, pl.ds(row0, SSM_GROUP), :] = (
                blk.reshape(N_GROUPS, SSM_GROUP, ROWS_PER_TILE).astype(ut_ref.dtype))

    proj = jnp.dot(norm_mod(x_ref[...]), w_ref[:, 0:3 * ATT_WIDTH],
                   preferred_element_type=F32)
    q = proj[:, 0:ATT_WIDTH]
    k = proj[:, ATT_WIDTH:2 * ATT_WIDTH]
    v = proj[:, 2 * ATT_WIDTH:3 * ATT_WIDTH]
    qscale = HEAD_DIM ** -0.5 * math.log2(math.e)
    if rope:
        cos, sa, sb = cos_ref[...], sa_ref[...], sb_ref[...]
        for hd in range(N_HEADS):
            sl = slice(hd * HEAD_W, (hd + 1) * HEAD_W)
            q_ref[:, sl] = (_rope(q[:, sl], cos, sa, sb) * qscale).astype(q_ref.dtype)
            k_ref[:, sl] = _rope(k[:, sl], cos, sa, sb).astype(k_ref.dtype)
    else:
        q_ref[...] = (q * qscale).astype(q_ref.dtype)
        k_ref[...] = k.astype(k_ref.dtype)
        seq = SUB_TOK // seqs_per_sub
        k_t = k.T
        for b in range(seqs_per_sub):
            kc_ref[b] = k_t[:, b * seq:(b + 1) * seq]
            for hd in range(N_HEADS):
                vc_ref[b, :, hd, :] = v[b * seq:(b + 1) * seq, hd * HEAD_W:(hd + 1) * HEAD_W]
    v_ref[...] = v.astype(v_ref.dtype)

    for d0 in range(0, t_per_sub, 2):
        ssm_input(d0)


def _in_proj(x, mods, mod_rows, g0, w_in, rope_tabs):
    n_batch, seq_len = x.shape[:2]
    n_tok = n_batch * seq_len
    n_rows = n_tok // CHUNK
    n_tiles = n_tok // TOK_PER_TILE
    mod_row0, n_mod = mod_rows
    tiles_per_mod = n_tiles // n_mod
    rope = rope_tabs is not None
    seqs_per_sub = max(1, SUB_TOK // seq_len)
    in_specs = [pl.BlockSpec((SUB_TOK, D_MODEL), lambda i, j: (i * N_SUB + j, 0)),
                pl.BlockSpec(memory_space=pl.ANY),
                pl.BlockSpec((None, 1, 2 * D_MODEL),
                             lambda i, j: (mod_row0 + i // tiles_per_mod, 0, 0)),
                pl.BlockSpec((1, D_MODEL), lambda i, j: (0, 0)),
                pl.BlockSpec((D_MODEL, 4 * ATT_WIDTH), lambda i, j: (0, 0))]
    args = [x.reshape(n_tok, D_MODEL), x.reshape(n_rows, CHUNK, D_MODEL), mods, g0, w_in]
    row_spec = pl.BlockSpec((SUB_TOK, ATT_WIDTH), lambda i, j: (i * N_SUB + j, 0))
    row_shape = jax.ShapeDtypeStruct((n_tok, ATT_WIDTH), BF16)
    out_specs = [row_spec, row_spec, row_spec]
    out_shape = [row_shape, row_shape, row_shape]
    if rope:
        assert seq_len == TOK_PER_TILE
        for tab in rope_tabs:
            in_specs.append(pl.BlockSpec((SUB_TOK, HEAD_W), lambda i, j: (j, 0)))
            args.append(tab)
    else:
        out_specs += [pl.BlockSpec((seqs_per_sub, ATT_WIDTH, seq_len),
                                   lambda i, j: (i * N_SUB + j, 0, 0)),
                      pl.BlockSpec((seqs_per_sub, seq_len, N_HEADS, HEAD_W),
                                   lambda i, j: (i * N_SUB + j, 0, 0, 0))]
        out_shape += [jax.ShapeDtypeStruct((n_batch, ATT_WIDTH, seq_len), F32),
                      jax.ShapeDtypeStruct((n_batch, seq_len, N_HEADS, HEAD_W), F32)]
    out_specs.append(pl.BlockSpec((N_GROUPS, CW, ROWS_PER_TILE), lambda i, j: (0, 0, i)))
    out_shape.append(jax.ShapeDtypeStruct((N_GROUPS, CW, n_rows), BF16))
    return pl.pallas_call(
        functools.partial(_in_proj_kernel, rope=rope, seqs_per_sub=seqs_per_sub),
        grid=(n_tiles, N_SUB),
        in_specs=in_specs,
        out_specs=out_specs,
        out_shape=out_shape,
        scratch_shapes=[pltpu.VMEM((SSM_WIDTH, D_MODEL), BF16),
                        pltpu.VMEM((CHUNK, ROWS_PER_TILE, D_MODEL), F32),
                        pltpu.SemaphoreType.DMA((CHUNK,))],
        compiler_params=_cparams(2),
        name="in_proj",
    )(*args)


def _attn_kernel(*refs, has_ctx, lam_init, n_seq, seq_len, tq, n_cast):
    if n_cast:
        cast_in = refs[len(refs) - 2 * n_cast - 1:len(refs) - n_cast - 1]
        cast_out = refs[len(refs) - n_cast:]
        refs = refs[:len(refs) - 2 * n_cast - 1] + (refs[len(refs) - n_cast - 1],)
        for src, dst in zip(cast_in, cast_out):
            dst[...] = src[...].astype(dst.dtype)
    if has_ctx:
        lam_ref, sg_ref, q_ref, ck_ref, cv_ref, k_ref, v_ref, o_ref = refs
    else:
        lam_ref, sg_ref, q_ref, k_ref, v_ref, o_ref = refs
    lp = lam_ref[...]
    lam = (jnp.exp(jnp.sum(lp[0:1] * lp[1:2], axis=-1, keepdims=True))
           - jnp.exp(jnp.sum(lp[2:3] * lp[3:4], axis=-1, keepdims=True)) + lam_init)
    first_map = lax.broadcasted_iota(jnp.int32, (1, HEAD_W), 1) < HEAD_DIM
    ti = min(TQ_ITEM, tq)
    for row0 in range(0, n_seq * tq, ti):
        b = row0 // tq
        q_rows = slice(row0, row0 + ti)
        for hd in range(N_HEADS):
            sl = slice(hd * HEAD_W, (hd + 1) * HEAD_W)
            qh = q_ref[q_rows, sl]
            zero = jnp.zeros_like(qh)
            qs = jnp.concatenate([jnp.where(first_map, qh, zero),
                                  jnp.where(first_map, zero, qh)], axis=0)
            kv_rows = slice(b * seq_len, (b + 1) * seq_len)
            parts = [(k_ref[kv_rows, sl], v_ref[kv_rows, sl])]
            scores = [lax.dot_general(qs, parts[0][0], NT_DIMS, preferred_element_type=F32)]
            if has_ctx:
                parts.insert(0, (None, cv_ref[:, hd, :].astype(BF16)))
                scores.insert(0, jnp.dot(qs, ck_ref[sl, :].astype(BF16),
                                         preferred_element_type=F32))
            mx = scores[0].max(axis=-1, keepdims=True)
            for s in scores[1:]:
                mx = jnp.maximum(mx, s.max(axis=-1, keepdims=True))
            acc = None
            for s, (_, vv) in zip(scores, parts):
                e = jnp.exp2(s - mx).astype(BF16)
                v_one = jnp.concatenate([vv, jnp.ones_like(vv)], axis=1)
                pv = jnp.dot(e, v_one, preferred_element_type=F32)
                acc = pv if acc is None else acc + pv
            num = acc[:, 0:HEAD_W] / acc[:, HEAD_W:2 * HEAD_W]
            o = num[0:ti] - lam * num[ti:2 * ti]
            o = _rms(o, sg_ref[...]) * (1.0 - lam_init)
            o_ref[q_rows, sl] = o.astype(o_ref.dtype)


def _attention(q, k, v, ctx_k, ctx_v, lam_params, subln_g, n_batch, seq_len, lam_init,
               cast_weights=()):
    has_ctx = ctx_k is not None
    tq = min(1024, seq_len)
    n_q = seq_len // tq
    n_seq = 1 if n_q > 1 else min(4, n_batch)
    in_specs = [pl.BlockSpec((4, HEAD_DIM), lambda b, i: (0, 0)),
                pl.BlockSpec((1, HEAD_W), lambda b, i: (0, 0)),
                pl.BlockSpec((n_seq * tq, ATT_WIDTH), lambda b, i: (b * n_q + i, 0))]
    args = [lam_params, subln_g.reshape(1, HEAD_W), q]
    if has_ctx:
        past = ctx_v.shape[1]
        in_specs +=[pl.BlockSpec((None, ATT_WIDTH, past), lambda b, i: (b, 0, 0)),
                     pl.BlockSpec((None, past, N_HEADS, HEAD_W), lambda b, i: (b, 0, 0, 0))]
        args += [ctx_k, ctx_v]
    kv_spec = pl.BlockSpec((n_seq * seq_len, ATT_WIDTH), lambda b, i: (b, 0))
    in_specs += [kv_spec, kv_spec]
    args += [k, v]
    out_specs = [pl.BlockSpec((n_seq * tq, ATT_WIDTH), lambda b, i: (b * n_q + i, 0))]
    out_shape = [jax.ShapeDtypeStruct((n_batch * seq_len, ATT_WIDTH), BF16)]
    n_steps = (n_batch // n_seq) * n_q
    for w in cast_weights:
        rows = w.shape[0] // n_steps
        spec = pl.BlockSpec((rows, w.shape[1]), lambda b, i: (b * n_q + i, 0))
        in_specs.append(spec)
        args.append(w)
        out_specs.append(spec)
        out_shape.append(jax.ShapeDtypeStruct(w.shape, BF16))
    outs = pl.pallas_call(
        functools.partial(_attn_kernel, has_ctx=has_ctx, lam_init=lam_init,
                          n_seq=n_seq, seq_len=seq_len, tq=tq, n_cast=len(cast_weights)),
        grid=(n_batch // n_seq, n_q),
        in_specs=in_specs,
        out_specs=out_specs,
        out_shape=out_shape,
        compiler_params=_cparams(2),
        name="diff_attention",
    )(*args)
    return outs[0], tuple(outs[1:])


def _cmul(ar, ai, br, bi):
    return ar * br - ai * bi, ar * bi + ai * br


def _ssm_prep_kernel(lre_ref, lim_ref, ls_ref, bre_ref, bim_ref, cre_ref, cim_ref, d_ref,
                     mt_ref, gt_ref, wo_ref, at_ref):
    lane = lax.broadcasted_iota(jnp.int32, (SSM_GROUP, CW), 1)
    chan = lax.broadcasted_iota(jnp.int32, (SSM_GROUP, CW), 0)
    for gi in range(GROUP_BLOCK):
        gt_cols, wo_cols, at_cols, toeplitz = [], [], [], []
        for dr in range(2):
            lr = jnp.minimum(lre_ref[dr, gi], -1e-4)
            li = lim_ref[dr, gi]
            step = jnp.exp(ls_ref[dr, gi])
            mag = jnp.exp(lr * step)
            a_re = mag * jnp.cos(li * step)
            a_im = mag * jnp.sin(li * step)
            den = lr * lr + li * li
            nr = a_re - 1.0
            f_re = (nr * lr + a_im * li) / den
            f_im = (a_im * lr - nr * li) / den
            bt_re, bt_im = bre_ref[dr, gi], bim_ref[dr, gi]
            bb_re, bb_im = _cmul(f_re, f_im, bt_re, bt_im)
            c_re, c_im = cre_ref[dr, gi], cim_ref[dr, gi]
            pw = [(jnp.ones_like(a_re), jnp.zeros_like(a_im))]
            for _ in range(CHUNK):
                pw.append(_cmul(pw[-1][0], pw[-1][1], a_re, a_im))
            g_re, g_im, e_re, e_im = [], [], [], []
            for t in range(CHUNK):
                pr, pi = pw[CHUNK - 1 - t] if dr == 0 else pw[t]
                r, i = _cmul(bb_re, bb_im, pr, pi)
                g_re.append(r)
                g_im.append(i)
                pr, pi = pw[t + 1] if dr == 0 else pw[CHUNK - t]
                r, i = _cmul(c_re, c_im, pr, pi)
                e_re.append(r)
                e_im.append(-i)
            g_cat = jnp.concatenate([jnp.concatenate(g_re, axis=0),
                                     jnp.concatenate(g_im, axis=0)], axis=1)
            gt_cols.append(g_cat)
            wo_cols.append(jnp.concatenate([jnp.concatenate(e_re, axis=0),
                                            jnp.concatenate(e_im, axis=0)], axis=1))
            c_cat = jnp.concatenate([c_re, -c_im], axis=1)
            toeplitz.append(lax.dot_general(c_cat, g_cat, NT_DIMS,
                                            precision=lax.Precision.HIGHEST,
                                            preferred_element_type=F32))
            apw = [pw[CHUNK]]
            for _ in range(SCAN_BLOCK - 1):
                apw.append(_cmul(apw[-1][0], apw[-1][1], apw[0][0], apw[0][1]))
            order = list(range(SCAN_BLOCK)) if dr == 0 else list(range(SCAN_BLOCK - 1, -1, -1))
            order += [2 ** l - 1 for l in range(SCAN_LEVELS)]
            order += [0] * (AT_ROWS - len(order))
            at_cols += [jnp.concatenate([jnp.concatenate([apw[i][0], apw[i][0]], axis=1)
                                         for i in order], axis=0),
                        jnp.concatenate([jnp.concatenate([-apw[i][1], apw[i][1]], axis=1)
                                         for i in order], axis=0)]
        kf_rev, kb = toeplitz
        d_skip = d_ref[gi]
        blocks = []
        for t in range(CHUNK):
            fwd = pltpu.roll(kf_rev, (CW - (CHUNK - 1 - t) * SSM_GROUP) % CW, axis=1)
            bwd = pltpu.roll(kb, t * SSM_GROUP, axis=1)
            blocks.append(jnp.where(lane < (t + 1) * SSM_GROUP, fwd, 0.0)
                          + jnp.where(lane >= t * SSM_GROUP, bwd, 0.0)
                          + jnp.where(lane == chan + t * SSM_GROUP, d_skip, 0.0))
        mt_ref[gi] = jnp.concatenate(blocks, axis=0).astype(mt_ref.dtype)
        gt_ref[gi] = jnp.concatenate(gt_cols, axis=1).astype(gt_ref.dtype)
        wo_ref[gi] = jnp.concatenate(wo_cols, axis=1).astype(wo_ref.dtype)
        at_ref[gi] = jnp.concatenate(at_cols, axis=1)


N_PREP_IN, N_PREP_OUT, N_MOD_IN = 8, 4, 4


def _prep_mod_kernel(*refs):
    prep_in = refs[:N_PREP_IN]
    mod_in = refs[N_PREP_IN:N_PREP_IN + N_MOD_IN]
    w_in_ref = refs[N_PREP_IN + N_MOD_IN]
    outs = refs[N_PREP_IN + N_MOD_IN + 1:]
    _ssm_prep_kernel(*prep_in, *outs[:N_PREP_OUT])
    _mod_kernel(*mod_in, outs[N_PREP_OUT])
    outs[N_PREP_OUT + 1][...] = w_in_ref[...].astype(BF16)


def _ssm_prep_and_modulation(lam_re, lam_im, log_step, b_re, b_im, c_re, c_im, d_skip,
                             c_ctx, c, w_mod, b_mod, w_in):
    row = lambda a: a.reshape(2, N_GROUPS, 1, SSM_STATE)
    bt = lambda a: jnp.swapaxes(a, 2, 3)
    d_row = jnp.tile((d_skip[0] + d_skip[1]).reshape(N_GROUPS, 1, SSM_GROUP), (1, 1, CHUNK))
    gb = GROUP_BLOCK
    n_steps = N_GROUPS // gb
    vec_spec = pl.BlockSpec((2, gb, 1, SSM_STATE), lambda i: (0, i, 0, 0))
    mat_spec = pl.BlockSpec((2, gb, SSM_GROUP, SSM_STATE), lambda i: (0, i, 0, 0))
    w_spec = pl.BlockSpec((gb, CW, CW), lambda i: (i, 0, 0))
    w_shape = jax.ShapeDtypeStruct((N_GROUPS, CW, CW), BF16)
    n_mod = w_mod.shape[1]
    tk = D_MODEL // n_steps
    assert 1 + c.shape[0] <= MOD_ROWS
    outs = pl.pallas_call(
        _prep_mod_kernel,
        grid=(n_steps,),
        in_specs=[vec_spec, vec_spec,
                  pl.BlockSpec((2, gb, 1, 1), lambda i: (0, i, 0, 0)),
                  mat_spec, mat_spec, mat_spec, mat_spec,
                  pl.BlockSpec((gb, 1, CW), lambda i: (i, 0, 0)),
                  pl.BlockSpec((1, tk), lambda k: (0, k)),
                  pl.BlockSpec((c.shape[0], tk), lambda k: (0, k)),
                  pl.BlockSpec((tk, n_mod), lambda k: (k, 0)),
                  pl.BlockSpec((1, n_mod), lambda k: (0, 0)),
                  pl.BlockSpec((tk, w_in.shape[1]), lambda k: (k, 0))],
        out_specs=[w_spec, w_spec, w_spec,
                   pl.BlockSpec((gb, AT_ROWS, 4 * 2 * SSM_STATE), lambda i: (i, 0, 0)),
                   pl.BlockSpec((MOD_ROWS, 1, n_mod), lambda k: (0, 0, 0)),
                   pl.BlockSpec((tk, w_in.shape[1]), lambda k: (k, 0))],
        out_shape=[w_shape, w_shape, w_shape,
                   jax.ShapeDtypeStruct((N_GROUPS, AT_ROWS, 4 * 2 * SSM_STATE), F32),
                   jax.ShapeDtypeStruct((MOD_ROWS, 1, n_mod), F32),
                   jax.ShapeDtypeStruct(w_in.shape, BF16)],
        compiler_params=_cparams(1),
        name="ssm_prep_modulation",
    )(row(lam_re), row(lam_im), log_step.reshape(2, N_GROUPS, 1, 1),
      bt(b_re), bt(b_im), c_re, c_im, d_row,
      c_ctx.reshape(1, D_MODEL), c, w_mod, b_mod.reshape(1, n_mod), w_in)
    return outs[:N_PREP_OUT], outs[N_PREP_OUT], outs[N_PREP_OUT + 1]


def _shift_rows(x, m, down):
    n = x.shape[0]
    return pltpu.roll(x, m if down else n - m, axis=0)


def _ssm_kernel(*refs, n_seq, has_h0):
    if has_h0:
        (xt_ref, mt_ref, gt_ref, wo_ref, at_ref, wg_ref, bg_ref, h0_ref,
         z_hbm, zs_ref, zb_ref, zb_sem) = refs
    else:
        (xt_ref, mt_ref, gt_ref, wo_ref, at_ref, wg_ref, bg_ref,
         z_hbm, st_ref, zs_ref, zb_ref, zb_sem, fin_ref) = refs
    step = pl.program_id(0)
    n_rows = xt_ref.shape[-1]
    seg = n_rows // n_seq
    cw2 = 2 * SSM_STATE
    assert seg % SCAN_BLOCK == 0
    n_blk = n_rows // SCAN_BLOCK
    blk_per_seq = seg // SCAN_BLOCK
    pos = lax.broadcasted_iota(jnp.int32, (n_rows, cw2), 0) % seg
    row_blk = lax.broadcasted_iota(jnp.int32, (SCAN_BLOCK, cw2), 0)

    def low_half(shape):
        return lax.broadcasted_iota(jnp.int32, shape, 1) < SSM_STATE

    def swap(v):
        return pltpu.roll(v, SSM_STATE, axis=1)

    def to_planes(va, vb):
        lo = low_half((va.shape[0], cw2))
        va_l = pltpu.roll(va, 3 * SSM_STATE, axis=1)
        vb_r = pltpu.roll(vb, SSM_STATE, axis=1)
        return (jnp.where(lo, va[:, 0:cw2], vb_r[:, 0:cw2]),
                jnp.where(lo, va_l[:, 0:cw2], vb[:, 0:cw2]),
                jnp.where(lo, va[:, cw2:2 * cw2], vb_r[:, cw2:2 * cw2]),
                jnp.where(lo, va_l[:, cw2:2 * cw2], vb[:, cw2:2 * cw2]))

    def from_planes(f_re, f_im, b_re, b_im):
        lo = low_half(f_re.shape)
        va = jnp.concatenate([jnp.where(lo, f_re, swap(f_im)), jnp.where(lo, b_re, swap(b_im))], axis=1)
        vb = jnp.concatenate([jnp.where(lo, swap(f_re), f_im), jnp.where(lo, swap(b_re), b_im)], axis=1)
        return va, vb

    for ga in range(0, GROUP_BLOCK, 2):
        gb = ga + 1
        xts = [xt_ref[ga], xt_ref[gb]]
        s_pair = [lax.dot_general(xts[i], gt_ref[g], TN_DIMS, preferred_element_type=F32)
                  for i, g in enumerate((ga, gb))]
        planes = to_planes(*s_pair)
        lo_t = low_half((AT_ROWS, cw2))
        if has_h0:
            h0_planes = to_planes(h0_ref[ga], h0_ref[gb])
        ent_planes = []
        for dr in range(2):
            down = dr == 0
            pa, pb = (at_ref[g][:, (2 * dr) * cw2:(2 * dr + 1) * cw2] for g in (ga, gb))
            qa, qb = (at_ref[g][:, (2 * dr + 1) * cw2:(2 * dr + 2) * cw2] for g in (ga, gb))
            ar_tab = jnp.where(lo_t, pa, pb)
            ai_tab = jnp.where(lo_t, -qa, qb)
            re3 = planes[2 * dr].reshape(n_blk, SCAN_BLOCK, cw2)
            im3 = planes[2 * dr + 1].reshape(n_blk, SCAN_BLOCK, cw2)
            for lvl in range(SCAN_LEVELS):
                m = 2 ** lvl
                valid = (row_blk >= m) if down else (row_blk < SCAN_BLOCK - m)
                row = SCAN_BLOCK + lvl
                ar = jnp.where(valid, ar_tab[row:row + 1], 0.0)
                ai = jnp.where(valid, ai_tab[row:row + 1], 0.0)
                shift = m if down else SCAN_BLOCK - m
                sh_re = pltpu.roll(re3, shift, axis=1)
                sh_im = pltpu.roll(im3, shift, axis=1)
                re3, im3 = re3 + ar * sh_re - ai * sh_im, im3 + ar * sh_im + ai * sh_re
            ar_blk, ai_blk = ar_tab[0:SCAN_BLOCK], ai_tab[0:SCAN_BLOCK]
            blk_re = [re3[i] for i in range(n_blk)]
            blk_im = [im3[i] for i in range(n_blk)]
            edge = slice(SCAN_BLOCK - 1, SCAN_BLOCK) if down else slice(0, 1)
            for q_i in range(n_seq):
                idxs = list(range(q_i * blk_per_seq, (q_i + 1) * blk_per_seq))
                idxs = idxs if down else idxs[::-1]
                for prev, cur in zip([None] + idxs[:-1], idxs):
                    if prev is not None:
                        c_re, c_im = blk_re[prev][edge], blk_im[prev][edge]
                    elif has_h0:
                        c_re = h0_planes[2 * dr][q_i:q_i + 1]
                        c_im = h0_planes[2 * dr + 1][q_i:q_i + 1]
                    else:
                        continue
                    c_re = jnp.broadcast_to(c_re, (SCAN_BLOCK, cw2))
                    c_im = jnp.broadcast_to(c_im, (SCAN_BLOCK, cw2))
                    blk_re[cur] = blk_re[cur] + ar_blk * c_re - ai_blk * c_im
                    blk_im[cur] = blk_im[cur] + ar_blk * c_im + ai_blk * c_re
            for part, blks in ((0, blk_re), (1, blk_im)):
                s = jnp.concatenate(blks, axis=0)
                if not has_h0:
                    fin_ref[part] = s
                ent = _shift_rows(s, 1, down)
                ent = jnp.where((pos >= 1) if down else (pos < seg - 1), ent, 0.0)
                if has_h0:
                    ent_blk = [ent[i * SCAN_BLOCK:(i + 1) * SCAN_BLOCK] for i in range(n_blk)]
                    at_edge = row_blk == (0 if down else SCAN_BLOCK - 1)
                    for q_i in range(n_seq):
                        bi = q_i * blk_per_seq if down else (q_i + 1) * blk_per_seq - 1
                        ent_blk[bi] = jnp.where(
                            at_edge, h0_planes[2 * dr + part][q_i:q_i + 1], ent_blk[bi])
                    ent = jnp.concatenate(ent_blk, axis=0)
                ent_planes.append(ent)
            if not has_h0:
                rows = pl.ds(seg - 1 if down else 0, n_seq, stride=seg)
                f_re, f_im = fin_ref[0, rows, :], fin_ref[1, rows, :]
                lo_s = low_half((n_seq, cw2))
                st_ref[ga, :, dr * cw2:(dr + 1) * cw2] = jnp.where(lo_s, f_re, swap(f_im))
                st_ref[gb, :, dr * cw2:(dr + 1) * cw2] = jnp.where(lo_s, swap(f_re), f_im)
        h_pair = from_planes(*ent_planes)
        for i, g in enumerate((ga, gb)):
            yt = (jnp.dot(mt_ref[g], xts[i], preferred_element_type=F32)
                  + lax.dot_general(wo_ref[g], h_pair[i].astype(BF16), NT_DIMS,
                                    preferred_element_type=F32))
            z = jax.nn.gelu(yt, approximate=True)
            grp = step * GROUP_BLOCK + g
            for t in range(CHUNK):
                zs_ref[t, pl.ds(pl.multiple_of(grp * SSM_GROUP, SSM_GROUP), SSM_GROUP), :] = (
                    z[t * SSM_GROUP:(t + 1) * SSM_GROUP, :])

    @pl.when(step == pl.num_programs(0) - 1)
    def _():
        def put(t):
            return pltpu.make_async_copy(zb_ref.at[t], z_hbm.at[:, t, :], zb_sem.at[t])

        for t0 in range(0, CHUNK, Z_BATCH):
            for t in range(t0, t0 + Z_BATCH):
                zt = zs_ref[t]
                gate = jnp.dot(wg_ref[...], zt.astype(BF16),
                               preferred_element_type=F32) + bg_ref[...]
                zb_ref[t] = (zt * jax.nn.sigmoid(gate)).T
            for t in range(t0, t0 + Z_BATCH):
                put(t).start()
        for t in range(CHUNK):
            put(t).wait()


def _ssm(xt, mt, gt, wo, at, w_glu_t, b_glu_col, h0, n_seq):
    n_rows = xt.shape[-1]
    has_h0 = h0 is not None
    gb = GROUP_BLOCK
    w_spec = pl.BlockSpec((gb, CW, CW), lambda i: (i, 0, 0))
    in_specs = [pl.BlockSpec((gb, CW, n_rows), lambda i: (i, 0, 0)),
                w_spec, w_spec, w_spec,
                pl.BlockSpec((gb, AT_ROWS, 4 * 2 * SSM_STATE), lambda i: (i, 0, 0)),
                pl.BlockSpec((SSM_WIDTH, SSM_WIDTH), lambda i: (0, 0)),
                pl.BlockSpec((SSM_WIDTH, 1), lambda i: (0, 0))]
    args = [xt, mt, gt, wo, at, w_glu_t, b_glu_col]
    out_specs = [pl.BlockSpec(memory_space=pl.ANY)]
    out_shape = [jax.ShapeDtypeStruct((n_rows, CHUNK, SSM_WIDTH), F32)]
    scratch = [pltpu.VMEM((CHUNK, SSM_WIDTH, n_rows), F32),
               pltpu.VMEM((CHUNK, n_rows, SSM_WIDTH), F32),
               pltpu.SemaphoreType.DMA((CHUNK,))]
    if has_h0:
        in_specs.append(pl.BlockSpec((gb, n_seq, CW), lambda i: (i, 0, 0)))
        args.append(h0)
    else:
        out_specs.append(pl.BlockSpec((gb, n_seq, CW), lambda i: (i, 0, 0)))
        out_shape.append(jax.ShapeDtypeStruct((N_GROUPS, n_seq, CW), F32))
        scratch.append(pltpu.VMEM((2, n_rows, 2 * SSM_STATE), F32))
    return pl.pallas_call(
        functools.partial(_ssm_kernel, n_seq=n_seq, has_h0=has_h0),
        grid=(N_GROUPS // gb,),
        in_specs=in_specs,
        out_specs=out_specs,
        out_shape=out_shape,
        scratch_shapes=scratch,
        compiler_params=_cparams(1),
        name="ssm_scan_glu",
    )(*args)


FF_CHUNK = 256
FFN_TOK = 1024


def _out_ffn_kernel(x_ref, attn_ref, z_ref, mod_ref, g_ref, wo_ref, wfi_ref, wfo_ref,
                    o_ref, act_ref):
    gate1 = mod_ref[:, 2 * D_MODEL:3 * D_MODEL]
    shift2 = mod_ref[:, 3 * D_MODEL:4 * D_MODEL]
    scale2 = mod_ref[:, 4 * D_MODEL:5 * D_MODEL]
    gate2 = mod_ref[:, 5 * D_MODEL:6 * D_MODEL]
    half = FFN_TOK // 2
    gain2 = g_ref[2:3, :] * (1.0 + scale2)

    def pre_ffn(hf):
        r = slice(hf * half, (hf + 1) * half)
        mixer = jnp.concatenate([attn_ref[r, :], z_ref[r, :].astype(BF16)], axis=1)
        mix = jnp.dot(mixer, wo_ref[...], preferred_element_type=F32)
        x1 = x_ref[r, :] + gate1 * _rms(mix, g_ref[1:2, :])
        ms = jnp.mean(x1 * x1, axis=-1, keepdims=True)
        return x1, (x1 * lax.rsqrt(ms + NORM_EPS) * gain2 + shift2).astype(BF16)

    def ffn_in(hf, h, chunks):
        r = slice(hf * half, (hf + 1) * half)
        for c in chunks:
            lo = c * FF_CHUNK
            gt = jnp.dot(h, wfi_ref[:, lo:lo + FF_CHUNK], preferred_element_type=F32)
            up = jnp.dot(h, wfi_ref[:, D_FF + lo:D_FF + lo + FF_CHUNK],
                         preferred_element_type=F32)
            act_ref[r, lo:lo + FF_CHUNK] = (_silu(gt) * up).astype(BF16)

    def ffn_out(hf, x1):
        r = slice(hf * half, (hf + 1) * half)
        f = jnp.dot(act_ref[r, :], wfo_ref[...], preferred_element_type=F32)
        o_ref[r, :] = x1 + gate2 * _rms(f, g_ref[3:4, :])

    n_chunks = D_FF // FF_CHUNK
    x1_a, h_a = pre_ffn(0)
    x1_b, h_b = pre_ffn(1)
    ffn_in(0, h_a, range(n_chunks))
    ffn_out(0, x1_a)
    ffn_in(1, h_b, range(n_chunks))
    ffn_out(1, x1_b)


def _out_ffn(x2d, attn, z, mods, mod_rows, norm_g, w_o, w_ffn_in, w_ffn_out):
    n_tok = x2d.shape[0]
    z2d = z.reshape(n_tok, SSM_WIDTH)
    n_steps = n_tok // FFN_TOK
    mod_row0, n_mod = mod_rows
    steps_per_mod = n_steps // n_mod
    const = lambda i: (0, 0)
    row_spec = lambda w: pl.BlockSpec((FFN_TOK, w), lambda i: (i, 0))
    return pl.pallas_call(
        _out_ffn_kernel,
        grid=(n_steps,),
        in_specs=[row_spec(D_MODEL), row_spec(ATT_WIDTH), row_spec(SSM_WIDTH),
                  pl.BlockSpec((None, 1, N_MOD * D_MODEL),
                               lambda i: (mod_row0 + i // steps_per_mod, 0, 0)),
                  pl.BlockSpec((4, D_MODEL), const),
                  pl.BlockSpec((2 * ATT_WIDTH, D_MODEL), const, pipeline_mode=pl.Buffered(1)),
                  pl.BlockSpec((D_MODEL, 2 * D_FF), const, pipeline_mode=pl.Buffered(1)),
                  pl.BlockSpec((D_FF, D_MODEL), const, pipeline_mode=pl.Buffered(1))],
        out_specs=row_spec(D_MODEL),
        out_shape=jax.ShapeDtypeStruct((n_tok, D_MODEL), F32),
        scratch_shapes=[pltpu.VMEM((FFN_TOK, D_FF), BF16)],
        compiler_params=_cparams(1),
        name="out_proj_ffn",
    )(x2d, attn, z2d, mods, norm_g, w_o, w_ffn_in, w_ffn_out)


def _rope_tables(seq_len):
    t = np.arange(seq_len)
    row = (t // GRID_W).astype(np.float32)
    col = (t % GRID_W).astype(np.float32)
    half = HEAD_DIM // 2
    inv_freq = (np.float32(ROPE_BASE)
                ** (-np.arange(0, half, 2, dtype=np.float32) / np.float32(half))).astype(np.float32)
    ang_r = row[:, None] * inv_freq
    ang_c = col[:, None] * inv_freq
    ang = np.concatenate([ang_r, ang_r, ang_c, ang_c], axis=-1)
    cos, sin = np.cos(ang), np.sin(ang)
    upper = (np.arange(HEAD_DIM) % 32) < 16
    sa = np.where(upper, -sin, 0.0)
    sb = np.where(upper, 0.0, sin)
    two = lambda a: jnp.asarray(np.concatenate([a, a], axis=-1), dtype=F32)
    return two(cos), two(sa), two(sb)


def _layer(x, mods, mod_rows, lam_init, rope_tabs, ctx_k, ctx_v, h0, weights, prep):
    n_batch, seq_len = x.shape[:2]
    g = weights['norm_g']
    outs = _in_proj(x, mods, mod_rows, g[0:1], weights['w_in'], rope_tabs)
    q, k, v = outs[:3]
    pending = () if 'late_bf16' in weights else weights['late_f32']
    attn, cast = _attention(q, k, v, ctx_k, ctx_v, weights['lam'], weights['subln_g'],
                            n_batch, seq_len, lam_init, cast_weights=pending)
    if pending:
        weights['late_bf16'] = cast
    ssm_out = _ssm(outs[-1], *prep, weights['w_glu_t'], weights['b_glu_col'], h0, n_batch)
    y = _out_ffn(x.reshape(n_batch * seq_len, D_MODEL), attn, ssm_out[0], mods, mod_rows, g,
                 *weights['late_bf16'])
    return y.reshape(x.shape), outs[3:-1], ssm_out[1:]


def kernel(x_prompt, x_sample, cache_k, cache_v, state_ssm_re, state_ssm_im, c, c_ctx, w_mod, b_mod, norm_g, w_in, lam_params, subln_g, ssm_lambda_re, ssm_lambda_im, ssm_log_step, ssm_b_re, ssm_b_im, ssm_c_re, ssm_c_im, ssm_d, w_glu, b_glu, w_o, w_ffn_in, w_ffn_out):
    depth = w_mod.shape[0]
    assert depth == 1
    bp = x_prompt.shape[0]
    bd, ld_len = x_sample.shape[:2]
    past = cache_k.shape[2]
    xp, xs = x_prompt, x_sample
    rope_tabs = _rope_tables(ld_len)
    ks_out, vs_out, hr_out, hi_out = [], [], [], []
    for l in range(depth):
        lam_init = 0.8 - 0.6 * math.exp(-0.3 * l)
        prep, mods, w_in_bf16 = _ssm_prep_and_modulation(
            ssm_lambda_re[l], ssm_lambda_im[l], ssm_log_step[l], ssm_b_re[l], ssm_b_im[l],
            ssm_c_re[l], ssm_c_im[l], ssm_d[l], c_ctx, c, w_mod[l], b_mod[l], w_in[l])
        weights = {
            'norm_g': norm_g[l],
            'w_in': w_in_bf16,
            'lam': lam_params[l], 'subln_g': subln_g[l],
            'w_glu_t': w_glu[l].T.astype(BF16), 'b_glu_col': b_glu[l].reshape(SSM_WIDTH, 1),
            'late_f32': (w_o[l], w_ffn_in[l], w_ffn_out[l]),
        }
        ck = jnp.transpose(cache_k[:, l], (0, 2, 3, 1)).reshape(bd, ATT_WIDTH, past)
        cv = cache_v[:, l]
        h0 = jnp.stack([state_ssm_re[:, l], state_ssm_im[:, l]], axis=2)
        h0 = h0.transpose(3, 0, 1, 2, 4).reshape(N_GROUPS, bd, CW)
        xs, _, _ = _layer(xs, mods, (1, bd), lam_init, rope_tabs, ck, cv, h0, weights, prep)
        xp, (k_ctx, v_ctx), (st,) = _layer(xp, mods, (0, 1), lam_init, None, None, None, None,
                                           weights, prep)
        ks_out.append(jnp.swapaxes(k_ctx, 1, 2).reshape(bp, -1, 2 * N_HEADS, HEAD_DIM))
        vs_out.append(v_ctx)
        fin = st.reshape(N_GROUPS, bp, 2, 2, SSM_STATE).transpose(1, 2, 3, 0, 4)
        hr_out.append(fin[:, :, 0])
        hi_out.append(fin[:, :, 1])
    return (xp, xs, jnp.stack(ks_out, axis=1), jnp.stack(vs_out, axis=1),
            jnp.stack(hr_out, axis=1), jnp.stack(hi_out, axis=1))
```

```python
import functools
import math

import jax
import jax.numpy as jnp
import numpy as np
from jax import lax
from jax.experimental import pallas as pl
from jax.experimental.pallas import tpu as pltpu

F32 = jnp.float32
BF16 = jnp.bfloat16

D_MODEL = 1024
GRID_W = 64
ATT_WIDTH = 512
SSM_WIDTH = 512
HEAD_DIM = 64
N_HEADS = 4
HEAD_W = 2 * HEAD_DIM
SSM_GROUP = 16
N_GROUPS = 32
SSM_STATE = 64
D_FF = 2816
N_MOD = 6
ROPE_BASE = 10000.0
NORM_EPS = 1e-6

CHUNK = 16
CW = CHUNK * SSM_GROUP
SCAN_BLOCK = 8
SCAN_LEVELS = 3
AT_ROWS = 16
GROUP_BLOCK = 4
Z_BATCH = 8

ROWS_PER_TILE = 128
TOK_PER_TILE = ROWS_PER_TILE * CHUNK
SUB_TOK = 1024
TQ_ITEM = 256
N_SUB = TOK_PER_TILE // SUB_TOK

VMEM_LIMIT = 56 * 1024 * 1024

NT_DIMS = (((1,), (1,)), ((), ()))
TN_DIMS = (((0,), (0,)), ((), ()))


def _cparams(n_axes):
    return pltpu.CompilerParams(
        dimension_semantics=("arbitrary",) * n_axes,
        vmem_limit_bytes=VMEM_LIMIT)


def _rms(x, g):
    ms = jnp.mean(x * x, axis=-1, keepdims=True)
    return x * lax.rsqrt(ms + NORM_EPS) * g


def _silu(x):
    return x * jax.nn.sigmoid(x)


MOD_ROWS = 8


def _mod_kernel(ctx_ref, c_ref, w_ref, b_ref, o_ref):
    n_lat, tk = c_ref.shape
    row = lax.broadcasted_iota(jnp.int32, (MOD_ROWS, tk), 0)
    cond = jnp.where(row == 0, ctx_ref[...], 0.0)
    for b in range(n_lat):
        cond = jnp.where(row == 1 + b, c_ref[b:b + 1, :], cond)
    part = jnp.dot(_silu(cond).astype(BF16), w_ref[...].astype(BF16),
                   preferred_element_type=F32)

    @pl.when(pl.program_id(0) == 0)
    def _():
        o_ref[:, 0, :] = part + b_ref[...]

    @pl.when(pl.program_id(0) > 0)
    def _():
        o_ref[:, 0, :] += part


def _rope(x, cos, sa, sb):
    return (x * cos + pltpu.roll(x, HEAD_W - 16, axis=1) * sa
            + pltpu.roll(x, 16, axis=1) * sb)


def _in_proj_kernel(*refs, rope, seqs_per_sub):
    x_ref, x3_hbm, mod_ref, g_ref, w_ref = refs[:5]
    refs = refs[5:]
    if rope:
        cos_ref, sa_ref, sb_ref = refs[:3]
        refs = refs[3:]
        q_ref, k_ref, v_ref, ut_ref, wut_ref, xt_ref, xt_sem = refs
    else:
        q_ref, k_ref, v_ref, kc_ref, vc_ref, ut_ref, wut_ref, xt_ref, xt_sem = refs
    tile = pl.program_id(0)
    j = pl.program_id(1)
    t_per_sub = CHUNK // N_SUB

    def gather(tile_idx, t):
        slot = tile_idx % 2
        src = x3_hbm.at[pl.ds(tile_idx * ROWS_PER_TILE, ROWS_PER_TILE), t, :]
        return pltpu.make_async_copy(src, xt_ref.at[slot, t], xt_sem.at[slot, t])

    @pl.when(j == 0)
    def _():
        @pl.when(tile == 0)
        def _():
            for t in range(CHUNK):
                gather(0, t).start()

        @pl.when(tile + 1 < pl.num_programs(0))
        def _():
            for t in range(CHUNK):
                gather(tile + 1, t).start()

    t_base = j * t_per_sub
    for d in range(t_per_sub):
        gather(tile, t_base + d).wait()

    @pl.when((tile == 0) & (j == 0))
    def _():
        wut_ref[...] = w_ref[:, 3 * ATT_WIDTH:].T

    shift = mod_ref[:, 0:D_MODEL]
    gain = g_ref[...] * (1.0 + mod_ref[:, D_MODEL:2 * D_MODEL])

    def norm_mod(xv):
        ms = jnp.mean(xv * xv, axis=-1, keepdims=True)
        return (xv * lax.rsqrt(ms + NORM_EPS) * gain + shift).astype(BF16)

    def ssm_input(d0):
        xt = jnp.concatenate([xt_ref[tile % 2, t_base + d0],
                              xt_ref[tile % 2, t_base + d0 + 1]], axis=0)
        ut = lax.dot_general(wut_ref[...], norm_mod(xt), NT_DIMS,
                             preferred_element_type=F32)
        for d in range(2):
            blk = ut[:, d * ROWS_PER_TILE:(d + 1) * ROWS_PER_TILE]
            row0 = pl.multiple_of((t_base + d0 + d) * SSM_GROUP, SSM_GROUP)
            ut_ref[:, pl.ds(row0, SSM_GROUP), :] = (
                blk.reshape(N_GROUPS, SSM_GROUP, ROWS_PER_TILE).astype(ut_ref.dtype))

    proj = jnp.dot(norm_mod(x_ref[...]), w_ref[:, 0:3 * ATT_WIDTH],
                   preferred_element_type=F32)
    q = proj[:, 0:ATT_WIDTH]
    k = proj[:, ATT_WIDTH:2 * ATT_WIDTH]
    v = proj[:, 2 * ATT_WIDTH:3 * ATT_WIDTH]
    qscale = HEAD_DIM ** -0.5 * math.log2(math.e)
    if rope:
        cos, sa, sb = cos_ref[...], sa_ref[...], sb_ref[...]
        for hd in range(N_HEADS):
            sl = slice(hd * HEAD_W, (hd + 1) * HEAD_W)
            q_ref[:, sl] = (_rope(q[:, sl], cos, sa, sb) * qscale).astype(q_ref.dtype)
            k_ref[:, sl] = _rope(k[:, sl], cos, sa, sb).astype(k_ref.dtype)
    else:
        q_ref[...] = (q * qscale).astype(q_ref.dtype)
        k_ref[...] = k.astype(k_ref.dtype)
        seq = SUB_TOK // seqs_per_sub
        k_t = k.T
        for b in range(seqs_per_sub):
            kc_ref[b] = k_t[:, b * seq:(b + 1) * seq]
            for hd in range(N_HEADS):
                vc_ref[b, :, hd, :] = v[b * seq:(b + 1) * seq, hd * HEAD_W:(hd + 1) * HEAD_W]
    v_ref[...] = v.astype(v_ref.dtype)

    for d0 in range(0, t_per_sub, 2):
        ssm_input(d0)


def _in_proj(x, mods, mod_rows, g0, w_in, rope_tabs):
    n_batch, seq_len = x.shape[:2]
    n_tok = n_batch * seq_len
    n_rows = n_tok // CHUNK
    n_tiles = n_tok // TOK_PER_TILE
    mod_row0, n_mod = mod_rows
    tiles_per_mod = n_tiles // n_mod
    rope = rope_tabs is not None
    seqs_per_sub = max(1, SUB_TOK // seq_len)
    in_specs = [pl.BlockSpec((SUB_TOK, D_MODEL), lambda i, j: (i * N_SUB + j, 0)),
                pl.BlockSpec(memory_space=pl.ANY),
                pl.BlockSpec((None, 1, 2 * D_MODEL),
                             lambda i, j: (mod_row0 + i // tiles_per_mod, 0, 0)),
                pl.BlockSpec((1, D_MODEL), lambda i, j: (0, 0)),
                pl.BlockSpec((D_MODEL, 4 * ATT_WIDTH), lambda i, j: (0, 0))]
    args = [x.reshape(n_tok, D_MODEL), x.reshape(n_rows, CHUNK, D_MODEL), mods, g0, w_in]
    row_spec = pl.BlockSpec((SUB_TOK, ATT_WIDTH), lambda i, j: (i * N_SUB + j, 0))
    row_shape = jax.ShapeDtypeStruct((n_tok, ATT_WIDTH), BF16)
    out_specs = [row_spec, row_spec, row_spec]
    out_shape = [row_shape, row_shape, row_shape]
    if rope:
        assert seq_len == TOK_PER_TILE
        for tab in rope_tabs:
            in_specs.append(pl.BlockSpec((SUB_TOK, HEAD_W), lambda i, j: (j, 0)))
            args.append(tab)
    else:
        out_specs += [pl.BlockSpec((seqs_per_sub, ATT_WIDTH, seq_len),
                                   lambda i, j: (i * N_SUB + j, 0, 0)),
                      pl.BlockSpec((seqs_per_sub, seq_len, N_HEADS, HEAD_W),
                                   lambda i, j: (i * N_SUB + j, 0, 0, 0))]
        out_shape += [jax.ShapeDtypeStruct((n_batch, ATT_WIDTH, seq_len), F32),
                      jax.ShapeDtypeStruct((n_batch, seq_len, N_HEADS, HEAD_W), F32)]
    out_specs.append(pl.BlockSpec((N_GROUPS, CW, ROWS_PER_TILE), lambda i, j: (0, 0, i)))
    out_shape.append(jax.ShapeDtypeStruct((N_GROUPS, CW, n_rows), BF16))
    return pl.pallas_call(
        functools.partial(_in_proj_kernel, rope=rope, seqs_per_sub=seqs_per_sub),
        grid=(n_tiles, N_SUB),
        in_specs=in_specs,
        out_specs=out_specs,
        out_shape=out_shape,
        scratch_shapes=[pltpu.VMEM((SSM_WIDTH, D_MODEL), BF16),
                        pltpu.VMEM((2, CHUNK, ROWS_PER_TILE, D_MODEL), F32),
                        pltpu.SemaphoreType.DMA((2, CHUNK))],
        compiler_params=_cparams(2),
        name="in_proj",
    )(*args)


def _attn_kernel(*refs, has_ctx, lam_init, n_seq, seq_len, tq, n_cast):
    if n_cast:
        cast_in = refs[len(refs) - 2 * n_cast - 1:len(refs) - n_cast - 1]
        cast_out = refs[len(refs) - n_cast:]
        refs = refs[:len(refs) - 2 * n_cast - 1] + (refs[len(refs) - n_cast - 1],)
        for src, dst in zip(cast_in, cast_out):
            dst[...] = src[...].astype(dst.dtype)
    if has_ctx:
        lam_ref, sg_ref, q_ref, ck_ref, cv_ref, k_ref, v_ref, o_ref = refs
    else:
        lam_ref, sg_ref, q_ref, k_ref, v_ref, o_ref = refs
    lp = lam_ref[...]
    lam = (jnp.exp(jnp.sum(lp[0:1] * lp[1:2], axis=-1, keepdims=True))
           - jnp.exp(jnp.sum(lp[2:3] * lp[3:4], axis=-1, keepdims=True)) + lam_init)
    first_map = lax.broadcasted_iota(jnp.int32, (1, HEAD_W), 1) < HEAD_DIM
    ti = min(TQ_ITEM, tq)
    for row0 in range(0, n_seq * tq, ti):
        b = row0 // tq
        q_rows = slice(row0, row0 + ti)
        for hd in range(N_HEADS):
            sl = slice(hd * HEAD_W, (hd + 1) * HEAD_W)
            qh = q_ref[q_rows, sl]
            zero = jnp.zeros_like(qh)
            qs = jnp.concatenate([jnp.where(first_map, qh, zero),
                                  jnp.where(first_map, zero, qh)], axis=0)
            kv_rows = slice(b * seq_len, (b + 1) * seq_len)
            parts = [(k_ref[kv_rows, sl], v_ref[kv_rows, sl])]
            scores = [lax.dot_general(qs, parts[0][0], NT_DIMS, preferred_element_type=F32)]
            if has_ctx:
                parts.insert(0, (None, cv_ref[:, hd, :].astype(BF16)))
                scores.insert(0, jnp.dot(qs, ck_ref[sl, :].astype(BF16),
                                         preferred_element_type=F32))
            mx = scores[0].max(axis=-1, keepdims=True)
            for s in scores[1:]:
                mx = jnp.maximum(mx, s.max(axis=-1, keepdims=True))
            acc = None
            for s, (_, vv) in zip(scores, parts):
                e = jnp.exp2(s - mx).astype(BF16)
                v_one = jnp.concatenate([vv, jnp.ones_like(vv)], axis=1)
                pv = jnp.dot(e, v_one, preferred_element_type=F32)
                acc = pv if acc is None else acc + pv
            num = acc[:, 0:HEAD_W] / acc[:, HEAD_W:2 * HEAD_W]
            o = num[0:ti] - lam * num[ti:2 * ti]
            o = _rms(o, sg_ref[...]) * (1.0 - lam_init)
            o_ref[q_rows, sl] = o.astype(o_ref.dtype)


def _attention(q, k, v, ctx_k, ctx_v, lam_params, subln_g, n_batch, seq_len, lam_init,
               cast_weights=()):
    has_ctx = ctx_k is not None
    tq = min(1024, seq_len)
    n_q = seq_len // tq
    n_seq = 1 if n_q > 1 else min(4, n_batch)
    in_specs = [pl.BlockSpec((4, HEAD_DIM), lambda b, i: (0, 0)),
                pl.BlockSpec((1, HEAD_W), lambda b, i: (0, 0)),
                pl.BlockSpec((n_seq * tq, ATT_WIDTH), lambda b, i: (b * n_q + i, 0))]
    args = [lam_params, subln_g.reshape(1, HEAD_W), q]
    if has_ctx:
        past = ctx_v.shape[1]
        in_specs +=[pl.BlockSpec((None, ATT_WIDTH, past), lambda b, i: (b, 0, 0)),
                     pl.BlockSpec((None, past, N_HEADS, HEAD_W), lambda b, i: (b, 0, 0, 0))]
        args += [ctx_k, ctx_v]
    kv_spec = pl.BlockSpec((n_seq * seq_len, ATT_WIDTH), lambda b, i: (b, 0))
    in_specs += [kv_spec, kv_spec]
    args += [k, v]
    out_specs = [pl.BlockSpec((n_seq * tq, ATT_WIDTH), lambda b, i: (b * n_q + i, 0))]
    out_shape = [jax.ShapeDtypeStruct((n_batch * seq_len, ATT_WIDTH), BF16)]
    n_steps = (n_batch // n_seq) * n_q
    for w in cast_weights:
        rows = w.shape[0] // n_steps
        spec = pl.BlockSpec((rows, w.shape[1]), lambda b, i: (b * n_q + i, 0))
        in_specs.append(spec)
        args.append(w)
        out_specs.append(spec)
        out_shape.append(jax.ShapeDtypeStruct(w.shape, BF16))
    outs = pl.pallas_call(
        functools.partial(_attn_kernel, has_ctx=has_ctx, lam_init=lam_init,
                          n_seq=n_seq, seq_len=seq_len, tq=tq, n_cast=len(cast_weights)),
        grid=(n_batch // n_seq, n_q),
        in_specs=in_specs,
        out_specs=out_specs,
        out_shape=out_shape,
        compiler_params=_cparams(2),
        name="diff_attention",
    )(*args)
    return outs[0], tuple(outs[1:])


def _cmul(ar, ai, br, bi):
    return ar * br - ai * bi, ar * bi + ai * br


def _ssm_prep_kernel(lre_ref, lim_ref, ls_ref, bre_ref, bim_ref, cre_ref, cim_ref, d_ref,
                     mt_ref, gt_ref, wo_ref, at_ref):
    lane = lax.broadcasted_iota(jnp.int32, (SSM_GROUP, CW), 1)
    chan = lax.broadcasted_iota(jnp.int32, (SSM_GROUP, CW), 0)
    for gi in range(GROUP_BLOCK):
        gt_cols, wo_cols, at_cols, toeplitz = [], [], [], []
        for dr in range(2):
            lr = jnp.minimum(lre_ref[dr, gi], -1e-4)
            li = lim_ref[dr, gi]
            step = jnp.exp(ls_ref[dr, gi])
            mag = jnp.exp(lr * step)
            a_re = mag * jnp.cos(li * step)
            a_im = mag * jnp.sin(li * step)
            den = lr * lr + li * li
            nr = a_re - 1.0
            f_re = (nr * lr + a_im * li) / den
            f_im = (a_im * lr - nr * li) / den
            bt_re, bt_im = bre_ref[dr, gi], bim_ref[dr, gi]
            bb_re, bb_im = _cmul(f_re, f_im, bt_re, bt_im)
            c_re, c_im = cre_ref[dr, gi], cim_ref[dr, gi]
            pw = [(jnp.ones_like(a_re), jnp.zeros_like(a_im))]
            for _ in range(CHUNK):
                pw.append(_cmul(pw[-1][0], pw[-1][1], a_re, a_im))
            g_re, g_im, e_re, e_im = [], [], [], []
            for t in range(CHUNK):
                pr, pi = pw[CHUNK - 1 - t] if dr == 0 else pw[t]
                r, i = _cmul(bb_re, bb_im, pr, pi)
                g_re.append(r)
                g_im.append(i)
                pr, pi = pw[t + 1] if dr == 0 else pw[CHUNK - t]
                r, i = _cmul(c_re, c_im, pr, pi)
                e_re.append(r)
                e_im.append(-i)
            g_cat = jnp.concatenate([jnp.concatenate(g_re, axis=0),
                                     jnp.concatenate(g_im, axis=0)], axis=1)
            gt_cols.append(g_cat)
            wo_cols.append(jnp.concatenate([jnp.concatenate(e_re, axis=0),
                                            jnp.concatenate(e_im, axis=0)], axis=1))
            c_cat = jnp.concatenate([c_re, -c_im], axis=1)
            toeplitz.append(lax.dot_general(c_cat, g_cat, NT_DIMS,
                                            precision=lax.Precision.HIGHEST,
                                            preferred_element_type=F32))
            apw = [pw[CHUNK]]
            for _ in range(SCAN_BLOCK - 1):
                apw.append(_cmul(apw[-1][0], apw[-1][1], apw[0][0], apw[0][1]))
            order = list(range(SCAN_BLOCK)) if dr == 0 else list(range(SCAN_BLOCK - 1, -1, -1))
            order += [2 ** l - 1 for l in range(SCAN_LEVELS)]
            order += [0] * (AT_ROWS - len(order))
            at_cols += [jnp.concatenate([jnp.concatenate([apw[i][0], apw[i][0]], axis=1)
                                         for i in order], axis=0),
                        jnp.concatenate([jnp.concatenate([-apw[i][1], apw[i][1]], axis=1)
                                         for i in order], axis=0)]
        kf_rev, kb = toeplitz
        d_skip = d_ref[gi]
        blocks = []
        for t in range(CHUNK):
            fwd = pltpu.roll(kf_rev, (CW - (CHUNK - 1 - t) * SSM_GROUP) % CW, axis=1)
            bwd = pltpu.roll(kb, t * SSM_GROUP, axis=1)
            blocks.append(jnp.where(lane < (t + 1) * SSM_GROUP, fwd, 0.0)
                          + jnp.where(lane >= t * SSM_GROUP, bwd, 0.0)
                          + jnp.where(lane == chan + t * SSM_GROUP, d_skip, 0.0))
        mt_ref[gi] = jnp.concatenate(blocks, axis=0).astype(mt_ref.dtype)
        gt_ref[gi] = jnp.concatenate(gt_cols, axis=1).astype(gt_ref.dtype)
        wo_ref[gi] = jnp.concatenate(wo_cols, axis=1).astype(wo_ref.dtype)
        at_ref[gi] = jnp.concatenate(at_cols, axis=1)


N_PREP_IN, N_PREP_OUT, N_MOD_IN = 8, 4, 4


def _prep_mod_kernel(*refs):
    prep_in = refs[:N_PREP_IN]
    mod_in = refs[N_PREP_IN:N_PREP_IN + N_MOD_IN]
    w_in_ref = refs[N_PREP_IN + N_MOD_IN]
    outs = refs[N_PREP_IN + N_MOD_IN + 1:]
    _ssm_prep_kernel(*prep_in, *outs[:N_PREP_OUT])
    _mod_kernel(*mod_in, outs[N_PREP_OUT])
    outs[N_PREP_OUT + 1][...] = w_in_ref[...].astype(BF16)


def _ssm_prep_and_modulation(lam_re, lam_im, log_step, b_re, b_im, c_re, c_im, d_skip,
                             c_ctx, c, w_mod, b_mod, w_in):
    row = lambda a: a.reshape(2, N_GROUPS, 1, SSM_STATE)
    bt = lambda a: jnp.swapaxes(a, 2, 3)
    d_row = jnp.tile((d_skip[0] + d_skip[1]).reshape(N_GROUPS, 1, SSM_GROUP), (1, 1, CHUNK))
    gb = GROUP_BLOCK
    n_steps = N_GROUPS // gb
    vec_spec = pl.BlockSpec((2, gb, 1, SSM_STATE), lambda i: (0, i, 0, 0))
    mat_spec = pl.BlockSpec((2, gb, SSM_GROUP, SSM_STATE), lambda i: (0, i, 0, 0))
    w_spec = pl.BlockSpec((gb, CW, CW), lambda i: (i, 0, 0))
    w_shape = jax.ShapeDtypeStruct((N_GROUPS, CW, CW), BF16)
    n_mod = w_mod.shape[1]
    tk = D_MODEL // n_steps
    assert 1 + c.shape[0] <= MOD_ROWS
    outs = pl.pallas_call(
        _prep_mod_kernel,
        grid=(n_steps,),
        in_specs=[vec_spec, vec_spec,
                  pl.BlockSpec((2, gb, 1, 1), lambda i: (0, i, 0, 0)),
                  mat_spec, mat_spec, mat_spec, mat_spec,
                  pl.BlockSpec((gb, 1, CW), lambda i: (i, 0, 0)),
                  pl.BlockSpec((1, tk), lambda k: (0, k)),
                  pl.BlockSpec((c.shape[0], tk), lambda k: (0, k)),
                  pl.BlockSpec((tk, n_mod), lambda k: (k, 0)),
                  pl.BlockSpec((1, n_mod), lambda k: (0, 0)),
                  pl.BlockSpec((tk, w_in.shape[1]), lambda k: (k, 0))],
        out_specs=[w_spec, w_spec, w_spec,
                   pl.BlockSpec((gb, AT_ROWS, 4 * 2 * SSM_STATE), lambda i: (i, 0, 0)),
                   pl.BlockSpec((MOD_ROWS, 1, n_mod), lambda k: (0, 0, 0)),
                   pl.BlockSpec((tk, w_in.shape[1]), lambda k: (k, 0))],
        out_shape=[w_shape, w_shape, w_shape,
                   jax.ShapeDtypeStruct((N_GROUPS, AT_ROWS, 4 * 2 * SSM_STATE), F32),
                   jax.ShapeDtypeStruct((MOD_ROWS, 1, n_mod), F32),
                   jax.ShapeDtypeStruct(w_in.shape, BF16)],
        compiler_params=_cparams(1),
        name="ssm_prep_modulation",
    )(row(lam_re), row(lam_im), log_step.reshape(2, N_GROUPS, 1, 1),
      bt(b_re), bt(b_im), c_re, c_im, d_row,
      c_ctx.reshape(1, D_MODEL), c, w_mod, b_mod.reshape(1, n_mod), w_in)
    return outs[:N_PREP_OUT], outs[N_PREP_OUT], outs[N_PREP_OUT + 1]


def _shift_rows(x, m, down):
    n = x.shape[0]
    return pltpu.roll(x, m if down else n - m, axis=0)


def _ssm_kernel(*refs, n_seq, has_h0):
    if has_h0:
        (xt_ref, mt_ref, gt_ref, wo_ref, at_ref, wg_ref, bg_ref, h0_ref,
         z_hbm, zs_ref, zb_ref, zb_sem) = refs
    else:
        (xt_ref, mt_ref, gt_ref, wo_ref, at_ref, wg_ref, bg_ref,
         z_hbm, st_ref, zs_ref, zb_ref, zb_sem, fin_ref) = refs
    step = pl.program_id(0)
    n_rows = xt_ref.shape[-1]
    seg = n_rows // n_seq
    cw2 = 2 * SSM_STATE
    assert seg % SCAN_BLOCK == 0
    n_blk = n_rows // SCAN_BLOCK
    blk_per_seq = seg // SCAN_BLOCK
    pos = lax.broadcasted_iota(jnp.int32, (n_rows, cw2), 0) % seg
    row_blk = lax.broadcasted_iota(jnp.int32, (SCAN_BLOCK, cw2), 0)

    def low_half(shape):
        return lax.broadcasted_iota(jnp.int32, shape, 1) < SSM_STATE

    def swap(v):
        return pltpu.roll(v, SSM_STATE, axis=1)

    def to_planes(va, vb):
        lo = low_half((va.shape[0], cw2))
        va_l = pltpu.roll(va, 3 * SSM_STATE, axis=1)
        vb_r = pltpu.roll(vb, SSM_STATE, axis=1)
        return (jnp.where(lo, va[:, 0:cw2], vb_r[:, 0:cw2]),
                jnp.where(lo, va_l[:, 0:cw2], vb[:, 0:cw2]),
                jnp.where(lo, va[:, cw2:2 * cw2], vb_r[:, cw2:2 * cw2]),
                jnp.where(lo, va_l[:, cw2:2 * cw2], vb[:, cw2:2 * cw2]))

    def from_planes(f_re, f_im, b_re, b_im):
        lo = low_half(f_re.shape)
        va = jnp.concatenate([jnp.where(lo, f_re, swap(f_im)), jnp.where(lo, b_re, swap(b_im))], axis=1)
        vb = jnp.concatenate([jnp.where(lo, swap(f_re), f_im), jnp.where(lo, swap(b_re), b_im)], axis=1)
        return va, vb

    for ga in range(0, GROUP_BLOCK, 2):
        gb = ga + 1
        xts = [xt_ref[ga], xt_ref[gb]]
        s_pair = [lax.dot_general(xts[i], gt_ref[g], TN_DIMS, preferred_element_type=F32)
                  for i, g in enumerate((ga, gb))]
        planes = to_planes(*s_pair)
        lo_t = low_half((AT_ROWS, cw2))
        if has_h0:
            h0_planes = to_planes(h0_ref[ga], h0_ref[gb])
        ent_planes = []
        for dr in range(2):
            down = dr == 0
            pa, pb = (at_ref[g][:, (2 * dr) * cw2:(2 * dr + 1) * cw2] for g in (ga, gb))
            qa, qb = (at_ref[g][:, (2 * dr + 1) * cw2:(2 * dr + 2) * cw2] for g in (ga, gb))
            ar_tab = jnp.where(lo_t, pa, pb)
            ai_tab = jnp.where(lo_t, -qa, qb)
            re3 = planes[2 * dr].reshape(n_blk, SCAN_BLOCK, cw2)
            im3 = planes[2 * dr + 1].reshape(n_blk, SCAN_BLOCK, cw2)
            for lvl in range(SCAN_LEVELS):
                m = 2 ** lvl
                valid = (row_blk >= m) if down else (row_blk < SCAN_BLOCK - m)
                row = SCAN_BLOCK + lvl
                ar = jnp.where(valid, ar_tab[row:row + 1], 0.0)
                ai = jnp.where(valid, ai_tab[row:row + 1], 0.0)
                shift = m if down else SCAN_BLOCK - m
                sh_re = pltpu.roll(re3, shift, axis=1)
                sh_im = pltpu.roll(im3, shift, axis=1)
                re3, im3 = re3 + ar * sh_re - ai * sh_im, im3 + ar * sh_im + ai * sh_re
            ar_blk, ai_blk = ar_tab[0:SCAN_BLOCK], ai_tab[0:SCAN_BLOCK]
            blk_re = [re3[i] for i in range(n_blk)]
            blk_im = [im3[i] for i in range(n_blk)]
            edge = slice(SCAN_BLOCK - 1, SCAN_BLOCK) if down else slice(0, 1)
            for q_i in range(n_seq):
                idxs = list(range(q_i * blk_per_seq, (q_i + 1) * blk_per_seq))
                idxs = idxs if down else idxs[::-1]
                for prev, cur in zip([None] + idxs[:-1], idxs):
                    if prev is not None:
                        c_re, c_im = blk_re[prev][edge], blk_im[prev][edge]
                    elif has_h0:
                        c_re = h0_planes[2 * dr][q_i:q_i + 1]
                        c_im = h0_planes[2 * dr + 1][q_i:q_i + 1]
                    else:
                        continue
                    c_re = jnp.broadcast_to(c_re, (SCAN_BLOCK, cw2))
                    c_im = jnp.broadcast_to(c_im, (SCAN_BLOCK, cw2))
                    blk_re[cur] = blk_re[cur] + ar_blk * c_re - ai_blk * c_im
                    blk_im[cur] = blk_im[cur] + ar_blk * c_im + ai_blk * c_re
            for part, blks in ((0, blk_re), (1, blk_im)):
                s = jnp.concatenate(blks, axis=0)
                if not has_h0:
                    fin_ref[part] = s
                ent = _shift_rows(s, 1, down)
                ent = jnp.where((pos >= 1) if down else (pos < seg - 1), ent, 0.0)
                if has_h0:
                    ent_blk = [ent[i * SCAN_BLOCK:(i + 1) * SCAN_BLOCK] for i in range(n_blk)]
                    at_edge = row_blk == (0 if down else SCAN_BLOCK - 1)
                    for q_i in range(n_seq):
                        bi = q_i * blk_per_seq if down else (q_i + 1) * blk_per_seq - 1
                        ent_blk[bi] = jnp.where(
                            at_edge, h0_planes[2 * dr + part][q_i:q_i + 1], ent_blk[bi])
                    ent = jnp.concatenate(ent_blk, axis=0)
                ent_planes.append(ent)
            if not has_h0:
                rows = pl.ds(seg - 1 if down else 0, n_seq, stride=seg)
                f_re, f_im = fin_ref[0, rows, :], fin_ref[1, rows, :]
                lo_s = low_half((n_seq, cw2))
                st_ref[ga, :, dr * cw2:(dr + 1) * cw2] = jnp.where(lo_s, f_re, swap(f_im))
                st_ref[gb, :, dr * cw2:(dr + 1) * cw2] = jnp.where(lo_s, swap(f_re), f_im)
        h_pair = from_planes(*ent_planes)
        for i, g in enumerate((ga, gb)):
            yt = (jnp.dot(mt_ref[g], xts[i], preferred_element_type=F32)
                  + lax.dot_general(wo_ref[g], h_pair[i].astype(BF16), NT_DIMS,
                                    preferred_element_type=F32))
            z = jax.nn.gelu(yt, approximate=True)
            grp = step * GROUP_BLOCK + g
            for t in range(CHUNK):
                zs_ref[t, pl.ds(pl.multiple_of(grp * SSM_GROUP, SSM_GROUP), SSM_GROUP), :] = (
                    z[t * SSM_GROUP:(t + 1) * SSM_GROUP, :])

    @pl.when(step == pl.num_programs(0) - 1)
    def _():
        def put(t):
            return pltpu.make_async_copy(zb_ref.at[t], z_hbm.at[:, t, :], zb_sem.at[t])

        for t0 in range(0, CHUNK, Z_BATCH):
            for t in range(t0, t0 + Z_BATCH):
                zt = zs_ref[t]
                gate = jnp.dot(wg_ref[...], zt.astype(BF16),
                               preferred_element_type=F32) + bg_ref[...]
                zb_ref[t] = (zt * jax.nn.sigmoid(gate)).T
            for t in range(t0, t0 + Z_BATCH):
                put(t).start()
        for t in range(CHUNK):
            put(t).wait()


def _ssm(xt, mt, gt, wo, at, w_glu_t, b_glu_col, h0, n_seq):
    n_rows = xt.shape[-1]
    has_h0 = h0 is not None
    gb = GROUP_BLOCK
    w_spec = pl.BlockSpec((gb, CW, CW), lambda i: (i, 0, 0))
    in_specs = [pl.BlockSpec((gb, CW, n_rows), lambda i: (i, 0, 0)),
                w_spec, w_spec, w_spec,
                pl.BlockSpec((gb, AT_ROWS, 4 * 2 * SSM_STATE), lambda i: (i, 0, 0)),
                pl.BlockSpec((SSM_WIDTH, SSM_WIDTH), lambda i: (0, 0)),
                pl.BlockSpec((SSM_WIDTH, 1), lambda i: (0, 0))]
    args = [xt, mt, gt, wo, at, w_glu_t, b_glu_col]
    out_specs = [pl.BlockSpec(memory_space=pl.ANY)]
    out_shape = [jax.ShapeDtypeStruct((n_rows, CHUNK, SSM_WIDTH), F32)]
    scratch = [pltpu.VMEM((CHUNK, SSM_WIDTH, n_rows), F32),
               pltpu.VMEM((CHUNK, n_rows, SSM_WIDTH), F32),
               pltpu.SemaphoreType.DMA((CHUNK,))]
    if has_h0:
        in_specs.append(pl.BlockSpec((gb, n_seq, CW), lambda i: (i, 0, 0)))
        args.append(h0)
    else:
        out_specs.append(pl.BlockSpec((gb, n_seq, CW), lambda i: (i, 0, 0)))
        out_shape.append(jax.ShapeDtypeStruct((N_GROUPS, n_seq, CW), F32))
        scratch.append(pltpu.VMEM((2, n_rows, 2 * SSM_STATE), F32))
    return pl.pallas_call(
        functools.partial(_ssm_kernel, n_seq=n_seq, has_h0=has_h0),
        grid=(N_GROUPS // gb,),
        in_specs=in_specs,
        out_specs=out_specs,
        out_shape=out_shape,
        scratch_shapes=scratch,
        compiler_params=_cparams(1),
        name="ssm_scan_glu",
    )(*args)


FF_CHUNK = 256
FFN_TOK = 1024


def _out_ffn_kernel(x_ref, attn_ref, z_ref, mod_ref, g_ref, wo_ref, wfi_ref, wfo_ref,
                    o_ref, act_ref):
    gate1 = mod_ref[:, 2 * D_MODEL:3 * D_MODEL]
    shift2 = mod_ref[:, 3 * D_MODEL:4 * D_MODEL]
    scale2 = mod_ref[:, 4 * D_MODEL:5 * D_MODEL]
    gate2 = mod_ref[:, 5 * D_MODEL:6 * D_MODEL]
    half = FFN_TOK // 2
    gain2 = g_ref[2:3, :] * (1.0 + scale2)

    def pre_ffn(hf):
        r = slice(hf * half, (hf + 1) * half)
        mixer = jnp.concatenate([attn_ref[r, :], z_ref[r, :].astype(BF16)], axis=1)
        mix = jnp.dot(mixer, wo_ref[...], preferred_element_type=F32)
        x1 = x_ref[r, :] + gate1 * _rms(mix, g_ref[1:2, :])
        ms = jnp.mean(x1 * x1, axis=-1, keepdims=True)
        return x1, (x1 * lax.rsqrt(ms + NORM_EPS) * gain2 + shift2).astype(BF16)

    def ffn_in(hf, h, chunks):
        r = slice(hf * half, (hf + 1) * half)
        for c in chunks:
            lo = c * FF_CHUNK
            gt = jnp.dot(h, wfi_ref[:, lo:lo + FF_CHUNK], preferred_element_type=F32)
            up = jnp.dot(h, wfi_ref[:, D_FF + lo:D_FF + lo + FF_CHUNK],
                         preferred_element_type=F32)
            act_ref[r, lo:lo + FF_CHUNK] = (_silu(gt) * up).astype(BF16)

    def ffn_out(hf, x1):
        r = slice(hf * half, (hf + 1) * half)
        f = jnp.dot(act_ref[r, :], wfo_ref[...], preferred_element_type=F32)
        o_ref[r, :] = x1 + gate2 * _rms(f, g_ref[3:4, :])

    n_chunks = D_FF // FF_CHUNK
    x1_a, h_a = pre_ffn(0)
    x1_b, h_b = pre_ffn(1)
    ffn_in(0, h_a, range(n_chunks))
    ffn_out(0, x1_a)
    ffn_in(1, h_b, range(n_chunks))
    ffn_out(1, x1_b)


def _out_ffn(x2d, attn, z, mods, mod_rows, norm_g, w_o, w_ffn_in, w_ffn_out):
    n_tok = x2d.shape[0]
    z2d = z.reshape(n_tok, SSM_WIDTH)
    n_steps = n_tok // FFN_TOK
    mod_row0, n_mod = mod_rows
    steps_per_mod = n_steps // n_mod
    const = lambda i: (0, 0)
    row_spec = lambda w: pl.BlockSpec((FFN_TOK, w), lambda i: (i, 0))
    return pl.pallas_call(
        _out_ffn_kernel,
        grid=(n_steps,),
        in_specs=[row_spec(D_MODEL), row_spec(ATT_WIDTH), row_spec(SSM_WIDTH),
                  pl.BlockSpec((None, 1, N_MOD * D_MODEL),
                               lambda i: (mod_row0 + i // steps_per_mod, 0, 0)),
                  pl.BlockSpec((4, D_MODEL), const),
                  pl.BlockSpec((2 * ATT_WIDTH, D_MODEL), const, pipeline_mode=pl.Buffered(1)),
                  pl.BlockSpec((D_MODEL, 2 * D_FF), const, pipeline_mode=pl.Buffered(1)),
                  pl.BlockSpec((D_FF, D_MODEL), const, pipeline_mode=pl.Buffered(1))],
        out_specs=row_spec(D_MODEL),
        out_shape=jax.ShapeDtypeStruct((n_tok, D_MODEL), F32),
        scratch_shapes=[pltpu.VMEM((FFN_TOK, D_FF), BF16)],
        compiler_params=_cparams(1),
        name="out_proj_ffn",
    )(x2d, attn, z2d, mods, norm_g, w_o, w_ffn_in, w_ffn_out)


def _rope_tables(seq_len):
    t = np.arange(seq_len)
    row = (t // GRID_W).astype(np.float32)
    col = (t % GRID_W).astype(np.float32)
    half = HEAD_DIM // 2
    inv_freq = (np.float32(ROPE_BASE)
                ** (-np.arange(0, half, 2, dtype=np.float32) / np.float32(half))).astype(np.float32)
    ang_r = row[:, None] * inv_freq
    ang_c = col[:, None] * inv_freq
    ang = np.concatenate([ang_r, ang_r, ang_c, ang_c], axis=-1)
    cos, sin = np.cos(ang), np.sin(ang)
    upper = (np.arange(HEAD_DIM) % 32) < 16
    sa = np.where(upper, -sin, 0.0)
    sb = np.where(upper, 0.0, sin)
    two = lambda a: jnp.asarray(np.concatenate([a, a], axis=-1), dtype=F32)
    return two(cos), two(sa), two(sb)


def _layer(x, mods, mod_rows, lam_init, rope_tabs, ctx_k, ctx_v, h0, weights, prep):
    n_batch, seq_len = x.shape[:2]
    g = weights['norm_g']
    outs = _in_proj(x, mods, mod_rows, g[0:1], weights['w_in'], rope_tabs)
    q, k, v = outs[:3]
    pending = () if 'late_bf16' in weights else weights['late_f32']
    attn, cast = _attention(q, k, v, ctx_k, ctx_v, weights['lam'], weights['subln_g'],
                            n_batch, seq_len, lam_init, cast_weights=pending)
    if pending:
        weights['late_bf16'] = cast
    ssm_out = _ssm(outs[-1], *prep, weights['w_glu_t'], weights['b_glu_col'], h0, n_batch)
    y = _out_ffn(x.reshape(n_batch * seq_len, D_MODEL), attn, ssm_out[0], mods, mod_rows, g,
                 *weights['late_bf16'])
    return y.reshape(x.shape), outs[3:-1], ssm_out[1:]


def kernel(x_prompt, x_sample, cache_k, cache_v, state_ssm_re, state_ssm_im, c, c_ctx, w_mod, b_mod, norm_g, w_in, lam_params, subln_g, ssm_lambda_re, ssm_lambda_im, ssm_log_step, ssm_b_re, ssm_b_im, ssm_c_re, ssm_c_im, ssm_d, w_glu, b_glu, w_o, w_ffn_in, w_ffn_out):
    depth = w_mod.shape[0]
    assert depth == 1
    bp = x_prompt.shape[0]
    bd, ld_len = x_sample.shape[:2]
    past = cache_k.shape[2]
    xp, xs = x_prompt, x_sample
    rope_tabs = _rope_tables(ld_len)
    ks_out, vs_out, hr_out, hi_out = [], [], [], []
    for l in range(depth):
        lam_init = 0.8 - 0.6 * math.exp(-0.3 * l)
        prep, mods, w_in_bf16 = _ssm_prep_and_modulation(
            ssm_lambda_re[l], ssm_lambda_im[l], ssm_log_step[l], ssm_b_re[l], ssm_b_im[l],
            ssm_c_re[l], ssm_c_im[l], ssm_d[l], c_ctx, c, w_mod[l], b_mod[l], w_in[l])
        weights = {
            'norm_g': norm_g[l],
            'w_in': w_in_bf16,
            'lam': lam_params[l], 'subln_g': subln_g[l],
            'w_glu_t': w_glu[l].T.astype(BF16), 'b_glu_col': b_glu[l].reshape(SSM_WIDTH, 1),
            'late_f32': (w_o[l], w_ffn_in[l], w_ffn_out[l]),
        }
        ck = jnp.transpose(cache_k[:, l], (0, 2, 3, 1)).reshape(bd, ATT_WIDTH, past)
        cv = cache_v[:, l]
        h0 = jnp.stack([state_ssm_re[:, l], state_ssm_im[:, l]], axis=2)
        h0 = h0.transpose(3, 0, 1, 2, 4).reshape(N_GROUPS, bd, CW)
        xs, _, _ = _layer(xs, mods, (1, bd), lam_init, rope_tabs, ck, cv, h0, weights, prep)
        xp, (k_ctx, v_ctx), (st,) = _layer(xp, mods, (0, 1), lam_init, None, None, None, None,
                                           weights, prep)
        ks_out.append(jnp.swapaxes(k_ctx, 1, 2).reshape(bp, -1, 2 * N_HEADS, HEAD_DIM))
        vs_out.append(v_ctx)
        fin = st.reshape(N_GROUPS, bp, 2, 2, SSM_STATE).transpose(1, 2, 3, 0, 4)
        hr_out.append(fin[:, :, 0])
        hi_out.append(fin[:, :, 1])
    return (xp, xs, jnp.stack(ks_out, axis=1), jnp.stack(vs_out, axis=1),
            jnp.stack(hr_out, axis=1), jnp.stack(hi_out, axis=1))
```

```python
import functools
import math

import jax
import jax.numpy as jnp
import numpy as np
from jax import lax
from jax.experimental import pallas as pl
from jax.experimental.pallas import tpu as pltpu

F32 = jnp.float32
BF16 = jnp.bfloat16

D_MODEL = 1024
GRID_W = 64
ATT_WIDTH = 512
SSM_WIDTH = 512
HEAD_DIM = 64
N_HEADS = 4
HEAD_W = 2 * HEAD_DIM
SSM_GROUP = 16
N_GROUPS = 32
SSM_STATE = 64
D_FF = 2816
N_MOD = 6
ROPE_BASE = 10000.0
NORM_EPS = 1e-6

CHUNK = 16
CW = CHUNK * SSM_GROUP
SCAN_BLOCK = 8
SCAN_LEVELS = 3
AT_ROWS = 16
GROUP_BLOCK = 4
Z_BATCH = 8

ROWS_PER_TILE = 128
TOK_PER_TILE = ROWS_PER_TILE * CHUNK
SUB_TOK = 1024
TQ_ITEM = 256
N_SUB = TOK_PER_TILE // SUB_TOK

VMEM_LIMIT = 56 * 1024 * 1024

NT_DIMS = (((1,), (1,)), ((), ()))
TN_DIMS = (((0,), (0,)), ((), ()))


def _cparams(n_axes):
    return pltpu.CompilerParams(
        dimension_semantics=("arbitrary",) * n_axes,
        vmem_limit_bytes=VMEM_LIMIT)


def _rms(x, g):
    ms = jnp.mean(x * x, axis=-1, keepdims=True)
    return x * lax.rsqrt(ms + NORM_EPS) * g


def _silu(x):
    return x * jax.nn.sigmoid(x)


MOD_ROWS = 8


def _mod_kernel(ctx_ref, c_ref, w_ref, b_ref, o_ref):
    n_lat, tk = c_ref.shape
    row = lax.broadcasted_iota(jnp.int32, (MOD_ROWS, tk), 0)
    cond = jnp.where(row == 0, ctx_ref[...], 0.0)
    for b in range(n_lat):
        cond = jnp.where(row == 1 + b, c_ref[b:b + 1, :], cond)
    part = jnp.dot(_silu(cond).astype(BF16), w_ref[...].astype(BF16),
                   preferred_element_type=F32)

    @pl.when(pl.program_id(0) == 0)
    def _():
        o_ref[:, 0, :] = part + b_ref[...]

    @pl.when(pl.program_id(0) > 0)
    def _():
        o_ref[:, 0, :] += part


def _rope(x, cos, sa, sb):
    return (x * cos + pltpu.roll(x, HEAD_W - 16, axis=1) * sa
            + pltpu.roll(x, 16, axis=1) * sb)


def _in_proj_kernel(*refs, rope, seqs_per_sub):
    x_ref, x3_hbm, mod_ref, g_ref, w_ref = refs[:5]
    refs = refs[5:]
    if rope:
        cos_ref, sa_ref, sb_ref = refs[:3]
        refs = refs[3:]
        q_ref, k_ref, v_ref, ut_ref, wut_ref, xt_ref, xt_sem = refs
    else:
        q_ref, k_ref, v_ref, kc_ref, vc_ref, ut_ref, wut_ref, xt_ref, xt_sem = refs
    tile = pl.program_id(0)
    j = pl.program_id(1)
    t_per_sub = CHUNK // N_SUB
    t_early = CHUNK - t_per_sub

    def gather(tile_idx, t):
        src = x3_hbm.at[pl.ds(tile_idx * ROWS_PER_TILE, ROWS_PER_TILE), t, :]
        return pltpu.make_async_copy(src, xt_ref.at[t], xt_sem.at[t])

    @pl.when(j == 0)
    def _():
        @pl.when(tile == 0)
        def _():
            for t in range(t_early):
                gather(0, t).start()
        for t in range(t_early, CHUNK):
            gather(tile, t).start()

    t_base = j * t_per_sub
    for d in range(t_per_sub):
        gather(tile, t_base + d).wait()

    @pl.when((j == N_SUB - 1) & (tile + 1 < pl.num_programs(0)))
    def _():
        for t in range(t_early):
            gather(tile + 1, t).start()

    @pl.when((tile == 0) & (j == 0))
    def _():
        wut_ref[...] = w_ref[:, 3 * ATT_WIDTH:].T

    shift = mod_ref[:, 0:D_MODEL]
    gain = g_ref[...] * (1.0 + mod_ref[:, D_MODEL:2 * D_MODEL])

    def norm_mod(xv):
        ms = jnp.mean(xv * xv, axis=-1, keepdims=True)
        return (xv * lax.rsqrt(ms + NORM_EPS) * gain + shift).astype(BF16)

    def ssm_input(d0):
        xt = jnp.concatenate([xt_ref[t_base + d0], xt_ref[t_base + d0 + 1]], axis=0)
        ut = lax.dot_general(wut_ref[...], norm_mod(xt), NT_DIMS,
                             preferred_element_type=F32)
        for d in range(2):
            blk = ut[:, d * ROWS_PER_TILE:(d + 1) * ROWS_PER_TILE]
            row0 = pl.multiple_of((t_base + d0 + d) * SSM_GROUP, SSM_GROUP)
            ut_ref[:, pl.ds(row0, SSM_GROUP), :] = (
                blk.reshape(N_GROUPS, SSM_GROUP, ROWS_PER_TILE).astype(ut_ref.dtype))

    proj = jnp.dot(norm_mod(x_ref[...]), w_ref[:, 0:3 * ATT_WIDTH],
                   preferred_element_type=F32)
    q = proj[:, 0:ATT_WIDTH]
    k = proj[:, ATT_WIDTH:2 * ATT_WIDTH]
    v = proj[:, 2 * ATT_WIDTH:3 * ATT_WIDTH]
    qscale = HEAD_DIM ** -0.5 * math.log2(math.e)
    if rope:
        cos, sa, sb = cos_ref[...], sa_ref[...], sb_ref[...]
        for hd in range(N_HEADS):
            sl = slice(hd * HEAD_W, (hd + 1) * HEAD_W)
            q_ref[:, sl] = (_rope(q[:, sl], cos, sa, sb) * qscale).astype(q_ref.dtype)
            k_ref[:, sl] = _rope(k[:, sl], cos, sa, sb).astype(k_ref.dtype)
    else:
        q_ref[...] = (q * qscale).astype(q_ref.dtype)
        k_ref[...] = k.astype(k_ref.dtype)
        seq = SUB_TOK // seqs_per_sub
        k_t = k.T
        for b in range(seqs_per_sub):
            kc_ref[b] = k_t[:, b * seq:(b + 1) * seq]
            for hd in range(N_HEADS):
                vc_ref[b, :, hd, :] = v[b * seq:(b + 1) * seq, hd * HEAD_W:(hd + 1) * HEAD_W]
    v_ref[...] = v.astype(v_ref.dtype)

    for d0 in range(0, t_per_sub, 2):
        ssm_input(d0)


def _in_proj(x, mods, mod_rows, g0, w_in, rope_tabs):
    n_batch, seq_len = x.shape[:2]
    n_tok = n_batch * seq_len
    n_rows = n_tok // CHUNK
    n_tiles = n_tok // TOK_PER_TILE
    mod_row0, n_mod = mod_rows
    tiles_per_mod = n_tiles // n_mod
    rope = rope_tabs is not None
    seqs_per_sub = max(1, SUB_TOK // seq_len)
    in_specs = [pl.BlockSpec((SUB_TOK, D_MODEL), lambda i, j: (i * N_SUB + j, 0)),
                pl.BlockSpec(memory_space=pl.ANY),
                pl.BlockSpec((None, 1, 2 * D_MODEL),
                             lambda i, j: (mod_row0 + i // tiles_per_mod, 0, 0)),
                pl.BlockSpec((1, D_MODEL), lambda i, j: (0, 0)),
                pl.BlockSpec((D_MODEL, 4 * ATT_WIDTH), lambda i, j: (0, 0))]
    args = [x.reshape(n_tok, D_MODEL), x.reshape(n_rows, CHUNK, D_MODEL), mods, g0, w_in]
    row_spec = pl.BlockSpec((SUB_TOK, ATT_WIDTH), lambda i, j: (i * N_SUB + j, 0))
    row_shape = jax.ShapeDtypeStruct((n_tok, ATT_WIDTH), BF16)
    out_specs = [row_spec, row_spec, row_spec]
    out_shape = [row_shape, row_shape, row_shape]
    if rope:
        assert seq_len == TOK_PER_TILE
        for tab in rope_tabs:
            in_specs.append(pl.BlockSpec((SUB_TOK, HEAD_W), lambda i, j: (j, 0)))
            args.append(tab)
    else:
        out_specs += [pl.BlockSpec((seqs_per_sub, ATT_WIDTH, seq_len),
                                   lambda i, j: (i * N_SUB + j, 0, 0)),
                      pl.BlockSpec((seqs_per_sub, seq_len, N_HEADS, HEAD_W),
                                   lambda i, j: (i * N_SUB + j, 0, 0, 0))]
        out_shape += [jax.ShapeDtypeStruct((n_batch, ATT_WIDTH, seq_len), F32),
                      jax.ShapeDtypeStruct((n_batch, seq_len, N_HEADS, HEAD_W), F32)]
    out_specs.append(pl.BlockSpec((N_GROUPS, CW, ROWS_PER_TILE), lambda i, j: (0, 0, i)))
    out_shape.append(jax.ShapeDtypeStruct((N_GROUPS, CW, n_rows), BF16))
    return pl.pallas_call(
        functools.partial(_in_proj_kernel, rope=rope, seqs_per_sub=seqs_per_sub),
        grid=(n_tiles, N_SUB),
        in_specs=in_specs,
        out_specs=out_specs,
        out_shape=out_shape,
        scratch_shapes=[pltpu.VMEM((SSM_WIDTH, D_MODEL), BF16),
                        pltpu.VMEM((CHUNK, ROWS_PER_TILE, D_MODEL), F32),
                        pltpu.SemaphoreType.DMA((CHUNK,))],
        compiler_params=_cparams(2),
        name="in_proj",
    )(*args)


def _attn_kernel(*refs, has_ctx, lam_init, n_seq, seq_len, tq, n_cast):
    if n_cast:
        cast_in = refs[len(refs) - 2 * n_cast - 1:len(refs) - n_cast - 1]
        cast_out = refs[len(refs) - n_cast:]
        refs = refs[:len(refs) - 2 * n_cast - 1] + (refs[len(refs) - n_cast - 1],)
        for src, dst in zip(cast_in, cast_out):
            dst[...] = src[...].astype(dst.dtype)
    if has_ctx:
        lam_ref, sg_ref, q_ref, ck_ref, cv_ref, k_ref, v_ref, o_ref = refs
    else:
        lam_ref, sg_ref, q_ref, k_ref, v_ref, o_ref = refs
    lp = lam_ref[...]
    lam = (jnp.exp(jnp.sum(lp[0:1] * lp[1:2], axis=-1, keepdims=True))
           - jnp.exp(jnp.sum(lp[2:3] * lp[3:4], axis=-1, keepdims=True)) + lam_init)
    first_map = lax.broadcasted_iota(jnp.int32, (1, HEAD_W), 1) < HEAD_DIM
    ti = min(TQ_ITEM, tq)
    for row0 in range(0, n_seq * tq, ti):
        b = row0 // tq
        q_rows = slice(row0, row0 + ti)
        for hd in range(N_HEADS):
            sl = slice(hd * HEAD_W, (hd + 1) * HEAD_W)
            qh = q_ref[q_rows, sl]
            zero = jnp.zeros_like(qh)
            qs = jnp.concatenate([jnp.where(first_map, qh, zero),
                                  jnp.where(first_map, zero, qh)], axis=0)
            kv_rows = slice(b * seq_len, (b + 1) * seq_len)
            parts = [(k_ref[kv_rows, sl], v_ref[kv_rows, sl])]
            scores = [lax.dot_general(qs, parts[0][0], NT_DIMS, preferred_element_type=F32)]
            if has_ctx:
                parts.insert(0, (None, cv_ref[:, hd, :].astype(BF16)))
                scores.insert(0, jnp.dot(qs, ck_ref[sl, :].astype(BF16),
                                         preferred_element_type=F32))
            mx = scores[0].max(axis=-1, keepdims=True)
            for s in scores[1:]:
                mx = jnp.maximum(mx, s.max(axis=-1, keepdims=True))
            acc = None
            for s, (_, vv) in zip(scores, parts):
                e = jnp.exp2(s - mx).astype(BF16)
                v_one = jnp.concatenate([vv, jnp.ones_like(vv)], axis=1)
                pv = jnp.dot(e, v_one, preferred_element_type=F32)
                acc = pv if acc is None else acc + pv
            num = acc[:, 0:HEAD_W] / acc[:, HEAD_W:2 * HEAD_W]
            o = num[0:ti] - lam * num[ti:2 * ti]
            o = _rms(o, sg_ref[...]) * (1.0 - lam_init)
            o_ref[q_rows, sl] = o.astype(o_ref.dtype)


def _attention(q, k, v, ctx_k, ctx_v, lam_params, subln_g, n_batch, seq_len, lam_init,
               cast_weights=()):
    has_ctx = ctx_k is not None
    tq = min(1024, seq_len)
    n_q = seq_len // tq
    n_seq = 1 if n_q > 1 else min(4, n_batch)
    in_specs = [pl.BlockSpec((4, HEAD_DIM), lambda b, i: (0, 0)),
                pl.BlockSpec((1, HEAD_W), lambda b, i: (0, 0)),
                pl.BlockSpec((n_seq * tq, ATT_WIDTH), lambda b, i: (b * n_q + i, 0))]
    args = [lam_params, subln_g.reshape(1, HEAD_W), q]
    if has_ctx:
        past = ctx_v.shape[1]
        in_specs +=[pl.BlockSpec((None, ATT_WIDTH, past), lambda b, i: (b, 0, 0)),
                     pl.BlockSpec((None, past, N_HEADS, HEAD_W), lambda b, i: (b, 0, 0, 0))]
        args += [ctx_k, ctx_v]
    kv_spec = pl.BlockSpec((n_seq * seq_len, ATT_WIDTH), lambda b, i: (b, 0))
    in_specs += [kv_spec, kv_spec]
    args += [k, v]
    out_specs = [pl.BlockSpec((n_seq * tq, ATT_WIDTH), lambda b, i: (b * n_q + i, 0))]
    out_shape = [jax.ShapeDtypeStruct((n_batch * seq_len, ATT_WIDTH), BF16)]
    n_steps = (n_batch // n_seq) * n_q
    for w in cast_weights:
        rows = w.shape[0] // n_steps
        spec = pl.BlockSpec((rows, w.shape[1]), lambda b, i: (b * n_q + i, 0))
        in_specs.append(spec)
        args.append(w)
        out_specs.append(spec)
        out_shape.append(jax.ShapeDtypeStruct(w.shape, BF16))
    outs = pl.pallas_call(
        functools.partial(_attn_kernel, has_ctx=has_ctx, lam_init=lam_init,
                          n_seq=n_seq, seq_len=seq_len, tq=tq, n_cast=len(cast_weights)),
        grid=(n_batch // n_seq, n_q),
        in_specs=in_specs,
        out_specs=out_specs,
        out_shape=out_shape,
        compiler_params=_cparams(2),
        name="diff_attention",
    )(*args)
    return outs[0], tuple(outs[1:])


def _cmul(ar, ai, br, bi):
    return ar * br - ai * bi, ar * bi + ai * br


def _ssm_prep_kernel(lre_ref, lim_ref, ls_ref, bre_ref, bim_ref, cre_ref, cim_ref, d_ref,
                     mt_ref, gt_ref, wo_ref, at_ref):
    lane = lax.broadcasted_iota(jnp.int32, (SSM_GROUP, CW), 1)
    chan = lax.broadcasted_iota(jnp.int32, (SSM_GROUP, CW), 0)
    for gi in range(GROUP_BLOCK):
        gt_cols, wo_cols, at_cols, toeplitz = [], [], [], []
        for dr in range(2):
            lr = jnp.minimum(lre_ref[dr, gi], -1e-4)
            li = lim_ref[dr, gi]
            step = jnp.exp(ls_ref[dr, gi])
            mag = jnp.exp(lr * step)
            a_re = mag * jnp.cos(li * step)
            a_im = mag * jnp.sin(li * step)
            den = lr * lr + li * li
            nr = a_re - 1.0
            f_re = (nr * lr + a_im * li) / den
            f_im = (a_im * lr - nr * li) / den
            bt_re, bt_im = bre_ref[dr, gi], bim_ref[dr, gi]
            bb_re, bb_im = _cmul(f_re, f_im, bt_re, bt_im)
            c_re, c_im = cre_ref[dr, gi], cim_ref[dr, gi]
            pw = [(jnp.ones_like(a_re), jnp.zeros_like(a_im))]
            for _ in range(CHUNK):
                pw.append(_cmul(pw[-1][0], pw[-1][1], a_re, a_im))
            g_re, g_im, e_re, e_im = [], [], [], []
            for t in range(CHUNK):
                pr, pi = pw[CHUNK - 1 - t] if dr == 0 else pw[t]
                r, i = _cmul(bb_re, bb_im, pr, pi)
                g_re.append(r)
                g_im.append(i)
                pr, pi = pw[t + 1] if dr == 0 else pw[CHUNK - t]
                r, i = _cmul(c_re, c_im, pr, pi)
                e_re.append(r)
                e_im.append(-i)
            g_cat = jnp.concatenate([jnp.concatenate(g_re, axis=0),
                                     jnp.concatenate(g_im, axis=0)], axis=1)
            gt_cols.append(g_cat)
            wo_cols.append(jnp.concatenate([jnp.concatenate(e_re, axis=0),
                                            jnp.concatenate(e_im, axis=0)], axis=1))
            c_cat = jnp.concatenate([c_re, -c_im], axis=1)
            toeplitz.append(lax.dot_general(c_cat, g_cat, NT_DIMS,
                                            precision=lax.Precision.HIGHEST,
                                            preferred_element_type=F32))
            apw = [pw[CHUNK]]
            for _ in range(SCAN_BLOCK - 1):
                apw.append(_cmul(apw[-1][0], apw[-1][1], apw[0][0], apw[0][1]))
            order = list(range(SCAN_BLOCK)) if dr == 0 else list(range(SCAN_BLOCK - 1, -1, -1))
            order += [2 ** l - 1 for l in range(SCAN_LEVELS)]
            order += [0] * (AT_ROWS - len(order))
            at_cols += [jnp.concatenate([jnp.concatenate([apw[i][0], apw[i][0]], axis=1)
                                         for i in order], axis=0),
                        jnp.concatenate([jnp.concatenate([-apw[i][1], apw[i][1]], axis=1)
                                         for i in order], axis=0)]
        kf_rev, kb = toeplitz
        d_skip = d_ref[gi]
        blocks = []
        for t in range(CHUNK):
            fwd = pltpu.roll(kf_rev, (CW - (CHUNK - 1 - t) * SSM_GROUP) % CW, axis=1)
            bwd = pltpu.roll(kb, t * SSM_GROUP, axis=1)
            blocks.append(jnp.where(lane < (t + 1) * SSM_GROUP, fwd, 0.0)
                          + jnp.where(lane >= t * SSM_GROUP, bwd, 0.0)
                          + jnp.where(lane == chan + t * SSM_GROUP, d_skip, 0.0))
        mt_ref[gi] = jnp.concatenate(blocks, axis=0).astype(mt_ref.dtype)
        gt_ref[gi] = jnp.concatenate(gt_cols, axis=1).astype(gt_ref.dtype)
        wo_ref[gi] = jnp.concatenate(wo_cols, axis=1).astype(wo_ref.dtype)
        at_ref[gi] = jnp.concatenate(at_cols, axis=1)


N_PREP_IN, N_PREP_OUT, N_MOD_IN = 8, 4, 4


def _prep_mod_kernel(*refs):
    prep_in = refs[:N_PREP_IN]
    mod_in = refs[N_PREP_IN:N_PREP_IN + N_MOD_IN]
    w_in_ref = refs[N_PREP_IN + N_MOD_IN]
    outs = refs[N_PREP_IN + N_MOD_IN + 1:]
    _ssm_prep_kernel(*prep_in, *outs[:N_PREP_OUT])
    _mod_kernel(*mod_in, outs[N_PREP_OUT])
    outs[N_PREP_OUT + 1][...] = w_in_ref[...].astype(BF16)


def _ssm_prep_and_modulation(lam_re, lam_im, log_step, b_re, b_im, c_re, c_im, d_skip,
                             c_ctx, c, w_mod, b_mod, w_in):
    row = lambda a: a.reshape(2, N_GROUPS, 1, SSM_STATE)
    bt = lambda a: jnp.swapaxes(a, 2, 3)
    d_row = jnp.tile((d_skip[0] + d_skip[1]).reshape(N_GROUPS, 1, SSM_GROUP), (1, 1, CHUNK))
    gb = GROUP_BLOCK
    n_steps = N_GROUPS // gb
    vec_spec = pl.BlockSpec((2, gb, 1, SSM_STATE), lambda i: (0, i, 0, 0))
    mat_spec = pl.BlockSpec((2, gb, SSM_GROUP, SSM_STATE), lambda i: (0, i, 0, 0))
    w_spec = pl.BlockSpec((gb, CW, CW), lambda i: (i, 0, 0))
    w_shape = jax.ShapeDtypeStruct((N_GROUPS, CW, CW), BF16)
    n_mod = w_mod.shape[1]
    tk = D_MODEL // n_steps
    assert 1 + c.shape[0] <= MOD_ROWS
    outs = pl.pallas_call(
        _prep_mod_kernel,
        grid=(n_steps,),
        in_specs=[vec_spec, vec_spec,
                  pl.BlockSpec((2, gb, 1, 1), lambda i: (0, i, 0, 0)),
                  mat_spec, mat_spec, mat_spec, mat_spec,
                  pl.BlockSpec((gb, 1, CW), lambda i: (i, 0, 0)),
                  pl.BlockSpec((1, tk), lambda k: (0, k)),
                  pl.BlockSpec((c.shape[0], tk), lambda k: (0, k)),
                  pl.BlockSpec((tk, n_mod), lambda k: (k, 0)),
                  pl.BlockSpec((1, n_mod), lambda k: (0, 0)),
                  pl.BlockSpec((tk, w_in.shape[1]), lambda k: (k, 0))],
        out_specs=[w_spec, w_spec, w_spec,
                   pl.BlockSpec((gb, AT_ROWS, 4 * 2 * SSM_STATE), lambda i: (i, 0, 0)),
                   pl.BlockSpec((MOD_ROWS, 1, n_mod), lambda k: (0, 0, 0)),
                   pl.BlockSpec((tk, w_in.shape[1]), lambda k: (k, 0))],
        out_shape=[w_shape, w_shape, w_shape,
                   jax.ShapeDtypeStruct((N_GROUPS, AT_ROWS, 4 * 2 * SSM_STATE), F32),
                   jax.ShapeDtypeStruct((MOD_ROWS, 1, n_mod), F32),
                   jax.ShapeDtypeStruct(w_in.shape, BF16)],
        compiler_params=_cparams(1),
        name="ssm_prep_modulation",
    )(row(lam_re), row(lam_im), log_step.reshape(2, N_GROUPS, 1, 1),
      bt(b_re), bt(b_im), c_re, c_im, d_row,
      c_ctx.reshape(1, D_MODEL), c, w_mod, b_mod.reshape(1, n_mod), w_in)
    return outs[:N_PREP_OUT], outs[N_PREP_OUT], outs[N_PREP_OUT + 1]


def _shift_rows(x, m, down):
    n = x.shape[0]
    return pltpu.roll(x, m if down else n - m, axis=0)


def _ssm_kernel(*refs, n_seq, has_h0):
    if has_h0:
        (xt_ref, mt_ref, gt_ref, wo_ref, at_ref, wg_ref, bg_ref, h0_ref,
         z_hbm, zs_ref, zb_ref, zb_sem) = refs
    else:
        (xt_ref, mt_ref, gt_ref, wo_ref, at_ref, wg_ref, bg_ref,
         z_hbm, st_ref, zs_ref, zb_ref, zb_sem, fin_ref) = refs
    step = pl.program_id(0)
    n_rows = xt_ref.shape[-1]
    seg = n_rows // n_seq
    cw2 = 2 * SSM_STATE
    assert seg % SCAN_BLOCK == 0
    n_blk = n_rows // SCAN_BLOCK
    blk_per_seq = seg // SCAN_BLOCK
    pos = lax.broadcasted_iota(jnp.int32, (n_rows, cw2), 0) % seg
    row_blk = lax.broadcasted_iota(jnp.int32, (SCAN_BLOCK, cw2), 0)

    def low_half(shape):
        return lax.broadcasted_iota(jnp.int32, shape, 1) < SSM_STATE

    def swap(v):
        return pltpu.roll(v, SSM_STATE, axis=1)

    def to_planes(va, vb):
        lo = low_half((va.shape[0], cw2))
        va_l = pltpu.roll(va, 3 * SSM_STATE, axis=1)
        vb_r = pltpu.roll(vb, SSM_STATE, axis=1)
        return (jnp.where(lo, va[:, 0:cw2], vb_r[:, 0:cw2]),
                jnp.where(lo, va_l[:, 0:cw2], vb[:, 0:cw2]),
                jnp.where(lo, va[:, cw2:2 * cw2], vb_r[:, cw2:2 * cw2]),
                jnp.where(lo, va_l[:, cw2:2 * cw2], vb[:, cw2:2 * cw2]))

    def from_planes(f_re, f_im, b_re, b_im):
        lo = low_half(f_re.shape)
        va = jnp.concatenate([jnp.where(lo, f_re, swap(f_im)), jnp.where(lo, b_re, swap(b_im))], axis=1)
        vb = jnp.concatenate([jnp.where(lo, swap(f_re), f_im), jnp.where(lo, swap(b_re), b_im)], axis=1)
        return va, vb

    for ga in range(0, GROUP_BLOCK, 2):
        gb = ga + 1
        xts = [xt_ref[ga], xt_ref[gb]]
        s_pair = [lax.dot_general(xts[i], gt_ref[g], TN_DIMS, preferred_element_type=F32)
                  for i, g in enumerate((ga, gb))]
        planes = to_planes(*s_pair)
        lo_t = low_half((AT_ROWS, cw2))
        if has_h0:
            h0_planes = to_planes(h0_ref[ga], h0_ref[gb])
        ent_planes = []
        for dr in range(2):
            down = dr == 0
            pa, pb = (at_ref[g][:, (2 * dr) * cw2:(2 * dr + 1) * cw2] for g in (ga, gb))
            qa, qb = (at_ref[g][:, (2 * dr + 1) * cw2:(2 * dr + 2) * cw2] for g in (ga, gb))
            ar_tab = jnp.where(lo_t, pa, pb)
            ai_tab = jnp.where(lo_t, -qa, qb)
            re3 = planes[2 * dr].reshape(n_blk, SCAN_BLOCK, cw2)
            im3 = planes[2 * dr + 1].reshape(n_blk, SCAN_BLOCK, cw2)
            for lvl in range(SCAN_LEVELS):
                m = 2 ** lvl
                valid = (row_blk >= m) if down else (row_blk < SCAN_BLOCK - m)
                row = SCAN_BLOCK + lvl
                ar = jnp.where(valid, ar_tab[row:row + 1], 0.0)
                ai = jnp.where(valid, ai_tab[row:row + 1], 0.0)
                shift = m if down else SCAN_BLOCK - m
                sh_re = pltpu.roll(re3, shift, axis=1)
                sh_im = pltpu.roll(im3, shift, axis=1)
                re3, im3 = re3 + ar * sh_re - ai * sh_im, im3 + ar * sh_im + ai * sh_re
            ar_blk, ai_blk = ar_tab[0:SCAN_BLOCK], ai_tab[0:SCAN_BLOCK]
            blk_re = [re3[i] for i in range(n_blk)]
            blk_im = [im3[i] for i in range(n_blk)]
            edge = slice(SCAN_BLOCK - 1, SCAN_BLOCK) if down else slice(0, 1)
            for q_i in range(n_seq):
                idxs = list(range(q_i * blk_per_seq, (q_i + 1) * blk_per_seq))
                idxs = idxs if down else idxs[::-1]
                for prev, cur in zip([None] + idxs[:-1], idxs):
                    if prev is not None:
                        c_re, c_im = blk_re[prev][edge], blk_im[prev][edge]
                    elif has_h0:
                        c_re = h0_planes[2 * dr][q_i:q_i + 1]
                        c_im = h0_planes[2 * dr + 1][q_i:q_i + 1]
                    else:
                        continue
                    c_re = jnp.broadcast_to(c_re, (SCAN_BLOCK, cw2))
                    c_im = jnp.broadcast_to(c_im, (SCAN_BLOCK, cw2))
                    blk_re[cur] = blk_re[cur] + ar_blk * c_re - ai_blk * c_im
                    blk_im[cur] = blk_im[cur] + ar_blk * c_im + ai_blk * c_re
            for part, blks in ((0, blk_re), (1, blk_im)):
                s = jnp.concatenate(blks, axis=0)
                if not has_h0:
                    fin_ref[part] = s
                ent = _shift_rows(s, 1, down)
                ent = jnp.where((pos >= 1) if down else (pos < seg - 1), ent, 0.0)
                if has_h0:
                    ent_blk = [ent[i * SCAN_BLOCK:(i + 1) * SCAN_BLOCK] for i in range(n_blk)]
                    at_edge = row_blk == (0 if down else SCAN_BLOCK - 1)
                    for q_i in range(n_seq):
                        bi = q_i * blk_per_seq if down else (q_i + 1) * blk_per_seq - 1
                        ent_blk[bi] = jnp.where(
                            at_edge, h0_planes[2 * dr + part][q_i:q_i + 1], ent_blk[bi])
                    ent = jnp.concatenate(ent_blk, axis=0)
                ent_planes.append(ent)
            if not has_h0:
                rows = pl.ds(seg - 1 if down else 0, n_seq, stride=seg)
                f_re, f_im = fin_ref[0, rows, :], fin_ref[1, rows, :]
                lo_s = low_half((n_seq, cw2))
                st_ref[ga, :, dr * cw2:(dr + 1) * cw2] = jnp.where(lo_s, f_re, swap(f_im))
                st_ref[gb, :, dr * cw2:(dr + 1) * cw2] = jnp.where(lo_s, swap(f_re), f_im)
        h_pair = from_planes(*ent_planes)
        for i, g in enumerate((ga, gb)):
            yt = (jnp.dot(mt_ref[g], xts[i], preferred_element_type=F32)
                  + lax.dot_general(wo_ref[g], h_pair[i].astype(BF16), NT_DIMS,
                                    preferred_element_type=F32))
            z = jax.nn.gelu(yt, approximate=True)
            grp = step * GROUP_BLOCK + g
            for t in range(CHUNK):
                zs_ref[t, pl.ds(pl.multiple_of(grp * SSM_GROUP, SSM_GROUP), SSM_GROUP), :] = (
                    z[t * SSM_GROUP:(t + 1) * SSM_GROUP, :])

    @pl.when(step == pl.num_programs(0) - 1)
    def _():
        def put(t):
            return pltpu.make_async_copy(zb_ref.at[t], z_hbm.at[:, t, :], zb_sem.at[t])

        for t0 in range(0, CHUNK, Z_BATCH):
            for t in range(t0, t0 + Z_BATCH):
                zt = zs_ref[t]
                gate = jnp.dot(wg_ref[...], zt.astype(BF16),
                               preferred_element_type=F32) + bg_ref[...]
                zb_ref[t] = (zt * jax.nn.sigmoid(gate)).T
            for t in range(t0, t0 + Z_BATCH):
                put(t).start()
        for t in range(CHUNK):
            put(t).wait()


def _ssm(xt, mt, gt, wo, at, w_glu_t, b_glu_col, h0, n_seq):
    n_rows = xt.shape[-1]
    has_h0 = h0 is not None
    gb = GROUP_BLOCK
    w_spec = pl.BlockSpec((gb, CW, CW), lambda i: (i, 0, 0))
    in_specs = [pl.BlockSpec((gb, CW, n_rows), lambda i: (i, 0, 0)),
                w_spec, w_spec, w_spec,
                pl.BlockSpec((gb, AT_ROWS, 4 * 2 * SSM_STATE), lambda i: (i, 0, 0)),
                pl.BlockSpec((SSM_WIDTH, SSM_WIDTH), lambda i: (0, 0)),
                pl.BlockSpec((SSM_WIDTH, 1), lambda i: (0, 0))]
    args = [xt, mt, gt, wo, at, w_glu_t, b_glu_col]
    out_specs = [pl.BlockSpec(memory_space=pl.ANY)]
    out_shape = [jax.ShapeDtypeStruct((n_rows, CHUNK, SSM_WIDTH), F32)]
    scratch = [pltpu.VMEM((CHUNK, SSM_WIDTH, n_rows), F32),
               pltpu.VMEM((CHUNK, n_rows, SSM_WIDTH), F32),
               pltpu.SemaphoreType.DMA((CHUNK,))]
    if has_h0:
        in_specs.append(pl.BlockSpec((gb, n_seq, CW), lambda i: (i, 0, 0)))
        args.append(h0)
    else:
        out_specs.append(pl.BlockSpec((gb, n_seq, CW), lambda i: (i, 0, 0)))
        out_shape.append(jax.ShapeDtypeStruct((N_GROUPS, n_seq, CW), F32))
        scratch.append(pltpu.VMEM((2, n_rows, 2 * SSM_STATE), F32))
    return pl.pallas_call(
        functools.partial(_ssm_kernel, n_seq=n_seq, has_h0=has_h0),
        grid=(N_GROUPS // gb,),
        in_specs=in_specs,
        out_specs=out_specs,
        out_shape=out_shape,
        scratch_shapes=scratch,
        compiler_params=_cparams(1),
        name="ssm_scan_glu",
    )(*args)


FF_CHUNK = 256
FFN_TOK = 1024


def _out_ffn_kernel(x_ref, attn_ref, z_ref, mod_ref, g_ref, wo_ref, wfi_ref, wfo_ref,
                    o_ref, act_ref):
    gate1 = mod_ref[:, 2 * D_MODEL:3 * D_MODEL]
    shift2 = mod_ref[:, 3 * D_MODEL:4 * D_MODEL]
    scale2 = mod_ref[:, 4 * D_MODEL:5 * D_MODEL]
    gate2 = mod_ref[:, 5 * D_MODEL:6 * D_MODEL]
    half = FFN_TOK // 2
    gain2 = g_ref[2:3, :] * (1.0 + scale2)

    def pre_ffn(hf):
        r = slice(hf * half, (hf + 1) * half)
        mixer = jnp.concatenate([attn_ref[r, :], z_ref[r, :].astype(BF16)], axis=1)
        mix = jnp.dot(mixer, wo_ref[...], preferred_element_type=F32)
        x1 = x_ref[r, :] + gate1 * _rms(mix, g_ref[1:2, :])
        ms = jnp.mean(x1 * x1, axis=-1, keepdims=True)
        return x1, (x1 * lax.rsqrt(ms + NORM_EPS) * gain2 + shift2).astype(BF16)

    def ffn_in(hf, h, chunks):
        r = slice(hf * half, (hf + 1) * half)
        for c in chunks:
            lo = c * FF_CHUNK
            gt = jnp.dot(h, wfi_ref[:, lo:lo + FF_CHUNK], preferred_element_type=F32)
            up = jnp.dot(h, wfi_ref[:, D_FF + lo:D_FF + lo + FF_CHUNK],
                         preferred_element_type=F32)
            act_ref[r, lo:lo + FF_CHUNK] = (_silu(gt) * up).astype(BF16)

    def ffn_out(hf, x1):
        r = slice(hf * half, (hf + 1) * half)
        f = jnp.dot(act_ref[r, :], wfo_ref[...], preferred_element_type=F32)
        o_ref[r, :] = x1 + gate2 * _rms(f, g_ref[3:4, :])

    n_chunks = D_FF // FF_CHUNK
    x1_a, h_a = pre_ffn(0)
    x1_b, h_b = pre_ffn(1)
    ffn_in(0, h_a, range(n_chunks))
    ffn_out(0, x1_a)
    ffn_in(1, h_b, range(n_chunks))
    ffn_out(1, x1_b)


def _out_ffn(x2d, attn, z, mods, mod_rows, norm_g, w_o, w_ffn_in, w_ffn_out):
    n_tok = x2d.shape[0]
    z2d = z.reshape(n_tok, SSM_WIDTH)
    n_steps = n_tok // FFN_TOK
    mod_row0, n_mod = mod_rows
    steps_per_mod = n_steps // n_mod
    const = lambda i: (0, 0)
    row_spec = lambda w: pl.BlockSpec((FFN_TOK, w), lambda i: (i, 0))
    return pl.pallas_call(
        _out_ffn_kernel,
        grid=(n_steps,),
        in_specs=[row_spec(D_MODEL), row_spec(ATT_WIDTH), row_spec(SSM_WIDTH),
                  pl.BlockSpec((None, 1, N_MOD * D_MODEL),
                               lambda i: (mod_row0 + i // steps_per_mod, 0, 0)),
                  pl.BlockSpec((4, D_MODEL), const),
                  pl.BlockSpec((2 * ATT_WIDTH, D_MODEL), const, pipeline_mode=pl.Buffered(1)),
                  pl.BlockSpec((D_MODEL, 2 * D_FF), const, pipeline_mode=pl.Buffered(1)),
                  pl.BlockSpec((D_FF, D_MODEL), const, pipeline_mode=pl.Buffered(1))],
        out_specs=row_spec(D_MODEL),
        out_shape=jax.ShapeDtypeStruct((n_tok, D_MODEL), F32),
        scratch_shapes=[pltpu.VMEM((FFN_TOK, D_FF), BF16)],
        compiler_params=_cparams(1),
        name="out_proj_ffn",
    )(x2d, attn, z2d, mods, norm_g, w_o, w_ffn_in, w_ffn_out)


def _rope_tables(seq_len):
    t = np.arange(seq_len)
    row = (t // GRID_W).astype(np.float32)
    col = (t % GRID_W).astype(np.float32)
    half = HEAD_DIM // 2
    inv_freq = (np.float32(ROPE_BASE)
                ** (-np.arange(0, half, 2, dtype=np.float32) / np.float32(half))).astype(np.float32)
    ang_r = row[:, None] * inv_freq
    ang_c = col[:, None] * inv_freq
    ang = np.concatenate([ang_r, ang_r, ang_c, ang_c], axis=-1)
    cos, sin = np.cos(ang), np.sin(ang)
    upper = (np.arange(HEAD_DIM) % 32) < 16
    sa = np.where(upper, -sin, 0.0)
    sb = np.where(upper, 0.0, sin)
    two = lambda a: jnp.asarray(np.concatenate([a, a], axis=-1), dtype=F32)
    return two(cos), two(sa), two(sb)


def _layer(x, mods, mod_rows, lam_init, rope_tabs, ctx_k, ctx_v, h0, weights, prep):
    n_batch, seq_len = x.shape[:2]
    g = weights['norm_g']
    outs = _in_proj(x, mods, mod_rows, g[0:1], weights['w_in'], rope_tabs)
    q, k, v = outs[:3]
    pending = () if 'late_bf16' in weights else weights['late_f32']
    attn, cast = _attention(q, k, v, ctx_k, ctx_v, weights['lam'], weights['subln_g'],
                            n_batch, seq_len, lam_init, cast_weights=pending)
    if pending:
        weights['late_bf16'] = cast
    ssm_out = _ssm(outs[-1], *prep, weights['w_glu_t'], weights['b_glu_col'], h0, n_batch)
    y = _out_ffn(x.reshape(n_batch * seq_len, D_MODEL), attn, ssm_out[0], mods, mod_rows, g,
                 *weights['late_bf16'])
    return y.reshape(x.shape), outs[3:-1], ssm_out[1:]


def kernel(x_prompt, x_sample, cache_k, cache_v, state_ssm_re, state_ssm_im, c, c_ctx, w_mod, b_mod, norm_g, w_in, lam_params, subln_g, ssm_lambda_re, ssm_lambda_im, ssm_log_step, ssm_b_re, ssm_b_im, ssm_c_re, ssm_c_im, ssm_d, w_glu, b_glu, w_o, w_ffn_in, w_ffn_out):
    depth = w_mod.shape[0]
    assert depth == 1
    bp = x_prompt.shape[0]
    bd, ld_len = x_sample.shape[:2]
    past = cache_k.shape[2]
    xp, xs = x_prompt, x_sample
    rope_tabs = _rope_tables(ld_len)
    ks_out, vs_out, hr_out, hi_out = [], [], [], []
    for l in range(depth):
        lam_init = 0.8 - 0.6 * math.exp(-0.3 * l)
        prep, mods, w_in_bf16 = _ssm_prep_and_modulation(
            ssm_lambda_re[l], ssm_lambda_im[l], ssm_log_step[l], ssm_b_re[l], ssm_b_im[l],
            ssm_c_re[l], ssm_c_im[l], ssm_d[l], c_ctx, c, w_mod[l], b_mod[l], w_in[l])
        weights = {
            'norm_g': norm_g[l],
            'w_in': w_in_bf16,
            'lam': lam_params[l], 'subln_g': subln_g[l],
            'w_glu_t': w_glu[l].T.astype(BF16), 'b_glu_col': b_glu[l].reshape(SSM_WIDTH, 1),
            'late_f32': (w_o[l], w_ffn_in[l], w_ffn_out[l]),
        }
        ck = jnp.transpose(cache_k[:, l], (0, 2, 3, 1)).reshape(bd, ATT_WIDTH, past)
        cv = cache_v[:, l]
        h0 = jnp.stack([state_ssm_re[:, l], state_ssm_im[:, l]], axis=2)
        h0 = h0.transpose(3, 0, 1, 2, 4).reshape(N_GROUPS, bd, CW)
        g = weights['norm_g']
        lp_len = xp.shape[1]
        s_outs = _in_proj(xs, mods, (1, bd), g[0:1], w_in_bf16, rope_tabs)
        p_outs = _in_proj(xp, mods, (0, 1), g[0:1], w_in_bf16, None)
        k_ctx, v_ctx = p_outs[3:5]
        p_attn, _ = _attention(*p_outs[:3], None, None, weights['lam'], weights['subln_g'],
                               bp, lp_len, lam_init)
        p_z, st = _ssm(p_outs[-1], *prep, weights['w_glu_t'], weights['b_glu_col'], None, bp)
        s_attn, late_bf16 = _attention(*s_outs[:3], ck, cv, weights['lam'], weights['subln_g'],
                                       bd, ld_len, lam_init, cast_weights=weights['late_f32'])
        (s_z,) = _ssm(s_outs[-1], *prep, weights['w_glu_t'], weights['b_glu_col'], h0, bd)
        xs = _out_ffn(xs.reshape(bd * ld_len, D_MODEL), s_attn, s_z, mods, (1, bd), g,
                      *late_bf16).reshape(xs.shape)
        xp = _out_ffn(xp.reshape(bp * lp_len, D_MODEL), p_attn, p_z, mods, (0, 1), g,
                      *late_bf16).reshape(xp.shape)
        ks_out.append(jnp.swapaxes(k_ctx, 1, 2).reshape(bp, -1, 2 * N_HEADS, HEAD_DIM))
        vs_out.append(v_ctx)
        fin = st.reshape(N_GROUPS, bp, 2, 2, SSM_STATE).transpose(1, 2, 3, 0, 4)
        hr_out.append(fin[:, :, 0])
        hi_out.append(fin[:, :, 1])
    return (xp, xs, jnp.stack(ks_out, axis=1), jnp.stack(vs_out, axis=1),
            jnp.stack(hr_out, axis=1), jnp.stack(hi_out, axis=1))
```

```python
import functools
import math

import jax
import jax.numpy as jnp
import numpy as np
from jax import lax
from jax.experimental import pallas as pl
from jax.experimental.pallas import tpu as pltpu

F32 = jnp.float32
BF16 = jnp.bfloat16

D_MODEL = 1024
GRID_W = 64
ATT_WIDTH = 512
SSM_WIDTH = 512
HEAD_DIM = 64
N_HEADS = 4
HEAD_W = 2 * HEAD_DIM
SSM_GROUP = 16
N_GROUPS = 32
SSM_STATE = 64
D_FF = 2816
N_MOD = 6
ROPE_BASE = 10000.0
NORM_EPS = 1e-6

CHUNK = 16
CW = CHUNK * SSM_GROUP
SCAN_BLOCK = 8
SCAN_LEVELS = 3
AT_ROWS = 16
GROUP_BLOCK = 4
Z_BATCH = 8

ROWS_PER_TILE = 128
TOK_PER_TILE = ROWS_PER_TILE * CHUNK
SUB_TOK = 1024
TQ_ITEM = 256
N_SUB = TOK_PER_TILE // SUB_TOK

VMEM_LIMIT = 56 * 1024 * 1024

NT_DIMS = (((1,), (1,)), ((), ()))
TN_DIMS = (((0,), (0,)), ((), ()))


def _cparams(n_axes):
    return pltpu.CompilerParams(
        dimension_semantics=("arbitrary",) * n_axes,
        vmem_limit_bytes=VMEM_LIMIT)


def _rms(x, g):
    ms = jnp.mean(x * x, axis=-1, keepdims=True)
    return x * lax.rsqrt(ms + NORM_EPS) * g


def _silu(x):
    return x * jax.nn.sigmoid(x)


MOD_ROWS = 8


def _mod_kernel(ctx_ref, c_ref, w_ref, b_ref, o_ref):
    n_lat, tk = c_ref.shape
    row = lax.broadcasted_iota(jnp.int32, (MOD_ROWS, tk), 0)
    cond = jnp.where(row == 0, ctx_ref[...], 0.0)
    for b in range(n_lat):
        cond = jnp.where(row == 1 + b, c_ref[b:b + 1, :], cond)
    part = jnp.dot(_silu(cond).astype(BF16), w_ref[...].astype(BF16),
                   preferred_element_type=F32)

    @pl.when(pl.program_id(0) == 0)
    def _():
        o_ref[:, 0, :] = part + b_ref[...]

    @pl.when(pl.program_id(0) > 0)
    def _():
        o_ref[:, 0, :] += part


def _rope(x, cos, sa, sb):
    return (x * cos + pltpu.roll(x, HEAD_W - 16, axis=1) * sa
            + pltpu.roll(x, 16, axis=1) * sb)


def _in_proj_kernel(*refs, rope, seqs_per_sub):
    x_ref, x3_hbm, mod_ref, g_ref, w_ref = refs[:5]
    refs = refs[5:]
    if rope:
        cos_ref, sa_ref, sb_ref = refs[:3]
        refs = refs[3:]
        q_ref, k_ref, v_ref, ut_ref, wut_ref, xt_ref, xt_sem = refs
    else:
        q_ref, k_ref, v_ref, kc_ref, vc_ref, ut_ref, wut_ref, xt_ref, xt_sem = refs
    tile = pl.program_id(0)
    j = pl.program_id(1)
    t_per_sub = CHUNK // N_SUB
    t_early = CHUNK - t_per_sub

    def gather(tile_idx, t):
        src = x3_hbm.at[pl.ds(tile_idx * ROWS_PER_TILE, ROWS_PER_TILE), t, :]
        return pltpu.make_async_copy(src, xt_ref.at[t], xt_sem.at[t])

    @pl.when(j == 0)
    def _():
        @pl.when(tile == 0)
        def _():
            for t in range(t_early):
                gather(0, t).start()
        for t in range(t_early, CHUNK):
            gather(tile, t).start()

    t_base = j * t_per_sub
    for d in range(t_per_sub):
        gather(tile, t_base + d).wait()

    @pl.when((j == N_SUB - 1) & (tile + 1 < pl.num_programs(0)))
    def _():
        for t in range(t_early):
            gather(tile + 1, t).start()

    @pl.when((tile == 0) & (j == 0))
    def _():
        wut_ref[...] = w_ref[:, 3 * ATT_WIDTH:].T

    shift = mod_ref[:, 0:D_MODEL]
    gain = g_ref[...] * (1.0 + mod_ref[:, D_MODEL:2 * D_MODEL])

    def norm_mod(xv):
        ms = jnp.mean(xv * xv, axis=-1, keepdims=True)
        return (xv * lax.rsqrt(ms + NORM_EPS) * gain + shift).astype(BF16)

    def ssm_input(d0):
        xt = jnp.concatenate([xt_ref[t_base + d0], xt_ref[t_base + d0 + 1]], axis=0)
        ut = lax.dot_general(wut_ref[...], norm_mod(xt), NT_DIMS,
                             preferred_element_type=F32)
        for d in range(2):
            blk = ut[:, d * ROWS_PER_TILE:(d + 1) * ROWS_PER_TILE]
            row0 = pl.multiple_of((t_base + d0 + d) * SSM_GROUP, SSM_GROUP)
            ut_ref[:, pl.ds(row0, SSM_GROUP), :] = (
                blk.reshape(N_GROUPS, SSM_GROUP, ROWS_PER_TILE).astype(ut_ref.dtype))

    proj = jnp.dot(norm_mod(x_ref[...]), w_ref[:, 0:3 * ATT_WIDTH],
                   preferred_element_type=F32)
    q = proj[:, 0:ATT_WIDTH]
    k = proj[:, ATT_WIDTH:2 * ATT_WIDTH]
    v = proj[:, 2 * ATT_WIDTH:3 * ATT_WIDTH]
    qscale = HEAD_DIM ** -0.5 * math.log2(math.e)
    if rope:
        cos, sa, sb = cos_ref[...], sa_ref[...], sb_ref[...]
        for hd in range(N_HEADS):
            sl = slice(hd * HEAD_W, (hd + 1) * HEAD_W)
            q_ref[:, sl] = (_rope(q[:, sl], cos, sa, sb) * qscale).astype(q_ref.dtype)
            k_ref[:, sl] = _rope(k[:, sl], cos, sa, sb).astype(k_ref.dtype)
    else:
        q_ref[...] = (q * qscale).astype(q_ref.dtype)
        k_ref[...] = k.astype(k_ref.dtype)
        seq = SUB_TOK // seqs_per_sub
        k_t = k.T
        for b in range(seqs_per_sub):
            kc_ref[b] = k_t[:, b * seq:(b + 1) * seq]
            for hd in range(N_HEADS):
                vc_ref[b, :, hd, :] = v[b * seq:(b + 1) * seq, hd * HEAD_W:(hd + 1) * HEAD_W]
    v_ref[...] = v.astype(v_ref.dtype)

    for d0 in range(0, t_per_sub, 2):
        ssm_input(d0)


def _in_proj(x, mods, mod_rows, g0, w_in, rope_tabs):
    n_batch, seq_len = x.shape[:2]
    n_tok = n_batch * seq_len
    n_rows = n_tok // CHUNK
    n_tiles = n_tok // TOK_PER_TILE
    mod_row0, n_mod = mod_rows
    tiles_per_mod = n_tiles // n_mod
    rope = rope_tabs is not None
    seqs_per_sub = max(1, SUB_TOK // seq_len)
    in_specs = [pl.BlockSpec((SUB_TOK, D_MODEL), lambda i, j: (i * N_SUB + j, 0)),
                pl.BlockSpec(memory_space=pl.ANY),
                pl.BlockSpec((None, 1, 2 * D_MODEL),
                             lambda i, j: (mod_row0 + i // tiles_per_mod, 0, 0)),
                pl.BlockSpec((1, D_MODEL), lambda i, j: (0, 0)),
                pl.BlockSpec((D_MODEL, 4 * ATT_WIDTH), lambda i, j: (0, 0))]
    args = [x.reshape(n_tok, D_MODEL), x.reshape(n_rows, CHUNK, D_MODEL), mods, g0, w_in]
    row_spec = pl.BlockSpec((SUB_TOK, ATT_WIDTH), lambda i, j: (i * N_SUB + j, 0))
    row_shape = jax.ShapeDtypeStruct((n_tok, ATT_WIDTH), BF16)
    out_specs = [row_spec, row_spec, row_spec]
    out_shape = [row_shape, row_shape, row_shape]
    if rope:
        assert seq_len == TOK_PER_TILE
        for tab in rope_tabs:
            in_specs.append(pl.BlockSpec((SUB_TOK, HEAD_W), lambda i, j: (j, 0)))
            args.append(tab)
    else:
        out_specs += [pl.BlockSpec((seqs_per_sub, ATT_WIDTH, seq_len),
                                   lambda i, j: (i * N_SUB + j, 0, 0)),
                      pl.BlockSpec((seqs_per_sub, seq_len, N_HEADS, HEAD_W),
                                   lambda i, j: (i * N_SUB + j, 0, 0, 0))]
        out_shape += [jax.ShapeDtypeStruct((n_batch, ATT_WIDTH, seq_len), F32),
                      jax.ShapeDtypeStruct((n_batch, seq_len, N_HEADS, HEAD_W), F32)]
    out_specs.append(pl.BlockSpec((N_GROUPS, CW, ROWS_PER_TILE), lambda i, j: (0, 0, i)))
    out_shape.append(jax.ShapeDtypeStruct((N_GROUPS, CW, n_rows), BF16))
    return pl.pallas_call(
        functools.partial(_in_proj_kernel, rope=rope, seqs_per_sub=seqs_per_sub),
        grid=(n_tiles, N_SUB),
        in_specs=in_specs,
        out_specs=out_specs,
        out_shape=out_shape,
        scratch_shapes=[pltpu.VMEM((SSM_WIDTH, D_MODEL), BF16),
                        pltpu.VMEM((CHUNK, ROWS_PER_TILE, D_MODEL), F32),
                        pltpu.SemaphoreType.DMA((CHUNK,))],
        compiler_params=_cparams(2),
        name="in_proj",
    )(*args)


def _attn_kernel(*refs, has_ctx, lam_init, n_seq, seq_len, tq, n_cast):
    if n_cast:
        cast_in = refs[len(refs) - 2 * n_cast - 1:len(refs) - n_cast - 1]
        cast_out = refs[len(refs) - n_cast:]
        refs = refs[:len(refs) - 2 * n_cast - 1] + (refs[len(refs) - n_cast - 1],)
        for src, dst in zip(cast_in, cast_out):
            dst[...] = src[...].astype(dst.dtype)
    if has_ctx:
        lam_ref, sg_ref, q_ref, ck_ref, cv_ref, k_ref, v_ref, o_ref = refs
    else:
        lam_ref, sg_ref, q_ref, k_ref, v_ref, o_ref = refs
    lp = lam_ref[...]
    lam = (jnp.exp(jnp.sum(lp[0:1] * lp[1:2], axis=-1, keepdims=True))
           - jnp.exp(jnp.sum(lp[2:3] * lp[3:4], axis=-1, keepdims=True)) + lam_init)
    first_map = lax.broadcasted_iota(jnp.int32, (1, HEAD_W), 1) < HEAD_DIM
    ti = min(TQ_ITEM, tq)
    for row0 in range(0, n_seq * tq, ti):
        b = row0 // tq
        q_rows = slice(row0, row0 + ti)
        for hd in range(N_HEADS):
            sl = slice(hd * HEAD_W, (hd + 1) * HEAD_W)
            qh = q_ref[q_rows, sl]
            zero = jnp.zeros_like(qh)
            qs = jnp.concatenate([jnp.where(first_map, qh, zero),
                                  jnp.where(first_map, zero, qh)], axis=0)
            kv_rows = slice(b * seq_len, (b + 1) * seq_len)
            parts = [(k_ref[kv_rows, sl], v_ref[kv_rows, sl])]
            scores = [lax.dot_general(qs, parts[0][0], NT_DIMS, preferred_element_type=F32)]
            if has_ctx:
                parts.insert(0, (None, cv_ref[:, hd, :].astype(BF16)))
                scores.insert(0, jnp.dot(qs, ck_ref[sl, :].astype(BF16),
                                         preferred_element_type=F32))
            mx = scores[0].max(axis=-1, keepdims=True)
            for s in scores[1:]:
                mx = jnp.maximum(mx, s.max(axis=-1, keepdims=True))
            acc = None
            for s, (_, vv) in zip(scores, parts):
                e = jnp.exp2(s - mx).astype(BF16)
                v_one = jnp.concatenate([vv, jnp.ones_like(vv)], axis=1)
                pv = jnp.dot(e, v_one, preferred_element_type=F32)
                acc = pv if acc is None else acc + pv
            num = acc[:, 0:HEAD_W] / acc[:, HEAD_W:2 * HEAD_W]
            o = num[0:ti] - lam * num[ti:2 * ti]
            o = _rms(o, sg_ref[...]) * (1.0 - lam_init)
            o_ref[q_rows, sl] = o.astype(o_ref.dtype)


def _attention(q, k, v, ctx_k, ctx_v, lam_params, subln_g, n_batch, seq_len, lam_init,
               cast_weights=()):
    has_ctx = ctx_k is not None
    tq = min(1024, seq_len)
    n_q = seq_len // tq
    n_seq = 1 if n_q > 1 else min(4, n_batch)
    in_specs = [pl.BlockSpec((4, HEAD_DIM), lambda b, i: (0, 0)),
                pl.BlockSpec((1, HEAD_W), lambda b, i: (0, 0)),
                pl.BlockSpec((n_seq * tq, ATT_WIDTH), lambda b, i: (b * n_q + i, 0))]
    args = [lam_params, subln_g.reshape(1, HEAD_W), q]
    if has_ctx:
        past = ctx_v.shape[1]
        in_specs += [pl.BlockSpec((None, ATT_WIDTH, past), lambda b, i: (b, 0, 0)),
                     pl.BlockSpec((None, past, N_HEADS, HEAD_W), lambda b, i: (b, 0, 0, 0))]
        args += [ctx_k, ctx_v]
    kv_spec = pl.BlockSpec((n_seq * seq_len, ATT_WIDTH), lambda b, i: (b, 0))
    in_specs += [kv_spec, kv_spec]
    args += [k, v]
    out_specs = [pl.BlockSpec((n_seq * tq, ATT_WIDTH), lambda b, i: (b * n_q + i, 0))]
    out_shape = [jax.ShapeDtypeStruct((n_batch * seq_len, ATT_WIDTH), BF16)]
    n_steps = (n_batch // n_seq) * n_q
    for w in cast_weights:
        rows = w.shape[0] // n_steps
        spec = pl.BlockSpec((rows, w.shape[1]), lambda b, i: (b * n_q + i, 0))
        in_specs.append(spec)
        args.append(w)
        out_specs.append(spec)
        out_shape.append(jax.ShapeDtypeStruct(w.shape, BF16))
    outs = pl.pallas_call(
        functools.partial(_attn_kernel, has_ctx=has_ctx, lam_init=lam_init,
                          n_seq=n_seq, seq_len=seq_len, tq=tq, n_cast=len(cast_weights)),
        grid=(n_batch // n_seq, n_q),
        in_specs=in_specs,
        out_specs=out_specs,
        out_shape=out_shape,
        compiler_params=_cparams(2),
        name="diff_attention",
    )(*args)
    return outs[0], tuple(outs[1:])


def _cmul(ar, ai, br, bi):
    return ar * br - ai * bi, ar * bi + ai * br


def _ssm_prep_kernel(lre_ref, lim_ref, ls_ref, bre_ref, bim_ref, cre_ref, cim_ref, d_ref,
                     mt_ref, gt_ref, wo_ref, at_ref):
    lane = lax.broadcasted_iota(jnp.int32, (SSM_GROUP, CW), 1)
    chan = lax.broadcasted_iota(jnp.int32, (SSM_GROUP, CW), 0)
    for gi in range(GROUP_BLOCK):
        gt_cols, wo_cols, at_cols, toeplitz = [], [], [], []
        for dr in range(2):
            lr = jnp.minimum(lre_ref[dr, gi], -1e-4)
            li = lim_ref[dr, gi]
            step = jnp.exp(ls_ref[dr, gi])
            mag = jnp.exp(lr * step)
            a_re = mag * jnp.cos(li * step)
            a_im = mag * jnp.sin(li * step)
            den = lr * lr + li * li
            nr = a_re - 1.0
            f_re = (nr * lr + a_im * li) / den
            f_im = (a_im * lr - nr * li) / den
            bt_re, bt_im = bre_ref[dr, gi], bim_ref[dr, gi]
            bb_re, bb_im = _cmul(f_re, f_im, bt_re, bt_im)
            c_re, c_im = cre_ref[dr, gi], cim_ref[dr, gi]
            pw = [(jnp.ones_like(a_re), jnp.zeros_like(a_im))]
            for _ in range(CHUNK):
                pw.append(_cmul(pw[-1][0], pw[-1][1], a_re, a_im))
            g_re, g_im, e_re, e_im = [], [], [], []
            for t in range(CHUNK):
                pr, pi = pw[CHUNK - 1 - t] if dr == 0 else pw[t]
                r, i = _cmul(bb_re, bb_im, pr, pi)
                g_re.append(r)
                g_im.append(i)
                pr, pi = pw[t + 1] if dr == 0 else pw[CHUNK - t]
                r, i = _cmul(c_re, c_im, pr, pi)
                e_re.append(r)
                e_im.append(-i)
            g_cat = jnp.concatenate([jnp.concatenate(g_re, axis=0),
                                     jnp.concatenate(g_im, axis=0)], axis=1)
            gt_cols.append(g_cat)
            wo_cols.append(jnp.concatenate([jnp.concatenate(e_re, axis=0),
                                            jnp.concatenate(e_im, axis=0)], axis=1))
            c_cat = jnp.concatenate([c_re, -c_im], axis=1)
            toeplitz.append(lax.dot_general(c_cat, g_cat, NT_DIMS,
                                            precision=lax.Precision.HIGHEST,
                                            preferred_element_type=F32))
            apw = [pw[CHUNK]]
            for _ in range(SCAN_BLOCK - 1):
                apw.append(_cmul(apw[-1][0], apw[-1][1], apw[0][0], apw[0][1]))
            order = list(range(SCAN_BLOCK)) if dr == 0 else list(range(SCAN_BLOCK - 1, -1, -1))
            order += [2 ** l - 1 for l in range(SCAN_LEVELS)]
            order += [0] * (AT_ROWS - len(order))
            at_cols += [jnp.concatenate([jnp.concatenate([apw[i][0], apw[i][0]], axis=1)
                                         for i in order], axis=0),
                        jnp.concatenate([jnp.concatenate([-apw[i][1], apw[i][1]], axis=1)
                                         for i in order], axis=0)]
        kf_rev, kb = toeplitz
        d_skip = d_ref[gi]
        blocks = []
        for t in range(CHUNK):
            fwd = pltpu.roll(kf_rev, (CW - (CHUNK - 1 - t) * SSM_GROUP) % CW, axis=1)
            bwd = pltpu.roll(kb, t * SSM_GROUP, axis=1)
            blocks.append(jnp.where(lane < (t + 1) * SSM_GROUP, fwd, 0.0)
                          + jnp.where(lane >= t * SSM_GROUP, bwd, 0.0)
                          + jnp.where(lane == chan + t * SSM_GROUP, d_skip, 0.0))
        mt_ref[gi] = jnp.concatenate(blocks, axis=0).astype(mt_ref.dtype)
        gt_ref[gi] = jnp.concatenate(gt_cols, axis=1).astype(gt_ref.dtype)
        wo_ref[gi] = jnp.concatenate(wo_cols, axis=1).astype(wo_ref.dtype)
        at_ref[gi] = jnp.concatenate(at_cols, axis=1)


N_PREP_IN, N_PREP_OUT, N_MOD_IN = 8, 4, 4


def _prep_mod_kernel(*refs):
    prep_in = refs[:N_PREP_IN]
    mod_in = refs[N_PREP_IN:N_PREP_IN + N_MOD_IN]
    w_in_ref = refs[N_PREP_IN + N_MOD_IN]
    outs = refs[N_PREP_IN + N_MOD_IN + 1:]
    _ssm_prep_kernel(*prep_in, *outs[:N_PREP_OUT])
    _mod_kernel(*mod_in, outs[N_PREP_OUT])
    outs[N_PREP_OUT + 1][...] = w_in_ref[...].astype(BF16)


def _ssm_prep_and_modulation(lam_re, lam_im, log_step, b_re, b_im, c_re, c_im, d_skip,
                             c_ctx, c, w_mod, b_mod, w_in):
    row = lambda a: a.reshape(2, N_GROUPS, 1, SSM_STATE)
    bt = lambda a: jnp.swapaxes(a, 2, 3)
    d_row = jnp.tile((d_skip[0] + d_skip[1]).reshape(N_GROUPS, 1, SSM_GROUP), (1, 1, CHUNK))
    gb = GROUP_BLOCK
    n_steps = N_GROUPS // gb
    vec_spec = pl.BlockSpec((2, gb, 1, SSM_STATE), lambda i: (0, i, 0, 0))
    mat_spec = pl.BlockSpec((2, gb, SSM_GROUP, SSM_STATE), lambda i: (0, i, 0, 0))
    w_spec = pl.BlockSpec((gb, CW, CW), lambda i: (i, 0, 0))
    w_shape = jax.ShapeDtypeStruct((N_GROUPS, CW, CW), BF16)
    n_mod = w_mod.shape[1]
    tk = D_MODEL // n_steps
    assert 1 + c.shape[0] <= MOD_ROWS
    outs = pl.pallas_call(
        _prep_mod_kernel,
        grid=(n_steps,),
        in_specs=[vec_spec, vec_spec,
                  pl.BlockSpec((2, gb, 1, 1), lambda i: (0, i, 0, 0)),
                  mat_spec, mat_spec, mat_spec, mat_spec,
                  pl.BlockSpec((gb, 1, CW), lambda i: (i, 0, 0)),
                  pl.BlockSpec((1, tk), lambda k: (0, k)),
                  pl.BlockSpec((c.shape[0], tk), lambda k: (0, k)),
                  pl.BlockSpec((tk, n_mod), lambda k: (k, 0)),
                  pl.BlockSpec((1, n_mod), lambda k: (0, 0)),
                  pl.BlockSpec((tk, w_in.shape[1]), lambda k: (k, 0))],
        out_specs=[w_spec, w_spec, w_spec,
                   pl.BlockSpec((gb, AT_ROWS, 4 * 2 * SSM_STATE), lambda i: (i, 0, 0)),
                   pl.BlockSpec((MOD_ROWS, 1, n_mod), lambda k: (0, 0, 0)),
                   pl.BlockSpec((tk, w_in.shape[1]), lambda k: (k, 0))],
        out_shape=[w_shape, w_shape, w_shape,
                   jax.ShapeDtypeStruct((N_GROUPS, AT_ROWS, 4 * 2 * SSM_STATE), F32),
                   jax.ShapeDtypeStruct((MOD_ROWS, 1, n_mod), F32),
                   jax.ShapeDtypeStruct(w_in.shape, BF16)],
        compiler_params=_cparams(1),
        name="ssm_prep_modulation",
    )(row(lam_re), row(lam_im), log_step.reshape(2, N_GROUPS, 1, 1),
      bt(b_re), bt(b_im), c_re, c_im, d_row,
      c_ctx.reshape(1, D_MODEL), c, w_mod, b_mod.reshape(1, n_mod), w_in)
    return outs[:N_PREP_OUT], outs[N_PREP_OUT], outs[N_PREP_OUT + 1]


def _shift_rows(x, m, down):
    n = x.shape[0]
    return pltpu.roll(x, m if down else n - m, axis=0)


def _ssm_kernel(*refs, n_seq, has_h0):
    if has_h0:
        (xt_ref, mt_ref, gt_ref, wo_ref, at_ref, wg_ref, bg_ref, h0_ref,
         z_hbm, zs_ref, zb_ref, zb_sem) = refs
    else:
        (xt_ref, mt_ref, gt_ref, wo_ref, at_ref, wg_ref, bg_ref,
         z_hbm, st_ref, zs_ref, zb_ref, zb_sem, fin_ref) = refs
    step = pl.program_id(0)
    n_rows = xt_ref.shape[-1]
    seg = n_rows // n_seq
    cw2 = 2 * SSM_STATE
    assert seg % SCAN_BLOCK == 0
    n_blk = n_rows // SCAN_BLOCK
    blk_per_seq = seg // SCAN_BLOCK
    pos = lax.broadcasted_iota(jnp.int32, (n_rows, cw2), 0) % seg
    row_blk = lax.broadcasted_iota(jnp.int32, (SCAN_BLOCK, cw2), 0)

    def low_half(shape):
        return lax.broadcasted_iota(jnp.int32, shape, 1) < SSM_STATE

    def swap(v):
        return pltpu.roll(v, SSM_STATE, axis=1)

    def to_planes(va, vb):
        lo = low_half((va.shape[0], cw2))
        va_l = pltpu.roll(va, 3 * SSM_STATE, axis=1)
        vb_r = pltpu.roll(vb, SSM_STATE, axis=1)
        return (jnp.where(lo, va[:, 0:cw2], vb_r[:, 0:cw2]),
                jnp.where(lo, va_l[:, 0:cw2], vb[:, 0:cw2]),
                jnp.where(lo, va[:, cw2:2 * cw2], vb_r[:, cw2:2 * cw2]),
                jnp.where(lo, va_l[:, cw2:2 * cw2], vb[:, cw2:2 * cw2]))

    def from_planes(f_re, f_im, b_re, b_im):
        lo = low_half(f_re.shape)
        va = jnp.concatenate([jnp.where(lo, f_re, swap(f_im)), jnp.where(lo, b_re, swap(b_im))], axis=1)
        vb = jnp.concatenate([jnp.where(lo, swap(f_re), f_im), jnp.where(lo, swap(b_re), b_im)], axis=1)
        return va, vb

    for ga in range(0, GROUP_BLOCK, 2):
        gb = ga + 1
        xts = [xt_ref[ga], xt_ref[gb]]
        s_pair = [lax.dot_general(xts[i], gt_ref[g], TN_DIMS, preferred_element_type=F32)
                  for i, g in enumerate((ga, gb))]
        planes = to_planes(*s_pair)
        lo_t = low_half((AT_ROWS, cw2))
        if has_h0:
            h0_planes = to_planes(h0_ref[ga], h0_ref[gb])
        ent_planes = []
        for dr in range(2):
            down = dr == 0
            pa, pb = (at_ref[g][:, (2 * dr) * cw2:(2 * dr + 1) * cw2] for g in (ga, gb))
            qa, qb = (at_ref[g][:, (2 * dr + 1) * cw2:(2 * dr + 2) * cw2] for g in (ga, gb))
            ar_tab = jnp.where(lo_t, pa, pb)
            ai_tab = jnp.where(lo_t, -qa, qb)
            re3 = planes[2 * dr].reshape(n_blk, SCAN_BLOCK, cw2)
            im3 = planes[2 * dr + 1].reshape(n_blk, SCAN_BLOCK, cw2)
            for lvl in range(SCAN_LEVELS):
                m = 2 ** lvl
                valid = (row_blk >= m) if down else (row_blk < SCAN_BLOCK - m)
                row = SCAN_BLOCK + lvl
                ar = jnp.where(valid, ar_tab[row:row + 1], 0.0)
                ai = jnp.where(valid, ai_tab[row:row + 1], 0.0)
                shift = m if down else SCAN_BLOCK - m
                sh_re = pltpu.roll(re3, shift, axis=1)
                sh_im = pltpu.roll(im3, shift, axis=1)
                re3, im3 = re3 + ar * sh_re - ai * sh_im, im3 + ar * sh_im + ai * sh_re
            ar_blk, ai_blk = ar_tab[0:SCAN_BLOCK], ai_tab[0:SCAN_BLOCK]
            blk_re = [re3[i] for i in range(n_blk)]
            blk_im = [im3[i] for i in range(n_blk)]
            edge = slice(SCAN_BLOCK - 1, SCAN_BLOCK) if down else slice(0, 1)
            for q_i in range(n_seq):
                idxs = list(range(q_i * blk_per_seq, (q_i + 1) * blk_per_seq))
                idxs = idxs if down else idxs[::-1]
                for prev, cur in zip([None] + idxs[:-1], idxs):
                    if prev is not None:
                        c_re, c_im = blk_re[prev][edge], blk_im[prev][edge]
                    elif has_h0:
                        c_re = h0_planes[2 * dr][q_i:q_i + 1]
                        c_im = h0_planes[2 * dr + 1][q_i:q_i + 1]
                    else:
                        continue
                    c_re = jnp.broadcast_to(c_re, (SCAN_BLOCK, cw2))
                    c_im = jnp.broadcast_to(c_im, (SCAN_BLOCK, cw2))
                    blk_re[cur] = blk_re[cur] + ar_blk * c_re - ai_blk * c_im
                    blk_im[cur] = blk_im[cur] + ar_blk * c_im + ai_blk * c_re
            for part, blks in ((0, blk_re), (1, blk_im)):
                s = jnp.concatenate(blks, axis=0)
                if not has_h0:
                    fin_ref[part] = s
                ent = _shift_rows(s, 1, down)
                ent = jnp.where((pos >= 1) if down else (pos < seg - 1), ent, 0.0)
                if has_h0:
                    ent_blk = [ent[i * SCAN_BLOCK:(i + 1) * SCAN_BLOCK] for i in range(n_blk)]
                    at_edge = row_blk == (0 if down else SCAN_BLOCK - 1)
                    for q_i in range(n_seq):
                        bi = q_i * blk_per_seq if down else (q_i + 1) * blk_per_seq - 1
                        ent_blk[bi] = jnp.where(
                            at_edge, h0_planes[2 * dr + part][q_i:q_i + 1], ent_blk[bi])
                    ent = jnp.concatenate(ent_blk, axis=0)
                ent_planes.append(ent)
            if not has_h0:
                rows = pl.ds(seg - 1 if down else 0, n_seq, stride=seg)
                f_re, f_im = fin_ref[0, rows, :], fin_ref[1, rows, :]
                lo_s = low_half((n_seq, cw2))
                st_ref[ga, :, dr * cw2:(dr + 1) * cw2] = jnp.where(lo_s, f_re, swap(f_im))
                st_ref[gb, :, dr * cw2:(dr + 1) * cw2] = jnp.where(lo_s, swap(f_re), f_im)
        h_pair = from_planes(*ent_planes)
        for i, g in enumerate((ga, gb)):
            yt = (jnp.dot(mt_ref[g], xts[i], preferred_element_type=F32)
                  + lax.dot_general(wo_ref[g], h_pair[i].astype(BF16), NT_DIMS,
                                    preferred_element_type=F32))
            z = jax.nn.gelu(yt, approximate=True)
            grp = step * GROUP_BLOCK + g
            for t in range(CHUNK):
                zs_ref[t, pl.ds(pl.multiple_of(grp * SSM_GROUP, SSM_GROUP), SSM_GROUP), :] = (
                    z[t * SSM_GROUP:(t + 1) * SSM_GROUP, :])

    @pl.when(step == pl.num_programs(0) - 1)
    def _():
        def put(t):
            return pltpu.make_async_copy(zb_ref.at[t], z_hbm.at[:, t, :], zb_sem.at[t])

        for t0 in range(0, CHUNK, Z_BATCH):
            for t in range(t0, t0 + Z_BATCH):
                zt = zs_ref[t]
                gate = jnp.dot(wg_ref[...], zt.astype(BF16),
                               preferred_element_type=F32) + bg_ref[...]
                zb_ref[t] = (zt * jax.nn.sigmoid(gate)).T
            for t in range(t0, t0 + Z_BATCH):
                put(t).start()
        for t in range(CHUNK):
            put(t).wait()


def _ssm(xt, mt, gt, wo, at, w_glu_t, b_glu_col, h0, n_seq):
    n_rows = xt.shape[-1]
    has_h0 = h0 is not None
    gb = GROUP_BLOCK
    w_spec = pl.BlockSpec((gb, CW, CW), lambda i: (i, 0, 0))
    in_specs = [pl.BlockSpec((gb, CW, n_rows), lambda i: (i, 0, 0)),
                w_spec, w_spec, w_spec,
                pl.BlockSpec((gb, AT_ROWS, 4 * 2 * SSM_STATE), lambda i: (i, 0, 0)),
                pl.BlockSpec((SSM_WIDTH, SSM_WIDTH), lambda i: (0, 0)),
                pl.BlockSpec((SSM_WIDTH, 1), lambda i: (0, 0))]
    args = [xt, mt, gt, wo, at, w_glu_t, b_glu_col]
    out_specs = [pl.BlockSpec(memory_space=pl.ANY)]
    out_shape = [jax.ShapeDtypeStruct((n_rows, CHUNK, SSM_WIDTH), F32)]
    scratch = [pltpu.VMEM((CHUNK, SSM_WIDTH, n_rows), F32),
               pltpu.VMEM((CHUNK, n_rows, SSM_WIDTH), F32),
               pltpu.SemaphoreType.DMA((CHUNK,))]
    if has_h0:
        in_specs.append(pl.BlockSpec((gb, n_seq, CW), lambda i: (i, 0, 0)))
        args.append(h0)
    else:
        out_specs.append(pl.BlockSpec((gb, n_seq, CW), lambda i: (i, 0, 0)))
        out_shape.append(jax.ShapeDtypeStruct((N_GROUPS, n_seq, CW), F32))
        scratch.append(pltpu.VMEM((2, n_rows, 2 * SSM_STATE), F32))
    return pl.pallas_call(
        functools.partial(_ssm_kernel, n_seq=n_seq, has_h0=has_h0),
        grid=(N_GROUPS // gb,),
        in_specs=in_specs,
        out_specs=out_specs,
        out_shape=out_shape,
        scratch_shapes=scratch,
        compiler_params=_cparams(1),
        name="ssm_scan_glu",
    )(*args)


FF_CHUNK = 256
FFN_TOK = 1024


def _out_ffn_kernel(x_ref, attn_ref, z_ref, mod_ref, g_ref, wo_ref, wfi_ref, wfo_ref,
                    o_ref, act_ref):
    gate1 = mod_ref[:, 2 * D_MODEL:3 * D_MODEL]
    shift2 = mod_ref[:, 3 * D_MODEL:4 * D_MODEL]
    scale2 = mod_ref[:, 4 * D_MODEL:5 * D_MODEL]
    gate2 = mod_ref[:, 5 * D_MODEL:6 * D_MODEL]
    half = FFN_TOK // 2
    gain2 = g_ref[2:3, :] * (1.0 + scale2)

    def pre_ffn(hf):
        r = slice(hf * half, (hf + 1) * half)
        mixer = jnp.concatenate([attn_ref[r, :], z_ref[r, :].astype(BF16)], axis=1)
        mix = jnp.dot(mixer, wo_ref[...], preferred_element_type=F32)
        x1 = x_ref[r, :] + gate1 * _rms(mix, g_ref[1:2, :])
        ms = jnp.mean(x1 * x1, axis=-1, keepdims=True)
        return x1, (x1 * lax.rsqrt(ms + NORM_EPS) * gain2 + shift2).astype(BF16)

    def ffn_in(hf, h, chunks):
        r = slice(hf * half, (hf + 1) * half)
        for c in chunks:
            lo = c * FF_CHUNK
            gt = jnp.dot(h, wfi_ref[:, lo:lo + FF_CHUNK], preferred_element_type=F32)
            up = jnp.dot(h, wfi_ref[:, D_FF + lo:D_FF + lo + FF_CHUNK],
                         preferred_element_type=F32)
            act_ref[r, lo:lo + FF_CHUNK] = (_silu(gt) * up).astype(BF16)

    def ffn_out(hf, x1):
        r = slice(hf * half, (hf + 1) * half)
        f = jnp.dot(act_ref[r, :], wfo_ref[...], preferred_element_type=F32)
        o_ref[r, :] = x1 + gate2 * _rms(f, g_ref[3:4, :])

    n_chunks = D_FF // FF_CHUNK
    x1_a, h_a = pre_ffn(0)
    x1_b, h_b = pre_ffn(1)
    ffn_in(0, h_a, range(n_chunks))
    ffn_out(0, x1_a)
    ffn_in(1, h_b, range(n_chunks))
    ffn_out(1, x1_b)


def _out_ffn(x2d, attn, z, mods, mod_rows, norm_g, w_o, w_ffn_in, w_ffn_out):
    n_tok = x2d.shape[0]
    z2d = z.reshape(n_tok, SSM_WIDTH)
    n_steps = n_tok // FFN_TOK
    mod_row0, n_mod = mod_rows
    steps_per_mod = n_steps // n_mod
    const = lambda i: (0, 0)
    row_spec = lambda w: pl.BlockSpec((FFN_TOK, w), lambda i: (i, 0))
    return pl.pallas_call(
        _out_ffn_kernel,
        grid=(n_steps,),
        in_specs=[row_spec(D_MODEL), row_spec(ATT_WIDTH), row_spec(SSM_WIDTH),
                  pl.BlockSpec((None, 1, N_MOD * D_MODEL),
                               lambda i: (mod_row0 + i // steps_per_mod, 0, 0)),
                  pl.BlockSpec((4, D_MODEL), const),
                  pl.BlockSpec((2 * ATT_WIDTH, D_MODEL), const, pipeline_mode=pl.Buffered(1)),
                  pl.BlockSpec((D_MODEL, 2 * D_FF), const, pipeline_mode=pl.Buffered(1)),
                  pl.BlockSpec((D_FF, D_MODEL), const, pipeline_mode=pl.Buffered(1))],
        out_specs=row_spec(D_MODEL),
        out_shape=jax.ShapeDtypeStruct((n_tok, D_MODEL), F32),
        scratch_shapes=[pltpu.VMEM((FFN_TOK, D_FF), BF16)],
        compiler_params=_cparams(1),
        name="out_proj_ffn",
    )(x2d, attn, z2d, mods, norm_g, w_o, w_ffn_in, w_ffn_out)


def _rope_tables(seq_len):
    t = np.arange(seq_len)
    row = (t // GRID_W).astype(np.float32)
    col = (t % GRID_W).astype(np.float32)
    half = HEAD_DIM // 2
    inv_freq = (np.float32(ROPE_BASE)
                ** (-np.arange(0, half, 2, dtype=np.float32) / np.float32(half))).astype(np.float32)
    ang_r = row[:, None] * inv_freq
    ang_c = col[:, None] * inv_freq
    ang = np.concatenate([ang_r, ang_r, ang_c, ang_c], axis=-1)
    cos, sin = np.cos(ang), np.sin(ang)
    upper = (np.arange(HEAD_DIM) % 32) < 16
    sa = np.where(upper, -sin, 0.0)
    sb = np.where(upper, 0.0, sin)
    two = lambda a: jnp.asarray(np.concatenate([a, a], axis=-1), dtype=F32)
    return two(cos), two(sa), two(sb)


def _layer(x, mods, mod_rows, lam_init, rope_tabs, ctx_k, ctx_v, h0, weights, prep):
    n_batch, seq_len = x.shape[:2]
    g = weights['norm_g']
    outs = _in_proj(x, mods, mod_rows, g[0:1], weights['w_in'], rope_tabs)
    q, k, v = outs[:3]
    pending = () if 'late_bf16' in weights else weights['late_f32']
    attn, cast = _attention(q, k, v, ctx_k, ctx_v, weights['lam'], weights['subln_g'],
                            n_batch, seq_len, lam_init, cast_weights=pending)
    if pending:
        weights['late_bf16'] = cast
    ssm_out = _ssm(outs[-1], *prep, weights['w_glu_t'], weights['b_glu_col'], h0, n_batch)
    y = _out_ffn(x.reshape(n_batch * seq_len, D_MODEL), attn, ssm_out[0], mods, mod_rows, g,
                 *weights['late_bf16'])
    return y.reshape(x.shape), outs[3:-1], ssm_out[1:]


def kernel(x_prompt, x_sample, cache_k, cache_v, state_ssm_re, state_ssm_im, c, c_ctx, w_mod, b_mod, norm_g, w_in, lam_params, subln_g, ssm_lambda_re, ssm_lambda_im, ssm_log_step, ssm_b_re, ssm_b_im, ssm_c_re, ssm_c_im, ssm_d, w_glu, b_glu, w_o, w_ffn_in, w_ffn_out):
    depth = w_mod.shape[0]
    assert depth == 1
    bp = x_prompt.shape[0]
    bd, ld_len = x_sample.shape[:2]
    past = cache_k.shape[2]
    xp, xs = x_prompt, x_sample
    rope_tabs = _rope_tables(ld_len)
    ks_out, vs_out, hr_out, hi_out = [], [], [], []
    for l in range(depth):
        lam_init = 0.8 - 0.6 * math.exp(-0.3 * l)
        prep, mods, w_in_bf16 = _ssm_prep_and_modulation(
            ssm_lambda_re[l], ssm_lambda_im[l], ssm_log_step[l], ssm_b_re[l], ssm_b_im[l],
            ssm_c_re[l], ssm_c_im[l], ssm_d[l], c_ctx, c, w_mod[l], b_mod[l], w_in[l])
        weights = {
            'norm_g': norm_g[l],
            'w_in': w_in_bf16,
            'lam': lam_params[l], 'subln_g': subln_g[l],
            'w_glu_t': w_glu[l].T.astype(BF16), 'b_glu_col': b_glu[l].reshape(SSM_WIDTH, 1),
            'late_f32': (w_o[l], w_ffn_in[l], w_ffn_out[l]),
        }
        ck = jnp.transpose(cache_k[:, l], (0, 2, 3, 1)).reshape(bd, ATT_WIDTH, past)
        cv = cache_v[:, l]
        h0 = jnp.stack([state_ssm_re[:, l], state_ssm_im[:, l]], axis=2)
        h0 = h0.transpose(3, 0, 1, 2, 4).reshape(N_GROUPS, bd, CW)
        xs, _, _ = _layer(xs, mods, (1, bd), lam_init, rope_tabs, ck, cv, h0, weights, prep)
        xp, (k_ctx, v_ctx), (st,) = _layer(xp, mods, (0, 1), lam_init, None, None, None, None,
                                           weights, prep)
        ks_out.append(jnp.swapaxes(k_ctx, 1, 2).reshape(bp, -1, 2 * N_HEADS, HEAD_DIM))
        vs_out.append(v_ctx)
        fin = st.reshape(N_GROUPS, bp, 2, 2, SSM_STATE).transpose(1, 2, 3, 0, 4)
        hr_out.append(fin[:, :, 0])
        hi_out.append(fin[:, :, 1])
    return (xp, xs, jnp.stack(ks_out, axis=1), jnp.stack(vs_out, axis=1),
            jnp.stack(hr_out, axis=1), jnp.stack(hi_out, axis=1))
```

```python
import functools
import math

import jax
import jax.numpy as jnp
import numpy as np
from jax import lax
from jax.experimental import pallas as pl
from jax.experimental.pallas import tpu as pltpu

F32 = jnp.float32
BF16 = jnp.bfloat16

D_MODEL = 1024
GRID_W = 64
ATT_WIDTH = 512
SSM_WIDTH = 512
HEAD_DIM = 64
N_HEADS = 4
HEAD_W = 2 * HEAD_DIM
SSM_GROUP = 16
N_GROUPS = 32
SSM_STATE = 64
D_FF = 2816
N_MOD = 6
ROPE_BASE = 10000.0
NORM_EPS = 1e-6

CHUNK = 16
CW = CHUNK * SSM_GROUP
SCAN_BLOCK = 8
SCAN_LEVELS = 3
AT_ROWS = 16
GROUP_BLOCK = 4
Z_BATCH = 8

ROWS_PER_TILE = 128
TOK_PER_TILE = ROWS_PER_TILE * CHUNK
SUB_TOK = 1024
TQ_ITEM = 512
N_SUB = TOK_PER_TILE // SUB_TOK

VMEM_LIMIT = 56 * 1024 * 1024

NT_DIMS = (((1,), (1,)), ((), ()))
TN_DIMS = (((0,), (0,)), ((), ()))


def _cparams(n_axes):
    return pltpu.CompilerParams(
        dimension_semantics=("arbitrary",) * n_axes,
        vmem_limit_bytes=VMEM_LIMIT)


def _rms(x, g):
    ms = jnp.mean(x * x, axis=-1, keepdims=True)
    return x * lax.rsqrt(ms + NORM_EPS) * g


def _silu(x):
    return x * jax.nn.sigmoid(x)


MOD_ROWS = 8


def _mod_kernel(ctx_ref, c_ref, w_ref, b_ref, o_ref):
    n_lat, tk = c_ref.shape
    row = lax.broadcasted_iota(jnp.int32, (MOD_ROWS, tk), 0)
    cond = jnp.where(row == 0, ctx_ref[...], 0.0)
    for b in range(n_lat):
        cond = jnp.where(row == 1 + b, c_ref[b:b + 1, :], cond)
    part = jnp.dot(_silu(cond).astype(BF16), w_ref[...].astype(BF16),
                   preferred_element_type=F32)

    @pl.when(pl.program_id(0) == 0)
    def _():
        o_ref[:, 0, :] = part + b_ref[...]

    @pl.when(pl.program_id(0) > 0)
    def _():
        o_ref[:, 0, :] += part


def _rope(x, cos, sa, sb):
    return (x * cos + pltpu.roll(x, HEAD_W - 16, axis=1) * sa
            + pltpu.roll(x, 16, axis=1) * sb)


def _in_proj_kernel(*refs, rope, seqs_per_sub):
    x_ref, x3_hbm, mod_ref, g_ref, w_ref = refs[:5]
    refs = refs[5:]
    if rope:
        cos_ref, sa_ref, sb_ref = refs[:3]
        refs = refs[3:]
        q_ref, k_ref, v_ref, ut_ref, wut_ref, xt_ref, xt_sem = refs
    else:
        q_ref, k_ref, v_ref, kc_ref, vc_ref, ut_ref, wut_ref, xt_ref, xt_sem = refs
    tile = pl.program_id(0)
    j = pl.program_id(1)
    t_per_sub = CHUNK // N_SUB
    t_early = CHUNK - t_per_sub

    def gather(tile_idx, t):
        src = x3_hbm.at[pl.ds(tile_idx * ROWS_PER_TILE, ROWS_PER_TILE), t, :]
        return pltpu.make_async_copy(src, xt_ref.at[t], xt_sem.at[t])

    @pl.when(j == 0)
    def _():
        @pl.when(tile == 0)
        def _():
            for t in range(t_early):
                gather(0, t).start()
        for t in range(t_early, CHUNK):
            gather(tile, t).start()

    t_base = j * t_per_sub
    for d in range(t_per_sub):
        gather(tile, t_base + d).wait()

    @pl.when((j == N_SUB - 1) & (tile + 1 < pl.num_programs(0)))
    def _():
        for t in range(t_early):
            gather(tile + 1, t).start()

    @pl.when((tile == 0) & (j == 0))
    def _():
        wut_ref[...] = w_ref[:, 3 * ATT_WIDTH:].T

    shift = mod_ref[:, 0:D_MODEL]
    gain = g_ref[...] * (1.0 + mod_ref[:, D_MODEL:2 * D_MODEL])

    def norm_mod(xv):
        ms = jnp.mean(xv * xv, axis=-1, keepdims=True)
        return (xv * lax.rsqrt(ms + NORM_EPS) * gain + shift).astype(BF16)

    def ssm_input(d0):
        xt = jnp.concatenate([xt_ref[t_base + d0], xt_ref[t_base + d0 + 1]], axis=0)
        ut = lax.dot_general(wut_ref[...], norm_mod(xt), NT_DIMS,
                             preferred_element_type=F32)
        for d in range(2):
            blk = ut[:, d * ROWS_PER_TILE:(d + 1) * ROWS_PER_TILE]
            row0 = pl.multiple_of((t_base + d0 + d) * SSM_GROUP, SSM_GROUP)
            ut_ref[:, pl.ds(row0, SSM_GROUP), :] = (
                blk.reshape(N_GROUPS, SSM_GROUP, ROWS_PER_TILE).astype(ut_ref.dtype))

    proj = jnp.dot(norm_mod(x_ref[...]), w_ref[:, 0:3 * ATT_WIDTH],
                   preferred_element_type=F32)
    q = proj[:, 0:ATT_WIDTH]
    k = proj[:, ATT_WIDTH:2 * ATT_WIDTH]
    v = proj[:, 2 * ATT_WIDTH:3 * ATT_WIDTH]
    qscale = HEAD_DIM ** -0.5 * math.log2(math.e)
    if rope:
        cos, sa, sb = cos_ref[...], sa_ref[...], sb_ref[...]
        for hd in range(N_HEADS):
            sl = slice(hd * HEAD_W, (hd + 1) * HEAD_W)
            q_ref[:, sl] = (_rope(q[:, sl], cos, sa, sb) * qscale).astype(q_ref.dtype)
            k_ref[:, sl] = _rope(k[:, sl], cos, sa, sb).astype(k_ref.dtype)
    else:
        q_ref[...] = (q * qscale).astype(q_ref.dtype)
        k_ref[...] = k.astype(k_ref.dtype)
        seq = SUB_TOK // seqs_per_sub
        k_t = k.T
        for b in range(seqs_per_sub):
            kc_ref[b] = k_t[:, b * seq:(b + 1) * seq]
            for hd in range(N_HEADS):
                vc_ref[b, :, hd, :] = v[b * seq:(b + 1) * seq, hd * HEAD_W:(hd + 1) * HEAD_W]
    v_ref[...] = v.astype(v_ref.dtype)

    for d0 in range(0, t_per_sub, 2):
        ssm_input(d0)


def _in_proj(x, mods, mod_rows, g0, w_in, rope_tabs):
    n_batch, seq_len = x.shape[:2]
    n_tok = n_batch * seq_len
    n_rows = n_tok // CHUNK
    n_tiles = n_tok // TOK_PER_TILE
    mod_row0, n_mod = mod_rows
    tiles_per_mod = n_tiles // n_mod
    rope = rope_tabs is not None
    seqs_per_sub = max(1, SUB_TOK // seq_len)
    in_specs = [pl.BlockSpec((SUB_TOK, D_MODEL), lambda i, j: (i * N_SUB + j, 0)),
                pl.BlockSpec(memory_space=pl.ANY),
                pl.BlockSpec((None, 1, 2 * D_MODEL),
                             lambda i, j: (mod_row0 + i // tiles_per_mod, 0, 0)),
                pl.BlockSpec((1, D_MODEL), lambda i, j: (0, 0)),
                pl.BlockSpec((D_MODEL, 4 * ATT_WIDTH), lambda i, j: (0, 0))]
    args = [x.reshape(n_tok, D_MODEL), x.reshape(n_rows, CHUNK, D_MODEL), mods, g0, w_in]
    row_spec = pl.BlockSpec((SUB_TOK, ATT_WIDTH), lambda i, j: (i * N_SUB + j, 0))
    row_shape = jax.ShapeDtypeStruct((n_tok, ATT_WIDTH), BF16)
    out_specs = [row_spec, row_spec, row_spec]
    out_shape = [row_shape, row_shape, row_shape]
    if rope:
        assert seq_len == TOK_PER_TILE
        for tab in rope_tabs:
            in_specs.append(pl.BlockSpec((SUB_TOK, HEAD_W), lambda i, j: (j, 0)))
            args.append(tab)
    else:
        out_specs += [pl.BlockSpec((seqs_per_sub, ATT_WIDTH, seq_len),
                                   lambda i, j: (i * N_SUB + j, 0, 0)),
                      pl.BlockSpec((seqs_per_sub, seq_len, N_HEADS, HEAD_W),
                                   lambda i, j: (i * N_SUB + j, 0, 0, 0))]
        out_shape += [jax.ShapeDtypeStruct((n_batch, ATT_WIDTH, seq_len), F32),
                      jax.ShapeDtypeStruct((n_batch, seq_len, N_HEADS, HEAD_W), F32)]
    out_specs.append(pl.BlockSpec((N_GROUPS, CW, ROWS_PER_TILE), lambda i, j: (0, 0, i)))
    out_shape.append(jax.ShapeDtypeStruct((N_GROUPS, CW, n_rows), BF16))
    return pl.pallas_call(
        functools.partial(_in_proj_kernel, rope=rope, seqs_per_sub=seqs_per_sub),
        grid=(n_tiles, N_SUB),
        in_specs=in_specs,
        out_specs=out_specs,
        out_shape=out_shape,
        scratch_shapes=[pltpu.VMEM((SSM_WIDTH, D_MODEL), BF16),
                        pltpu.VMEM((CHUNK, ROWS_PER_TILE, D_MODEL), F32),
                        pltpu.SemaphoreType.DMA((CHUNK,))],
        compiler_params=_cparams(2),
        name="in_proj",
    )(*args)


def _attn_kernel(*refs, has_ctx, lam_init, n_seq, seq_len, tq, n_cast):
    if n_cast:
        cast_in = refs[len(refs) - 2 * n_cast - 1:len(refs) - n_cast - 1]
        cast_out = refs[len(refs) - n_cast:]
        refs = refs[:len(refs) - 2 * n_cast - 1] + (refs[len(refs) - n_cast - 1],)
        for src, dst in zip(cast_in, cast_out):
            dst[...] = src[...].astype(dst.dtype)
    if has_ctx:
        lam_ref, sg_ref, q_ref, ck_ref, cv_ref, k_ref, v_ref, o_ref = refs
    else:
        lam_ref, sg_ref, q_ref, k_ref, v_ref, o_ref = refs
    lp = lam_ref[...]
    lam = (jnp.exp(jnp.sum(lp[0:1] * lp[1:2], axis=-1, keepdims=True))
           - jnp.exp(jnp.sum(lp[2:3] * lp[3:4], axis=-1, keepdims=True)) + lam_init)
    first_map = lax.broadcasted_iota(jnp.int32, (1, HEAD_W), 1) < HEAD_DIM
    ti = min(TQ_ITEM, tq)
    for row0 in range(0, n_seq * tq, ti):
        b = row0 // tq
        q_rows = slice(row0, row0 + ti)
        for hd in range(N_HEADS):
            sl = slice(hd * HEAD_W, (hd + 1) * HEAD_W)
            qh = q_ref[q_rows, sl]
            zero = jnp.zeros_like(qh)
            qs = jnp.concatenate([jnp.where(first_map, qh, zero),
                                  jnp.where(first_map, zero, qh)], axis=0)
            kv_rows = slice(b * seq_len, (b + 1) * seq_len)
            parts = [(k_ref[kv_rows, sl], v_ref[kv_rows, sl])]
            scores = [lax.dot_general(qs, parts[0][0], NT_DIMS, preferred_element_type=F32)]
            if has_ctx:
                parts.insert(0, (None, cv_ref[:, hd, :].astype(BF16)))
                scores.insert(0, jnp.dot(qs, ck_ref[sl, :].astype(BF16),
                                         preferred_element_type=F32))
            mx = scores[0].max(axis=-1, keepdims=True)
            for s in scores[1:]:
                mx = jnp.maximum(mx, s.max(axis=-1, keepdims=True))
            acc = None
            for s, (_, vv) in zip(scores, parts):
                e = jnp.exp2(s - mx).astype(BF16)
                v_one = jnp.concatenate([vv, jnp.ones_like(vv)], axis=1)
                pv = jnp.dot(e, v_one, preferred_element_type=F32)
                acc = pv if acc is None else acc + pv
            num = acc[:, 0:HEAD_W] / acc[:, HEAD_W:2 * HEAD_W]
            o = num[0:ti] - lam * num[ti:2 * ti]
            o = _rms(o, sg_ref[...]) * (1.0 - lam_init)
            o_ref[q_rows, sl] = o.astype(o_ref.dtype)


def _attention(q, k, v, ctx_k, ctx_v, lam_params, subln_g, n_batch, seq_len, lam_init,
               cast_weights=()):
    has_ctx = ctx_k is not None
    tq = min(1024, seq_len)
    n_q = seq_len // tq
    n_seq = 1 if n_q > 1 else min(4, n_batch)
    in_specs = [pl.BlockSpec((4, HEAD_DIM), lambda b, i: (0, 0)),
                pl.BlockSpec((1, HEAD_W), lambda b, i: (0, 0)),
                pl.BlockSpec((n_seq * tq, ATT_WIDTH), lambda b, i: (b * n_q + i, 0))]
    args = [lam_params, subln_g.reshape(1, HEAD_W), q]
    if has_ctx:
        past = ctx_v.shape[1]
        in_specs += [pl.BlockSpec((None, ATT_WIDTH, past), lambda b, i: (b, 0, 0)),
                     pl.BlockSpec((None, past, N_HEADS, HEAD_W), lambda b, i: (b, 0, 0, 0))]
        args += [ctx_k, ctx_v]
    kv_spec = pl.BlockSpec((n_seq * seq_len, ATT_WIDTH), lambda b, i: (b, 0))
    in_specs += [kv_spec, kv_spec]
    args += [k, v]
    out_specs = [pl.BlockSpec((n_seq * tq, ATT_WIDTH), lambda b, i: (b * n_q + i, 0))]
    out_shape = [jax.ShapeDtypeStruct((n_batch * seq_len, ATT_WIDTH), BF16)]
    n_steps = (n_batch // n_seq) * n_q
    for w in cast_weights:
        rows = w.shape[0] // n_steps
        spec = pl.BlockSpec((rows, w.shape[1]), lambda b, i: (b * n_q + i, 0))
        in_specs.append(spec)
        args.append(w)
        out_specs.append(spec)
        out_shape.append(jax.ShapeDtypeStruct(w.shape, BF16))
    outs = pl.pallas_call(
        functools.partial(_attn_kernel, has_ctx=has_ctx, lam_init=lam_init,
                          n_seq=n_seq, seq_len=seq_len, tq=tq, n_cast=len(cast_weights)),
        grid=(n_batch // n_seq, n_q),
        in_specs=in_specs,
        out_specs=out_specs,
        out_shape=out_shape,
        compiler_params=_cparams(2),
        name="diff_attention",
    )(*args)
    return outs[0], tuple(outs[1:])


def _cmul(ar, ai, br, bi):
    return ar * br - ai * bi, ar * bi + ai * br


def _ssm_prep_kernel(lre_ref, lim_ref, ls_ref, bre_ref, bim_ref, cre_ref, cim_ref, d_ref,
                     mt_ref, gt_ref, wo_ref, at_ref):
    lane = lax.broadcasted_iota(jnp.int32, (SSM_GROUP, CW), 1)
    chan = lax.broadcasted_iota(jnp.int32, (SSM_GROUP, CW), 0)
    for gi in range(GROUP_BLOCK):
        gt_cols, wo_cols, at_cols, toeplitz = [], [], [], []
        for dr in range(2):
            lr = jnp.minimum(lre_ref[dr, gi], -1e-4)
            li = lim_ref[dr, gi]
            step = jnp.exp(ls_ref[dr, gi])
            mag = jnp.exp(lr * step)
            a_re = mag * jnp.cos(li * step)
            a_im = mag * jnp.sin(li * step)
            den = lr * lr + li * li
            nr = a_re - 1.0
            f_re = (nr * lr + a_im * li) / den
            f_im = (a_im * lr - nr * li) / den
            bt_re, bt_im = bre_ref[dr, gi], bim_ref[dr, gi]
            bb_re, bb_im = _cmul(f_re, f_im, bt_re, bt_im)
            c_re, c_im = cre_ref[dr, gi], cim_ref[dr, gi]
            pw = [(jnp.ones_like(a_re), jnp.zeros_like(a_im))]
            for _ in range(CHUNK):
                pw.append(_cmul(pw[-1][0], pw[-1][1], a_re, a_im))
            g_re, g_im, e_re, e_im = [], [], [], []
            for t in range(CHUNK):
                pr, pi = pw[CHUNK - 1 - t] if dr == 0 else pw[t]
                r, i = _cmul(bb_re, bb_im, pr, pi)
                g_re.append(r)
                g_im.append(i)
                pr, pi = pw[t + 1] if dr == 0 else pw[CHUNK - t]
                r, i = _cmul(c_re, c_im, pr, pi)
                e_re.append(r)
                e_im.append(-i)
            g_cat = jnp.concatenate([jnp.concatenate(g_re, axis=0),
                                     jnp.concatenate(g_im, axis=0)], axis=1)
            gt_cols.append(g_cat)
            wo_cols.append(jnp.concatenate([jnp.concatenate(e_re, axis=0),
                                            jnp.concatenate(e_im, axis=0)], axis=1))
            c_cat = jnp.concatenate([c_re, -c_im], axis=1)
            toeplitz.append(lax.dot_general(c_cat, g_cat, NT_DIMS,
                                            precision=lax.Precision.HIGHEST,
                                            preferred_element_type=F32))
            apw = [pw[CHUNK]]
            for _ in range(SCAN_BLOCK - 1):
                apw.append(_cmul(apw[-1][0], apw[-1][1], apw[0][0], apw[0][1]))
            order = list(range(SCAN_BLOCK)) if dr == 0 else list(range(SCAN_BLOCK - 1, -1, -1))
            order += [2 ** l - 1 for l in range(SCAN_LEVELS)]
            order += [0] * (AT_ROWS - len(order))
            at_cols += [jnp.concatenate([jnp.concatenate([apw[i][0], apw[i][0]], axis=1)
                                         for i in order], axis=0),
                        jnp.concatenate([jnp.concatenate([-apw[i][1], apw[i][1]], axis=1)
                                         for i in order], axis=0)]
        kf_rev, kb = toeplitz
        d_skip = d_ref[gi]
        blocks = []
        for t in range(CHUNK):
            fwd = pltpu.roll(kf_rev, (CW - (CHUNK - 1 - t) * SSM_GROUP) % CW, axis=1)
            bwd = pltpu.roll(kb, t * SSM_GROUP, axis=1)
            blocks.append(jnp.where(lane < (t + 1) * SSM_GROUP, fwd, 0.0)
                          + jnp.where(lane >= t * SSM_GROUP, bwd, 0.0)
                          + jnp.where(lane == chan + t * SSM_GROUP, d_skip, 0.0))
        mt_ref[gi] = jnp.concatenate(blocks, axis=0).astype(mt_ref.dtype)
        gt_ref[gi] = jnp.concatenate(gt_cols, axis=1).astype(gt_ref.dtype)
        wo_ref[gi] = jnp.concatenate(wo_cols, axis=1).astype(wo_ref.dtype)
        at_ref[gi] = jnp.concatenate(at_cols, axis=1)


N_PREP_IN, N_PREP_OUT, N_MOD_IN = 8, 4, 4


def _prep_mod_kernel(*refs):
    prep_in = refs[:N_PREP_IN]
    mod_in = refs[N_PREP_IN:N_PREP_IN + N_MOD_IN]
    w_in_ref = refs[N_PREP_IN + N_MOD_IN]
    outs = refs[N_PREP_IN + N_MOD_IN + 1:]
    _ssm_prep_kernel(*prep_in, *outs[:N_PREP_OUT])
    _mod_kernel(*mod_in, outs[N_PREP_OUT])
    outs[N_PREP_OUT + 1][...] = w_in_ref[...].astype(BF16)


def _ssm_prep_and_modulation(lam_re, lam_im, log_step, b_re, b_im, c_re, c_im, d_skip,
                             c_ctx, c, w_mod, b_mod, w_in):
    row = lambda a: a.reshape(2, N_GROUPS, 1, SSM_STATE)
    bt = lambda a: jnp.swapaxes(a, 2, 3)
    d_row = jnp.tile((d_skip[0] + d_skip[1]).reshape(N_GROUPS, 1, SSM_GROUP), (1, 1, CHUNK))
    gb = GROUP_BLOCK
    n_steps = N_GROUPS // gb
    vec_spec = pl.BlockSpec((2, gb, 1, SSM_STATE), lambda i: (0, i, 0, 0))
    mat_spec = pl.BlockSpec((2, gb, SSM_GROUP, SSM_STATE), lambda i: (0, i, 0, 0))
    w_spec = pl.BlockSpec((gb, CW, CW), lambda i: (i, 0, 0))
    w_shape = jax.ShapeDtypeStruct((N_GROUPS, CW, CW), BF16)
    n_mod = w_mod.shape[1]
    tk = D_MODEL // n_steps
    assert 1 + c.shape[0] <= MOD_ROWS
    outs = pl.pallas_call(
        _prep_mod_kernel,
        grid=(n_steps,),
        in_specs=[vec_spec, vec_spec,
                  pl.BlockSpec((2, gb, 1, 1), lambda i: (0, i, 0, 0)),
                  mat_spec, mat_spec, mat_spec, mat_spec,
                  pl.BlockSpec((gb, 1, CW), lambda i: (i, 0, 0)),
                  pl.BlockSpec((1, tk), lambda k: (0, k)),
                  pl.BlockSpec((c.shape[0], tk), lambda k: (0, k)),
                  pl.BlockSpec((tk, n_mod), lambda k: (k, 0)),
                  pl.BlockSpec((1, n_mod), lambda k: (0, 0)),
                  pl.BlockSpec((tk, w_in.shape[1]), lambda k: (k, 0))],
        out_specs=[w_spec, w_spec, w_spec,
                   pl.BlockSpec((gb, AT_ROWS, 4 * 2 * SSM_STATE), lambda i: (i, 0, 0)),
                   pl.BlockSpec((MOD_ROWS, 1, n_mod), lambda k: (0, 0, 0)),
                   pl.BlockSpec((tk, w_in.shape[1]), lambda k: (k, 0))],
        out_shape=[w_shape, w_shape, w_shape,
                   jax.ShapeDtypeStruct((N_GROUPS, AT_ROWS, 4 * 2 * SSM_STATE), F32),
                   jax.ShapeDtypeStruct((MOD_ROWS, 1, n_mod), F32),
                   jax.ShapeDtypeStruct(w_in.shape, BF16)],
        compiler_params=_cparams(1),
        name="ssm_prep_modulation",
    )(row(lam_re), row(lam_im), log_step.reshape(2, N_GROUPS, 1, 1),
      bt(b_re), bt(b_im), c_re, c_im, d_row,
      c_ctx.reshape(1, D_MODEL), c, w_mod, b_mod.reshape(1, n_mod), w_in)
    return outs[:N_PREP_OUT], outs[N_PREP_OUT], outs[N_PREP_OUT + 1]


def _shift_rows(x, m, down):
    n = x.shape[0]
    return pltpu.roll(x, m if down else n - m, axis=0)


def _ssm_kernel(*refs, n_seq, has_h0):
    if has_h0:
        (xt_ref, mt_ref, gt_ref, wo_ref, at_ref, wg_ref, bg_ref, h0_ref,
         z_hbm, zs_ref, zb_ref, zb_sem) = refs
    else:
        (xt_ref, mt_ref, gt_ref, wo_ref, at_ref, wg_ref, bg_ref,
         z_hbm, st_ref, zs_ref, zb_ref, zb_sem, fin_ref) = refs
    step = pl.program_id(0)
    n_rows = xt_ref.shape[-1]
    seg = n_rows // n_seq
    cw2 = 2 * SSM_STATE
    assert seg % SCAN_BLOCK == 0
    n_blk = n_rows // SCAN_BLOCK
    blk_per_seq = seg // SCAN_BLOCK
    pos = lax.broadcasted_iota(jnp.int32, (n_rows, cw2), 0) % seg
    row_blk = lax.broadcasted_iota(jnp.int32, (SCAN_BLOCK, cw2), 0)

    def low_half(shape):
        return lax.broadcasted_iota(jnp.int32, shape, 1) < SSM_STATE

    def swap(v):
        return pltpu.roll(v, SSM_STATE, axis=1)

    def to_planes(va, vb):
        lo = low_half((va.shape[0], cw2))
        va_l = pltpu.roll(va, 3 * SSM_STATE, axis=1)
        vb_r = pltpu.roll(vb, SSM_STATE, axis=1)
        return (jnp.where(lo, va[:, 0:cw2], vb_r[:, 0:cw2]),
                jnp.where(lo, va_l[:, 0:cw2], vb[:, 0:cw2]),
                jnp.where(lo, va[:, cw2:2 * cw2], vb_r[:, cw2:2 * cw2]),
                jnp.where(lo, va_l[:, cw2:2 * cw2], vb[:, cw2:2 * cw2]))

    def from_planes(f_re, f_im, b_re, b_im):
        lo = low_half(f_re.shape)
        va = jnp.concatenate([jnp.where(lo, f_re, swap(f_im)), jnp.where(lo, b_re, swap(b_im))], axis=1)
        vb = jnp.concatenate([jnp.where(lo, swap(f_re), f_im), jnp.where(lo, swap(b_re), b_im)], axis=1)
        return va, vb

    for ga in range(0, GROUP_BLOCK, 2):
        gb = ga + 1
        xts = [xt_ref[ga], xt_ref[gb]]
        s_pair = [lax.dot_general(xts[i], gt_ref[g], TN_DIMS, preferred_element_type=F32)
                  for i, g in enumerate((ga, gb))]
        planes = to_planes(*s_pair)
        lo_t = low_half((AT_ROWS, cw2))
        if has_h0:
            h0_planes = to_planes(h0_ref[ga], h0_ref[gb])
        ent_planes = []
        for dr in range(2):
            down = dr == 0
            pa, pb = (at_ref[g][:, (2 * dr) * cw2:(2 * dr + 1) * cw2] for g in (ga, gb))
            qa, qb = (at_ref[g][:, (2 * dr + 1) * cw2:(2 * dr + 2) * cw2] for g in (ga, gb))
            ar_tab = jnp.where(lo_t, pa, pb)
            ai_tab = jnp.where(lo_t, -qa, qb)
            re3 = planes[2 * dr].reshape(n_blk, SCAN_BLOCK, cw2)
            im3 = planes[2 * dr + 1].reshape(n_blk, SCAN_BLOCK, cw2)
            for lvl in range(SCAN_LEVELS):
                m = 2 ** lvl
                valid = (row_blk >= m) if down else (row_blk < SCAN_BLOCK - m)
                row = SCAN_BLOCK + lvl
                ar = jnp.where(valid, ar_tab[row:row + 1], 0.0)
                ai = jnp.where(valid, ai_tab[row:row + 1], 0.0)
                shift = m if down else SCAN_BLOCK - m
                sh_re = pltpu.roll(re3, shift, axis=1)
                sh_im = pltpu.roll(im3, shift, axis=1)
                re3, im3 = re3 + ar * sh_re - ai * sh_im, im3 + ar * sh_im + ai * sh_re
            ar_blk, ai_blk = ar_tab[0:SCAN_BLOCK], ai_tab[0:SCAN_BLOCK]
            blk_re = [re3[i] for i in range(n_blk)]
            blk_im = [im3[i] for i in range(n_blk)]
            edge = slice(SCAN_BLOCK - 1, SCAN_BLOCK) if down else slice(0, 1)
            for q_i in range(n_seq):
                idxs = list(range(q_i * blk_per_seq, (q_i + 1) * blk_per_seq))
                idxs = idxs if down else idxs[::-1]
                for prev, cur in zip([None] + idxs[:-1], idxs):
                    if prev is not None:
                        c_re, c_im = blk_re[prev][edge], blk_im[prev][edge]
                    elif has_h0:
                        c_re = h0_planes[2 * dr][q_i:q_i + 1]
                        c_im = h0_planes[2 * dr + 1][q_i:q_i + 1]
                    else:
                        continue
                    c_re = jnp.broadcast_to(c_re, (SCAN_BLOCK, cw2))
                    c_im = jnp.broadcast_to(c_im, (SCAN_BLOCK, cw2))
                    blk_re[cur] = blk_re[cur] + ar_blk * c_re - ai_blk * c_im
                    blk_im[cur] = blk_im[cur] + ar_blk * c_im + ai_blk * c_re
            for part, blks in ((0, blk_re), (1, blk_im)):
                s = jnp.concatenate(blks, axis=0)
                if not has_h0:
                    fin_ref[part] = s
                ent = _shift_rows(s, 1, down)
                ent = jnp.where((pos >= 1) if down else (pos < seg - 1), ent, 0.0)
                if has_h0:
                    ent_blk = [ent[i * SCAN_BLOCK:(i + 1) * SCAN_BLOCK] for i in range(n_blk)]
                    at_edge = row_blk == (0 if down else SCAN_BLOCK - 1)
                    for q_i in range(n_seq):
                        bi = q_i * blk_per_seq if down else (q_i + 1) * blk_per_seq - 1
                        ent_blk[bi] = jnp.where(
                            at_edge, h0_planes[2 * dr + part][q_i:q_i + 1], ent_blk[bi])
                    ent = jnp.concatenate(ent_blk, axis=0)
                ent_planes.append(ent)
            if not has_h0:
                rows = pl.ds(seg - 1 if down else 0, n_seq, stride=seg)
                f_re, f_im = fin_ref[0, rows, :], fin_ref[1, rows, :]
                lo_s = low_half((n_seq, cw2))
                st_ref[ga, :, dr * cw2:(dr + 1) * cw2] = jnp.where(lo_s, f_re, swap(f_im))
                st_ref[gb, :, dr * cw2:(dr + 1) * cw2] = jnp.where(lo_s, swap(f_re), f_im)
        h_pair = from_planes(*ent_planes)
        for i, g in enumerate((ga, gb)):
            yt = (jnp.dot(mt_ref[g], xts[i], preferred_element_type=F32)
                  + lax.dot_general(wo_ref[g], h_pair[i].astype(BF16), NT_DIMS,
                                    preferred_element_type=F32))
            z = jax.nn.gelu(yt, approximate=True)
            grp = step * GROUP_BLOCK + g
            for t in range(CHUNK):
                zs_ref[t, pl.ds(pl.multiple_of(grp * SSM_GROUP, SSM_GROUP), SSM_GROUP), :] = (
                    z[t * SSM_GROUP:(t + 1) * SSM_GROUP, :])

    @pl.when(step == pl.num_programs(0) - 1)
    def _():
        def put(t):
            return pltpu.make_async_copy(zb_ref.at[t], z_hbm.at[:, t, :], zb_sem.at[t])

        for t0 in range(0, CHUNK, Z_BATCH):
            for t in range(t0, t0 + Z_BATCH):
                zt = zs_ref[t]
                gate = jnp.dot(wg_ref[...], zt.astype(BF16),
                               preferred_element_type=F32) + bg_ref[...]
                zb_ref[t] = (zt * jax.nn.sigmoid(gate)).T
            for t in range(t0, t0 + Z_BATCH):
                put(t).start()
        for t in range(CHUNK):
            put(t).wait()


def _ssm(xt, mt, gt, wo, at, w_glu_t, b_glu_col, h0, n_seq):
    n_rows = xt.shape[-1]
    has_h0 = h0 is not None
    gb = GROUP_BLOCK
    w_spec = pl.BlockSpec((gb, CW, CW), lambda i: (i, 0, 0))
    in_specs = [pl.BlockSpec((gb, CW, n_rows), lambda i: (i, 0, 0)),
                w_spec, w_spec, w_spec,
                pl.BlockSpec((gb, AT_ROWS, 4 * 2 * SSM_STATE), lambda i: (i, 0, 0)),
                pl.BlockSpec((SSM_WIDTH, SSM_WIDTH), lambda i: (0, 0)),
                pl.BlockSpec((SSM_WIDTH, 1), lambda i: (0, 0))]
    args = [xt, mt, gt, wo, at, w_glu_t, b_glu_col]
    out_specs = [pl.BlockSpec(memory_space=pl.ANY)]
    out_shape = [jax.ShapeDtypeStruct((n_rows, CHUNK, SSM_WIDTH), F32)]
    scratch = [pltpu.VMEM((CHUNK, SSM_WIDTH, n_rows), F32),
               pltpu.VMEM((CHUNK, n_rows, SSM_WIDTH), F32),
               pltpu.SemaphoreType.DMA((CHUNK,))]
    if has_h0:
        in_specs.append(pl.BlockSpec((gb, n_seq, CW), lambda i: (i, 0, 0)))
        args.append(h0)
    else:
        out_specs.append(pl.BlockSpec((gb, n_seq, CW), lambda i: (i, 0, 0)))
        out_shape.append(jax.ShapeDtypeStruct((N_GROUPS, n_seq, CW), F32))
        scratch.append(pltpu.VMEM((2, n_rows, 2 * SSM_STATE), F32))
    return pl.pallas_call(
        functools.partial(_ssm_kernel, n_seq=n_seq, has_h0=has_h0),
        grid=(N_GROUPS // gb,),
        in_specs=in_specs,
        out_specs=out_specs,
        out_shape=out_shape,
        scratch_shapes=scratch,
        compiler_params=_cparams(1),
        name="ssm_scan_glu",
    )(*args)


FF_CHUNK = 256
FFN_TOK = 1024


def _out_ffn_kernel(x_ref, attn_ref, z_ref, mod_ref, g_ref, wo_ref, wfi_ref, wfo_ref,
                    o_ref, act_ref):
    gate1 = mod_ref[:, 2 * D_MODEL:3 * D_MODEL]
    shift2 = mod_ref[:, 3 * D_MODEL:4 * D_MODEL]
    scale2 = mod_ref[:, 4 * D_MODEL:5 * D_MODEL]
    gate2 = mod_ref[:, 5 * D_MODEL:6 * D_MODEL]
    half = FFN_TOK // 2
    gain2 = g_ref[2:3, :] * (1.0 + scale2)

    def pre_ffn(hf):
        r = slice(hf * half, (hf + 1) * half)
        mixer = jnp.concatenate([attn_ref[r, :], z_ref[r, :].astype(BF16)], axis=1)
        mix = jnp.dot(mixer, wo_ref[...], preferred_element_type=F32)
        x1 = x_ref[r, :] + gate1 * _rms(mix, g_ref[1:2, :])
        ms = jnp.mean(x1 * x1, axis=-1, keepdims=True)
        return x1, (x1 * lax.rsqrt(ms + NORM_EPS) * gain2 + shift2).astype(BF16)

    def ffn_in(hf, h, chunks):
        r = slice(hf * half, (hf + 1) * half)
        for c in chunks:
            lo = c * FF_CHUNK
            gt = jnp.dot(h, wfi_ref[:, lo:lo + FF_CHUNK], preferred_element_type=F32)
            up = jnp.dot(h, wfi_ref[:, D_FF + lo:D_FF + lo + FF_CHUNK],
                         preferred_element_type=F32)
            act_ref[r, lo:lo + FF_CHUNK] = (_silu(gt) * up).astype(BF16)

    def ffn_out(hf, x1):
        r = slice(hf * half, (hf + 1) * half)
        f = jnp.dot(act_ref[r, :], wfo_ref[...], preferred_element_type=F32)
        o_ref[r, :] = x1 + gate2 * _rms(f, g_ref[3:4, :])

    n_chunks = D_FF // FF_CHUNK
    x1_a, h_a = pre_ffn(0)
    x1_b, h_b = pre_ffn(1)
    ffn_in(0, h_a, range(n_chunks))
    ffn_out(0, x1_a)
    ffn_in(1, h_b, range(n_chunks))
    ffn_out(1, x1_b)


def _out_ffn(x2d, attn, z, mods, mod_rows, norm_g, w_o, w_ffn_in, w_ffn_out):
    n_tok = x2d.shape[0]
    z2d = z.reshape(n_tok, SSM_WIDTH)
    n_steps = n_tok // FFN_TOK
    mod_row0, n_mod = mod_rows
    steps_per_mod = n_steps // n_mod
    const = lambda i: (0, 0)
    row_spec = lambda w: pl.BlockSpec((FFN_TOK, w), lambda i: (i, 0))
    return pl.pallas_call(
        _out_ffn_kernel,
        grid=(n_steps,),
        in_specs=[row_spec(D_MODEL), row_spec(ATT_WIDTH), row_spec(SSM_WIDTH),
                  pl.BlockSpec((None, 1, N_MOD * D_MODEL),
                               lambda i: (mod_row0 + i // steps_per_mod, 0, 0)),
                  pl.BlockSpec((4, D_MODEL), const),
                  pl.BlockSpec((2 * ATT_WIDTH, D_MODEL), const, pipeline_mode=pl.Buffered(1)),
                  pl.BlockSpec((D_MODEL, 2 * D_FF), const, pipeline_mode=pl.Buffered(1)),
                  pl.BlockSpec((D_FF, D_MODEL), const, pipeline_mode=pl.Buffered(1))],
        out_specs=row_spec(D_MODEL),
        out_shape=jax.ShapeDtypeStruct((n_tok, D_MODEL), F32),
        scratch_shapes=[pltpu.VMEM((FFN_TOK, D_FF), BF16)],
        compiler_params=_cparams(1),
        name="out_proj_ffn",
    )(x2d, attn, z2d, mods, norm_g, w_o, w_ffn_in, w_ffn_out)


def _rope_tables(seq_len):
    t = np.arange(seq_len)
    row = (t // GRID_W).astype(np.float32)
    col = (t % GRID_W).astype(np.float32)
    half = HEAD_DIM // 2
    inv_freq = (np.float32(ROPE_BASE)
                ** (-np.arange(0, half, 2, dtype=np.float32) / np.float32(half))).astype(np.float32)
    ang_r = row[:, None] * inv_freq
    ang_c = col[:, None] * inv_freq
    ang = np.concatenate([ang_r, ang_r, ang_c, ang_c], axis=-1)
    cos, sin = np.cos(ang), np.sin(ang)
    upper = (np.arange(HEAD_DIM) % 32) < 16
    sa = np.where(upper, -sin, 0.0)
    sb = np.where(upper, 0.0, sin)
    two = lambda a: jnp.asarray(np.concatenate([a, a], axis=-1), dtype=F32)
    return two(cos), two(sa), two(sb)


def _layer(x, mods, mod_rows, lam_init, rope_tabs, ctx_k, ctx_v, h0, weights, prep):
    n_batch, seq_len = x.shape[:2]
    g = weights['norm_g']
    outs = _in_proj(x, mods, mod_rows, g[0:1], weights['w_in'], rope_tabs)
    q, k, v = outs[:3]
    pending = () if 'late_bf16' in weights else weights['late_f32']
    attn, cast = _attention(q, k, v, ctx_k, ctx_v, weights['lam'], weights['subln_g'],
                            n_batch, seq_len, lam_init, cast_weights=pending)
    if pending:
        weights['late_bf16'] = cast
    ssm_out = _ssm(outs[-1], *prep, weights['w_glu_t'], weights['b_glu_col'], h0, n_batch)
    y = _out_ffn(x.reshape(n_batch * seq_len, D_MODEL), attn, ssm_out[0], mods, mod_rows, g,
                 *weights['late_bf16'])
    return y.reshape(x.shape), outs[3:-1], ssm_out[1:]


def kernel(x_prompt, x_sample, cache_k, cache_v, state_ssm_re, state_ssm_im, c, c_ctx, w_mod, b_mod, norm_g, w_in, lam_params, subln_g, ssm_lambda_re, ssm_lambda_im, ssm_log_step, ssm_b_re, ssm_b_im, ssm_c_re, ssm_c_im, ssm_d, w_glu, b_glu, w_o, w_ffn_in, w_ffn_out):
    depth = w_mod.shape[0]
    assert depth == 1
    bp = x_prompt.shape[0]
    bd, ld_len = x_sample.shape[:2]
    past = cache_k.shape[2]
    xp, xs = x_prompt, x_sample
    rope_tabs = _rope_tables(ld_len)
    ks_out, vs_out, hr_out, hi_out = [], [], [], []
    for l in range(depth):
        lam_init = 0.8 - 0.6 * math.exp(-0.3 * l)
        prep, mods, w_in_bf16 = _ssm_prep_and_modulation(
            ssm_lambda_re[l], ssm_lambda_im[l], ssm_log_step[l], ssm_b_re[l], ssm_b_im[l],
            ssm_c_re[l], ssm_c_im[l], ssm_d[l], c_ctx, c, w_mod[l], b_mod[l], w_in[l])
        weights = {
            'norm_g': norm_g[l],
            'w_in': w_in_bf16,
            'lam': lam_params[l], 'subln_g': subln_g[l],
            'w_glu_t': w_glu[l].T.astype(BF16), 'b_glu_col': b_glu[l].reshape(SSM_WIDTH, 1),
            'late_f32': (w_o[l], w_ffn_in[l], w_ffn_out[l]),
        }
        ck = jnp.transpose(cache_k[:, l], (0, 2, 3, 1)).reshape(bd, ATT_WIDTH, past)
        cv = cache_v[:, l]
        h0 = jnp.stack([state_ssm_re[:, l], state_ssm_im[:, l]], axis=2)
        h0 = h0.transpose(3, 0, 1, 2, 4).reshape(N_GROUPS, bd, CW)
        xs, _, _ = _layer(xs, mods, (1, bd), lam_init, rope_tabs, ck, cv, h0, weights, prep)
        xp, (k_ctx, v_ctx), (st,) = _layer(xp, mods, (0, 1), lam_init, None, None, None, None,
                                           weights, prep)
        ks_out.append(jnp.swapaxes(k_ctx, 1, 2).reshape(bp, -1, 2 * N_HEADS, HEAD_DIM))
        vs_out.append(v_ctx)
        fin = st.reshape(N_GROUPS, bp, 2, 2, SSM_STATE).transpose(1, 2, 3, 0, 4)
        hr_out.append(fin[:, :, 0])
        hi_out.append(fin[:, :, 1])
    return (xp, xs, jnp.stack(ks_out, axis=1), jnp.stack(vs_out, axis=1),
            jnp.stack(hr_out, axis=1), jnp.stack(hi_out, axis=1))
```

```python
import functools
import math

import jax
import jax.numpy as jnp
import numpy as np
from jax import lax
from jax.experimental import pallas as pl
from jax.experimental.pallas import tpu as pltpu

F32 = jnp.float32
BF16 = jnp.bfloat16

D_MODEL = 1024
GRID_W = 64
ATT_WIDTH = 512
SSM_WIDTH = 512
HEAD_DIM = 64
N_HEADS = 4
HEAD_W = 2 * HEAD_DIM
SSM_GROUP = 16
N_GROUPS = 32
SSM_STATE = 64
D_FF = 2816
N_MOD = 6
ROPE_BASE = 10000.0
NORM_EPS = 1e-6

CHUNK = 16
CW = CHUNK * SSM_GROUP
SCAN_BLOCK = 8
SCAN_LEVELS = 3
AT_ROWS = 16
GROUP_BLOCK = 4
Z_BATCH = 8

ROWS_PER_TILE = 128
TOK_PER_TILE = ROWS_PER_TILE * CHUNK
SUB_TOK = 1024
TQ_ITEM = 256
N_SUB = TOK_PER_TILE // SUB_TOK

VMEM_LIMIT = 56 * 1024 * 1024

NT_DIMS = (((1,), (1,)), ((), ()))
TN_DIMS = (((0,), (0,)), ((), ()))


def _cparams(n_axes):
    return pltpu.CompilerParams(
        dimension_semantics=("arbitrary",) * n_axes,
        vmem_limit_bytes=VMEM_LIMIT)


def _rms(x, g):
    ms = jnp.mean(x * x, axis=-1, keepdims=True)
    return x * lax.rsqrt(ms + NORM_EPS) * g


def _silu(x):
    return x * jax.nn.sigmoid(x)


MOD_ROWS = 8


def _mod_kernel(ctx_ref, c_ref, w_ref, b_ref, o_ref):
    n_lat, tk = c_ref.shape
    row = lax.broadcasted_iota(jnp.int32, (MOD_ROWS, tk), 0)
    cond = jnp.where(row == 0, ctx_ref[...], 0.0)
    for b in range(n_lat):
        cond = jnp.where(row == 1 + b, c_ref[b:b + 1, :], cond)
    part = jnp.dot(_silu(cond).astype(BF16), w_ref[...].astype(BF16),
                   preferred_element_type=F32)

    @pl.when(pl.program_id(0) == 0)
    def _():
        o_ref[:, 0, :] = part + b_ref[...]

    @pl.when(pl.program_id(0) > 0)
    def _():
        o_ref[:, 0, :] += part


def _rope(x, cos, sa, sb):
    return (x * cos + pltpu.roll(x, HEAD_W - 16, axis=1) * sa
            + pltpu.roll(x, 16, axis=1) * sb)


def _in_proj_kernel(*refs, rope, seqs_per_sub):
    x_ref, x3_hbm, mod_ref, g_ref, w_ref = refs[:5]
    refs = refs[5:]
    if rope:
        cos_ref, sa_ref, sb_ref = refs[:3]
        refs = refs[3:]
        q_ref, k_ref, v_ref, ut_ref, wut_ref, xt_ref, xt_sem = refs
    else:
        q_ref, k_ref, v_ref, kc_ref, vc_ref, ut_ref, wut_ref, xt_ref, xt_sem = refs
    tile = pl.program_id(0)
    j = pl.program_id(1)
    t_per_sub = CHUNK // N_SUB
    t_early = CHUNK - t_per_sub

    def gather(tile_idx, t):
        src = x3_hbm.at[pl.ds(tile_idx * ROWS_PER_TILE, ROWS_PER_TILE), t, :]
        return pltpu.make_async_copy(src, xt_ref.at[t], xt_sem.at[t])

    @pl.when(j == 0)
    def _():
        @pl.when(tile == 0)
        def _():
            for t in range(t_early):
                gather(0, t).start()
        for t in range(t_early, CHUNK):
            gather(tile, t).start()

    t_base = j * t_per_sub
    for d in range(t_per_sub):
        gather(tile, t_base + d).wait()

    @pl.when((j == N_SUB - 1) & (tile + 1 < pl.num_programs(0)))
    def _():
        for t in range(t_early):
            gather(tile + 1, t).start()

    @pl.when((tile == 0) & (j == 0))
    def _():
        wut_ref[...] = w_ref[:, 3 * ATT_WIDTH:].T

    shift = mod_ref[:, 0:D_MODEL]
    gain = g_ref[...] * (1.0 + mod_ref[:, D_MODEL:2 * D_MODEL])

    def norm_mod(xv):
        ms = jnp.mean(xv * xv, axis=-1, keepdims=True)
        return (xv * lax.rsqrt(ms + NORM_EPS) * gain + shift).astype(BF16)

    def ssm_input(d0):
        xt = jnp.concatenate([xt_ref[t_base + d0], xt_ref[t_base + d0 + 1]], axis=0)
        ut = lax.dot_general(wut_ref[...], norm_mod(xt), NT_DIMS,
                             preferred_element_type=F32)
        for d in range(2):
            blk = ut[:, d * ROWS_PER_TILE:(d + 1) * ROWS_PER_TILE]
            row0 = pl.multiple_of((t_base + d0 + d) * SSM_GROUP, SSM_GROUP)
            ut_ref[:, pl.ds(row0, SSM_GROUP), :] = (
                blk.reshape(N_GROUPS, SSM_GROUP, ROWS_PER_TILE).astype(ut_ref.dtype))

    proj = jnp.dot(norm_mod(x_ref[...]), w_ref[:, 0:3 * ATT_WIDTH],
                   preferred_element_type=F32)
    q = proj[:, 0:ATT_WIDTH]
    k = proj[:, ATT_WIDTH:2 * ATT_WIDTH]
    v = proj[:, 2 * ATT_WIDTH:3 * ATT_WIDTH]
    qscale = HEAD_DIM ** -0.5 * math.log2(math.e)
    if rope:
        cos, sa, sb = cos_ref[...], sa_ref[...], sb_ref[...]
        for hd in range(N_HEADS):
            sl = slice(hd * HEAD_W, (hd + 1) * HEAD_W)
            q_ref[:, sl] = (_rope(q[:, sl], cos, sa, sb) * qscale).astype(q_ref.dtype)
            k_ref[:, sl] = _rope(k[:, sl], cos, sa, sb).astype(k_ref.dtype)
    else:
        q_ref[...] = (q * qscale).astype(q_ref.dtype)
        k_ref[...] = k.astype(k_ref.dtype)
        seq = SUB_TOK // seqs_per_sub
        k_t = k.T
        for b in range(seqs_per_sub):
            kc_ref[b] = k_t[:, b * seq:(b + 1) * seq]
            for hd in range(N_HEADS):
                vc_ref[b, :, hd, :] = v[b * seq:(b + 1) * seq, hd * HEAD_W:(hd + 1) * HEAD_W]
    v_ref[...] = v.astype(v_ref.dtype)

    for d0 in range(0, t_per_sub, 2):
        ssm_input(d0)


def _in_proj(x, mods, mod_rows, g0, w_in, rope_tabs):
    n_batch, seq_len = x.shape[:2]
    n_tok = n_batch * seq_len
    n_rows = n_tok // CHUNK
    n_tiles = n_tok // TOK_PER_TILE
    mod_row0, n_mod = mod_rows
    tiles_per_mod = n_tiles // n_mod
    rope = rope_tabs is not None
    seqs_per_sub = max(1, SUB_TOK // seq_len)
    in_specs = [pl.BlockSpec((SUB_TOK, D_MODEL), lambda i, j: (i * N_SUB + j, 0)),
                pl.BlockSpec(memory_space=pl.ANY),
                pl.BlockSpec((None, 1, 2 * D_MODEL),
                             lambda i, j: (mod_row0 + i // tiles_per_mod, 0, 0)),
                pl.BlockSpec((1, D_MODEL), lambda i, j: (0, 0)),
                pl.BlockSpec((D_MODEL, 4 * ATT_WIDTH), lambda i, j: (0, 0))]
    args = [x.reshape(n_tok, D_MODEL), x.reshape(n_rows, CHUNK, D_MODEL), mods, g0, w_in]
    row_spec = pl.BlockSpec((SUB_TOK, ATT_WIDTH), lambda i, j: (i * N_SUB + j, 0))
    row_shape = jax.ShapeDtypeStruct((n_tok, ATT_WIDTH), BF16)
    out_specs = [row_spec, row_spec, row_spec]
    out_shape = [row_shape, row_shape, row_shape]
    if rope:
        assert seq_len == TOK_PER_TILE
        for tab in rope_tabs:
            in_specs.append(pl.BlockSpec((SUB_TOK, HEAD_W), lambda i, j: (j, 0)))
            args.append(tab)
    else:
        out_specs += [pl.BlockSpec((seqs_per_sub, ATT_WIDTH, seq_len),
                                   lambda i, j: (i * N_SUB + j, 0, 0)),
                      pl.BlockSpec((seqs_per_sub, seq_len, N_HEADS, HEAD_W),
                                   lambda i, j: (i * N_SUB + j, 0, 0, 0))]
        out_shape += [jax.ShapeDtypeStruct((n_batch, ATT_WIDTH, seq_len), F32),
                      jax.ShapeDtypeStruct((n_batch, seq_len, N_HEADS, HEAD_W), F32)]
    out_specs.append(pl.BlockSpec((N_GROUPS, CW, ROWS_PER_TILE), lambda i, j: (0, 0, i)))
    out_shape.append(jax.ShapeDtypeStruct((N_GROUPS, CW, n_rows), BF16))
    return pl.pallas_call(
        functools.partial(_in_proj_kernel, rope=rope, seqs_per_sub=seqs_per_sub),
        grid=(n_tiles, N_SUB),
        in_specs=in_specs,
        out_specs=out_specs,
        out_shape=out_shape,
        scratch_shapes=[pltpu.VMEM((SSM_WIDTH, D_MODEL), BF16),
                        pltpu.VMEM((CHUNK, ROWS_PER_TILE, D_MODEL), F32),
                        pltpu.SemaphoreType.DMA((CHUNK,))],
        compiler_params=_cparams(2),
        name="in_proj",
    )(*args)


def _attn_kernel(*refs, has_ctx, lam_init, n_seq, seq_len, tq, n_cast):
    if n_cast:
        cast_in = refs[len(refs) - 2 * n_cast - 1:len(refs) - n_cast - 1]
        cast_out = refs[len(refs) - n_cast:]
        refs = refs[:len(refs) - 2 * n_cast - 1] + (refs[len(refs) - n_cast - 1],)
        for src, dst in zip(cast_in, cast_out):
            dst[...] = src[...].astype(dst.dtype)
    if has_ctx:
        lam_ref, sg_ref, q_ref, ck_ref, cv_ref, k_ref, v_ref, o_ref = refs
    else:
        lam_ref, sg_ref, q_ref, k_ref, v_ref, o_ref = refs
    lp = lam_ref[...]
    lam = (jnp.exp(jnp.sum(lp[0:1] * lp[1:2], axis=-1, keepdims=True))
           - jnp.exp(jnp.sum(lp[2:3] * lp[3:4], axis=-1, keepdims=True)) + lam_init)
    first_map = lax.broadcasted_iota(jnp.int32, (1, HEAD_W), 1) < HEAD_DIM
    ti = min(TQ_ITEM, tq)
    for row0 in range(0, n_seq * tq, ti):
        b = row0 // tq
        q_rows = slice(row0, row0 + ti)
        for hd in range(N_HEADS):
            sl = slice(hd * HEAD_W, (hd + 1) * HEAD_W)
            qh = q_ref[q_rows, sl]
            zero = jnp.zeros_like(qh)
            qs = jnp.concatenate([jnp.where(first_map, qh, zero),
                                  jnp.where(first_map, zero, qh)], axis=0)
            kv_rows = slice(b * seq_len, (b + 1) * seq_len)
            parts = [(k_ref[kv_rows, sl], v_ref[kv_rows, sl])]
            scores = [lax.dot_general(qs, parts[0][0], NT_DIMS, preferred_element_type=F32)]
            if has_ctx:
                parts.insert(0, (None, cv_ref[:, hd, :].astype(BF16)))
                scores.insert(0, jnp.dot(qs, ck_ref[sl, :].astype(BF16),
                                         preferred_element_type=F32))
            mx = scores[0].max(axis=-1, keepdims=True)
            for s in scores[1:]:
                mx = jnp.maximum(mx, s.max(axis=-1, keepdims=True))
            acc = None
            for s, (_, vv) in zip(scores, parts):
                e = jnp.exp2(s - mx).astype(BF16)
                v_one = jnp.concatenate([vv, jnp.ones_like(vv)], axis=1)
                pv = jnp.dot(e, v_one, preferred_element_type=F32)
                acc = pv if acc is None else acc + pv
            num = acc[:, 0:HEAD_W] / acc[:, HEAD_W:2 * HEAD_W]
            o = num[0:ti] - lam * num[ti:2 * ti]
            o = _rms(o, sg_ref[...]) * (1.0 - lam_init)
            o_ref[q_rows, sl] = o.astype(o_ref.dtype)


def _attention(q, k, v, ctx_k, ctx_v, lam_params, subln_g, n_batch, seq_len, lam_init,
               cast_weights=()):
    has_ctx = ctx_k is not None
    tq = min(1024, seq_len)
    n_q = seq_len // tq
    n_seq = 1 if n_q > 1 else min(4, n_batch)
    in_specs = [pl.BlockSpec((4, HEAD_DIM), lambda b, i: (0, 0)),
                pl.BlockSpec((1, HEAD_W), lambda b, i: (0, 0)),
                pl.BlockSpec((n_seq * tq, ATT_WIDTH), lambda b, i: (b * n_q + i, 0))]
    args = [lam_params, subln_g.reshape(1, HEAD_W), q]
    if has_ctx:
        past = ctx_v.shape[1]
        in_specs += [pl.BlockSpec((None, ATT_WIDTH, past), lambda b, i: (b, 0, 0)),
                     pl.BlockSpec((None, past, N_HEADS, HEAD_W), lambda b, i: (b, 0, 0, 0))]
        args += [ctx_k, ctx_v]
    kv_spec = pl.BlockSpec((n_seq * seq_len, ATT_WIDTH), lambda b, i: (b, 0))
    in_specs += [kv_spec, kv_spec]
    args += [k, v]
    out_specs = [pl.BlockSpec((n_seq * tq, ATT_WIDTH), lambda b, i: (b * n_q + i, 0))]
    out_shape = [jax.ShapeDtypeStruct((n_batch * seq_len, ATT_WIDTH), BF16)]
    n_steps = (n_batch // n_seq) * n_q
    for w in cast_weights:
        rows = w.shape[0] // n_steps
        spec = pl.BlockSpec((rows, w.shape[1]), lambda b, i: (b * n_q + i, 0))
        in_specs.append(spec)
        args.append(w)
        out_specs.append(spec)
        out_shape.append(jax.ShapeDtypeStruct(w.shape, BF16))
    outs = pl.pallas_call(
        functools.partial(_attn_kernel, has_ctx=has_ctx, lam_init=lam_init,
                          n_seq=n_seq, seq_len=seq_len, tq=tq, n_cast=len(cast_weights)),
        grid=(n_batch // n_seq, n_q),
        in_specs=in_specs,
        out_specs=out_specs,
        out_shape=out_shape,
        compiler_params=_cparams(2),
        name="diff_attention",
    )(*args)
    return outs[0], tuple(outs[1:])


def _cmul(ar, ai, br, bi):
    return ar * br - ai * bi, ar * bi + ai * br


def _ssm_prep_kernel(lre_ref, lim_ref, ls_ref, bre_ref, bim_ref, cre_ref, cim_ref, d_ref,
                     mt_ref, gt_ref, wo_ref, at_ref):
    lane = lax.broadcasted_iota(jnp.int32, (SSM_GROUP, CW), 1)
    chan = lax.broadcasted_iota(jnp.int32, (SSM_GROUP, CW), 0)
    for gi in range(GROUP_BLOCK):
        gt_cols, wo_cols, at_cols, toeplitz = [], [], [], []
        for dr in range(2):
            lr = jnp.minimum(lre_ref[dr, gi], -1e-4)
            li = lim_ref[dr, gi]
            step = jnp.exp(ls_ref[dr, gi])
            mag = jnp.exp(lr * step)
            a_re = mag * jnp.cos(li * step)
            a_im = mag * jnp.sin(li * step)
            den = lr * lr + li * li
            nr = a_re - 1.0
            f_re = (nr * lr + a_im * li) / den
            f_im = (a_im * lr - nr * li) / den
            bt_re, bt_im = bre_ref[dr, gi], bim_ref[dr, gi]
            bb_re, bb_im = _cmul(f_re, f_im, bt_re, bt_im)
            c_re, c_im = cre_ref[dr, gi], cim_ref[dr, gi]
            pw = [(jnp.ones_like(a_re), jnp.zeros_like(a_im))]
            for _ in range(CHUNK):
                pw.append(_cmul(pw[-1][0], pw[-1][1], a_re, a_im))
            g_re, g_im, e_re, e_im = [], [], [], []
            for t in range(CHUNK):
                pr, pi = pw[CHUNK - 1 - t] if dr == 0 else pw[t]
                r, i = _cmul(bb_re, bb_im, pr, pi)
                g_re.append(r)
                g_im.append(i)
                pr, pi = pw[t + 1] if dr == 0 else pw[CHUNK - t]
                r, i = _cmul(c_re, c_im, pr, pi)
                e_re.append(r)
                e_im.append(-i)
            g_cat = jnp.concatenate([jnp.concatenate(g_re, axis=0),
                                     jnp.concatenate(g_im, axis=0)], axis=1)
            gt_cols.append(g_cat)
            wo_cols.append(jnp.concatenate([jnp.concatenate(e_re, axis=0),
                                            jnp.concatenate(e_im, axis=0)], axis=1))
            c_cat = jnp.concatenate([c_re, -c_im], axis=1)
            toeplitz.append(lax.dot_general(c_cat, g_cat, NT_DIMS,
                                            precision=lax.Precision.HIGHEST,
                                            preferred_element_type=F32))
            apw = [pw[CHUNK]]
            for _ in range(SCAN_BLOCK - 1):
                apw.append(_cmul(apw[-1][0], apw[-1][1], apw[0][0], apw[0][1]))
            order = list(range(SCAN_BLOCK)) if dr == 0 else list(range(SCAN_BLOCK - 1, -1, -1))
            order += [2 ** l - 1 for l in range(SCAN_LEVELS)]
            order += [0] * (AT_ROWS - len(order))
            at_cols += [jnp.concatenate([jnp.concatenate([apw[i][0], apw[i][0]], axis=1)
                                         for i in order], axis=0),
                        jnp.concatenate([jnp.concatenate([-apw[i][1], apw[i][1]], axis=1)
                                         for i in order], axis=0)]
        kf_rev, kb = toeplitz
        d_skip = d_ref[gi]
        blocks = []
        for t in range(CHUNK):
            fwd = pltpu.roll(kf_rev, (CW - (CHUNK - 1 - t) * SSM_GROUP) % CW, axis=1)
            bwd = pltpu.roll(kb, t * SSM_GROUP, axis=1)
            blocks.append(jnp.where(lane < (t + 1) * SSM_GROUP, fwd, 0.0)
                          + jnp.where(lane >= t * SSM_GROUP, bwd, 0.0)
                          + jnp.where(lane == chan + t * SSM_GROUP, d_skip, 0.0))
        mt_ref[gi] = jnp.concatenate(blocks, axis=0).astype(mt_ref.dtype)
        gt_ref[gi] = jnp.concatenate(gt_cols, axis=1).astype(gt_ref.dtype)
        wo_ref[gi] = jnp.concatenate(wo_cols, axis=1).astype(wo_ref.dtype)
        at_ref[gi] = jnp.concatenate(at_cols, axis=1)


N_PREP_IN, N_PREP_OUT, N_MOD_IN = 8, 4, 4


def _prep_mod_kernel(*refs):
    prep_in = refs[:N_PREP_IN]
    mod_in = refs[N_PREP_IN:N_PREP_IN + N_MOD_IN]
    w_in_ref = refs[N_PREP_IN + N_MOD_IN]
    outs = refs[N_PREP_IN + N_MOD_IN + 1:]
    _ssm_prep_kernel(*prep_in, *outs[:N_PREP_OUT])
    _mod_kernel(*mod_in, outs[N_PREP_OUT])
    outs[N_PREP_OUT + 1][...] = w_in_ref[...].astype(BF16)


def _ssm_prep_and_modulation(lam_re, lam_im, log_step, b_re, b_im, c_re, c_im, d_skip,
                             c_ctx, c, w_mod, b_mod, w_in):
    row = lambda a: a.reshape(2, N_GROUPS, 1, SSM_STATE)
    bt = lambda a: jnp.swapaxes(a, 2, 3)
    d_row = jnp.tile((d_skip[0] + d_skip[1]).reshape(N_GROUPS, 1, SSM_GROUP), (1, 1, CHUNK))
    gb = GROUP_BLOCK
    n_steps = N_GROUPS // gb
    vec_spec = pl.BlockSpec((2, gb, 1, SSM_STATE), lambda i: (0, i, 0, 0))
    mat_spec = pl.BlockSpec((2, gb, SSM_GROUP, SSM_STATE), lambda i: (0, i, 0, 0))
    w_spec = pl.BlockSpec((gb, CW, CW), lambda i: (i, 0, 0))
    w_shape = jax.ShapeDtypeStruct((N_GROUPS, CW, CW), BF16)
    n_mod = w_mod.shape[1]
    tk = D_MODEL // n_steps
    assert 1 + c.shape[0] <= MOD_ROWS
    outs = pl.pallas_call(
        _prep_mod_kernel,
        grid=(n_steps,),
        in_specs=[vec_spec, vec_spec,
                  pl.BlockSpec((2, gb, 1, 1), lambda i: (0, i, 0, 0)),
                  mat_spec, mat_spec, mat_spec, mat_spec,
                  pl.BlockSpec((gb, 1, CW), lambda i: (i, 0, 0)),
                  pl.BlockSpec((1, tk), lambda k: (0, k)),
                  pl.BlockSpec((c.shape[0], tk), lambda k: (0, k)),
                  pl.BlockSpec((tk, n_mod), lambda k: (k, 0)),
                  pl.BlockSpec((1, n_mod), lambda k: (0, 0)),
                  pl.BlockSpec((tk, w_in.shape[1]), lambda k: (k, 0))],
        out_specs=[w_spec, w_spec, w_spec,
                   pl.BlockSpec((gb, AT_ROWS, 4 * 2 * SSM_STATE), lambda i: (i, 0, 0)),
                   pl.BlockSpec((MOD_ROWS, 1, n_mod), lambda k: (0, 0, 0)),
                   pl.BlockSpec((tk, w_in.shape[1]), lambda k: (k, 0))],
        out_shape=[w_shape, w_shape, w_shape,
                   jax.ShapeDtypeStruct((N_GROUPS, AT_ROWS, 4 * 2 * SSM_STATE), F32),
                   jax.ShapeDtypeStruct((MOD_ROWS, 1, n_mod), F32),
                   jax.ShapeDtypeStruct(w_in.shape, BF16)],
        compiler_params=_cparams(1),
        name="ssm_prep_modulation",
    )(row(lam_re), row(lam_im), log_step.reshape(2, N_GROUPS, 1, 1),
      bt(b_re), bt(b_im), c_re, c_im, d_row,
      c_ctx.reshape(1, D_MODEL), c, w_mod, b_mod.reshape(1, n_mod), w_in)
    return outs[:N_PREP_OUT], outs[N_PREP_OUT], outs[N_PREP_OUT + 1]


def _shift_rows(x, m, down):
    n = x.shape[0]
    return pltpu.roll(x, m if down else n - m, axis=0)


def _ssm_kernel(*refs, n_seq, has_h0):
    if has_h0:
        (xt_ref, mt_ref, gt_ref, wo_ref, at_ref, wg_ref, bg_ref, h0_ref,
         z_hbm, zs_ref, zb_ref, zb_sem) = refs
    else:
        (xt_ref, mt_ref, gt_ref, wo_ref, at_ref, wg_ref, bg_ref,
         z_hbm, st_ref, zs_ref, zb_ref, zb_sem, fin_ref) = refs
    step = pl.program_id(0)
    n_rows = xt_ref.shape[-1]
    seg = n_rows // n_seq
    cw2 = 2 * SSM_STATE
    assert seg % SCAN_BLOCK == 0
    n_blk = n_rows // SCAN_BLOCK
    blk_per_seq = seg // SCAN_BLOCK
    pos = lax.broadcasted_iota(jnp.int32, (n_rows, cw2), 0) % seg
    row_blk = lax.broadcasted_iota(jnp.int32, (SCAN_BLOCK, cw2), 0)

    def low_half(shape):
        return lax.broadcasted_iota(jnp.int32, shape, 1) < SSM_STATE

    def swap(v):
        return pltpu.roll(v, SSM_STATE, axis=1)

    def to_planes(va, vb):
        lo = low_half((va.shape[0], cw2))
        va_l = pltpu.roll(va, 3 * SSM_STATE, axis=1)
        vb_r = pltpu.roll(vb, SSM_STATE, axis=1)
        return (jnp.where(lo, va[:, 0:cw2], vb_r[:, 0:cw2]),
                jnp.where(lo, va_l[:, 0:cw2], vb[:, 0:cw2]),
                jnp.where(lo, va[:, cw2:2 * cw2], vb_r[:, cw2:2 * cw2]),
                jnp.where(lo, va_l[:, cw2:2 * cw2], vb[:, cw2:2 * cw2]))

    def from_planes(f_re, f_im, b_re, b_im):
        lo = low_half(f_re.shape)
        va = jnp.concatenate([jnp.where(lo, f_re, swap(f_im)), jnp.where(lo, b_re, swap(b_im))], axis=1)
        vb = jnp.concatenate([jnp.where(lo, swap(f_re), f_im), jnp.where(lo, swap(b_re), b_im)], axis=1)
        return va, vb

    for ga in range(0, GROUP_BLOCK, 2):
        gb = ga + 1
        xts = [xt_ref[ga], xt_ref[gb]]
        s_pair = [lax.dot_general(xts[i], gt_ref[g], TN_DIMS, preferred_element_type=F32)
                  for i, g in enumerate((ga, gb))]
        planes = to_planes(*s_pair)
        lo_t = low_half((AT_ROWS, cw2))
        if has_h0:
            h0_planes = to_planes(h0_ref[ga], h0_ref[gb])
        ent_planes = []
        for dr in range(2):
            down = dr == 0
            pa, pb = (at_ref[g][:, (2 * dr) * cw2:(2 * dr + 1) * cw2] for g in (ga, gb))
            qa, qb = (at_ref[g][:, (2 * dr + 1) * cw2:(2 * dr + 2) * cw2] for g in (ga, gb))
            ar_tab = jnp.where(lo_t, pa, pb)
            ai_tab = jnp.where(lo_t, -qa, qb)
            re3 = planes[2 * dr].reshape(n_blk, SCAN_BLOCK, cw2)
            im3 = planes[2 * dr + 1].reshape(n_blk, SCAN_BLOCK, cw2)
            for lvl in range(SCAN_LEVELS):
                m = 2 ** lvl
                valid = (row_blk >= m) if down else (row_blk < SCAN_BLOCK - m)
                row = SCAN_BLOCK + lvl
                ar = jnp.where(valid, ar_tab[row:row + 1], 0.0)
                ai = jnp.where(valid, ai_tab[row:row + 1], 0.0)
                shift = m if down else SCAN_BLOCK - m
                sh_re = pltpu.roll(re3, shift, axis=1)
                sh_im = pltpu.roll(im3, shift, axis=1)
                re3, im3 = re3 + ar * sh_re - ai * sh_im, im3 + ar * sh_im + ai * sh_re
            ar_blk, ai_blk = ar_tab[0:SCAN_BLOCK], ai_tab[0:SCAN_BLOCK]
            blk_re = [re3[i] for i in range(n_blk)]
            blk_im = [im3[i] for i in range(n_blk)]
            edge = slice(SCAN_BLOCK - 1, SCAN_BLOCK) if down else slice(0, 1)
            for q_i in range(n_seq):
                idxs = list(range(q_i * blk_per_seq, (q_i + 1) * blk_per_seq))
                idxs = idxs if down else idxs[::-1]
                for prev, cur in zip([None] + idxs[:-1], idxs):
                    if prev is not None:
                        c_re, c_im = blk_re[prev][edge], blk_im[prev][edge]
                    elif has_h0:
                        c_re = h0_planes[2 * dr][q_i:q_i + 1]
                        c_im = h0_planes[2 * dr + 1][q_i:q_i + 1]
                    else:
                        continue
                    c_re = jnp.broadcast_to(c_re, (SCAN_BLOCK, cw2))
                    c_im = jnp.broadcast_to(c_im, (SCAN_BLOCK, cw2))
                    blk_re[cur] = blk_re[cur] + ar_blk * c_re - ai_blk * c_im
                    blk_im[cur] = blk_im[cur] + ar_blk * c_im + ai_blk * c_re
            for part, blks in ((0, blk_re), (1, blk_im)):
                s = jnp.concatenate(blks, axis=0)
                if not has_h0:
                    fin_ref[part] = s
                ent = _shift_rows(s, 1, down)
                ent = jnp.where((pos >= 1) if down else (pos < seg - 1), ent, 0.0)
                if has_h0:
                    ent_blk = [ent[i * SCAN_BLOCK:(i + 1) * SCAN_BLOCK] for i in range(n_blk)]
                    at_edge = row_blk == (0 if down else SCAN_BLOCK - 1)
                    for q_i in range(n_seq):
                        bi = q_i * blk_per_seq if down else (q_i + 1) * blk_per_seq - 1
                        ent_blk[bi] = jnp.where(
                            at_edge, h0_planes[2 * dr + part][q_i:q_i + 1], ent_blk[bi])
                    ent = jnp.concatenate(ent_blk, axis=0)
                ent_planes.append(ent)
            if not has_h0:
                rows = pl.ds(seg - 1 if down else 0, n_seq, stride=seg)
                f_re, f_im = fin_ref[0, rows, :], fin_ref[1, rows, :]
                lo_s = low_half((n_seq, cw2))
                st_ref[ga, :, dr * cw2:(dr + 1) * cw2] = jnp.where(lo_s, f_re, swap(f_im))
                st_ref[gb, :, dr * cw2:(dr + 1) * cw2] = jnp.where(lo_s, swap(f_re), f_im)
        h_pair = from_planes(*ent_planes)
        for i, g in enumerate((ga, gb)):
            yt = (jnp.dot(mt_ref[g], xts[i], preferred_element_type=F32)
                  + lax.dot_general(wo_ref[g], h_pair[i].astype(BF16), NT_DIMS,
                                    preferred_element_type=F32))
            z = jax.nn.gelu(yt, approximate=True)
            grp = step * GROUP_BLOCK + g
            for t in range(CHUNK):
                zs_ref[t, pl.ds(pl.multiple_of(grp * SSM_GROUP, SSM_GROUP), SSM_GROUP), :] = (
                    z[t * SSM_GROUP:(t + 1) * SSM_GROUP, :])

    @pl.when(step == pl.num_programs(0) - 1)
    def _():
        def put(t):
            return pltpu.make_async_copy(zb_ref.at[t], z_hbm.at[:, t, :], zb_sem.at[t])

        for t0 in range(0, CHUNK, Z_BATCH):
            for t in range(t0, t0 + Z_BATCH):
                zt = zs_ref[t]
                gate = jnp.dot(wg_ref[...], zt.astype(BF16),
                               preferred_element_type=F32) + bg_ref[...]
                zb_ref[t] = (zt * jax.nn.sigmoid(gate)).T
            for t in range(t0, t0 + Z_BATCH):
                put(t).start()
        for t in range(CHUNK):
            put(t).wait()


def _ssm(xt, mt, gt, wo, at, w_glu_t, b_glu_col, h0, n_seq):
    n_rows = xt.shape[-1]
    has_h0 = h0 is not None
    gb = GROUP_BLOCK
    w_spec = pl.BlockSpec((gb, CW, CW), lambda i: (i, 0, 0))
    in_specs = [pl.BlockSpec((gb, CW, n_rows), lambda i: (i, 0, 0)),
                w_spec, w_spec, w_spec,
                pl.BlockSpec((gb, AT_ROWS, 4 * 2 * SSM_STATE), lambda i: (i, 0, 0)),
                pl.BlockSpec((SSM_WIDTH, SSM_WIDTH), lambda i: (0, 0)),
                pl.BlockSpec((SSM_WIDTH, 1), lambda i: (0, 0))]
    args = [xt, mt, gt, wo, at, w_glu_t, b_glu_col]
    out_specs = [pl.BlockSpec(memory_space=pl.ANY)]
    out_shape = [jax.ShapeDtypeStruct((n_rows, CHUNK, SSM_WIDTH), F32)]
    scratch = [pltpu.VMEM((CHUNK, SSM_WIDTH, n_rows), F32),
               pltpu.VMEM((CHUNK, n_rows, SSM_WIDTH), F32),
               pltpu.SemaphoreType.DMA((CHUNK,))]
    if has_h0:
        in_specs.append(pl.BlockSpec((gb, n_seq, CW), lambda i: (i, 0, 0)))
        args.append(h0)
    else:
        out_specs.append(pl.BlockSpec((gb, n_seq, CW), lambda i: (i, 0, 0)))
        out_shape.append(jax.ShapeDtypeStruct((N_GROUPS, n_seq, CW), F32))
        scratch.append(pltpu.VMEM((2, n_rows, 2 * SSM_STATE), F32))
    return pl.pallas_call(
        functools.partial(_ssm_kernel, n_seq=n_seq, has_h0=has_h0),
        grid=(N_GROUPS // gb,),
        in_specs=in_specs,
        out_specs=out_specs,
        out_shape=out_shape,
        scratch_shapes=scratch,
        compiler_params=_cparams(1),
        name="ssm_scan_glu",
    )(*args)


FF_CHUNK = 256
FFN_TOK = 1024


def _out_ffn_kernel(x_ref, attn_ref, z_ref, mod_ref, g_ref, wo_hbm, wfi_hbm, wfo_hbm,
                    o_ref, act_ref, wo_ref, wfi_ref, wfo_ref, w_sem):
    first = pl.program_id(0) == 0
    fetch = [pltpu.make_async_copy(src, dst, w_sem.at[i]) for i, (src, dst) in
             enumerate(((wo_hbm, wo_ref), (wfi_hbm, wfi_ref), (wfo_hbm, wfo_ref)))]

    @pl.when(first)
    def _():
        for cp in fetch:
            cp.start()
        fetch[0].wait()

    gate1 = mod_ref[:, 2 * D_MODEL:3 * D_MODEL]
    shift2 = mod_ref[:, 3 * D_MODEL:4 * D_MODEL]
    scale2 = mod_ref[:, 4 * D_MODEL:5 * D_MODEL]
    gate2 = mod_ref[:, 5 * D_MODEL:6 * D_MODEL]
    half = FFN_TOK // 2
    gain2 = g_ref[2:3, :] * (1.0 + scale2)

    def pre_ffn(hf):
        r = slice(hf * half, (hf + 1) * half)
        mixer = jnp.concatenate([attn_ref[r, :], z_ref[r, :].astype(BF16)], axis=1)
        mix = jnp.dot(mixer, wo_ref[...], preferred_element_type=F32)
        x1 = x_ref[r, :] + gate1 * _rms(mix, g_ref[1:2, :])
        ms = jnp.mean(x1 * x1, axis=-1, keepdims=True)
        return x1, (x1 * lax.rsqrt(ms + NORM_EPS) * gain2 + shift2).astype(BF16)

    def ffn_in(hf, h, chunks):
        r = slice(hf * half, (hf + 1) * half)
        for c in chunks:
            lo = c * FF_CHUNK
            gt = jnp.dot(h, wfi_ref[:, lo:lo + FF_CHUNK], preferred_element_type=F32)
            up = jnp.dot(h, wfi_ref[:, D_FF + lo:D_FF + lo + FF_CHUNK],
                         preferred_element_type=F32)
            act_ref[r, lo:lo + FF_CHUNK] = (_silu(gt) * up).astype(BF16)

    def ffn_out(hf, x1):
        r = slice(hf * half, (hf + 1) * half)
        f = jnp.dot(act_ref[r, :], wfo_ref[...], preferred_element_type=F32)
        o_ref[r, :] = x1 + gate2 * _rms(f, g_ref[3:4, :])

    n_chunks = D_FF // FF_CHUNK
    x1_a, h_a = pre_ffn(0)
    x1_b, h_b = pre_ffn(1)
    @pl.when(first)
    def _():
        fetch[1].wait()
        fetch[2].wait()

    ffn_in(0, h_a, range(n_chunks))
    ffn_out(0, x1_a)
    ffn_in(1, h_b, range(n_chunks))
    ffn_out(1, x1_b)


def _out_ffn(x2d, attn, z, mods, mod_rows, norm_g, w_o, w_ffn_in, w_ffn_out):
    n_tok = x2d.shape[0]
    z2d = z.reshape(n_tok, SSM_WIDTH)
    n_steps = n_tok // FFN_TOK
    mod_row0, n_mod = mod_rows
    steps_per_mod = n_steps // n_mod
    const = lambda i: (0, 0)
    row_spec = lambda w: pl.BlockSpec((FFN_TOK, w), lambda i: (i, 0))
    return pl.pallas_call(
        _out_ffn_kernel,
        grid=(n_steps,),
        in_specs=[row_spec(D_MODEL), row_spec(ATT_WIDTH), row_spec(SSM_WIDTH),
                  pl.BlockSpec((None, 1, N_MOD * D_MODEL),
                               lambda i: (mod_row0 + i // steps_per_mod, 0, 0)),
                  pl.BlockSpec((4, D_MODEL), const),
                  pl.BlockSpec(memory_space=pl.ANY),
                  pl.BlockSpec(memory_space=pl.ANY),
                  pl.BlockSpec(memory_space=pl.ANY)],
        out_specs=row_spec(D_MODEL),
        out_shape=jax.ShapeDtypeStruct((n_tok, D_MODEL), F32),
        scratch_shapes=[pltpu.VMEM((FFN_TOK, D_FF), BF16),
                        pltpu.VMEM((2 * ATT_WIDTH, D_MODEL), BF16),
                        pltpu.VMEM((D_MODEL, 2 * D_FF), BF16),
                        pltpu.VMEM((D_FF, D_MODEL), BF16),
                        pltpu.SemaphoreType.DMA((3,))],
        compiler_params=_cparams(1),
        name="out_proj_ffn",
    )(x2d, attn, z2d, mods, norm_g, w_o, w_ffn_in, w_ffn_out)


def _rope_tables(seq_len):
    t = np.arange(seq_len)
    row = (t // GRID_W).astype(np.float32)
    col = (t % GRID_W).astype(np.float32)
    half = HEAD_DIM // 2
    inv_freq = (np.float32(ROPE_BASE)
                ** (-np.arange(0, half, 2, dtype=np.float32) / np.float32(half))).astype(np.float32)
    ang_r = row[:, None] * inv_freq
    ang_c = col[:, None] * inv_freq
    ang = np.concatenate([ang_r, ang_r, ang_c, ang_c], axis=-1)
    cos, sin = np.cos(ang), np.sin(ang)
    upper = (np.arange(HEAD_DIM) % 32) < 16
    sa = np.where(upper, -sin, 0.0)
    sb = np.where(upper, 0.0, sin)
    two = lambda a: jnp.asarray(np.concatenate([a, a], axis=-1), dtype=F32)
    return two(cos), two(sa), two(sb)


def _layer(x, mods, mod_rows, lam_init, rope_tabs, ctx_k, ctx_v, h0, weights, prep):
    n_batch, seq_len = x.shape[:2]
    g = weights['norm_g']
    outs = _in_proj(x, mods, mod_rows, g[0:1], weights['w_in'], rope_tabs)
    q, k, v = outs[:3]
    pending = () if 'late_bf16' in weights else weights['late_f32']
    attn, cast = _attention(q, k, v, ctx_k, ctx_v, weights['lam'], weights['subln_g'],
                            n_batch, seq_len, lam_init, cast_weights=pending)
    if pending:
        weights['late_bf16'] = cast
    ssm_out = _ssm(outs[-1], *prep, weights['w_glu_t'], weights['b_glu_col'], h0, n_batch)
    y = _out_ffn(x.reshape(n_batch * seq_len, D_MODEL), attn, ssm_out[0], mods, mod_rows, g,
                 *weights['late_bf16'])
    return y.reshape(x.shape), outs[3:-1], ssm_out[1:]


def kernel(x_prompt, x_sample, cache_k, cache_v, state_ssm_re, state_ssm_im, c, c_ctx, w_mod, b_mod, norm_g, w_in, lam_params, subln_g, ssm_lambda_re, ssm_lambda_im, ssm_log_step, ssm_b_re, ssm_b_im, ssm_c_re, ssm_c_im, ssm_d, w_glu, b_glu, w_o, w_ffn_in, w_ffn_out):
    depth = w_mod.shape[0]
    assert depth == 1
    bp = x_prompt.shape[0]
    bd, ld_len = x_sample.shape[:2]
    past = cache_k.shape[2]
    xp, xs = x_prompt, x_sample
    rope_tabs = _rope_tables(ld_len)
    ks_out, vs_out, hr_out, hi_out = [], [], [], []
    for l in range(depth):
        lam_init = 0.8 - 0.6 * math.exp(-0.3 * l)
        prep, mods, w_in_bf16 = _ssm_prep_and_modulation(
            ssm_lambda_re[l], ssm_lambda_im[l], ssm_log_step[l], ssm_b_re[l], ssm_b_im[l],
            ssm_c_re[l], ssm_c_im[l], ssm_d[l], c_ctx, c, w_mod[l], b_mod[l], w_in[l])
        weights = {
            'norm_g': norm_g[l],
            'w_in': w_in_bf16,
            'lam': lam_params[l], 'subln_g': subln_g[l],
            'w_glu_t': w_glu[l].T.astype(BF16), 'b_glu_col': b_glu[l].reshape(SSM_WIDTH, 1),
            'late_f32': (w_o[l], w_ffn_in[l], w_ffn_out[l]),
        }
        ck = jnp.transpose(cache_k[:, l], (0, 2, 3, 1)).reshape(bd, ATT_WIDTH, past)
        cv = cache_v[:, l]
        h0 = jnp.stack([state_ssm_re[:, l], state_ssm_im[:, l]], axis=2)
        h0 = h0.transpose(3, 0, 1, 2, 4).reshape(N_GROUPS, bd, CW)
        xs, _, _ = _layer(xs, mods, (1, bd), lam_init, rope_tabs, ck, cv, h0, weights, prep)
        xp, (k_ctx, v_ctx), (st,) = _layer(xp, mods, (0, 1), lam_init, None, None, None, None,
                                           weights, prep)
        ks_out.append(jnp.swapaxes(k_ctx, 1, 2).reshape(bp, -1, 2 * N_HEADS, HEAD_DIM))
        vs_out.append(v_ctx)
        fin = st.reshape(N_GROUPS, bp, 2, 2, SSM_STATE).transpose(1, 2, 3, 0, 4)
        hr_out.append(fin[:, :, 0])
        hi_out.append(fin[:, :, 1])
    return (xp, xs, jnp.stack(ks_out, axis=1), jnp.stack(vs_out, axis=1),
            jnp.stack(hr_out, axis=1), jnp.stack(hi_out, axis=1))
```

```python
import functools
import math

import jax
import jax.numpy as jnp
import numpy as np
from jax import lax
from jax.experimental import pallas as pl
from jax.experimental.pallas import tpu as pltpu

F32 = jnp.float32
BF16 = jnp.bfloat16

D_MODEL = 1024
GRID_W = 64
ATT_WIDTH = 512
SSM_WIDTH = 512
HEAD_DIM = 64
N_HEADS = 4
HEAD_W = 2 * HEAD_DIM
SSM_GROUP = 16
N_GROUPS = 32
SSM_STATE = 64
D_FF = 2816
N_MOD = 6
ROPE_BASE = 10000.0
NORM_EPS = 1e-6

CHUNK = 16
CW = CHUNK * SSM_GROUP
SCAN_BLOCK = 8
SCAN_LEVELS = 3
AT_ROWS = 16
GROUP_BLOCK = 4
Z_BATCH = 8

ROWS_PER_TILE = 128
TOK_PER_TILE = ROWS_PER_TILE * CHUNK
SUB_TOK = 1024
TQ_ITEM = 256
N_SUB = TOK_PER_TILE // SUB_TOK

VMEM_LIMIT = 56 * 1024 * 1024
N_DMA_THREADS = 2

NT_DIMS = (((1,), (1,)), ((), ()))
TN_DIMS = (((0,), (0,)), ((), ()))


def _cparams(n_axes):
    return pltpu.CompilerParams(
        dimension_semantics=("arbitrary",) * n_axes,
        vmem_limit_bytes=VMEM_LIMIT)


def _rms(x, g):
    ms = jnp.mean(x * x, axis=-1, keepdims=True)
    return x * lax.rsqrt(ms + NORM_EPS) * g


def _silu(x):
    return x * jax.nn.sigmoid(x)


MOD_ROWS = 8


def _mod_kernel(ctx_ref, c_ref, w_ref, b_ref, o_ref):
    n_lat, tk = c_ref.shape
    row = lax.broadcasted_iota(jnp.int32, (MOD_ROWS, tk), 0)
    cond = jnp.where(row == 0, ctx_ref[...], 0.0)
    for b in range(n_lat):
        cond = jnp.where(row == 1 + b, c_ref[b:b + 1, :], cond)
    part = jnp.dot(_silu(cond).astype(BF16), w_ref[...].astype(BF16),
                   preferred_element_type=F32)

    @pl.when(pl.program_id(0) == 0)
    def _():
        o_ref[:, 0, :] = part + b_ref[...]

    @pl.when(pl.program_id(0) > 0)
    def _():
        o_ref[:, 0, :] += part


def _rope(x, cos, sa, sb):
    return (x * cos + pltpu.roll(x, HEAD_W - 16, axis=1) * sa
            + pltpu.roll(x, 16, axis=1) * sb)


def _in_proj_kernel(*refs, rope, seqs_per_sub):
    x_ref, x3_hbm, mod_ref, g_ref, w_ref = refs[:5]
    refs = refs[5:]
    if rope:
        cos_ref, sa_ref, sb_ref = refs[:3]
        refs = refs[3:]
        q_ref, k_ref, v_ref, ut_ref, wut_ref, xt_ref, xt_sem = refs
    else:
        q_ref, k_ref, v_ref, kc_ref, vc_ref, ut_ref, wut_ref, xt_ref, xt_sem = refs
    tile = pl.program_id(0)
    j = pl.program_id(1)
    t_per_sub = CHUNK // N_SUB
    t_early = CHUNK - t_per_sub

    def gather(tile_idx, t):
        src = x3_hbm.at[pl.ds(tile_idx * ROWS_PER_TILE, ROWS_PER_TILE), t, :]
        return pltpu.make_async_copy(src, xt_ref.at[t], xt_sem.at[t])

    @pl.when(j == 0)
    def _():
        @pl.when(tile == 0)
        def _():
            for t in range(t_early):
                gather(0, t).start(priority=t % N_DMA_THREADS)
        for t in range(t_early, CHUNK):
            gather(tile, t).start(priority=t % N_DMA_THREADS)

    t_base = j * t_per_sub
    for d in range(t_per_sub):
        gather(tile, t_base + d).wait()

    @pl.when((j == N_SUB - 1) & (tile + 1 < pl.num_programs(0)))
    def _():
        for t in range(t_early):
            gather(tile + 1, t).start(priority=t % N_DMA_THREADS)

    @pl.when((tile == 0) & (j == 0))
    def _():
        wut_ref[...] = w_ref[:, 3 * ATT_WIDTH:].T

    shift = mod_ref[:, 0:D_MODEL]
    gain = g_ref[...] * (1.0 + mod_ref[:, D_MODEL:2 * D_MODEL])

    def norm_mod(xv):
        ms = jnp.mean(xv * xv, axis=-1, keepdims=True)
        return (xv * lax.rsqrt(ms + NORM_EPS) * gain + shift).astype(BF16)

    def ssm_input(d0):
        xt = jnp.concatenate([xt_ref[t_base + d0], xt_ref[t_base + d0 + 1]], axis=0)
        ut = lax.dot_general(wut_ref[...], norm_mod(xt), NT_DIMS,
                             preferred_element_type=F32)
        for d in range(2):
            blk = ut[:, d * ROWS_PER_TILE:(d + 1) * ROWS_PER_TILE]
            row0 = pl.multiple_of((t_base + d0 + d) * SSM_GROUP, SSM_GROUP)
            ut_ref[:, pl.ds(row0, SSM_GROUP), :] = (
                blk.reshape(N_GROUPS, SSM_GROUP, ROWS_PER_TILE).astype(ut_ref.dtype))

    proj = jnp.dot(norm_mod(x_ref[...]), w_ref[:, 0:3 * ATT_WIDTH],
                   preferred_element_type=F32)
    q = proj[:, 0:ATT_WIDTH]
    k = proj[:, ATT_WIDTH:2 * ATT_WIDTH]
    v = proj[:, 2 * ATT_WIDTH:3 * ATT_WIDTH]
    qscale = HEAD_DIM ** -0.5 * math.log2(math.e)
    if rope:
        cos, sa, sb = cos_ref[...], sa_ref[...], sb_ref[...]
        for hd in range(N_HEADS):
            sl = slice(hd * HEAD_W, (hd + 1) * HEAD_W)
            q_ref[:, sl] = (_rope(q[:, sl], cos, sa, sb) * qscale).astype(q_ref.dtype)
            k_ref[:, sl] = _rope(k[:, sl], cos, sa, sb).astype(k_ref.dtype)
    else:
        q_ref[...] = (q * qscale).astype(q_ref.dtype)
        k_ref[...] = k.astype(k_ref.dtype)
        seq = SUB_TOK // seqs_per_sub
        k_t = k.T
        for b in range(seqs_per_sub):
            kc_ref[b] = k_t[:, b * seq:(b + 1) * seq]
            for hd in range(N_HEADS):
                vc_ref[b, :, hd, :] = v[b * seq:(b + 1) * seq, hd * HEAD_W:(hd + 1) * HEAD_W]
    v_ref[...] = v.astype(v_ref.dtype)

    for d0 in range(0, t_per_sub, 2):
        ssm_input(d0)


def _in_proj(x, mods, mod_rows, g0, w_in, rope_tabs):
    n_batch, seq_len = x.shape[:2]
    n_tok = n_batch * seq_len
    n_rows = n_tok // CHUNK
    n_tiles = n_tok // TOK_PER_TILE
    mod_row0, n_mod = mod_rows
    tiles_per_mod = n_tiles // n_mod
    rope = rope_tabs is not None
    seqs_per_sub = max(1, SUB_TOK // seq_len)
    in_specs = [pl.BlockSpec((SUB_TOK, D_MODEL), lambda i, j: (i * N_SUB + j, 0)),
                pl.BlockSpec(memory_space=pl.ANY),
                pl.BlockSpec((None, 1, 2 * D_MODEL),
                             lambda i, j: (mod_row0 + i // tiles_per_mod, 0, 0)),
                pl.BlockSpec((1, D_MODEL), lambda i, j: (0, 0)),
                pl.BlockSpec((D_MODEL, 4 * ATT_WIDTH), lambda i, j: (0, 0))]
    args = [x.reshape(n_tok, D_MODEL), x.reshape(n_rows, CHUNK, D_MODEL), mods, g0, w_in]
    row_spec = pl.BlockSpec((SUB_TOK, ATT_WIDTH), lambda i, j: (i * N_SUB + j, 0))
    row_shape = jax.ShapeDtypeStruct((n_tok, ATT_WIDTH), BF16)
    out_specs = [row_spec, row_spec, row_spec]
    out_shape = [row_shape, row_shape, row_shape]
    if rope:
        assert seq_len == TOK_PER_TILE
        for tab in rope_tabs:
            in_specs.append(pl.BlockSpec((SUB_TOK, HEAD_W), lambda i, j: (j, 0)))
            args.append(tab)
    else:
        out_specs += [pl.BlockSpec((seqs_per_sub, ATT_WIDTH, seq_len),
                                   lambda i, j: (i * N_SUB + j, 0, 0)),
                      pl.BlockSpec((seqs_per_sub, seq_len, N_HEADS, HEAD_W),
                                   lambda i, j: (i * N_SUB + j, 0, 0, 0))]
        out_shape += [jax.ShapeDtypeStruct((n_batch, ATT_WIDTH, seq_len), F32),
                      jax.ShapeDtypeStruct((n_batch, seq_len, N_HEADS, HEAD_W), F32)]
    out_specs.append(pl.BlockSpec((N_GROUPS, CW, ROWS_PER_TILE), lambda i, j: (0, 0, i)))
    out_shape.append(jax.ShapeDtypeStruct((N_GROUPS, CW, n_rows), BF16))
    return pl.pallas_call(
        functools.partial(_in_proj_kernel, rope=rope, seqs_per_sub=seqs_per_sub),
        grid=(n_tiles, N_SUB),
        in_specs=in_specs,
        out_specs=out_specs,
        out_shape=out_shape,
        scratch_shapes=[pltpu.VMEM((SSM_WIDTH, D_MODEL), BF16),
                        pltpu.VMEM((CHUNK, ROWS_PER_TILE, D_MODEL), F32),
                        pltpu.SemaphoreType.DMA((CHUNK,))],
        compiler_params=_cparams(2),
        name="in_proj",
    )(*args)


def _attn_kernel(*refs, has_ctx, lam_init, n_seq, seq_len, tq, n_cast):
    if n_cast:
        cast_in = refs[len(refs) - 2 * n_cast - 1:len(refs) - n_cast - 1]
        cast_out = refs[len(refs) - n_cast:]
        refs = refs[:len(refs) - 2 * n_cast - 1] + (refs[len(refs) - n_cast - 1],)
        for src, dst in zip(cast_in, cast_out):
            dst[...] = src[...].astype(dst.dtype)
    if has_ctx:
        lam_ref, sg_ref, q_ref, ck_ref, cv_ref, k_ref, v_ref, o_ref = refs
    else:
        lam_ref, sg_ref, q_ref, k_ref, v_ref, o_ref = refs
    lp = lam_ref[...]
    lam = (jnp.exp(jnp.sum(lp[0:1] * lp[1:2], axis=-1, keepdims=True))
           - jnp.exp(jnp.sum(lp[2:3] * lp[3:4], axis=-1, keepdims=True)) + lam_init)
    first_map = lax.broadcasted_iota(jnp.int32, (1, HEAD_W), 1) < HEAD_DIM
    ti = min(TQ_ITEM, tq)
    for row0 in range(0, n_seq * tq, ti):
        b = row0 // tq
        q_rows = slice(row0, row0 + ti)
        for hd in range(N_HEADS):
            sl = slice(hd * HEAD_W, (hd + 1) * HEAD_W)
            qh = q_ref[q_rows, sl]
            zero = jnp.zeros_like(qh)
            qs = jnp.concatenate([jnp.where(first_map, qh, zero),
                                  jnp.where(first_map, zero, qh)], axis=0)
            kv_rows = slice(b * seq_len, (b + 1) * seq_len)
            parts = [(k_ref[kv_rows, sl], v_ref[kv_rows, sl])]
            scores = [lax.dot_general(qs, parts[0][0], NT_DIMS, preferred_element_type=F32)]
            if has_ctx:
                parts.insert(0, (None, cv_ref[:, hd, :].astype(BF16)))
                scores.insert(0, jnp.dot(qs, ck_ref[sl, :].astype(BF16),
                                         preferred_element_type=F32))
            mx = scores[0].max(axis=-1, keepdims=True)
            for s in scores[1:]:
                mx = jnp.maximum(mx, s.max(axis=-1, keepdims=True))
            acc = None
            for s, (_, vv) in zip(scores, parts):
                e = jnp.exp2(s - mx).astype(BF16)
                v_one = jnp.concatenate([vv, jnp.ones_like(vv)], axis=1)
                pv = jnp.dot(e, v_one, preferred_element_type=F32)
                acc = pv if acc is None else acc + pv
            num = acc[:, 0:HEAD_W] / acc[:, HEAD_W:2 * HEAD_W]
            o = num[0:ti] - lam * num[ti:2 * ti]
            o = _rms(o, sg_ref[...]) * (1.0 - lam_init)
            o_ref[q_rows, sl] = o.astype(o_ref.dtype)


def _attention(q, k, v, ctx_k, ctx_v, lam_params, subln_g, n_batch, seq_len, lam_init,
               cast_weights=()):
    has_ctx = ctx_k is not None
    tq = min(1024, seq_len)
    n_q = seq_len // tq
    n_seq = 1 if n_q > 1 else min(4, n_batch)
    in_specs = [pl.BlockSpec((4, HEAD_DIM), lambda b, i: (0, 0)),
                pl.BlockSpec((1, HEAD_W), lambda b, i: (0, 0)),
                pl.BlockSpec((n_seq * tq, ATT_WIDTH), lambda b, i: (b * n_q + i, 0))]
    args = [lam_params, subln_g.reshape(1, HEAD_W), q]
    if has_ctx:
        past = ctx_v.shape[1]
        in_specs += [pl.BlockSpec((None, ATT_WIDTH, past), lambda b, i: (b, 0, 0)),
                     pl.BlockSpec((None, past, N_HEADS, HEAD_W), lambda b, i: (b, 0, 0, 0))]
        args += [ctx_k, ctx_v]
    kv_spec = pl.BlockSpec((n_seq * seq_len, ATT_WIDTH), lambda b, i: (b, 0))
    in_specs += [kv_spec, kv_spec]
    args += [k, v]
    out_specs = [pl.BlockSpec((n_seq * tq, ATT_WIDTH), lambda b, i: (b * n_q + i, 0))]
    out_shape = [jax.ShapeDtypeStruct((n_batch * seq_len, ATT_WIDTH), BF16)]
    n_steps = (n_batch // n_seq) * n_q
    for w in cast_weights:
        rows = w.shape[0] // n_steps
        spec = pl.BlockSpec((rows, w.shape[1]), lambda b, i: (b * n_q + i, 0))
        in_specs.append(spec)
        args.append(w)
        out_specs.append(spec)
        out_shape.append(jax.ShapeDtypeStruct(w.shape, BF16))
    outs = pl.pallas_call(
        functools.partial(_attn_kernel, has_ctx=has_ctx, lam_init=lam_init,
                          n_seq=n_seq, seq_len=seq_len, tq=tq, n_cast=len(cast_weights)),
        grid=(n_batch // n_seq, n_q),
        in_specs=in_specs,
        out_specs=out_specs,
        out_shape=out_shape,
        compiler_params=_cparams(2),
        name="diff_attention",
    )(*args)
    return outs[0], tuple(outs[1:])


def _cmul(ar, ai, br, bi):
    return ar * br - ai * bi, ar * bi + ai * br


def _ssm_prep_kernel(lre_ref, lim_ref, ls_ref, bre_ref, bim_ref, cre_ref, cim_ref, d_ref,
                     mt_ref, gt_ref, wo_ref, at_ref):
    lane = lax.broadcasted_iota(jnp.int32, (SSM_GROUP, CW), 1)
    chan = lax.broadcasted_iota(jnp.int32, (SSM_GROUP, CW), 0)
    for gi in range(GROUP_BLOCK):
        gt_cols, wo_cols, at_cols, toeplitz = [], [], [], []
        for dr in range(2):
            lr = jnp.minimum(lre_ref[dr, gi], -1e-4)
            li = lim_ref[dr, gi]
            step = jnp.exp(ls_ref[dr, gi])
            mag = jnp.exp(lr * step)
            a_re = mag * jnp.cos(li * step)
            a_im = mag * jnp.sin(li * step)
            den = lr * lr + li * li
            nr = a_re - 1.0
            f_re = (nr * lr + a_im * li) / den
            f_im = (a_im * lr - nr * li) / den
            bt_re, bt_im = bre_ref[dr, gi], bim_ref[dr, gi]
            bb_re, bb_im = _cmul(f_re, f_im, bt_re, bt_im)
            c_re, c_im = cre_ref[dr, gi], cim_ref[dr, gi]
            pw = [(jnp.ones_like(a_re), jnp.zeros_like(a_im))]
            for _ in range(CHUNK):
                pw.append(_cmul(pw[-1][0], pw[-1][1], a_re, a_im))
            g_re, g_im, e_re, e_im = [], [], [], []
            for t in range(CHUNK):
                pr, pi = pw[CHUNK - 1 - t] if dr == 0 else pw[t]
                r, i = _cmul(bb_re, bb_im, pr, pi)
                g_re.append(r)
                g_im.append(i)
                pr, pi = pw[t + 1] if dr == 0 else pw[CHUNK - t]
                r, i = _cmul(c_re, c_im, pr, pi)
                e_re.append(r)
                e_im.append(-i)
            g_cat = jnp.concatenate([jnp.concatenate(g_re, axis=0),
                                     jnp.concatenate(g_im, axis=0)], axis=1)
            gt_cols.append(g_cat)
            wo_cols.append(jnp.concatenate([jnp.concatenate(e_re, axis=0),
                                            jnp.concatenate(e_im, axis=0)], axis=1))
            c_cat = jnp.concatenate([c_re, -c_im], axis=1)
            toeplitz.append(lax.dot_general(c_cat, g_cat, NT_DIMS,
                                            precision=lax.Precision.HIGHEST,
                                            preferred_element_type=F32))
            apw = [pw[CHUNK]]
            for _ in range(SCAN_BLOCK - 1):
                apw.append(_cmul(apw[-1][0], apw[-1][1], apw[0][0], apw[0][1]))
            order = list(range(SCAN_BLOCK)) if dr == 0 else list(range(SCAN_BLOCK - 1, -1, -1))
            order += [2 ** l - 1 for l in range(SCAN_LEVELS)]
            order += [0] * (AT_ROWS - len(order))
            at_cols += [jnp.concatenate([jnp.concatenate([apw[i][0], apw[i][0]], axis=1)
                                         for i in order], axis=0),
                        jnp.concatenate([jnp.concatenate([-apw[i][1], apw[i][1]], axis=1)
                                         for i in order], axis=0)]
        kf_rev, kb = toeplitz
        d_skip = d_ref[gi]
        blocks = []
        for t in range(CHUNK):
            fwd = pltpu.roll(kf_rev, (CW - (CHUNK - 1 - t) * SSM_GROUP) % CW, axis=1)
            bwd = pltpu.roll(kb, t * SSM_GROUP, axis=1)
            blocks.append(jnp.where(lane < (t + 1) * SSM_GROUP, fwd, 0.0)
                          + jnp.where(lane >= t * SSM_GROUP, bwd, 0.0)
                          + jnp.where(lane == chan + t * SSM_GROUP, d_skip, 0.0))
        mt_ref[gi] = jnp.concatenate(blocks, axis=0).astype(mt_ref.dtype)
        gt_ref[gi] = jnp.concatenate(gt_cols, axis=1).astype(gt_ref.dtype)
        wo_ref[gi] = jnp.concatenate(wo_cols, axis=1).astype(wo_ref.dtype)
        at_ref[gi] = jnp.concatenate(at_cols, axis=1)


N_PREP_IN, N_PREP_OUT, N_MOD_IN = 8, 4, 4


def _prep_mod_kernel(*refs):
    prep_in = refs[:N_PREP_IN]
    mod_in = refs[N_PREP_IN:N_PREP_IN + N_MOD_IN]
    w_in_ref = refs[N_PREP_IN + N_MOD_IN]
    outs = refs[N_PREP_IN + N_MOD_IN + 1:]
    _ssm_prep_kernel(*prep_in, *outs[:N_PREP_OUT])
    _mod_kernel(*mod_in, outs[N_PREP_OUT])
    outs[N_PREP_OUT + 1][...] = w_in_ref[...].astype(BF16)


def _ssm_prep_and_modulation(lam_re, lam_im, log_step, b_re, b_im, c_re, c_im, d_skip,
                             c_ctx, c, w_mod, b_mod, w_in):
    row = lambda a: a.reshape(2, N_GROUPS, 1, SSM_STATE)
    bt = lambda a: jnp.swapaxes(a, 2, 3)
    d_row = jnp.tile((d_skip[0] + d_skip[1]).reshape(N_GROUPS, 1, SSM_GROUP), (1, 1, CHUNK))
    gb = GROUP_BLOCK
    n_steps = N_GROUPS // gb
    vec_spec = pl.BlockSpec((2, gb, 1, SSM_STATE), lambda i: (0, i, 0, 0))
    mat_spec = pl.BlockSpec((2, gb, SSM_GROUP, SSM_STATE), lambda i: (0, i, 0, 0))
    w_spec = pl.BlockSpec((gb, CW, CW), lambda i: (i, 0, 0))
    w_shape = jax.ShapeDtypeStruct((N_GROUPS, CW, CW), BF16)
    n_mod = w_mod.shape[1]
    tk = D_MODEL // n_steps
    assert 1 + c.shape[0] <= MOD_ROWS
    outs = pl.pallas_call(
        _prep_mod_kernel,
        grid=(n_steps,),
        in_specs=[vec_spec, vec_spec,
                  pl.BlockSpec((2, gb, 1, 1), lambda i: (0, i, 0, 0)),
                  mat_spec, mat_spec, mat_spec, mat_spec,
                  pl.BlockSpec((gb, 1, CW), lambda i: (i, 0, 0)),
                  pl.BlockSpec((1, tk), lambda k: (0, k)),
                  pl.BlockSpec((c.shape[0], tk), lambda k: (0, k)),
                  pl.BlockSpec((tk, n_mod), lambda k: (k, 0)),
                  pl.BlockSpec((1, n_mod), lambda k: (0, 0)),
                  pl.BlockSpec((tk, w_in.shape[1]), lambda k: (k, 0))],
        out_specs=[w_spec, w_spec, w_spec,
                   pl.BlockSpec((gb, AT_ROWS, 4 * 2 * SSM_STATE), lambda i: (i, 0, 0)),
                   pl.BlockSpec((MOD_ROWS, 1, n_mod), lambda k: (0, 0, 0)),
                   pl.BlockSpec((tk, w_in.shape[1]), lambda k: (k, 0))],
        out_shape=[w_shape, w_shape, w_shape,
                   jax.ShapeDtypeStruct((N_GROUPS, AT_ROWS, 4 * 2 * SSM_STATE), F32),
                   jax.ShapeDtypeStruct((MOD_ROWS, 1, n_mod), F32),
                   jax.ShapeDtypeStruct(w_in.shape, BF16)],
        compiler_params=_cparams(1),
        name="ssm_prep_modulation",
    )(row(lam_re), row(lam_im), log_step.reshape(2, N_GROUPS, 1, 1),
      bt(b_re), bt(b_im), c_re, c_im, d_row,
      c_ctx.reshape(1, D_MODEL), c, w_mod, b_mod.reshape(1, n_mod), w_in)
    return outs[:N_PREP_OUT], outs[N_PREP_OUT], outs[N_PREP_OUT + 1]


def _shift_rows(x, m, down):
    n = x.shape[0]
    return pltpu.roll(x, m if down else n - m, axis=0)


def _ssm_kernel(*refs, n_seq, has_h0):
    if has_h0:
        (xt_ref, mt_ref, gt_ref, wo_ref, at_ref, wg_ref, bg_ref, h0_ref,
         z_hbm, zs_ref, zb_ref, zb_sem) = refs
    else:
        (xt_ref, mt_ref, gt_ref, wo_ref, at_ref, wg_ref, bg_ref,
         z_hbm, st_ref, zs_ref, zb_ref, zb_sem, fin_ref) = refs
    step = pl.program_id(0)
    n_rows = xt_ref.shape[-1]
    seg = n_rows // n_seq
    cw2 = 2 * SSM_STATE
    assert seg % SCAN_BLOCK == 0
    n_blk = n_rows // SCAN_BLOCK
    blk_per_seq = seg // SCAN_BLOCK
    pos = lax.broadcasted_iota(jnp.int32, (n_rows, cw2), 0) % seg
    row_blk = lax.broadcasted_iota(jnp.int32, (SCAN_BLOCK, cw2), 0)

    def low_half(shape):
        return lax.broadcasted_iota(jnp.int32, shape, 1) < SSM_STATE

    def swap(v):
        return pltpu.roll(v, SSM_STATE, axis=1)

    def to_planes(va, vb):
        lo = low_half((va.shape[0], cw2))
        va_l = pltpu.roll(va, 3 * SSM_STATE, axis=1)
        vb_r = pltpu.roll(vb, SSM_STATE, axis=1)
        return (jnp.where(lo, va[:, 0:cw2], vb_r[:, 0:cw2]),
                jnp.where(lo, va_l[:, 0:cw2], vb[:, 0:cw2]),
                jnp.where(lo, va[:, cw2:2 * cw2], vb_r[:, cw2:2 * cw2]),
                jnp.where(lo, va_l[:, cw2:2 * cw2], vb[:, cw2:2 * cw2]))

    def from_planes(f_re, f_im, b_re, b_im):
        lo = low_half(f_re.shape)
        va = jnp.concatenate([jnp.where(lo, f_re, swap(f_im)), jnp.where(lo, b_re, swap(b_im))], axis=1)
        vb = jnp.concatenate([jnp.where(lo, swap(f_re), f_im), jnp.where(lo, swap(b_re), b_im)], axis=1)
        return va, vb

    for ga in range(0, GROUP_BLOCK, 2):
        gb = ga + 1
        xts = [xt_ref[ga], xt_ref[gb]]
        s_pair = [lax.dot_general(xts[i], gt_ref[g], TN_DIMS, preferred_element_type=F32)
                  for i, g in enumerate((ga, gb))]
        planes = to_planes(*s_pair)
        lo_t = low_half((AT_ROWS, cw2))
        if has_h0:
            h0_planes = to_planes(h0_ref[ga], h0_ref[gb])
        ent_planes = []
        for dr in range(2):
            down = dr == 0
            pa, pb = (at_ref[g][:, (2 * dr) * cw2:(2 * dr + 1) * cw2] for g in (ga, gb))
            qa, qb = (at_ref[g][:, (2 * dr + 1) * cw2:(2 * dr + 2) * cw2] for g in (ga, gb))
            ar_tab = jnp.where(lo_t, pa, pb)
            ai_tab = jnp.where(lo_t, -qa, qb)
            re3 = planes[2 * dr].reshape(n_blk, SCAN_BLOCK, cw2)
            im3 = planes[2 * dr + 1].reshape(n_blk, SCAN_BLOCK, cw2)
            for lvl in range(SCAN_LEVELS):
                m = 2 ** lvl
                valid = (row_blk >= m) if down else (row_blk < SCAN_BLOCK - m)
                row = SCAN_BLOCK + lvl
                ar = jnp.where(valid, ar_tab[row:row + 1], 0.0)
                ai = jnp.where(valid, ai_tab[row:row + 1], 0.0)
                shift = m if down else SCAN_BLOCK - m
                sh_re = pltpu.roll(re3, shift, axis=1)
                sh_im = pltpu.roll(im3, shift, axis=1)
                re3, im3 = re3 + ar * sh_re - ai * sh_im, im3 + ar * sh_im + ai * sh_re
            ar_blk, ai_blk = ar_tab[0:SCAN_BLOCK], ai_tab[0:SCAN_BLOCK]
            blk_re = [re3[i] for i in range(n_blk)]
            blk_im = [im3[i] for i in range(n_blk)]
            edge = slice(SCAN_BLOCK - 1, SCAN_BLOCK) if down else slice(0, 1)
            for q_i in range(n_seq):
                idxs = list(range(q_i * blk_per_seq, (q_i + 1) * blk_per_seq))
                idxs = idxs if down else idxs[::-1]
                for prev, cur in zip([None] + idxs[:-1], idxs):
                    if prev is not None:
                        c_re, c_im = blk_re[prev][edge], blk_im[prev][edge]
                    elif has_h0:
                        c_re = h0_planes[2 * dr][q_i:q_i + 1]
                        c_im = h0_planes[2 * dr + 1][q_i:q_i + 1]
                    else:
                        continue
                    c_re = jnp.broadcast_to(c_re, (SCAN_BLOCK, cw2))
                    c_im = jnp.broadcast_to(c_im, (SCAN_BLOCK, cw2))
                    blk_re[cur] = blk_re[cur] + ar_blk * c_re - ai_blk * c_im
                    blk_im[cur] = blk_im[cur] + ar_blk * c_im + ai_blk * c_re
            for part, blks in ((0, blk_re), (1, blk_im)):
                s = jnp.concatenate(blks, axis=0)
                if not has_h0:
                    fin_ref[part] = s
                ent = _shift_rows(s, 1, down)
                ent = jnp.where((pos >= 1) if down else (pos < seg - 1), ent, 0.0)
                if has_h0:
                    ent_blk = [ent[i * SCAN_BLOCK:(i + 1) * SCAN_BLOCK] for i in range(n_blk)]
                    at_edge = row_blk == (0 if down else SCAN_BLOCK - 1)
                    for q_i in range(n_seq):
                        bi = q_i * blk_per_seq if down else (q_i + 1) * blk_per_seq - 1
                        ent_blk[bi] = jnp.where(
                            at_edge, h0_planes[2 * dr + part][q_i:q_i + 1], ent_blk[bi])
                    ent = jnp.concatenate(ent_blk, axis=0)
                ent_planes.append(ent)
            if not has_h0:
                rows = pl.ds(seg - 1 if down else 0, n_seq, stride=seg)
                f_re, f_im = fin_ref[0, rows, :], fin_ref[1, rows, :]
                lo_s = low_half((n_seq, cw2))
                st_ref[ga, :, dr * cw2:(dr + 1) * cw2] = jnp.where(lo_s, f_re, swap(f_im))
                st_ref[gb, :, dr * cw2:(dr + 1) * cw2] = jnp.where(lo_s, swap(f_re), f_im)
        h_pair = from_planes(*ent_planes)
        for i, g in enumerate((ga, gb)):
            yt = (jnp.dot(mt_ref[g], xts[i], preferred_element_type=F32)
                  + lax.dot_general(wo_ref[g], h_pair[i].astype(BF16), NT_DIMS,
                                    preferred_element_type=F32))
            z = jax.nn.gelu(yt, approximate=True)
            grp = step * GROUP_BLOCK + g
            for t in range(CHUNK):
                zs_ref[t, pl.ds(pl.multiple_of(grp * SSM_GROUP, SSM_GROUP), SSM_GROUP), :] = (
                    z[t * SSM_GROUP:(t + 1) * SSM_GROUP, :])

    @pl.when(step == pl.num_programs(0) - 1)
    def _():
        def put(t):
            return pltpu.make_async_copy(zb_ref.at[t], z_hbm.at[:, t, :], zb_sem.at[t])

        for t0 in range(0, CHUNK, Z_BATCH):
            for t in range(t0, t0 + Z_BATCH):
                zt = zs_ref[t]
                gate = jnp.dot(wg_ref[...], zt.astype(BF16),
                               preferred_element_type=F32) + bg_ref[...]
                zb_ref[t] = (zt * jax.nn.sigmoid(gate)).T
            for t in range(t0, t0 + Z_BATCH):
                put(t).start(priority=t % N_DMA_THREADS)
        for t in range(CHUNK):
            put(t).wait()


def _ssm(xt, mt, gt, wo, at, w_glu_t, b_glu_col, h0, n_seq):
    n_rows = xt.shape[-1]
    has_h0 = h0 is not None
    gb = GROUP_BLOCK
    w_spec = pl.BlockSpec((gb, CW, CW), lambda i: (i, 0, 0))
    in_specs = [pl.BlockSpec((gb, CW, n_rows), lambda i: (i, 0, 0)),
                w_spec, w_spec, w_spec,
                pl.BlockSpec((gb, AT_ROWS, 4 * 2 * SSM_STATE), lambda i: (i, 0, 0)),
                pl.BlockSpec((SSM_WIDTH, SSM_WIDTH), lambda i: (0, 0)),
                pl.BlockSpec((SSM_WIDTH, 1), lambda i: (0, 0))]
    args = [xt, mt, gt, wo, at, w_glu_t, b_glu_col]
    out_specs = [pl.BlockSpec(memory_space=pl.ANY)]
    out_shape = [jax.ShapeDtypeStruct((n_rows, CHUNK, SSM_WIDTH), F32)]
    scratch = [pltpu.VMEM((CHUNK, SSM_WIDTH, n_rows), F32),
               pltpu.VMEM((CHUNK, n_rows, SSM_WIDTH), F32),
               pltpu.SemaphoreType.DMA((CHUNK,))]
    if has_h0:
        in_specs.append(pl.BlockSpec((gb, n_seq, CW), lambda i: (i, 0, 0)))
        args.append(h0)
    else:
        out_specs.append(pl.BlockSpec((gb, n_seq, CW), lambda i: (i, 0, 0)))
        out_shape.append(jax.ShapeDtypeStruct((N_GROUPS, n_seq, CW), F32))
        scratch.append(pltpu.VMEM((2, n_rows, 2 * SSM_STATE), F32))
    return pl.pallas_call(
        functools.partial(_ssm_kernel, n_seq=n_seq, has_h0=has_h0),
        grid=(N_GROUPS // gb,),
        in_specs=in_specs,
        out_specs=out_specs,
        out_shape=out_shape,
        scratch_shapes=scratch,
        compiler_params=_cparams(1),
        name="ssm_scan_glu",
    )(*args)


FF_CHUNK = 256
FFN_TOK = 1024


def _out_ffn_kernel(x_ref, attn_ref, z_ref, mod_ref, g_ref, wo_hbm, wfi_hbm, wfo_hbm,
                    o_ref, act_ref, wo_ref, wfi_ref, wfo_ref, w_sem):
    first = pl.program_id(0) == 0
    fetch = [pltpu.make_async_copy(src, dst, w_sem.at[i]) for i, (src, dst) in
             enumerate(((wo_hbm, wo_ref), (wfi_hbm, wfi_ref), (wfo_hbm, wfo_ref)))]

    @pl.when(first)
    def _():
        for i, cp in enumerate(fetch):
            cp.start(priority=min(i, N_DMA_THREADS - 1))
        fetch[0].wait()

    gate1 = mod_ref[:, 2 * D_MODEL:3 * D_MODEL]
    shift2 = mod_ref[:, 3 * D_MODEL:4 * D_MODEL]
    scale2 = mod_ref[:, 4 * D_MODEL:5 * D_MODEL]
    gate2 = mod_ref[:, 5 * D_MODEL:6 * D_MODEL]
    half = FFN_TOK // 2
    gain2 = g_ref[2:3, :] * (1.0 + scale2)

    def pre_ffn(hf):
        r = slice(hf * half, (hf + 1) * half)
        mixer = jnp.concatenate([attn_ref[r, :], z_ref[r, :].astype(BF16)], axis=1)
        mix = jnp.dot(mixer, wo_ref[...], preferred_element_type=F32)
        x1 = x_ref[r, :] + gate1 * _rms(mix, g_ref[1:2, :])
        ms = jnp.mean(x1 * x1, axis=-1, keepdims=True)
        return x1, (x1 * lax.rsqrt(ms + NORM_EPS) * gain2 + shift2).astype(BF16)

    def ffn_in(hf, h, chunks):
        r = slice(hf * half, (hf + 1) * half)
        for c in chunks:
            lo = c * FF_CHUNK
            gt = jnp.dot(h, wfi_ref[:, lo:lo + FF_CHUNK], preferred_element_type=F32)
            up = jnp.dot(h, wfi_ref[:, D_FF + lo:D_FF + lo + FF_CHUNK],
                         preferred_element_type=F32)
            act_ref[r, lo:lo + FF_CHUNK] = (_silu(gt) * up).astype(BF16)

    def ffn_out(hf, x1):
        r = slice(hf * half, (hf + 1) * half)
        f = jnp.dot(act_ref[r, :], wfo_ref[...], preferred_element_type=F32)
        o_ref[r, :] = x1 + gate2 * _rms(f, g_ref[3:4, :])

    n_chunks = D_FF // FF_CHUNK
    x1_a, h_a = pre_ffn(0)
    x1_b, h_b = pre_ffn(1)
    @pl.when(first)
    def _():
        fetch[1].wait()
        fetch[2].wait()

    ffn_in(0, h_a, range(n_chunks))
    ffn_out(0, x1_a)
    ffn_in(1, h_b, range(n_chunks))
    ffn_out(1, x1_b)


def _out_ffn(x2d, attn, z, mods, mod_rows, norm_g, w_o, w_ffn_in, w_ffn_out):
    n_tok = x2d.shape[0]
    z2d = z.reshape(n_tok, SSM_WIDTH)
    n_steps = n_tok // FFN_TOK
    mod_row0, n_mod = mod_rows
    steps_per_mod = n_steps // n_mod
    const = lambda i: (0, 0)
    row_spec = lambda w: pl.BlockSpec((FFN_TOK, w), lambda i: (i, 0))
    return pl.pallas_call(
        _out_ffn_kernel,
        grid=(n_steps,),
        in_specs=[row_spec(D_MODEL), row_spec(ATT_WIDTH), row_spec(SSM_WIDTH),
                  pl.BlockSpec((None, 1, N_MOD * D_MODEL),
                               lambda i: (mod_row0 + i // steps_per_mod, 0, 0)),
                  pl.BlockSpec((4, D_MODEL), const),
                  pl.BlockSpec(memory_space=pl.ANY),
                  pl.BlockSpec(memory_space=pl.ANY),
                  pl.BlockSpec(memory_space=pl.ANY)],
        out_specs=row_spec(D_MODEL),
        out_shape=jax.ShapeDtypeStruct((n_tok, D_MODEL), F32),
        scratch_shapes=[pltpu.VMEM((FFN_TOK, D_FF), BF16),
                        pltpu.VMEM((2 * ATT_WIDTH, D_MODEL), BF16),
                        pltpu.VMEM((D_MODEL, 2 * D_FF), BF16),
                        pltpu.VMEM((D_FF, D_MODEL), BF16),
                        pltpu.SemaphoreType.DMA((3,))],
        compiler_params=_cparams(1),
        name="out_proj_ffn",
    )(x2d, attn, z2d, mods, norm_g, w_o, w_ffn_in, w_ffn_out)


def _rope_tables(seq_len):
    t = np.arange(seq_len)
    row = (t // GRID_W).astype(np.float32)
    col = (t % GRID_W).astype(np.float32)
    half = HEAD_DIM // 2
    inv_freq = (np.float32(ROPE_BASE)
                ** (-np.arange(0, half, 2, dtype=np.float32) / np.float32(half))).astype(np.float32)
    ang_r = row[:, None] * inv_freq
    ang_c = col[:, None] * inv_freq
    ang = np.concatenate([ang_r, ang_r, ang_c, ang_c], axis=-1)
    cos, sin = np.cos(ang), np.sin(ang)
    upper = (np.arange(HEAD_DIM) % 32) < 16
    sa = np.where(upper, -sin, 0.0)
    sb = np.where(upper, 0.0, sin)
    two = lambda a: jnp.asarray(np.concatenate([a, a], axis=-1), dtype=F32)
    return two(cos), two(sa), two(sb)


def _layer(x, mods, mod_rows, lam_init, rope_tabs, ctx_k, ctx_v, h0, weights, prep):
    n_batch, seq_len = x.shape[:2]
    g = weights['norm_g']
    outs = _in_proj(x, mods, mod_rows, g[0:1], weights['w_in'], rope_tabs)
    q, k, v = outs[:3]
    pending = () if 'late_bf16' in weights else weights['late_f32']
    attn, cast = _attention(q, k, v, ctx_k, ctx_v, weights['lam'], weights['subln_g'],
                            n_batch, seq_len, lam_init, cast_weights=pending)
    if pending:
        weights['late_bf16'] = cast
    ssm_out = _ssm(outs[-1], *prep, weights['w_glu_t'], weights['b_glu_col'], h0, n_batch)
    y = _out_ffn(x.reshape(n_batch * seq_len, D_MODEL), attn, ssm_out[0], mods, mod_rows, g,
                 *weights['late_bf16'])
    return y.reshape(x.shape), outs[3:-1], ssm_out[1:]


def kernel(x_prompt, x_sample, cache_k, cache_v, state_ssm_re, state_ssm_im, c, c_ctx, w_mod, b_mod, norm_g, w_in, lam_params, subln_g, ssm_lambda_re, ssm_lambda_im, ssm_log_step, ssm_b_re, ssm_b_im, ssm_c_re, ssm_c_im, ssm_d, w_glu, b_glu, w_o, w_ffn_in, w_ffn_out):
    depth = w_mod.shape[0]
    assert depth == 1
    bp = x_prompt.shape[0]
    bd, ld_len = x_sample.shape[:2]
    past = cache_k.shape[2]
    xp, xs = x_prompt, x_sample
    rope_tabs = _rope_tables(ld_len)
    ks_out, vs_out, hr_out, hi_out = [], [], [], []
    for l in range(depth):
        lam_init = 0.8 - 0.6 * math.exp(-0.3 * l)
        prep, mods, w_in_bf16 = _ssm_prep_and_modulation(
            ssm_lambda_re[l], ssm_lambda_im[l], ssm_log_step[l], ssm_b_re[l], ssm_b_im[l],
            ssm_c_re[l], ssm_c_im[l], ssm_d[l], c_ctx, c, w_mod[l], b_mod[l], w_in[l])
        weights = {
            'norm_g': norm_g[l],
            'w_in': w_in_bf16,
            'lam': lam_params[l], 'subln_g': subln_g[l],
            'w_glu_t': w_glu[l].T.astype(BF16), 'b_glu_col': b_glu[l].reshape(SSM_WIDTH, 1),
            'late_f32': (w_o[l], w_ffn_in[l], w_ffn_out[l]),
        }
        ck = jnp.transpose(cache_k[:, l], (0, 2, 3, 1)).reshape(bd, ATT_WIDTH, past)
        cv = cache_v[:, l]
        h0 = jnp.stack([state_ssm_re[:, l], state_ssm_im[:, l]], axis=2)
        h0 = h0.transpose(3, 0, 1, 2, 4).reshape(N_GROUPS, bd, CW)
        xs, _, _ = _layer(xs, mods, (1, bd), lam_init, rope_tabs, ck, cv, h0, weights, prep)
        xp, (k_ctx, v_ctx), (st,) = _layer(xp, mods, (0, 1), lam_init, None, None, None, None,
                                           weights, prep)
        ks_out.append(jnp.swapaxes(k_ctx, 1, 2).reshape(bp, -1, 2 * N_HEADS, HEAD_DIM))
        vs_out.append(v_ctx)
        fin = st.reshape(N_GROUPS, bp, 2, 2, SSM_STATE).transpose(1, 2, 3, 0, 4)
        hr_out.append(fin[:, :, 0])
        hi_out.append(fin[:, :, 1])
    return (xp, xs, jnp.stack(ks_out, axis=1), jnp.stack(vs_out, axis=1),
            jnp.stack(hr_out, axis=1), jnp.stack(hi_out, axis=1))
```

```python
import functools
import math

import jax
import jax.numpy as jnp
import numpy as np
from jax import lax
from jax.experimental import pallas as pl
from jax.experimental.pallas import tpu as pltpu

F32 = jnp.float32
BF16 = jnp.bfloat16

D_MODEL = 1024
GRID_W = 64
ATT_WIDTH = 512
SSM_WIDTH = 512
HEAD_DIM = 64
N_HEADS = 4
HEAD_W = 2 * HEAD_DIM
SSM_GROUP = 16
N_GROUPS = 32
SSM_STATE = 64
D_FF = 2816
N_MOD = 6
ROPE_BASE = 10000.0
NORM_EPS = 1e-6

CHUNK = 16
CW = CHUNK * SSM_GROUP
SCAN_BLOCK = 8
SCAN_LEVELS = 3
AT_ROWS = 16
GROUP_BLOCK = 4
Z_BATCH = 8

ROWS_PER_TILE = 128
TOK_PER_TILE = ROWS_PER_TILE * CHUNK
SUB_TOK = 1024
TQ_ITEM = 256
N_SUB = TOK_PER_TILE // SUB_TOK

VMEM_LIMIT = 56 * 1024 * 1024
N_DMA_THREADS = 2

NT_DIMS = (((1,), (1,)), ((), ()))
TN_DIMS = (((0,), (0,)), ((), ()))


def _cparams(n_axes, fuse_inputs=None):
    return pltpu.CompilerParams(
        dimension_semantics=("arbitrary",) * n_axes,
        allow_input_fusion=fuse_inputs,
        vmem_limit_bytes=VMEM_LIMIT)


def _rms(x, g):
    ms = jnp.mean(x * x, axis=-1, keepdims=True)
    return x * lax.rsqrt(ms + NORM_EPS) * g


def _silu(x):
    return x * jax.nn.sigmoid(x)


MOD_ROWS = 8


def _mod_kernel(ctx_ref, c_ref, w_ref, b_ref, o_ref):
    n_lat, tk = c_ref.shape
    row = lax.broadcasted_iota(jnp.int32, (MOD_ROWS, tk), 0)
    cond = jnp.where(row == 0, ctx_ref[...], 0.0)
    for b in range(n_lat):
        cond = jnp.where(row == 1 + b, c_ref[b:b + 1, :], cond)
    part = jnp.dot(_silu(cond).astype(BF16), w_ref[...].astype(BF16),
                   preferred_element_type=F32)

    @pl.when(pl.program_id(0) == 0)
    def _():
        o_ref[:, 0, :] = part + b_ref[...]

    @pl.when(pl.program_id(0) > 0)
    def _():
        o_ref[:, 0, :] += part


def _rope(x, cos, sa, sb):
    return (x * cos + pltpu.roll(x, HEAD_W - 16, axis=1) * sa
            + pltpu.roll(x, 16, axis=1) * sb)


def _in_proj_kernel(*refs, rope, seqs_per_sub):
    x_ref, x3_hbm, mod_ref, g_ref, w_ref = refs[:5]
    refs = refs[5:]
    if rope:
        cos_ref, sa_ref, sb_ref = refs[:3]
        refs = refs[3:]
        q_ref, k_ref, v_ref, ut_ref, wut_ref, xt_ref, xt_sem = refs
    else:
        q_ref, k_ref, v_ref, kc_ref, vc_ref, ut_ref, wut_ref, xt_ref, xt_sem = refs
    tile = pl.program_id(0)
    j = pl.program_id(1)
    t_per_sub = CHUNK // N_SUB
    t_early = CHUNK - t_per_sub

    def gather(tile_idx, t):
        src = x3_hbm.at[pl.ds(tile_idx * ROWS_PER_TILE, ROWS_PER_TILE), t, :]
        return pltpu.make_async_copy(src, xt_ref.at[t], xt_sem.at[t])

    @pl.when(j == 0)
    def _():
        @pl.when(tile == 0)
        def _():
            for t in range(t_early):
                gather(0, t).start(priority=t % N_DMA_THREADS)
        for t in range(t_early, CHUNK):
            gather(tile, t).start(priority=t % N_DMA_THREADS)

    t_base = j * t_per_sub
    for d in range(t_per_sub):
        gather(tile, t_base + d).wait()

    @pl.when((j == N_SUB - 1) & (tile + 1 < pl.num_programs(0)))
    def _():
        for t in range(t_early):
            gather(tile + 1, t).start(priority=t % N_DMA_THREADS)

    @pl.when((tile == 0) & (j == 0))
    def _():
        wut_ref[...] = w_ref[:, 3 * ATT_WIDTH:].T

    shift = mod_ref[:, 0:D_MODEL]
    gain = g_ref[...] * (1.0 + mod_ref[:, D_MODEL:2 * D_MODEL])

    def norm_mod(xv):
        ms = jnp.mean(xv * xv, axis=-1, keepdims=True)
        return (xv * lax.rsqrt(ms + NORM_EPS) * gain + shift).astype(BF16)

    def ssm_input(d0):
        xt = jnp.concatenate([xt_ref[t_base + d0], xt_ref[t_base + d0 + 1]], axis=0)
        ut = lax.dot_general(wut_ref[...], norm_mod(xt), NT_DIMS,
                             preferred_element_type=F32)
        for d in range(2):
            blk = ut[:, d * ROWS_PER_TILE:(d + 1) * ROWS_PER_TILE]
            row0 = pl.multiple_of((t_base + d0 + d) * SSM_GROUP, SSM_GROUP)
            ut_ref[:, pl.ds(row0, SSM_GROUP), :] = (
                blk.reshape(N_GROUPS, SSM_GROUP, ROWS_PER_TILE).astype(ut_ref.dtype))

    proj = jnp.dot(norm_mod(x_ref[...]), w_ref[:, 0:3 * ATT_WIDTH],
                   preferred_element_type=F32)
    q = proj[:, 0:ATT_WIDTH]
    k = proj[:, ATT_WIDTH:2 * ATT_WIDTH]
    v = proj[:, 2 * ATT_WIDTH:3 * ATT_WIDTH]
    qscale = HEAD_DIM ** -0.5 * math.log2(math.e)
    if rope:
        cos, sa, sb = cos_ref[...], sa_ref[...], sb_ref[...]
        for hd in range(N_HEADS):
            sl = slice(hd * HEAD_W, (hd + 1) * HEAD_W)
            q_ref[:, sl] = (_rope(q[:, sl], cos, sa, sb) * qscale).astype(q_ref.dtype)
            k_ref[:, sl] = _rope(k[:, sl], cos, sa, sb).astype(k_ref.dtype)
    else:
        q_ref[...] = (q * qscale).astype(q_ref.dtype)
        k_ref[...] = k.astype(k_ref.dtype)
        seq = SUB_TOK // seqs_per_sub
        k_t = k.T
        for b in range(seqs_per_sub):
            kc_ref[b] = k_t[:, b * seq:(b + 1) * seq]
            for hd in range(N_HEADS):
                vc_ref[b, :, hd, :] = v[b * seq:(b + 1) * seq, hd * HEAD_W:(hd + 1) * HEAD_W]
    v_ref[...] = v.astype(v_ref.dtype)

    for d0 in range(0, t_per_sub, 2):
        ssm_input(d0)


def _in_proj(x, mods, mod_rows, g0, w_in, rope_tabs):
    n_batch, seq_len = x.shape[:2]
    n_tok = n_batch * seq_len
    n_rows = n_tok // CHUNK
    n_tiles = n_tok // TOK_PER_TILE
    mod_row0, n_mod = mod_rows
    tiles_per_mod = n_tiles // n_mod
    rope = rope_tabs is not None
    seqs_per_sub = max(1, SUB_TOK // seq_len)
    in_specs = [pl.BlockSpec((SUB_TOK, D_MODEL), lambda i, j: (i * N_SUB + j, 0)),
                pl.BlockSpec(memory_space=pl.ANY),
                pl.BlockSpec((None, 1, 2 * D_MODEL),
                             lambda i, j: (mod_row0 + i // tiles_per_mod, 0, 0)),
                pl.BlockSpec((1, D_MODEL), lambda i, j: (0, 0)),
                pl.BlockSpec((D_MODEL, 4 * ATT_WIDTH), lambda i, j: (0, 0))]
    args = [x.reshape(n_tok, D_MODEL), x.reshape(n_rows, CHUNK, D_MODEL), mods, g0, w_in]
    row_spec = pl.BlockSpec((SUB_TOK, ATT_WIDTH), lambda i, j: (i * N_SUB + j, 0))
    row_shape = jax.ShapeDtypeStruct((n_tok, ATT_WIDTH), BF16)
    out_specs = [row_spec, row_spec, row_spec]
    out_shape = [row_shape, row_shape, row_shape]
    if rope:
        assert seq_len == TOK_PER_TILE
        for tab in rope_tabs:
            in_specs.append(pl.BlockSpec((SUB_TOK, HEAD_W), lambda i, j: (j, 0)))
            args.append(tab)
    else:
        out_specs += [pl.BlockSpec((seqs_per_sub, ATT_WIDTH, seq_len),
                                   lambda i, j: (i * N_SUB + j, 0, 0)),
                      pl.BlockSpec((seqs_per_sub, seq_len, N_HEADS, HEAD_W),
                                   lambda i, j: (i * N_SUB + j, 0, 0, 0))]
        out_shape += [jax.ShapeDtypeStruct((n_batch, ATT_WIDTH, seq_len), F32),
                      jax.ShapeDtypeStruct((n_batch, seq_len, N_HEADS, HEAD_W), F32)]
    out_specs.append(pl.BlockSpec((N_GROUPS, CW, ROWS_PER_TILE), lambda i, j: (0, 0, i)))
    out_shape.append(jax.ShapeDtypeStruct((N_GROUPS, CW, n_rows), BF16))
    return pl.pallas_call(
        functools.partial(_in_proj_kernel, rope=rope, seqs_per_sub=seqs_per_sub),
        grid=(n_tiles, N_SUB),
        in_specs=in_specs,
        out_specs=out_specs,
        out_shape=out_shape,
        scratch_shapes=[pltpu.VMEM((SSM_WIDTH, D_MODEL), BF16),
                        pltpu.VMEM((CHUNK, ROWS_PER_TILE, D_MODEL), F32),
                        pltpu.SemaphoreType.DMA((CHUNK,))],
        compiler_params=_cparams(2),
        name="in_proj",
    )(*args)


def _attn_kernel(*refs, has_ctx, lam_init, n_seq, seq_len, tq, n_cast):
    if n_cast:
        cast_in = refs[len(refs) - 2 * n_cast - 1:len(refs) - n_cast - 1]
        cast_out = refs[len(refs) - n_cast:]
        refs = refs[:len(refs) - 2 * n_cast - 1] + (refs[len(refs) - n_cast - 1],)
        for src, dst in zip(cast_in, cast_out):
            dst[...] = src[...].astype(dst.dtype)
    if has_ctx:
        lam_ref, sg_ref, q_ref, ck_ref, cv_ref, k_ref, v_ref, o_ref = refs
    else:
        lam_ref, sg_ref, q_ref, k_ref, v_ref, o_ref = refs
    lp = lam_ref[...]
    lam = (jnp.exp(jnp.sum(lp[0:1] * lp[1:2], axis=-1, keepdims=True))
           - jnp.exp(jnp.sum(lp[2:3] * lp[3:4], axis=-1, keepdims=True)) + lam_init)
    first_map = lax.broadcasted_iota(jnp.int32, (1, HEAD_W), 1) < HEAD_DIM
    ti = min(TQ_ITEM, tq)
    for row0 in range(0, n_seq * tq, ti):
        b = row0 // tq
        q_rows = slice(row0, row0 + ti)
        for hd in range(N_HEADS):
            sl = slice(hd * HEAD_W, (hd + 1) * HEAD_W)
            qh = q_ref[q_rows, sl]
            zero = jnp.zeros_like(qh)
            qs = jnp.concatenate([jnp.where(first_map, qh, zero),
                                  jnp.where(first_map, zero, qh)], axis=0)
            kv_rows = slice(b * seq_len, (b + 1) * seq_len)
            parts = [(k_ref[kv_rows, sl], v_ref[kv_rows, sl])]
            scores = [lax.dot_general(qs, parts[0][0], NT_DIMS, preferred_element_type=F32)]
            if has_ctx:
                parts.insert(0, (None, cv_ref[:, hd, :].astype(BF16)))
                scores.insert(0, jnp.dot(qs, ck_ref[sl, :].astype(BF16),
                                         preferred_element_type=F32))
            mx = scores[0].max(axis=-1, keepdims=True)
            for s in scores[1:]:
                mx = jnp.maximum(mx, s.max(axis=-1, keepdims=True))
            acc = None
            for s, (_, vv) in zip(scores, parts):
                e = jnp.exp2(s - mx).astype(BF16)
                v_one = jnp.concatenate([vv, jnp.ones_like(vv)], axis=1)
                pv = jnp.dot(e, v_one, preferred_element_type=F32)
                acc = pv if acc is None else acc + pv
            num = acc[:, 0:HEAD_W] / acc[:, HEAD_W:2 * HEAD_W]
            o = num[0:ti] - lam * num[ti:2 * ti]
            o = _rms(o, sg_ref[...]) * (1.0 - lam_init)
            o_ref[q_rows, sl] = o.astype(o_ref.dtype)


def _attention(q, k, v, ctx_k, ctx_v, lam_params, subln_g, n_batch, seq_len, lam_init,
               cast_weights=()):
    has_ctx = ctx_k is not None
    tq = min(1024, seq_len)
    n_q = seq_len // tq
    n_seq = 1 if n_q > 1 else min(4, n_batch)
    in_specs = [pl.BlockSpec((4, HEAD_DIM), lambda b, i: (0, 0)),
                pl.BlockSpec((1, HEAD_W), lambda b, i: (0, 0)),
                pl.BlockSpec((n_seq * tq, ATT_WIDTH), lambda b, i: (b * n_q + i, 0))]
    args = [lam_params, subln_g.reshape(1, HEAD_W), q]
    if has_ctx:
        past = ctx_v.shape[1]
        in_specs += [pl.BlockSpec((None, ATT_WIDTH, past), lambda b, i: (b, 0, 0)),
                     pl.BlockSpec((None, past, N_HEADS, HEAD_W), lambda b, i: (b, 0, 0, 0))]
        args += [ctx_k, ctx_v]
    kv_spec = pl.BlockSpec((n_seq * seq_len, ATT_WIDTH), lambda b, i: (b, 0))
    in_specs += [kv_spec, kv_spec]
    args += [k, v]
    out_specs = [pl.BlockSpec((n_seq * tq, ATT_WIDTH), lambda b, i: (b * n_q + i, 0))]
    out_shape = [jax.ShapeDtypeStruct((n_batch * seq_len, ATT_WIDTH), BF16)]
    n_steps = (n_batch // n_seq) * n_q
    for w in cast_weights:
        rows = w.shape[0] // n_steps
        spec = pl.BlockSpec((rows, w.shape[1]), lambda b, i: (b * n_q + i, 0))
        in_specs.append(spec)
        args.append(w)
        out_specs.append(spec)
        out_shape.append(jax.ShapeDtypeStruct(w.shape, BF16))
    outs = pl.pallas_call(
        functools.partial(_attn_kernel, has_ctx=has_ctx, lam_init=lam_init,
                          n_seq=n_seq, seq_len=seq_len, tq=tq, n_cast=len(cast_weights)),
        grid=(n_batch // n_seq, n_q),
        in_specs=in_specs,
        out_specs=out_specs,
        out_shape=out_shape,
        compiler_params=_cparams(2),
        name="diff_attention",
    )(*args)
    return outs[0], tuple(outs[1:])


def _cmul(ar, ai, br, bi):
    return ar * br - ai * bi, ar * bi + ai * br


def _ssm_prep_kernel(lre_ref, lim_ref, ls_ref, bre_ref, bim_ref, cre_ref, cim_ref, d_ref,
                     mt_ref, gt_ref, wo_ref, at_ref):
    lane = lax.broadcasted_iota(jnp.int32, (SSM_GROUP, CW), 1)
    chan = lax.broadcasted_iota(jnp.int32, (SSM_GROUP, CW), 0)
    for gi in range(GROUP_BLOCK):
        gt_cols, wo_cols, at_cols, toeplitz = [], [], [], []
        for dr in range(2):
            lr = jnp.minimum(lre_ref[dr, gi], -1e-4)
            li = lim_ref[dr, gi]
            step = jnp.exp(ls_ref[dr, gi])
            mag = jnp.exp(lr * step)
            a_re = mag * jnp.cos(li * step)
            a_im = mag * jnp.sin(li * step)
            den = lr * lr + li * li
            nr = a_re - 1.0
            f_re = (nr * lr + a_im * li) / den
            f_im = (a_im * lr - nr * li) / den
            bt_re, bt_im = bre_ref[dr, gi], bim_ref[dr, gi]
            bb_re, bb_im = _cmul(f_re, f_im, bt_re, bt_im)
            c_re, c_im = cre_ref[dr, gi], cim_ref[dr, gi]
            pw = [(jnp.ones_like(a_re), jnp.zeros_like(a_im))]
            for _ in range(CHUNK):
                pw.append(_cmul(pw[-1][0], pw[-1][1], a_re, a_im))
            g_re, g_im, e_re, e_im = [], [], [], []
            for t in range(CHUNK):
                pr, pi = pw[CHUNK - 1 - t] if dr == 0 else pw[t]
                r, i = _cmul(bb_re, bb_im, pr, pi)
                g_re.append(r)
                g_im.append(i)
                pr, pi = pw[t + 1] if dr == 0 else pw[CHUNK - t]
                r, i = _cmul(c_re, c_im, pr, pi)
                e_re.append(r)
                e_im.append(-i)
            g_cat = jnp.concatenate([jnp.concatenate(g_re, axis=0),
                                     jnp.concatenate(g_im, axis=0)], axis=1)
            gt_cols.append(g_cat)
            wo_cols.append(jnp.concatenate([jnp.concatenate(e_re, axis=0),
                                            jnp.concatenate(e_im, axis=0)], axis=1))
            c_cat = jnp.concatenate([c_re, -c_im], axis=1)
            toeplitz.append(lax.dot_general(c_cat, g_cat, NT_DIMS,
                                            precision=lax.Precision.HIGHEST,
                                            preferred_element_type=F32))
            apw = [pw[CHUNK]]
            for _ in range(SCAN_BLOCK - 1):
                apw.append(_cmul(apw[-1][0], apw[-1][1], apw[0][0], apw[0][1]))
            order = list(range(SCAN_BLOCK)) if dr == 0 else list(range(SCAN_BLOCK - 1, -1, -1))
            order += [2 ** l - 1 for l in range(SCAN_LEVELS)]
            order += [0] * (AT_ROWS - len(order))
            at_cols += [jnp.concatenate([jnp.concatenate([apw[i][0], apw[i][0]], axis=1)
                                         for i in order], axis=0),
                        jnp.concatenate([jnp.concatenate([-apw[i][1], apw[i][1]], axis=1)
                                         for i in order], axis=0)]
        kf_rev, kb = toeplitz
        d_skip = d_ref[gi]
        blocks = []
        for t in range(CHUNK):
            fwd = pltpu.roll(kf_rev, (CW - (CHUNK - 1 - t) * SSM_GROUP) % CW, axis=1)
            bwd = pltpu.roll(kb, t * SSM_GROUP, axis=1)
            blocks.append(jnp.where(lane < (t + 1) * SSM_GROUP, fwd, 0.0)
                          + jnp.where(lane >= t * SSM_GROUP, bwd, 0.0)
                          + jnp.where(lane == chan + t * SSM_GROUP, d_skip, 0.0))
        mt_ref[gi] = jnp.concatenate(blocks, axis=0).astype(mt_ref.dtype)
        gt_ref[gi] = jnp.concatenate(gt_cols, axis=1).astype(gt_ref.dtype)
        wo_ref[gi] = jnp.concatenate(wo_cols, axis=1).astype(wo_ref.dtype)
        at_ref[gi] = jnp.concatenate(at_cols, axis=1)


N_PREP_IN, N_PREP_OUT, N_MOD_IN = 8, 4, 4


def _prep_mod_kernel(*refs):
    prep_in = refs[:N_PREP_IN]
    mod_in = refs[N_PREP_IN:N_PREP_IN + N_MOD_IN]
    w_in_ref = refs[N_PREP_IN + N_MOD_IN]
    outs = refs[N_PREP_IN + N_MOD_IN + 1:]
    _ssm_prep_kernel(*prep_in, *outs[:N_PREP_OUT])
    _mod_kernel(*mod_in, outs[N_PREP_OUT])
    outs[N_PREP_OUT + 1][...] = w_in_ref[...].astype(BF16)


def _ssm_prep_and_modulation(lam_re, lam_im, log_step, b_re, b_im, c_re, c_im, d_skip,
                             c_ctx, c, w_mod, b_mod, w_in):
    row = lambda a: a.reshape(2, N_GROUPS, 1, SSM_STATE)
    bt = lambda a: jnp.swapaxes(a, 2, 3)
    d_row = jnp.tile((d_skip[0] + d_skip[1]).reshape(N_GROUPS, 1, SSM_GROUP), (1, 1, CHUNK))
    gb = GROUP_BLOCK
    n_steps = N_GROUPS // gb
    vec_spec = pl.BlockSpec((2, gb, 1, SSM_STATE), lambda i: (0, i, 0, 0))
    mat_spec = pl.BlockSpec((2, gb, SSM_GROUP, SSM_STATE), lambda i: (0, i, 0, 0))
    w_spec = pl.BlockSpec((gb, CW, CW), lambda i: (i, 0, 0))
    w_shape = jax.ShapeDtypeStruct((N_GROUPS, CW, CW), BF16)
    n_mod = w_mod.shape[1]
    tk = D_MODEL // n_steps
    assert 1 + c.shape[0] <= MOD_ROWS
    outs = pl.pallas_call(
        _prep_mod_kernel,
        grid=(n_steps,),
        in_specs=[vec_spec, vec_spec,
                  pl.BlockSpec((2, gb, 1, 1), lambda i: (0, i, 0, 0)),
                  mat_spec, mat_spec, mat_spec, mat_spec,
                  pl.BlockSpec((gb, 1, CW), lambda i: (i, 0, 0)),
                  pl.BlockSpec((1, tk), lambda k: (0, k)),
                  pl.BlockSpec((c.shape[0], tk), lambda k: (0, k)),
                  pl.BlockSpec((tk, n_mod), lambda k: (k, 0)),
                  pl.BlockSpec((1, n_mod), lambda k: (0, 0)),
                  pl.BlockSpec((tk, w_in.shape[1]), lambda k: (k, 0))],
        out_specs=[w_spec, w_spec, w_spec,
                   pl.BlockSpec((gb, AT_ROWS, 4 * 2 * SSM_STATE), lambda i: (i, 0, 0)),
                   pl.BlockSpec((MOD_ROWS, 1, n_mod), lambda k: (0, 0, 0)),
                   pl.BlockSpec((tk, w_in.shape[1]), lambda k: (k, 0))],
        out_shape=[w_shape, w_shape, w_shape,
                   jax.ShapeDtypeStruct((N_GROUPS, AT_ROWS, 4 * 2 * SSM_STATE), F32),
                   jax.ShapeDtypeStruct((MOD_ROWS, 1, n_mod), F32),
                   jax.ShapeDtypeStruct(w_in.shape, BF16)],
        compiler_params=_cparams(1, fuse_inputs=[True] * 9 + [False, False, True, False]),
        name="ssm_prep_modulation",
    )(row(lam_re), row(lam_im), log_step.reshape(2, N_GROUPS, 1, 1),
      bt(b_re), bt(b_im), c_re, c_im, d_row,
      c_ctx.reshape(1, D_MODEL), c, w_mod, b_mod.reshape(1, n_mod), w_in)
    return outs[:N_PREP_OUT], outs[N_PREP_OUT], outs[N_PREP_OUT + 1]


def _shift_rows(x, m, down):
    n = x.shape[0]
    return pltpu.roll(x, m if down else n - m, axis=0)


def _ssm_kernel(*refs, n_seq, has_h0):
    if has_h0:
        (xt_ref, mt_ref, gt_ref, wo_ref, at_ref, wg_ref, bg_ref, h0_ref,
         z_hbm, zs_ref, zb_ref, zb_sem) = refs
    else:
        (xt_ref, mt_ref, gt_ref, wo_ref, at_ref, wg_ref, bg_ref,
         z_hbm, st_ref, zs_ref, zb_ref, zb_sem, fin_ref) = refs
    step = pl.program_id(0)
    n_rows = xt_ref.shape[-1]
    seg = n_rows // n_seq
    cw2 = 2 * SSM_STATE
    assert seg % SCAN_BLOCK == 0
    n_blk = n_rows // SCAN_BLOCK
    blk_per_seq = seg // SCAN_BLOCK
    pos = lax.broadcasted_iota(jnp.int32, (n_rows, cw2), 0) % seg
    row_blk = lax.broadcasted_iota(jnp.int32, (SCAN_BLOCK, cw2), 0)

    def low_half(shape):
        return lax.broadcasted_iota(jnp.int32, shape, 1) < SSM_STATE

    def swap(v):
        return pltpu.roll(v, SSM_STATE, axis=1)

    def to_planes(va, vb):
        lo = low_half((va.shape[0], cw2))
        va_l = pltpu.roll(va, 3 * SSM_STATE, axis=1)
        vb_r = pltpu.roll(vb, SSM_STATE, axis=1)
        return (jnp.where(lo, va[:, 0:cw2], vb_r[:, 0:cw2]),
                jnp.where(lo, va_l[:, 0:cw2], vb[:, 0:cw2]),
                jnp.where(lo, va[:, cw2:2 * cw2], vb_r[:, cw2:2 * cw2]),
                jnp.where(lo, va_l[:, cw2:2 * cw2], vb[:, cw2:2 * cw2]))

    def from_planes(f_re, f_im, b_re, b_im):
        lo = low_half(f_re.shape)
        va = jnp.concatenate([jnp.where(lo, f_re, swap(f_im)), jnp.where(lo, b_re, swap(b_im))], axis=1)
        vb = jnp.concatenate([jnp.where(lo, swap(f_re), f_im), jnp.where(lo, swap(b_re), b_im)], axis=1)
        return va, vb

    for ga in range(0, GROUP_BLOCK, 2):
        gb = ga + 1
        xts = [xt_ref[ga], xt_ref[gb]]
        s_pair = [lax.dot_general(xts[i], gt_ref[g], TN_DIMS, preferred_element_type=F32)
                  for i, g in enumerate((ga, gb))]
        planes = to_planes(*s_pair)
        lo_t = low_half((AT_ROWS, cw2))
        if has_h0:
            h0_planes = to_planes(h0_ref[ga], h0_ref[gb])
        ent_planes = []
        for dr in range(2):
            down = dr == 0
            pa, pb = (at_ref[g][:, (2 * dr) * cw2:(2 * dr + 1) * cw2] for g in (ga, gb))
            qa, qb = (at_ref[g][:, (2 * dr + 1) * cw2:(2 * dr + 2) * cw2] for g in (ga, gb))
            ar_tab = jnp.where(lo_t, pa, pb)
            ai_tab = jnp.where(lo_t, -qa, qb)
            re3 = planes[2 * dr].reshape(n_blk, SCAN_BLOCK, cw2)
            im3 = planes[2 * dr + 1].reshape(n_blk, SCAN_BLOCK, cw2)
            for lvl in range(SCAN_LEVELS):
                m = 2 ** lvl
                valid = (row_blk >= m) if down else (row_blk < SCAN_BLOCK - m)
                row = SCAN_BLOCK + lvl
                ar = jnp.where(valid, ar_tab[row:row + 1], 0.0)
                ai = jnp.where(valid, ai_tab[row:row + 1], 0.0)
                shift = m if down else SCAN_BLOCK - m
                sh_re = pltpu.roll(re3, shift, axis=1)
                sh_im = pltpu.roll(im3, shift, axis=1)
                re3, im3 = re3 + ar * sh_re - ai * sh_im, im3 + ar * sh_im + ai * sh_re
            ar_blk, ai_blk = ar_tab[0:SCAN_BLOCK], ai_tab[0:SCAN_BLOCK]
            blk_re = [re3[i] for i in range(n_blk)]
            blk_im = [im3[i] for i in range(n_blk)]
            edge = slice(SCAN_BLOCK - 1, SCAN_BLOCK) if down else slice(0, 1)
            for q_i in range(n_seq):
                idxs = list(range(q_i * blk_per_seq, (q_i + 1) * blk_per_seq))
                idxs = idxs if down else idxs[::-1]
                for prev, cur in zip([None] + idxs[:-1], idxs):
                    if prev is not None:
                        c_re, c_im = blk_re[prev][edge], blk_im[prev][edge]
                    elif has_h0:
                        c_re = h0_planes[2 * dr][q_i:q_i + 1]
                        c_im = h0_planes[2 * dr + 1][q_i:q_i + 1]
                    else:
                        continue
                    c_re = jnp.broadcast_to(c_re, (SCAN_BLOCK, cw2))
                    c_im = jnp.broadcast_to(c_im, (SCAN_BLOCK, cw2))
                    blk_re[cur] = blk_re[cur] + ar_blk * c_re - ai_blk * c_im
                    blk_im[cur] = blk_im[cur] + ar_blk * c_im + ai_blk * c_re
            for part, blks in ((0, blk_re), (1, blk_im)):
                s = jnp.concatenate(blks, axis=0)
                if not has_h0:
                    fin_ref[part] = s
                ent = _shift_rows(s, 1, down)
                ent = jnp.where((pos >= 1) if down else (pos < seg - 1), ent, 0.0)
                if has_h0:
                    ent_blk = [ent[i * SCAN_BLOCK:(i + 1) * SCAN_BLOCK] for i in range(n_blk)]
                    at_edge = row_blk == (0 if down else SCAN_BLOCK - 1)
                    for q_i in range(n_seq):
                        bi = q_i * blk_per_seq if down else (q_i + 1) * blk_per_seq - 1
                        ent_blk[bi] = jnp.where(
                            at_edge, h0_planes[2 * dr + part][q_i:q_i + 1], ent_blk[bi])
                    ent = jnp.concatenate(ent_blk, axis=0)
                ent_planes.append(ent)
            if not has_h0:
                rows = pl.ds(seg - 1 if down else 0, n_seq, stride=seg)
                f_re, f_im = fin_ref[0, rows, :], fin_ref[1, rows, :]
                lo_s = low_half((n_seq, cw2))
                st_ref[ga, :, dr * cw2:(dr + 1) * cw2] = jnp.where(lo_s, f_re, swap(f_im))
                st_ref[gb, :, dr * cw2:(dr + 1) * cw2] = jnp.where(lo_s, swap(f_re), f_im)
        h_pair = from_planes(*ent_planes)
        for i, g in enumerate((ga, gb)):
            yt = (jnp.dot(mt_ref[g], xts[i], preferred_element_type=F32)
                  + lax.dot_general(wo_ref[g], h_pair[i].astype(BF16), NT_DIMS,
                                    preferred_element_type=F32))
            z = jax.nn.gelu(yt, approximate=True)
            grp = step * GROUP_BLOCK + g
            for t in range(CHUNK):
                zs_ref[t, pl.ds(pl.multiple_of(grp * SSM_GROUP, SSM_GROUP), SSM_GROUP), :] = (
                    z[t * SSM_GROUP:(t + 1) * SSM_GROUP, :])

    @pl.when(step == pl.num_programs(0) - 1)
    def _():
        def put(t):
            return pltpu.make_async_copy(zb_ref.at[t], z_hbm.at[:, t, :], zb_sem.at[t])

        for t0 in range(0, CHUNK, Z_BATCH):
            for t in range(t0, t0 + Z_BATCH):
                zt = zs_ref[t]
                gate = jnp.dot(wg_ref[...], zt.astype(BF16),
                               preferred_element_type=F32) + bg_ref[...]
                zb_ref[t] = (zt * jax.nn.sigmoid(gate)).T
            for t in range(t0, t0 + Z_BATCH):
                put(t).start(priority=t % N_DMA_THREADS)
        for t in range(CHUNK):
            put(t).wait()


def _ssm(xt, mt, gt, wo, at, w_glu_t, b_glu_col, h0, n_seq):
    n_rows = xt.shape[-1]
    has_h0 = h0 is not None
    gb = GROUP_BLOCK
    w_spec = pl.BlockSpec((gb, CW, CW), lambda i: (i, 0, 0))
    in_specs = [pl.BlockSpec((gb, CW, n_rows), lambda i: (i, 0, 0)),
                w_spec, w_spec, w_spec,
                pl.BlockSpec((gb, AT_ROWS, 4 * 2 * SSM_STATE), lambda i: (i, 0, 0)),
                pl.BlockSpec((SSM_WIDTH, SSM_WIDTH), lambda i: (0, 0)),
                pl.BlockSpec((SSM_WIDTH, 1), lambda i: (0, 0))]
    args = [xt, mt, gt, wo, at, w_glu_t, b_glu_col]
    out_specs = [pl.BlockSpec(memory_space=pl.ANY)]
    out_shape = [jax.ShapeDtypeStruct((n_rows, CHUNK, SSM_WIDTH), F32)]
    scratch = [pltpu.VMEM((CHUNK, SSM_WIDTH, n_rows), F32),
               pltpu.VMEM((CHUNK, n_rows, SSM_WIDTH), F32),
               pltpu.SemaphoreType.DMA((CHUNK,))]
    if has_h0:
        in_specs.append(pl.BlockSpec((gb, n_seq, CW), lambda i: (i, 0, 0)))
        args.append(h0)
    else:
        out_specs.append(pl.BlockSpec((gb, n_seq, CW), lambda i: (i, 0, 0)))
        out_shape.append(jax.ShapeDtypeStruct((N_GROUPS, n_seq, CW), F32))
        scratch.append(pltpu.VMEM((2, n_rows, 2 * SSM_STATE), F32))
    return pl.pallas_call(
        functools.partial(_ssm_kernel, n_seq=n_seq, has_h0=has_h0),
        grid=(N_GROUPS // gb,),
        in_specs=in_specs,
        out_specs=out_specs,
        out_shape=out_shape,
        scratch_shapes=scratch,
        compiler_params=_cparams(1),
        name="ssm_scan_glu",
    )(*args)


FF_CHUNK = 256
FFN_TOK = 1024


def _out_ffn_kernel(x_ref, attn_ref, z_ref, mod_ref, g_ref, wo_hbm, wfi_hbm, wfo_hbm,
                    o_ref, act_ref, wo_ref, wfi_ref, wfo_ref, w_sem):
    first = pl.program_id(0) == 0
    fetch = [pltpu.make_async_copy(src, dst, w_sem.at[i]) for i, (src, dst) in
             enumerate(((wo_hbm, wo_ref), (wfi_hbm, wfi_ref), (wfo_hbm, wfo_ref)))]

    @pl.when(first)
    def _():
        for i, cp in enumerate(fetch):
            cp.start(priority=min(i, N_DMA_THREADS - 1))
        fetch[0].wait()

    gate1 = mod_ref[:, 2 * D_MODEL:3 * D_MODEL]
    shift2 = mod_ref[:, 3 * D_MODEL:4 * D_MODEL]
    scale2 = mod_ref[:, 4 * D_MODEL:5 * D_MODEL]
    gate2 = mod_ref[:, 5 * D_MODEL:6 * D_MODEL]
    half = FFN_TOK // 2
    gain2 = g_ref[2:3, :] * (1.0 + scale2)

    def pre_ffn(hf):
        r = slice(hf * half, (hf + 1) * half)
        mixer = jnp.concatenate([attn_ref[r, :], z_ref[r, :].astype(BF16)], axis=1)
        mix = jnp.dot(mixer, wo_ref[...], preferred_element_type=F32)
        x1 = x_ref[r, :] + gate1 * _rms(mix, g_ref[1:2, :])
        ms = jnp.mean(x1 * x1, axis=-1, keepdims=True)
        return x1, (x1 * lax.rsqrt(ms + NORM_EPS) * gain2 + shift2).astype(BF16)

    def ffn_in(hf, h, chunks):
        r = slice(hf * half, (hf + 1) * half)
        for c in chunks:
            lo = c * FF_CHUNK
            gt = jnp.dot(h, wfi_ref[:, lo:lo + FF_CHUNK], preferred_element_type=F32)
            up = jnp.dot(h, wfi_ref[:, D_FF + lo:D_FF + lo + FF_CHUNK],
                         preferred_element_type=F32)
            act_ref[r, lo:lo + FF_CHUNK] = (_silu(gt) * up).astype(BF16)

    def ffn_out(hf, x1):
        r = slice(hf * half, (hf + 1) * half)
        f = jnp.dot(act_ref[r, :], wfo_ref[...], preferred_element_type=F32)
        o_ref[r, :] = x1 + gate2 * _rms(f, g_ref[3:4, :])

    n_chunks = D_FF // FF_CHUNK
    x1_a, h_a = pre_ffn(0)
    x1_b, h_b = pre_ffn(1)
    @pl.when(first)
    def _():
        fetch[1].wait()
        fetch[2].wait()

    ffn_in(0, h_a, range(n_chunks))
    ffn_out(0, x1_a)
    ffn_in(1, h_b, range(n_chunks))
    ffn_out(1, x1_b)


def _out_ffn(x2d, attn, z, mods, mod_rows, norm_g, w_o, w_ffn_in, w_ffn_out):
    n_tok = x2d.shape[0]
    z2d = z.reshape(n_tok, SSM_WIDTH)
    n_steps = n_tok // FFN_TOK
    mod_row0, n_mod = mod_rows
    steps_per_mod = n_steps // n_mod
    const = lambda i: (0, 0)
    row_spec = lambda w: pl.BlockSpec((FFN_TOK, w), lambda i: (i, 0))
    return pl.pallas_call(
        _out_ffn_kernel,
        grid=(n_steps,),
        in_specs=[row_spec(D_MODEL), row_spec(ATT_WIDTH), row_spec(SSM_WIDTH),
                  pl.BlockSpec((None, 1, N_MOD * D_MODEL),
                               lambda i: (mod_row0 + i // steps_per_mod, 0, 0)),
                  pl.BlockSpec((4, D_MODEL), const),
                  pl.BlockSpec(memory_space=pl.ANY),
                  pl.BlockSpec(memory_space=pl.ANY),
                  pl.BlockSpec(memory_space=pl.ANY)],
        out_specs=row_spec(D_MODEL),
        out_shape=jax.ShapeDtypeStruct((n_tok, D_MODEL), F32),
        scratch_shapes=[pltpu.VMEM((FFN_TOK, D_FF), BF16),
                        pltpu.VMEM((2 * ATT_WIDTH, D_MODEL), BF16),
                        pltpu.VMEM((D_MODEL, 2 * D_FF), BF16),
                        pltpu.VMEM((D_FF, D_MODEL), BF16),
                        pltpu.SemaphoreType.DMA((3,))],
        compiler_params=_cparams(1),
        name="out_proj_ffn",
    )(x2d, attn, z2d, mods, norm_g, w_o, w_ffn_in, w_ffn_out)


def _rope_tables(seq_len):
    t = np.arange(seq_len)
    row = (t // GRID_W).astype(np.float32)
    col = (t % GRID_W).astype(np.float32)
    half = HEAD_DIM // 2
    inv_freq = (np.float32(ROPE_BASE)
                ** (-np.arange(0, half, 2, dtype=np.float32) / np.float32(half))).astype(np.float32)
    ang_r = row[:, None] * inv_freq
    ang_c = col[:, None] * inv_freq
    ang = np.concatenate([ang_r, ang_r, ang_c, ang_c], axis=-1)
    cos, sin = np.cos(ang), np.sin(ang)
    upper = (np.arange(HEAD_DIM) % 32) < 16
    sa = np.where(upper, -sin, 0.0)
    sb = np.where(upper, 0.0, sin)
    two = lambda a: jnp.asarray(np.concatenate([a, a], axis=-1), dtype=F32)
    return two(cos), two(sa), two(sb)


def _layer(x, mods, mod_rows, lam_init, rope_tabs, ctx_k, ctx_v, h0, weights, prep):
    n_batch, seq_len = x.shape[:2]
    g = weights['norm_g']
    outs = _in_proj(x, mods, mod_rows, g[0:1], weights['w_in'], rope_tabs)
    q, k, v = outs[:3]
    pending = () if 'late_bf16' in weights else weights['late_f32']
    attn, cast = _attention(q, k, v, ctx_k, ctx_v, weights['lam'], weights['subln_g'],
                            n_batch, seq_len, lam_init, cast_weights=pending)
    if pending:
        weights['late_bf16'] = cast
    ssm_out = _ssm(outs[-1], *prep, weights['w_glu_t'], weights['b_glu_col'], h0, n_batch)
    y = _out_ffn(x.reshape(n_batch * seq_len, D_MODEL), attn, ssm_out[0], mods, mod_rows, g,
                 *weights['late_bf16'])
    return y.reshape(x.shape), outs[3:-1], ssm_out[1:]


def kernel(x_prompt, x_sample, cache_k, cache_v, state_ssm_re, state_ssm_im, c, c_ctx, w_mod, b_mod, norm_g, w_in, lam_params, subln_g, ssm_lambda_re, ssm_lambda_im, ssm_log_step, ssm_b_re, ssm_b_im, ssm_c_re, ssm_c_im, ssm_d, w_glu, b_glu, w_o, w_ffn_in, w_ffn_out):
    depth = w_mod.shape[0]
    assert depth == 1
    bp = x_prompt.shape[0]
    bd, ld_len = x_sample.shape[:2]
    past = cache_k.shape[2]
    xp, xs = x_prompt, x_sample
    rope_tabs = _rope_tables(ld_len)
    ks_out, vs_out, hr_out, hi_out = [], [], [], []
    for l in range(depth):
        lam_init = 0.8 - 0.6 * math.exp(-0.3 * l)
        prep, mods, w_in_bf16 = _ssm_prep_and_modulation(
            ssm_lambda_re[l], ssm_lambda_im[l], ssm_log_step[l], ssm_b_re[l], ssm_b_im[l],
            ssm_c_re[l], ssm_c_im[l], ssm_d[l], c_ctx, c, w_mod[l], b_mod[l], w_in[l])
        weights = {
            'norm_g': norm_g[l],
            'w_in': w_in_bf16,
            'lam': lam_params[l], 'subln_g': subln_g[l],
            'w_glu_t': w_glu[l].T.astype(BF16), 'b_glu_col': b_glu[l].reshape(SSM_WIDTH, 1),
            'late_f32': (w_o[l], w_ffn_in[l], w_ffn_out[l]),
        }
        ck = jnp.transpose(cache_k[:, l], (0, 2, 3, 1)).reshape(bd, ATT_WIDTH, past)
        cv = cache_v[:, l]
        h0 = jnp.stack([state_ssm_re[:, l], state_ssm_im[:, l]], axis=2)
        h0 = h0.transpose(3, 0, 1, 2, 4).reshape(N_GROUPS, bd, CW)
        xs, _, _ = _layer(xs, mods, (1, bd), lam_init, rope_tabs, ck, cv, h0, weights, prep)
        xp, (k_ctx, v_ctx), (st,) = _layer(xp, mods, (0, 1), lam_init, None, None, None, None,
                                           weights, prep)
        ks_out.append(jnp.swapaxes(k_ctx, 1, 2).reshape(bp, -1, 2 * N_HEADS, HEAD_DIM))
        vs_out.append(v_ctx)
        fin = st.reshape(N_GROUPS, bp, 2, 2, SSM_STATE).transpose(1, 2, 3, 0, 4)
        hr_out.append(fin[:, :, 0])
        hi_out.append(fin[:, :, 1])
    return (xp, xs, jnp.stack(ks_out, axis=1), jnp.stack(vs_out, axis=1),
            jnp.stack(hr_out, axis=1), jnp.stack(hi_out, axis=1))
```

```python
import functools
import math

import jax
import jax.numpy as jnp
import numpy as np
from jax import lax
from jax.experimental import pallas as pl
from jax.experimental.pallas import tpu as pltpu

F32 = jnp.float32
BF16 = jnp.bfloat16

D_MODEL = 1024
GRID_W = 64
ATT_WIDTH = 512
SSM_WIDTH = 512
HEAD_DIM = 64
N_HEADS = 4
HEAD_W = 2 * HEAD_DIM
SSM_GROUP = 16
N_GROUPS = 32
SSM_STATE = 64
D_FF = 2816
N_MOD = 6
ROPE_BASE = 10000.0
NORM_EPS = 1e-6

CHUNK = 16
CW = CHUNK * SSM_GROUP
SCAN_BLOCK = 8
SCAN_LEVELS = 3
AT_ROWS = 16
GROUP_BLOCK = 4
Z_BATCH = 8

ROWS_PER_TILE = 128
TOK_PER_TILE = ROWS_PER_TILE * CHUNK
SUB_TOK = 1024
TQ_ITEM = 256
N_SUB = TOK_PER_TILE // SUB_TOK

VMEM_LIMIT = 56 * 1024 * 1024
N_DMA_THREADS = 2

NT_DIMS = (((1,), (1,)), ((), ()))
TN_DIMS = (((0,), (0,)), ((), ()))


def _cparams(n_axes, fuse_inputs=None):
    return pltpu.CompilerParams(
        dimension_semantics=("arbitrary",) * n_axes,
        allow_input_fusion=fuse_inputs,
        vmem_limit_bytes=VMEM_LIMIT)


def _rms(x, g):
    ms = jnp.mean(x * x, axis=-1, keepdims=True)
    return x * lax.rsqrt(ms + NORM_EPS) * g


def _silu(x):
    return x * jax.nn.sigmoid(x)


MOD_ROWS = 8


def _mod_kernel(ctx_ref, c_ref, w_ref, b_ref, o_ref):
    n_lat, tk = c_ref.shape
    row = lax.broadcasted_iota(jnp.int32, (MOD_ROWS, tk), 0)
    cond = jnp.where(row == 0, ctx_ref[...], 0.0)
    for b in range(n_lat):
        cond = jnp.where(row == 1 + b, c_ref[b:b + 1, :], cond)
    part = jnp.dot(_silu(cond).astype(BF16), w_ref[...].astype(BF16),
                   preferred_element_type=F32)

    @pl.when(pl.program_id(0) == 0)
    def _():
        o_ref[:, 0, :] = part + b_ref[...]

    @pl.when(pl.program_id(0) > 0)
    def _():
        o_ref[:, 0, :] += part


def _rope(x, cos, sa, sb):
    return (x * cos + pltpu.roll(x, HEAD_W - 16, axis=1) * sa
            + pltpu.roll(x, 16, axis=1) * sb)


def _in_proj_kernel(*refs, rope, seqs_per_sub):
    x_ref, x3_hbm, mod_ref, g_ref, w_ref = refs[:5]
    refs = refs[5:]
    if rope:
        cos_ref, sa_ref, sb_ref = refs[:3]
        refs = refs[3:]
        q_ref, k_ref, v_ref, ut_ref, wut_ref, xt_ref, xt_sem = refs
    else:
        q_ref, k_ref, v_ref, kc_ref, vc_ref, ut_ref, wut_ref, xt_ref, xt_sem = refs
    tile = pl.program_id(0)
    j = pl.program_id(1)
    t_per_sub = CHUNK // N_SUB
    t_early = CHUNK - t_per_sub

    def gather(tile_idx, t):
        src = x3_hbm.at[pl.ds(tile_idx * ROWS_PER_TILE, ROWS_PER_TILE), t, :]
        return pltpu.make_async_copy(src, xt_ref.at[t], xt_sem.at[t])

    @pl.when(j == 0)
    def _():
        @pl.when(tile == 0)
        def _():
            for t in range(t_early):
                gather(0, t).start(priority=t % N_DMA_THREADS)
        for t in range(t_early, CHUNK):
            gather(tile, t).start(priority=t % N_DMA_THREADS)

    t_base = j * t_per_sub
    for d in range(t_per_sub):
        gather(tile, t_base + d).wait()

    @pl.when((j == N_SUB - 1) & (tile + 1 < pl.num_programs(0)))
    def _():
        for t in range(t_early):
            gather(tile + 1, t).start(priority=t % N_DMA_THREADS)

    @pl.when((tile == 0) & (j == 0))
    def _():
        wut_ref[...] = w_ref[:, 3 * ATT_WIDTH:].T

    shift = mod_ref[:, 0:D_MODEL]
    gain = g_ref[...] * (1.0 + mod_ref[:, D_MODEL:2 * D_MODEL])

    def norm_mod(xv):
        ms = jnp.mean(xv * xv, axis=-1, keepdims=True)
        return (xv * lax.rsqrt(ms + NORM_EPS) * gain + shift).astype(BF16)

    def ssm_input(d0):
        xt = jnp.concatenate([xt_ref[t_base + d0], xt_ref[t_base + d0 + 1]], axis=0)
        ut = lax.dot_general(wut_ref[...], norm_mod(xt), NT_DIMS,
                             preferred_element_type=F32)
        for d in range(2):
            blk = ut[:, d * ROWS_PER_TILE:(d + 1) * ROWS_PER_TILE]
            row0 = pl.multiple_of((t_base + d0 + d) * SSM_GROUP, SSM_GROUP)
            ut_ref[:, pl.ds(row0, SSM_GROUP), :] = (
                blk.reshape(N_GROUPS, SSM_GROUP, ROWS_PER_TILE).astype(ut_ref.dtype))

    proj = jnp.dot(norm_mod(x_ref[...]), w_ref[:, 0:3 * ATT_WIDTH],
                   preferred_element_type=F32)
    q = proj[:, 0:ATT_WIDTH]
    k = proj[:, ATT_WIDTH:2 * ATT_WIDTH]
    v = proj[:, 2 * ATT_WIDTH:3 * ATT_WIDTH]
    qscale = HEAD_DIM ** -0.5 * math.log2(math.e)
    if rope:
        cos, sa, sb = cos_ref[...], sa_ref[...], sb_ref[...]
        for hd in range(N_HEADS):
            sl = slice(hd * HEAD_W, (hd + 1) * HEAD_W)
            q_ref[:, sl] = (_rope(q[:, sl], cos, sa, sb) * qscale).astype(q_ref.dtype)
            k_ref[:, sl] = _rope(k[:, sl], cos, sa, sb).astype(k_ref.dtype)
    else:
        q_ref[...] = (q * qscale).astype(q_ref.dtype)
        k_ref[...] = k.astype(k_ref.dtype)
        seq = SUB_TOK // seqs_per_sub
        k_t = k.T
        for b in range(seqs_per_sub):
            kc_ref[b] = k_t[:, b * seq:(b + 1) * seq]
            for hd in range(N_HEADS):
                vc_ref[b, :, hd, :] = v[b * seq:(b + 1) * seq, hd * HEAD_W:(hd + 1) * HEAD_W]
    v_ref[...] = v.astype(v_ref.dtype)

    for d0 in range(0, t_per_sub, 2):
        ssm_input(d0)


def _in_proj(x, mods, mod_rows, g0, w_in, rope_tabs):
    n_batch, seq_len = x.shape[:2]
    n_tok = n_batch * seq_len
    n_rows = n_tok // CHUNK
    n_tiles = n_tok // TOK_PER_TILE
    mod_row0, n_mod = mod_rows
    tiles_per_mod = n_tiles // n_mod
    rope = rope_tabs is not None
    seqs_per_sub = max(1, SUB_TOK // seq_len)
    in_specs = [pl.BlockSpec((SUB_TOK, D_MODEL), lambda i, j: (i * N_SUB + j, 0)),
                pl.BlockSpec(memory_space=pl.ANY),
                pl.BlockSpec((None, 1, 2 * D_MODEL),
                             lambda i, j: (mod_row0 + i // tiles_per_mod, 0, 0)),
                pl.BlockSpec((1, D_MODEL), lambda i, j: (0, 0)),
                pl.BlockSpec((D_MODEL, 4 * ATT_WIDTH), lambda i, j: (0, 0))]
    args = [x.reshape(n_tok, D_MODEL), x.reshape(n_rows, CHUNK, D_MODEL), mods, g0, w_in]
    row_spec = pl.BlockSpec((SUB_TOK, ATT_WIDTH), lambda i, j: (i * N_SUB + j, 0))
    row_shape = jax.ShapeDtypeStruct((n_tok, ATT_WIDTH), BF16)
    out_specs = [row_spec, row_spec, row_spec]
    out_shape = [row_shape, row_shape, row_shape]
    if rope:
        assert seq_len == TOK_PER_TILE
        for tab in rope_tabs:
            in_specs.append(pl.BlockSpec((SUB_TOK, HEAD_W), lambda i, j: (j, 0)))
            args.append(tab)
    else:
        out_specs += [pl.BlockSpec((seqs_per_sub, ATT_WIDTH, seq_len),
                                   lambda i, j: (i * N_SUB + j, 0, 0)),
                      pl.BlockSpec((seqs_per_sub, seq_len, N_HEADS, HEAD_W),
                                   lambda i, j: (i * N_SUB + j, 0, 0, 0))]
        out_shape += [jax.ShapeDtypeStruct((n_batch, ATT_WIDTH, seq_len), F32),
                      jax.ShapeDtypeStruct((n_batch, seq_len, N_HEADS, HEAD_W), F32)]
    out_specs.append(pl.BlockSpec((N_GROUPS, CW, ROWS_PER_TILE), lambda i, j: (0, 0, i)))
    out_shape.append(jax.ShapeDtypeStruct((N_GROUPS, CW, n_rows), BF16))
    return pl.pallas_call(
        functools.partial(_in_proj_kernel, rope=rope, seqs_per_sub=seqs_per_sub),
        grid=(n_tiles, N_SUB),
        in_specs=in_specs,
        out_specs=out_specs,
        out_shape=out_shape,
        scratch_shapes=[pltpu.VMEM((SSM_WIDTH, D_MODEL), BF16),
                        pltpu.VMEM((CHUNK, ROWS_PER_TILE, D_MODEL), F32),
                        pltpu.SemaphoreType.DMA((CHUNK,))],
        compiler_params=_cparams(2),
        name="in_proj",
    )(*args)


def _attn_kernel(*refs, has_ctx, lam_init, n_seq, seq_len, tq, n_cast):
    if n_cast:
        cast_in = refs[len(refs) - 2 * n_cast - 1:len(refs) - n_cast - 1]
        cast_out = refs[len(refs) - n_cast:]
        refs = refs[:len(refs) - 2 * n_cast - 1] + (refs[len(refs) - n_cast - 1],)
        for src, dst in zip(cast_in, cast_out):
            dst[...] = src[...].astype(dst.dtype)
    if has_ctx:
        lam_ref, sg_ref, q_ref, ck_ref, cv_ref, k_ref, v_ref, o_ref = refs
    else:
        lam_ref, sg_ref, q_ref, k_ref, v_ref, o_ref = refs
    lp = lam_ref[...]
    lam = (jnp.exp(jnp.sum(lp[0:1] * lp[1:2], axis=-1, keepdims=True))
           - jnp.exp(jnp.sum(lp[2:3] * lp[3:4], axis=-1, keepdims=True)) + lam_init)
    first_map = lax.broadcasted_iota(jnp.int32, (1, HEAD_W), 1) < HEAD_DIM
    ti = min(TQ_ITEM, tq)
    for row0 in range(0, n_seq * tq, ti):
        b = row0 // tq
        q_rows = slice(row0, row0 + ti)
        for hd in range(N_HEADS):
            sl = slice(hd * HEAD_W, (hd + 1) * HEAD_W)
            qh = q_ref[q_rows, sl]
            zero = jnp.zeros_like(qh)
            qs = jnp.concatenate([jnp.where(first_map, qh, zero),
                                  jnp.where(first_map, zero, qh)], axis=0)
            kv_rows = slice(b * seq_len, (b + 1) * seq_len)
            parts = [(k_ref[kv_rows, sl], v_ref[kv_rows, sl])]
            scores = [lax.dot_general(qs, parts[0][0], NT_DIMS, preferred_element_type=F32)]
            if has_ctx:
                parts.insert(0, (None, cv_ref[:, hd, :].astype(BF16)))
                scores.insert(0, jnp.dot(qs, ck_ref[sl, :].astype(BF16),
                                         preferred_element_type=F32))
            mx = scores[0].max(axis=-1, keepdims=True)
            for s in scores[1:]:
                mx = jnp.maximum(mx, s.max(axis=-1, keepdims=True))
            acc = None
            for s, (_, vv) in zip(scores, parts):
                e = jnp.exp2(s - mx).astype(BF16)
                v_one = jnp.concatenate([vv, jnp.ones_like(vv)], axis=1)
                pv = jnp.dot(e, v_one, preferred_element_type=F32)
                acc = pv if acc is None else acc + pv
            num = acc[:, 0:HEAD_W] / acc[:, HEAD_W:2 * HEAD_W]
            o = num[0:ti] - lam * num[ti:2 * ti]
            o = _rms(o, sg_ref[...]) * (1.0 - lam_init)
            o_ref[q_rows, sl] = o.astype(o_ref.dtype)


def _attention(q, k, v, ctx_k, ctx_v, lam_params, subln_g, n_batch, seq_len, lam_init,
               cast_weights=()):
    has_ctx = ctx_k is not None
    tq = min(1024, seq_len)
    n_q = seq_len // tq
    n_seq = 1 if n_q > 1 else min(4, n_batch)
    in_specs = [pl.BlockSpec((4, HEAD_DIM), lambda b, i: (0, 0)),
                pl.BlockSpec((1, HEAD_W), lambda b, i: (0, 0)),
                pl.BlockSpec((n_seq * tq, ATT_WIDTH), lambda b, i: (b * n_q + i, 0))]
    args = [lam_params, subln_g.reshape(1, HEAD_W), q]
    if has_ctx:
        past = ctx_v.shape[1]
        in_specs += [pl.BlockSpec((None, ATT_WIDTH, past), lambda b, i: (b, 0, 0)),
                     pl.BlockSpec((None, past, N_HEADS, HEAD_W), lambda b, i: (b, 0, 0, 0))]
        args += [ctx_k, ctx_v]
    kv_spec = pl.BlockSpec((n_seq * seq_len, ATT_WIDTH), lambda b, i: (b, 0))
    in_specs += [kv_spec, kv_spec]
    args += [k, v]
    out_specs = [pl.BlockSpec((n_seq * tq, ATT_WIDTH), lambda b, i: (b * n_q + i, 0))]
    out_shape = [jax.ShapeDtypeStruct((n_batch * seq_len, ATT_WIDTH), BF16)]
    n_steps = (n_batch // n_seq) * n_q
    for w in cast_weights:
        rows = w.shape[0] // n_steps
        spec = pl.BlockSpec((rows, w.shape[1]), lambda b, i: (b * n_q + i, 0))
        in_specs.append(spec)
        args.append(w)
        out_specs.append(spec)
        out_shape.append(jax.ShapeDtypeStruct(w.shape, BF16))
    outs = pl.pallas_call(
        functools.partial(_attn_kernel, has_ctx=has_ctx, lam_init=lam_init,
                          n_seq=n_seq, seq_len=seq_len, tq=tq, n_cast=len(cast_weights)),
        grid=(n_batch // n_seq, n_q),
        in_specs=in_specs,
        out_specs=out_specs,
        out_shape=out_shape,
        compiler_params=_cparams(2, fuse_inputs=[True, True] + [False] * (len(args) - 2)),
        name="diff_attention",
    )(*args)
    return outs[0], tuple(outs[1:])


def _cmul(ar, ai, br, bi):
    return ar * br - ai * bi, ar * bi + ai * br


def _ssm_prep_kernel(lre_ref, lim_ref, ls_ref, bre_ref, bim_ref, cre_ref, cim_ref, d_ref,
                     mt_ref, gt_ref, wo_ref, at_ref):
    lane = lax.broadcasted_iota(jnp.int32, (SSM_GROUP, CW), 1)
    chan = lax.broadcasted_iota(jnp.int32, (SSM_GROUP, CW), 0)
    for gi in range(GROUP_BLOCK):
        gt_cols, wo_cols, at_cols, toeplitz = [], [], [], []
        for dr in range(2):
            lr = jnp.minimum(lre_ref[dr, gi], -1e-4)
            li = lim_ref[dr, gi]
            step = jnp.exp(ls_ref[dr, gi])
            mag = jnp.exp(lr * step)
            a_re = mag * jnp.cos(li * step)
            a_im = mag * jnp.sin(li * step)
            den = lr * lr + li * li
            nr = a_re - 1.0
            f_re = (nr * lr + a_im * li) / den
            f_im = (a_im * lr - nr * li) / den
            bt_re, bt_im = bre_ref[dr, gi], bim_ref[dr, gi]
            bb_re, bb_im = _cmul(f_re, f_im, bt_re, bt_im)
            c_re, c_im = cre_ref[dr, gi], cim_ref[dr, gi]
            pw = [(jnp.ones_like(a_re), jnp.zeros_like(a_im))]
            for _ in range(CHUNK):
                pw.append(_cmul(pw[-1][0], pw[-1][1], a_re, a_im))
            g_re, g_im, e_re, e_im = [], [], [], []
            for t in range(CHUNK):
                pr, pi = pw[CHUNK - 1 - t] if dr == 0 else pw[t]
                r, i = _cmul(bb_re, bb_im, pr, pi)
                g_re.append(r)
                g_im.append(i)
                pr, pi = pw[t + 1] if dr == 0 else pw[CHUNK - t]
                r, i = _cmul(c_re, c_im, pr, pi)
                e_re.append(r)
                e_im.append(-i)
            g_cat = jnp.concatenate([jnp.concatenate(g_re, axis=0),
                                     jnp.concatenate(g_im, axis=0)], axis=1)
            gt_cols.append(g_cat)
            wo_cols.append(jnp.concatenate([jnp.concatenate(e_re, axis=0),
                                            jnp.concatenate(e_im, axis=0)], axis=1))
            c_cat = jnp.concatenate([c_re, -c_im], axis=1)
            toeplitz.append(lax.dot_general(c_cat, g_cat, NT_DIMS,
                                            precision=lax.Precision.HIGHEST,
                                            preferred_element_type=F32))
            apw = [pw[CHUNK]]
            for _ in range(SCAN_BLOCK - 1):
                apw.append(_cmul(apw[-1][0], apw[-1][1], apw[0][0], apw[0][1]))
            order = list(range(SCAN_BLOCK)) if dr == 0 else list(range(SCAN_BLOCK - 1, -1, -1))
            order += [2 ** l - 1 for l in range(SCAN_LEVELS)]
            order += [0] * (AT_ROWS - len(order))
            at_cols += [jnp.concatenate([jnp.concatenate([apw[i][0], apw[i][0]], axis=1)
                                         for i in order], axis=0),
                        jnp.concatenate([jnp.concatenate([-apw[i][1], apw[i][1]], axis=1)
                                         for i in order], axis=0)]
        kf_rev, kb = toeplitz
        d_skip = d_ref[gi]
        blocks = []
        for t in range(CHUNK):
            fwd = pltpu.roll(kf_rev, (CW - (CHUNK - 1 - t) * SSM_GROUP) % CW, axis=1)
            bwd = pltpu.roll(kb, t * SSM_GROUP, axis=1)
            blocks.append(jnp.where(lane < (t + 1) * SSM_GROUP, fwd, 0.0)
                          + jnp.where(lane >= t * SSM_GROUP, bwd, 0.0)
                          + jnp.where(lane == chan + t * SSM_GROUP, d_skip, 0.0))
        mt_ref[gi] = jnp.concatenate(blocks, axis=0).astype(mt_ref.dtype)
        gt_ref[gi] = jnp.concatenate(gt_cols, axis=1).astype(gt_ref.dtype)
        wo_ref[gi] = jnp.concatenate(wo_cols, axis=1).astype(wo_ref.dtype)
        at_ref[gi] = jnp.concatenate(at_cols, axis=1)


N_PREP_IN, N_PREP_OUT, N_MOD_IN = 8, 4, 4


def _prep_mod_kernel(*refs):
    prep_in = refs[:N_PREP_IN]
    mod_in = refs[N_PREP_IN:N_PREP_IN + N_MOD_IN]
    w_in_ref = refs[N_PREP_IN + N_MOD_IN]
    outs = refs[N_PREP_IN + N_MOD_IN + 1:]
    _ssm_prep_kernel(*prep_in, *outs[:N_PREP_OUT])
    _mod_kernel(*mod_in, outs[N_PREP_OUT])
    outs[N_PREP_OUT + 1][...] = w_in_ref[...].astype(BF16)


def _ssm_prep_and_modulation(lam_re, lam_im, log_step, b_re, b_im, c_re, c_im, d_skip,
                             c_ctx, c, w_mod, b_mod, w_in):
    row = lambda a: a.reshape(2, N_GROUPS, 1, SSM_STATE)
    bt = lambda a: jnp.swapaxes(a, 2, 3)
    d_row = jnp.tile((d_skip[0] + d_skip[1]).reshape(N_GROUPS, 1, SSM_GROUP), (1, 1, CHUNK))
    gb = GROUP_BLOCK
    n_steps = N_GROUPS // gb
    vec_spec = pl.BlockSpec((2, gb, 1, SSM_STATE), lambda i: (0, i, 0, 0))
    mat_spec = pl.BlockSpec((2, gb, SSM_GROUP, SSM_STATE), lambda i: (0, i, 0, 0))
    w_spec = pl.BlockSpec((gb, CW, CW), lambda i: (i, 0, 0))
    w_shape = jax.ShapeDtypeStruct((N_GROUPS, CW, CW), BF16)
    n_mod = w_mod.shape[1]
    tk = D_MODEL // n_steps
    assert 1 + c.shape[0] <= MOD_ROWS
    outs = pl.pallas_call(
        _prep_mod_kernel,
        grid=(n_steps,),
        in_specs=[vec_spec, vec_spec,
                  pl.BlockSpec((2, gb, 1, 1), lambda i: (0, i, 0, 0)),
                  mat_spec, mat_spec, mat_spec, mat_spec,
                  pl.BlockSpec((gb, 1, CW), lambda i: (i, 0, 0)),
                  pl.BlockSpec((1, tk), lambda k: (0, k)),
                  pl.BlockSpec((c.shape[0], tk), lambda k: (0, k)),
                  pl.BlockSpec((tk, n_mod), lambda k: (k, 0)),
                  pl.BlockSpec((1, n_mod), lambda k: (0, 0)),
                  pl.BlockSpec((tk, w_in.shape[1]), lambda k: (k, 0))],
        out_specs=[w_spec, w_spec, w_spec,
                   pl.BlockSpec((gb, AT_ROWS, 4 * 2 * SSM_STATE), lambda i: (i, 0, 0)),
                   pl.BlockSpec((MOD_ROWS, 1, n_mod), lambda k: (0, 0, 0)),
                   pl.BlockSpec((tk, w_in.shape[1]), lambda k: (k, 0))],
        out_shape=[w_shape, w_shape, w_shape,
                   jax.ShapeDtypeStruct((N_GROUPS, AT_ROWS, 4 * 2 * SSM_STATE), F32),
                   jax.ShapeDtypeStruct((MOD_ROWS, 1, n_mod), F32),
                   jax.ShapeDtypeStruct(w_in.shape, BF16)],
        compiler_params=_cparams(1, fuse_inputs=[True] * 9 + [False, False, True, False]),
        name="ssm_prep_modulation",
    )(row(lam_re), row(lam_im), log_step.reshape(2, N_GROUPS, 1, 1),
      bt(b_re), bt(b_im), c_re, c_im, d_row,
      c_ctx.reshape(1, D_MODEL), c, w_mod, b_mod.reshape(1, n_mod), w_in)
    return outs[:N_PREP_OUT], outs[N_PREP_OUT], outs[N_PREP_OUT + 1]


def _shift_rows(x, m, down):
    n = x.shape[0]
    return pltpu.roll(x, m if down else n - m, axis=0)


def _ssm_kernel(*refs, n_seq, has_h0):
    if has_h0:
        (xt_ref, mt_ref, gt_ref, wo_ref, at_ref, wg_ref, bg_ref, h0_ref,
         z_hbm, zs_ref, zb_ref, zb_sem) = refs
    else:
        (xt_ref, mt_ref, gt_ref, wo_ref, at_ref, wg_ref, bg_ref,
         z_hbm, st_ref, zs_ref, zb_ref, zb_sem, fin_ref) = refs
    step = pl.program_id(0)
    n_rows = xt_ref.shape[-1]
    seg = n_rows // n_seq
    cw2 = 2 * SSM_STATE
    assert seg % SCAN_BLOCK == 0
    n_blk = n_rows // SCAN_BLOCK
    blk_per_seq = seg // SCAN_BLOCK
    pos = lax.broadcasted_iota(jnp.int32, (n_rows, cw2), 0) % seg
    row_blk = lax.broadcasted_iota(jnp.int32, (SCAN_BLOCK, cw2), 0)

    def low_half(shape):
        return lax.broadcasted_iota(jnp.int32, shape, 1) < SSM_STATE

    def swap(v):
        return pltpu.roll(v, SSM_STATE, axis=1)

    def to_planes(va, vb):
        lo = low_half((va.shape[0], cw2))
        va_l = pltpu.roll(va, 3 * SSM_STATE, axis=1)
        vb_r = pltpu.roll(vb, SSM_STATE, axis=1)
        return (jnp.where(lo, va[:, 0:cw2], vb_r[:, 0:cw2]),
                jnp.where(lo, va_l[:, 0:cw2], vb[:, 0:cw2]),
                jnp.where(lo, va[:, cw2:2 * cw2], vb_r[:, cw2:2 * cw2]),
                jnp.where(lo, va_l[:, cw2:2 * cw2], vb[:, cw2:2 * cw2]))

    def from_planes(f_re, f_im, b_re, b_im):
        lo = low_half(f_re.shape)
        va = jnp.concatenate([jnp.where(lo, f_re, swap(f_im)), jnp.where(lo, b_re, swap(b_im))], axis=1)
        vb = jnp.concatenate([jnp.where(lo, swap(f_re), f_im), jnp.where(lo, swap(b_re), b_im)], axis=1)
        return va, vb

    for ga in range(0, GROUP_BLOCK, 2):
        gb = ga + 1
        xts = [xt_ref[ga], xt_ref[gb]]
        s_pair = [lax.dot_general(xts[i], gt_ref[g], TN_DIMS, preferred_element_type=F32)
                  for i, g in enumerate((ga, gb))]
        planes = to_planes(*s_pair)
        lo_t = low_half((AT_ROWS, cw2))
        if has_h0:
            h0_planes = to_planes(h0_ref[ga], h0_ref[gb])
        ent_planes = []
        for dr in range(2):
            down = dr == 0
            pa, pb = (at_ref[g][:, (2 * dr) * cw2:(2 * dr + 1) * cw2] for g in (ga, gb))
            qa, qb = (at_ref[g][:, (2 * dr + 1) * cw2:(2 * dr + 2) * cw2] for g in (ga, gb))
            ar_tab = jnp.where(lo_t, pa, pb)
            ai_tab = jnp.where(lo_t, -qa, qb)
            re3 = planes[2 * dr].reshape(n_blk, SCAN_BLOCK, cw2)
            im3 = planes[2 * dr + 1].reshape(n_blk, SCAN_BLOCK, cw2)
            for lvl in range(SCAN_LEVELS):
                m = 2 ** lvl
                valid = (row_blk >= m) if down else (row_blk < SCAN_BLOCK - m)
                row = SCAN_BLOCK + lvl
                ar = jnp.where(valid, ar_tab[row:row + 1], 0.0)
                ai = jnp.where(valid, ai_tab[row:row + 1], 0.0)
                shift = m if down else SCAN_BLOCK - m
                sh_re = pltpu.roll(re3, shift, axis=1)
                sh_im = pltpu.roll(im3, shift, axis=1)
                re3, im3 = re3 + ar * sh_re - ai * sh_im, im3 + ar * sh_im + ai * sh_re
            ar_blk, ai_blk = ar_tab[0:SCAN_BLOCK], ai_tab[0:SCAN_BLOCK]
            blk_re = [re3[i] for i in range(n_blk)]
            blk_im = [im3[i] for i in range(n_blk)]
            edge = slice(SCAN_BLOCK - 1, SCAN_BLOCK) if down else slice(0, 1)
            for q_i in range(n_seq):
                idxs = list(range(q_i * blk_per_seq, (q_i + 1) * blk_per_seq))
                idxs = idxs if down else idxs[::-1]
                for prev, cur in zip([None] + idxs[:-1], idxs):
                    if prev is not None:
                        c_re, c_im = blk_re[prev][edge], blk_im[prev][edge]
                    elif has_h0:
                        c_re = h0_planes[2 * dr][q_i:q_i + 1]
                        c_im = h0_planes[2 * dr + 1][q_i:q_i + 1]
                    else:
                        continue
                    c_re = jnp.broadcast_to(c_re, (SCAN_BLOCK, cw2))
                    c_im = jnp.broadcast_to(c_im, (SCAN_BLOCK, cw2))
                    blk_re[cur] = blk_re[cur] + ar_blk * c_re - ai_blk * c_im
                    blk_im[cur] = blk_im[cur] + ar_blk * c_im + ai_blk * c_re
            for part, blks in ((0, blk_re), (1, blk_im)):
                s = jnp.concatenate(blks, axis=0)
                if not has_h0:
                    fin_ref[part] = s
                ent = _shift_rows(s, 1, down)
                ent = jnp.where((pos >= 1) if down else (pos < seg - 1), ent, 0.0)
                if has_h0:
                    ent_blk = [ent[i * SCAN_BLOCK:(i + 1) * SCAN_BLOCK] for i in range(n_blk)]
                    at_edge = row_blk == (0 if down else SCAN_BLOCK - 1)
                    for q_i in range(n_seq):
                        bi = q_i * blk_per_seq if down else (q_i + 1) * blk_per_seq - 1
                        ent_blk[bi] = jnp.where(
                            at_edge, h0_planes[2 * dr + part][q_i:q_i + 1], ent_blk[bi])
                    ent = jnp.concatenate(ent_blk, axis=0)
                ent_planes.append(ent)
            if not has_h0:
                rows = pl.ds(seg - 1 if down else 0, n_seq, stride=seg)
                f_re, f_im = fin_ref[0, rows, :], fin_ref[1, rows, :]
                lo_s = low_half((n_seq, cw2))
                st_ref[ga, :, dr * cw2:(dr + 1) * cw2] = jnp.where(lo_s, f_re, swap(f_im))
                st_ref[gb, :, dr * cw2:(dr + 1) * cw2] = jnp.where(lo_s, swap(f_re), f_im)
        h_pair = from_planes(*ent_planes)
        for i, g in enumerate((ga, gb)):
            yt = (jnp.dot(mt_ref[g], xts[i], preferred_element_type=F32)
                  + lax.dot_general(wo_ref[g], h_pair[i].astype(BF16), NT_DIMS,
                                    preferred_element_type=F32))
            z = jax.nn.gelu(yt, approximate=True)
            grp = step * GROUP_BLOCK + g
            for t in range(CHUNK):
                zs_ref[t, pl.ds(pl.multiple_of(grp * SSM_GROUP, SSM_GROUP), SSM_GROUP), :] = (
                    z[t * SSM_GROUP:(t + 1) * SSM_GROUP, :])

    @pl.when(step == pl.num_programs(0) - 1)
    def _():
        def put(t):
            return pltpu.make_async_copy(zb_ref.at[t], z_hbm.at[:, t, :], zb_sem.at[t])

        for t0 in range(0, CHUNK, Z_BATCH):
            for t in range(t0, t0 + Z_BATCH):
                zt = zs_ref[t]
                gate = jnp.dot(wg_ref[...], zt.astype(BF16),
                               preferred_element_type=F32) + bg_ref[...]
                zb_ref[t] = (zt * jax.nn.sigmoid(gate)).T
            for t in range(t0, t0 + Z_BATCH):
                put(t).start(priority=t % N_DMA_THREADS)
        for t in range(CHUNK):
            put(t).wait()


def _ssm(xt, mt, gt, wo, at, w_glu_t, b_glu_col, h0, n_seq):
    n_rows = xt.shape[-1]
    has_h0 = h0 is not None
    gb = GROUP_BLOCK
    w_spec = pl.BlockSpec((gb, CW, CW), lambda i: (i, 0, 0))
    in_specs = [pl.BlockSpec((gb, CW, n_rows), lambda i: (i, 0, 0)),
                w_spec, w_spec, w_spec,
                pl.BlockSpec((gb, AT_ROWS, 4 * 2 * SSM_STATE), lambda i: (i, 0, 0)),
                pl.BlockSpec((SSM_WIDTH, SSM_WIDTH), lambda i: (0, 0)),
                pl.BlockSpec((SSM_WIDTH, 1), lambda i: (0, 0))]
    args = [xt, mt, gt, wo, at, w_glu_t, b_glu_col]
    out_specs = [pl.BlockSpec(memory_space=pl.ANY)]
    out_shape = [jax.ShapeDtypeStruct((n_rows, CHUNK, SSM_WIDTH), F32)]
    scratch = [pltpu.VMEM((CHUNK, SSM_WIDTH, n_rows), F32),
               pltpu.VMEM((CHUNK, n_rows, SSM_WIDTH), F32),
               pltpu.SemaphoreType.DMA((CHUNK,))]
    if has_h0:
        in_specs.append(pl.BlockSpec((gb, n_seq, CW), lambda i: (i, 0, 0)))
        args.append(h0)
    else:
        out_specs.append(pl.BlockSpec((gb, n_seq, CW), lambda i: (i, 0, 0)))
        out_shape.append(jax.ShapeDtypeStruct((N_GROUPS, n_seq, CW), F32))
        scratch.append(pltpu.VMEM((2, n_rows, 2 * SSM_STATE), F32))
    return pl.pallas_call(
        functools.partial(_ssm_kernel, n_seq=n_seq, has_h0=has_h0),
        grid=(N_GROUPS // gb,),
        in_specs=in_specs,
        out_specs=out_specs,
        out_shape=out_shape,
        scratch_shapes=scratch,
        compiler_params=_cparams(1, fuse_inputs=[False] * 5 + [True] * (len(args) - 5)),
        name="ssm_scan_glu",
    )(*args)


FF_CHUNK = 256
FFN_TOK = 1024


def _out_ffn_kernel(x_ref, attn_ref, z_ref, mod_ref, g_ref, wo_hbm, wfi_hbm, wfo_hbm,
                    o_ref, act_ref, wo_ref, wfi_ref, wfo_ref, w_sem):
    first = pl.program_id(0) == 0
    fetch = [pltpu.make_async_copy(src, dst, w_sem.at[i]) for i, (src, dst) in
             enumerate(((wo_hbm, wo_ref), (wfi_hbm, wfi_ref), (wfo_hbm, wfo_ref)))]

    @pl.when(first)
    def _():
        for i, cp in enumerate(fetch):
            cp.start(priority=min(i, N_DMA_THREADS - 1))
        fetch[0].wait()

    gate1 = mod_ref[:, 2 * D_MODEL:3 * D_MODEL]
    shift2 = mod_ref[:, 3 * D_MODEL:4 * D_MODEL]
    scale2 = mod_ref[:, 4 * D_MODEL:5 * D_MODEL]
    gate2 = mod_ref[:, 5 * D_MODEL:6 * D_MODEL]
    half = FFN_TOK // 2
    gain2 = g_ref[2:3, :] * (1.0 + scale2)

    def pre_ffn(hf):
        r = slice(hf * half, (hf + 1) * half)
        mixer = jnp.concatenate([attn_ref[r, :], z_ref[r, :].astype(BF16)], axis=1)
        mix = jnp.dot(mixer, wo_ref[...], preferred_element_type=F32)
        x1 = x_ref[r, :] + gate1 * _rms(mix, g_ref[1:2, :])
        ms = jnp.mean(x1 * x1, axis=-1, keepdims=True)
        return x1, (x1 * lax.rsqrt(ms + NORM_EPS) * gain2 + shift2).astype(BF16)

    def ffn_in(hf, h, chunks):
        r = slice(hf * half, (hf + 1) * half)
        for c in chunks:
            lo = c * FF_CHUNK
            gt = jnp.dot(h, wfi_ref[:, lo:lo + FF_CHUNK], preferred_element_type=F32)
            up = jnp.dot(h, wfi_ref[:, D_FF + lo:D_FF + lo + FF_CHUNK],
                         preferred_element_type=F32)
            act_ref[r, lo:lo + FF_CHUNK] = (_silu(gt) * up).astype(BF16)

    def ffn_out(hf, x1):
        r = slice(hf * half, (hf + 1) * half)
        f = jnp.dot(act_ref[r, :], wfo_ref[...], preferred_element_type=F32)
        o_ref[r, :] = x1 + gate2 * _rms(f, g_ref[3:4, :])

    n_chunks = D_FF // FF_CHUNK
    x1_a, h_a = pre_ffn(0)
    x1_b, h_b = pre_ffn(1)
    @pl.when(first)
    def _():
        fetch[1].wait()
        fetch[2].wait()

    ffn_in(0, h_a, range(n_chunks))
    ffn_out(0, x1_a)
    ffn_in(1, h_b, range(n_chunks))
    ffn_out(1, x1_b)


def _out_ffn(x2d, attn, z, mods, mod_rows, norm_g, w_o, w_ffn_in, w_ffn_out):
    n_tok = x2d.shape[0]
    z2d = z.reshape(n_tok, SSM_WIDTH)
    n_steps = n_tok // FFN_TOK
    mod_row0, n_mod = mod_rows
    steps_per_mod = n_steps // n_mod
    const = lambda i: (0, 0)
    row_spec = lambda w: pl.BlockSpec((FFN_TOK, w), lambda i: (i, 0))
    return pl.pallas_call(
        _out_ffn_kernel,
        grid=(n_steps,),
        in_specs=[row_spec(D_MODEL), row_spec(ATT_WIDTH), row_spec(SSM_WIDTH),
                  pl.BlockSpec((None, 1, N_MOD * D_MODEL),
                               lambda i: (mod_row0 + i // steps_per_mod, 0, 0)),
                  pl.BlockSpec((4, D_MODEL), const),
                  pl.BlockSpec(memory_space=pl.ANY),
                  pl.BlockSpec(memory_space=pl.ANY),
                  pl.BlockSpec(memory_space=pl.ANY)],
        out_specs=row_spec(D_MODEL),
        out_shape=jax.ShapeDtypeStruct((n_tok, D_MODEL), F32),
        scratch_shapes=[pltpu.VMEM((FFN_TOK, D_FF), BF16),
                        pltpu.VMEM((2 * ATT_WIDTH, D_MODEL), BF16),
                        pltpu.VMEM((D_MODEL, 2 * D_FF), BF16),
                        pltpu.VMEM((D_FF, D_MODEL), BF16),
                        pltpu.SemaphoreType.DMA((3,))],
        compiler_params=_cparams(1),
        name="out_proj_ffn",
    )(x2d, attn, z2d, mods, norm_g, w_o, w_ffn_in, w_ffn_out)


def _rope_tables(seq_len):
    t = np.arange(seq_len)
    row = (t // GRID_W).astype(np.float32)
    col = (t % GRID_W).astype(np.float32)
    half = HEAD_DIM // 2
    inv_freq = (np.float32(ROPE_BASE)
                ** (-np.arange(0, half, 2, dtype=np.float32) / np.float32(half))).astype(np.float32)
    ang_r = row[:, None] * inv_freq
    ang_c = col[:, None] * inv_freq
    ang = np.concatenate([ang_r, ang_r, ang_c, ang_c], axis=-1)
    cos, sin = np.cos(ang), np.sin(ang)
    upper = (np.arange(HEAD_DIM) % 32) < 16
    sa = np.where(upper, -sin, 0.0)
    sb = np.where(upper, 0.0, sin)
    two = lambda a: jnp.asarray(np.concatenate([a, a], axis=-1), dtype=F32)
    return two(cos), two(sa), two(sb)


def _layer(x, mods, mod_rows, lam_init, rope_tabs, ctx_k, ctx_v, h0, weights, prep):
    n_batch, seq_len = x.shape[:2]
    g = weights['norm_g']
    outs = _in_proj(x, mods, mod_rows, g[0:1], weights['w_in'], rope_tabs)
    q, k, v = outs[:3]
    pending = () if 'late_bf16' in weights else weights['late_f32']
    attn, cast = _attention(q, k, v, ctx_k, ctx_v, weights['lam'], weights['subln_g'],
                            n_batch, seq_len, lam_init, cast_weights=pending)
    if pending:
        weights['late_bf16'] = cast
    ssm_out = _ssm(outs[-1], *prep, weights['w_glu_t'], weights['b_glu_col'], h0, n_batch)
    y = _out_ffn(x.reshape(n_batch * seq_len, D_MODEL), attn, ssm_out[0], mods, mod_rows, g,
                 *weights['late_bf16'])
    return y.reshape(x.shape), outs[3:-1], ssm_out[1:]


def kernel(x_prompt, x_sample, cache_k, cache_v, state_ssm_re, state_ssm_im, c, c_ctx, w_mod, b_mod, norm_g, w_in, lam_params, subln_g, ssm_lambda_re, ssm_lambda_im, ssm_log_step, ssm_b_re, ssm_b_im, ssm_c_re, ssm_c_im, ssm_d, w_glu, b_glu, w_o, w_ffn_in, w_ffn_out):
    depth = w_mod.shape[0]
    assert depth == 1
    bp = x_prompt.shape[0]
    bd, ld_len = x_sample.shape[:2]
    past = cache_k.shape[2]
    xp, xs = x_prompt, x_sample
    rope_tabs = _rope_tables(ld_len)
    ks_out, vs_out, hr_out, hi_out = [], [], [], []
    for l in range(depth):
        lam_init = 0.8 - 0.6 * math.exp(-0.3 * l)
        prep, mods, w_in_bf16 = _ssm_prep_and_modulation(
            ssm_lambda_re[l], ssm_lambda_im[l], ssm_log_step[l], ssm_b_re[l], ssm_b_im[l],
            ssm_c_re[l], ssm_c_im[l], ssm_d[l], c_ctx, c, w_mod[l], b_mod[l], w_in[l])
        weights = {
            'norm_g': norm_g[l],
            'w_in': w_in_bf16,
            'lam': lam_params[l], 'subln_g': subln_g[l],
            'w_glu_t': w_glu[l].T.astype(BF16), 'b_glu_col': b_glu[l].reshape(SSM_WIDTH, 1),
            'late_f32': (w_o[l], w_ffn_in[l], w_ffn_out[l]),
        }
        ck = jnp.transpose(cache_k[:, l], (0, 2, 3, 1)).reshape(bd, ATT_WIDTH, past)
        cv = cache_v[:, l]
        h0 = jnp.stack([state_ssm_re[:, l], state_ssm_im[:, l]], axis=2)
        h0 = h0.transpose(3, 0, 1, 2, 4).reshape(N_GROUPS, bd, CW)
        xs, _, _ = _layer(xs, mods, (1, bd), lam_init, rope_tabs, ck, cv, h0, weights, prep)
        xp, (k_ctx, v_ctx), (st,) = _layer(xp, mods, (0, 1), lam_init, None, None, None, None,
                                           weights, prep)
        ks_out.append(jnp.swapaxes(k_ctx, 1, 2).reshape(bp, -1, 2 * N_HEADS, HEAD_DIM))
        vs_out.append(v_ctx)
        fin = st.reshape(N_GROUPS, bp, 2, 2, SSM_STATE).transpose(1, 2, 3, 0, 4)
        hr_out.append(fin[:, :, 0])
        hi_out.append(fin[:, :, 1])
    return (xp, xs, jnp.stack(ks_out, axis=1), jnp.stack(vs_out, axis=1),
            jnp.stack(hr_out, axis=1), jnp.stack(hi_out, axis=1))
```

```python
import functools
import math

import jax
import jax.numpy as jnp
import numpy as np
from jax import lax
from jax.experimental import pallas as pl
from jax.experimental.pallas import tpu as pltpu

F32 = jnp.float32
BF16 = jnp.bfloat16

D_MODEL = 1024
GRID_W = 64
ATT_WIDTH = 512
SSM_WIDTH = 512
HEAD_DIM = 64
N_HEADS = 4
HEAD_W = 2 * HEAD_DIM
SSM_GROUP = 16
N_GROUPS = 32
SSM_STATE = 64
D_FF = 2816
N_MOD = 6
ROPE_BASE = 10000.0
NORM_EPS = 1e-6

CHUNK = 16
CW = CHUNK * SSM_GROUP
SCAN_BLOCK = 8
SCAN_LEVELS = 3
AT_ROWS = 16
GROUP_BLOCK = 4
Z_BATCH = 8

ROWS_PER_TILE = 128
TOK_PER_TILE = ROWS_PER_TILE * CHUNK
SUB_TOK = 1024
TQ_ITEM = 256
N_SUB = TOK_PER_TILE // SUB_TOK

VMEM_LIMIT = 56 * 1024 * 1024
N_DMA_THREADS = 2

NT_DIMS = (((1,), (1,)), ((), ()))
TN_DIMS = (((0,), (0,)), ((), ()))


def _cparams(n_axes, fuse_inputs=None):
    return pltpu.CompilerParams(
        dimension_semantics=("arbitrary",) * n_axes,
        allow_input_fusion=fuse_inputs,
        vmem_limit_bytes=VMEM_LIMIT)


def _rms(x, g):
    ms = jnp.mean(x * x, axis=-1, keepdims=True)
    return x * lax.rsqrt(ms + NORM_EPS) * g


def _silu(x):
    return x * jax.nn.sigmoid(x)


MOD_ROWS = 8


def _mod_kernel(ctx_ref, c_ref, w_ref, b_ref, o_ref):
    n_lat, tk = c_ref.shape
    row = lax.broadcasted_iota(jnp.int32, (MOD_ROWS, tk), 0)
    cond = jnp.where(row == 0, ctx_ref[...], 0.0)
    for b in range(n_lat):
        cond = jnp.where(row == 1 + b, c_ref[b:b + 1, :], cond)
    part = jnp.dot(_silu(cond).astype(BF16), w_ref[...].astype(BF16),
                   preferred_element_type=F32)

    @pl.when(pl.program_id(0) == 0)
    def _():
        o_ref[:, 0, :] = part + b_ref[...]

    @pl.when(pl.program_id(0) > 0)
    def _():
        o_ref[:, 0, :] += part


def _rope(x, cos, sa, sb):
    return (x * cos + pltpu.roll(x, HEAD_W - 16, axis=1) * sa
            + pltpu.roll(x, 16, axis=1) * sb)


def _in_proj_kernel(*refs, rope, seqs_per_sub):
    x_ref, x3_hbm, mod_ref, g_ref, w_ref = refs[:5]
    refs = refs[5:]
    if rope:
        cos_ref, sa_ref, sb_ref = refs[:3]
        refs = refs[3:]
        q_ref, k_ref, v_ref, ut_ref, wut_ref, xt_ref, xt_sem = refs
    else:
        q_ref, k_ref, v_ref, kc_ref, vc_ref, ut_ref, wut_ref, xt_ref, xt_sem = refs
    tile = pl.program_id(0)
    j = pl.program_id(1)
    t_per_sub = CHUNK // N_SUB
    t_early = CHUNK - t_per_sub

    def gather(tile_idx, t):
        src = x3_hbm.at[pl.ds(tile_idx * ROWS_PER_TILE, ROWS_PER_TILE), t, :]
        return pltpu.make_async_copy(src, xt_ref.at[t], xt_sem.at[t])

    @pl.when(j == 0)
    def _():
        @pl.when(tile == 0)
        def _():
            for t in range(t_early):
                gather(0, t).start(priority=t % N_DMA_THREADS)
        for t in range(t_early, CHUNK):
            gather(tile, t).start(priority=t % N_DMA_THREADS)

    t_base = j * t_per_sub
    for d in range(t_per_sub):
        gather(tile, t_base + d).wait()

    @pl.when((j == N_SUB - 1) & (tile + 1 < pl.num_programs(0)))
    def _():
        for t in range(t_early):
            gather(tile + 1, t).start(priority=t % N_DMA_THREADS)

    @pl.when((tile == 0) & (j == 0))
    def _():
        wut_ref[...] = w_ref[:, 3 * ATT_WIDTH:].T

    shift = mod_ref[:, 0:D_MODEL]
    gain = g_ref[...] * (1.0 + mod_ref[:, D_MODEL:2 * D_MODEL])

    def norm_mod(xv):
        ms = jnp.mean(xv * xv, axis=-1, keepdims=True)
        return (xv * lax.rsqrt(ms + NORM_EPS) * gain + shift).astype(BF16)

    def ssm_input(d0):
        xt = jnp.concatenate([xt_ref[t_base + d0], xt_ref[t_base + d0 + 1]], axis=0)
        ut = lax.dot_general(wut_ref[...], norm_mod(xt), NT_DIMS,
                             preferred_element_type=F32)
        for d in range(2):
            blk = ut[:, d * ROWS_PER_TILE:(d + 1) * ROWS_PER_TILE]
            row0 = pl.multiple_of((t_base + d0 + d) * SSM_GROUP, SSM_GROUP)
            ut_ref[:, pl.ds(row0, SSM_GROUP), :] = (
                blk.reshape(N_GROUPS, SSM_GROUP, ROWS_PER_TILE).astype(ut_ref.dtype))

    proj = jnp.dot(norm_mod(x_ref[...]), w_ref[:, 0:3 * ATT_WIDTH],
                   preferred_element_type=F32)
    q = proj[:, 0:ATT_WIDTH]
    k = proj[:, ATT_WIDTH:2 * ATT_WIDTH]
    v = proj[:, 2 * ATT_WIDTH:3 * ATT_WIDTH]
    qscale = HEAD_DIM ** -0.5 * math.log2(math.e)
    if rope:
        cos, sa, sb = cos_ref[...], sa_ref[...], sb_ref[...]
        for hd in range(N_HEADS):
            sl = slice(hd * HEAD_W, (hd + 1) * HEAD_W)
            q_ref[:, sl] = (_rope(q[:, sl], cos, sa, sb) * qscale).astype(q_ref.dtype)
            k_ref[:, sl] = _rope(k[:, sl], cos, sa, sb).astype(k_ref.dtype)
    else:
        q_ref[...] = (q * qscale).astype(q_ref.dtype)
        k_ref[...] = k.astype(k_ref.dtype)
        seq = SUB_TOK // seqs_per_sub
        k_t = k.T
        for b in range(seqs_per_sub):
            kc_ref[b] = k_t[:, b * seq:(b + 1) * seq]
            for hd in range(N_HEADS):
                vc_ref[b, :, hd, :] = v[b * seq:(b + 1) * seq, hd * HEAD_W:(hd + 1) * HEAD_W]
    v_ref[...] = v.astype(v_ref.dtype)

    for d0 in range(0, t_per_sub, 2):
        ssm_input(d0)


def _in_proj(x, mods, mod_rows, g0, w_in, rope_tabs):
    n_batch, seq_len = x.shape[:2]
    n_tok = n_batch * seq_len
    n_rows = n_tok // CHUNK
    n_tiles = n_tok // TOK_PER_TILE
    mod_row0, n_mod = mod_rows
    tiles_per_mod = n_tiles // n_mod
    rope = rope_tabs is not None
    seqs_per_sub = max(1, SUB_TOK // seq_len)
    in_specs = [pl.BlockSpec((SUB_TOK, D_MODEL), lambda i, j: (i * N_SUB + j, 0)),
                pl.BlockSpec(memory_space=pl.ANY),
                pl.BlockSpec((None, 1, 2 * D_MODEL),
                             lambda i, j: (mod_row0 + i // tiles_per_mod, 0, 0)),
                pl.BlockSpec((1, D_MODEL), lambda i, j: (0, 0)),
                pl.BlockSpec((D_MODEL, 4 * ATT_WIDTH), lambda i, j: (0, 0))]
    args = [x.reshape(n_tok, D_MODEL), x.reshape(n_rows, CHUNK, D_MODEL), mods, g0, w_in]
    row_spec = pl.BlockSpec((SUB_TOK, ATT_WIDTH), lambda i, j: (i * N_SUB + j, 0))
    row_shape = jax.ShapeDtypeStruct((n_tok, ATT_WIDTH), BF16)
    out_specs = [row_spec, row_spec, row_spec]
    out_shape = [row_shape, row_shape, row_shape]
    if rope:
        assert seq_len == TOK_PER_TILE
        for tab in rope_tabs:
            in_specs.append(pl.BlockSpec((SUB_TOK, HEAD_W), lambda i, j: (j, 0)))
            args.append(tab)
    else:
        out_specs += [pl.BlockSpec((seqs_per_sub, ATT_WIDTH, seq_len),
                                   lambda i, j: (i * N_SUB + j, 0, 0)),
                      pl.BlockSpec((seqs_per_sub, seq_len, N_HEADS, HEAD_W),
                                   lambda i, j: (i * N_SUB + j, 0, 0, 0))]
        out_shape += [jax.ShapeDtypeStruct((n_batch, ATT_WIDTH, seq_len), F32),
                      jax.ShapeDtypeStruct((n_batch, seq_len, N_HEADS, HEAD_W), F32)]
    out_specs.append(pl.BlockSpec((N_GROUPS, CW, ROWS_PER_TILE), lambda i, j: (0, 0, i)))
    out_shape.append(jax.ShapeDtypeStruct((N_GROUPS, CW, n_rows), BF16))
    return pl.pallas_call(
        functools.partial(_in_proj_kernel, rope=rope, seqs_per_sub=seqs_per_sub),
        grid=(n_tiles, N_SUB),
        in_specs=in_specs,
        out_specs=out_specs,
        out_shape=out_shape,
        scratch_shapes=[pltpu.VMEM((SSM_WIDTH, D_MODEL), BF16),
                        pltpu.VMEM((CHUNK, ROWS_PER_TILE, D_MODEL), F32),
                        pltpu.SemaphoreType.DMA((CHUNK,))],
        compiler_params=_cparams(2),
        name="in_proj",
    )(*args)


def _attn_kernel(*refs, has_ctx, lam_init, n_seq, seq_len, tq, n_cast):
    if n_cast:
        cast_in = refs[len(refs) - 2 * n_cast - 1:len(refs) - n_cast - 1]
        cast_out = refs[len(refs) - n_cast:]
        refs = refs[:len(refs) - 2 * n_cast - 1] + (refs[len(refs) - n_cast - 1],)
        for src, dst in zip(cast_in, cast_out):
            dst[...] = src[...].astype(dst.dtype)
    if has_ctx:
        lam_ref, sg_ref, q_ref, ck_ref, cv_ref, k_ref, v_ref, o_ref = refs
    else:
        lam_ref, sg_ref, q_ref, k_ref, v_ref, o_ref = refs
    lp = lam_ref[...]
    lam = (jnp.exp(jnp.sum(lp[0:1] * lp[1:2], axis=-1, keepdims=True))
           - jnp.exp(jnp.sum(lp[2:3] * lp[3:4], axis=-1, keepdims=True)) + lam_init)
    first_map = lax.broadcasted_iota(jnp.int32, (1, HEAD_W), 1) < HEAD_DIM
    ti = min(TQ_ITEM, tq)
    for row0 in range(0, n_seq * tq, ti):
        b = row0 // tq
        q_rows = slice(row0, row0 + ti)
        for hd in range(N_HEADS):
            sl = slice(hd * HEAD_W, (hd + 1) * HEAD_W)
            qh = q_ref[q_rows, sl]
            zero = jnp.zeros_like(qh)
            qs = jnp.concatenate([jnp.where(first_map, qh, zero),
                                  jnp.where(first_map, zero, qh)], axis=0)
            kv_rows = slice(b * seq_len, (b + 1) * seq_len)
            parts = [(k_ref[kv_rows, sl], v_ref[kv_rows, sl])]
            scores = [lax.dot_general(qs, parts[0][0], NT_DIMS, preferred_element_type=F32)]
            if has_ctx:
                parts.insert(0, (None, cv_ref[:, hd, :].astype(BF16)))
                scores.insert(0, jnp.dot(qs, ck_ref[sl, :].astype(BF16),
                                         preferred_element_type=F32))
            mx = scores[0].max(axis=-1, keepdims=True)
            for s in scores[1:]:
                mx = jnp.maximum(mx, s.max(axis=-1, keepdims=True))
            acc = None
            for s, (_, vv) in zip(scores, parts):
                e = jnp.exp2(s - mx).astype(BF16)
                v_one = jnp.concatenate([vv, jnp.ones_like(vv)], axis=1)
                pv = jnp.dot(e, v_one, preferred_element_type=F32)
                acc = pv if acc is None else acc + pv
            num = acc[:, 0:HEAD_W] / acc[:, HEAD_W:2 * HEAD_W]
            o = num[0:ti] - lam * num[ti:2 * ti]
            o = _rms(o, sg_ref[...]) * (1.0 - lam_init)
            o_ref[q_rows, sl] = o.astype(o_ref.dtype)


def _attention(q, k, v, ctx_k, ctx_v, lam_params, subln_g, n_batch, seq_len, lam_init,
               cast_weights=()):
    has_ctx = ctx_k is not None
    tq = min(1024, seq_len)
    n_q = seq_len // tq
    n_seq = 1 if n_q > 1 else min(4, n_batch)
    in_specs = [pl.BlockSpec((4, HEAD_DIM), lambda b, i: (0, 0)),
                pl.BlockSpec((1, HEAD_W), lambda b, i: (0, 0)),
                pl.BlockSpec((n_seq * tq, ATT_WIDTH), lambda b, i: (b * n_q + i, 0))]
    args = [lam_params, subln_g.reshape(1, HEAD_W), q]
    if has_ctx:
        past = ctx_v.shape[1]
        in_specs += [pl.BlockSpec((None, ATT_WIDTH, past), lambda b, i: (b, 0, 0)),
                     pl.BlockSpec((None, past, N_HEADS, HEAD_W), lambda b, i: (b, 0, 0, 0))]
        args += [ctx_k, ctx_v]
    kv_spec = pl.BlockSpec((n_seq * seq_len, ATT_WIDTH), lambda b, i: (b, 0))
    in_specs += [kv_spec, kv_spec]
    args += [k, v]
    out_specs = [pl.BlockSpec((n_seq * tq, ATT_WIDTH), lambda b, i: (b * n_q + i, 0))]
    out_shape = [jax.ShapeDtypeStruct((n_batch * seq_len, ATT_WIDTH), BF16)]
    n_steps = (n_batch // n_seq) * n_q
    for w in cast_weights:
        rows = w.shape[0] // n_steps
        spec = pl.BlockSpec((rows, w.shape[1]), lambda b, i: (b * n_q + i, 0))
        in_specs.append(spec)
        args.append(w)
        out_specs.append(spec)
        out_shape.append(jax.ShapeDtypeStruct(w.shape, BF16))
    outs = pl.pallas_call(
        functools.partial(_attn_kernel, has_ctx=has_ctx, lam_init=lam_init,
                          n_seq=n_seq, seq_len=seq_len, tq=tq, n_cast=len(cast_weights)),
        grid=(n_batch // n_seq, n_q),
        in_specs=in_specs,
        out_specs=out_specs,
        out_shape=out_shape,
        compiler_params=_cparams(2, fuse_inputs=[True, True, False] + [has_ctx] * (2 * has_ctx)
                                 + [False] * (2 + len(cast_weights))),
        name="diff_attention",
    )(*args)
    return outs[0], tuple(outs[1:])


def _cmul(ar, ai, br, bi):
    return ar * br - ai * bi, ar * bi + ai * br


def _ssm_prep_kernel(lre_ref, lim_ref, ls_ref, bre_ref, bim_ref, cre_ref, cim_ref, d_ref,
                     mt_ref, gt_ref, wo_ref, at_ref):
    lane = lax.broadcasted_iota(jnp.int32, (SSM_GROUP, CW), 1)
    chan = lax.broadcasted_iota(jnp.int32, (SSM_GROUP, CW), 0)
    for gi in range(GROUP_BLOCK):
        gt_cols, wo_cols, at_cols, toeplitz = [], [], [], []
        for dr in range(2):
            lr = jnp.minimum(lre_ref[dr, gi], -1e-4)
            li = lim_ref[dr, gi]
            step = jnp.exp(ls_ref[dr, gi])
            mag = jnp.exp(lr * step)
            a_re = mag * jnp.cos(li * step)
            a_im = mag * jnp.sin(li * step)
            den = lr * lr + li * li
            nr = a_re - 1.0
            f_re = (nr * lr + a_im * li) / den
            f_im = (a_im * lr - nr * li) / den
            bt_re, bt_im = bre_ref[dr, gi], bim_ref[dr, gi]
            bb_re, bb_im = _cmul(f_re, f_im, bt_re, bt_im)
            c_re, c_im = cre_ref[dr, gi], cim_ref[dr, gi]
            pw = [(jnp.ones_like(a_re), jnp.zeros_like(a_im))]
            for _ in range(CHUNK):
                pw.append(_cmul(pw[-1][0], pw[-1][1], a_re, a_im))
            g_re, g_im, e_re, e_im = [], [], [], []
            for t in range(CHUNK):
                pr, pi = pw[CHUNK - 1 - t] if dr == 0 else pw[t]
                r, i = _cmul(bb_re, bb_im, pr, pi)
                g_re.append(r)
                g_im.append(i)
                pr, pi = pw[t + 1] if dr == 0 else pw[CHUNK - t]
                r, i = _cmul(c_re, c_im, pr, pi)
                e_re.append(r)
                e_im.append(-i)
            g_cat = jnp.concatenate([jnp.concatenate(g_re, axis=0),
                                     jnp.concatenate(g_im, axis=0)], axis=1)
            gt_cols.append(g_cat)
            wo_cols.append(jnp.concatenate([jnp.concatenate(e_re, axis=0),
                                            jnp.concatenate(e_im, axis=0)], axis=1))
            c_cat = jnp.concatenate([c_re, -c_im], axis=1)
            toeplitz.append(lax.dot_general(c_cat, g_cat, NT_DIMS,
                                            precision=lax.Precision.HIGHEST,
                                            preferred_element_type=F32))
            apw = [pw[CHUNK]]
            for _ in range(SCAN_BLOCK - 1):
                apw.append(_cmul(apw[-1][0], apw[-1][1], apw[0][0], apw[0][1]))
            order = list(range(SCAN_BLOCK)) if dr == 0 else list(range(SCAN_BLOCK - 1, -1, -1))
            order += [2 ** l - 1 for l in range(SCAN_LEVELS)]
            order += [0] * (AT_ROWS - len(order))
            at_cols += [jnp.concatenate([jnp.concatenate([apw[i][0], apw[i][0]], axis=1)
                                         for i in order], axis=0),
                        jnp.concatenate([jnp.concatenate([-apw[i][1], apw[i][1]], axis=1)
                                         for i in order], axis=0)]
        kf_rev, kb = toeplitz
        d_skip = d_ref[gi]
        blocks = []
        for t in range(CHUNK):
            fwd = pltpu.roll(kf_rev, (CW - (CHUNK - 1 - t) * SSM_GROUP) % CW, axis=1)
            bwd = pltpu.roll(kb, t * SSM_GROUP, axis=1)
            blocks.append(jnp.where(lane < (t + 1) * SSM_GROUP, fwd, 0.0)
                          + jnp.where(lane >= t * SSM_GROUP, bwd, 0.0)
                          + jnp.where(lane == chan + t * SSM_GROUP, d_skip, 0.0))
        mt_ref[gi] = jnp.concatenate(blocks, axis=0).astype(mt_ref.dtype)
        gt_ref[gi] = jnp.concatenate(gt_cols, axis=1).astype(gt_ref.dtype)
        wo_ref[gi] = jnp.concatenate(wo_cols, axis=1).astype(wo_ref.dtype)
        at_ref[gi] = jnp.concatenate(at_cols, axis=1)


N_PREP_IN, N_PREP_OUT, N_MOD_IN = 8, 4, 4


def _prep_mod_kernel(*refs):
    prep_in = refs[:N_PREP_IN]
    mod_in = refs[N_PREP_IN:N_PREP_IN + N_MOD_IN]
    w_in_ref = refs[N_PREP_IN + N_MOD_IN]
    outs = refs[N_PREP_IN + N_MOD_IN + 1:]
    _ssm_prep_kernel(*prep_in, *outs[:N_PREP_OUT])
    _mod_kernel(*mod_in, outs[N_PREP_OUT])
    outs[N_PREP_OUT + 1][...] = w_in_ref[...].astype(BF16)


def _ssm_prep_and_modulation(lam_re, lam_im, log_step, b_re, b_im, c_re, c_im, d_skip,
                             c_ctx, c, w_mod, b_mod, w_in):
    row = lambda a: a.reshape(2, N_GROUPS, 1, SSM_STATE)
    bt = lambda a: jnp.swapaxes(a, 2, 3)
    d_row = jnp.tile((d_skip[0] + d_skip[1]).reshape(N_GROUPS, 1, SSM_GROUP), (1, 1, CHUNK))
    gb = GROUP_BLOCK
    n_steps = N_GROUPS // gb
    vec_spec = pl.BlockSpec((2, gb, 1, SSM_STATE), lambda i: (0, i, 0, 0))
    mat_spec = pl.BlockSpec((2, gb, SSM_GROUP, SSM_STATE), lambda i: (0, i, 0, 0))
    w_spec = pl.BlockSpec((gb, CW, CW), lambda i: (i, 0, 0))
    w_shape = jax.ShapeDtypeStruct((N_GROUPS, CW, CW), BF16)
    n_mod = w_mod.shape[1]
    tk = D_MODEL // n_steps
    assert 1 + c.shape[0] <= MOD_ROWS
    outs = pl.pallas_call(
        _prep_mod_kernel,
        grid=(n_steps,),
        in_specs=[vec_spec, vec_spec,
                  pl.BlockSpec((2, gb, 1, 1), lambda i: (0, i, 0, 0)),
                  mat_spec, mat_spec, mat_spec, mat_spec,
                  pl.BlockSpec((gb, 1, CW), lambda i: (i, 0, 0)),
                  pl.BlockSpec((1, tk), lambda k: (0, k)),
                  pl.BlockSpec((c.shape[0], tk), lambda k: (0, k)),
                  pl.BlockSpec((tk, n_mod), lambda k: (k, 0)),
                  pl.BlockSpec((1, n_mod), lambda k: (0, 0)),
                  pl.BlockSpec((tk, w_in.shape[1]), lambda k: (k, 0))],
        out_specs=[w_spec, w_spec, w_spec,
                   pl.BlockSpec((gb, AT_ROWS, 4 * 2 * SSM_STATE), lambda i: (i, 0, 0)),
                   pl.BlockSpec((MOD_ROWS, 1, n_mod), lambda k: (0, 0, 0)),
                   pl.BlockSpec((tk, w_in.shape[1]), lambda k: (k, 0))],
        out_shape=[w_shape, w_shape, w_shape,
                   jax.ShapeDtypeStruct((N_GROUPS, AT_ROWS, 4 * 2 * SSM_STATE), F32),
                   jax.ShapeDtypeStruct((MOD_ROWS, 1, n_mod), F32),
                   jax.ShapeDtypeStruct(w_in.shape, BF16)],
        compiler_params=_cparams(1, fuse_inputs=[True] * 9 + [False, False, True, False]),
        name="ssm_prep_modulation",
    )(row(lam_re), row(lam_im), log_step.reshape(2, N_GROUPS, 1, 1),
      bt(b_re), bt(b_im), c_re, c_im, d_row,
      c_ctx.reshape(1, D_MODEL), c, w_mod, b_mod.reshape(1, n_mod), w_in)
    return outs[:N_PREP_OUT], outs[N_PREP_OUT], outs[N_PREP_OUT + 1]


def _shift_rows(x, m, down):
    n = x.shape[0]
    return pltpu.roll(x, m if down else n - m, axis=0)


def _ssm_kernel(*refs, n_seq, has_h0):
    if has_h0:
        (xt_ref, mt_ref, gt_ref, wo_ref, at_ref, wg_ref, bg_ref, h0_ref,
         z_hbm, zs_ref, zb_ref, zb_sem) = refs
    else:
        (xt_ref, mt_ref, gt_ref, wo_ref, at_ref, wg_ref, bg_ref,
         z_hbm, st_ref, zs_ref, zb_ref, zb_sem, fin_ref) = refs
    step = pl.program_id(0)
    n_rows = xt_ref.shape[-1]
    seg = n_rows // n_seq
    cw2 = 2 * SSM_STATE
    assert seg % SCAN_BLOCK == 0
    n_blk = n_rows // SCAN_BLOCK
    blk_per_seq = seg // SCAN_BLOCK
    pos = lax.broadcasted_iota(jnp.int32, (n_rows, cw2), 0) % seg
    row_blk = lax.broadcasted_iota(jnp.int32, (SCAN_BLOCK, cw2), 0)

    def low_half(shape):
        return lax.broadcasted_iota(jnp.int32, shape, 1) < SSM_STATE

    def swap(v):
        return pltpu.roll(v, SSM_STATE, axis=1)

    def to_planes(va, vb):
        lo = low_half((va.shape[0], cw2))
        va_l = pltpu.roll(va, 3 * SSM_STATE, axis=1)
        vb_r = pltpu.roll(vb, SSM_STATE, axis=1)
        return (jnp.where(lo, va[:, 0:cw2], vb_r[:, 0:cw2]),
                jnp.where(lo, va_l[:, 0:cw2], vb[:, 0:cw2]),
                jnp.where(lo, va[:, cw2:2 * cw2], vb_r[:, cw2:2 * cw2]),
                jnp.where(lo, va_l[:, cw2:2 * cw2], vb[:, cw2:2 * cw2]))

    def from_planes(f_re, f_im, b_re, b_im):
        lo = low_half(f_re.shape)
        va = jnp.concatenate([jnp.where(lo, f_re, swap(f_im)), jnp.where(lo, b_re, swap(b_im))], axis=1)
        vb = jnp.concatenate([jnp.where(lo, swap(f_re), f_im), jnp.where(lo, swap(b_re), b_im)], axis=1)
        return va, vb

    for ga in range(0, GROUP_BLOCK, 2):
        gb = ga + 1
        xts = [xt_ref[ga], xt_ref[gb]]
        s_pair = [lax.dot_general(xts[i], gt_ref[g], TN_DIMS, preferred_element_type=F32)
                  for i, g in enumerate((ga, gb))]
        planes = to_planes(*s_pair)
        lo_t = low_half((AT_ROWS, cw2))
        if has_h0:
            h0_planes = to_planes(h0_ref[ga], h0_ref[gb])
        ent_planes = []
        for dr in range(2):
            down = dr == 0
            pa, pb = (at_ref[g][:, (2 * dr) * cw2:(2 * dr + 1) * cw2] for g in (ga, gb))
            qa, qb = (at_ref[g][:, (2 * dr + 1) * cw2:(2 * dr + 2) * cw2] for g in (ga, gb))
            ar_tab = jnp.where(lo_t, pa, pb)
            ai_tab = jnp.where(lo_t, -qa, qb)
            re3 = planes[2 * dr].reshape(n_blk, SCAN_BLOCK, cw2)
            im3 = planes[2 * dr + 1].reshape(n_blk, SCAN_BLOCK, cw2)
            for lvl in range(SCAN_LEVELS):
                m = 2 ** lvl
                valid = (row_blk >= m) if down else (row_blk < SCAN_BLOCK - m)
                row = SCAN_BLOCK + lvl
                ar = jnp.where(valid, ar_tab[row:row + 1], 0.0)
                ai = jnp.where(valid, ai_tab[row:row + 1], 0.0)
                shift = m if down else SCAN_BLOCK - m
                sh_re = pltpu.roll(re3, shift, axis=1)
                sh_im = pltpu.roll(im3, shift, axis=1)
                re3, im3 = re3 + ar * sh_re - ai * sh_im, im3 + ar * sh_im + ai * sh_re
            ar_blk, ai_blk = ar_tab[0:SCAN_BLOCK], ai_tab[0:SCAN_BLOCK]
            blk_re = [re3[i] for i in range(n_blk)]
            blk_im = [im3[i] for i in range(n_blk)]
            edge = slice(SCAN_BLOCK - 1, SCAN_BLOCK) if down else slice(0, 1)
            for q_i in range(n_seq):
                idxs = list(range(q_i * blk_per_seq, (q_i + 1) * blk_per_seq))
                idxs = idxs if down else idxs[::-1]
                for prev, cur in zip([None] + idxs[:-1], idxs):
                    if prev is not None:
                        c_re, c_im = blk_re[prev][edge], blk_im[prev][edge]
                    elif has_h0:
                        c_re = h0_planes[2 * dr][q_i:q_i + 1]
                        c_im = h0_planes[2 * dr + 1][q_i:q_i + 1]
                    else:
                        continue
                    c_re = jnp.broadcast_to(c_re, (SCAN_BLOCK, cw2))
                    c_im = jnp.broadcast_to(c_im, (SCAN_BLOCK, cw2))
                    blk_re[cur] = blk_re[cur] + ar_blk * c_re - ai_blk * c_im
                    blk_im[cur] = blk_im[cur] + ar_blk * c_im + ai_blk * c_re
            for part, blks in ((0, blk_re), (1, blk_im)):
                s = jnp.concatenate(blks, axis=0)
                if not has_h0:
                    fin_ref[part] = s
                ent = _shift_rows(s, 1, down)
                ent = jnp.where((pos >= 1) if down else (pos < seg - 1), ent, 0.0)
                if has_h0:
                    ent_blk = [ent[i * SCAN_BLOCK:(i + 1) * SCAN_BLOCK] for i in range(n_blk)]
                    at_edge = row_blk == (0 if down else SCAN_BLOCK - 1)
                    for q_i in range(n_seq):
                        bi = q_i * blk_per_seq if down else (q_i + 1) * blk_per_seq - 1
                        ent_blk[bi] = jnp.where(
                            at_edge, h0_planes[2 * dr + part][q_i:q_i + 1], ent_blk[bi])
                    ent = jnp.concatenate(ent_blk, axis=0)
                ent_planes.append(ent)
            if not has_h0:
                rows = pl.ds(seg - 1 if down else 0, n_seq, stride=seg)
                f_re, f_im = fin_ref[0, rows, :], fin_ref[1, rows, :]
                lo_s = low_half((n_seq, cw2))
                st_ref[ga, :, dr * cw2:(dr + 1) * cw2] = jnp.where(lo_s, f_re, swap(f_im))
                st_ref[gb, :, dr * cw2:(dr + 1) * cw2] = jnp.where(lo_s, swap(f_re), f_im)
        h_pair = from_planes(*ent_planes)
        for i, g in enumerate((ga, gb)):
            yt = (jnp.dot(mt_ref[g], xts[i], preferred_element_type=F32)
                  + lax.dot_general(wo_ref[g], h_pair[i].astype(BF16), NT_DIMS,
                                    preferred_element_type=F32))
            z = jax.nn.gelu(yt, approximate=True)
            grp = step * GROUP_BLOCK + g
            for t in range(CHUNK):
                zs_ref[t, pl.ds(pl.multiple_of(grp * SSM_GROUP, SSM_GROUP), SSM_GROUP), :] = (
                    z[t * SSM_GROUP:(t + 1) * SSM_GROUP, :])

    @pl.when(step == pl.num_programs(0) - 1)
    def _():
        def put(t):
            return pltpu.make_async_copy(zb_ref.at[t], z_hbm.at[:, t, :], zb_sem.at[t])

        for t0 in range(0, CHUNK, Z_BATCH):
            for t in range(t0, t0 + Z_BATCH):
                zt = zs_ref[t]
                gate = jnp.dot(wg_ref[...], zt.astype(BF16),
                               preferred_element_type=F32) + bg_ref[...]
                zb_ref[t] = (zt * jax.nn.sigmoid(gate)).T
            for t in range(t0, t0 + Z_BATCH):
                put(t).start(priority=t % N_DMA_THREADS)
        for t in range(CHUNK):
            put(t).wait()


def _ssm(xt, mt, gt, wo, at, w_glu_t, b_glu_col, h0, n_seq):
    n_rows = xt.shape[-1]
    has_h0 = h0 is not None
    gb = GROUP_BLOCK
    w_spec = pl.BlockSpec((gb, CW, CW), lambda i: (i, 0, 0))
    in_specs = [pl.BlockSpec((gb, CW, n_rows), lambda i: (i, 0, 0)),
                w_spec, w_spec, w_spec,
                pl.BlockSpec((gb, AT_ROWS, 4 * 2 * SSM_STATE), lambda i: (i, 0, 0)),
                pl.BlockSpec((SSM_WIDTH, SSM_WIDTH), lambda i: (0, 0)),
                pl.BlockSpec((SSM_WIDTH, 1), lambda i: (0, 0))]
    args = [xt, mt, gt, wo, at, w_glu_t, b_glu_col]
    out_specs = [pl.BlockSpec(memory_space=pl.ANY)]
    out_shape = [jax.ShapeDtypeStruct((n_rows, CHUNK, SSM_WIDTH), F32)]
    scratch = [pltpu.VMEM((CHUNK, SSM_WIDTH, n_rows), F32),
               pltpu.VMEM((CHUNK, n_rows, SSM_WIDTH), F32),
               pltpu.SemaphoreType.DMA((CHUNK,))]
    if has_h0:
        in_specs.append(pl.BlockSpec((gb, n_seq, CW), lambda i: (i, 0, 0)))
        args.append(h0)
    else:
        out_specs.append(pl.BlockSpec((gb, n_seq, CW), lambda i: (i, 0, 0)))
        out_shape.append(jax.ShapeDtypeStruct((N_GROUPS, n_seq, CW), F32))
        scratch.append(pltpu.VMEM((2, n_rows, 2 * SSM_STATE), F32))
    return pl.pallas_call(
        functools.partial(_ssm_kernel, n_seq=n_seq, has_h0=has_h0),
        grid=(N_GROUPS // gb,),
        in_specs=in_specs,
        out_specs=out_specs,
        out_shape=out_shape,
        scratch_shapes=scratch,
        compiler_params=_cparams(1, fuse_inputs=[False] * 5 + [True] * (len(args) - 5)),
        name="ssm_scan_glu",
    )(*args)


FF_CHUNK = 256
FFN_TOK = 1024


def _out_ffn_kernel(x_ref, attn_ref, z_ref, mod_ref, g_ref, wo_hbm, wfi_hbm, wfo_hbm,
                    o_ref, act_ref, wo_ref, wfi_ref, wfo_ref, w_sem):
    first = pl.program_id(0) == 0
    fetch = [pltpu.make_async_copy(src, dst, w_sem.at[i]) for i, (src, dst) in
             enumerate(((wo_hbm, wo_ref), (wfi_hbm, wfi_ref), (wfo_hbm, wfo_ref)))]

    @pl.when(first)
    def _():
        for i, cp in enumerate(fetch):
            cp.start(priority=min(i, N_DMA_THREADS - 1))
        fetch[0].wait()

    gate1 = mod_ref[:, 2 * D_MODEL:3 * D_MODEL]
    shift2 = mod_ref[:, 3 * D_MODEL:4 * D_MODEL]
    scale2 = mod_ref[:, 4 * D_MODEL:5 * D_MODEL]
    gate2 = mod_ref[:, 5 * D_MODEL:6 * D_MODEL]
    half = FFN_TOK // 2
    gain2 = g_ref[2:3, :] * (1.0 + scale2)

    def pre_ffn(hf):
        r = slice(hf * half, (hf + 1) * half)
        mixer = jnp.concatenate([attn_ref[r, :], z_ref[r, :].astype(BF16)], axis=1)
        mix = jnp.dot(mixer, wo_ref[...], preferred_element_type=F32)
        x1 = x_ref[r, :] + gate1 * _rms(mix, g_ref[1:2, :])
        ms = jnp.mean(x1 * x1, axis=-1, keepdims=True)
        return x1, (x1 * lax.rsqrt(ms + NORM_EPS) * gain2 + shift2).astype(BF16)

    def ffn_in(hf, h, chunks):
        r = slice(hf * half, (hf + 1) * half)
        for c in chunks:
            lo = c * FF_CHUNK
            gt = jnp.dot(h, wfi_ref[:, lo:lo + FF_CHUNK], preferred_element_type=F32)
            up = jnp.dot(h, wfi_ref[:, D_FF + lo:D_FF + lo + FF_CHUNK],
                         preferred_element_type=F32)
            act_ref[r, lo:lo + FF_CHUNK] = (_silu(gt) * up).astype(BF16)

    def ffn_out(hf, x1):
        r = slice(hf * half, (hf + 1) * half)
        f = jnp.dot(act_ref[r, :], wfo_ref[...], preferred_element_type=F32)
        o_ref[r, :] = x1 + gate2 * _rms(f, g_ref[3:4, :])

    n_chunks = D_FF // FF_CHUNK
    x1_a, h_a = pre_ffn(0)
    x1_b, h_b = pre_ffn(1)
    @pl.when(first)
    def _():
        fetch[1].wait()
        fetch[2].wait()

    ffn_in(0, h_a, range(n_chunks))
    ffn_out(0, x1_a)
    ffn_in(1, h_b, range(n_chunks))
    ffn_out(1, x1_b)


def _out_ffn(x2d, attn, z, mods, mod_rows, norm_g, w_o, w_ffn_in, w_ffn_out):
    n_tok = x2d.shape[0]
    z2d = z.reshape(n_tok, SSM_WIDTH)
    n_steps = n_tok // FFN_TOK
    mod_row0, n_mod = mod_rows
    steps_per_mod = n_steps // n_mod
    const = lambda i: (0, 0)
    row_spec = lambda w: pl.BlockSpec((FFN_TOK, w), lambda i: (i, 0))
    return pl.pallas_call(
        _out_ffn_kernel,
        grid=(n_steps,),
        in_specs=[row_spec(D_MODEL), row_spec(ATT_WIDTH), row_spec(SSM_WIDTH),
                  pl.BlockSpec((None, 1, N_MOD * D_MODEL),
                               lambda i: (mod_row0 + i // steps_per_mod, 0, 0)),
                  pl.BlockSpec((4, D_MODEL), const),
                  pl.BlockSpec(memory_space=pl.ANY),
                  pl.BlockSpec(memory_space=pl.ANY),
                  pl.BlockSpec(memory_space=pl.ANY)],
        out_specs=row_spec(D_MODEL),
        out_shape=jax.ShapeDtypeStruct((n_tok, D_MODEL), F32),
        scratch_shapes=[pltpu.VMEM((FFN_TOK, D_FF), BF16),
                        pltpu.VMEM((2 * ATT_WIDTH, D_MODEL), BF16),
                        pltpu.VMEM((D_MODEL, 2 * D_FF), BF16),
                        pltpu.VMEM((D_FF, D_MODEL), BF16),
                        pltpu.SemaphoreType.DMA((3,))],
        compiler_params=_cparams(1),
        name="out_proj_ffn",
    )(x2d, attn, z2d, mods, norm_g, w_o, w_ffn_in, w_ffn_out)


def _rope_tables(seq_len):
    t = np.arange(seq_len)
    row = (t // GRID_W).astype(np.float32)
    col = (t % GRID_W).astype(np.float32)
    half = HEAD_DIM // 2
    inv_freq = (np.float32(ROPE_BASE)
                ** (-np.arange(0, half, 2, dtype=np.float32) / np.float32(half))).astype(np.float32)
    ang_r = row[:, None] * inv_freq
    ang_c = col[:, None] * inv_freq
    ang = np.concatenate([ang_r, ang_r, ang_c, ang_c], axis=-1)
    cos, sin = np.cos(ang), np.sin(ang)
    upper = (np.arange(HEAD_DIM) % 32) < 16
    sa = np.where(upper, -sin, 0.0)
    sb = np.where(upper, 0.0, sin)
    two = lambda a: jnp.asarray(np.concatenate([a, a], axis=-1), dtype=F32)
    return two(cos), two(sa), two(sb)


def _layer(x, mods, mod_rows, lam_init, rope_tabs, ctx_k, ctx_v, h0, weights, prep):
    n_batch, seq_len = x.shape[:2]
    g = weights['norm_g']
    outs = _in_proj(x, mods, mod_rows, g[0:1], weights['w_in'], rope_tabs)
    q, k, v = outs[:3]
    pending = () if 'late_bf16' in weights else weights['late_f32']
    attn, cast = _attention(q, k, v, ctx_k, ctx_v, weights['lam'], weights['subln_g'],
                            n_batch, seq_len, lam_init, cast_weights=pending)
    if pending:
        weights['late_bf16'] = cast
    ssm_out = _ssm(outs[-1], *prep, weights['w_glu_t'], weights['b_glu_col'], h0, n_batch)
    y = _out_ffn(x.reshape(n_batch * seq_len, D_MODEL), attn, ssm_out[0], mods, mod_rows, g,
                 *weights['late_bf16'])
    return y.reshape(x.shape), outs[3:-1], ssm_out[1:]


def kernel(x_prompt, x_sample, cache_k, cache_v, state_ssm_re, state_ssm_im, c, c_ctx, w_mod, b_mod, norm_g, w_in, lam_params, subln_g, ssm_lambda_re, ssm_lambda_im, ssm_log_step, ssm_b_re, ssm_b_im, ssm_c_re, ssm_c_im, ssm_d, w_glu, b_glu, w_o, w_ffn_in, w_ffn_out):
    depth = w_mod.shape[0]
    assert depth == 1
    bp = x_prompt.shape[0]
    bd, ld_len = x_sample.shape[:2]
    past = cache_k.shape[2]
    xp, xs = x_prompt, x_sample
    rope_tabs = _rope_tables(ld_len)
    ks_out, vs_out, hr_out, hi_out = [], [], [], []
    for l in range(depth):
        lam_init = 0.8 - 0.6 * math.exp(-0.3 * l)
        prep, mods, w_in_bf16 = _ssm_prep_and_modulation(
            ssm_lambda_re[l], ssm_lambda_im[l], ssm_log_step[l], ssm_b_re[l], ssm_b_im[l],
            ssm_c_re[l], ssm_c_im[l], ssm_d[l], c_ctx, c, w_mod[l], b_mod[l], w_in[l])
        weights = {
            'norm_g': norm_g[l],
            'w_in': w_in_bf16,
            'lam': lam_params[l], 'subln_g': subln_g[l],
            'w_glu_t': w_glu[l].T.astype(BF16), 'b_glu_col': b_glu[l].reshape(SSM_WIDTH, 1),
            'late_f32': (w_o[l], w_ffn_in[l], w_ffn_out[l]),
        }
        ck = jnp.transpose(cache_k[:, l], (0, 2, 3, 1)).reshape(bd, ATT_WIDTH, past)
        cv = cache_v[:, l]
        h0 = jnp.stack([state_ssm_re[:, l], state_ssm_im[:, l]], axis=2)
        h0 = h0.transpose(3, 0, 1, 2, 4).reshape(N_GROUPS, bd, CW)
        xs, _, _ = _layer(xs, mods, (1, bd), lam_init, rope_tabs, ck, cv, h0, weights, prep)
        xp, (k_ctx, v_ctx), (st,) = _layer(xp, mods, (0, 1), lam_init, None, None, None, None,
                                           weights, prep)
        ks_out.append(jnp.swapaxes(k_ctx, 1, 2).reshape(bp, -1, 2 * N_HEADS, HEAD_DIM))
        vs_out.append(v_ctx)
        fin = st.reshape(N_GROUPS, bp, 2, 2, SSM_STATE).transpose(1, 2, 3, 0, 4)
        hr_out.append(fin[:, :, 0])
        hi_out.append(fin[:, :, 1])
    return (xp, xs, jnp.stack(ks_out, axis=1), jnp.stack(vs_out, axis=1),
            jnp.stack(hr_out, axis=1), jnp.stack(hi_out, axis=1))
```
